```python
import jax, jax.numpy as jnp
from jax import lax
import numpy as np

D_MODEL = 1024
BATCH = 16
SEQ = 256
DEPTH = 2
DEC_BATCH = 2
DEC_SEQ = 2048
PAST_LEN = 256

GRID_W = 64
HEAD_DIM = 64
D_MIX = D_MODEL
F_GROUPS = 4
F_WIDTH = F_GROUPS * HEAD_DIM
WIN_HEADS = 6
WIN_KV_HEADS = 2
WINDOW = 128
WIN_BLOCK = 128
NA_HEADS = 6
NA_ROWS = 8
NA_COLS = 16
N_EXPERTS = 16
EC_CAPACITY = 2
D_FF_EXPERT = 2816
ROPE_BASE = 10000.0
RMS_EPS = 1e-6
Q_BLOCK = 128
NEG_INF = -1e30
ATTN_SCALE = HEAD_DIM ** -0.5
WIN_Q = WIN_HEADS * HEAD_DIM
WIN_KV = WIN_KV_HEADS * HEAD_DIM
NA_W = NA_HEADS * HEAD_DIM
N_IN = F_WIDTH + WIN_Q + 2 * WIN_KV + 3 * NA_W
SPLIT_POINTS = (F_WIDTH, F_WIDTH + WIN_Q, F_WIDTH + WIN_Q + WIN_KV, F_WIDTH + WIN_Q + 2 * WIN_KV, F_WIDTH + WIN_Q + 2 * WIN_KV + NA_W, F_WIDTH + WIN_Q + 2 * WIN_KV + 2 * NA_W)

kernel_name = 'hybrid_flow_prefix_step'


def rmsnorm(x, g):
    x32 = x.astype(jnp.float32)
    y = x32 * lax.rsqrt(jnp.mean(x32 * x32, axis=-1, keepdims=True) + RMS_EPS)
    return (y * g.astype(jnp.float32)).astype(x.dtype)


def adaln(cond, w_mod, b_mod):
    m = jax.nn.silu(cond) @ w_mod + b_mod
    return jnp.split(m, 6, axis=-1)


def modulate(h, shift, scale):
    return h * (1 + scale) + shift


def project(x, g, shift, scale, w_in):
    b, s, _ = x.shape
    h = modulate(rmsnorm(x, g), shift, scale)
    f, qw, kw, vw, qn, kn, vn = jnp.split(h @ w_in, SPLIT_POINTS, axis=-1)
    hd = lambda t: t.reshape(b, s, -1, HEAD_DIM)
    return f, hd(qw), hd(kw), hd(vw), hd(qn), hd(kn), hd(vn)


def fourier_mix(f):
    b, s, _ = f.shape
    z = jnp.fft.fft2(f.astype(jnp.float32).reshape(b, s, F_GROUPS, HEAD_DIM), axes=(1, 3), norm='ortho')
    return jnp.real(z).reshape(b, s, F_WIDTH).astype(f.dtype)


def axial_rope(x):
    b, s, h, d = x.shape
    half = d // 2
    nf = half // 2
    pos = jnp.arange(s)
    inv = 1.0 / (ROPE_BASE ** (jnp.arange(nf, dtype=jnp.float32) / nf))
    ang_r = (pos // GRID_W).astype(jnp.float32)[:, None] * inv
    ang_c = (pos % GRID_W).astype(jnp.float32)[:, None] * inv

    def rot(t, ang):
        cos = jnp.cos(ang)[None, :, None, :]
        sin = jnp.sin(ang)[None, :, None, :]
        t1, t2 = t[..., :nf], t[..., nf:]
        return jnp.concatenate([t1 * cos - t2 * sin, t2 * cos + t1 * sin], axis=-1)

    x32 = x.astype(jnp.float32)
    return jnp.concatenate([rot(x32[..., :half], ang_r), rot(x32[..., half:], ang_c)], axis=-1).astype(x.dtype)


def attn_softmax(s, sink):
    if sink is None:
        return jax.nn.softmax(s, axis=-1)
    col = jnp.broadcast_to(sink.astype(jnp.float32), s.shape[:-1] + (1,))
    return jax.nn.softmax(jnp.concatenate([s, col], axis=-1), axis=-1)[..., :-1]


def context_attention(q, k, v, sink):
    b, s, h, d = q.shape
    hkv = k.shape[2]
    g = h // hkv
    nq = s // Q_BLOCK
    qb = q.reshape(b, nq, Q_BLOCK, hkv, g, d).transpose(1, 0, 2, 3, 4, 5)
    sink_b = None if sink is None else sink.reshape(hkv, g, 1, 1)

    def block(qi):
        sc = jnp.einsum('bqkgd,bpkd->bkgqp', qi, k).astype(jnp.float32) * ATTN_SCALE
        p = attn_softmax(sc, sink_b).astype(v.dtype)
        return jnp.einsum('bkgqp,bpkd->bqkgd', p, v)

    o = lax.map(block, qb)
    return o.transpose(1, 0, 2, 3, 4, 5).reshape(b, s, h * d)


def window_attention(q, k, v, ck, cv, sink):
    b, s, h, d = q.shape
    hkv = k.shape[2]
    g = h // hkv
    nb = s // WIN_BLOCK
    qb = q.reshape(b, nb, WIN_BLOCK, hkv, g, d)

    def band(t):
        tp = jnp.pad(t, ((0, 0), (WIN_BLOCK, WIN_BLOCK), (0, 0), (0, 0))).reshape(b, nb + 2, WIN_BLOCK, hkv, d)
        return jnp.concatenate([tp[:, :-2], tp[:, 1:-1], tp[:, 2:]], axis=2)

    kb, vb = band(k), band(v)
    blk = jnp.arange(nb)[:, None]
    qpos = blk * WIN_BLOCK + jnp.arange(WIN_BLOCK)[None, :]
    kpos = (blk - 1) * WIN_BLOCK + jnp.arange(3 * WIN_BLOCK)[None, :]
    kp = kpos[:, None, :]
    mask = (jnp.abs(qpos[:, :, None] - kp) <= WINDOW) & (kp >= 0) & (kp < s)
    s_loc = jnp.einsum('bnqkgd,bnjkd->bnkgqj', qb, kb).astype(jnp.float32) * ATTN_SCALE
    s_loc = jnp.where(mask[None, :, None, None], s_loc, NEG_INF)
    s_ctx = jnp.einsum('bnqkgd,bpkd->bnkgqp', qb, ck).astype(jnp.float32) * ATTN_SCALE
    p = attn_softmax(jnp.concatenate([s_loc, s_ctx], axis=-1), sink.reshape(hkv, g, 1, 1)).astype(v.dtype)
    n_loc = 3 * WIN_BLOCK
    o = jnp.einsum('bnkgqj,bnjkd->bnqkgd', p[..., :n_loc], vb) + jnp.einsum('bnkgqp,bpkd->bnqkgd', p[..., n_loc:], cv)
    return o.reshape(b, s, h * d)


def neighbourhood_attention(q, k, v, ck, cv, rpb):
    b, s, h, d = q.shape
    rows = s // GRID_W
    kr = min(NA_ROWS, rows)
    r = jnp.arange(rows)
    cq = jnp.arange(GRID_W)
    rs = jnp.clip(r - kr // 2, 0, rows - kr)
    row_idx = rs[:, None] + jnp.arange(kr)[None, :]
    cs = jnp.clip(cq - NA_COLS // 2, 0, GRID_W - NA_COLS)
    col_mask = (cq[None, :] >= cs[:, None]) & (cq[None, :] < cs[:, None] + NA_COLS)
    qg = q.reshape(b, rows, GRID_W, h, d)
    k_rows = k.reshape(b, rows, GRID_W, h, d)[:, row_idx]
    v_rows = v.reshape(b, rows, GRID_W, h, d)[:, row_idx]
    s_nb = jnp.einsum('brchd,bramhd->brhcam', qg, k_rows).astype(jnp.float32) * ATTN_SCALE
    rel_r = row_idx - r[:, None] + NA_ROWS - 1
    rel_c = jnp.clip(cq[None, :] - cq[:, None] + NA_COLS - 1, 0, 2 * NA_COLS - 2)
    bias = rpb[:, rel_r[:, None, :, None], rel_c[None, :, None, :]]
    s_nb = s_nb + bias.transpose(1, 0, 2, 3, 4).astype(jnp.float32)[None]
    s_nb = jnp.where(col_mask[None, None, None, :, None, :], s_nb, NEG_INF).reshape(b, rows, h, GRID_W, kr * GRID_W)
    s_ctx = jnp.einsum('brchd,bphd->brhcp', qg, ck).astype(jnp.float32) * ATTN_SCALE
    p = attn_softmax(jnp.concatenate([s_nb, s_ctx], axis=-1), None).astype(v.dtype)
    n_loc = kr * GRID_W
    p_nb = p[..., :n_loc].reshape(b, rows, h, GRID_W, kr, GRID_W)
    o = jnp.einsum('brhcam,bramhd->brchd', p_nb, v_rows) + jnp.einsum('brhcp,bphd->brchd', p[..., n_loc:], cv)
    return o.reshape(b, s, h * d)


def expert_choice_ffn(h, w_router, w_gate, w_up, w_down):
    b, n, d = h.shape
    cap = max(1, EC_CAPACITY * n // N_EXPERTS)
    aff = jax.nn.softmax(jnp.einsum('bnd,de->bne', h, w_router).astype(jnp.float32), axis=-1)
    gate, idx = lax.top_k(jnp.swapaxes(aff, 1, 2), cap)
    xs = jax.vmap(lambda hb, ib: hb[ib])(h, idx)
    a = jnp.einsum('becd,edf->becf', xs, w_gate)
    u = jnp.einsum('becd,edf->becf', xs, w_up)
    y = jnp.einsum('becf,efd->becd', jax.nn.silu(a) * u, w_down) * gate[..., None].astype(h.dtype)
    scatter = lambda ib, yb: jnp.zeros((n, d), h.dtype).at[ib.reshape(-1)].add(yb.reshape(-1, d))
    return jax.vmap(scatter)(idx, y)


def context_layer(x, c_ctx, w_mod, b_mod, g_mix, g_ffn, w_in, w_out, sink, w_router, w_gate, w_up, w_down):
    sh1, sc1, gt1, sh2, sc2, gt2 = adaln(c_ctx, w_mod, b_mod)
    f, qw, kw, vw, qn, kn, vn = project(x, g_mix, sh1, sc1, w_in)
    mixed = jnp.concatenate([fourier_mix(f), context_attention(qw, kw, vw, sink), context_attention(qn, kn, vn, None)], axis=-1)
    x = x + gt1 * (mixed @ w_out)
    x = x + gt2 * expert_choice_ffn(modulate(rmsnorm(x, g_ffn), sh2, sc2), w_router, w_gate, w_up, w_down)
    return x, kw, vw, kn, vn


def latent_layer(x, cond, ck_w, cv_w, ck_n, cv_n, w_mod, b_mod, g_mix, g_ffn, w_in, w_out, sink, rpb, w_router, w_gate, w_up, w_down):
    sh1, sc1, gt1, sh2, sc2, gt2 = adaln(cond, w_mod, b_mod)
    f, qw, kw, vw, qn, kn, vn = project(x, g_mix, sh1, sc1, w_in)
    mixed = jnp.concatenate([
        fourier_mix(f),
        window_attention(axial_rope(qw), axial_rope(kw), vw, ck_w, cv_w, sink),
        neighbourhood_attention(qn, kn, vn, ck_n, cv_n, rpb),
    ], axis=-1)
    x = x + gt1 * (mixed @ w_out)
    x = x + gt2 * expert_choice_ffn(modulate(rmsnorm(x, g_ffn), sh2, sc2), w_router, w_gate, w_up, w_down)
    return x


def setup_inputs(seed: int = 0) -> dict:
    key = jax.random.key(seed)
    ks = jax.random.split(key, 22)
    nrm = lambda k, shape, s: s * jax.random.normal(k, shape, jnp.float32)
    D = D_MODEL
    return {
        'x_prompt': nrm(ks[0], (BATCH, SEQ, D), 1.0),
        'x_sample': nrm(ks[1], (DEC_BATCH, DEC_SEQ, D), 1.0),
        'cache_win_k': nrm(ks[2], (DEC_BATCH, DEPTH, PAST_LEN, WIN_KV_HEADS, HEAD_DIM), 1.0),
        'cache_win_v': nrm(ks[3], (DEC_BATCH, DEPTH, PAST_LEN, WIN_KV_HEADS, HEAD_DIM), 1.0),
        'cache_nat_k': nrm(ks[4], (DEC_BATCH, DEPTH, PAST_LEN, NA_HEADS, HEAD_DIM), 1.0),
        'cache_nat_v': nrm(ks[5], (DEC_BATCH, DEPTH, PAST_LEN, NA_HEADS, HEAD_DIM), 1.0),
        'c': nrm(ks[6], (DEC_BATCH, D), 1.0),
        'c_ctx': nrm(ks[7], (D,), 1.0),
        'w_mod': nrm(ks[8], (DEPTH, D, 6 * D), 0.5 * D ** -0.5),
        'b_mod': nrm(ks[9], (DEPTH, 6 * D), 0.01),
        'g_mix': 1.0 + nrm(ks[10], (DEPTH, D), 0.01),
        'g_ffn': 1.0 + nrm(ks[11], (DEPTH, D), 0.01),
        'w_in': nrm(ks[12], (DEPTH, D, N_IN), D ** -0.5),
        'w_out': nrm(ks[13], (DEPTH, D_MIX, D), D_MIX ** -0.5),
        'win_sink': nrm(ks[14], (DEPTH, WIN_HEADS), 0.5),
        'nat_rpb': nrm(ks[15], (DEPTH, NA_HEADS, 2 * NA_ROWS - 1, 2 * NA_COLS - 1), 0.02),
        'w_router': nrm(ks[16], (DEPTH, D, N_EXPERTS), D ** -0.5),
        'w_gate': nrm(ks[17], (DEPTH, N_EXPERTS, D, D_FF_EXPERT), D ** -0.5),
        'w_up': nrm(ks[18], (DEPTH, N_EXPERTS, D, D_FF_EXPERT), D ** -0.5),
        'w_down': nrm(ks[19], (DEPTH, N_EXPERTS, D_FF_EXPERT, D), D_FF_EXPERT ** -0.5),
        'g_final': 1.0 + nrm(ks[20], (D,), 0.01),
    }


def reference(x_prompt, x_sample, cache_win_k, cache_win_v, cache_nat_k, cache_nat_v, c, c_ctx, w_mod, b_mod, g_mix, g_ffn, w_in, w_out, win_sink, nat_rpb, w_router, w_gate, w_up, w_down, g_final):
    y_p = x_prompt
    y_s = x_sample
    c_lat = c[:, None, :]
    win_k, win_v, nat_k, nat_v = [], [], [], []
    for l in range(DEPTH):
        y_p, kw, vw, kn, vn = context_layer(y_p, c_ctx, w_mod[l], b_mod[l], g_mix[l], g_ffn[l], w_in[l], w_out[l], win_sink[l], w_router[l], w_gate[l], w_up[l], w_down[l])
        win_k.append(kw)
        win_v.append(vw)
        nat_k.append(kn)
        nat_v.append(vn)
        y_s = latent_layer(y_s, c_lat, cache_win_k[:, l], cache_win_v[:, l], cache_nat_k[:, l], cache_nat_v[:, l], w_mod[l], b_mod[l], g_mix[l], g_ffn[l], w_in[l], w_out[l], win_sink[l], nat_rpb[l], w_router[l], w_gate[l], w_up[l], w_down[l])
    y_prompt = rmsnorm(y_p, g_final)
    y_sample = rmsnorm(y_s, g_final)
    new_win_k = jnp.stack(win_k, axis=1)
    new_win_v = jnp.stack(win_v, axis=1)
    new_nat_k = jnp.stack(nat_k, axis=1)
    new_nat_v = jnp.stack(nat_v, axis=1)
    return (y_prompt, y_sample, new_win_k, new_win_v, new_nat_k, new_nat_v)
```

```python
import functools

import numpy as np
import jax
import jax.numpy as jnp
from jax import lax
from jax.experimental import pallas as pl
from jax.experimental.pallas import tpu as pltpu

D_MODEL = 1024
BATCH = 16
SEQ = 256
DEPTH = 2
DEC_BATCH = 2
DEC_SEQ = 2048
PAST_LEN = 256
GRID_W = 64
HEAD_DIM = 64
F_WIDTH = 256
WIN_HEADS = 6
WIN_KV_HEADS = 2
WINDOW = 128
WIN_BLOCK = 128
NA_HEADS = 6
NA_ROWS = 8
NA_COLS = 16
N_EXPERTS = 16
EC_CAPACITY = 2
D_FF = 2816
ROPE_BASE = 10000.0
RMS_EPS = 1e-6
NEG_INF = -1e30
ATTN_SCALE = HEAD_DIM ** -0.5
WIN_Q = WIN_HEADS * HEAD_DIM
WIN_KV = WIN_KV_HEADS * HEAD_DIM
NA_W = NA_HEADS * HEAD_DIM
N_IN = F_WIDTH + WIN_Q + 2 * WIN_KV + 3 * NA_W
SPLITS = (0, F_WIDTH, F_WIDTH + WIN_Q, F_WIDTH + WIN_Q + WIN_KV, F_WIDTH + WIN_Q + 2 * WIN_KV,
          F_WIDTH + WIN_Q + 2 * WIN_KV + NA_W, F_WIDTH + WIN_Q + 2 * WIN_KV + 2 * NA_W, N_IN)

N_CTX = BATCH * SEQ
N_LAT = DEC_BATCH * DEC_SEQ
N_TOK = N_CTX + N_LAT
GRID_ROWS = DEC_SEQ // GRID_W
CAP_CTX = EC_CAPACITY * SEQ // N_EXPERTS
CAP_LAT = EC_CAPACITY * DEC_SEQ // N_EXPERTS
ROWS_CTX = BATCH * CAP_CTX
ROWS_LAT = DEC_BATCH * CAP_LAT
N_COND = 8

LANES = 128
TM = 512
TN_MOD = 1536
TF = 256
VMEM_LIMIT = 56 * 1024 * 1024

F32 = jnp.float32
BF16 = jnp.bfloat16


def _params(*sem):
    return pltpu.CompilerParams(dimension_semantics=sem, vmem_limit_bytes=VMEM_LIMIT)


def _dot(a, b):
    return jnp.dot(a, b, preferred_element_type=F32)


def _dot_nt(a, b):
    return lax.dot_general(a, b, (((1,), (1,)), ((), ())), preferred_element_type=F32)


def _split(x):
    hi = x.astype(BF16)
    lo = (x - hi.astype(F32)).astype(BF16)
    return hi, lo


def _dot3(a_hi, a_lo, b_hi, b_lo):
    return _dot(a_hi, b_hi) + (_dot(a_lo, b_hi) + _dot(a_hi, b_lo))


def _silu(x):
    return x / (1.0 + jnp.exp(-x))


def _rms_mod(x, g, shift, scale):
    y = x * lax.rsqrt(jnp.mean(x * x, axis=-1, keepdims=True) + RMS_EPS)
    return (y * g) * (1.0 + scale) + shift


def _softmax_parts(parts, sink=None):
    m = parts[0].max(axis=-1, keepdims=True)
    for s in parts[1:]:
        m = jnp.maximum(m, s.max(axis=-1, keepdims=True))
    if sink is not None:
        m = jnp.maximum(m, sink)
    es = [jnp.exp(s - m) for s in parts]
    den = es[0].sum(axis=-1, keepdims=True)
    for e in es[1:]:
        den = den + e.sum(axis=-1, keepdims=True)
    if sink is not None:
        den = den + jnp.exp(sink - m)
    inv = 1.0 / den
    return [e * inv for e in es]


def _cond_of_block(i):
    n_ctx_blocks = N_CTX // TM
    return jnp.where(i < n_ctx_blocks, 0, 1 + (i - n_ctx_blocks) // (DEC_SEQ // TM))


def _two_stream_specs(width):
    n_ctx_blocks = N_CTX // TM
    ctx = pl.BlockSpec((TM, width), lambda i: (jnp.minimum(i, n_ctx_blocks - 1), 0))
    lat = pl.BlockSpec((TM, width), lambda i: (jnp.maximum(i - n_ctx_blocks, 0), 0))
    return ctx, lat


def _pick_stream(ctx_ref, lat_ref):
    return jnp.where(pl.program_id(0) < N_CTX // TM, ctx_ref[...], lat_ref[...])


def _adaln_kernel(c_ref, w_ref, b_ref, o_ref):
    s_hi, s_lo = _split(_silu(c_ref[...]))
    w_hi, w_lo = _split(w_ref[...])
    o_ref[...] = _dot3(s_hi, s_lo, w_hi, w_lo) + b_ref[...]


def _adaln(cond, w_mod, b_mod):
    return pl.pallas_call(
        _adaln_kernel,
        grid=(DEPTH, 6 * D_MODEL // TN_MOD),
        in_specs=[
            pl.BlockSpec((N_COND, D_MODEL), lambda l, j: (0, 0)),
            pl.BlockSpec((None, D_MODEL, TN_MOD), lambda l, j: (l, 0, j)),
            pl.BlockSpec((None, 1, TN_MOD), lambda l, j: (l, 0, j)),
        ],
        out_specs=pl.BlockSpec((None, N_COND, TN_MOD), lambda l, j: (l, 0, j)),
        out_shape=jax.ShapeDtypeStruct((DEPTH, N_COND, 6 * D_MODEL), F32),
        compiler_params=_params("arbitrary", "arbitrary"),
        name="adaln",
    )(cond, w_mod, b_mod.reshape(DEPTH, 1, 6 * D_MODEL))


def _proj_kernel(xc_ref, xl_ref, mod_ref, g_ref, w_ref, *rest):
    out_refs, wb_ref = rest[:-1], rest[-1]

    @pl.when(pl.program_id(0) == 0)
    def _():
        wb_ref[...] = w_ref[...].astype(BF16)

    x = _pick_stream(xc_ref, xl_ref)
    h = _rms_mod(x, g_ref[...], mod_ref[:, 0:D_MODEL], mod_ref[:, D_MODEL:2 * D_MODEL]).astype(BF16)
    for o_ref, a, b in zip(out_refs, SPLITS[:-1], SPLITS[1:]):
        o_ref[...] = _dot(h, wb_ref[:, a:b])


def _project(l, x_ctx, x_lat, mod4, g_mix, w_in):
    widths = [b - a for a, b in zip(SPLITS[:-1], SPLITS[1:])]
    xc_spec, xl_spec = _two_stream_specs(D_MODEL)
    return pl.pallas_call(
        _proj_kernel,
        grid=(N_TOK // TM,),
        in_specs=[
            xc_spec, xl_spec,
            pl.BlockSpec((None, None, 1, 6 * D_MODEL), lambda i: (l, _cond_of_block(i), 0, 0)),
            pl.BlockSpec((None, 1, D_MODEL), lambda i: (l, 0, 0)),
            pl.BlockSpec((None, D_MODEL, N_IN), lambda i: (l, 0, 0)),
        ],
        out_specs=[pl.BlockSpec((TM, w), lambda i: (i, 0)) for w in widths],
        out_shape=[jax.ShapeDtypeStruct((N_TOK, w), F32) for w in widths],
        scratch_shapes=[pltpu.VMEM((D_MODEL, N_IN), BF16)],
        compiler_params=_params("arbitrary"),
        name=f"project{l}",
    )(x_ctx, x_lat, mod4, g_mix.reshape(DEPTH, 1, D_MODEL), w_in)


def _lane_is_low(shape):
    return lax.broadcasted_iota(jnp.int32, shape, len(shape) - 1) < HEAD_DIM


def _swap_halves(x):
    return pltpu.roll(x, HEAD_DIM, axis=x.ndim - 1)


def _pair_attention(q_pair, kv_sets, extra_logit):
    low = _lane_is_low(q_pair.shape)
    outs = []
    for half in (0, 1):
        keep = low if half == 0 else jnp.logical_not(low)
        qm = jnp.where(keep, q_pair, 0.0).astype(BF16)
        scores = []
        for k, _, post in kv_sets[half]:
            s = _dot_nt(qm, k)
            scores.append(post(s) if post is not None else s)
        probs = _softmax_parts(scores, extra_logit[half])
        o = None
        for p, (_, v, _) in zip(probs, kv_sets[half]):
            t = _dot(p.astype(BF16), v)
            o = t if o is None else o + t
        outs.append(o)
    return jnp.where(low, outs[0], outs[1])


def _ctx_mixer_kernel(l, sink_ref, f_ref, qw_ref, kw_ref, vw_ref, qn_ref, kn_ref, vn_ref,
                      bc_ref, bs_ref, cs_ref, ss_ref, o_ref):
    f_hi, f_lo = _split(f_ref[...])
    fc_hi, fc_lo = _split(_dot3(f_hi, f_lo, *_split(bc_ref[...])))
    fs_hi, fs_lo = _split(_dot3(f_hi, f_lo, *_split(bs_ref[...])))
    z = _dot3(*_split(cs_ref[...]), fc_hi, fc_lo) - _dot3(*_split(ss_ref[...]), fs_hi, fs_lo)
    o_ref[:, 0:F_WIDTH] = (z * (SEQ * HEAD_DIM) ** -0.5).astype(BF16)

    k = kw_ref[...].astype(BF16)
    v = vw_ref[...].astype(BF16)
    k_sw = _swap_halves(kw_ref[...]).astype(BF16)
    v_sw = _swap_halves(vw_ref[...]).astype(BF16)
    group = WIN_HEADS // WIN_KV_HEADS
    for j in range(WIN_HEADS // 2):
        q_pair = qw_ref[:, LANES * j:LANES * (j + 1)] * ATTN_SCALE
        kv_sets, sinks = [], []
        for half in (0, 1):
            h = 2 * j + half
            same = (h // group) == half
            kv_sets.append([(k if same else k_sw, v if same else v_sw, None)])
            sinks.append(sink_ref[l, h])
        o = _pair_attention(q_pair, kv_sets, sinks)
        o_ref[:, F_WIDTH + LANES * j:F_WIDTH + LANES * (j + 1)] = o.astype(BF16)

    for j in range(NA_HEADS // 2):
        sl = slice(LANES * j, LANES * (j + 1))
        q_pair = qn_ref[:, sl] * ATTN_SCALE
        kp = kn_ref[:, sl].astype(BF16)
        vp = vn_ref[:, sl].astype(BF16)
        o = _pair_attention(q_pair, [[(kp, vp, None)], [(kp, vp, None)]], [None, None])
        base = F_WIDTH + WIN_Q + LANES * j
        o_ref[:, base:base + LANES] = o.astype(BF16)


def _ctx_mixer(l, win_sink, proj, dft_ch, dft_seq):
    f, qw, kw, vw, qn, kn, vn = proj
    row = lambda w: pl.BlockSpec((SEQ, w), lambda b: (b, 0))
    const = lambda n: pl.BlockSpec((n, n), lambda b: (0, 0))
    return pl.pallas_call(
        functools.partial(_ctx_mixer_kernel, l),
        grid=(BATCH,),
        in_specs=[pl.BlockSpec(memory_space=pltpu.SMEM),
                  row(F_WIDTH), row(WIN_Q), row(WIN_KV), row(WIN_KV), row(NA_W), row(NA_W), row(NA_W)]
                 + [const(F_WIDTH)] * 2 + [const(SEQ)] * 2,
        out_specs=pl.BlockSpec((SEQ, D_MODEL), lambda b: (b, 0)),
        out_shape=jax.ShapeDtypeStruct((N_CTX, D_MODEL), BF16),
        compiler_params=_params("arbitrary"),
        name=f"ctx_mixer{l}",
    )(win_sink, f, qw, kw, vw, qn, kn, vn, *dft_ch, *dft_seq)


FT_ROWS = 256


def _lat_fourier_kernel(f_ref, bc_ref, bs_ref, cs_ref, ss_ref, o_ref, st_ref):
    b = pl.program_id(1)

    @pl.when(pl.program_id(0) == 0)
    def _():
        f_hi, f_lo = _split(f_ref[pl.ds(pl.multiple_of(b * DEC_SEQ, DEC_SEQ), DEC_SEQ), :])
        fc_hi, fc_lo = _split(_dot3(f_hi, f_lo, *_split(bc_ref[...])))
        fs_hi, fs_lo = _split(_dot3(f_hi, f_lo, *_split(bs_ref[...])))
        st_ref[b, 0] = fc_hi
        st_ref[b, 1] = fc_lo
        st_ref[b, 2] = fs_hi
        st_ref[b, 3] = fs_lo

    z = (_dot3(*_split(cs_ref[...]), st_ref[b, 0], st_ref[b, 1])
         - _dot3(*_split(ss_ref[...]), st_ref[b, 2], st_ref[b, 3]))
    o_ref[...] = (z * (DEC_SEQ * HEAD_DIM) ** -0.5).astype(BF16)


def _lat_fourier(f, dft_ch, dft_seq):
    nrb = DEC_SEQ // FT_ROWS
    const = pl.BlockSpec((F_WIDTH, F_WIDTH), lambda r, b: (0, 0))
    rows = pl.BlockSpec((FT_ROWS, DEC_SEQ), lambda r, b: (r, 0))
    return pl.pallas_call(
        _lat_fourier_kernel,
        grid=(nrb, DEC_BATCH),
        in_specs=[pl.BlockSpec((N_LAT, F_WIDTH), lambda r, b: (1, 0))] + [const] * 2 + [rows] * 2,
        out_specs=pl.BlockSpec((FT_ROWS, F_WIDTH), lambda r, b: (b * nrb + r, 0)),
        out_shape=jax.ShapeDtypeStruct((N_LAT, F_WIDTH), BF16),
        scratch_shapes=[pltpu.VMEM((DEC_BATCH, 4, DEC_SEQ, F_WIDTH), BF16)],
        compiler_params=_params("arbitrary", "arbitrary"),
        name="lat_fourier",
    )(f, *dft_ch, *dft_seq)


def _rope(x, cos, sin_signed):
    n = x.shape[-1]
    lane = lax.broadcasted_iota(jnp.int32, x.shape, x.ndim - 1)
    first = (lane % 32) < 16
    partner = jnp.where(first, pltpu.roll(x, n - 16, axis=x.ndim - 1), pltpu.roll(x, 16, axis=x.ndim - 1))
    return x * cos + partner * sin_signed


def _win_kernel(l, sink_ref, q_ref, k_ref, v_ref, ck_ref, cv_ref, cos_ref, sin_ref, cosq_ref, sinq_ref,
                o_ref, kp_ref, vp_ref, cp_ref):
    n = pl.program_id(1)
    nb = DEC_SEQ // WIN_BLOCK
    pad = WIN_BLOCK

    @pl.when(n == 0)
    def _():
        zeros = jnp.zeros((pad, LANES), BF16)
        kr = _rope(k_ref[...], cos_ref[...], sin_ref[...])
        v = v_ref[...]
        for idx, (kk, vv) in enumerate(((kr, v), (_swap_halves(kr), _swap_halves(v)))):
            kp_ref[idx, 0:pad] = zeros
            kp_ref[idx, pad + DEC_SEQ:] = zeros
            kp_ref[idx, pad:pad + DEC_SEQ] = kk.astype(BF16)
            vp_ref[idx, 0:pad] = zeros
            vp_ref[idx, pad + DEC_SEQ:] = zeros
            vp_ref[idx, pad:pad + DEC_SEQ] = vv.astype(BF16)
        ck = ck_ref[...]
        cv = cv_ref[...]
        cp_ref[0] = ck.astype(BF16)
        cp_ref[1] = _swap_halves(ck).astype(BF16)
        cp_ref[2] = cv.astype(BF16)
        cp_ref[3] = _swap_halves(cv).astype(BF16)

    i = lax.broadcasted_iota(jnp.int32, (WIN_BLOCK, 3 * WIN_BLOCK), 0)
    j = lax.broadcasted_iota(jnp.int32, (WIN_BLOCK, 3 * WIN_BLOCK), 1)
    lo = jnp.where(n == 0, WIN_BLOCK, 0)
    hi = jnp.where(n == nb - 1, 2 * WIN_BLOCK, 3 * WIN_BLOCK)
    mask = (j >= i + WIN_BLOCK - WINDOW) & (j <= i + WIN_BLOCK + WINDOW) & (j >= lo) & (j < hi)
    band = lambda s: jnp.where(mask, s, NEG_INF)

    start = pl.multiple_of(n * WIN_BLOCK, WIN_BLOCK)
    win = pl.ds(start, 3 * WIN_BLOCK)
    group = WIN_HEADS // WIN_KV_HEADS
    for jp in range(WIN_HEADS // 2):
        q_pair = _rope(q_ref[:, LANES * jp:LANES * (jp + 1)], cosq_ref[...], sinq_ref[...]) * ATTN_SCALE
        kv_sets, sinks = [], []
        for half in (0, 1):
            h = 2 * jp + half
            idx = 0 if (h // group) == half else 1
            kv_sets.append([(kp_ref[idx, win, :], vp_ref[idx, win, :], band),
                            (cp_ref[idx], cp_ref[2 + idx], None)])
            sinks.append(sink_ref[l, h])
        o = _pair_attention(q_pair, kv_sets, sinks)
        o_ref[:, LANES * jp:LANES * (jp + 1)] = o.astype(BF16)


def _lat_window(l, win_sink, qw, kw, vw, cache_k, cache_v, cos_t, sin_t):
    nb = DEC_SEQ // WIN_BLOCK
    lat_blk0 = N_CTX // DEC_SEQ
    kv_spec = pl.BlockSpec((DEC_SEQ, WIN_KV), lambda b, n: (lat_blk0 + b, 0))
    cache_spec = pl.BlockSpec((None, None, PAST_LEN, WIN_KV), lambda b, n: (b, l, 0, 0))
    tab_all = pl.BlockSpec((DEC_SEQ, LANES), lambda b, n: (0, 0))
    tab_blk = pl.BlockSpec((WIN_BLOCK, LANES), lambda b, n: (n, 0))
    return pl.pallas_call(
        functools.partial(_win_kernel, l),
        grid=(DEC_BATCH, nb),
        in_specs=[pl.BlockSpec(memory_space=pltpu.SMEM),
                  pl.BlockSpec((WIN_BLOCK, WIN_Q), lambda b, n: (N_CTX // WIN_BLOCK + b * nb + n, 0)),
                  kv_spec, kv_spec, cache_spec, cache_spec, tab_all, tab_all, tab_blk, tab_blk],
        out_specs=pl.BlockSpec((WIN_BLOCK, WIN_Q), lambda b, n: (b * nb + n, 0)),
        out_shape=jax.ShapeDtypeStruct((N_LAT, WIN_Q), BF16),
        scratch_shapes=[pltpu.VMEM((2, DEC_SEQ + 2 * WIN_BLOCK, LANES), BF16),
                        pltpu.VMEM((2, DEC_SEQ + 2 * WIN_BLOCK, LANES), BF16),
                        pltpu.VMEM((4, PAST_LEN, LANES), BF16)],
        compiler_params=_params("arbitrary", "arbitrary"),
        name=f"lat_window{l}",
    )(win_sink, qw, kw, vw, cache_k, cache_v, cos_t, sin_t, cos_t, sin_t)


def _na_row_start(r):
    return jnp.clip(r - NA_ROWS // 2, 0, GRID_ROWS - NA_ROWS)


def _na_kernel(q_ref, k_ref, v_ref, ck_ref, cv_ref, bias_ref, o_ref):
    r = pl.program_id(1)
    start = pl.multiple_of(_na_row_start(r) * GRID_W, GRID_W)
    win = pl.ds(start, NA_ROWS * GRID_W)
    for jp in range(NA_HEADS // 2):
        sl = slice(LANES * jp, LANES * (jp + 1))
        q_pair = q_ref[:, sl] * ATTN_SCALE
        kp = k_ref[win, sl].astype(BF16)
        vp = v_ref[win, sl].astype(BF16)
        ckp = ck_ref[:, sl].astype(BF16)
        cvp = cv_ref[:, sl].astype(BF16)
        kv_sets = []
        for half in (0, 1):
            bias = bias_ref[2 * jp + half]
            kv_sets.append([(kp, vp, lambda s, bias=bias: s + bias), (ckp, cvp, None)])
        o = _pair_attention(q_pair, kv_sets, [None, None])
        o_ref[:, sl] = o.astype(BF16)


def _lat_neighbourhood(l, qn, kn, vn, cache_k, cache_v, bias_tab):
    lat_blk0 = N_CTX // DEC_SEQ
    kv_spec = pl.BlockSpec((DEC_SEQ, NA_W), lambda b, r: (lat_blk0 + b, 0))
    cache_spec = pl.BlockSpec((None, None, PAST_LEN, NA_W), lambda b, r: (b, l, 0, 0))
    return pl.pallas_call(
        _na_kernel,
        grid=(DEC_BATCH, GRID_ROWS),
        in_specs=[pl.BlockSpec((GRID_W, NA_W), lambda b, r: (N_CTX // GRID_W + b * GRID_ROWS + r, 0)),
                  kv_spec, kv_spec, cache_spec, cache_spec,
                  pl.BlockSpec((None, NA_HEADS, GRID_W, NA_ROWS * GRID_W),
                               lambda b, r: (_na_row_start(r) - r + NA_ROWS - 1, 0, 0, 0))],
        out_specs=pl.BlockSpec((GRID_W, NA_W), lambda b, r: (b * GRID_ROWS + r, 0)),
        out_shape=jax.ShapeDtypeStruct((N_LAT, NA_W), BF16),
        compiler_params=_params("arbitrary", "arbitrary"),
        name=f"lat_neighbourhood{l}",
    )(qn, kn, vn, cache_k, cache_v, bias_tab)


def _outproj_kernel(xc_ref, xl_ref, mc_ref, mf_ref, mw_ref, mn_ref, mod_ref, g_ref, w_ref, r_hi, r_lo,
                    x_ref, h_ref, lg_ref, wb_ref):
    is_ctx = pl.program_id(0) < N_CTX // TM

    @pl.when(pl.program_id(0) == 0)
    def _():
        wb_ref[...] = w_ref[...].astype(BF16)

    a, b = F_WIDTH, F_WIDTH + WIN_Q
    mc = mc_ref[...]
    mf = jnp.where(is_ctx, mc[:, 0:a], mf_ref[...])
    mw = jnp.where(is_ctx, mc[:, a:b], mw_ref[...])
    mn = jnp.where(is_ctx, mc[:, b:], mn_ref[...])
    mixed = _dot(mf, wb_ref[0:a, :]) + _dot(mw, wb_ref[a:b, :]) + _dot(mn, wb_ref[b:, :])
    d = D_MODEL
    x = jnp.where(is_ctx, xc_ref[...], xl_ref[...]) + mod_ref[:, 2 * d:3 * d] * mixed
    x_ref[...] = x
    h = _rms_mod(x, g_ref[...], mod_ref[:, 3 * d:4 * d], mod_ref[:, 4 * d:5 * d])
    h_ref[...] = h.astype(BF16)
    h_hi, h_lo = _split(h)
    lg_ref[...] = _dot3(h_hi, h_lo, r_hi[...], r_lo[...])


def _outproj(l, x_ctx, x_lat, mixed_ctx, mixed_f, mixed_w, mixed_n, mod4, g_ffn, w_out, r_hi, r_lo):
    n_ctx_blocks = N_CTX // TM
    xc_spec, xl_spec = _two_stream_specs(D_MODEL)
    lat = lambda w: pl.BlockSpec((TM, w), lambda i: (jnp.maximum(i - n_ctx_blocks, 0), 0))
    whole = lambda shape: pl.BlockSpec(shape, lambda i: (0,) * len(shape))
    row = lambda w: pl.BlockSpec((TM, w), lambda i: (i, 0))
    return pl.pallas_call(
        _outproj_kernel,
        grid=(N_TOK // TM,),
        in_specs=[xc_spec, xl_spec,
                  pl.BlockSpec((TM, D_MODEL), lambda i: (jnp.minimum(i, n_ctx_blocks - 1), 0)),
                  lat(F_WIDTH), lat(WIN_Q), lat(NA_W),
                  pl.BlockSpec((None, None, 1, 6 * D_MODEL), lambda i: (l, _cond_of_block(i), 0, 0)),
                  pl.BlockSpec((None, 1, D_MODEL), lambda i: (l, 0, 0)),
                  pl.BlockSpec((None, D_MODEL, D_MODEL), lambda i: (l, 0, 0)),
                  whole((D_MODEL, LANES)), whole((D_MODEL, LANES))],
        out_specs=[row(D_MODEL), row(D_MODEL), row(LANES)],
        out_shape=[jax.ShapeDtypeStruct((N_TOK, D_MODEL), F32),
                   jax.ShapeDtypeStruct((N_TOK, D_MODEL), BF16),
                   jax.ShapeDtypeStruct((N_TOK, LANES), F32)],
        scratch_shapes=[pltpu.VMEM((D_MODEL, D_MODEL), BF16)],
        compiler_params=_params("arbitrary"),
        name=f"outproj{l}",
    )(x_ctx, x_lat, mixed_ctx, mixed_f, mixed_w, mixed_n, mod4, g_ffn.reshape(DEPTH, 1, D_MODEL), w_out,
      r_hi, r_lo)


PREFIX_CHUNK = 256
MANTISSA_STEPS = 44


def _prefix_exclusive(m):
    rows, n = m.shape
    t0 = lax.broadcasted_iota(jnp.int32, (PREFIX_CHUNK, PREFIX_CHUNK), 0)
    t1 = lax.broadcasted_iota(jnp.int32, (PREFIX_CHUNK, PREFIX_CHUNK), 1)
    upper = jnp.where(t0 < t1, 1.0, 0.0).astype(BF16)
    carry = jnp.zeros((rows, 1), F32)
    outs = []
    for c in range(n // PREFIX_CHUNK):
        blk = m[:, c * PREFIX_CHUNK:(c + 1) * PREFIX_CHUNK]
        outs.append(_dot(blk.astype(BF16), upper) + carry)
        carry = carry + blk.sum(axis=-1, keepdims=True)
    return outs[0] if len(outs) == 1 else jnp.concatenate(outs, axis=-1)


def _route_kernel(cap, lg_ref, slot_ref, aff_ref):
    x = lg_ref[...]
    e = jnp.exp(x - x.max(axis=1, keepdims=True))
    aff3 = e / e.sum(axis=1, keepdims=True)
    aff = aff3.reshape(aff3.shape[0] * aff3.shape[1], aff3.shape[2])
    capf = float(cap)

    def count_ge(t):
        return jnp.where(aff >= t, 1.0, 0.0).sum(axis=-1, keepdims=True)

    above = jnp.full((aff.shape[0], 1), 2.0, F32)
    for s in (64, 32, 16, 8, 4, 2, 1):
        cand = above * (2.0 ** -s)
        above = jnp.where(count_ge(cand) >= capf, above, cand)
    base = above * 0.5
    base = jnp.where(count_ge(base) >= capf, base, 0.0)

    def refine(_, carry):
        thr, inc = carry
        cand = thr + inc
        return jnp.where(count_ge(cand) >= capf, cand, thr), inc * 0.5

    thr, _ = lax.fori_loop(0, MANTISSA_STEPS, refine, (base, base * 0.5))
    gt = jnp.where(aff > thr, 1.0, 0.0)
    eq = jnp.where(aff == thr, 1.0, 0.0)
    need = capf - gt.sum(axis=-1, keepdims=True)
    sel = gt + eq * jnp.where(_prefix_exclusive(eq) < need, 1.0, 0.0)
    slot = _prefix_exclusive(sel)
    slot_ref[...] = jnp.where(sel > 0.0, slot, -1.0).astype(jnp.int32)
    aff_ref[...] = aff


def _route(lg3, cap, name):
    groups, n_e, n = lg3.shape
    rows = groups * n_e
    return pl.pallas_call(
        functools.partial(_route_kernel, cap),
        out_shape=[jax.ShapeDtypeStruct((rows, n), jnp.int32), jax.ShapeDtypeStruct((rows, n), F32)],
        compiler_params=pltpu.CompilerParams(vmem_limit_bytes=VMEM_LIMIT),
        name=name,
    )(lg3)


def _gather_ctx_kernel(h_ref, slot_ref, aff_ref, x_ref, g_ref, p_ref):
    s_iota = lax.broadcasted_iota(jnp.int32, (CAP_CTX, SEQ), 0)
    for e in range(N_EXPERTS):
        hit = s_iota == slot_ref[e:e + 1, :]
        p_ref[e * CAP_CTX:(e + 1) * CAP_CTX, :] = jnp.where(hit, 1.0, 0.0).astype(BF16)
        g_ref[e] = jnp.where(hit, aff_ref[e:e + 1, :], 0.0).sum(axis=-1, keepdims=True)
    x = _dot(p_ref[...], h_ref[...]).astype(BF16)
    x_ref[...] = x.reshape(N_EXPERTS, CAP_CTX, D_MODEL)


def _gather_ctx(h, slot, aff):
    return pl.pallas_call(
        _gather_ctx_kernel,
        grid=(BATCH,),
        in_specs=[pl.BlockSpec((SEQ, D_MODEL), lambda b: (b, 0)),
                  pl.BlockSpec((N_EXPERTS, SEQ), lambda b: (b, 0)),
                  pl.BlockSpec((N_EXPERTS, SEQ), lambda b: (b, 0))],
        out_specs=[pl.BlockSpec((N_EXPERTS, CAP_CTX, D_MODEL), lambda b: (0, b, 0)),
                   pl.BlockSpec((N_EXPERTS, CAP_CTX, 1), lambda b: (0, b, 0))],
        out_shape=[jax.ShapeDtypeStruct((N_EXPERTS, ROWS_CTX, D_MODEL), BF16),
                   jax.ShapeDtypeStruct((N_EXPERTS, ROWS_CTX, 1), F32)],
        scratch_shapes=[pltpu.VMEM((N_EXPERTS * CAP_CTX, SEQ), BF16)],
        compiler_params=_params("arbitrary"),
        name="gather_ctx",
    )(h, slot, aff)


def _gather_lat_kernel(h_ref, slot_ref, aff_ref, x_ref, g_ref):
    e = pl.program_id(1)
    s_iota = lax.broadcasted_iota(jnp.int32, (CAP_LAT, DEC_SEQ), 0)
    hit = s_iota == slot_ref[pl.ds(e, 1), :]
    x_ref[...] = _dot(jnp.where(hit, 1.0, 0.0).astype(BF16), h_ref[...]).astype(BF16)
    g_ref[...] = jnp.where(hit, aff_ref[pl.ds(e, 1), :], 0.0).sum(axis=-1, keepdims=True)


def _gather_lat(h, slot, aff):
    lat_blk0 = N_CTX // DEC_SEQ
    return pl.pallas_call(
        _gather_lat_kernel,
        grid=(DEC_BATCH, N_EXPERTS),
        in_specs=[pl.BlockSpec((DEC_SEQ, D_MODEL), lambda b, e: (lat_blk0 + b, 0)),
                  pl.BlockSpec((N_EXPERTS, DEC_SEQ), lambda b, e: (b, 0)),
                  pl.BlockSpec((N_EXPERTS, DEC_SEQ), lambda b, e: (b, 0))],
        out_specs=[pl.BlockSpec((None, CAP_LAT, D_MODEL), lambda b, e: (e, b, 0)),
                   pl.BlockSpec((None, CAP_LAT, 1), lambda b, e: (e, b, 0))],
        out_shape=[jax.ShapeDtypeStruct((N_EXPERTS, ROWS_LAT, D_MODEL), BF16),
                   jax.ShapeDtypeStruct((N_EXPERTS, ROWS_LAT, 1), F32)],
        compiler_params=_params("arbitrary", "arbitrary"),
        name="gather_lat",
    )(h, slot, aff)


def _ffn_kernel(xc_ref, xl_ref, gc_ref, gl_ref, wg_ref, wu_ref, wd_ref, y_ref, x_sc, acc_sc):
    j = pl.program_id(1)

    @pl.when(j == 0)
    def _():
        x_sc[0:ROWS_CTX, :] = xc_ref[...]
        x_sc[ROWS_CTX:, :] = xl_ref[...]

    x = x_sc[...]
    a = _dot(x, wg_ref[...].astype(BF16))
    u = _dot(x, wu_ref[...].astype(BF16))
    part = _dot((_silu(a) * u).astype(BF16), wd_ref[...].astype(BF16))

    @pl.when(j == 0)
    def _():
        acc_sc[...] = part

    @pl.when(j > 0)
    def _():
        acc_sc[...] += part

    @pl.when(j == pl.num_programs(1) - 1)
    def _():
        y_ref[0:ROWS_CTX, :] = (acc_sc[0:ROWS_CTX, :] * gc_ref[...]).astype(BF16)
        y_ref[ROWS_CTX:, :] = (acc_sc[ROWS_CTX:, :] * gl_ref[...]).astype(BF16)


def _ffn(l, x_c, x_l, g_c, g_l, w_gate, w_up, w_down):
    rows = ROWS_CTX + ROWS_LAT
    xin = lambda r: pl.BlockSpec((None, r, D_MODEL), lambda e, j: (e, 0, 0))
    gin = lambda r: pl.BlockSpec((None, r, 1), lambda e, j: (e, 0, 0))
    return pl.pallas_call(
        _ffn_kernel,
        grid=(N_EXPERTS, D_FF // TF),
        in_specs=[xin(ROWS_CTX), xin(ROWS_LAT), gin(ROWS_CTX), gin(ROWS_LAT),
                  pl.BlockSpec((None, None, D_MODEL, TF), lambda e, j: (l, e, 0, j)),
                  pl.BlockSpec((None, None, D_MODEL, TF), lambda e, j: (l, e, 0, j)),
                  pl.BlockSpec((None, None, TF, D_MODEL), lambda e, j: (l, e, j, 0))],
        out_specs=pl.BlockSpec((None, rows, D_MODEL), lambda e, j: (e, 0, 0)),
        out_shape=jax.ShapeDtypeStruct((N_EXPERTS, rows, D_MODEL), BF16),
        scratch_shapes=[pltpu.VMEM((rows, D_MODEL), BF16), pltpu.VMEM((rows, D_MODEL), F32)],
        compiler_params=_params("arbitrary", "arbitrary"),
        name=f"experts{l}",
    )(x_c, x_l, g_c, g_l, w_gate, w_up, w_down)


def _finish(x, res, mod_ref, gf_ref, final):
    y = x + mod_ref[:, 5 * D_MODEL:] * res
    if final:
        y = y * lax.rsqrt(jnp.mean(y * y, axis=-1, keepdims=True) + RMS_EPS) * gf_ref[...]
    return y


def _combine_ctx_kernel(final, x_ref, y_ref, slot_ref, rep_ref, mod_ref, gf_ref, o_ref):
    n_col = N_EXPERTS * CAP_CTX
    spread = _dot(slot_ref[...].astype(F32).astype(BF16), rep_ref[...])
    col = lax.broadcasted_iota(jnp.int32, (SEQ, n_col), 1) % CAP_CTX
    p = jnp.where(spread == col.astype(F32), 1.0, 0.0).astype(BF16)
    res = _dot(p, y_ref[...].reshape(n_col, D_MODEL))
    o_ref[...] = _finish(x_ref[...], res, mod_ref, gf_ref, final)


def _combine_ctx(l, final, x_new, y, slot_t, mod4, g_final):
    n_col = N_EXPERTS * CAP_CTX
    rep = (np.arange(n_col)[None, :] // CAP_CTX == np.arange(N_EXPERTS)[:, None]).astype(np.float32)
    return pl.pallas_call(
        functools.partial(_combine_ctx_kernel, final),
        grid=(BATCH,),
        in_specs=[pl.BlockSpec((SEQ, D_MODEL), lambda b: (b, 0)),
                  pl.BlockSpec((N_EXPERTS, CAP_CTX, D_MODEL), lambda b: (0, b, 0)),
                  pl.BlockSpec((SEQ, N_EXPERTS), lambda b: (b, 0)),
                  pl.BlockSpec((N_EXPERTS, n_col), lambda b: (0, 0)),
                  pl.BlockSpec((None, None, 1, 6 * D_MODEL), lambda b: (l, 0, 0, 0)),
                  pl.BlockSpec((1, D_MODEL), lambda b: (0, 0))],
        out_specs=pl.BlockSpec((SEQ, D_MODEL), lambda b: (b, 0)),
        out_shape=jax.ShapeDtypeStruct((N_CTX, D_MODEL), F32),
        compiler_params=_params("arbitrary"),
        name=f"combine_ctx{l}",
    )(x_new, y, slot_t, jnp.asarray(rep, BF16), mod4, g_final.reshape(1, D_MODEL))


TMC = 512


def _combine_lat_kernel(final, x_ref, y_ref, slot_ref, mod_ref, gf_ref, o_ref):
    s_iota = lax.broadcasted_iota(jnp.int32, (TMC, CAP_LAT), 1)
    slot = slot_ref[...]
    res = None
    for e in range(N_EXPERTS):
        p = jnp.where(slot[:, e:e + 1] == s_iota, 1.0, 0.0).astype(BF16)
        t = _dot(p, y_ref[e])
        res = t if res is None else res + t
    o_ref[...] = _finish(x_ref[...], res, mod_ref, gf_ref, final)


def _combine_lat(l, final, x_new, y, slot_t, mod4, g_final):
    nt = DEC_SEQ // TMC
    return pl.pallas_call(
        functools.partial(_combine_lat_kernel, final),
        grid=(DEC_BATCH, nt),
        in_specs=[pl.BlockSpec((TMC, D_MODEL), lambda b, t: (N_CTX // TMC + b * nt + t, 0)),
                  pl.BlockSpec((N_EXPERTS, CAP_LAT, D_MODEL), lambda b, t: (0, ROWS_CTX // CAP_LAT + b, 0)),
                  pl.BlockSpec((TMC, N_EXPERTS), lambda b, t: (b * nt + t, 0)),
                  pl.BlockSpec((None, None, 1, 6 * D_MODEL), lambda b, t: (l, 1 + b, 0, 0)),
                  pl.BlockSpec((1, D_MODEL), lambda b, t: (0, 0))],
        out_specs=pl.BlockSpec((TMC, D_MODEL), lambda b, t: (b * nt + t, 0)),
        out_shape=jax.ShapeDtypeStruct((N_LAT, D_MODEL), F32),
        compiler_params=_params("arbitrary", "arbitrary"),
        name=f"combine_lat{l}",
    )(x_new, y, slot_t, mod4, g_final.reshape(1, D_MODEL))


def _split_table(t):
    hi = t.astype(BF16)
    return hi, (t - hi.astype(F32)).astype(BF16)


def _dft_tables(n):
    p = np.arange(n, dtype=np.int64)
    ang = ((p[:, None] * p[None, :]) % n).astype(np.float64) * (2.0 * np.pi / n)
    return np.cos(ang).astype(np.float32), np.sin(ang).astype(np.float32)


def _channel_dft_tables():
    c = np.arange(F_WIDTH, dtype=np.int64)
    same = (c[:, None] // HEAD_DIM) == (c[None, :] // HEAD_DIM)
    ang = (((c[:, None] % HEAD_DIM) * (c[None, :] % HEAD_DIM)) % HEAD_DIM).astype(np.float64) * (2.0 * np.pi / HEAD_DIM)
    return (np.where(same, np.cos(ang), 0.0).astype(np.float32),
            np.where(same, np.sin(ang), 0.0).astype(np.float32))


def _rope_tables():
    half = HEAD_DIM // 2
    nf = half // 2
    pos = jnp.arange(DEC_SEQ)
    inv = 1.0 / (ROPE_BASE ** (jnp.arange(nf, dtype=F32) / nf))
    ang_r = (pos // GRID_W).astype(F32)[:, None] * inv
    ang_c = (pos % GRID_W).astype(F32)[:, None] * inv

    def head(fn, sign):
        return jnp.concatenate([sign * fn(ang_r), fn(ang_r), sign * fn(ang_c), fn(ang_c)], axis=-1)

    cos = head(jnp.cos, 1.0)
    sin = head(jnp.sin, -1.0)
    return jnp.concatenate([cos, cos], axis=-1), jnp.concatenate([sin, sin], axis=-1)


def _na_bias_table(rpb):
    cq = np.arange(GRID_W)
    rel_c = np.clip(cq[None, :] - cq[:, None] + NA_COLS - 1, 0, 2 * NA_COLS - 2)
    pick = (rel_c[:, :, None] == np.arange(2 * NA_COLS - 1)).astype(np.float32)
    cs = np.clip(cq - NA_COLS // 2, 0, GRID_W - NA_COLS)
    col_mask = (cq[None, :] >= cs[:, None]) & (cq[None, :] < cs[:, None] + NA_COLS)
    b = jnp.einsum('hrj,qkj->hqrk', rpb, pick, precision=lax.Precision.HIGHEST)
    b = jnp.where(col_mask[None, :, None, :], b, NEG_INF)
    tabs = jnp.stack([b[:, :, o:o + NA_ROWS] for o in range(NA_ROWS)])
    return tabs.reshape(NA_ROWS, NA_HEADS, GRID_W, NA_ROWS * GRID_W)


def kernel(x_prompt, x_sample, cache_win_k, cache_win_v, cache_nat_k, cache_nat_v, c, c_ctx, w_mod, b_mod, g_mix, g_ffn, w_in, w_out, win_sink, nat_rpb, w_router, w_gate, w_up, w_down, g_final):
    x_ctx = x_prompt.reshape(N_CTX, D_MODEL)
    x_lat = x_sample.reshape(N_LAT, D_MODEL)
    cond = jnp.concatenate([c_ctx[None, :], c, jnp.zeros((N_COND - 1 - DEC_BATCH, D_MODEL), F32)], axis=0)
    mod4 = _adaln(cond, w_mod, b_mod).reshape(DEPTH, N_COND, 1, 6 * D_MODEL)

    dft_ch = _channel_dft_tables()
    dft_ctx = _dft_tables(SEQ)
    dft_lat = _dft_tables(DEC_SEQ)
    cos_t, sin_t = _rope_tables()
    cwk = cache_win_k.reshape(DEC_BATCH, DEPTH, PAST_LEN, WIN_KV)
    cwv = cache_win_v.reshape(DEC_BATCH, DEPTH, PAST_LEN, WIN_KV)
    cnk = cache_nat_k.reshape(DEC_BATCH, DEPTH, PAST_LEN, NA_W)
    cnv = cache_nat_v.reshape(DEC_BATCH, DEPTH, PAST_LEN, NA_W)
    r_pad = jnp.pad(w_router, ((0, 0), (0, 0), (0, LANES - N_EXPERTS)))

    new_kv = [[], [], [], []]
    for l in range(DEPTH):
        final = l == DEPTH - 1
        proj = _project(l, x_ctx, x_lat, mod4, g_mix, w_in)
        f, qw, kw, vw, qn, kn, vn = proj
        for dst, src, heads in zip(new_kv, (kw, vw, kn, vn), (WIN_KV_HEADS, WIN_KV_HEADS, NA_HEADS, NA_HEADS)):
            dst.append(src[:N_CTX].reshape(BATCH, SEQ, heads, HEAD_DIM))

        mixed_ctx = _ctx_mixer(l, win_sink, proj, dft_ch, dft_ctx)
        mixed_f = _lat_fourier(f, dft_ch, dft_lat)
        mixed_w = _lat_window(l, win_sink, qw, kw, vw, cwk, cwv, cos_t, sin_t)
        mixed_n = _lat_neighbourhood(l, qn, kn, vn, cnk, cnv, _na_bias_table(nat_rpb[l]))

        r_hi, r_lo = _split_table(r_pad[l])
        x_new, h, logits = _outproj(l, x_ctx, x_lat, mixed_ctx, mixed_f, mixed_w, mixed_n, mod4, g_ffn,
                                    w_out, r_hi, r_lo)

        lg = logits[:, :N_EXPERTS]
        lg_ctx = lg[:N_CTX].reshape(BATCH, SEQ, N_EXPERTS).transpose(0, 2, 1)
        lg_lat = lg[N_CTX:].reshape(DEC_BATCH, DEC_SEQ, N_EXPERTS).transpose(0, 2, 1)
        slot_c, aff_c = _route(lg_ctx, CAP_CTX, "route_ctx")
        slot_l, aff_l = _route(lg_lat, CAP_LAT, "route_lat")
        xg_c, gate_c = _gather_ctx(h, slot_c, aff_c)
        xg_l, gate_l = _gather_lat(h, slot_l, aff_l)
        y = _ffn(l, xg_c, xg_l, gate_c, gate_l, w_gate, w_up, w_down)
        slot_ct = slot_c.reshape(BATCH, N_EXPERTS, SEQ).transpose(0, 2, 1).reshape(N_CTX, N_EXPERTS)
        slot_lt = slot_l.reshape(DEC_BATCH, N_EXPERTS, DEC_SEQ).transpose(0, 2, 1).reshape(N_LAT, N_EXPERTS)
        x_ctx = _combine_ctx(l, final, x_new, y, slot_ct, mod4, g_final)
        x_lat = _combine_lat(l, final, x_new, y, slot_lt, mod4, g_final)

    y_prompt = x_ctx.reshape(BATCH, SEQ, D_MODEL)
    y_sample = x_lat.reshape(DEC_BATCH, DEC_SEQ, D_MODEL)
    return (y_prompt, y_sample, *(jnp.stack(t, axis=1) for t in new_kv))
```

```python
import functools

import numpy as np
import jax
import jax.numpy as jnp
from jax import lax
from jax.experimental import pallas as pl
from jax.experimental.pallas import tpu as pltpu

D_MODEL = 1024
BATCH = 16
SEQ = 256
DEPTH = 2
DEC_BATCH = 2
DEC_SEQ = 2048
PAST_LEN = 256
GRID_W = 64
HEAD_DIM = 64
F_WIDTH = 256
WIN_HEADS = 6
WIN_KV_HEADS = 2
WINDOW = 128
WIN_BLOCK = 128
NA_HEADS = 6
NA_ROWS = 8
NA_COLS = 16
N_EXPERTS = 16
EC_CAPACITY = 2
D_FF = 2816
ROPE_BASE = 10000.0
RMS_EPS = 1e-6
NEG_INF = -1e30
ATTN_SCALE = HEAD_DIM ** -0.5
WIN_Q = WIN_HEADS * HEAD_DIM
WIN_KV = WIN_KV_HEADS * HEAD_DIM
NA_W = NA_HEADS * HEAD_DIM
N_IN = F_WIDTH + WIN_Q + 2 * WIN_KV + 3 * NA_W
SPLITS = (0, F_WIDTH, F_WIDTH + WIN_Q, F_WIDTH + WIN_Q + WIN_KV, F_WIDTH + WIN_Q + 2 * WIN_KV,
          F_WIDTH + WIN_Q + 2 * WIN_KV + NA_W, F_WIDTH + WIN_Q + 2 * WIN_KV + 2 * NA_W, N_IN)

N_CTX = BATCH * SEQ
N_LAT = DEC_BATCH * DEC_SEQ
N_TOK = N_CTX + N_LAT
GRID_ROWS = DEC_SEQ // GRID_W
CAP_CTX = EC_CAPACITY * SEQ // N_EXPERTS
CAP_LAT = EC_CAPACITY * DEC_SEQ // N_EXPERTS
ROWS_CTX = BATCH * CAP_CTX
ROWS_LAT = DEC_BATCH * CAP_LAT
N_COND = 8

LANES = 128
MXU_COLS = 256
TM = 512
TN_MOD = 1536
TF = 256
TD = 256
VMEM_LIMIT = 56 * 1024 * 1024

F32 = jnp.float32
BF16 = jnp.bfloat16


def _params(*sem):
    return pltpu.CompilerParams(dimension_semantics=sem, vmem_limit_bytes=VMEM_LIMIT)


def _dot(a, b):
    return jnp.dot(a, b, preferred_element_type=F32)


def _dot_nt(a, b):
    return lax.dot_general(a, b, (((1,), (1,)), ((), ())), preferred_element_type=F32)


def _split(x):
    hi = x.astype(BF16)
    lo = (x - hi.astype(F32)).astype(BF16)
    return hi, lo


def _dot3(a_hi, a_lo, b_hi, b_lo):
    return _dot(a_hi, b_hi) + (_dot(a_lo, b_hi) + _dot(a_hi, b_lo))


def _silu(x):
    return x / (1.0 + jnp.exp(-x))


def _rms_mod(x, g, shift, scale):
    y = x * lax.rsqrt(jnp.mean(x * x, axis=-1, keepdims=True) + RMS_EPS)
    return (y * g) * (1.0 + scale) + shift


def _softmax_parts(parts, sink=None):
    m = parts[0].max(axis=-1, keepdims=True)
    for s in parts[1:]:
        m = jnp.maximum(m, s.max(axis=-1, keepdims=True))
    if sink is not None:
        m = jnp.maximum(m, sink)
    es = [jnp.exp(s - m) for s in parts]
    den = es[0].sum(axis=-1, keepdims=True)
    for e in es[1:]:
        den = den + e.sum(axis=-1, keepdims=True)
    if sink is not None:
        den = den + jnp.exp(sink - m)
    inv = 1.0 / den
    return [e * inv for e in es]


def _cond_of_block(i):
    n_ctx_blocks = N_CTX // TM
    return jnp.where(i < n_ctx_blocks, 0, 1 + (i - n_ctx_blocks) // (DEC_SEQ // TM))


def _two_stream_specs(width):
    n_ctx_blocks = N_CTX // TM
    ctx = pl.BlockSpec((TM, width), lambda i: (jnp.minimum(i, n_ctx_blocks - 1), 0))
    lat = pl.BlockSpec((TM, width), lambda i: (jnp.maximum(i - n_ctx_blocks, 0), 0))
    return ctx, lat


def _pick_stream(ctx_ref, lat_ref):
    return jnp.where(pl.program_id(0) < N_CTX // TM, ctx_ref[...], lat_ref[...])


def _adaln_kernel(c_ref, w_ref, b_ref, o_ref):
    s_hi, s_lo = _split(_silu(c_ref[...]))
    w_hi, w_lo = _split(w_ref[...])
    o_ref[...] = _dot3(s_hi, s_lo, w_hi, w_lo) + b_ref[...]


def _adaln(cond, w_mod, b_mod):
    return pl.pallas_call(
        _adaln_kernel,
        grid=(DEPTH, 6 * D_MODEL // TN_MOD),
        in_specs=[
            pl.BlockSpec((N_COND, D_MODEL), lambda l, j: (0, 0)),
            pl.BlockSpec((None, D_MODEL, TN_MOD), lambda l, j: (l, 0, j)),
            pl.BlockSpec((None, 1, TN_MOD), lambda l, j: (l, 0, j)),
        ],
        out_specs=pl.BlockSpec((None, N_COND, TN_MOD), lambda l, j: (l, 0, j)),
        out_shape=jax.ShapeDtypeStruct((DEPTH, N_COND, 6 * D_MODEL), F32),
        compiler_params=_params("arbitrary", "arbitrary"),
        name="adaln",
    )(cond, w_mod, b_mod.reshape(DEPTH, 1, 6 * D_MODEL))


def _proj_kernel(xc_ref, xl_ref, mod_ref, g_ref, w_ref, *rest):
    out_refs, wb_ref = rest[:-1], rest[-1]

    @pl.when(pl.program_id(0) == 0)
    def _():
        wb_ref[...] = w_ref[...].astype(BF16)

    x = _pick_stream(xc_ref, xl_ref)
    h = _rms_mod(x, g_ref[...], mod_ref[:, 0:D_MODEL], mod_ref[:, D_MODEL:2 * D_MODEL]).astype(BF16)
    for t0 in range(0, N_IN, MXU_COLS):
        acc = _dot(h, wb_ref[:, t0:t0 + MXU_COLS])
        for o_ref, a, b in zip(out_refs, SPLITS[:-1], SPLITS[1:]):
            lo, hi = max(a, t0), min(b, t0 + MXU_COLS)
            if lo < hi:
                o_ref[:, lo - a:hi - a] = acc[:, lo - t0:hi - t0]


def _project(l, x_ctx, x_lat, mod4, g_mix, w_in):
    widths = [b - a for a, b in zip(SPLITS[:-1], SPLITS[1:])]
    xc_spec, xl_spec = _two_stream_specs(D_MODEL)
    return pl.pallas_call(
        _proj_kernel,
        grid=(N_TOK // TM,),
        in_specs=[
            xc_spec, xl_spec,
            pl.BlockSpec((None, None, 1, 6 * D_MODEL), lambda i: (l, _cond_of_block(i), 0, 0)),
            pl.BlockSpec((None, 1, D_MODEL), lambda i: (l, 0, 0)),
            pl.BlockSpec((None, D_MODEL, N_IN), lambda i: (l, 0, 0)),
        ],
        out_specs=[pl.BlockSpec((TM, w), lambda i: (i, 0)) for w in widths],
        out_shape=[jax.ShapeDtypeStruct((N_TOK, w), F32) for w in widths],
        scratch_shapes=[pltpu.VMEM((D_MODEL, N_IN), BF16)],
        compiler_params=_params("arbitrary"),
        name=f"project{l}",
    )(x_ctx, x_lat, mod4, g_mix.reshape(DEPTH, 1, D_MODEL), w_in)


def _lane_is_low(shape):
    return lax.broadcasted_iota(jnp.int32, shape, len(shape) - 1) < HEAD_DIM


def _swap_halves(x):
    return pltpu.roll(x, HEAD_DIM, axis=x.ndim - 1)


def _win_kv_copy(h):
    return 0 if (h // (WIN_HEADS // WIN_KV_HEADS)) == (h % 2) else 1


def _stack_heads(q_pairs, heads):
    low = _lane_is_low(q_pairs[heads[0] // 2].shape)
    rows = [jnp.where(low if h % 2 == 0 else jnp.logical_not(low), q_pairs[h // 2], 0.0).astype(BF16)
            for h in heads]
    return rows[0] if len(rows) == 1 else jnp.concatenate(rows, axis=0)


def _per_head_column(values, rows_per_head):
    blk = lax.broadcasted_iota(jnp.int32, (len(values) * rows_per_head, 1), 0) // rows_per_head
    col = jnp.full(blk.shape, values[0], F32)
    for i in range(1, len(values)):
        col = jnp.where(blk == i, values[i], col)
    return col


def _attend(q_stack, kv_list, extra_logit=None):
    scores = []
    for k, _, post in kv_list:
        s = _dot_nt(q_stack, k)
        scores.append(post(s) if post is not None else s)
    probs = _softmax_parts(scores, extra_logit)
    o = None
    for p, (_, v, _) in zip(probs, kv_list):
        t = _dot(p.astype(BF16), v)
        o = t if o is None else o + t
    return o


def _merge_pair(o_even, o_odd):
    return jnp.where(_lane_is_low(o_even.shape), o_even, o_odd)


def _gqa_attention(q_pairs, rows, kv_for_copy, sinks):
    per_head = {}
    for copy in (0, 1):
        heads = [h for h in range(WIN_HEADS) if _win_kv_copy(h) == copy]
        o = _attend(_stack_heads(q_pairs, heads), kv_for_copy(copy),
                    _per_head_column([sinks[h] for h in heads], rows))
        for i, h in enumerate(heads):
            per_head[h] = o[i * rows:(i + 1) * rows]
    return [_merge_pair(per_head[2 * j], per_head[2 * j + 1]) for j in range(WIN_HEADS // 2)]


def _ctx_mixer_kernel(l, sink_ref, f_ref, qw_ref, kw_ref, vw_ref, qn_ref, kn_ref, vn_ref,
                      bc_ref, bs_ref, cs_ref, ss_ref, o_ref):
    f_hi, f_lo = _split(f_ref[...])
    fc_hi, fc_lo = _split(_dot3(f_hi, f_lo, *_split(bc_ref[...])))
    fs_hi, fs_lo = _split(_dot3(f_hi, f_lo, *_split(bs_ref[...])))
    z = _dot3(*_split(cs_ref[...]), fc_hi, fc_lo) - _dot3(*_split(ss_ref[...]), fs_hi, fs_lo)
    o_ref[:, 0:F_WIDTH] = (z * (SEQ * HEAD_DIM) ** -0.5).astype(BF16)

    kv = [(kw_ref[...].astype(BF16), vw_ref[...].astype(BF16)),
          (_swap_halves(kw_ref[...]).astype(BF16), _swap_halves(vw_ref[...]).astype(BF16))]
    q_pairs = [qw_ref[:, LANES * j:LANES * (j + 1)] * ATTN_SCALE for j in range(WIN_HEADS // 2)]
    outs = _gqa_attention(q_pairs, SEQ, lambda c: [(kv[c][0], kv[c][1], None)],
                          [sink_ref[l, h] for h in range(WIN_HEADS)])
    for j, o in enumerate(outs):
        o_ref[:, F_WIDTH + LANES * j:F_WIDTH + LANES * (j + 1)] = o.astype(BF16)

    for j in range(NA_HEADS // 2):
        sl = slice(LANES * j, LANES * (j + 1))
        q_pairs = {j: qn_ref[:, sl] * ATTN_SCALE}
        o = _attend(_stack_heads(q_pairs, (2 * j, 2 * j + 1)),
                    [(kn_ref[:, sl].astype(BF16), vn_ref[:, sl].astype(BF16), None)])
        base = F_WIDTH + WIN_Q + LANES * j
        o_ref[:, base:base + LANES] = _merge_pair(o[:SEQ], o[SEQ:]).astype(BF16)


def _ctx_mixer(l, win_sink, proj, dft_ch, dft_seq):
    f, qw, kw, vw, qn, kn, vn = proj
    row = lambda w: pl.BlockSpec((SEQ, w), lambda b: (b, 0))
    const = lambda n: pl.BlockSpec((n, n), lambda b: (0, 0))
    return pl.pallas_call(
        functools.partial(_ctx_mixer_kernel, l),
        grid=(BATCH,),
        in_specs=[pl.BlockSpec(memory_space=pltpu.SMEM),
                  row(F_WIDTH), row(WIN_Q), row(WIN_KV), row(WIN_KV), row(NA_W), row(NA_W), row(NA_W)]
                 + [const(F_WIDTH)] * 2 + [const(SEQ)] * 2,
        out_specs=pl.BlockSpec((SEQ, D_MODEL), lambda b: (b, 0)),
        out_shape=jax.ShapeDtypeStruct((N_CTX, D_MODEL), BF16),
        compiler_params=_params("arbitrary"),
        name=f"ctx_mixer{l}",
    )(win_sink, f, qw, kw, vw, qn, kn, vn, *dft_ch, *dft_seq)


FT_ROWS = 256


def _lat_fourier_kernel(f_ref, bc_ref, bs_ref, cs_ref, ss_ref, o_ref, st_ref):
    b = pl.program_id(1)

    @pl.when(pl.program_id(0) == 0)
    def _():
        f_hi, f_lo = _split(f_ref[pl.ds(pl.multiple_of(b * DEC_SEQ, DEC_SEQ), DEC_SEQ), :])
        fc_hi, fc_lo = _split(_dot3(f_hi, f_lo, *_split(bc_ref[...])))
        fs_hi, fs_lo = _split(_dot3(f_hi, f_lo, *_split(bs_ref[...])))
        st_ref[b, 0] = fc_hi
        st_ref[b, 1] = fc_lo
        st_ref[b, 2] = fs_hi
        st_ref[b, 3] = fs_lo

    z = (_dot3(*_split(cs_ref[...]), st_ref[b, 0], st_ref[b, 1])
         - _dot3(*_split(ss_ref[...]), st_ref[b, 2], st_ref[b, 3]))
    o_ref[...] = (z * (DEC_SEQ * HEAD_DIM) ** -0.5).astype(BF16)


def _lat_fourier(f, dft_ch, dft_seq):
    nrb = DEC_SEQ // FT_ROWS
    const = pl.BlockSpec((F_WIDTH, F_WIDTH), lambda r, b: (0, 0))
    rows = pl.BlockSpec((FT_ROWS, DEC_SEQ), lambda r, b: (r, 0))
    return pl.pallas_call(
        _lat_fourier_kernel,
        grid=(nrb, DEC_BATCH),
        in_specs=[pl.BlockSpec((N_LAT, F_WIDTH), lambda r, b: (1, 0))] + [const] * 2 + [rows] * 2,
        out_specs=pl.BlockSpec((FT_ROWS, F_WIDTH), lambda r, b: (b * nrb + r, 0)),
        out_shape=jax.ShapeDtypeStruct((N_LAT, F_WIDTH), BF16),
        scratch_shapes=[pltpu.VMEM((DEC_BATCH, 4, DEC_SEQ, F_WIDTH), BF16)],
        compiler_params=_params("arbitrary", "arbitrary"),
        name="lat_fourier",
    )(f, *dft_ch, *dft_seq)


def _rope(x, cos, sin_signed):
    n = x.shape[-1]
    lane = lax.broadcasted_iota(jnp.int32, x.shape, x.ndim - 1)
    first = (lane % 32) < 16
    partner = jnp.where(first, pltpu.roll(x, n - 16, axis=x.ndim - 1), pltpu.roll(x, 16, axis=x.ndim - 1))
    return x * cos + partner * sin_signed


def _win_kernel(l, sink_ref, q_ref, k_ref, v_ref, ck_ref, cv_ref, cos_ref, sin_ref, cosq_ref, sinq_ref,
                o_ref, kp_ref, vp_ref, cp_ref):
    n = pl.program_id(1)
    nb = DEC_SEQ // WIN_BLOCK
    pad = WIN_BLOCK

    @pl.when(n == 0)
    def _():
        zeros = jnp.zeros((pad, LANES), BF16)
        kr = _rope(k_ref[...], cos_ref[...], sin_ref[...])
        v = v_ref[...]
        for idx, (kk, vv) in enumerate(((kr, v), (_swap_halves(kr), _swap_halves(v)))):
            kp_ref[idx, 0:pad] = zeros
            kp_ref[idx, pad + DEC_SEQ:] = zeros
            kp_ref[idx, pad:pad + DEC_SEQ] = kk.astype(BF16)
            vp_ref[idx, 0:pad] = zeros
            vp_ref[idx, pad + DEC_SEQ:] = zeros
            vp_ref[idx, pad:pad + DEC_SEQ] = vv.astype(BF16)
        ck = ck_ref[...]
        cv = cv_ref[...]
        cp_ref[0] = ck.astype(BF16)
        cp_ref[1] = _swap_halves(ck).astype(BF16)
        cp_ref[2] = cv.astype(BF16)
        cp_ref[3] = _swap_halves(cv).astype(BF16)

    lo = jnp.where(n == 0, WIN_BLOCK, 0)
    hi = jnp.where(n == nb - 1, 2 * WIN_BLOCK, 3 * WIN_BLOCK)

    i = lax.broadcasted_iota(jnp.int32, (WIN_BLOCK, 3 * WIN_BLOCK), 0)
    j = lax.broadcasted_iota(jnp.int32, (WIN_BLOCK, 3 * WIN_BLOCK), 1)
    mask = (j >= i + WIN_BLOCK - WINDOW) & (j <= i + WIN_BLOCK + WINDOW) & (j >= lo) & (j < hi)
    band_bias = jnp.where(mask, 0.0, NEG_INF)

    def band(s):
        heads = s.shape[0] // WIN_BLOCK
        return (s.reshape(heads, WIN_BLOCK, s.shape[1]) + band_bias[None]).reshape(s.shape)

    start = pl.multiple_of(n * WIN_BLOCK, WIN_BLOCK)
    win = pl.ds(start, 3 * WIN_BLOCK)
    q_pairs = [_rope(q_ref[:, LANES * jp:LANES * (jp + 1)], cosq_ref[...], sinq_ref[...]) * ATTN_SCALE
               for jp in range(WIN_HEADS // 2)]
    outs = _gqa_attention(
        q_pairs, WIN_BLOCK,
        lambda c: [(kp_ref[c, win, :], vp_ref[c, win, :], band), (cp_ref[c], cp_ref[2 + c], None)],
        [sink_ref[l, h] for h in range(WIN_HEADS)])
    for jp, o in enumerate(outs):
        o_ref[:, LANES * jp:LANES * (jp + 1)] = o.astype(BF16)


def _lat_window(l, win_sink, qw, kw, vw, cache_k, cache_v, cos_t, sin_t):
    nb = DEC_SEQ // WIN_BLOCK
    lat_blk0 = N_CTX // DEC_SEQ
    kv_spec = pl.BlockSpec((DEC_SEQ, WIN_KV), lambda b, n: (lat_blk0 + b, 0))
    cache_spec = pl.BlockSpec((None, None, PAST_LEN, WIN_KV), lambda b, n: (b, l, 0, 0))
    tab_all = pl.BlockSpec((DEC_SEQ, LANES), lambda b, n: (0, 0))
    tab_blk = pl.BlockSpec((WIN_BLOCK, LANES), lambda b, n: (n, 0))
    return pl.pallas_call(
        functools.partial(_win_kernel, l),
        grid=(DEC_BATCH, nb),
        in_specs=[pl.BlockSpec(memory_space=pltpu.SMEM),
                  pl.BlockSpec((WIN_BLOCK, WIN_Q), lambda b, n: (N_CTX // WIN_BLOCK + b * nb + n, 0)),
                  kv_spec, kv_spec, cache_spec, cache_spec, tab_all, tab_all, tab_blk, tab_blk],
        out_specs=pl.BlockSpec((WIN_BLOCK, WIN_Q), lambda b, n: (b * nb + n, 0)),
        out_shape=jax.ShapeDtypeStruct((N_LAT, WIN_Q), BF16),
        scratch_shapes=[pltpu.VMEM((2, DEC_SEQ + 2 * WIN_BLOCK, LANES), BF16),
                        pltpu.VMEM((2, DEC_SEQ + 2 * WIN_BLOCK, LANES), BF16),
                        pltpu.VMEM((4, PAST_LEN, LANES), BF16)],
        compiler_params=_params("arbitrary", "arbitrary"),
        name=f"lat_window{l}",
    )(win_sink, qw, kw, vw, cache_k, cache_v, cos_t, sin_t, cos_t, sin_t)


NA_G = 4
NA_Q = NA_G * GRID_W
NA_WIN_ROWS = NA_ROWS + NA_G
NA_BLOCKS = GRID_ROWS // NA_G
NA_VARIANTS = 3


def _na_block_start(g):
    return jnp.clip(NA_G * g - NA_ROWS // 2, 0, GRID_ROWS - NA_WIN_ROWS)


def _na_kernel(q_ref, k_ref, v_ref, ck_ref, cv_ref, bias_ref, o_ref):
    g = pl.program_id(1)
    start = pl.multiple_of(_na_block_start(g) * GRID_W, GRID_W)
    win = pl.ds(start, NA_WIN_ROWS * GRID_W)
    for jp in range(NA_HEADS // 2):
        sl = slice(LANES * jp, LANES * (jp + 1))
        q_stack = _stack_heads({jp: q_ref[:, sl] * ATTN_SCALE}, (2 * jp, 2 * jp + 1))
        bias = bias_ref[jp]
        o = _attend(q_stack, [(k_ref[win, sl].astype(BF16), v_ref[win, sl].astype(BF16), lambda s: s + bias),
                              (ck_ref[:, sl].astype(BF16), cv_ref[:, sl].astype(BF16), None)])
        o_ref[:, sl] = _merge_pair(o[:NA_Q], o[NA_Q:]).astype(BF16)


def _lat_neighbourhood(l, qn, kn, vn, cache_k, cache_v, bias_tab):
    lat_blk0 = N_CTX // DEC_SEQ
    kv_spec = pl.BlockSpec((DEC_SEQ, NA_W), lambda b, g: (lat_blk0 + b, 0))
    cache_spec = pl.BlockSpec((None, None, PAST_LEN, NA_W), lambda b, g: (b, l, 0, 0))
    variant = lambda g: jnp.where(g == 0, 0, jnp.where(g == NA_BLOCKS - 1, 2, 1))
    return pl.pallas_call(
        _na_kernel,
        grid=(DEC_BATCH, NA_BLOCKS),
        in_specs=[pl.BlockSpec((NA_Q, NA_W), lambda b, g: (N_CTX // NA_Q + b * NA_BLOCKS + g, 0)),
                  kv_spec, kv_spec, cache_spec, cache_spec,
                  pl.BlockSpec((None, None, NA_HEADS // 2, 2 * NA_Q, NA_WIN_ROWS * GRID_W),
                               lambda b, g: (l, variant(g), 0, 0, 0))],
        out_specs=pl.BlockSpec((NA_Q, NA_W), lambda b, g: (b * NA_BLOCKS + g, 0)),
        out_shape=jax.ShapeDtypeStruct((N_LAT, NA_W), BF16),
        compiler_params=_params("arbitrary", "arbitrary"),
        name=f"lat_neighbourhood{l}",
    )(qn, kn, vn, cache_k, cache_v, bias_tab)


def _outproj_kernel(xc_ref, xl_ref, mc_ref, mf_ref, mw_ref, mn_ref, mod_ref, g_ref, w_ref, r_hi, r_lo,
                    x_ref, h_ref, lg_ref, wb_ref):
    is_ctx = pl.program_id(0) < N_CTX // TM

    @pl.when(pl.program_id(0) == 0)
    def _():
        wb_ref[...] = w_ref[...].astype(BF16)

    m_lat = jnp.concatenate([mf_ref[...], mw_ref[...], mn_ref[...]], axis=1)
    mixed = _dot(jnp.where(is_ctx, mc_ref[...], m_lat), wb_ref[...])
    d = D_MODEL
    x = jnp.where(is_ctx, xc_ref[...], xl_ref[...]) + mod_ref[:, 2 * d:3 * d] * mixed
    x_ref[...] = x
    h = _rms_mod(x, g_ref[...], mod_ref[:, 3 * d:4 * d], mod_ref[:, 4 * d:5 * d])
    h_ref[...] = h.astype(BF16)
    h_hi, h_lo = _split(h)
    lg_ref[...] = _dot3(h_hi, h_lo, r_hi[...], r_lo[...])


def _outproj(l, x_ctx, x_lat, mixed_ctx, mixed_f, mixed_w, mixed_n, mod4, g_ffn, w_out, r_hi, r_lo):
    n_ctx_blocks = N_CTX // TM
    xc_spec, xl_spec = _two_stream_specs(D_MODEL)
    lat = lambda w: pl.BlockSpec((TM, w), lambda i: (jnp.maximum(i - n_ctx_blocks, 0), 0))
    whole = lambda shape: pl.BlockSpec(shape, lambda i: (0,) * len(shape))
    row = lambda w: pl.BlockSpec((TM, w), lambda i: (i, 0))
    return pl.pallas_call(
        _outproj_kernel,
        grid=(N_TOK // TM,),
        in_specs=[xc_spec, xl_spec,
                  pl.BlockSpec((TM, D_MODEL), lambda i: (jnp.minimum(i, n_ctx_blocks - 1), 0)),
                  lat(F_WIDTH), lat(WIN_Q), lat(NA_W),
                  pl.BlockSpec((None, None, 1, 6 * D_MODEL), lambda i: (l, _cond_of_block(i), 0, 0)),
                  pl.BlockSpec((None, 1, D_MODEL), lambda i: (l, 0, 0)),
                  pl.BlockSpec((None, D_MODEL, D_MODEL), lambda i: (l, 0, 0)),
                  whole((D_MODEL, LANES)), whole((D_MODEL, LANES))],
        out_specs=[row(D_MODEL), row(D_MODEL), row(LANES)],
        out_shape=[jax.ShapeDtypeStruct((N_TOK, D_MODEL), F32),
                   jax.ShapeDtypeStruct((N_TOK, D_MODEL), BF16),
                   jax.ShapeDtypeStruct((N_TOK, LANES), F32)],
        scratch_shapes=[pltpu.VMEM((D_MODEL, D_MODEL), BF16)],
        compiler_params=_params("arbitrary"),
        name=f"outproj{l}",
    )(x_ctx, x_lat, mixed_ctx, mixed_f, mixed_w, mixed_n, mod4, g_ffn.reshape(DEPTH, 1, D_MODEL), w_out,
      r_hi, r_lo)


PREFIX_CHUNK = 256
MANTISSA_STEPS = 44


def _prefix_exclusive(m):
    rows, n = m.shape
    t0 = lax.broadcasted_iota(jnp.int32, (PREFIX_CHUNK, PREFIX_CHUNK), 0)
    t1 = lax.broadcasted_iota(jnp.int32, (PREFIX_CHUNK, PREFIX_CHUNK), 1)
    upper = jnp.where(t0 < t1, 1.0, 0.0).astype(BF16)
    carry = jnp.zeros((rows, 1), F32)
    outs = []
    for c in range(n // PREFIX_CHUNK):
        blk = m[:, c * PREFIX_CHUNK:(c + 1) * PREFIX_CHUNK]
        outs.append(_dot(blk.astype(BF16), upper) + carry)
        carry = carry + blk.sum(axis=-1, keepdims=True)
    return outs[0] if len(outs) == 1 else jnp.concatenate(outs, axis=-1)


def _route_kernel(cap, lg_ref, slot_ref, aff_ref):
    x = lg_ref[...]
    e = jnp.exp(x - x.max(axis=1, keepdims=True))
    aff3 = e / e.sum(axis=1, keepdims=True)
    aff = aff3.reshape(aff3.shape[0] * aff3.shape[1], aff3.shape[2])
    capf = float(cap)

    def count_ge(t):
        return jnp.where(aff >= t, 1.0, 0.0).sum(axis=-1, keepdims=True)

    above = jnp.full((aff.shape[0], 1), 2.0, F32)
    for s in (64, 32, 16, 8, 4, 2, 1):
        cand = above * (2.0 ** -s)
        above = jnp.where(count_ge(cand) >= capf, above, cand)
    base = above * 0.5
    base = jnp.where(count_ge(base) >= capf, base, 0.0)

    def refine(_, carry):
        thr, inc = carry
        cand = thr + inc
        return jnp.where(count_ge(cand) >= capf, cand, thr), inc * 0.5

    thr, _ = lax.fori_loop(0, MANTISSA_STEPS, refine, (base, base * 0.5))
    gt = jnp.where(aff > thr, 1.0, 0.0)
    eq = jnp.where(aff == thr, 1.0, 0.0)
    need = capf - gt.sum(axis=-1, keepdims=True)
    sel = gt + eq * jnp.where(_prefix_exclusive(eq) < need, 1.0, 0.0)
    slot = _prefix_exclusive(sel)
    slot_ref[...] = jnp.where(sel > 0.0, slot, -1.0).astype(jnp.int32)
    aff_ref[...] = aff


def _route(lg3, cap, name):
    groups, n_e, n = lg3.shape
    rows = groups * n_e
    return pl.pallas_call(
        functools.partial(_route_kernel, cap),
        out_shape=[jax.ShapeDtypeStruct((rows, n), jnp.int32), jax.ShapeDtypeStruct((rows, n), F32)],
        compiler_params=pltpu.CompilerParams(vmem_limit_bytes=VMEM_LIMIT),
        name=name,
    )(lg3)


def _gather_ctx_kernel(h_ref, slot_ref, aff_ref, x_ref, g_ref, p_ref):
    s_iota = lax.broadcasted_iota(jnp.int32, (CAP_CTX, SEQ), 0)
    for e in range(N_EXPERTS):
        hit = s_iota == slot_ref[e:e + 1, :]
        p_ref[e * CAP_CTX:(e + 1) * CAP_CTX, :] = jnp.where(hit, 1.0, 0.0).astype(BF16)
        g_ref[e] = jnp.where(hit, aff_ref[e:e + 1, :], 0.0).sum(axis=-1, keepdims=True)
    x = _dot(p_ref[...], h_ref[...]).astype(BF16)
    x_ref[...] = x.reshape(N_EXPERTS, CAP_CTX, D_MODEL)


def _gather_ctx(h, slot, aff):
    return pl.pallas_call(
        _gather_ctx_kernel,
        grid=(BATCH,),
        in_specs=[pl.BlockSpec((SEQ, D_MODEL), lambda b: (b, 0)),
                  pl.BlockSpec((N_EXPERTS, SEQ), lambda b: (b, 0)),
                  pl.BlockSpec((N_EXPERTS, SEQ), lambda b: (b, 0))],
        out_specs=[pl.BlockSpec((N_EXPERTS, CAP_CTX, D_MODEL), lambda b: (0, b, 0)),
                   pl.BlockSpec((N_EXPERTS, CAP_CTX, 1), lambda b: (0, b, 0))],
        out_shape=[jax.ShapeDtypeStruct((N_EXPERTS, ROWS_CTX, D_MODEL), BF16),
                   jax.ShapeDtypeStruct((N_EXPERTS, ROWS_CTX, 1), F32)],
        scratch_shapes=[pltpu.VMEM((N_EXPERTS * CAP_CTX, SEQ), BF16)],
        compiler_params=_params("arbitrary"),
        name="gather_ctx",
    )(h, slot, aff)


def _gather_lat_kernel(h_ref, slot_ref, aff_ref, x_ref, g_ref):
    e = pl.program_id(1)
    s_iota = lax.broadcasted_iota(jnp.int32, (CAP_LAT, DEC_SEQ), 0)
    hit = s_iota == slot_ref[pl.ds(e, 1), :]
    x_ref[...] = _dot(jnp.where(hit, 1.0, 0.0).astype(BF16), h_ref[...]).astype(BF16)
    g_ref[...] = jnp.where(hit, aff_ref[pl.ds(e, 1), :], 0.0).sum(axis=-1, keepdims=True)


def _gather_lat(h, slot, aff):
    lat_blk0 = N_CTX // DEC_SEQ
    return pl.pallas_call(
        _gather_lat_kernel,
        grid=(DEC_BATCH, N_EXPERTS),
        in_specs=[pl.BlockSpec((DEC_SEQ, D_MODEL), lambda b, e: (lat_blk0 + b, 0)),
                  pl.BlockSpec((N_EXPERTS, DEC_SEQ), lambda b, e: (b, 0)),
                  pl.BlockSpec((N_EXPERTS, DEC_SEQ), lambda b, e: (b, 0))],
        out_specs=[pl.BlockSpec((None, CAP_LAT, D_MODEL), lambda b, e: (e, b, 0)),
                   pl.BlockSpec((None, CAP_LAT, 1), lambda b, e: (e, b, 0))],
        out_shape=[jax.ShapeDtypeStruct((N_EXPERTS, ROWS_LAT, D_MODEL), BF16),
                   jax.ShapeDtypeStruct((N_EXPERTS, ROWS_LAT, 1), F32)],
        compiler_params=_params("arbitrary", "arbitrary"),
        name="gather_lat",
    )(h, slot, aff)


N_UP_STEPS = D_FF // TF
N_DOWN_STEPS = D_MODEL // TD


def _silu_tanh(x):
    return x * (0.5 + 0.5 * jnp.tanh(0.5 * x))


def _ffn_kernel(xc_ref, xl_ref, gc_ref, gl_ref, wg_ref, wu_ref, wd_ref, y_ref, x_sc, h_sc):
    j = pl.program_id(1)

    @pl.when(j == 0)
    def _():
        x_sc[0:ROWS_CTX, :] = xc_ref[...]
        x_sc[ROWS_CTX:, :] = xl_ref[...]

    @pl.when(j < N_UP_STEPS)
    def _():
        x = x_sc[...]
        a = _dot(x, wg_ref[...].astype(BF16))
        u = _dot(x, wu_ref[...].astype(BF16))
        h_sc[j] = (_silu_tanh(a) * u).astype(BF16)

    @pl.when(j >= N_UP_STEPS)
    def _():
        acc = None
        for k in range(N_UP_STEPS):
            t = _dot(h_sc[k], wd_ref[k * TF:(k + 1) * TF, :].astype(BF16))
            acc = t if acc is None else acc + t
        y_ref[0:ROWS_CTX, :] = (acc[0:ROWS_CTX] * gc_ref[...]).astype(BF16)
        y_ref[ROWS_CTX:, :] = (acc[ROWS_CTX:] * gl_ref[...]).astype(BF16)


def _ffn(l, x_c, x_l, g_c, g_l, w_gate, w_up, w_down):
    rows = ROWS_CTX + ROWS_LAT
    xin = lambda r: pl.BlockSpec((None, r, D_MODEL), lambda e, j: (e, 0, 0))
    gin = lambda r: pl.BlockSpec((None, r, 1), lambda e, j: (e, 0, 0))
    up_chunk = lambda e, j: (l, e, 0, jnp.minimum(j, N_UP_STEPS - 1))
    down_chunk = lambda j: jnp.maximum(j - N_UP_STEPS, 0)
    return pl.pallas_call(
        _ffn_kernel,
        grid=(N_EXPERTS, N_UP_STEPS + N_DOWN_STEPS),
        in_specs=[xin(ROWS_CTX), xin(ROWS_LAT), gin(ROWS_CTX), gin(ROWS_LAT),
                  pl.BlockSpec((None, None, D_MODEL, TF), up_chunk),
                  pl.BlockSpec((None, None, D_MODEL, TF), up_chunk),
                  pl.BlockSpec((None, None, D_FF, TD), lambda e, j: (l, e, 0, down_chunk(j)))],
        out_specs=pl.BlockSpec((None, rows, TD), lambda e, j: (e, 0, down_chunk(j))),
        out_shape=jax.ShapeDtypeStruct((N_EXPERTS, rows, D_MODEL), BF16),
        scratch_shapes=[pltpu.VMEM((rows, D_MODEL), BF16), pltpu.VMEM((N_UP_STEPS, rows, TF), BF16)],
        compiler_params=_params("arbitrary", "arbitrary"),
        name=f"experts{l}",
    )(x_c, x_l, g_c, g_l, w_gate, w_up, w_down)


def _finish(x, res, mod_ref, gf_ref, final):
    y = x + mod_ref[:, 5 * D_MODEL:] * res
    if final:
        y = y * lax.rsqrt(jnp.mean(y * y, axis=-1, keepdims=True) + RMS_EPS) * gf_ref[...]
    return y


def _combine_ctx_kernel(final, x_ref, y_ref, slot_ref, rep_ref, mod_ref, gf_ref, o_ref):
    n_col = N_EXPERTS * CAP_CTX
    spread = _dot(slot_ref[...].astype(F32).astype(BF16), rep_ref[...])
    col = lax.broadcasted_iota(jnp.int32, (SEQ, n_col), 1) % CAP_CTX
    p = jnp.where(spread == col.astype(F32), 1.0, 0.0).astype(BF16)
    res = _dot(p, y_ref[...].reshape(n_col, D_MODEL))
    o_ref[...] = _finish(x_ref[...], res, mod_ref, gf_ref, final)


def _combine_ctx(l, final, x_new, y, slot_t, mod4, g_final):
    n_col = N_EXPERTS * CAP_CTX
    rep = (np.arange(n_col)[None, :] // CAP_CTX == np.arange(N_EXPERTS)[:, None]).astype(np.float32)
    return pl.pallas_call(
        functools.partial(_combine_ctx_kernel, final),
        grid=(BATCH,),
        in_specs=[pl.BlockSpec((SEQ, D_MODEL), lambda b: (b, 0)),
                  pl.BlockSpec((N_EXPERTS, CAP_CTX, D_MODEL), lambda b: (0, b, 0)),
                  pl.BlockSpec((SEQ, N_EXPERTS), lambda b: (b, 0)),
                  pl.BlockSpec((N_EXPERTS, n_col), lambda b: (0, 0)),
                  pl.BlockSpec((None, None, 1, 6 * D_MODEL), lambda b: (l, 0, 0, 0)),
                  pl.BlockSpec((1, D_MODEL), lambda b: (0, 0))],
        out_specs=pl.BlockSpec((SEQ, D_MODEL), lambda b: (b, 0)),
        out_shape=jax.ShapeDtypeStruct((N_CTX, D_MODEL), F32),
        compiler_params=_params("arbitrary"),
        name=f"combine_ctx{l}",
    )(x_new, y, slot_t, jnp.asarray(rep, BF16), mod4, g_final.reshape(1, D_MODEL))


TMC = 512


def _combine_lat_kernel(final, x_ref, y_ref, slot_ref, mod_ref, gf_ref, o_ref):
    s_iota = lax.broadcasted_iota(jnp.int32, (TMC, CAP_LAT), 1)
    slot = slot_ref[...]
    res = None
    for e in range(N_EXPERTS):
        p = jnp.where(slot[:, e:e + 1] == s_iota, 1.0, 0.0).astype(BF16)
        t = _dot(p, y_ref[e])
        res = t if res is None else res + t
    o_ref[...] = _finish(x_ref[...], res, mod_ref, gf_ref, final)


def _combine_lat(l, final, x_new, y, slot_t, mod4, g_final):
    nt = DEC_SEQ // TMC
    return pl.pallas_call(
        functools.partial(_combine_lat_kernel, final),
        grid=(DEC_BATCH, nt),
        in_specs=[pl.BlockSpec((TMC, D_MODEL), lambda b, t: (N_CTX // TMC + b * nt + t, 0)),
                  pl.BlockSpec((N_EXPERTS, CAP_LAT, D_MODEL), lambda b, t: (0, ROWS_CTX // CAP_LAT + b, 0)),
                  pl.BlockSpec((TMC, N_EXPERTS), lambda b, t: (b * nt + t, 0)),
                  pl.BlockSpec((None, None, 1, 6 * D_MODEL), lambda b, t: (l, 1 + b, 0, 0)),
                  pl.BlockSpec((1, D_MODEL), lambda b, t: (0, 0))],
        out_specs=pl.BlockSpec((TMC, D_MODEL), lambda b, t: (b * nt + t, 0)),
        out_shape=jax.ShapeDtypeStruct((N_LAT, D_MODEL), F32),
        compiler_params=_params("arbitrary", "arbitrary"),
        name=f"combine_lat{l}",
    )(x_new, y, slot_t, mod4, g_final.reshape(1, D_MODEL))


def _split_table(t):
    hi = t.astype(BF16)
    return hi, (t - hi.astype(F32)).astype(BF16)


def _dft_tables(n):
    p = np.arange(n, dtype=np.int64)
    ang = ((p[:, None] * p[None, :]) % n).astype(np.float64) * (2.0 * np.pi / n)
    return np.cos(ang).astype(np.float32), np.sin(ang).astype(np.float32)


def _channel_dft_tables():
    c = np.arange(F_WIDTH, dtype=np.int64)
    same = (c[:, None] // HEAD_DIM) == (c[None, :] // HEAD_DIM)
    ang = (((c[:, None] % HEAD_DIM) * (c[None, :] % HEAD_DIM)) % HEAD_DIM).astype(np.float64) * (2.0 * np.pi / HEAD_DIM)
    return (np.where(same, np.cos(ang), 0.0).astype(np.float32),
            np.where(same, np.sin(ang), 0.0).astype(np.float32))


def _rope_tables():
    half = HEAD_DIM // 2
    nf = half // 2
    pos = jnp.arange(DEC_SEQ)
    inv = 1.0 / (ROPE_BASE ** (jnp.arange(nf, dtype=F32) / nf))
    ang_r = (pos // GRID_W).astype(F32)[:, None] * inv
    ang_c = (pos % GRID_W).astype(F32)[:, None] * inv

    def head(fn, sign):
        return jnp.concatenate([sign * fn(ang_r), fn(ang_r), sign * fn(ang_c), fn(ang_c)], axis=-1)

    cos = head(jnp.cos, 1.0)
    sin = head(jnp.sin, -1.0)
    return jnp.concatenate([cos, cos], axis=-1), jnp.concatenate([sin, sin], axis=-1)


def _na_bias_tables(rpb):
    cq = np.arange(GRID_W)
    rel_c = np.clip(cq[None, :] - cq[:, None] + NA_COLS - 1, 0, 2 * NA_COLS - 2)
    pick = (rel_c[:, :, None] == np.arange(2 * NA_COLS - 1)).astype(np.float32)
    cs = np.clip(cq - NA_COLS // 2, 0, GRID_W - NA_COLS)
    col_ok = (cq[None, :] >= cs[:, None]) & (cq[None, :] < cs[:, None] + NA_COLS)
    sel = np.zeros((NA_VARIANTS, NA_G, NA_WIN_ROWS, 2 * NA_ROWS - 1), np.float32)
    for v, g in enumerate((0, 1, NA_BLOCKS - 1)):
        start = int(np.clip(NA_G * g - NA_ROWS // 2, 0, GRID_ROWS - NA_WIN_ROWS))
        for a in range(NA_G):
            r = NA_G * g + a
            rs = int(np.clip(r - NA_ROWS // 2, 0, GRID_ROWS - NA_ROWS))
            for w in range(NA_WIN_ROWS):
                if rs <= start + w < rs + NA_ROWS:
                    sel[v, a, w, start + w - r + NA_ROWS - 1] = 1.0
    valid = (sel.sum(-1) > 0)[:, None, :, None, :, None] & col_ok[None, None, None, :, None, :]
    tab = jnp.einsum('lhrj,qkj,vawr->lvhaqwk', rpb, pick, sel, precision=lax.Precision.HIGHEST)
    tab = jnp.where(valid[None], tab, NEG_INF)
    return tab.reshape(DEPTH, NA_VARIANTS, NA_HEADS // 2, 2 * NA_Q, NA_WIN_ROWS * GRID_W)


def kernel(x_prompt, x_sample, cache_win_k, cache_win_v, cache_nat_k, cache_nat_v, c, c_ctx, w_mod, b_mod, g_mix, g_ffn, w_in, w_out, win_sink, nat_rpb, w_router, w_gate, w_up, w_down, g_final):
    x_ctx = x_prompt.reshape(N_CTX, D_MODEL)
    x_lat = x_sample.reshape(N_LAT, D_MODEL)
    cond = jnp.concatenate([c_ctx[None, :], c, jnp.zeros((N_COND - 1 - DEC_BATCH, D_MODEL), F32)], axis=0)
    mod4 = _adaln(cond, w_mod, b_mod).reshape(DEPTH, N_COND, 1, 6 * D_MODEL)

    dft_ch = _channel_dft_tables()
    dft_ctx = _dft_tables(SEQ)
    dft_lat = _dft_tables(DEC_SEQ)
    cos_t, sin_t = _rope_tables()
    cwk = cache_win_k.reshape(DEC_BATCH, DEPTH, PAST_LEN, WIN_KV)
    cwv = cache_win_v.reshape(DEC_BATCH, DEPTH, PAST_LEN, WIN_KV)
    cnk = cache_nat_k.reshape(DEC_BATCH, DEPTH, PAST_LEN, NA_W)
    cnv = cache_nat_v.reshape(DEC_BATCH, DEPTH, PAST_LEN, NA_W)
    r_pad = jnp.pad(w_router, ((0, 0), (0, 0), (0, LANES - N_EXPERTS)))
    na_bias = _na_bias_tables(nat_rpb)

    new_kv = [[], [], [], []]
    for l in range(DEPTH):
        final = l == DEPTH - 1
        proj = _project(l, x_ctx, x_lat, mod4, g_mix, w_in)
        f, qw, kw, vw, qn, kn, vn = proj
        for dst, src, heads in zip(new_kv, (kw, vw, kn, vn), (WIN_KV_HEADS, WIN_KV_HEADS, NA_HEADS, NA_HEADS)):
            dst.append(src[:N_CTX].reshape(BATCH, SEQ, heads, HEAD_DIM))

        mixed_ctx = _ctx_mixer(l, win_sink, proj, dft_ch, dft_ctx)
        mixed_f = _lat_fourier(f, dft_ch, dft_lat)
        mixed_w = _lat_window(l, win_sink, qw, kw, vw, cwk, cwv, cos_t, sin_t)
        mixed_n = _lat_neighbourhood(l, qn, kn, vn, cnk, cnv, na_bias)

        r_hi, r_lo = _split_table(r_pad[l])
        x_new, h, logits = _outproj(l, x_ctx, x_lat, mixed_ctx, mixed_f, mixed_w, mixed_n, mod4, g_ffn,
                                    w_out, r_hi, r_lo)

        lg = logits[:, :N_EXPERTS]
        lg_ctx = lg[:N_CTX].reshape(BATCH, SEQ, N_EXPERTS).transpose(0, 2, 1)
        lg_lat = lg[N_CTX:].reshape(DEC_BATCH, DEC_SEQ, N_EXPERTS).transpose(0, 2, 1)
        slot_c, aff_c = _route(lg_ctx, CAP_CTX, "route_ctx")
        slot_l, aff_l = _route(lg_lat, CAP_LAT, "route_lat")
        xg_c, gate_c = _gather_ctx(h, slot_c, aff_c)
        xg_l, gate_l = _gather_lat(h, slot_l, aff_l)
        y = _ffn(l, xg_c, xg_l, gate_c, gate_l, w_gate, w_up, w_down)
        slot_ct = slot_c.reshape(BATCH, N_EXPERTS, SEQ).transpose(0, 2, 1).reshape(N_CTX, N_EXPERTS)
        slot_lt = slot_l.reshape(DEC_BATCH, N_EXPERTS, DEC_SEQ).transpose(0, 2, 1).reshape(N_LAT, N_EXPERTS)
        x_ctx = _combine_ctx(l, final, x_new, y, slot_ct, mod4, g_final)
        x_lat = _combine_lat(l, final, x_new, y, slot_lt, mod4, g_final)

    y_prompt = x_ctx.reshape(BATCH, SEQ, D_MODEL)
    y_sample = x_lat.reshape(DEC_BATCH, DEC_SEQ, D_MODEL)
    return (y_prompt, y_sample, *(jnp.stack(t, axis=1) for t in new_kv))
```

```python
import functools

import numpy as np
import jax
import jax.numpy as jnp
from jax import lax
from jax.experimental import pallas as pl
from jax.experimental.pallas import tpu as pltpu

D_MODEL = 1024
BATCH = 16
SEQ = 256
DEPTH = 2
DEC_BATCH = 2
DEC_SEQ = 2048
PAST_LEN = 256
GRID_W = 64
HEAD_DIM = 64
F_WIDTH = 256
WIN_HEADS = 6
WIN_KV_HEADS = 2
WINDOW = 128
WIN_BLOCK = 128
NA_HEADS = 6
NA_ROWS = 8
NA_COLS = 16
N_EXPERTS = 16
EC_CAPACITY = 2
D_FF = 2816
ROPE_BASE = 10000.0
RMS_EPS = 1e-6
NEG_INF = -1e30
ATTN_SCALE = HEAD_DIM ** -0.5
WIN_Q = WIN_HEADS * HEAD_DIM
WIN_KV = WIN_KV_HEADS * HEAD_DIM
NA_W = NA_HEADS * HEAD_DIM
N_IN = F_WIDTH + WIN_Q + 2 * WIN_KV + 3 * NA_W
SPLITS = (0, F_WIDTH, F_WIDTH + WIN_Q, F_WIDTH + WIN_Q + WIN_KV, F_WIDTH + WIN_Q + 2 * WIN_KV,
          F_WIDTH + WIN_Q + 2 * WIN_KV + NA_W, F_WIDTH + WIN_Q + 2 * WIN_KV + 2 * NA_W, N_IN)

N_CTX = BATCH * SEQ
N_LAT = DEC_BATCH * DEC_SEQ
N_TOK = N_CTX + N_LAT
GRID_ROWS = DEC_SEQ // GRID_W
CAP_CTX = EC_CAPACITY * SEQ // N_EXPERTS
CAP_LAT = EC_CAPACITY * DEC_SEQ // N_EXPERTS
ROWS_CTX = BATCH * CAP_CTX
ROWS_LAT = DEC_BATCH * CAP_LAT
N_COND = 8

LANES = 128
MXU_COLS = 256
TM = 512
TN_MOD = 1536
TF = 256
TD = 256
VMEM_LIMIT = 56 * 1024 * 1024

F32 = jnp.float32
BF16 = jnp.bfloat16


def _params(*sem):
    return pltpu.CompilerParams(dimension_semantics=sem, vmem_limit_bytes=VMEM_LIMIT)


def _dot(a, b):
    return jnp.dot(a, b, preferred_element_type=F32)


def _dot_nt(a, b):
    return lax.dot_general(a, b, (((1,), (1,)), ((), ())), preferred_element_type=F32)


def _split(x):
    hi = x.astype(BF16)
    lo = (x - hi.astype(F32)).astype(BF16)
    return hi, lo


def _dot3(a_hi, a_lo, b_hi, b_lo):
    return _dot(a_hi, b_hi) + (_dot(a_lo, b_hi) + _dot(a_hi, b_lo))


def _silu(x):
    return x / (1.0 + jnp.exp(-x))


def _rms_mod(x, g, shift, scale):
    y = x * lax.rsqrt(jnp.mean(x * x, axis=-1, keepdims=True) + RMS_EPS)
    return (y * g) * (1.0 + scale) + shift


def _softmax_parts(parts, sink=None):
    m = parts[0].max(axis=-1, keepdims=True)
    for s in parts[1:]:
        m = jnp.maximum(m, s.max(axis=-1, keepdims=True))
    if sink is not None:
        m = jnp.maximum(m, sink)
    es = [jnp.exp(s - m) for s in parts]
    den = es[0].sum(axis=-1, keepdims=True)
    for e in es[1:]:
        den = den + e.sum(axis=-1, keepdims=True)
    if sink is not None:
        den = den + jnp.exp(sink - m)
    inv = 1.0 / den
    return [e * inv for e in es]


def _cond_of_block(i):
    n_ctx_blocks = N_CTX // TM
    return jnp.where(i < n_ctx_blocks, 0, 1 + (i - n_ctx_blocks) // (DEC_SEQ // TM))


def _two_stream_specs(width):
    n_ctx_blocks = N_CTX // TM
    ctx = pl.BlockSpec((TM, width), lambda i: (jnp.minimum(i, n_ctx_blocks - 1), 0))
    lat = pl.BlockSpec((TM, width), lambda i: (jnp.maximum(i - n_ctx_blocks, 0), 0))
    return ctx, lat


def _pick_stream(ctx_ref, lat_ref):
    return jnp.where(pl.program_id(0) < N_CTX // TM, ctx_ref[...], lat_ref[...])


def _adaln_kernel(c_ref, w_ref, b_ref, o_ref):
    s_hi, s_lo = _split(_silu(c_ref[...]))
    w_hi, w_lo = _split(w_ref[...])
    o_ref[...] = _dot3(s_hi, s_lo, w_hi, w_lo) + b_ref[...]


def _adaln(cond, w_mod, b_mod):
    return pl.pallas_call(
        _adaln_kernel,
        grid=(DEPTH, 6 * D_MODEL // TN_MOD),
        in_specs=[
            pl.BlockSpec((N_COND, D_MODEL), lambda l, j: (0, 0)),
            pl.BlockSpec((None, D_MODEL, TN_MOD), lambda l, j: (l, 0, j)),
            pl.BlockSpec((None, 1, TN_MOD), lambda l, j: (l, 0, j)),
        ],
        out_specs=pl.BlockSpec((None, N_COND, TN_MOD), lambda l, j: (l, 0, j)),
        out_shape=jax.ShapeDtypeStruct((DEPTH, N_COND, 6 * D_MODEL), F32),
        compiler_params=_params("arbitrary", "arbitrary"),
        name="adaln",
    )(cond, w_mod, b_mod.reshape(DEPTH, 1, 6 * D_MODEL))


def _proj_kernel(xc_ref, xl_ref, mod_ref, g_ref, w_ref, *rest):
    out_refs, wb_ref = rest[:-1], rest[-1]

    @pl.when(pl.program_id(0) == 0)
    def _():
        wb_ref[...] = w_ref[...].astype(BF16)

    x = _pick_stream(xc_ref, xl_ref)
    h = _rms_mod(x, g_ref[...], mod_ref[:, 0:D_MODEL], mod_ref[:, D_MODEL:2 * D_MODEL]).astype(BF16)
    for t0 in range(0, N_IN, MXU_COLS):
        acc = _dot(h, wb_ref[:, t0:t0 + MXU_COLS])
        for o_ref, a, b in zip(out_refs, SPLITS[:-1], SPLITS[1:]):
            lo, hi = max(a, t0), min(b, t0 + MXU_COLS)
            if lo < hi:
                o_ref[:, lo - a:hi - a] = acc[:, lo - t0:hi - t0]


def _project(l, x_ctx, x_lat, mod4, g_mix, w_in):
    widths = [b - a for a, b in zip(SPLITS[:-1], SPLITS[1:])]
    xc_spec, xl_spec = _two_stream_specs(D_MODEL)
    return pl.pallas_call(
        _proj_kernel,
        grid=(N_TOK // TM,),
        in_specs=[
            xc_spec, xl_spec,
            pl.BlockSpec((None, None, 1, 6 * D_MODEL), lambda i: (l, _cond_of_block(i), 0, 0)),
            pl.BlockSpec((None, 1, D_MODEL), lambda i: (l, 0, 0)),
            pl.BlockSpec((None, D_MODEL, N_IN), lambda i: (l, 0, 0)),
        ],
        out_specs=[pl.BlockSpec((TM, w), lambda i: (i, 0)) for w in widths],
        out_shape=[jax.ShapeDtypeStruct((N_TOK, w), F32) for w in widths],
        scratch_shapes=[pltpu.VMEM((D_MODEL, N_IN), BF16)],
        compiler_params=_params("arbitrary"),
        name=f"project{l}",
    )(x_ctx, x_lat, mod4, g_mix.reshape(DEPTH, 1, D_MODEL), w_in)


def _lane_is_low(shape):
    return lax.broadcasted_iota(jnp.int32, shape, len(shape) - 1) < HEAD_DIM


def _swap_halves(x):
    return pltpu.roll(x, HEAD_DIM, axis=x.ndim - 1)


def _win_kv_copy(h):
    return 0 if (h // (WIN_HEADS // WIN_KV_HEADS)) == (h % 2) else 1


def _stack_heads(q_pairs, heads):
    low = _lane_is_low(q_pairs[heads[0] // 2].shape)
    rows = [jnp.where(low if h % 2 == 0 else jnp.logical_not(low), q_pairs[h // 2], 0.0).astype(BF16)
            for h in heads]
    return rows[0] if len(rows) == 1 else jnp.concatenate(rows, axis=0)


def _per_head_column(values, rows_per_head):
    blk = lax.broadcasted_iota(jnp.int32, (len(values) * rows_per_head, 1), 0) // rows_per_head
    col = jnp.full(blk.shape, values[0], F32)
    for i in range(1, len(values)):
        col = jnp.where(blk == i, values[i], col)
    return col


def _attend(q_stack, kv_list, extra_logit=None):
    scores = []
    for k, _, post in kv_list:
        s = _dot_nt(q_stack, k)
        scores.append(post(s) if post is not None else s)
    probs = _softmax_parts(scores, extra_logit)
    o = None
    for p, (_, v, _) in zip(probs, kv_list):
        t = _dot(p.astype(BF16), v)
        o = t if o is None else o + t
    return o


def _merge_pair(o_even, o_odd):
    return jnp.where(_lane_is_low(o_even.shape), o_even, o_odd)


def _gqa_attention(q_pairs, rows, kv_for_copy, sinks):
    per_head = {}
    for copy in (0, 1):
        heads = [h for h in range(WIN_HEADS) if _win_kv_copy(h) == copy]
        o = _attend(_stack_heads(q_pairs, heads), kv_for_copy(copy),
                    _per_head_column([sinks[h] for h in heads], rows))
        for i, h in enumerate(heads):
            per_head[h] = o[i * rows:(i + 1) * rows]
    return [_merge_pair(per_head[2 * j], per_head[2 * j + 1]) for j in range(WIN_HEADS // 2)]


def _ctx_mixer_kernel(l, sink_ref, f_ref, qw_ref, kw_ref, vw_ref, qn_ref, kn_ref, vn_ref,
                      bc_ref, bs_ref, cs_ref, ss_ref, o_ref):
    f_hi, f_lo = _split(f_ref[...])
    fc_hi, fc_lo = _split(_dot3(f_hi, f_lo, *_split(bc_ref[...])))
    fs_hi, fs_lo = _split(_dot3(f_hi, f_lo, *_split(bs_ref[...])))
    z = _dot3(*_split(cs_ref[...]), fc_hi, fc_lo) - _dot3(*_split(ss_ref[...]), fs_hi, fs_lo)
    o_ref[:, 0:F_WIDTH] = (z * (SEQ * HEAD_DIM) ** -0.5).astype(BF16)

    kv = [(kw_ref[...].astype(BF16), vw_ref[...].astype(BF16)),
          (_swap_halves(kw_ref[...]).astype(BF16), _swap_halves(vw_ref[...]).astype(BF16))]
    q_pairs = [qw_ref[:, LANES * j:LANES * (j + 1)] * ATTN_SCALE for j in range(WIN_HEADS // 2)]
    outs = _gqa_attention(q_pairs, SEQ, lambda c: [(kv[c][0], kv[c][1], None)],
                          [sink_ref[l, h] for h in range(WIN_HEADS)])
    for j, o in enumerate(outs):
        o_ref[:, F_WIDTH + LANES * j:F_WIDTH + LANES * (j + 1)] = o.astype(BF16)

    for j in range(NA_HEADS // 2):
        sl = slice(LANES * j, LANES * (j + 1))
        q_pairs = {j: qn_ref[:, sl] * ATTN_SCALE}
        o = _attend(_stack_heads(q_pairs, (2 * j, 2 * j + 1)),
                    [(kn_ref[:, sl].astype(BF16), vn_ref[:, sl].astype(BF16), None)])
        base = F_WIDTH + WIN_Q + LANES * j
        o_ref[:, base:base + LANES] = _merge_pair(o[:SEQ], o[SEQ:]).astype(BF16)


def _ctx_mixer(l, win_sink, proj, dft_ch, dft_seq):
    f, qw, kw, vw, qn, kn, vn = proj
    row = lambda w: pl.BlockSpec((SEQ, w), lambda b: (b, 0))
    const = lambda n: pl.BlockSpec((n, n), lambda b: (0, 0))
    return pl.pallas_call(
        functools.partial(_ctx_mixer_kernel, l),
        grid=(BATCH,),
        in_specs=[pl.BlockSpec(memory_space=pltpu.SMEM),
                  row(F_WIDTH), row(WIN_Q), row(WIN_KV), row(WIN_KV), row(NA_W), row(NA_W), row(NA_W)]
                 + [const(F_WIDTH)] * 2 + [const(SEQ)] * 2,
        out_specs=pl.BlockSpec((SEQ, D_MODEL), lambda b: (b, 0)),
        out_shape=jax.ShapeDtypeStruct((N_CTX, D_MODEL), BF16),
        compiler_params=_params("arbitrary"),
        name=f"ctx_mixer{l}",
    )(win_sink, f, qw, kw, vw, qn, kn, vn, *dft_ch, *dft_seq)


FT_ROWS = 256


def _lat_fourier_kernel(f_ref, bc_ref, bs_ref, cs_ref, ss_ref, o_ref, st_ref):
    b = pl.program_id(1)

    @pl.when(pl.program_id(0) == 0)
    def _():
        f_hi, f_lo = _split(f_ref[pl.ds(pl.multiple_of(b * DEC_SEQ, DEC_SEQ), DEC_SEQ), :])
        fc_hi, fc_lo = _split(_dot3(f_hi, f_lo, *_split(bc_ref[...])))
        fs_hi, fs_lo = _split(_dot3(f_hi, f_lo, *_split(bs_ref[...])))
        st_ref[b, 0] = fc_hi
        st_ref[b, 1] = fc_lo
        st_ref[b, 2] = fs_hi
        st_ref[b, 3] = fs_lo

    z = (_dot3(*_split(cs_ref[...]), st_ref[b, 0], st_ref[b, 1])
         - _dot3(*_split(ss_ref[...]), st_ref[b, 2], st_ref[b, 3]))
    o_ref[...] = (z * (DEC_SEQ * HEAD_DIM) ** -0.5).astype(BF16)


def _lat_fourier(f, dft_ch, dft_seq):
    nrb = DEC_SEQ // FT_ROWS
    const = pl.BlockSpec((F_WIDTH, F_WIDTH), lambda r, b: (0, 0))
    rows = pl.BlockSpec((FT_ROWS, DEC_SEQ), lambda r, b: (r, 0))
    return pl.pallas_call(
        _lat_fourier_kernel,
        grid=(nrb, DEC_BATCH),
        in_specs=[pl.BlockSpec((N_LAT, F_WIDTH), lambda r, b: (1, 0))] + [const] * 2 + [rows] * 2,
        out_specs=pl.BlockSpec((FT_ROWS, F_WIDTH), lambda r, b: (b * nrb + r, 0)),
        out_shape=jax.ShapeDtypeStruct((N_LAT, F_WIDTH), BF16),
        scratch_shapes=[pltpu.VMEM((DEC_BATCH, 4, DEC_SEQ, F_WIDTH), BF16)],
        compiler_params=_params("arbitrary", "arbitrary"),
        name="lat_fourier",
    )(f, *dft_ch, *dft_seq)


def _rope(x, cos, sin_signed):
    n = x.shape[-1]
    lane = lax.broadcasted_iota(jnp.int32, x.shape, x.ndim - 1)
    first = (lane % 32) < 16
    partner = jnp.where(first, pltpu.roll(x, n - 16, axis=x.ndim - 1), pltpu.roll(x, 16, axis=x.ndim - 1))
    return x * cos + partner * sin_signed


def _win_kernel(l, sink_ref, q_ref, k_ref, v_ref, ck_ref, cv_ref, cos_ref, sin_ref, cosq_ref, sinq_ref,
                o_ref, kp_ref, vp_ref, cp_ref):
    n = pl.program_id(1)
    nb = DEC_SEQ // WIN_BLOCK
    pad = WIN_BLOCK

    @pl.when(n == 0)
    def _():
        zeros = jnp.zeros((pad, LANES), BF16)
        kr = _rope(k_ref[...], cos_ref[...], sin_ref[...])
        v = v_ref[...]
        for idx, (kk, vv) in enumerate(((kr, v), (_swap_halves(kr), _swap_halves(v)))):
            kp_ref[idx, 0:pad] = zeros
            kp_ref[idx, pad + DEC_SEQ:] = zeros
            kp_ref[idx, pad:pad + DEC_SEQ] = kk.astype(BF16)
            vp_ref[idx, 0:pad] = zeros
            vp_ref[idx, pad + DEC_SEQ:] = zeros
            vp_ref[idx, pad:pad + DEC_SEQ] = vv.astype(BF16)
        ck = ck_ref[...]
        cv = cv_ref[...]
        cp_ref[0] = ck.astype(BF16)
        cp_ref[1] = _swap_halves(ck).astype(BF16)
        cp_ref[2] = cv.astype(BF16)
        cp_ref[3] = _swap_halves(cv).astype(BF16)

    lo = jnp.where(n == 0, WIN_BLOCK, 0)
    hi = jnp.where(n == nb - 1, 2 * WIN_BLOCK, 3 * WIN_BLOCK)

    i = lax.broadcasted_iota(jnp.int32, (WIN_BLOCK, 3 * WIN_BLOCK), 0)
    j = lax.broadcasted_iota(jnp.int32, (WIN_BLOCK, 3 * WIN_BLOCK), 1)
    mask = (j >= i + WIN_BLOCK - WINDOW) & (j <= i + WIN_BLOCK + WINDOW) & (j >= lo) & (j < hi)
    band_bias = jnp.where(mask, 0.0, NEG_INF)

    def band(s):
        heads = s.shape[0] // WIN_BLOCK
        return (s.reshape(heads, WIN_BLOCK, s.shape[1]) + band_bias[None]).reshape(s.shape)

    start = pl.multiple_of(n * WIN_BLOCK, WIN_BLOCK)
    win = pl.ds(start, 3 * WIN_BLOCK)
    q_pairs = [_rope(q_ref[:, LANES * jp:LANES * (jp + 1)], cosq_ref[...], sinq_ref[...]) * ATTN_SCALE
               for jp in range(WIN_HEADS // 2)]
    outs = _gqa_attention(
        q_pairs, WIN_BLOCK,
        lambda c: [(kp_ref[c, win, :], vp_ref[c, win, :], band), (cp_ref[c], cp_ref[2 + c], None)],
        [sink_ref[l, h] for h in range(WIN_HEADS)])
    for jp, o in enumerate(outs):
        o_ref[:, LANES * jp:LANES * (jp + 1)] = o.astype(BF16)


def _lat_window(l, win_sink, qw, kw, vw, cache_k, cache_v, cos_t, sin_t):
    nb = DEC_SEQ // WIN_BLOCK
    lat_blk0 = N_CTX // DEC_SEQ
    kv_spec = pl.BlockSpec((DEC_SEQ, WIN_KV), lambda b, n: (lat_blk0 + b, 0))
    cache_spec = pl.BlockSpec((None, None, PAST_LEN, WIN_KV), lambda b, n: (b, l, 0, 0))
    tab_all = pl.BlockSpec((DEC_SEQ, LANES), lambda b, n: (0, 0))
    tab_blk = pl.BlockSpec((WIN_BLOCK, LANES), lambda b, n: (n, 0))
    return pl.pallas_call(
        functools.partial(_win_kernel, l),
        grid=(DEC_BATCH, nb),
        in_specs=[pl.BlockSpec(memory_space=pltpu.SMEM),
                  pl.BlockSpec((WIN_BLOCK, WIN_Q), lambda b, n: (N_CTX // WIN_BLOCK + b * nb + n, 0)),
                  kv_spec, kv_spec, cache_spec, cache_spec, tab_all, tab_all, tab_blk, tab_blk],
        out_specs=pl.BlockSpec((WIN_BLOCK, WIN_Q), lambda b, n: (b * nb + n, 0)),
        out_shape=jax.ShapeDtypeStruct((N_LAT, WIN_Q), BF16),
        scratch_shapes=[pltpu.VMEM((2, DEC_SEQ + 2 * WIN_BLOCK, LANES), BF16),
                        pltpu.VMEM((2, DEC_SEQ + 2 * WIN_BLOCK, LANES), BF16),
                        pltpu.VMEM((4, PAST_LEN, LANES), BF16)],
        compiler_params=_params("arbitrary", "arbitrary"),
        name=f"lat_window{l}",
    )(win_sink, qw, kw, vw, cache_k, cache_v, cos_t, sin_t, cos_t, sin_t)


NA_G = 4
NA_Q = NA_G * GRID_W
NA_WIN_ROWS = NA_ROWS + NA_G
NA_BLOCKS = GRID_ROWS // NA_G
NA_VARIANTS = 3


def _na_block_start(g):
    return jnp.clip(NA_G * g - NA_ROWS // 2, 0, GRID_ROWS - NA_WIN_ROWS)


def _na_kernel(q_ref, k_ref, v_ref, ck_ref, cv_ref, bias_ref, o_ref):
    g = pl.program_id(1)
    start = pl.multiple_of(_na_block_start(g) * GRID_W, GRID_W)
    win = pl.ds(start, NA_WIN_ROWS * GRID_W)
    for jp in range(NA_HEADS // 2):
        sl = slice(LANES * jp, LANES * (jp + 1))
        q_stack = _stack_heads({jp: q_ref[:, sl] * ATTN_SCALE}, (2 * jp, 2 * jp + 1))
        bias = bias_ref[jp]
        o = _attend(q_stack, [(k_ref[win, sl].astype(BF16), v_ref[win, sl].astype(BF16), lambda s: s + bias),
                              (ck_ref[:, sl].astype(BF16), cv_ref[:, sl].astype(BF16), None)])
        o_ref[:, sl] = _merge_pair(o[:NA_Q], o[NA_Q:]).astype(BF16)


def _lat_neighbourhood(l, qn, kn, vn, cache_k, cache_v, bias_tab):
    lat_blk0 = N_CTX // DEC_SEQ
    kv_spec = pl.BlockSpec((DEC_SEQ, NA_W), lambda b, g: (lat_blk0 + b, 0))
    cache_spec = pl.BlockSpec((None, None, PAST_LEN, NA_W), lambda b, g: (b, l, 0, 0))
    variant = lambda g: jnp.where(g == 0, 0, jnp.where(g == NA_BLOCKS - 1, 2, 1))
    return pl.pallas_call(
        _na_kernel,
        grid=(DEC_BATCH, NA_BLOCKS),
        in_specs=[pl.BlockSpec((NA_Q, NA_W), lambda b, g: (N_CTX // NA_Q + b * NA_BLOCKS + g, 0)),
                  kv_spec, kv_spec, cache_spec, cache_spec,
                  pl.BlockSpec((None, None, NA_HEADS // 2, 2 * NA_Q, NA_WIN_ROWS * GRID_W),
                               lambda b, g: (l, variant(g), 0, 0, 0))],
        out_specs=pl.BlockSpec((NA_Q, NA_W), lambda b, g: (b * NA_BLOCKS + g, 0)),
        out_shape=jax.ShapeDtypeStruct((N_LAT, NA_W), BF16),
        compiler_params=_params("arbitrary", "arbitrary"),
        name=f"lat_neighbourhood{l}",
    )(qn, kn, vn, cache_k, cache_v, bias_tab)


OUTPROJ_ROWS = 256


def _outproj_kernel(xc_ref, xl_ref, mc_ref, mf_ref, mw_ref, mn_ref, mod_ref, g_ref, w_ref, r_hi, r_lo,
                    x_ref, h_ref, lg_ref, wb_ref):
    is_ctx = pl.program_id(0) < N_CTX // TM

    @pl.when(pl.program_id(0) == 0)
    def _():
        wb_ref[...] = w_ref[...].astype(BF16)

    d = D_MODEL
    for r0 in range(0, TM, OUTPROJ_ROWS):
        rows = slice(r0, r0 + OUTPROJ_ROWS)
        m_lat = jnp.concatenate([mf_ref[rows, :], mw_ref[rows, :], mn_ref[rows, :]], axis=1)
        mixed = _dot(jnp.where(is_ctx, mc_ref[rows, :], m_lat), wb_ref[...])
        x = jnp.where(is_ctx, xc_ref[rows, :], xl_ref[rows, :]) + mod_ref[:, 2 * d:3 * d] * mixed
        x_ref[rows, :] = x
        h = _rms_mod(x, g_ref[...], mod_ref[:, 3 * d:4 * d], mod_ref[:, 4 * d:5 * d])
        h_ref[rows, :] = h.astype(BF16)
        h_hi, h_lo = _split(h)
        lg_ref[rows, :] = _dot3(h_hi, h_lo, r_hi[...], r_lo[...])


def _outproj(l, x_ctx, x_lat, mixed_ctx, mixed_f, mixed_w, mixed_n, mod4, g_ffn, w_out, r_hi, r_lo):
    n_ctx_blocks = N_CTX // TM
    xc_spec, xl_spec = _two_stream_specs(D_MODEL)
    lat = lambda w: pl.BlockSpec((TM, w), lambda i: (jnp.maximum(i - n_ctx_blocks, 0), 0))
    whole = lambda shape: pl.BlockSpec(shape, lambda i: (0,) * len(shape))
    row = lambda w: pl.BlockSpec((TM, w), lambda i: (i, 0))
    return pl.pallas_call(
        _outproj_kernel,
        grid=(N_TOK // TM,),
        in_specs=[xc_spec, xl_spec,
                  pl.BlockSpec((TM, D_MODEL), lambda i: (jnp.minimum(i, n_ctx_blocks - 1), 0)),
                  lat(F_WIDTH), lat(WIN_Q), lat(NA_W),
                  pl.BlockSpec((None, None, 1, 6 * D_MODEL), lambda i: (l, _cond_of_block(i), 0, 0)),
                  pl.BlockSpec((None, 1, D_MODEL), lambda i: (l, 0, 0)),
                  pl.BlockSpec((None, D_MODEL, D_MODEL), lambda i: (l, 0, 0)),
                  whole((D_MODEL, LANES)), whole((D_MODEL, LANES))],
        out_specs=[row(D_MODEL), row(D_MODEL), row(LANES)],
        out_shape=[jax.ShapeDtypeStruct((N_TOK, D_MODEL), F32),
                   jax.ShapeDtypeStruct((N_TOK, D_MODEL), BF16),
                   jax.ShapeDtypeStruct((N_TOK, LANES), F32)],
        scratch_shapes=[pltpu.VMEM((D_MODEL, D_MODEL), BF16)],
        compiler_params=_params("arbitrary"),
        name=f"outproj{l}",
    )(x_ctx, x_lat, mixed_ctx, mixed_f, mixed_w, mixed_n, mod4, g_ffn.reshape(DEPTH, 1, D_MODEL), w_out,
      r_hi, r_lo)


PREFIX_CHUNK = 256
MANTISSA_STEPS = 44


def _prefix_exclusive(m):
    rows, n = m.shape
    t0 = lax.broadcasted_iota(jnp.int32, (PREFIX_CHUNK, PREFIX_CHUNK), 0)
    t1 = lax.broadcasted_iota(jnp.int32, (PREFIX_CHUNK, PREFIX_CHUNK), 1)
    upper = jnp.where(t0 < t1, 1.0, 0.0).astype(BF16)
    carry = jnp.zeros((rows, 1), F32)
    outs = []
    for c in range(n // PREFIX_CHUNK):
        blk = m[:, c * PREFIX_CHUNK:(c + 1) * PREFIX_CHUNK]
        outs.append(_dot(blk.astype(BF16), upper) + carry)
        carry = carry + blk.sum(axis=-1, keepdims=True)
    return outs[0] if len(outs) == 1 else jnp.concatenate(outs, axis=-1)


def _route_kernel(cap, lg_ref, slot_ref, aff_ref):
    x = lg_ref[...]
    e = jnp.exp(x - x.max(axis=1, keepdims=True))
    aff3 = e / e.sum(axis=1, keepdims=True)
    aff = aff3.reshape(aff3.shape[0] * aff3.shape[1], aff3.shape[2])
    capf = float(cap)

    def count_ge(t):
        return jnp.where(aff >= t, 1.0, 0.0).sum(axis=-1, keepdims=True)

    above = jnp.full((aff.shape[0], 1), 2.0, F32)
    for s in (64, 32, 16, 8, 4, 2, 1):
        cand = above * (2.0 ** -s)
        above = jnp.where(count_ge(cand) >= capf, above, cand)
    base = above * 0.5
    base = jnp.where(count_ge(base) >= capf, base, 0.0)

    def refine(_, carry):
        thr, inc = carry
        cand = thr + inc
        return jnp.where(count_ge(cand) >= capf, cand, thr), inc * 0.5

    thr, _ = lax.fori_loop(0, MANTISSA_STEPS, refine, (base, base * 0.5))
    gt = jnp.where(aff > thr, 1.0, 0.0)
    eq = jnp.where(aff == thr, 1.0, 0.0)
    need = capf - gt.sum(axis=-1, keepdims=True)
    sel = gt + eq * jnp.where(_prefix_exclusive(eq) < need, 1.0, 0.0)
    slot = _prefix_exclusive(sel)
    slot_ref[...] = jnp.where(sel > 0.0, slot, -1.0).astype(jnp.int32)
    aff_ref[...] = aff


def _route(lg3, cap, name):
    groups, n_e, n = lg3.shape
    rows = groups * n_e
    return pl.pallas_call(
        functools.partial(_route_kernel, cap),
        out_shape=[jax.ShapeDtypeStruct((rows, n), jnp.int32), jax.ShapeDtypeStruct((rows, n), F32)],
        compiler_params=pltpu.CompilerParams(vmem_limit_bytes=VMEM_LIMIT),
        name=name,
    )(lg3)


def _gather_ctx_kernel(h_ref, slot_ref, aff_ref, x_ref, g_ref, p_ref):
    s_iota = lax.broadcasted_iota(jnp.int32, (CAP_CTX, SEQ), 0)
    for e in range(N_EXPERTS):
        hit = s_iota == slot_ref[e:e + 1, :]
        p_ref[e * CAP_CTX:(e + 1) * CAP_CTX, :] = jnp.where(hit, 1.0, 0.0).astype(BF16)
        g_ref[e] = jnp.where(hit, aff_ref[e:e + 1, :], 0.0).sum(axis=-1, keepdims=True)
    x = _dot(p_ref[...], h_ref[...]).astype(BF16)
    x_ref[...] = x.reshape(N_EXPERTS, CAP_CTX, D_MODEL)


def _gather_ctx(h, slot, aff):
    return pl.pallas_call(
        _gather_ctx_kernel,
        grid=(BATCH,),
        in_specs=[pl.BlockSpec((SEQ, D_MODEL), lambda b: (b, 0)),
                  pl.BlockSpec((N_EXPERTS, SEQ), lambda b: (b, 0)),
                  pl.BlockSpec((N_EXPERTS, SEQ), lambda b: (b, 0))],
        out_specs=[pl.BlockSpec((N_EXPERTS, CAP_CTX, D_MODEL), lambda b: (0, b, 0)),
                   pl.BlockSpec((N_EXPERTS, CAP_CTX, 1), lambda b: (0, b, 0))],
        out_shape=[jax.ShapeDtypeStruct((N_EXPERTS, ROWS_CTX, D_MODEL), BF16),
                   jax.ShapeDtypeStruct((N_EXPERTS, ROWS_CTX, 1), F32)],
        scratch_shapes=[pltpu.VMEM((N_EXPERTS * CAP_CTX, SEQ), BF16)],
        compiler_params=_params("arbitrary"),
        name="gather_ctx",
    )(h, slot, aff)


def _gather_lat_kernel(h_ref, slot_ref, aff_ref, x_ref, g_ref):
    e = pl.program_id(1)
    s_iota = lax.broadcasted_iota(jnp.int32, (CAP_LAT, DEC_SEQ), 0)
    hit = s_iota == slot_ref[pl.ds(e, 1), :]
    x_ref[...] = _dot(jnp.where(hit, 1.0, 0.0).astype(BF16), h_ref[...]).astype(BF16)
    g_ref[...] = jnp.where(hit, aff_ref[pl.ds(e, 1), :], 0.0).sum(axis=-1, keepdims=True)


def _gather_lat(h, slot, aff):
    lat_blk0 = N_CTX // DEC_SEQ
    return pl.pallas_call(
        _gather_lat_kernel,
        grid=(DEC_BATCH, N_EXPERTS),
        in_specs=[pl.BlockSpec((DEC_SEQ, D_MODEL), lambda b, e: (lat_blk0 + b, 0)),
                  pl.BlockSpec((N_EXPERTS, DEC_SEQ), lambda b, e: (b, 0)),
                  pl.BlockSpec((N_EXPERTS, DEC_SEQ), lambda b, e: (b, 0))],
        out_specs=[pl.BlockSpec((None, CAP_LAT, D_MODEL), lambda b, e: (e, b, 0)),
                   pl.BlockSpec((None, CAP_LAT, 1), lambda b, e: (e, b, 0))],
        out_shape=[jax.ShapeDtypeStruct((N_EXPERTS, ROWS_LAT, D_MODEL), BF16),
                   jax.ShapeDtypeStruct((N_EXPERTS, ROWS_LAT, 1), F32)],
        compiler_params=_params("arbitrary", "arbitrary"),
        name="gather_lat",
    )(h, slot, aff)


N_UP_STEPS = D_FF // TF


def _silu_tanh(x):
    return x * (0.5 + 0.5 * jnp.tanh(0.5 * x))


def _ffn_kernel(xc_ref, xl_ref, gc_ref, gl_ref, wg_ref, wu_ref, wd_ref, y_ref, x_sc, h_sc):
    j = pl.program_id(1)

    @pl.when(j == 0)
    def _():
        x_sc[0:ROWS_CTX, :] = xc_ref[...]
        x_sc[ROWS_CTX:, :] = xl_ref[...]

    @pl.when(j < N_UP_STEPS)
    def _():
        x = x_sc[...]
        a = _dot(x, wg_ref[...].astype(BF16))
        u = _dot(x, wu_ref[...].astype(BF16))
        h_sc[j] = (_silu_tanh(a) * u).astype(BF16)

    @pl.when(j >= N_UP_STEPS)
    def _():
        for c0 in range(0, D_MODEL, TD):
            acc = None
            for k in range(N_UP_STEPS):
                t = _dot(h_sc[k], wd_ref[k * TF:(k + 1) * TF, c0:c0 + TD].astype(BF16))
                acc = t if acc is None else acc + t
            y_ref[0:ROWS_CTX, c0:c0 + TD] = (acc[0:ROWS_CTX] * gc_ref[...]).astype(BF16)
            y_ref[ROWS_CTX:, c0:c0 + TD] = (acc[ROWS_CTX:] * gl_ref[...]).astype(BF16)


def _ffn(l, x_c, x_l, g_c, g_l, w_gate, w_up, w_down):
    rows = ROWS_CTX + ROWS_LAT
    xin = lambda r: pl.BlockSpec((None, r, D_MODEL), lambda e, j: (e, 0, 0))
    gin = lambda r: pl.BlockSpec((None, r, 1), lambda e, j: (e, 0, 0))
    up_chunk = lambda e, j: (l, e, 0, jnp.minimum(j, N_UP_STEPS - 1))
    return pl.pallas_call(
        _ffn_kernel,
        grid=(N_EXPERTS, N_UP_STEPS + 1),
        in_specs=[xin(ROWS_CTX), xin(ROWS_LAT), gin(ROWS_CTX), gin(ROWS_LAT),
                  pl.BlockSpec((None, None, D_MODEL, TF), up_chunk),
                  pl.BlockSpec((None, None, D_MODEL, TF), up_chunk),
                  pl.BlockSpec((None, None, D_FF, D_MODEL), lambda e, j: (l, e, 0, 0))],
        out_specs=pl.BlockSpec((None, rows, D_MODEL), lambda e, j: (e, 0, 0)),
        out_shape=jax.ShapeDtypeStruct((N_EXPERTS, rows, D_MODEL), BF16),
        scratch_shapes=[pltpu.VMEM((rows, D_MODEL), BF16), pltpu.VMEM((N_UP_STEPS, rows, TF), BF16)],
        compiler_params=_params("arbitrary", "arbitrary"),
        name=f"experts{l}",
    )(x_c, x_l, g_c, g_l, w_gate, w_up, w_down)


def _finish(x, res, mod_ref, gf_ref, final):
    y = x + mod_ref[:, 5 * D_MODEL:] * res
    if final:
        y = y * lax.rsqrt(jnp.mean(y * y, axis=-1, keepdims=True) + RMS_EPS) * gf_ref[...]
    return y


def _combine_ctx_kernel(final, x_ref, y_ref, slot_ref, rep_ref, mod_ref, gf_ref, o_ref):
    n_col = N_EXPERTS * CAP_CTX
    spread = _dot(slot_ref[...].astype(F32).astype(BF16), rep_ref[...])
    col = lax.broadcasted_iota(jnp.int32, (SEQ, n_col), 1) % CAP_CTX
    p = jnp.where(spread == col.astype(F32), 1.0, 0.0).astype(BF16)
    res = _dot(p, y_ref[...].reshape(n_col, D_MODEL))
    o_ref[...] = _finish(x_ref[...], res, mod_ref, gf_ref, final)


def _combine_ctx(l, final, x_new, y, slot_t, mod4, g_final):
    n_col = N_EXPERTS * CAP_CTX
    rep = (np.arange(n_col)[None, :] // CAP_CTX == np.arange(N_EXPERTS)[:, None]).astype(np.float32)
    return pl.pallas_call(
        functools.partial(_combine_ctx_kernel, final),
        grid=(BATCH,),
        in_specs=[pl.BlockSpec((SEQ, D_MODEL), lambda b: (b, 0)),
                  pl.BlockSpec((N_EXPERTS, CAP_CTX, D_MODEL), lambda b: (0, b, 0)),
                  pl.BlockSpec((SEQ, N_EXPERTS), lambda b: (b, 0)),
                  pl.BlockSpec((N_EXPERTS, n_col), lambda b: (0, 0)),
                  pl.BlockSpec((None, None, 1, 6 * D_MODEL), lambda b: (l, 0, 0, 0)),
                  pl.BlockSpec((1, D_MODEL), lambda b: (0, 0))],
        out_specs=pl.BlockSpec((SEQ, D_MODEL), lambda b: (b, 0)),
        out_shape=jax.ShapeDtypeStruct((N_CTX, D_MODEL), F32),
        compiler_params=_params("arbitrary"),
        name=f"combine_ctx{l}",
    )(x_new, y, slot_t, jnp.asarray(rep, BF16), mod4, g_final.reshape(1, D_MODEL))


TMC = 512


def _combine_lat_kernel(final, x_ref, y_ref, slot_ref, mod_ref, gf_ref, o_ref):
    s_iota = lax.broadcasted_iota(jnp.int32, (TMC, CAP_LAT), 1)
    slot = slot_ref[...]
    res = None
    for e in range(N_EXPERTS):
        p = jnp.where(slot[:, e:e + 1] == s_iota, 1.0, 0.0).astype(BF16)
        t = _dot(p, y_ref[e])
        res = t if res is None else res + t
    o_ref[...] = _finish(x_ref[...], res, mod_ref, gf_ref, final)


def _combine_lat(l, final, x_new, y, slot_t, mod4, g_final):
    nt = DEC_SEQ // TMC
    return pl.pallas_call(
        functools.partial(_combine_lat_kernel, final),
        grid=(DEC_BATCH, nt),
        in_specs=[pl.BlockSpec((TMC, D_MODEL), lambda b, t: (N_CTX // TMC + b * nt + t, 0)),
                  pl.BlockSpec((N_EXPERTS, CAP_LAT, D_MODEL), lambda b, t: (0, ROWS_CTX // CAP_LAT + b, 0)),
                  pl.BlockSpec((TMC, N_EXPERTS), lambda b, t: (b * nt + t, 0)),
                  pl.BlockSpec((None, None, 1, 6 * D_MODEL), lambda b, t: (l, 1 + b, 0, 0)),
                  pl.BlockSpec((1, D_MODEL), lambda b, t: (0, 0))],
        out_specs=pl.BlockSpec((TMC, D_MODEL), lambda b, t: (b * nt + t, 0)),
        out_shape=jax.ShapeDtypeStruct((N_LAT, D_MODEL), F32),
        compiler_params=_params("arbitrary", "arbitrary"),
        name=f"combine_lat{l}",
    )(x_new, y, slot_t, mod4, g_final.reshape(1, D_MODEL))


def _split_table(t):
    hi = t.astype(BF16)
    return hi, (t - hi.astype(F32)).astype(BF16)


def _dft_tables(n):
    p = np.arange(n, dtype=np.int64)
    ang = ((p[:, None] * p[None, :]) % n).astype(np.float64) * (2.0 * np.pi / n)
    return np.cos(ang).astype(np.float32), np.sin(ang).astype(np.float32)


def _channel_dft_tables():
    c = np.arange(F_WIDTH, dtype=np.int64)
    same = (c[:, None] // HEAD_DIM) == (c[None, :] // HEAD_DIM)
    ang = (((c[:, None] % HEAD_DIM) * (c[None, :] % HEAD_DIM)) % HEAD_DIM).astype(np.float64) * (2.0 * np.pi / HEAD_DIM)
    return (np.where(same, np.cos(ang), 0.0).astype(np.float32),
            np.where(same, np.sin(ang), 0.0).astype(np.float32))


def _rope_tables():
    half = HEAD_DIM // 2
    nf = half // 2
    pos = jnp.arange(DEC_SEQ)
    inv = 1.0 / (ROPE_BASE ** (jnp.arange(nf, dtype=F32) / nf))
    ang_r = (pos // GRID_W).astype(F32)[:, None] * inv
    ang_c = (pos % GRID_W).astype(F32)[:, None] * inv

    def head(fn, sign):
        return jnp.concatenate([sign * fn(ang_r), fn(ang_r), sign * fn(ang_c), fn(ang_c)], axis=-1)

    cos = head(jnp.cos, 1.0)
    sin = head(jnp.sin, -1.0)
    return jnp.concatenate([cos, cos], axis=-1), jnp.concatenate([sin, sin], axis=-1)


def _na_bias_tables(rpb):
    cq = np.arange(GRID_W)
    rel_c = np.clip(cq[None, :] - cq[:, None] + NA_COLS - 1, 0, 2 * NA_COLS - 2)
    pick = (rel_c[:, :, None] == np.arange(2 * NA_COLS - 1)).astype(np.float32)
    cs = np.clip(cq - NA_COLS // 2, 0, GRID_W - NA_COLS)
    col_ok = (cq[None, :] >= cs[:, None]) & (cq[None, :] < cs[:, None] + NA_COLS)
    bc = jnp.einsum('lhrj,qkj->lhrqk', rpb, pick, precision=lax.Precision.HIGHEST)
    bc = jnp.where(col_ok[None, None, None], bc, NEG_INF)
    outside = jnp.full((DEPTH, NA_HEADS, GRID_W, GRID_W), NEG_INF, F32)
    variants = []
    for g in (0, 1, NA_BLOCKS - 1):
        start = int(np.clip(NA_G * g - NA_ROWS // 2, 0, GRID_ROWS - NA_WIN_ROWS))
        slabs = []
        for a in range(NA_G):
            r = NA_G * g + a
            rs = int(np.clip(r - NA_ROWS // 2, 0, GRID_ROWS - NA_ROWS))
            cols = [bc[:, :, start + w - r + NA_ROWS - 1] if rs <= start + w < rs + NA_ROWS else outside
                    for w in range(NA_WIN_ROWS)]
            slabs.append(jnp.concatenate(cols, axis=-1))
        variants.append(jnp.stack(slabs, axis=2))
    tab = jnp.stack(variants, axis=1)
    return tab.reshape(DEPTH, NA_VARIANTS, NA_HEADS // 2, 2 * NA_Q, NA_WIN_ROWS * GRID_W)


def kernel(x_prompt, x_sample, cache_win_k, cache_win_v, cache_nat_k, cache_nat_v, c, c_ctx, w_mod, b_mod, g_mix, g_ffn, w_in, w_out, win_sink, nat_rpb, w_router, w_gate, w_up, w_down, g_final):
    x_ctx = x_prompt.reshape(N_CTX, D_MODEL)
    x_lat = x_sample.reshape(N_LAT, D_MODEL)
    cond = jnp.concatenate([c_ctx[None, :], c, jnp.zeros((N_COND - 1 - DEC_BATCH, D_MODEL), F32)], axis=0)
    mod4 = _adaln(cond, w_mod, b_mod).reshape(DEPTH, N_COND, 1, 6 * D_MODEL)

    dft_ch = _channel_dft_tables()
    dft_ctx = _dft_tables(SEQ)
    dft_lat = _dft_tables(DEC_SEQ)
    cos_t, sin_t = _rope_tables()
    cwk = cache_win_k.reshape(DEC_BATCH, DEPTH, PAST_LEN, WIN_KV)
    cwv = cache_win_v.reshape(DEC_BATCH, DEPTH, PAST_LEN, WIN_KV)
    cnk = cache_nat_k.reshape(DEC_BATCH, DEPTH, PAST_LEN, NA_W)
    cnv = cache_nat_v.reshape(DEC_BATCH, DEPTH, PAST_LEN, NA_W)
    r_pad = jnp.pad(w_router, ((0, 0), (0, 0), (0, LANES - N_EXPERTS)))
    na_bias = _na_bias_tables(nat_rpb)

    new_kv = [[], [], [], []]
    for l in range(DEPTH):
        final = l == DEPTH - 1
        proj = _project(l, x_ctx, x_lat, mod4, g_mix, w_in)
        f, qw, kw, vw, qn, kn, vn = proj
        for dst, src, heads in zip(new_kv, (kw, vw, kn, vn), (WIN_KV_HEADS, WIN_KV_HEADS, NA_HEADS, NA_HEADS)):
            dst.append(src[:N_CTX].reshape(BATCH, SEQ, heads, HEAD_DIM))

        mixed_ctx = _ctx_mixer(l, win_sink, proj, dft_ch, dft_ctx)
        mixed_f = _lat_fourier(f, dft_ch, dft_lat)
        mixed_w = _lat_window(l, win_sink, qw, kw, vw, cwk, cwv, cos_t, sin_t)
        mixed_n = _lat_neighbourhood(l, qn, kn, vn, cnk, cnv, na_bias)

        r_hi, r_lo = _split_table(r_pad[l])
        x_new, h, logits = _outproj(l, x_ctx, x_lat, mixed_ctx, mixed_f, mixed_w, mixed_n, mod4, g_ffn,
                                    w_out, r_hi, r_lo)

        lg = logits[:, :N_EXPERTS]
        lg_ctx = lg[:N_CTX].reshape(BATCH, SEQ, N_EXPERTS).transpose(0, 2, 1)
        lg_lat = lg[N_CTX:].reshape(DEC_BATCH, DEC_SEQ, N_EXPERTS).transpose(0, 2, 1)
        slot_c, aff_c = _route(lg_ctx, CAP_CTX, "route_ctx")
        slot_l, aff_l = _route(lg_lat, CAP_LAT, "route_lat")
        xg_c, gate_c = _gather_ctx(h, slot_c, aff_c)
        xg_l, gate_l = _gather_lat(h, slot_l, aff_l)
        y = _ffn(l, xg_c, xg_l, gate_c, gate_l, w_gate, w_up, w_down)
        slot_ct = slot_c.reshape(BATCH, N_EXPERTS, SEQ).transpose(0, 2, 1).reshape(N_CTX, N_EXPERTS)
        slot_lt = slot_l.reshape(DEC_BATCH, N_EXPERTS, DEC_SEQ).transpose(0, 2, 1).reshape(N_LAT, N_EXPERTS)
        x_ctx = _combine_ctx(l, final, x_new, y, slot_ct, mod4, g_final)
        x_lat = _combine_lat(l, final, x_new, y, slot_lt, mod4, g_final)

    y_prompt = x_ctx.reshape(BATCH, SEQ, D_MODEL)
    y_sample = x_lat.reshape(DEC_BATCH, DEC_SEQ, D_MODEL)
    return (y_prompt, y_sample, *(jnp.stack(t, axis=1) for t in new_kv))
```

```python
import functools

import numpy as np
import jax
import jax.numpy as jnp
from jax import lax
from jax.experimental import pallas as pl
from jax.experimental.pallas import tpu as pltpu

D_MODEL = 1024
BATCH = 16
SEQ = 256
DEPTH = 2
DEC_BATCH = 2
DEC_SEQ = 2048
PAST_LEN = 256
GRID_W = 64
HEAD_DIM = 64
F_WIDTH = 256
WIN_HEADS = 6
WIN_KV_HEADS = 2
WINDOW = 128
WIN_BLOCK = 128
NA_HEADS = 6
NA_ROWS = 8
NA_COLS = 16
N_EXPERTS = 16
EC_CAPACITY = 2
D_FF = 2816
ROPE_BASE = 10000.0
RMS_EPS = 1e-6
NEG_INF = -1e30
ATTN_SCALE = HEAD_DIM ** -0.5
WIN_Q = WIN_HEADS * HEAD_DIM
WIN_KV = WIN_KV_HEADS * HEAD_DIM
NA_W = NA_HEADS * HEAD_DIM
N_IN = F_WIDTH + WIN_Q + 2 * WIN_KV + 3 * NA_W
SPLITS = (0, F_WIDTH, F_WIDTH + WIN_Q, F_WIDTH + WIN_Q + WIN_KV, F_WIDTH + WIN_Q + 2 * WIN_KV,
          F_WIDTH + WIN_Q + 2 * WIN_KV + NA_W, F_WIDTH + WIN_Q + 2 * WIN_KV + 2 * NA_W, N_IN)

N_CTX = BATCH * SEQ
N_LAT = DEC_BATCH * DEC_SEQ
N_TOK = N_CTX + N_LAT
GRID_ROWS = DEC_SEQ // GRID_W
CAP_CTX = EC_CAPACITY * SEQ // N_EXPERTS
CAP_LAT = EC_CAPACITY * DEC_SEQ // N_EXPERTS
ROWS_CTX = BATCH * CAP_CTX
ROWS_LAT = DEC_BATCH * CAP_LAT
N_COND = 8

LANES = 128
MXU_COLS = 256
TM = 512
TN_MOD = 1536
TF = 256
TD = 256
VMEM_LIMIT = 56 * 1024 * 1024

F32 = jnp.float32
BF16 = jnp.bfloat16


def _params(*sem):
    return pltpu.CompilerParams(dimension_semantics=sem, vmem_limit_bytes=VMEM_LIMIT)


def _dot(a, b):
    return jnp.dot(a, b, preferred_element_type=F32)


def _dot_nt(a, b):
    return lax.dot_general(a, b, (((1,), (1,)), ((), ())), preferred_element_type=F32)


def _split(x):
    hi = x.astype(BF16)
    lo = (x - hi.astype(F32)).astype(BF16)
    return hi, lo


def _dot3(a_hi, a_lo, b_hi, b_lo):
    return _dot(a_hi, b_hi) + (_dot(a_lo, b_hi) + _dot(a_hi, b_lo))


def _silu(x):
    return x / (1.0 + jnp.exp(-x))


def _rms_mod(x, g, shift, scale):
    y = x * lax.rsqrt(jnp.mean(x * x, axis=-1, keepdims=True) + RMS_EPS)
    return (y * g) * (1.0 + scale) + shift


def _softmax_parts(parts, sink=None):
    m = parts[0].max(axis=-1, keepdims=True)
    for s in parts[1:]:
        m = jnp.maximum(m, s.max(axis=-1, keepdims=True))
    if sink is not None:
        m = jnp.maximum(m, sink)
    es = [jnp.exp(s - m) for s in parts]
    den = es[0].sum(axis=-1, keepdims=True)
    for e in es[1:]:
        den = den + e.sum(axis=-1, keepdims=True)
    if sink is not None:
        den = den + jnp.exp(sink - m)
    inv = 1.0 / den
    return [e * inv for e in es]


def _cond_of_block(i):
    n_ctx_blocks = N_CTX // TM
    return jnp.where(i < n_ctx_blocks, 0, 1 + (i - n_ctx_blocks) // (DEC_SEQ // TM))


def _two_stream_specs(width):
    n_ctx_blocks = N_CTX // TM
    ctx = pl.BlockSpec((TM, width), lambda i: (jnp.minimum(i, n_ctx_blocks - 1), 0))
    lat = pl.BlockSpec((TM, width), lambda i: (jnp.maximum(i - n_ctx_blocks, 0), 0))
    return ctx, lat


def _pick_stream(ctx_ref, lat_ref):
    return jnp.where(pl.program_id(0) < N_CTX // TM, ctx_ref[...], lat_ref[...])


def _adaln_kernel(c_ref, w_ref, b_ref, o_ref):
    s_hi, s_lo = _split(_silu(c_ref[...]))
    w_hi, w_lo = _split(w_ref[...])
    o_ref[...] = _dot3(s_hi, s_lo, w_hi, w_lo) + b_ref[...]


def _adaln(cond, w_mod, b_mod):
    return pl.pallas_call(
        _adaln_kernel,
        grid=(DEPTH, 6 * D_MODEL // TN_MOD),
        in_specs=[
            pl.BlockSpec((N_COND, D_MODEL), lambda l, j: (0, 0)),
            pl.BlockSpec((None, D_MODEL, TN_MOD), lambda l, j: (l, 0, j)),
            pl.BlockSpec((None, 1, TN_MOD), lambda l, j: (l, 0, j)),
        ],
        out_specs=pl.BlockSpec((None, N_COND, TN_MOD), lambda l, j: (l, 0, j)),
        out_shape=jax.ShapeDtypeStruct((DEPTH, N_COND, 6 * D_MODEL), F32),
        compiler_params=_params("arbitrary", "arbitrary"),
        name="adaln",
    )(cond, w_mod, b_mod.reshape(DEPTH, 1, 6 * D_MODEL))


KV_NAMES = ("kw", "vw", "kn", "vn")
KV_WIDTHS = (WIN_KV, WIN_KV, NA_W, NA_W)
KV_SPLITS = (SPLITS[2], SPLITS[3], SPLITS[5], SPLITS[6])
KV_TOTAL = sum(KV_WIDTHS)
Q_SPLITS = (SPLITS[0], SPLITS[1], SPLITS[4])


def _proj_kernel(n_prev, xc_ref, xl_ref, mod_ref, g_ref, w_ref, *rest):
    rest = rest[n_prev:]
    f_ref, qw_ref, qn_ref = rest[0:3]
    lat_kv = rest[3:7]
    ctx_kvt = rest[7:11]
    wb_ref, wt_ref = rest[11:13]
    i = pl.program_id(0)

    @pl.when(i == 0)
    def _():
        wb_ref[...] = w_ref[...].astype(BF16)
        r0 = 0
        for c0, width in zip(KV_SPLITS, KV_WIDTHS):
            wt_ref[r0:r0 + width, :] = w_ref[:, c0:c0 + width].T.astype(BF16)
            r0 += width

    x = _pick_stream(xc_ref, xl_ref)
    h = _rms_mod(x, g_ref[...], mod_ref[:, 0:D_MODEL], mod_ref[:, D_MODEL:2 * D_MODEL]).astype(BF16)
    natural = dict(zip(SPLITS[:-1], (f_ref, qw_ref) + tuple(lat_kv[0:2]) + (qn_ref,) + tuple(lat_kv[2:4])))

    def project_tiles(wanted):
        for t0 in range(0, N_IN, MXU_COLS):
            hits = [(a, b) for a, b in zip(SPLITS[:-1], SPLITS[1:])
                    if a in wanted and max(a, t0) < min(b, t0 + MXU_COLS)]
            if not hits:
                continue
            acc = _dot(h, wb_ref[:, t0:t0 + MXU_COLS])
            for a, b in hits:
                lo, hi = max(a, t0), min(b, t0 + MXU_COLS)
                natural[a][:, lo - a:hi - a] = acc[:, lo - t0:hi - t0]

    @pl.when(i < N_CTX // TM)
    def _():
        project_tiles(Q_SPLITS)
        kvt =_dot_nt(wt_ref[...], h)
        r0 = 0
        for o_ref, width in zip(ctx_kvt, KV_WIDTHS):
            for bb in range(TM // SEQ):
                o_ref[bb] = kvt[r0:r0 + width, bb * SEQ:(bb + 1) * SEQ]
            r0 += width

    @pl.when(i >= N_CTX // TM)
    def _():
        project_tiles(Q_SPLITS + KV_SPLITS)


def _project(l, x_ctx, x_lat, mod4, g_mix, w_in, prev_kvt):
    n_ctx_blocks = N_CTX // TM
    xc_spec, xl_spec = _two_stream_specs(D_MODEL)
    both = lambda w: pl.BlockSpec((TM, w), lambda i: (i, 0))
    lat = lambda w: pl.BlockSpec((TM, w), lambda i: (jnp.maximum(i - n_ctx_blocks, 0), 0))
    ctx_t = lambda w: pl.BlockSpec((TM // SEQ, None, w, SEQ),
                                   lambda i: (jnp.minimum(i, n_ctx_blocks - 1), l, 0, 0))
    n_prev = len(prev_kvt)
    n_in = 5
    outs = pl.pallas_call(
        functools.partial(_proj_kernel, n_prev),
        grid=(N_TOK // TM,),
        in_specs=[
            xc_spec, xl_spec,
            pl.BlockSpec((None, None, 1, 6 * D_MODEL), lambda i: (l, _cond_of_block(i), 0, 0)),
            pl.BlockSpec((None, 1, D_MODEL), lambda i: (l, 0, 0)),
            pl.BlockSpec((None, D_MODEL, N_IN), lambda i: (l, 0, 0)),
        ] + [pl.BlockSpec(memory_space=pl.ANY)] * n_prev,
        out_specs=[both(F_WIDTH), both(WIN_Q), both(NA_W)] + [lat(w) for w in KV_WIDTHS]
                  + [ctx_t(w) for w in KV_WIDTHS],
        out_shape=[jax.ShapeDtypeStruct((N_TOK, w), F32) for w in (F_WIDTH, WIN_Q, NA_W)]
                  + [jax.ShapeDtypeStruct((N_LAT, w), F32) for w in KV_WIDTHS]
                  + [jax.ShapeDtypeStruct((BATCH, DEPTH, w, SEQ), F32) for w in KV_WIDTHS],
        input_output_aliases={n_in + k: 7 + k for k in range(n_prev)},
        scratch_shapes=[pltpu.VMEM((D_MODEL, N_IN), BF16), pltpu.VMEM((KV_TOTAL, D_MODEL), BF16)],
        compiler_params=_params("arbitrary"),
        name=f"project{l}",
    )(x_ctx, x_lat, mod4, g_mix.reshape(DEPTH, 1, D_MODEL), w_in, *prev_kvt)
    return outs[0:3], outs[3:7], outs[7:11]


def _lane_is_low(shape):
    return lax.broadcasted_iota(jnp.int32, shape, len(shape) - 1) < HEAD_DIM


def _swap_halves(x):
    return pltpu.roll(x, HEAD_DIM, axis=x.ndim - 1)


def _win_kv_copy(h):
    return 0 if (h // (WIN_HEADS // WIN_KV_HEADS)) == (h % 2) else 1


def _stack_heads(q_pairs, heads):
    low = _lane_is_low(q_pairs[heads[0] // 2].shape)
    rows = [jnp.where(low if h % 2 == 0 else jnp.logical_not(low), q_pairs[h // 2], 0.0).astype(BF16)
            for h in heads]
    return rows[0] if len(rows) == 1 else jnp.concatenate(rows, axis=0)


def _per_head_column(values, rows_per_head):
    blk = lax.broadcasted_iota(jnp.int32, (len(values) * rows_per_head, 1), 0) // rows_per_head
    col = jnp.full(blk.shape, values[0], F32)
    for i in range(1, len(values)):
        col = jnp.where(blk == i, values[i], col)
    return col


def _attend(q_stack, kv_list, extra_logit=None, transposed=False):
    scores = []
    for k, _, post in kv_list:
        s = _dot(q_stack, k) if transposed else _dot_nt(q_stack, k)
        scores.append(post(s) if post is not None else s)
    probs = _softmax_parts(scores, extra_logit)
    o = None
    for p, (_, v, _) in zip(probs, kv_list):
        t = _dot_nt(p.astype(BF16), v) if transposed else _dot(p.astype(BF16), v)
        o = t if o is None else o + t
    return o


def _merge_pair(o_even, o_odd):
    return jnp.where(_lane_is_low(o_even.shape), o_even, o_odd)


def _gqa_attention(q_pairs, rows, kv_for_copy, sinks, transposed=False):
    per_head = {}
    for copy in (0, 1):
        heads = [h for h in range(WIN_HEADS) if _win_kv_copy(h) == copy]
        o = _attend(_stack_heads(q_pairs, heads), kv_for_copy(copy),
                    _per_head_column([sinks[h] for h in heads], rows), transposed)
        for i, h in enumerate(heads):
            per_head[h] = o[i * rows:(i + 1) * rows]
    return [_merge_pair(per_head[2 * j], per_head[2 * j + 1]) for j in range(WIN_HEADS // 2)]


def _swap_row_halves(x):
    return jnp.concatenate([x[HEAD_DIM:], x[:HEAD_DIM]], axis=0)


def _ctx_mixer_kernel(l, sink_ref, f_ref, qw_ref, qn_ref, kw_ref, vw_ref, kn_ref, vn_ref,
                      bc_ref, bs_ref, cs_ref, ss_ref, o_ref):
    f_hi, f_lo = _split(f_ref[...])
    fc_hi, fc_lo = _split(_dot3(f_hi, f_lo, *_split(bc_ref[...])))
    fs_hi, fs_lo = _split(_dot3(f_hi, f_lo, *_split(bs_ref[...])))
    z = _dot3(*_split(cs_ref[...]), fc_hi, fc_lo) - _dot3(*_split(ss_ref[...]), fs_hi, fs_lo)
    o_ref[:, 0:F_WIDTH] = (z * (SEQ * HEAD_DIM) ** -0.5).astype(BF16)

    kv = [(kw_ref[...].astype(BF16), vw_ref[...].astype(BF16)),
          (_swap_row_halves(kw_ref[...]).astype(BF16), _swap_row_halves(vw_ref[...]).astype(BF16))]
    q_pairs = [qw_ref[:, LANES * j:LANES * (j + 1)] * ATTN_SCALE for j in range(WIN_HEADS // 2)]
    outs = _gqa_attention(q_pairs, SEQ, lambda c: [(kv[c][0], kv[c][1], None)],
                          [sink_ref[l, h] for h in range(WIN_HEADS)], transposed=True)
    for j, o in enumerate(outs):
        o_ref[:, F_WIDTH + LANES * j:F_WIDTH + LANES * (j + 1)] = o.astype(BF16)

    for j in range(NA_HEADS // 2):
        sl = slice(LANES * j, LANES * (j + 1))
        q_pairs = {j: qn_ref[:, sl] * ATTN_SCALE}
        o = _attend(_stack_heads(q_pairs, (2 * j, 2 * j + 1)),
                    [(kn_ref[sl, :].astype(BF16), vn_ref[sl, :].astype(BF16), None)], transposed=True)
        base = F_WIDTH + WIN_Q + LANES * j
        o_ref[:, base:base + LANES] = _merge_pair(o[:SEQ], o[SEQ:]).astype(BF16)


def _ctx_mixer(l, win_sink, f, qw, qn, kvt, dft_ch, dft_seq):
    row = lambda w: pl.BlockSpec((SEQ, w), lambda b: (b, 0))
    col = lambda w: pl.BlockSpec((None, None, w, SEQ), lambda b: (b, l, 0, 0))
    const = lambda n: pl.BlockSpec((n, n), lambda b: (0, 0))
    return pl.pallas_call(
        functools.partial(_ctx_mixer_kernel, l),
        grid=(BATCH,),
        in_specs=[pl.BlockSpec(memory_space=pltpu.SMEM),
                  row(F_WIDTH), row(WIN_Q), row(NA_W)] + [col(w) for w in KV_WIDTHS]
                 + [const(F_WIDTH)] * 2 + [const(SEQ)] * 2,
        out_specs=pl.BlockSpec((SEQ, D_MODEL), lambda b: (b, 0)),
        out_shape=jax.ShapeDtypeStruct((N_CTX, D_MODEL), BF16),
        compiler_params=_params("arbitrary"),
        name=f"ctx_mixer{l}",
    )(win_sink, f, qw, qn, *kvt, *dft_ch, *dft_seq)


FT_ROWS = 256


def _lat_fourier_kernel(f_ref, bc_ref, bs_ref, cs_ref, ss_ref, o_ref, st_ref):
    b = pl.program_id(1)

    @pl.when(pl.program_id(0) == 0)
    def _():
        f_hi, f_lo = _split(f_ref[pl.ds(pl.multiple_of(b * DEC_SEQ, DEC_SEQ), DEC_SEQ), :])
        fc_hi, fc_lo = _split(_dot3(f_hi, f_lo, *_split(bc_ref[...])))
        fs_hi, fs_lo = _split(_dot3(f_hi, f_lo, *_split(bs_ref[...])))
        st_ref[b, 0] = fc_hi
        st_ref[b, 1] = fc_lo
        st_ref[b, 2] = fs_hi
        st_ref[b, 3] = fs_lo

    z = (_dot3(*_split(cs_ref[...]), st_ref[b, 0], st_ref[b, 1])
         - _dot3(*_split(ss_ref[...]), st_ref[b, 2], st_ref[b, 3]))
    o_ref[...] = (z * (DEC_SEQ * HEAD_DIM) ** -0.5).astype(BF16)


def _lat_fourier(f, dft_ch, dft_seq):
    nrb = DEC_SEQ // FT_ROWS
    const = pl.BlockSpec((F_WIDTH, F_WIDTH), lambda r, b: (0, 0))
    rows = pl.BlockSpec((FT_ROWS, DEC_SEQ), lambda r, b: (r, 0))
    return pl.pallas_call(
        _lat_fourier_kernel,
        grid=(nrb, DEC_BATCH),
        in_specs=[pl.BlockSpec((N_LAT, F_WIDTH), lambda r, b: (1, 0))] + [const] * 2 + [rows] * 2,
        out_specs=pl.BlockSpec((FT_ROWS, F_WIDTH), lambda r, b: (b * nrb + r, 0)),
        out_shape=jax.ShapeDtypeStruct((N_LAT, F_WIDTH), BF16),
        scratch_shapes=[pltpu.VMEM((DEC_BATCH, 4, DEC_SEQ, F_WIDTH), BF16)],
        compiler_params=_params("arbitrary", "arbitrary"),
        name="lat_fourier",
    )(f, *dft_ch, *dft_seq)


def _rope(x, cos, sin_signed):
    n = x.shape[-1]
    lane = lax.broadcasted_iota(jnp.int32, x.shape, x.ndim - 1)
    first = (lane % 32) < 16
    partner = jnp.where(first, pltpu.roll(x, n - 16, axis=x.ndim - 1), pltpu.roll(x, 16, axis=x.ndim - 1))
    return x * cos + partner * sin_signed


def _win_kernel(l, sink_ref, q_ref, k_ref, v_ref, ck_ref, cv_ref, cos_ref, sin_ref, cosq_ref, sinq_ref,
                o_ref, kp_ref, vp_ref, cp_ref):
    n = pl.program_id(1)
    nb = DEC_SEQ // WIN_BLOCK
    pad = WIN_BLOCK

    @pl.when(n == 0)
    def _():
        zeros = jnp.zeros((pad, LANES), BF16)
        kr = _rope(k_ref[...], cos_ref[...], sin_ref[...])
        v = v_ref[...]
        for idx, (kk, vv) in enumerate(((kr, v), (_swap_halves(kr), _swap_halves(v)))):
            kp_ref[idx, 0:pad] = zeros
            kp_ref[idx, pad + DEC_SEQ:] = zeros
            kp_ref[idx, pad:pad + DEC_SEQ] = kk.astype(BF16)
            vp_ref[idx, 0:pad] = zeros
            vp_ref[idx, pad + DEC_SEQ:] = zeros
            vp_ref[idx, pad:pad + DEC_SEQ] = vv.astype(BF16)
        ck = ck_ref[...]
        cv = cv_ref[...]
        cp_ref[0] = ck.astype(BF16)
        cp_ref[1] = _swap_halves(ck).astype(BF16)
        cp_ref[2] = cv.astype(BF16)
        cp_ref[3] = _swap_halves(cv).astype(BF16)

    lo = jnp.where(n == 0, WIN_BLOCK, 0)
    hi = jnp.where(n == nb - 1, 2 * WIN_BLOCK, 3 * WIN_BLOCK)

    i = lax.broadcasted_iota(jnp.int32, (WIN_BLOCK, 3 * WIN_BLOCK), 0)
    j = lax.broadcasted_iota(jnp.int32, (WIN_BLOCK, 3 * WIN_BLOCK), 1)
    mask = (j >= i + WIN_BLOCK - WINDOW) & (j <= i + WIN_BLOCK + WINDOW) & (j >= lo) & (j < hi)
    band_bias = jnp.where(mask, 0.0, NEG_INF)

    def band(s):
        heads = s.shape[0] // WIN_BLOCK
        return (s.reshape(heads, WIN_BLOCK, s.shape[1]) + band_bias[None]).reshape(s.shape)

    start = pl.multiple_of(n * WIN_BLOCK, WIN_BLOCK)
    win = pl.ds(start, 3 * WIN_BLOCK)
    q_pairs = [_rope(q_ref[:, LANES * jp:LANES * (jp + 1)], cosq_ref[...], sinq_ref[...]) * ATTN_SCALE
               for jp in range(WIN_HEADS // 2)]
    outs = _gqa_attention(
        q_pairs, WIN_BLOCK,
        lambda c: [(kp_ref[c, win, :], vp_ref[c, win, :], band), (cp_ref[c], cp_ref[2 + c], None)],
        [sink_ref[l, h] for h in range(WIN_HEADS)])
    for jp, o in enumerate(outs):
        o_ref[:, LANES * jp:LANES * (jp + 1)] = o.astype(BF16)


def _lat_window(l, win_sink, qw, kw, vw, cache_k, cache_v, cos_t, sin_t):
    nb = DEC_SEQ // WIN_BLOCK
    kv_spec = pl.BlockSpec((DEC_SEQ, WIN_KV), lambda b, n: (b, 0))
    cache_spec = pl.BlockSpec((None, None, PAST_LEN, WIN_KV), lambda b, n: (b, l, 0, 0))
    tab_all = pl.BlockSpec((DEC_SEQ, LANES), lambda b, n: (0, 0))
    tab_blk = pl.BlockSpec((WIN_BLOCK, LANES), lambda b, n: (n, 0))
    return pl.pallas_call(
        functools.partial(_win_kernel, l),
        grid=(DEC_BATCH, nb),
        in_specs=[pl.BlockSpec(memory_space=pltpu.SMEM),
                  pl.BlockSpec((WIN_BLOCK, WIN_Q), lambda b, n: (N_CTX // WIN_BLOCK + b * nb + n, 0)),
                  kv_spec, kv_spec, cache_spec, cache_spec, tab_all, tab_all, tab_blk, tab_blk],
        out_specs=pl.BlockSpec((WIN_BLOCK, WIN_Q), lambda b, n: (b * nb + n, 0)),
        out_shape=jax.ShapeDtypeStruct((N_LAT, WIN_Q), BF16),
        scratch_shapes=[pltpu.VMEM((2, DEC_SEQ + 2 * WIN_BLOCK, LANES), BF16),
                        pltpu.VMEM((2, DEC_SEQ + 2 * WIN_BLOCK, LANES), BF16),
                        pltpu.VMEM((4, PAST_LEN, LANES), BF16)],
        compiler_params=_params("arbitrary", "arbitrary"),
        name=f"lat_window{l}",
    )(win_sink, qw, kw, vw, cache_k, cache_v, cos_t, sin_t, cos_t, sin_t)


NA_G = 4
NA_Q = NA_G * GRID_W
NA_WIN_ROWS = NA_ROWS + NA_G
NA_BLOCKS = GRID_ROWS // NA_G
NA_VARIANTS = 3


def _na_block_start(g):
    return jnp.clip(NA_G * g - NA_ROWS // 2, 0, GRID_ROWS - NA_WIN_ROWS)


def _na_kernel(q_ref, k_ref, v_ref, ck_ref, cv_ref, bias_ref, o_ref):
    g = pl.program_id(1)
    start = pl.multiple_of(_na_block_start(g) * GRID_W, GRID_W)
    win = pl.ds(start, NA_WIN_ROWS * GRID_W)
    for jp in range(NA_HEADS // 2):
        sl = slice(LANES * jp, LANES * (jp + 1))
        q_stack = _stack_heads({jp: q_ref[:, sl] * ATTN_SCALE}, (2 * jp, 2 * jp + 1))
        bias = bias_ref[jp]
        o = _attend(q_stack, [(k_ref[win, sl].astype(BF16), v_ref[win, sl].astype(BF16), lambda s: s + bias),
                              (ck_ref[:, sl].astype(BF16), cv_ref[:, sl].astype(BF16), None)])
        o_ref[:, sl] = _merge_pair(o[:NA_Q], o[NA_Q:]).astype(BF16)


def _lat_neighbourhood(l, qn, kn, vn, cache_k, cache_v, bias_tab):
    kv_spec = pl.BlockSpec((DEC_SEQ, NA_W), lambda b, g: (b, 0))
    cache_spec = pl.BlockSpec((None, None, PAST_LEN, NA_W), lambda b, g: (b, l, 0, 0))
    variant = lambda g: jnp.where(g == 0, 0, jnp.where(g == NA_BLOCKS - 1, 2, 1))
    return pl.pallas_call(
        _na_kernel,
        grid=(DEC_BATCH, NA_BLOCKS),
        in_specs=[pl.BlockSpec((NA_Q, NA_W), lambda b, g: (N_CTX // NA_Q + b * NA_BLOCKS + g, 0)),
                  kv_spec, kv_spec, cache_spec, cache_spec,
                  pl.BlockSpec((None, None, NA_HEADS // 2, 2 * NA_Q, NA_WIN_ROWS * GRID_W),
                               lambda b, g: (l, variant(g), 0, 0, 0))],
        out_specs=pl.BlockSpec((NA_Q, NA_W), lambda b, g: (b * NA_BLOCKS + g, 0)),
        out_shape=jax.ShapeDtypeStruct((N_LAT, NA_W), BF16),
        compiler_params=_params("arbitrary", "arbitrary"),
        name=f"lat_neighbourhood{l}",
    )(qn, kn, vn, cache_k, cache_v, bias_tab)


OUTPROJ_ROWS = 256


def _outproj_kernel(xc_ref, xl_ref, mc_ref, mf_ref, mw_ref, mn_ref, mod_ref, g_ref, w_ref, r_hi, r_lo,
                    x_ref, h_ref, lg_ref, wb_ref):
    is_ctx = pl.program_id(0) < N_CTX // TM

    @pl.when(pl.program_id(0) == 0)
    def _():
        wb_ref[...] = w_ref[...].astype(BF16)

    d = D_MODEL
    for r0 in range(0, TM, OUTPROJ_ROWS):
        rows = slice(r0, r0 + OUTPROJ_ROWS)
        m_lat = jnp.concatenate([mf_ref[rows, :], mw_ref[rows, :], mn_ref[rows, :]], axis=1)
        mixed = _dot(jnp.where(is_ctx, mc_ref[rows, :], m_lat), wb_ref[...])
        x = jnp.where(is_ctx, xc_ref[rows, :], xl_ref[rows, :]) + mod_ref[:, 2 * d:3 * d] * mixed
        x_ref[rows, :] = x
        h = _rms_mod(x, g_ref[...], mod_ref[:, 3 * d:4 * d], mod_ref[:, 4 * d:5 * d])
        h_ref[rows, :] = h.astype(BF16)
        h_hi, h_lo = _split(h)
        lg_ref[:, rows] = _dot3(h_hi, h_lo, r_hi[...], r_lo[...]).T[0:N_EXPERTS, :]


def _outproj(l, x_ctx, x_lat, mixed_ctx, mixed_f, mixed_w, mixed_n, mod4, g_ffn, w_out, r_hi, r_lo):
    n_ctx_blocks = N_CTX // TM
    xc_spec, xl_spec = _two_stream_specs(D_MODEL)
    lat = lambda w: pl.BlockSpec((TM, w), lambda i: (jnp.maximum(i - n_ctx_blocks, 0), 0))
    whole = lambda shape: pl.BlockSpec(shape, lambda i: (0,) * len(shape))
    row = lambda w: pl.BlockSpec((TM, w), lambda i: (i, 0))
    return pl.pallas_call(
        _outproj_kernel,
        grid=(N_TOK // TM,),
        in_specs=[xc_spec, xl_spec,
                  pl.BlockSpec((TM, D_MODEL), lambda i: (jnp.minimum(i, n_ctx_blocks - 1), 0)),
                  lat(F_WIDTH), lat(WIN_Q), lat(NA_W),
                  pl.BlockSpec((None, None, 1, 6 * D_MODEL), lambda i: (l, _cond_of_block(i), 0, 0)),
                  pl.BlockSpec((None, 1, D_MODEL), lambda i: (l, 0, 0)),
                  pl.BlockSpec((None, D_MODEL, D_MODEL), lambda i: (l, 0, 0)),
                  whole((D_MODEL, LANES)), whole((D_MODEL, LANES))],
        out_specs=[row(D_MODEL), row(D_MODEL), pl.BlockSpec((N_EXPERTS, TM), lambda i: (0, i))],
        out_shape=[jax.ShapeDtypeStruct((N_TOK, D_MODEL), F32),
                   jax.ShapeDtypeStruct((N_TOK, D_MODEL), BF16),
                   jax.ShapeDtypeStruct((N_EXPERTS, N_TOK), F32)],
        scratch_shapes=[pltpu.VMEM((D_MODEL, D_MODEL), BF16)],
        compiler_params=_params("arbitrary"),
        name=f"outproj{l}",
    )(x_ctx, x_lat, mixed_ctx, mixed_f, mixed_w, mixed_n, mod4, g_ffn.reshape(DEPTH, 1, D_MODEL), w_out,
      r_hi, r_lo)


PREFIX_CHUNK = 256
MANTISSA_STEPS = 44


def _prefix_exclusive(m):
    rows, n = m.shape
    t0 = lax.broadcasted_iota(jnp.int32, (PREFIX_CHUNK, PREFIX_CHUNK), 0)
    t1 = lax.broadcasted_iota(jnp.int32, (PREFIX_CHUNK, PREFIX_CHUNK), 1)
    upper = jnp.where(t0 < t1, 1.0, 0.0).astype(BF16)
    carry = jnp.zeros((rows, 1), F32)
    outs = []
    for c in range(n // PREFIX_CHUNK):
        blk = m[:, c * PREFIX_CHUNK:(c + 1) * PREFIX_CHUNK]
        outs.append(_dot(blk.astype(BF16), upper) + carry)
        carry = carry + blk.sum(axis=-1, keepdims=True)
    return outs[0] if len(outs) == 1 else jnp.concatenate(outs, axis=-1)


def _route_kernel(cap, groups, lg_ref, slot_ref, aff_ref, slott_ref):
    n = lg_ref.shape[1] // groups
    x = jnp.concatenate([lg_ref[:, g * n:(g + 1) * n] for g in range(groups)], axis=0)
    x = x.reshape(groups, N_EXPERTS, n)
    e = jnp.exp(x - x.max(axis=1, keepdims=True))
    aff3 = e / e.sum(axis=1, keepdims=True)
    aff = aff3.reshape(aff3.shape[0] * aff3.shape[1], aff3.shape[2])
    capf = float(cap)

    def count_ge(t):
        return jnp.where(aff >= t, 1.0, 0.0).sum(axis=-1, keepdims=True)

    above = jnp.full((aff.shape[0], 1), 2.0, F32)
    for s in (64, 32, 16, 8, 4, 2, 1):
        cand = above * (2.0 ** -s)
        above = jnp.where(count_ge(cand) >= capf, above, cand)
    base = above * 0.5
    base = jnp.where(count_ge(base) >= capf, base, 0.0)

    def refine(_, carry):
        thr, inc = carry
        cand = thr + inc
        return jnp.where(count_ge(cand) >= capf, cand, thr), inc * 0.5

    thr, _ = lax.fori_loop(0, MANTISSA_STEPS, refine, (base, base * 0.5))
    gt = jnp.where(aff > thr, 1.0, 0.0)
    eq = jnp.where(aff == thr, 1.0, 0.0)
    need = capf - gt.sum(axis=-1, keepdims=True)
    sel = gt + eq * jnp.where(_prefix_exclusive(eq) < need, 1.0, 0.0)
    slot = jnp.where(sel > 0.0, _prefix_exclusive(sel), -1.0)
    slot_ref[...] = slot.astype(jnp.int32)
    aff_ref[...] = aff
    unused = jnp.full((LANES - N_EXPERTS, n), -1.0, F32)
    for g in range(groups):
        tile = jnp.concatenate([slot[g * N_EXPERTS:(g + 1) * N_EXPERTS], unused], axis=0)
        slott_ref[g * n:(g + 1) * n, :] = tile.T.astype(jnp.int32)


def _route(lg_t, stream, groups, cap, name):
    n_tok = N_CTX
    n = n_tok // groups
    rows = groups * N_EXPERTS
    return pl.pallas_call(
        functools.partial(_route_kernel, cap, groups),
        grid=(1,),
        in_specs=[pl.BlockSpec((N_EXPERTS, n_tok), lambda i: (0, stream))],
        out_specs=[pl.BlockSpec((rows, n), lambda i: (0, 0)), pl.BlockSpec((rows, n), lambda i: (0, 0)),
                   pl.BlockSpec((n_tok, LANES), lambda i: (0, 0))],
        out_shape=[jax.ShapeDtypeStruct((rows, n), jnp.int32), jax.ShapeDtypeStruct((rows, n), F32),
                   jax.ShapeDtypeStruct((n_tok, LANES), jnp.int32)],
        compiler_params=_params("arbitrary"),
        name=name,
    )(lg_t)


def _gather_ctx_kernel(h_ref, slot_ref, aff_ref, x_ref, g_ref, p_ref):
    s_iota = lax.broadcasted_iota(jnp.int32, (CAP_CTX, SEQ), 0)
    for e in range(N_EXPERTS):
        hit = s_iota == slot_ref[e:e + 1, :]
        p_ref[e * CAP_CTX:(e + 1) * CAP_CTX, :] = jnp.where(hit, 1.0, 0.0).astype(BF16)
        g_ref[e] = jnp.where(hit, aff_ref[e:e + 1, :], 0.0).sum(axis=-1, keepdims=True)
    x = _dot(p_ref[...], h_ref[...]).astype(BF16)
    x_ref[...] = x.reshape(N_EXPERTS, CAP_CTX, D_MODEL)


def _gather_ctx(h, slot, aff):
    return pl.pallas_call(
        _gather_ctx_kernel,
        grid=(BATCH,),
        in_specs=[pl.BlockSpec((SEQ, D_MODEL), lambda b: (b, 0)),
                  pl.BlockSpec((N_EXPERTS, SEQ), lambda b: (b, 0)),
                  pl.BlockSpec((N_EXPERTS, SEQ), lambda b: (b, 0))],
        out_specs=[pl.BlockSpec((N_EXPERTS, CAP_CTX, D_MODEL), lambda b: (0, b, 0)),
                   pl.BlockSpec((N_EXPERTS, CAP_CTX, 1), lambda b: (0, b, 0))],
        out_shape=[jax.ShapeDtypeStruct((N_EXPERTS, ROWS_CTX, D_MODEL), BF16),
                   jax.ShapeDtypeStruct((N_EXPERTS, ROWS_CTX, 1), F32)],
        scratch_shapes=[pltpu.VMEM((N_EXPERTS * CAP_CTX, SEQ), BF16)],
        compiler_params=_params("arbitrary"),
        name="gather_ctx",
    )(h, slot, aff)


def _gather_lat_kernel(h_ref, slot_ref, aff_ref, x_ref, g_ref):
    e = pl.program_id(1)
    s_iota = lax.broadcasted_iota(jnp.int32, (CAP_LAT, DEC_SEQ), 0)
    hit = s_iota == slot_ref[pl.ds(e, 1), :]
    x_ref[...] = _dot(jnp.where(hit, 1.0, 0.0).astype(BF16), h_ref[...]).astype(BF16)
    g_ref[...] = jnp.where(hit, aff_ref[pl.ds(e, 1), :], 0.0).sum(axis=-1, keepdims=True)


def _gather_lat(h, slot, aff):
    lat_blk0 = N_CTX // DEC_SEQ
    return pl.pallas_call(
        _gather_lat_kernel,
        grid=(DEC_BATCH, N_EXPERTS),
        in_specs=[pl.BlockSpec((DEC_SEQ, D_MODEL), lambda b, e: (lat_blk0 + b, 0)),
                  pl.BlockSpec((N_EXPERTS, DEC_SEQ), lambda b, e: (b, 0)),
                  pl.BlockSpec((N_EXPERTS, DEC_SEQ), lambda b, e: (b, 0))],
        out_specs=[pl.BlockSpec((None, CAP_LAT, D_MODEL), lambda b, e: (e, b, 0)),
                   pl.BlockSpec((None, CAP_LAT, 1), lambda b, e: (e, b, 0))],
        out_shape=[jax.ShapeDtypeStruct((N_EXPERTS, ROWS_LAT, D_MODEL), BF16),
                   jax.ShapeDtypeStruct((N_EXPERTS, ROWS_LAT, 1), F32)],
        compiler_params=_params("arbitrary", "arbitrary"),
        name="gather_lat",
    )(h, slot, aff)


N_UP_STEPS = D_FF // TF


def _silu_tanh(x):
    return x * (0.5 + 0.5 * jnp.tanh(0.5 * x))


def _ffn_kernel(xc_ref, xl_ref, gc_ref, gl_ref, wg_ref, wu_ref, wd_ref, y_ref, x_sc, h_sc):
    j = pl.program_id(1)

    @pl.when(j == 0)
    def _():
        x_sc[0:ROWS_CTX, :] = xc_ref[...]
        x_sc[ROWS_CTX:, :] = xl_ref[...]

    @pl.when(j < N_UP_STEPS)
    def _():
        x = x_sc[...]
        a = _dot(x, wg_ref[...].astype(BF16))
        u = _dot(x, wu_ref[...].astype(BF16))
        h_sc[j] = (_silu_tanh(a) * u).astype(BF16)

    @pl.when(j >= N_UP_STEPS)
    def _():
        for c0 in range(0, D_MODEL, TD):
            acc = None
            for k in range(N_UP_STEPS):
                t = _dot(h_sc[k], wd_ref[k * TF:(k + 1) * TF, c0:c0 + TD].astype(BF16))
                acc = t if acc is None else acc + t
            y_ref[0:ROWS_CTX, c0:c0 + TD] = (acc[0:ROWS_CTX] * gc_ref[...]).astype(BF16)
            y_ref[ROWS_CTX:, c0:c0 + TD] = (acc[ROWS_CTX:] * gl_ref[...]).astype(BF16)


def _ffn(l, x_c, x_l, g_c, g_l, w_gate, w_up, w_down):
    rows = ROWS_CTX + ROWS_LAT
    xin = lambda r: pl.BlockSpec((None, r, D_MODEL), lambda e, j: (e, 0, 0))
    gin = lambda r: pl.BlockSpec((None, r, 1), lambda e, j: (e, 0, 0))
    up_chunk = lambda e, j: (l, e, 0, jnp.minimum(j, N_UP_STEPS - 1))
    return pl.pallas_call(
        _ffn_kernel,
        grid=(N_EXPERTS, N_UP_STEPS + 1),
        in_specs=[xin(ROWS_CTX), xin(ROWS_LAT), gin(ROWS_CTX), gin(ROWS_LAT),
                  pl.BlockSpec((None, None, D_MODEL, TF), up_chunk),
                  pl.BlockSpec((None, None, D_MODEL, TF), up_chunk),
                  pl.BlockSpec((None, None, D_FF, D_MODEL), lambda e, j: (l, e, 0, 0))],
        out_specs=pl.BlockSpec((None, rows, D_MODEL), lambda e, j: (e, 0, 0)),
        out_shape=jax.ShapeDtypeStruct((N_EXPERTS, rows, D_MODEL), BF16),
        scratch_shapes=[pltpu.VMEM((rows, D_MODEL), BF16), pltpu.VMEM((N_UP_STEPS, rows, TF), BF16)],
        compiler_params=_params("arbitrary", "arbitrary"),
        name=f"experts{l}",
    )(x_c, x_l, g_c, g_l, w_gate, w_up, w_down)


def _finish(x, res, mod_ref, gf_ref, final):
    y = x + mod_ref[:, 5 * D_MODEL:] * res
    if final:
        y = y * lax.rsqrt(jnp.mean(y * y, axis=-1, keepdims=True) + RMS_EPS) * gf_ref[...]
    return y


def _combine_ctx_kernel(final, x_ref, y_ref, slot_ref, rep_ref, mod_ref, gf_ref, o_ref):
    n_col = N_EXPERTS * CAP_CTX
    spread = _dot(slot_ref[...].astype(F32).astype(BF16), rep_ref[...])
    col = lax.broadcasted_iota(jnp.int32, (SEQ, n_col), 1) % CAP_CTX
    p = jnp.where(spread == col.astype(F32), 1.0, 0.0).astype(BF16)
    res = _dot(p, y_ref[...].reshape(n_col, D_MODEL))
    o_ref[...] = _finish(x_ref[...], res, mod_ref, gf_ref, final)


def _combine_ctx(l, final, x_new, y, slot_t, mod4, g_final):
    n_col = N_EXPERTS * CAP_CTX
    rep = (np.arange(n_col)[None, :] // CAP_CTX == np.arange(LANES)[:, None]).astype(np.float32)
    return pl.pallas_call(
        functools.partial(_combine_ctx_kernel, final),
        grid=(BATCH,),
        in_specs=[pl.BlockSpec((SEQ, D_MODEL), lambda b: (b, 0)),
                  pl.BlockSpec((N_EXPERTS, CAP_CTX, D_MODEL), lambda b: (0, b, 0)),
                  pl.BlockSpec((SEQ, LANES), lambda b: (b, 0)),
                  pl.BlockSpec((LANES, n_col), lambda b: (0, 0)),
                  pl.BlockSpec((None, None, 1, 6 * D_MODEL), lambda b: (l, 0, 0, 0)),
                  pl.BlockSpec((1, D_MODEL), lambda b: (0, 0))],
        out_specs=pl.BlockSpec((SEQ, D_MODEL), lambda b: (b, 0)),
        out_shape=jax.ShapeDtypeStruct((N_CTX, D_MODEL), F32),
        compiler_params=_params("arbitrary"),
        name=f"combine_ctx{l}",
    )(x_new, y, slot_t, jnp.asarray(rep, BF16), mod4, g_final.reshape(1, D_MODEL))


TMC = 512


def _combine_lat_kernel(final, x_ref, y_ref, slot_ref, mod_ref, gf_ref, o_ref):
    s_iota = lax.broadcasted_iota(jnp.int32, (TMC, CAP_LAT), 1)
    slot = slot_ref[...]
    res = None
    for e in range(N_EXPERTS):
        p = jnp.where(slot[:, e:e + 1] == s_iota, 1.0, 0.0).astype(BF16)
        t = _dot(p, y_ref[e])
        res = t if res is None else res + t
    o_ref[...] = _finish(x_ref[...], res, mod_ref, gf_ref, final)


def _combine_lat(l, final, x_new, y, slot_t, mod4, g_final):
    nt = DEC_SEQ // TMC
    return pl.pallas_call(
        functools.partial(_combine_lat_kernel, final),
        grid=(DEC_BATCH, nt),
        in_specs=[pl.BlockSpec((TMC, D_MODEL), lambda b, t: (N_CTX // TMC + b * nt + t, 0)),
                  pl.BlockSpec((N_EXPERTS, CAP_LAT, D_MODEL), lambda b, t: (0, ROWS_CTX // CAP_LAT + b, 0)),
                  pl.BlockSpec((TMC, LANES), lambda b, t: (b * nt + t, 0)),
                  pl.BlockSpec((None, None, 1, 6 * D_MODEL), lambda b, t: (l, 1 + b, 0, 0)),
                  pl.BlockSpec((1, D_MODEL), lambda b, t: (0, 0))],
        out_specs=pl.BlockSpec((TMC, D_MODEL), lambda b, t: (b * nt + t, 0)),
        out_shape=jax.ShapeDtypeStruct((N_LAT, D_MODEL), F32),
        compiler_params=_params("arbitrary", "arbitrary"),
        name=f"combine_lat{l}",
    )(x_new, y, slot_t, mod4, g_final.reshape(1, D_MODEL))


def _split_table(t):
    hi = t.astype(BF16)
    return hi, (t - hi.astype(F32)).astype(BF16)


def _dft_tables(n):
    p = np.arange(n, dtype=np.int64)
    ang = ((p[:, None] * p[None, :]) % n).astype(np.float64) * (2.0 * np.pi / n)
    return np.cos(ang).astype(np.float32), np.sin(ang).astype(np.float32)


def _channel_dft_tables():
    c = np.arange(F_WIDTH, dtype=np.int64)
    same = (c[:, None] // HEAD_DIM) == (c[None, :] // HEAD_DIM)
    ang = (((c[:, None] % HEAD_DIM) * (c[None, :] % HEAD_DIM)) % HEAD_DIM).astype(np.float64) * (2.0 * np.pi / HEAD_DIM)
    return (np.where(same, np.cos(ang), 0.0).astype(np.float32),
            np.where(same, np.sin(ang), 0.0).astype(np.float32))


def _rope_tables():
    half = HEAD_DIM // 2
    nf = half // 2
    pos = jnp.arange(DEC_SEQ)
    inv = 1.0 / (ROPE_BASE ** (jnp.arange(nf, dtype=F32) / nf))
    ang_r = (pos // GRID_W).astype(F32)[:, None] * inv
    ang_c = (pos % GRID_W).astype(F32)[:, None] * inv

    def head(fn, sign):
        return jnp.concatenate([sign * fn(ang_r), fn(ang_r), sign * fn(ang_c), fn(ang_c)], axis=-1)

    cos = head(jnp.cos, 1.0)
    sin = head(jnp.sin, -1.0)
    return jnp.concatenate([cos, cos], axis=-1), jnp.concatenate([sin, sin], axis=-1)


def _na_bias_tables(rpb):
    cq = np.arange(GRID_W)
    rel_c = np.clip(cq[None, :] - cq[:, None] + NA_COLS - 1, 0, 2 * NA_COLS - 2)
    pick = (rel_c[:, :, None] == np.arange(2 * NA_COLS - 1)).astype(np.float32)
    cs = np.clip(cq - NA_COLS // 2, 0, GRID_W - NA_COLS)
    col_ok = (cq[None, :] >= cs[:, None]) & (cq[None, :] < cs[:, None] + NA_COLS)
    bc = jnp.einsum('lhrj,qkj->lhrqk', rpb, pick, precision=lax.Precision.HIGHEST)
    bc = jnp.where(col_ok[None, None, None], bc, NEG_INF)
    outside = jnp.full((DEPTH, NA_HEADS, GRID_W, GRID_W), NEG_INF, F32)
    variants = []
    for g in (0, 1, NA_BLOCKS - 1):
        start = int(np.clip(NA_G * g - NA_ROWS // 2, 0, GRID_ROWS - NA_WIN_ROWS))
        slabs = []
        for a in range(NA_G):
            r = NA_G * g + a
            rs = int(np.clip(r - NA_ROWS // 2, 0, GRID_ROWS - NA_ROWS))
            cols = [bc[:, :, start + w - r + NA_ROWS - 1] if rs <= start + w < rs + NA_ROWS else outside
                    for w in range(NA_WIN_ROWS)]
            slabs.append(jnp.concatenate(cols, axis=-1))
        variants.append(jnp.stack(slabs, axis=2))
    tab = jnp.stack(variants, axis=1)
    return tab.reshape(DEPTH, NA_VARIANTS, NA_HEADS // 2, 2 * NA_Q, NA_WIN_ROWS * GRID_W)


def kernel(x_prompt, x_sample, cache_win_k, cache_win_v, cache_nat_k, cache_nat_v, c, c_ctx, w_mod, b_mod, g_mix, g_ffn, w_in, w_out, win_sink, nat_rpb, w_router, w_gate, w_up, w_down, g_final):
    x_ctx = x_prompt.reshape(N_CTX, D_MODEL)
    x_lat = x_sample.reshape(N_LAT, D_MODEL)
    cond = jnp.concatenate([c_ctx[None, :], c, jnp.zeros((N_COND - 1 - DEC_BATCH, D_MODEL), F32)], axis=0)
    mod4 = _adaln(cond, w_mod, b_mod).reshape(DEPTH, N_COND, 1, 6 * D_MODEL)

    dft_ch = _channel_dft_tables()
    dft_ctx = _dft_tables(SEQ)
    dft_lat = _dft_tables(DEC_SEQ)
    cos_t, sin_t = _rope_tables()
    cwk = cache_win_k.reshape(DEC_BATCH, DEPTH, PAST_LEN, WIN_KV)
    cwv = cache_win_v.reshape(DEC_BATCH, DEPTH, PAST_LEN, WIN_KV)
    cnk = cache_nat_k.reshape(DEC_BATCH, DEPTH, PAST_LEN, NA_W)
    cnv = cache_nat_v.reshape(DEC_BATCH, DEPTH, PAST_LEN, NA_W)
    r_pad = jnp.pad(w_router, ((0, 0), (0, 0), (0, LANES - N_EXPERTS)))
    na_bias = _na_bias_tables(nat_rpb)

    kvt = ()
    for l in range(DEPTH):
        final = l == DEPTH - 1
        (f, qw, qn), (kw, vw, kn, vn), kvt = _project(l, x_ctx, x_lat, mod4, g_mix, w_in, kvt)

        mixed_ctx = _ctx_mixer(l, win_sink, f, qw, qn, kvt, dft_ch, dft_ctx)
        mixed_f = _lat_fourier(f, dft_ch, dft_lat)
        mixed_w = _lat_window(l, win_sink, qw, kw, vw, cwk, cwv, cos_t, sin_t)
        mixed_n = _lat_neighbourhood(l, qn, kn, vn, cnk, cnv, na_bias)

        r_hi, r_lo = _split_table(r_pad[l])
        x_new, h, lg_t = _outproj(l, x_ctx, x_lat, mixed_ctx, mixed_f, mixed_w, mixed_n, mod4, g_ffn,
                                  w_out, r_hi, r_lo)

        slot_c, aff_c, slot_ct = _route(lg_t, 0, BATCH, CAP_CTX, "route_ctx")
        slot_l, aff_l, slot_lt = _route(lg_t, 1, DEC_BATCH, CAP_LAT, "route_lat")
        xg_c, gate_c = _gather_ctx(h, slot_c, aff_c)
        xg_l, gate_l = _gather_lat(h, slot_l, aff_l)
        y = _ffn(l, xg_c, xg_l, gate_c, gate_l, w_gate, w_up, w_down)
        x_ctx = _combine_ctx(l, final, x_new, y, slot_ct, mod4, g_final)
        x_lat = _combine_lat(l, final, x_new, y, slot_lt, mod4, g_final)

    y_prompt = x_ctx.reshape(BATCH, SEQ, D_MODEL)
    y_sample = x_lat.reshape(DEC_BATCH, DEC_SEQ, D_MODEL)
    new_kv = [t.reshape(BATCH, DEPTH, w // HEAD_DIM, HEAD_DIM, SEQ).transpose(0, 1, 4, 2, 3)
              for t, w in zip(kvt, KV_WIDTHS)]
    return (y_prompt, y_sample, *new_kv)
```

```python
import functools

import numpy as np
import jax
import jax.numpy as jnp
from jax import lax
from jax.experimental import pallas as pl
from jax.experimental.pallas import tpu as pltpu

D_MODEL = 1024
BATCH = 16
SEQ = 256
DEPTH = 2
DEC_BATCH = 2
DEC_SEQ = 2048
PAST_LEN = 256
GRID_W = 64
HEAD_DIM = 64
F_WIDTH = 256
WIN_HEADS = 6
WIN_KV_HEADS = 2
WINDOW = 128
WIN_BLOCK = 128
NA_HEADS = 6
NA_ROWS = 8
NA_COLS = 16
N_EXPERTS = 16
EC_CAPACITY = 2
D_FF = 2816
ROPE_BASE = 10000.0
RMS_EPS = 1e-6
NEG_INF = -1e30
ATTN_SCALE = HEAD_DIM ** -0.5
WIN_Q = WIN_HEADS * HEAD_DIM
WIN_KV = WIN_KV_HEADS * HEAD_DIM
NA_W = NA_HEADS * HEAD_DIM
N_IN = F_WIDTH + WIN_Q + 2 * WIN_KV + 3 * NA_W
SPLITS = (0, F_WIDTH, F_WIDTH + WIN_Q, F_WIDTH + WIN_Q + WIN_KV, F_WIDTH + WIN_Q + 2 * WIN_KV,
          F_WIDTH + WIN_Q + 2 * WIN_KV + NA_W, F_WIDTH + WIN_Q + 2 * WIN_KV + 2 * NA_W, N_IN)

N_CTX = BATCH * SEQ
N_LAT = DEC_BATCH * DEC_SEQ
N_TOK = N_CTX + N_LAT
GRID_ROWS = DEC_SEQ // GRID_W
CAP_CTX = EC_CAPACITY * SEQ // N_EXPERTS
CAP_LAT = EC_CAPACITY * DEC_SEQ // N_EXPERTS
ROWS_CTX = BATCH * CAP_CTX
ROWS_LAT = DEC_BATCH * CAP_LAT
N_COND = 8

LANES = 128
MXU_COLS = 256
TM = 512
TN_MOD = 1536
TF = 256
TD = 256
VMEM_LIMIT = 56 * 1024 * 1024

F32 = jnp.float32
BF16 = jnp.bfloat16


def _params(*sem):
    return pltpu.CompilerParams(dimension_semantics=sem, vmem_limit_bytes=VMEM_LIMIT)


def _dot(a, b):
    return jnp.dot(a, b, preferred_element_type=F32)


def _dot_nt(a, b):
    return lax.dot_general(a, b, (((1,), (1,)), ((), ())), preferred_element_type=F32)


def _split(x):
    hi = x.astype(BF16)
    lo = (x - hi.astype(F32)).astype(BF16)
    return hi, lo


def _dot3(a_hi, a_lo, b_hi, b_lo):
    return _dot(a_hi, b_hi) + (_dot(a_lo, b_hi) + _dot(a_hi, b_lo))


def _silu(x):
    return x / (1.0 + jnp.exp(-x))


def _rms_mod(x, g, shift, scale):
    y = x * lax.rsqrt(jnp.mean(x * x, axis=-1, keepdims=True) + RMS_EPS)
    return (y * g) * (1.0 + scale) + shift


def _softmax_parts(parts, sink=None):
    m = parts[0].max(axis=-1, keepdims=True)
    for s in parts[1:]:
        m = jnp.maximum(m, s.max(axis=-1, keepdims=True))
    if sink is not None:
        m = jnp.maximum(m, sink)
    es = [jnp.exp(s - m) for s in parts]
    den = es[0].sum(axis=-1, keepdims=True)
    for e in es[1:]:
        den = den + e.sum(axis=-1, keepdims=True)
    if sink is not None:
        den = den + jnp.exp(sink - m)
    inv = 1.0 / den
    return [e * inv for e in es]


def _cond_of_block(i):
    n_ctx_blocks = N_CTX // TM
    return jnp.where(i < n_ctx_blocks, 0, 1 + (i - n_ctx_blocks) // (DEC_SEQ // TM))


def _two_stream_specs(width):
    n_ctx_blocks = N_CTX // TM
    ctx = pl.BlockSpec((TM, width), lambda i: (jnp.minimum(i, n_ctx_blocks - 1), 0))
    lat = pl.BlockSpec((TM, width), lambda i: (jnp.maximum(i - n_ctx_blocks, 0), 0))
    return ctx, lat


def _pick_stream(ctx_ref, lat_ref):
    return jnp.where(pl.program_id(0) < N_CTX // TM, ctx_ref[...], lat_ref[...])


def _adaln_kernel(c_ref, w_ref, b_ref, o_ref):
    s_hi, s_lo = _split(_silu(c_ref[...]))
    w_hi, w_lo = _split(w_ref[...])
    o_ref[...] = _dot3(s_hi, s_lo, w_hi, w_lo) + b_ref[...]


def _adaln(cond, w_mod, b_mod):
    return pl.pallas_call(
        _adaln_kernel,
        grid=(DEPTH, 6 * D_MODEL // TN_MOD),
        in_specs=[
            pl.BlockSpec((N_COND, D_MODEL), lambda l, j: (0, 0)),
            pl.BlockSpec((None, D_MODEL, TN_MOD), lambda l, j: (l, 0, j)),
            pl.BlockSpec((None, 1, TN_MOD), lambda l, j: (l, 0, j)),
        ],
        out_specs=pl.BlockSpec((None, N_COND, TN_MOD), lambda l, j: (l, 0, j)),
        out_shape=jax.ShapeDtypeStruct((DEPTH, N_COND, 6 * D_MODEL), F32),
        compiler_params=_params("arbitrary", "arbitrary"),
        name="adaln",
    )(cond, w_mod, b_mod.reshape(DEPTH, 1, 6 * D_MODEL))


KV_NAMES = ("kw", "vw", "kn", "vn")
KV_WIDTHS = (WIN_KV, WIN_KV, NA_W, NA_W)
KV_SPLITS = (SPLITS[2], SPLITS[3], SPLITS[5], SPLITS[6])
KV_TOTAL = sum(KV_WIDTHS)
Q_SPLITS = (SPLITS[0], SPLITS[1], SPLITS[4])


def _proj_kernel(n_prev, xc_ref, xl_ref, mod_ref, g_ref, w_ref, *rest):
    rest = rest[n_prev:]
    f_ref, qw_ref, qn_ref = rest[0:3]
    lat_kv = rest[3:7]
    ctx_kvt = rest[7:11]
    wb_ref, wt_ref = rest[11:13]
    i = pl.program_id(0)

    @pl.when(i == 0)
    def _():
        wb_ref[...] = w_ref[...].astype(BF16)
        r0 = 0
        for c0, width in zip(KV_SPLITS, KV_WIDTHS):
            wt_ref[r0:r0 + width, :] = w_ref[:, c0:c0 + width].T.astype(BF16)
            r0 += width

    natural = dict(zip(SPLITS[:-1], (f_ref, qw_ref) + tuple(lat_kv[0:2]) + (qn_ref,) + tuple(lat_kv[2:4])))

    def hidden(x_ref):
        return _rms_mod(x_ref[...], g_ref[...], mod_ref[:, 0:D_MODEL], mod_ref[:, D_MODEL:2 * D_MODEL]).astype(BF16)

    def project_tiles(h, wanted):
        for t0 in range(0, N_IN, MXU_COLS):
            hits = [(a, b) for a, b in zip(SPLITS[:-1], SPLITS[1:])
                    if a in wanted and max(a, t0) < min(b, t0 + MXU_COLS)]
            if not hits:
                continue
            acc = _dot(h, wb_ref[:, t0:t0 + MXU_COLS])
            for a, b in hits:
                lo, hi = max(a, t0), min(b, t0 + MXU_COLS)
                natural[a][:, lo - a:hi - a] = acc[:, lo - t0:hi - t0]

    @pl.when(i < N_CTX // TM)
    def _():
        h = hidden(xc_ref)
        project_tiles(h, Q_SPLITS)
        kvt = _dot_nt(wt_ref[...], h)
        r0 = 0
        for o_ref, width in zip(ctx_kvt, KV_WIDTHS):
            for bb in range(TM // SEQ):
                o_ref[bb] = kvt[r0:r0 + width, bb * SEQ:(bb + 1) * SEQ]
            r0 += width

    @pl.when(i >= N_CTX // TM)
    def _():
        project_tiles(hidden(xl_ref), Q_SPLITS + KV_SPLITS)


def _project(l, x_ctx, x_lat, mod4, g_mix, w_in, prev_kvt):
    n_ctx_blocks = N_CTX // TM
    xc_spec, xl_spec = _two_stream_specs(D_MODEL)
    both = lambda w: pl.BlockSpec((TM, w), lambda i: (i, 0))
    lat = lambda w: pl.BlockSpec((TM, w), lambda i: (jnp.maximum(i - n_ctx_blocks, 0), 0))
    ctx_t = lambda w: pl.BlockSpec((TM // SEQ, None, w, SEQ),
                                   lambda i: (jnp.minimum(i, n_ctx_blocks - 1), l, 0, 0))
    n_prev = len(prev_kvt)
    n_in = 5
    outs = pl.pallas_call(
        functools.partial(_proj_kernel, n_prev),
        grid=(N_TOK // TM,),
        in_specs=[
            xc_spec, xl_spec,
            pl.BlockSpec((None, None, 1, 6 * D_MODEL), lambda i: (l, _cond_of_block(i), 0, 0)),
            pl.BlockSpec((None, 1, D_MODEL), lambda i: (l, 0, 0)),
            pl.BlockSpec((None, D_MODEL, N_IN), lambda i: (l, 0, 0)),
        ] + [pl.BlockSpec(memory_space=pl.ANY)] * n_prev,
        out_specs=[both(F_WIDTH), both(WIN_Q), both(NA_W)] + [lat(w) for w in KV_WIDTHS]
                  + [ctx_t(w) for w in KV_WIDTHS],
        out_shape=[jax.ShapeDtypeStruct((N_TOK, w), F32) for w in (F_WIDTH, WIN_Q, NA_W)]
                  + [jax.ShapeDtypeStruct((N_LAT, w), F32) for w in KV_WIDTHS]
                  + [jax.ShapeDtypeStruct((BATCH, DEPTH, w, SEQ), F32) for w in KV_WIDTHS],
        input_output_aliases={n_in + k: 7 + k for k in range(n_prev)},
        scratch_shapes=[pltpu.VMEM((D_MODEL, N_IN), BF16), pltpu.VMEM((KV_TOTAL, D_MODEL), BF16)],
        compiler_params=_params("arbitrary"),
        name=f"project{l}",
    )(x_ctx, x_lat, mod4, g_mix.reshape(DEPTH, 1, D_MODEL), w_in, *prev_kvt)
    return outs[0:3], outs[3:7], outs[7:11]


def _lane_is_low(shape):
    return lax.broadcasted_iota(jnp.int32, shape, len(shape) - 1) < HEAD_DIM


def _swap_halves(x):
    return pltpu.roll(x, HEAD_DIM, axis=x.ndim - 1)


def _win_kv_copy(h):
    return 0 if (h // (WIN_HEADS // WIN_KV_HEADS)) == (h % 2) else 1


def _stack_heads(q_pairs, heads):
    low = _lane_is_low(q_pairs[heads[0] // 2].shape)
    rows = [jnp.where(low if h % 2 == 0 else jnp.logical_not(low), q_pairs[h // 2], 0.0).astype(BF16)
            for h in heads]
    return rows[0] if len(rows) == 1 else jnp.concatenate(rows, axis=0)


def _per_head_column(values, rows_per_head):
    blk = lax.broadcasted_iota(jnp.int32, (len(values) * rows_per_head, 1), 0) // rows_per_head
    col = jnp.full(blk.shape, values[0], F32)
    for i in range(1, len(values)):
        col = jnp.where(blk == i, values[i], col)
    return col


def _attend(q_stack, kv_list, extra_logit=None, transposed=False):
    scores = []
    for k, _, post in kv_list:
        s = _dot(q_stack, k) if transposed else _dot_nt(q_stack, k)
        scores.append(post(s) if post is not None else s)
    probs = _softmax_parts(scores, extra_logit)
    o = None
    for p, (_, v, _) in zip(probs, kv_list):
        t = _dot_nt(p.astype(BF16), v) if transposed else _dot(p.astype(BF16), v)
        o = t if o is None else o + t
    return o


def _merge_pair(o_even, o_odd):
    return jnp.where(_lane_is_low(o_even.shape), o_even, o_odd)


def _gqa_attention(q_pairs, rows, kv_for_copy, sinks, transposed=False):
    per_head = {}
    for copy in (0, 1):
        heads = [h for h in range(WIN_HEADS) if _win_kv_copy(h) == copy]
        o = _attend(_stack_heads(q_pairs, heads), kv_for_copy(copy),
                    _per_head_column([sinks[h] for h in heads], rows), transposed)
        for i, h in enumerate(heads):
            per_head[h] = o[i * rows:(i + 1) * rows]
    return [_merge_pair(per_head[2 * j], per_head[2 * j + 1]) for j in range(WIN_HEADS // 2)]


def _swap_row_halves(x):
    return jnp.concatenate([x[HEAD_DIM:], x[:HEAD_DIM]], axis=0)


def _ctx_mixer_kernel(l, sink_ref, f_ref, qw_ref, qn_ref, kw_ref, vw_ref, kn_ref, vn_ref,
                      bc_ref, bs_ref, cs_ref, ss_ref, o_ref):
    f_hi, f_lo = _split(f_ref[...])
    fc_hi, fc_lo = _split(_dot3(f_hi, f_lo, *_split(bc_ref[...])))
    fs_hi, fs_lo = _split(_dot3(f_hi, f_lo, *_split(bs_ref[...])))
    z = _dot3(*_split(cs_ref[...]), fc_hi, fc_lo) - _dot3(*_split(ss_ref[...]), fs_hi, fs_lo)
    o_ref[:, 0:F_WIDTH] = (z * (SEQ * HEAD_DIM) ** -0.5).astype(BF16)

    kv = [(kw_ref[...].astype(BF16), vw_ref[...].astype(BF16)),
          (_swap_row_halves(kw_ref[...]).astype(BF16), _swap_row_halves(vw_ref[...]).astype(BF16))]
    q_pairs = [qw_ref[:, LANES * j:LANES * (j + 1)] * ATTN_SCALE for j in range(WIN_HEADS // 2)]
    outs = _gqa_attention(q_pairs, SEQ, lambda c: [(kv[c][0], kv[c][1], None)],
                          [sink_ref[l, h] for h in range(WIN_HEADS)], transposed=True)
    for j, o in enumerate(outs):
        o_ref[:, F_WIDTH + LANES * j:F_WIDTH + LANES * (j + 1)] = o.astype(BF16)

    for j in range(NA_HEADS // 2):
        sl = slice(LANES * j, LANES * (j + 1))
        q_pairs = {j: qn_ref[:, sl] * ATTN_SCALE}
        o = _attend(_stack_heads(q_pairs, (2 * j, 2 * j + 1)),
                    [(kn_ref[sl, :].astype(BF16), vn_ref[sl, :].astype(BF16), None)], transposed=True)
        base = F_WIDTH + WIN_Q + LANES * j
        o_ref[:, base:base + LANES] = _merge_pair(o[:SEQ], o[SEQ:]).astype(BF16)


def _ctx_mixer(l, win_sink, f, qw, qn, kvt, dft_ch, dft_seq):
    row = lambda w: pl.BlockSpec((SEQ, w), lambda b: (b, 0))
    col = lambda w: pl.BlockSpec((None, None, w, SEQ), lambda b: (b, l, 0, 0))
    const = lambda n: pl.BlockSpec((n, n), lambda b: (0, 0))
    return pl.pallas_call(
        functools.partial(_ctx_mixer_kernel, l),
        grid=(BATCH,),
        in_specs=[pl.BlockSpec(memory_space=pltpu.SMEM),
                  row(F_WIDTH), row(WIN_Q), row(NA_W)] + [col(w) for w in KV_WIDTHS]
                 + [const(F_WIDTH)] * 2 + [const(SEQ)] * 2,
        out_specs=pl.BlockSpec((SEQ, D_MODEL), lambda b: (b, 0)),
        out_shape=jax.ShapeDtypeStruct((N_CTX, D_MODEL), BF16),
        compiler_params=_params("arbitrary"),
        name=f"ctx_mixer{l}",
    )(win_sink, f, qw, qn, *kvt, *dft_ch, *dft_seq)


FT_ROWS = 256


def _lat_fourier_kernel(f_ref, bc_ref, bs_ref, cs_ref, ss_ref, o_ref, st_ref):
    b = pl.program_id(1)

    @pl.when(pl.program_id(0) == 0)
    def _():
        f_hi, f_lo = _split(f_ref[pl.ds(pl.multiple_of(b * DEC_SEQ, DEC_SEQ), DEC_SEQ), :])
        fc_hi, fc_lo = _split(_dot3(f_hi, f_lo, *_split(bc_ref[...])))
        fs_hi, fs_lo = _split(_dot3(f_hi, f_lo, *_split(bs_ref[...])))
        st_ref[b, 0] = fc_hi
        st_ref[b, 1] = fc_lo
        st_ref[b, 2] = fs_hi
        st_ref[b, 3] = fs_lo

    z = (_dot3(*_split(cs_ref[...]), st_ref[b, 0], st_ref[b, 1])
         - _dot3(*_split(ss_ref[...]), st_ref[b, 2], st_ref[b, 3]))
    o_ref[...] = (z * (DEC_SEQ * HEAD_DIM) ** -0.5).astype(BF16)


def _lat_fourier(f, dft_ch, dft_seq):
    nrb = DEC_SEQ // FT_ROWS
    const = pl.BlockSpec((F_WIDTH, F_WIDTH), lambda r, b: (0, 0))
    rows = pl.BlockSpec((FT_ROWS, DEC_SEQ), lambda r, b: (r, 0))
    return pl.pallas_call(
        _lat_fourier_kernel,
        grid=(nrb, DEC_BATCH),
        in_specs=[pl.BlockSpec((N_LAT, F_WIDTH), lambda r, b: (1, 0))] + [const] * 2 + [rows] * 2,
        out_specs=pl.BlockSpec((FT_ROWS, F_WIDTH), lambda r, b: (b * nrb + r, 0)),
        out_shape=jax.ShapeDtypeStruct((N_LAT, F_WIDTH), BF16),
        scratch_shapes=[pltpu.VMEM((DEC_BATCH, 4, DEC_SEQ, F_WIDTH), BF16)],
        compiler_params=_params("arbitrary", "arbitrary"),
        name="lat_fourier",
    )(f, *dft_ch, *dft_seq)


def _rope(x, cos, sin_signed):
    n = x.shape[-1]
    lane = lax.broadcasted_iota(jnp.int32, x.shape, x.ndim - 1)
    first = (lane % 32) < 16
    partner = jnp.where(first, pltpu.roll(x, n - 16, axis=x.ndim - 1), pltpu.roll(x, 16, axis=x.ndim - 1))
    return x * cos + partner * sin_signed


def _win_kernel(l, sink_ref, q_ref, k_ref, v_ref, ck_ref, cv_ref, cos_ref, sin_ref, cosq_ref, sinq_ref,
                o_ref, kp_ref, vp_ref, cp_ref):
    n = pl.program_id(1)
    nb = DEC_SEQ // WIN_BLOCK
    pad = WIN_BLOCK

    @pl.when(n == 0)
    def _():
        zeros = jnp.zeros((pad, LANES), BF16)
        kr = _rope(k_ref[...], cos_ref[...], sin_ref[...])
        v = v_ref[...]
        for idx, (kk, vv) in enumerate(((kr, v), (_swap_halves(kr), _swap_halves(v)))):
            kp_ref[idx, 0:pad] = zeros
            kp_ref[idx, pad + DEC_SEQ:] = zeros
            kp_ref[idx, pad:pad + DEC_SEQ] = kk.astype(BF16)
            vp_ref[idx, 0:pad] = zeros
            vp_ref[idx, pad + DEC_SEQ:] = zeros
            vp_ref[idx, pad:pad + DEC_SEQ] = vv.astype(BF16)
        ck = ck_ref[...]
        cv = cv_ref[...]
        cp_ref[0] = ck.astype(BF16)
        cp_ref[1] = _swap_halves(ck).astype(BF16)
        cp_ref[2] = cv.astype(BF16)
        cp_ref[3] = _swap_halves(cv).astype(BF16)

    lo = jnp.where(n == 0, WIN_BLOCK, 0)
    hi = jnp.where(n == nb - 1, 2 * WIN_BLOCK, 3 * WIN_BLOCK)

    i = lax.broadcasted_iota(jnp.int32, (WIN_BLOCK, 3 * WIN_BLOCK), 0)
    j = lax.broadcasted_iota(jnp.int32, (WIN_BLOCK, 3 * WIN_BLOCK), 1)
    mask = (j >= i + WIN_BLOCK - WINDOW) & (j <= i + WIN_BLOCK + WINDOW) & (j >= lo) & (j < hi)
    band_bias = jnp.where(mask, 0.0, NEG_INF)

    def band(s):
        heads = s.shape[0] // WIN_BLOCK
        return (s.reshape(heads, WIN_BLOCK, s.shape[1]) + band_bias[None]).reshape(s.shape)

    start = pl.multiple_of(n * WIN_BLOCK, WIN_BLOCK)
    win = pl.ds(start, 3 * WIN_BLOCK)
    q_pairs = [_rope(q_ref[:, LANES * jp:LANES * (jp + 1)], cosq_ref[...], sinq_ref[...]) * ATTN_SCALE
               for jp in range(WIN_HEADS // 2)]
    outs = _gqa_attention(
        q_pairs, WIN_BLOCK,
        lambda c: [(kp_ref[c, win, :], vp_ref[c, win, :], band), (cp_ref[c], cp_ref[2 + c], None)],
        [sink_ref[l, h] for h in range(WIN_HEADS)])
    for jp, o in enumerate(outs):
        o_ref[:, LANES * jp:LANES * (jp + 1)] = o.astype(BF16)


def _lat_window(l, win_sink, qw, kw, vw, cache_k, cache_v, cos_t, sin_t):
    nb = DEC_SEQ // WIN_BLOCK
    kv_spec = pl.BlockSpec((DEC_SEQ, WIN_KV), lambda b, n: (b, 0))
    cache_spec = pl.BlockSpec((None, None, PAST_LEN, WIN_KV), lambda b, n: (b, l, 0, 0))
    tab_all = pl.BlockSpec((DEC_SEQ, LANES), lambda b, n: (0, 0))
    tab_blk = pl.BlockSpec((WIN_BLOCK, LANES), lambda b, n: (n, 0))
    return pl.pallas_call(
        functools.partial(_win_kernel, l),
        grid=(DEC_BATCH, nb),
        in_specs=[pl.BlockSpec(memory_space=pltpu.SMEM),
                  pl.BlockSpec((WIN_BLOCK, WIN_Q), lambda b, n: (N_CTX // WIN_BLOCK + b * nb + n, 0)),
                  kv_spec, kv_spec, cache_spec, cache_spec, tab_all, tab_all, tab_blk, tab_blk],
        out_specs=pl.BlockSpec((WIN_BLOCK, WIN_Q), lambda b, n: (b * nb + n, 0)),
        out_shape=jax.ShapeDtypeStruct((N_LAT, WIN_Q), BF16),
        scratch_shapes=[pltpu.VMEM((2, DEC_SEQ + 2 * WIN_BLOCK, LANES), BF16),
                        pltpu.VMEM((2, DEC_SEQ + 2 * WIN_BLOCK, LANES), BF16),
                        pltpu.VMEM((4, PAST_LEN, LANES), BF16)],
        compiler_params=_params("arbitrary", "arbitrary"),
        name=f"lat_window{l}",
    )(win_sink, qw, kw, vw, cache_k, cache_v, cos_t, sin_t, cos_t, sin_t)


NA_G = 4
NA_Q = NA_G * GRID_W
NA_WIN_ROWS = NA_ROWS + NA_G
NA_BLOCKS = GRID_ROWS // NA_G
NA_VARIANTS = 3


def _na_block_start(g):
    return jnp.clip(NA_G * g - NA_ROWS // 2, 0, GRID_ROWS - NA_WIN_ROWS)


def _na_kernel(q_ref, k_ref, v_ref, ck_ref, cv_ref, bias_ref, o_ref):
    g = pl.program_id(1)
    start = pl.multiple_of(_na_block_start(g) * GRID_W, GRID_W)
    win = pl.ds(start, NA_WIN_ROWS * GRID_W)
    for jp in range(NA_HEADS // 2):
        sl = slice(LANES * jp, LANES * (jp + 1))
        q_stack = _stack_heads({jp: q_ref[:, sl] * ATTN_SCALE}, (2 * jp, 2 * jp + 1))
        bias = bias_ref[jp]
        o = _attend(q_stack, [(k_ref[win, sl].astype(BF16), v_ref[win, sl].astype(BF16), lambda s: s + bias),
                              (ck_ref[:, sl].astype(BF16), cv_ref[:, sl].astype(BF16), None)])
        o_ref[:, sl] = _merge_pair(o[:NA_Q], o[NA_Q:]).astype(BF16)


def _lat_neighbourhood(l, qn, kn, vn, cache_k, cache_v, bias_tab):
    kv_spec = pl.BlockSpec((DEC_SEQ, NA_W), lambda b, g: (b, 0))
    cache_spec = pl.BlockSpec((None, None, PAST_LEN, NA_W), lambda b, g: (b, l, 0, 0))
    variant = lambda g: jnp.where(g == 0, 0, jnp.where(g == NA_BLOCKS - 1, 2, 1))
    return pl.pallas_call(
        _na_kernel,
        grid=(DEC_BATCH, NA_BLOCKS),
        in_specs=[pl.BlockSpec((NA_Q, NA_W), lambda b, g: (N_CTX // NA_Q + b * NA_BLOCKS + g, 0)),
                  kv_spec, kv_spec, cache_spec, cache_spec,
                  pl.BlockSpec((None, None, NA_HEADS // 2, 2 * NA_Q, NA_WIN_ROWS * GRID_W),
                               lambda b, g: (l, variant(g), 0, 0, 0))],
        out_specs=pl.BlockSpec((NA_Q, NA_W), lambda b, g: (b * NA_BLOCKS + g, 0)),
        out_shape=jax.ShapeDtypeStruct((N_LAT, NA_W), BF16),
        compiler_params=_params("arbitrary", "arbitrary"),
        name=f"lat_neighbourhood{l}",
    )(qn, kn, vn, cache_k, cache_v, bias_tab)


OUTPROJ_ROWS = 256


def _outproj_kernel(xc_ref, xl_ref, mc_ref, mf_ref, mw_ref, mn_ref, mod_ref, g_ref, w_ref, r_hi, r_lo,
                    x_ref, h_ref, lg_ref, wb_ref):
    is_ctx = pl.program_id(0) < N_CTX // TM

    @pl.when(pl.program_id(0) == 0)
    def _():
        wb_ref[...] = w_ref[...].astype(BF16)

    d = D_MODEL
    for r0 in range(0, TM, OUTPROJ_ROWS):
        rows = slice(r0, r0 + OUTPROJ_ROWS)
        m_lat = jnp.concatenate([mf_ref[rows, :], mw_ref[rows, :], mn_ref[rows, :]], axis=1)
        mixed = _dot(jnp.where(is_ctx, mc_ref[rows, :], m_lat), wb_ref[...])
        x = jnp.where(is_ctx, xc_ref[rows, :], xl_ref[rows, :]) + mod_ref[:, 2 * d:3 * d] * mixed
        x_ref[rows, :] = x
        h = _rms_mod(x, g_ref[...], mod_ref[:, 3 * d:4 * d], mod_ref[:, 4 * d:5 * d])
        h_ref[rows, :] = h.astype(BF16)
        h_hi, h_lo = _split(h)
        lg_ref[:, rows] = _dot3(h_hi, h_lo, r_hi[...], r_lo[...]).T[0:N_EXPERTS, :]


def _outproj(l, x_ctx, x_lat, mixed_ctx, mixed_f, mixed_w, mixed_n, mod4, g_ffn, w_out, r_hi, r_lo):
    n_ctx_blocks = N_CTX // TM
    xc_spec, xl_spec = _two_stream_specs(D_MODEL)
    lat = lambda w: pl.BlockSpec((TM, w), lambda i: (jnp.maximum(i - n_ctx_blocks, 0), 0))
    whole = lambda shape: pl.BlockSpec(shape, lambda i: (0,) * len(shape))
    row = lambda w: pl.BlockSpec((TM, w), lambda i: (i, 0))
    return pl.pallas_call(
        _outproj_kernel,
        grid=(N_TOK // TM,),
        in_specs=[xc_spec, xl_spec,
                  pl.BlockSpec((TM, D_MODEL), lambda i: (jnp.minimum(i, n_ctx_blocks - 1), 0)),
                  lat(F_WIDTH), lat(WIN_Q), lat(NA_W),
                  pl.BlockSpec((None, None, 1, 6 * D_MODEL), lambda i: (l, _cond_of_block(i), 0, 0)),
                  pl.BlockSpec((None, 1, D_MODEL), lambda i: (l, 0, 0)),
                  pl.BlockSpec((None, D_MODEL, D_MODEL), lambda i: (l, 0, 0)),
                  whole((D_MODEL, LANES)), whole((D_MODEL, LANES))],
        out_specs=[row(D_MODEL), row(D_MODEL), pl.BlockSpec((N_EXPERTS, TM), lambda i: (0, i))],
        out_shape=[jax.ShapeDtypeStruct((N_TOK, D_MODEL), F32),
                   jax.ShapeDtypeStruct((N_TOK, D_MODEL), BF16),
                   jax.ShapeDtypeStruct((N_EXPERTS, N_TOK), F32)],
        scratch_shapes=[pltpu.VMEM((D_MODEL, D_MODEL), BF16)],
        compiler_params=_params("arbitrary"),
        name=f"outproj{l}",
    )(x_ctx, x_lat, mixed_ctx, mixed_f, mixed_w, mixed_n, mod4, g_ffn.reshape(DEPTH, 1, D_MODEL), w_out,
      r_hi, r_lo)


PREFIX_CHUNK = 256
MANTISSA_STEPS = 44


def _prefix_exclusive(m):
    rows, n = m.shape
    t0 = lax.broadcasted_iota(jnp.int32, (PREFIX_CHUNK, PREFIX_CHUNK), 0)
    t1 = lax.broadcasted_iota(jnp.int32, (PREFIX_CHUNK, PREFIX_CHUNK), 1)
    upper = jnp.where(t0 < t1, 1.0, 0.0).astype(BF16)
    carry = jnp.zeros((rows, 1), F32)
    outs = []
    for c in range(n // PREFIX_CHUNK):
        blk = m[:, c * PREFIX_CHUNK:(c + 1) * PREFIX_CHUNK]
        outs.append(_dot(blk.astype(BF16), upper) + carry)
        carry = carry + blk.sum(axis=-1, keepdims=True)
    return outs[0] if len(outs) == 1 else jnp.concatenate(outs, axis=-1)


def _route_kernel(cap, groups, lg_ref, slot_ref, aff_ref, slott_ref):
    n = lg_ref.shape[1] // groups
    x = jnp.concatenate([lg_ref[:, g * n:(g + 1) * n] for g in range(groups)], axis=0)
    x = x.reshape(groups, N_EXPERTS, n)
    e = jnp.exp(x - x.max(axis=1, keepdims=True))
    aff3 = e / e.sum(axis=1, keepdims=True)
    aff = aff3.reshape(aff3.shape[0] * aff3.shape[1], aff3.shape[2])
    capf = float(cap)

    def count_ge(t):
        return jnp.where(aff >= t, 1.0, 0.0).sum(axis=-1, keepdims=True)

    above = jnp.full((aff.shape[0], 1), 2.0, F32)
    for s in (64, 32, 16, 8, 4, 2, 1):
        cand = above * (2.0 ** -s)
        above = jnp.where(count_ge(cand) >= capf, above, cand)
    base = above * 0.5
    base = jnp.where(count_ge(base) >= capf, base, 0.0)

    def refine(_, carry):
        thr, inc = carry
        cand = thr + inc
        return jnp.where(count_ge(cand) >= capf, cand, thr), inc * 0.5

    thr, _ = lax.fori_loop(0, MANTISSA_STEPS, refine, (base, base * 0.5))
    gt = jnp.where(aff > thr, 1.0, 0.0)
    eq = jnp.where(aff == thr, 1.0, 0.0)
    need = capf - gt.sum(axis=-1, keepdims=True)
    sel = gt + eq * jnp.where(_prefix_exclusive(eq) < need, 1.0, 0.0)
    slot = jnp.where(sel > 0.0, _prefix_exclusive(sel), -1.0)
    slot_ref[...] = slot.astype(jnp.int32)
    aff_ref[...] = aff
    unused = jnp.full((LANES - N_EXPERTS, n), -1.0, F32)
    for g in range(groups):
        tile = jnp.concatenate([slot[g * N_EXPERTS:(g + 1) * N_EXPERTS], unused], axis=0)
        slott_ref[g * n:(g + 1) * n, :] = tile.T.astype(jnp.int32)


def _route(lg_t, stream, groups, cap, name):
    n_tok = N_CTX
    n = n_tok // groups
    rows = groups * N_EXPERTS
    return pl.pallas_call(
        functools.partial(_route_kernel, cap, groups),
        grid=(1,),
        in_specs=[pl.BlockSpec((N_EXPERTS, n_tok), lambda i: (0, stream))],
        out_specs=[pl.BlockSpec((rows, n), lambda i: (0, 0)), pl.BlockSpec((rows, n), lambda i: (0, 0)),
                   pl.BlockSpec((n_tok, LANES), lambda i: (0, 0))],
        out_shape=[jax.ShapeDtypeStruct((rows, n), jnp.int32), jax.ShapeDtypeStruct((rows, n), F32),
                   jax.ShapeDtypeStruct((n_tok, LANES), jnp.int32)],
        compiler_params=_params("arbitrary"),
        name=name,
    )(lg_t)


def _gather_ctx_kernel(h_ref, slot_ref, aff_ref, x_ref, g_ref, p_ref):
    s_iota = lax.broadcasted_iota(jnp.int32, (CAP_CTX, SEQ), 0)
    for e in range(N_EXPERTS):
        hit = s_iota == slot_ref[e:e + 1, :]
        p_ref[e * CAP_CTX:(e + 1) * CAP_CTX, :] = jnp.where(hit, 1.0, 0.0).astype(BF16)
        g_ref[e] = jnp.where(hit, aff_ref[e:e + 1, :], 0.0).sum(axis=-1, keepdims=True)
    x = _dot(p_ref[...], h_ref[...]).astype(BF16)
    x_ref[...] = x.reshape(N_EXPERTS, CAP_CTX, D_MODEL)


def _gather_ctx(h, slot, aff):
    return pl.pallas_call(
        _gather_ctx_kernel,
        grid=(BATCH,),
        in_specs=[pl.BlockSpec((SEQ, D_MODEL), lambda b: (b, 0)),
                  pl.BlockSpec((N_EXPERTS, SEQ), lambda b: (b, 0)),
                  pl.BlockSpec((N_EXPERTS, SEQ), lambda b: (b, 0))],
        out_specs=[pl.BlockSpec((N_EXPERTS, CAP_CTX, D_MODEL), lambda b: (0, b, 0)),
                   pl.BlockSpec((N_EXPERTS, CAP_CTX, 1), lambda b: (0, b, 0))],
        out_shape=[jax.ShapeDtypeStruct((N_EXPERTS, ROWS_CTX, D_MODEL), BF16),
                   jax.ShapeDtypeStruct((N_EXPERTS, ROWS_CTX, 1), F32)],
        scratch_shapes=[pltpu.VMEM((N_EXPERTS * CAP_CTX, SEQ), BF16)],
        compiler_params=_params("arbitrary"),
        name="gather_ctx",
    )(h, slot, aff)


def _gather_lat_kernel(h_ref, slot_ref, aff_ref, x_ref, g_ref):
    e = pl.program_id(1)
    s_iota = lax.broadcasted_iota(jnp.int32, (CAP_LAT, DEC_SEQ), 0)
    hit = s_iota == slot_ref[pl.ds(e, 1), :]
    x_ref[...] = _dot(jnp.where(hit, 1.0, 0.0).astype(BF16), h_ref[...]).astype(BF16)
    g_ref[...] = jnp.where(hit, aff_ref[pl.ds(e, 1), :], 0.0).sum(axis=-1, keepdims=True)


def _gather_lat(h, slot, aff):
    lat_blk0 = N_CTX // DEC_SEQ
    return pl.pallas_call(
        _gather_lat_kernel,
        grid=(DEC_BATCH, N_EXPERTS),
        in_specs=[pl.BlockSpec((DEC_SEQ, D_MODEL), lambda b, e: (lat_blk0 + b, 0)),
                  pl.BlockSpec((N_EXPERTS, DEC_SEQ), lambda b, e: (b, 0)),
                  pl.BlockSpec((N_EXPERTS, DEC_SEQ), lambda b, e: (b, 0))],
        out_specs=[pl.BlockSpec((None, CAP_LAT, D_MODEL), lambda b, e: (e, b, 0)),
                   pl.BlockSpec((None, CAP_LAT, 1), lambda b, e: (e, b, 0))],
        out_shape=[jax.ShapeDtypeStruct((N_EXPERTS, ROWS_LAT, D_MODEL), BF16),
                   jax.ShapeDtypeStruct((N_EXPERTS, ROWS_LAT, 1), F32)],
        compiler_params=_params("arbitrary", "arbitrary"),
        name="gather_lat",
    )(h, slot, aff)


N_UP_STEPS = D_FF // TF


def _silu_tanh(x):
    return x * (0.5 + 0.5 * jnp.tanh(0.5 * x))


def _ffn_kernel(xc_ref, xl_ref, gc_ref, gl_ref, wg_ref, wu_ref, wd_ref, y_ref, x_sc, h_sc):
    j = pl.program_id(1)

    @pl.when(j == 0)
    def _():
        x_sc[0:ROWS_CTX, :] = xc_ref[...]
        x_sc[ROWS_CTX:, :] = xl_ref[...]

    @pl.when(j < N_UP_STEPS)
    def _():
        x = x_sc[...]
        a = _dot(x, wg_ref[...].astype(BF16))
        u = _dot(x, wu_ref[...].astype(BF16))
        h_sc[j] = (_silu_tanh(a) * u).astype(BF16)

    @pl.when(j >= N_UP_STEPS)
    def _():
        for c0 in range(0, D_MODEL, TD):
            acc = None
            for k in range(N_UP_STEPS):
                t = _dot(h_sc[k], wd_ref[k * TF:(k + 1) * TF, c0:c0 + TD].astype(BF16))
                acc = t if acc is None else acc + t
            y_ref[0:ROWS_CTX, c0:c0 + TD] = (acc[0:ROWS_CTX] * gc_ref[...]).astype(BF16)
            y_ref[ROWS_CTX:, c0:c0 + TD] = (acc[ROWS_CTX:] * gl_ref[...]).astype(BF16)


def _ffn(l, x_c, x_l, g_c, g_l, w_gate, w_up, w_down):
    rows = ROWS_CTX + ROWS_LAT
    xin = lambda r: pl.BlockSpec((None, r, D_MODEL), lambda e, j: (e, 0, 0))
    gin = lambda r: pl.BlockSpec((None, r, 1), lambda e, j: (e, 0, 0))
    up_chunk = lambda e, j: (l, e, 0, jnp.minimum(j, N_UP_STEPS - 1))
    return pl.pallas_call(
        _ffn_kernel,
        grid=(N_EXPERTS, N_UP_STEPS + 1),
        in_specs=[xin(ROWS_CTX), xin(ROWS_LAT), gin(ROWS_CTX), gin(ROWS_LAT),
                  pl.BlockSpec((None, None, D_MODEL, TF), up_chunk),
                  pl.BlockSpec((None, None, D_MODEL, TF), up_chunk),
                  pl.BlockSpec((None, None, D_FF, D_MODEL), lambda e, j: (l, e, 0, 0))],
        out_specs=pl.BlockSpec((None, rows, D_MODEL), lambda e, j: (e, 0, 0)),
        out_shape=jax.ShapeDtypeStruct((N_EXPERTS, rows, D_MODEL), BF16),
        scratch_shapes=[pltpu.VMEM((rows, D_MODEL), BF16), pltpu.VMEM((N_UP_STEPS, rows, TF), BF16)],
        compiler_params=_params("arbitrary", "arbitrary"),
        name=f"experts{l}",
    )(x_c, x_l, g_c, g_l, w_gate, w_up, w_down)


def _finish(x, res, mod_ref, gf_ref, final):
    y = x + mod_ref[:, 5 * D_MODEL:] * res
    if final:
        y = y * lax.rsqrt(jnp.mean(y * y, axis=-1, keepdims=True) + RMS_EPS) * gf_ref[...]
    return y


def _combine_ctx_kernel(final, x_ref, y_ref, slot_ref, rep_ref, mod_ref, gf_ref, o_ref):
    n_col = N_EXPERTS * CAP_CTX
    spread = _dot(slot_ref[...].astype(F32).astype(BF16), rep_ref[...])
    col = lax.broadcasted_iota(jnp.int32, (SEQ, n_col), 1) % CAP_CTX
    p = jnp.where(spread == col.astype(F32), 1.0, 0.0).astype(BF16)
    res = _dot(p, y_ref[...].reshape(n_col, D_MODEL))
    o_ref[...] = _finish(x_ref[...], res, mod_ref, gf_ref, final)


def _combine_ctx(l, final, x_new, y, slot_t, mod4, g_final):
    n_col = N_EXPERTS * CAP_CTX
    rep = (np.arange(n_col)[None, :] // CAP_CTX == np.arange(LANES)[:, None]).astype(np.float32)
    return pl.pallas_call(
        functools.partial(_combine_ctx_kernel, final),
        grid=(BATCH,),
        in_specs=[pl.BlockSpec((SEQ, D_MODEL), lambda b: (b, 0)),
                  pl.BlockSpec((N_EXPERTS, CAP_CTX, D_MODEL), lambda b: (0, b, 0)),
                  pl.BlockSpec((SEQ, LANES), lambda b: (b, 0)),
                  pl.BlockSpec((LANES, n_col), lambda b: (0, 0)),
                  pl.BlockSpec((None, None, 1, 6 * D_MODEL), lambda b: (l, 0, 0, 0)),
                  pl.BlockSpec((1, D_MODEL), lambda b: (0, 0))],
        out_specs=pl.BlockSpec((SEQ, D_MODEL), lambda b: (b, 0)),
        out_shape=jax.ShapeDtypeStruct((N_CTX, D_MODEL), F32),
        compiler_params=_params("arbitrary"),
        name=f"combine_ctx{l}",
    )(x_new, y, slot_t, jnp.asarray(rep, BF16), mod4, g_final.reshape(1, D_MODEL))


TMC = 512


def _combine_lat_kernel(final, x_ref, y_ref, slot_ref, mod_ref, gf_ref, o_ref):
    s_iota = lax.broadcasted_iota(jnp.int32, (TMC, CAP_LAT), 1)
    slot = slot_ref[...]
    res = None
    for e in range(N_EXPERTS):
        p = jnp.where(slot[:, e:e + 1] == s_iota, 1.0, 0.0).astype(BF16)
        t = _dot(p, y_ref[e])
        res = t if res is None else res + t
    o_ref[...] = _finish(x_ref[...], res, mod_ref, gf_ref, final)


def _combine_lat(l, final, x_new, y, slot_t, mod4, g_final):
    nt = DEC_SEQ // TMC
    return pl.pallas_call(
        functools.partial(_combine_lat_kernel, final),
        grid=(DEC_BATCH, nt),
        in_specs=[pl.BlockSpec((TMC, D_MODEL), lambda b, t: (N_CTX // TMC + b * nt + t, 0)),
                  pl.BlockSpec((N_EXPERTS, CAP_LAT, D_MODEL), lambda b, t: (0, ROWS_CTX // CAP_LAT + b, 0)),
                  pl.BlockSpec((TMC, LANES), lambda b, t: (b * nt + t, 0)),
                  pl.BlockSpec((None, None, 1, 6 * D_MODEL), lambda b, t: (l, 1 + b, 0, 0)),
                  pl.BlockSpec((1, D_MODEL), lambda b, t: (0, 0))],
        out_specs=pl.BlockSpec((TMC, D_MODEL), lambda b, t: (b * nt + t, 0)),
        out_shape=jax.ShapeDtypeStruct((N_LAT, D_MODEL), F32),
        compiler_params=_params("arbitrary", "arbitrary"),
        name=f"combine_lat{l}",
    )(x_new, y, slot_t, mod4, g_final.reshape(1, D_MODEL))


def _split_table(t):
    hi = t.astype(BF16)
    return hi, (t - hi.astype(F32)).astype(BF16)


def _dft_tables(n):
    p = np.arange(n, dtype=np.int64)
    ang = ((p[:, None] * p[None, :]) % n).astype(np.float64) * (2.0 * np.pi / n)
    return np.cos(ang).astype(np.float32), np.sin(ang).astype(np.float32)


def _channel_dft_tables():
    c = np.arange(F_WIDTH, dtype=np.int64)
    same = (c[:, None] // HEAD_DIM) == (c[None, :] // HEAD_DIM)
    ang = (((c[:, None] % HEAD_DIM) * (c[None, :] % HEAD_DIM)) % HEAD_DIM).astype(np.float64) * (2.0 * np.pi / HEAD_DIM)
    return (np.where(same, np.cos(ang), 0.0).astype(np.float32),
            np.where(same, np.sin(ang), 0.0).astype(np.float32))


def _rope_tables():
    half = HEAD_DIM // 2
    nf = half // 2
    pos = np.arange(DEC_SEQ)
    inv = 1.0 / (ROPE_BASE ** (np.arange(nf, dtype=np.float64) / nf))
    ang_r = (pos // GRID_W).astype(np.float64)[:, None] * inv
    ang_c = (pos % GRID_W).astype(np.float64)[:, None] * inv

    def head(fn, sign):
        return np.concatenate([sign * fn(ang_r), fn(ang_r), sign * fn(ang_c), fn(ang_c)], axis=-1)

    cos = head(np.cos, 1.0)
    sin = head(np.sin, -1.0)
    return (np.concatenate([cos, cos], axis=-1).astype(np.float32),
            np.concatenate([sin, sin], axis=-1).astype(np.float32))


def _na_bias_tables(rpb):
    cq = np.arange(GRID_W)
    rel_c = np.clip(cq[None, :] - cq[:, None] + NA_COLS - 1, 0, 2 * NA_COLS - 2)
    pick = (rel_c[:, :, None] == np.arange(2 * NA_COLS - 1)).astype(np.float32)
    cs = np.clip(cq - NA_COLS // 2, 0, GRID_W - NA_COLS)
    col_ok = (cq[None, :] >= cs[:, None]) & (cq[None, :] < cs[:, None] + NA_COLS)
    bc = jnp.einsum('lhrj,qkj->lhrqk', rpb, pick, precision=lax.Precision.HIGHEST)
    bc = jnp.where(col_ok[None, None, None], bc, NEG_INF)
    return pl.pallas_call(
        _na_bias_kernel,
        grid=(DEPTH, NA_VARIANTS, NA_HEADS // 2),
        in_specs=[pl.BlockSpec((None, 2, 2 * NA_ROWS - 1, GRID_W, GRID_W), lambda l, v, p: (l, p, 0, 0, 0))],
        out_specs=pl.BlockSpec((None, None, None, 2 * NA_Q, NA_WIN_ROWS * GRID_W),
                               lambda l, v, p: (l, v, p, 0, 0)),
        out_shape=jax.ShapeDtypeStruct((DEPTH, NA_VARIANTS, NA_HEADS // 2, 2 * NA_Q, NA_WIN_ROWS * GRID_W), F32),
        compiler_params=_params("arbitrary", "arbitrary", "arbitrary"),
        name="na_bias",
    )(bc)


def _na_window_plan():
    plan = []
    for g in (0, 1, NA_BLOCKS - 1):
        start = int(np.clip(NA_G * g - NA_ROWS // 2, 0, GRID_ROWS - NA_WIN_ROWS))
        rows = []
        for a in range(NA_G):
            r = NA_G * g + a
            rs = int(np.clip(r - NA_ROWS // 2, 0, GRID_ROWS - NA_ROWS))
            rows.append([start + w - r + NA_ROWS - 1 if rs <= start + w < rs + NA_ROWS else None
                         for w in range(NA_WIN_ROWS)])
        plan.append(rows)
    return plan


def _na_bias_kernel(bc_ref, o_ref):
    outside = jnp.full((GRID_W, GRID_W), NEG_INF, F32)
    for v, rows in enumerate(_na_window_plan()):
        @pl.when(pl.program_id(1) == v)
        def _():
            for half in range(2):
                for a, rel in enumerate(rows):
                    tiles = [outside if rr is None else bc_ref[half, rr] for rr in rel]
                    r0 = half * NA_Q + a * GRID_W
                    o_ref[r0:r0 + GRID_W, :] = jnp.concatenate(tiles, axis=-1)


def kernel(x_prompt, x_sample, cache_win_k, cache_win_v, cache_nat_k, cache_nat_v, c, c_ctx, w_mod, b_mod, g_mix, g_ffn, w_in, w_out, win_sink, nat_rpb, w_router, w_gate, w_up, w_down, g_final):
    x_ctx = x_prompt.reshape(N_CTX, D_MODEL)
    x_lat = x_sample.reshape(N_LAT, D_MODEL)
    cond = jnp.concatenate([c_ctx[None, :], c, jnp.zeros((N_COND - 1 - DEC_BATCH, D_MODEL), F32)], axis=0)
    mod4 = _adaln(cond, w_mod, b_mod).reshape(DEPTH, N_COND, 1, 6 * D_MODEL)

    dft_ch = _channel_dft_tables()
    dft_ctx = _dft_tables(SEQ)
    dft_lat = _dft_tables(DEC_SEQ)
    cos_t, sin_t = _rope_tables()
    cwk = cache_win_k.reshape(DEC_BATCH, DEPTH, PAST_LEN, WIN_KV)
    cwv = cache_win_v.reshape(DEC_BATCH, DEPTH, PAST_LEN, WIN_KV)
    cnk = cache_nat_k.reshape(DEC_BATCH, DEPTH, PAST_LEN, NA_W)
    cnv = cache_nat_v.reshape(DEC_BATCH, DEPTH, PAST_LEN, NA_W)
    r_pad = jnp.pad(w_router, ((0, 0), (0, 0), (0, LANES - N_EXPERTS)))
    na_bias = _na_bias_tables(nat_rpb)

    kvt = ()
    for l in range(DEPTH):
        final = l == DEPTH - 1
        (f, qw, qn), (kw, vw, kn, vn), kvt = _project(l, x_ctx, x_lat, mod4, g_mix, w_in, kvt)

        mixed_ctx = _ctx_mixer(l, win_sink, f, qw, qn, kvt, dft_ch, dft_ctx)
        mixed_f = _lat_fourier(f, dft_ch, dft_lat)
        mixed_w = _lat_window(l, win_sink, qw, kw, vw, cwk, cwv, cos_t, sin_t)
        mixed_n = _lat_neighbourhood(l, qn, kn, vn, cnk, cnv, na_bias)

        r_hi, r_lo = _split_table(r_pad[l])
        x_new, h, lg_t = _outproj(l, x_ctx, x_lat, mixed_ctx, mixed_f, mixed_w, mixed_n, mod4, g_ffn,
                                  w_out, r_hi, r_lo)

        slot_c, aff_c, slot_ct = _route(lg_t, 0, BATCH, CAP_CTX, "route_ctx")
        slot_l, aff_l, slot_lt = _route(lg_t, 1, DEC_BATCH, CAP_LAT, "route_lat")
        xg_c, gate_c = _gather_ctx(h, slot_c, aff_c)
        xg_l, gate_l = _gather_lat(h, slot_l, aff_l)
        y = _ffn(l, xg_c, xg_l, gate_c, gate_l, w_gate, w_up, w_down)
        x_ctx = _combine_ctx(l, final, x_new, y, slot_ct, mod4, g_final)
        x_lat = _combine_lat(l, final, x_new, y, slot_lt, mod4, g_final)

    y_prompt = x_ctx.reshape(BATCH, SEQ, D_MODEL)
    y_sample = x_lat.reshape(DEC_BATCH, DEC_SEQ, D_MODEL)
    new_kv = [t.reshape(BATCH, DEPTH, w // HEAD_DIM, HEAD_DIM, SEQ).transpose(0, 1, 4, 2, 3)
              for t, w in zip(kvt, KV_WIDTHS)]
    return (y_prompt, y_sample, *new_kv)
```

```python
import functools

import numpy as np
import jax
import jax.numpy as jnp
from jax import lax
from jax.experimental import pallas as pl
from jax.experimental.pallas import tpu as pltpu

D_MODEL = 1024
BATCH = 16
SEQ = 256
DEPTH = 2
DEC_BATCH = 2
DEC_SEQ = 2048
PAST_LEN = 256
GRID_W = 64
HEAD_DIM = 64
F_WIDTH = 256
WIN_HEADS = 6
WIN_KV_HEADS = 2
WINDOW = 128
WIN_BLOCK = 128
NA_HEADS = 6
NA_ROWS = 8
NA_COLS = 16
N_EXPERTS = 16
EC_CAPACITY = 2
D_FF = 2816
ROPE_BASE = 10000.0
RMS_EPS = 1e-6
NEG_INF = -1e30
ATTN_SCALE = HEAD_DIM ** -0.5
WIN_Q = WIN_HEADS * HEAD_DIM
WIN_KV = WIN_KV_HEADS * HEAD_DIM
NA_W = NA_HEADS * HEAD_DIM
N_IN = F_WIDTH + WIN_Q + 2 * WIN_KV + 3 * NA_W
SPLITS = (0, F_WIDTH, F_WIDTH + WIN_Q, F_WIDTH + WIN_Q + WIN_KV, F_WIDTH + WIN_Q + 2 * WIN_KV,
          F_WIDTH + WIN_Q + 2 * WIN_KV + NA_W, F_WIDTH + WIN_Q + 2 * WIN_KV + 2 * NA_W, N_IN)

N_CTX = BATCH * SEQ
N_LAT = DEC_BATCH * DEC_SEQ
N_TOK = N_CTX + N_LAT
GRID_ROWS = DEC_SEQ // GRID_W
CAP_CTX = EC_CAPACITY * SEQ // N_EXPERTS
CAP_LAT = EC_CAPACITY * DEC_SEQ // N_EXPERTS
ROWS_CTX = BATCH * CAP_CTX
ROWS_LAT = DEC_BATCH * CAP_LAT
N_COND = 8

LANES = 128
MXU_COLS = 256
TM = 512
TN_MOD = 1536
TF = 256
TD = 256
VMEM_LIMIT = 56 * 1024 * 1024

F32 = jnp.float32
BF16 = jnp.bfloat16


def _params(*sem):
    return pltpu.CompilerParams(dimension_semantics=sem, vmem_limit_bytes=VMEM_LIMIT)


def _dot(a, b):
    return jnp.dot(a, b, preferred_element_type=F32)


def _dot_nt(a, b):
    return lax.dot_general(a, b, (((1,), (1,)), ((), ())), preferred_element_type=F32)


def _split(x):
    hi = x.astype(BF16)
    lo = (x - hi.astype(F32)).astype(BF16)
    return hi, lo


def _dot3(a_hi, a_lo, b_hi, b_lo):
    return _dot(a_hi, b_hi) + (_dot(a_lo, b_hi) + _dot(a_hi, b_lo))


def _silu(x):
    return x / (1.0 + jnp.exp(-x))


def _rms_mod(x, g, shift, scale):
    y = x * lax.rsqrt(jnp.mean(x * x, axis=-1, keepdims=True) + RMS_EPS)
    return (y * g) * (1.0 + scale) + shift


def _softmax_parts(parts, sink=None):
    m = parts[0].max(axis=-1, keepdims=True)
    for s in parts[1:]:
        m = jnp.maximum(m, s.max(axis=-1, keepdims=True))
    if sink is not None:
        m = jnp.maximum(m, sink)
    es = [jnp.exp(s - m) for s in parts]
    den = es[0].sum(axis=-1, keepdims=True)
    for e in es[1:]:
        den = den + e.sum(axis=-1, keepdims=True)
    if sink is not None:
        den = den + jnp.exp(sink - m)
    inv = 1.0 / den
    return [e * inv for e in es]


def _cond_of_block(i):
    n_ctx_blocks = N_CTX // TM
    return jnp.where(i < n_ctx_blocks, 0, 1 + (i - n_ctx_blocks) // (DEC_SEQ // TM))


def _two_stream_specs(width):
    n_ctx_blocks = N_CTX // TM
    ctx = pl.BlockSpec((TM, width), lambda i: (jnp.minimum(i, n_ctx_blocks - 1), 0))
    lat = pl.BlockSpec((TM, width), lambda i: (jnp.maximum(i - n_ctx_blocks, 0), 0))
    return ctx, lat


def _pick_stream(ctx_ref, lat_ref):
    return jnp.where(pl.program_id(0) < N_CTX // TM, ctx_ref[...], lat_ref[...])


def _adaln_kernel(c_ref, w_ref, b_ref, o_ref):
    s_hi, s_lo = _split(_silu(c_ref[...]))
    w_hi, w_lo = _split(w_ref[...])
    o_ref[...] = _dot3(s_hi, s_lo, w_hi, w_lo) + b_ref[...]


def _adaln(cond, w_mod, b_mod):
    return pl.pallas_call(
        _adaln_kernel,
        grid=(DEPTH, 6 * D_MODEL // TN_MOD),
        in_specs=[
            pl.BlockSpec((N_COND, D_MODEL), lambda l, j: (0, 0)),
            pl.BlockSpec((None, D_MODEL, TN_MOD), lambda l, j: (l, 0, j)),
            pl.BlockSpec((None, 1, TN_MOD), lambda l, j: (l, 0, j)),
        ],
        out_specs=pl.BlockSpec((None, N_COND, TN_MOD), lambda l, j: (l, 0, j)),
        out_shape=jax.ShapeDtypeStruct((DEPTH, N_COND, 6 * D_MODEL), F32),
        compiler_params=_params("arbitrary", "arbitrary"),
        name="adaln",
    )(cond, w_mod, b_mod.reshape(DEPTH, 1, 6 * D_MODEL))


KV_NAMES = ("kw", "vw", "kn", "vn")
KV_WIDTHS = (WIN_KV, WIN_KV, NA_W, NA_W)
KV_SPLITS = (SPLITS[2], SPLITS[3], SPLITS[5], SPLITS[6])
KV_TOTAL = sum(KV_WIDTHS)
Q_SPLITS = (SPLITS[0], SPLITS[1], SPLITS[4])


def _proj_kernel(n_prev, xc_ref, xl_ref, mod_ref, g_ref, w_ref, *rest):
    rest = rest[n_prev:]
    f_ref, qw_ref, qn_ref = rest[0:3]
    lat_kv = rest[3:7]
    ctx_kvt = rest[7:11]
    wb_ref, wt_ref = rest[11:13]
    i = pl.program_id(0)

    @pl.when(i == 0)
    def _():
        wb_ref[...] = w_ref[...].astype(BF16)
        r0 = 0
        for c0, width in zip(KV_SPLITS, KV_WIDTHS):
            wt_ref[r0:r0 + width, :] = w_ref[:, c0:c0 + width].T.astype(BF16)
            r0 += width

    natural = dict(zip(SPLITS[:-1], (f_ref, qw_ref) + tuple(lat_kv[0:2]) + (qn_ref,) + tuple(lat_kv[2:4])))

    def hidden(x_ref):
        return _rms_mod(x_ref[...], g_ref[...], mod_ref[:, 0:D_MODEL], mod_ref[:, D_MODEL:2 * D_MODEL]).astype(BF16)

    def project_tiles(h, wanted):
        for t0 in range(0, N_IN, MXU_COLS):
            hits = [(a, b) for a, b in zip(SPLITS[:-1], SPLITS[1:])
                    if a in wanted and max(a, t0) < min(b, t0 + MXU_COLS)]
            if not hits:
                continue
            acc = _dot(h, wb_ref[:, t0:t0 + MXU_COLS])
            for a, b in hits:
                lo, hi = max(a, t0), min(b, t0 + MXU_COLS)
                natural[a][:, lo - a:hi - a] = acc[:, lo - t0:hi - t0]

    @pl.when(i < N_CTX // TM)
    def _():
        h = hidden(xc_ref)
        project_tiles(h, Q_SPLITS)
        kvt = _dot_nt(wt_ref[...], h)
        r0 = 0
        for o_ref, width in zip(ctx_kvt, KV_WIDTHS):
            for bb in range(TM // SEQ):
                o_ref[bb] = kvt[r0:r0 + width, bb * SEQ:(bb + 1) * SEQ]
            r0 += width

    @pl.when(i >= N_CTX // TM)
    def _():
        project_tiles(hidden(xl_ref), Q_SPLITS + KV_SPLITS)


def _project(l, x_ctx, x_lat, mod4, g_mix, w_in, prev_kvt):
    n_ctx_blocks = N_CTX // TM
    xc_spec, xl_spec = _two_stream_specs(D_MODEL)
    both = lambda w: pl.BlockSpec((TM, w), lambda i: (i, 0))
    lat = lambda w: pl.BlockSpec((TM, w), lambda i: (jnp.maximum(i - n_ctx_blocks, 0), 0))
    ctx_t = lambda w: pl.BlockSpec((TM // SEQ, None, w, SEQ),
                                   lambda i: (jnp.minimum(i, n_ctx_blocks - 1), l, 0, 0))
    n_prev = len(prev_kvt)
    n_in = 5
    outs = pl.pallas_call(
        functools.partial(_proj_kernel, n_prev),
        grid=(N_TOK // TM,),
        in_specs=[
            xc_spec, xl_spec,
            pl.BlockSpec((None, None, 1, 6 * D_MODEL), lambda i: (l, _cond_of_block(i), 0, 0)),
            pl.BlockSpec((None, 1, D_MODEL), lambda i: (l, 0, 0)),
            pl.BlockSpec((None, D_MODEL, N_IN), lambda i: (l, 0, 0)),
        ] + [pl.BlockSpec(memory_space=pl.ANY)] * n_prev,
        out_specs=[both(F_WIDTH), both(WIN_Q), both(NA_W)] + [lat(w) for w in KV_WIDTHS]
                  + [ctx_t(w) for w in KV_WIDTHS],
        out_shape=[jax.ShapeDtypeStruct((N_TOK, w), F32) for w in (F_WIDTH, WIN_Q, NA_W)]
                  + [jax.ShapeDtypeStruct((N_LAT, w), F32) for w in KV_WIDTHS]
                  + [jax.ShapeDtypeStruct((BATCH, DEPTH, w, SEQ), F32) for w in KV_WIDTHS],
        input_output_aliases={n_in + k: 7 + k for k in range(n_prev)},
        scratch_shapes=[pltpu.VMEM((D_MODEL, N_IN), BF16), pltpu.VMEM((KV_TOTAL, D_MODEL), BF16)],
        compiler_params=_params("arbitrary"),
        name=f"project{l}",
    )(x_ctx, x_lat, mod4, g_mix.reshape(DEPTH, 1, D_MODEL), w_in, *prev_kvt)
    return outs[0:3], outs[3:7], outs[7:11]


def _lane_is_low(shape):
    return lax.broadcasted_iota(jnp.int32, shape, len(shape) - 1) < HEAD_DIM


def _swap_halves(x):
    return pltpu.roll(x, HEAD_DIM, axis=x.ndim - 1)


def _win_kv_copy(h):
    return 0 if (h // (WIN_HEADS // WIN_KV_HEADS)) == (h % 2) else 1


def _stack_heads(q_pairs, heads):
    low = _lane_is_low(q_pairs[heads[0] // 2].shape)
    rows = [jnp.where(low if h % 2 == 0 else jnp.logical_not(low), q_pairs[h // 2], 0.0).astype(BF16)
            for h in heads]
    return rows[0] if len(rows) == 1 else jnp.concatenate(rows, axis=0)


def _per_head_column(values, rows_per_head):
    blk = lax.broadcasted_iota(jnp.int32, (len(values) * rows_per_head, 1), 0) // rows_per_head
    col = jnp.full(blk.shape, values[0], F32)
    for i in range(1, len(values)):
        col = jnp.where(blk == i, values[i], col)
    return col


def _attend(q_stack, kv_list, extra_logit=None, transposed=False):
    scores = []
    for k, _, post in kv_list:
        s = _dot(q_stack, k) if transposed else _dot_nt(q_stack, k)
        scores.append(post(s) if post is not None else s)
    probs = _softmax_parts(scores, extra_logit)
    o = None
    for p, (_, v, _) in zip(probs, kv_list):
        t = _dot_nt(p.astype(BF16), v) if transposed else _dot(p.astype(BF16), v)
        o = t if o is None else o + t
    return o


def _merge_pair(o_even, o_odd):
    return jnp.where(_lane_is_low(o_even.shape), o_even, o_odd)


def _gqa_attention(q_pairs, rows, kv_for_copy, sinks, transposed=False):
    per_head = {}
    for copy in (0, 1):
        heads = [h for h in range(WIN_HEADS) if _win_kv_copy(h) == copy]
        o = _attend(_stack_heads(q_pairs, heads), kv_for_copy(copy),
                    _per_head_column([sinks[h] for h in heads], rows), transposed)
        for i, h in enumerate(heads):
            per_head[h] = o[i * rows:(i + 1) * rows]
    return [_merge_pair(per_head[2 * j], per_head[2 * j + 1]) for j in range(WIN_HEADS // 2)]


def _swap_row_halves(x):
    return jnp.concatenate([x[HEAD_DIM:], x[:HEAD_DIM]], axis=0)


def _ctx_mixer_kernel(l, sink_ref, f_ref, qw_ref, qn_ref, kw_ref, vw_ref, kn_ref, vn_ref,
                      bc_ref, bs_ref, cs_ref, ss_ref, o_ref):
    f_hi, f_lo = _split(f_ref[...])
    fc_hi, fc_lo = _split(_dot3(f_hi, f_lo, *_split(bc_ref[...])))
    fs_hi, fs_lo = _split(_dot3(f_hi, f_lo, *_split(bs_ref[...])))
    z = _dot3(*_split(cs_ref[...]), fc_hi, fc_lo) - _dot3(*_split(ss_ref[...]), fs_hi, fs_lo)
    o_ref[:, 0:F_WIDTH] = (z * (SEQ * HEAD_DIM) ** -0.5).astype(BF16)

    kv = [(kw_ref[...].astype(BF16), vw_ref[...].astype(BF16)),
          (_swap_row_halves(kw_ref[...]).astype(BF16), _swap_row_halves(vw_ref[...]).astype(BF16))]
    q_pairs = [qw_ref[:, LANES * j:LANES * (j + 1)] * ATTN_SCALE for j in range(WIN_HEADS // 2)]
    outs = _gqa_attention(q_pairs, SEQ, lambda c: [(kv[c][0], kv[c][1], None)],
                          [sink_ref[l, h] for h in range(WIN_HEADS)], transposed=True)
    for j, o in enumerate(outs):
        o_ref[:, F_WIDTH + LANES * j:F_WIDTH + LANES * (j + 1)] = o.astype(BF16)

    for j in range(NA_HEADS // 2):
        sl = slice(LANES * j, LANES * (j + 1))
        q_pairs = {j: qn_ref[:, sl] * ATTN_SCALE}
        o = _attend(_stack_heads(q_pairs, (2 * j, 2 * j + 1)),
                    [(kn_ref[sl, :].astype(BF16), vn_ref[sl, :].astype(BF16), None)], transposed=True)
        base = F_WIDTH + WIN_Q + LANES * j
        o_ref[:, base:base + LANES] = _merge_pair(o[:SEQ], o[SEQ:]).astype(BF16)


def _ctx_mixer(l, win_sink, f, qw, qn, kvt, dft_ch, dft_seq):
    row = lambda w: pl.BlockSpec((SEQ, w), lambda b: (b, 0))
    col = lambda w: pl.BlockSpec((None, None, w, SEQ), lambda b: (b, l, 0, 0))
    const = lambda n: pl.BlockSpec((n, n), lambda b: (0, 0))
    return pl.pallas_call(
        functools.partial(_ctx_mixer_kernel, l),
        grid=(BATCH,),
        in_specs=[pl.BlockSpec(memory_space=pltpu.SMEM),
                  row(F_WIDTH), row(WIN_Q), row(NA_W)] + [col(w) for w in KV_WIDTHS]
                 + [const(F_WIDTH)] * 2 + [const(SEQ)] * 2,
        out_specs=pl.BlockSpec((SEQ, D_MODEL), lambda b: (b, 0)),
        out_shape=jax.ShapeDtypeStruct((N_CTX, D_MODEL), BF16),
        compiler_params=_params("arbitrary"),
        name=f"ctx_mixer{l}",
    )(win_sink, f, qw, qn, *kvt, *dft_ch, *dft_seq)


FT_ROWS = 256


def _lat_fourier_kernel(f_ref, bc_ref, bs_ref, cs_ref, ss_ref, o_ref, st_ref):
    b = pl.program_id(1)

    @pl.when(pl.program_id(0) == 0)
    def _():
        f_hi, f_lo = _split(f_ref[pl.ds(pl.multiple_of(b * DEC_SEQ, DEC_SEQ), DEC_SEQ), :])
        fc_hi, fc_lo = _split(_dot3(f_hi, f_lo, *_split(bc_ref[...])))
        fs_hi, fs_lo = _split(_dot3(f_hi, f_lo, *_split(bs_ref[...])))
        st_ref[b, 0] = fc_hi
        st_ref[b, 1] = fc_lo
        st_ref[b, 2] = fs_hi
        st_ref[b, 3] = fs_lo

    z = (_dot3(*_split(cs_ref[...]), st_ref[b, 0], st_ref[b, 1])
         - _dot3(*_split(ss_ref[...]), st_ref[b, 2], st_ref[b, 3]))
    o_ref[...] = (z * (DEC_SEQ * HEAD_DIM) ** -0.5).astype(BF16)


def _lat_fourier(f, dft_ch, dft_seq):
    nrb = DEC_SEQ // FT_ROWS
    const = pl.BlockSpec((F_WIDTH, F_WIDTH), lambda r, b: (0, 0))
    rows = pl.BlockSpec((FT_ROWS, DEC_SEQ), lambda r, b: (r, 0))
    return pl.pallas_call(
        _lat_fourier_kernel,
        grid=(nrb, DEC_BATCH),
        in_specs=[pl.BlockSpec((N_LAT, F_WIDTH), lambda r, b: (1, 0))] + [const] * 2 + [rows] * 2,
        out_specs=pl.BlockSpec((FT_ROWS, F_WIDTH), lambda r, b: (b * nrb + r, 0)),
        out_shape=jax.ShapeDtypeStruct((N_LAT, F_WIDTH), BF16),
        scratch_shapes=[pltpu.VMEM((DEC_BATCH, 4, DEC_SEQ, F_WIDTH), BF16)],
        compiler_params=_params("arbitrary", "arbitrary"),
        name="lat_fourier",
    )(f, *dft_ch, *dft_seq)


def _rope(x, cos, sin_signed):
    n = x.shape[-1]
    lane = lax.broadcasted_iota(jnp.int32, x.shape, x.ndim - 1)
    first = (lane % 32) < 16
    partner = jnp.where(first, pltpu.roll(x, n - 16, axis=x.ndim - 1), pltpu.roll(x, 16, axis=x.ndim - 1))
    return x * cos + partner * sin_signed


def _win_kernel(l, sink_ref, q_ref, k_ref, v_ref, ck_ref, cv_ref, cos_ref, sin_ref, cosq_ref, sinq_ref,
                o_ref, kp_ref, vp_ref, cp_ref):
    n = pl.program_id(1)
    nb = DEC_SEQ // WIN_BLOCK
    pad = WIN_BLOCK

    @pl.when(n == 0)
    def _():
        zeros = jnp.zeros((pad, LANES), BF16)
        kr = _rope(k_ref[...], cos_ref[...], sin_ref[...])
        v = v_ref[...]
        for idx, (kk, vv) in enumerate(((kr, v), (_swap_halves(kr), _swap_halves(v)))):
            kp_ref[idx, 0:pad] = zeros
            kp_ref[idx, pad + DEC_SEQ:] = zeros
            kp_ref[idx, pad:pad + DEC_SEQ] = kk.astype(BF16)
            vp_ref[idx, 0:pad] = zeros
            vp_ref[idx, pad + DEC_SEQ:] = zeros
            vp_ref[idx, pad:pad + DEC_SEQ] = vv.astype(BF16)
        ck = ck_ref[...]
        cv = cv_ref[...]
        cp_ref[0] = ck.astype(BF16)
        cp_ref[1] = _swap_halves(ck).astype(BF16)
        cp_ref[2] = cv.astype(BF16)
        cp_ref[3] = _swap_halves(cv).astype(BF16)

    lo = jnp.where(n == 0, WIN_BLOCK, 0)
    hi = jnp.where(n == nb - 1, 2 * WIN_BLOCK, 3 * WIN_BLOCK)

    i = lax.broadcasted_iota(jnp.int32, (WIN_BLOCK, 3 * WIN_BLOCK), 0)
    j = lax.broadcasted_iota(jnp.int32, (WIN_BLOCK, 3 * WIN_BLOCK), 1)
    mask = (j >= i + WIN_BLOCK - WINDOW) & (j <= i + WIN_BLOCK + WINDOW) & (j >= lo) & (j < hi)
    band_bias = jnp.where(mask, 0.0, NEG_INF)

    def band(s):
        heads = s.shape[0] // WIN_BLOCK
        return (s.reshape(heads, WIN_BLOCK, s.shape[1]) + band_bias[None]).reshape(s.shape)

    start = pl.multiple_of(n * WIN_BLOCK, WIN_BLOCK)
    win = pl.ds(start, 3 * WIN_BLOCK)
    q_pairs = [_rope(q_ref[:, LANES * jp:LANES * (jp + 1)], cosq_ref[...], sinq_ref[...]) * ATTN_SCALE
               for jp in range(WIN_HEADS // 2)]
    outs = _gqa_attention(
        q_pairs, WIN_BLOCK,
        lambda c: [(kp_ref[c, win, :], vp_ref[c, win, :], band), (cp_ref[c], cp_ref[2 + c], None)],
        [sink_ref[l, h] for h in range(WIN_HEADS)])
    for jp, o in enumerate(outs):
        o_ref[:, LANES * jp:LANES * (jp + 1)] = o.astype(BF16)


def _lat_window(l, win_sink, qw, kw, vw, cache_k, cache_v, cos_t, sin_t):
    nb = DEC_SEQ // WIN_BLOCK
    kv_spec = pl.BlockSpec((DEC_SEQ, WIN_KV), lambda b, n: (b, 0))
    cache_spec = pl.BlockSpec((None, None, PAST_LEN, WIN_KV), lambda b, n: (b, l, 0, 0))
    tab_all = pl.BlockSpec((DEC_SEQ, LANES), lambda b, n: (0, 0))
    tab_blk = pl.BlockSpec((WIN_BLOCK, LANES), lambda b, n: (n, 0))
    return pl.pallas_call(
        functools.partial(_win_kernel, l),
        grid=(DEC_BATCH, nb),
        in_specs=[pl.BlockSpec(memory_space=pltpu.SMEM),
                  pl.BlockSpec((WIN_BLOCK, WIN_Q), lambda b, n: (N_CTX // WIN_BLOCK + b * nb + n, 0)),
                  kv_spec, kv_spec, cache_spec, cache_spec, tab_all, tab_all, tab_blk, tab_blk],
        out_specs=pl.BlockSpec((WIN_BLOCK, WIN_Q), lambda b, n: (b * nb + n, 0)),
        out_shape=jax.ShapeDtypeStruct((N_LAT, WIN_Q), BF16),
        scratch_shapes=[pltpu.VMEM((2, DEC_SEQ + 2 * WIN_BLOCK, LANES), BF16),
                        pltpu.VMEM((2, DEC_SEQ + 2 * WIN_BLOCK, LANES), BF16),
                        pltpu.VMEM((4, PAST_LEN, LANES), BF16)],
        compiler_params=_params("arbitrary", "arbitrary"),
        name=f"lat_window{l}",
    )(win_sink, qw, kw, vw, cache_k, cache_v, cos_t, sin_t, cos_t, sin_t)


NA_G = 4
NA_Q = NA_G * GRID_W
NA_WIN_ROWS = NA_ROWS + NA_G
NA_BLOCKS = GRID_ROWS // NA_G
NA_VARIANTS = 3


def _na_block_start(g):
    return jnp.clip(NA_G * g - NA_ROWS // 2, 0, GRID_ROWS - NA_WIN_ROWS)


def _na_kernel(q_ref, k_ref, v_ref, ck_ref, cv_ref, bias_ref, o_ref):
    g = pl.program_id(1)
    start = pl.multiple_of(_na_block_start(g) * GRID_W, GRID_W)
    win = pl.ds(start, NA_WIN_ROWS * GRID_W)
    for jp in range(NA_HEADS // 2):
        sl = slice(LANES * jp, LANES * (jp + 1))
        q_stack = _stack_heads({jp: q_ref[:, sl] * ATTN_SCALE}, (2 * jp, 2 * jp + 1))
        bias = bias_ref[jp]
        o = _attend(q_stack, [(k_ref[win, sl].astype(BF16), v_ref[win, sl].astype(BF16), lambda s: s + bias),
                              (ck_ref[:, sl].astype(BF16), cv_ref[:, sl].astype(BF16), None)])
        o_ref[:, sl] = _merge_pair(o[:NA_Q], o[NA_Q:]).astype(BF16)


def _lat_neighbourhood(l, qn, kn, vn, cache_k, cache_v, bias_tab):
    kv_spec = pl.BlockSpec((DEC_SEQ, NA_W), lambda b, g: (b, 0))
    cache_spec = pl.BlockSpec((None, None, PAST_LEN, NA_W), lambda b, g: (b, l, 0, 0))
    variant = lambda g: jnp.where(g == 0, 0, jnp.where(g == NA_BLOCKS - 1, 2, 1))
    return pl.pallas_call(
        _na_kernel,
        grid=(DEC_BATCH, NA_BLOCKS),
        in_specs=[pl.BlockSpec((NA_Q, NA_W), lambda b, g: (N_CTX // NA_Q + b * NA_BLOCKS + g, 0)),
                  kv_spec, kv_spec, cache_spec, cache_spec,
                  pl.BlockSpec((None, None, NA_HEADS // 2, 2 * NA_Q, NA_WIN_ROWS * GRID_W),
                               lambda b, g: (l, variant(g), 0, 0, 0))],
        out_specs=pl.BlockSpec((NA_Q, NA_W), lambda b, g: (b * NA_BLOCKS + g, 0)),
        out_shape=jax.ShapeDtypeStruct((N_LAT, NA_W), BF16),
        compiler_params=_params("arbitrary", "arbitrary"),
        name=f"lat_neighbourhood{l}",
    )(qn, kn, vn, cache_k, cache_v, bias_tab)


OUTPROJ_ROWS = 256


def _outproj_kernel(xc_ref, xl_ref, mc_ref, mf_ref, mw_ref, mn_ref, mod_ref, g_ref, w_ref, r_hi, r_lo,
                    x_ref, h_ref, lg_ref, wb_ref):
    i = pl.program_id(0)

    @pl.when(i == 0)
    def _():
        wb_ref[...] = w_ref[...].astype(BF16)

    d = D_MODEL

    def block(x_in_ref, mixed_rows):
        for r0 in range(0, TM, OUTPROJ_ROWS):
            rows = slice(r0, r0 + OUTPROJ_ROWS)
            x = x_in_ref[rows, :] + mod_ref[:, 2 * d:3 * d] * _dot(mixed_rows(rows), wb_ref[...])
            x_ref[rows, :] = x
            h = _rms_mod(x, g_ref[...], mod_ref[:, 3 * d:4 * d], mod_ref[:, 4 * d:5 * d])
            h_ref[rows, :] = h.astype(BF16)
            h_hi, h_lo = _split(h)
            lg_ref[:, rows] = _dot3(h_hi, h_lo, r_hi[...], r_lo[...]).T[0:N_EXPERTS, :]

    @pl.when(i < N_CTX // TM)
    def _():
        block(xc_ref, lambda rows: mc_ref[rows, :])

    @pl.when(i >= N_CTX // TM)
    def _():
        block(xl_ref, lambda rows: jnp.concatenate([mf_ref[rows, :], mw_ref[rows, :], mn_ref[rows, :]], axis=1))


def _outproj(l, x_ctx, x_lat, mixed_ctx, mixed_f, mixed_w, mixed_n, mod4, g_ffn, w_out, r_hi, r_lo):
    n_ctx_blocks = N_CTX // TM
    xc_spec, xl_spec = _two_stream_specs(D_MODEL)
    lat = lambda w: pl.BlockSpec((TM, w), lambda i: (jnp.maximum(i - n_ctx_blocks, 0), 0))
    whole = lambda shape: pl.BlockSpec(shape, lambda i: (0,) * len(shape))
    row = lambda w: pl.BlockSpec((TM, w), lambda i: (i, 0))
    return pl.pallas_call(
        _outproj_kernel,
        grid=(N_TOK // TM,),
        in_specs=[xc_spec, xl_spec,
                  pl.BlockSpec((TM, D_MODEL), lambda i: (jnp.minimum(i, n_ctx_blocks - 1), 0)),
                  lat(F_WIDTH), lat(WIN_Q), lat(NA_W),
                  pl.BlockSpec((None, None, 1, 6 * D_MODEL), lambda i: (l, _cond_of_block(i), 0, 0)),
                  pl.BlockSpec((None, 1, D_MODEL), lambda i: (l, 0, 0)),
                  pl.BlockSpec((None, D_MODEL, D_MODEL), lambda i: (l, 0, 0)),
                  whole((D_MODEL, LANES)), whole((D_MODEL, LANES))],
        out_specs=[row(D_MODEL), row(D_MODEL), pl.BlockSpec((N_EXPERTS, TM), lambda i: (0, i))],
        out_shape=[jax.ShapeDtypeStruct((N_TOK, D_MODEL), F32),
                   jax.ShapeDtypeStruct((N_TOK, D_MODEL), BF16),
                   jax.ShapeDtypeStruct((N_EXPERTS, N_TOK), F32)],
        scratch_shapes=[pltpu.VMEM((D_MODEL, D_MODEL), BF16)],
        compiler_params=_params("arbitrary"),
        name=f"outproj{l}",
    )(x_ctx, x_lat, mixed_ctx, mixed_f, mixed_w, mixed_n, mod4, g_ffn.reshape(DEPTH, 1, D_MODEL), w_out,
      r_hi, r_lo)


PREFIX_CHUNK = 256
MANTISSA_STEPS = 44


def _prefix_exclusive(m):
    rows, n = m.shape
    t0 = lax.broadcasted_iota(jnp.int32, (PREFIX_CHUNK, PREFIX_CHUNK), 0)
    t1 = lax.broadcasted_iota(jnp.int32, (PREFIX_CHUNK, PREFIX_CHUNK), 1)
    upper = jnp.where(t0 < t1, 1.0, 0.0).astype(BF16)
    carry = jnp.zeros((rows, 1), F32)
    outs = []
    for c in range(n // PREFIX_CHUNK):
        blk = m[:, c * PREFIX_CHUNK:(c + 1) * PREFIX_CHUNK]
        outs.append(_dot(blk.astype(BF16), upper) + carry)
        carry = carry + blk.sum(axis=-1, keepdims=True)
    return outs[0] if len(outs) == 1 else jnp.concatenate(outs, axis=-1)


def _route_kernel(cap, groups, lg_ref, slot_ref, aff_ref, slott_ref):
    n = lg_ref.shape[1] // groups
    x = jnp.concatenate([lg_ref[:, g * n:(g + 1) * n] for g in range(groups)], axis=0)
    x = x.reshape(groups, N_EXPERTS, n)
    e = jnp.exp(x - x.max(axis=1, keepdims=True))
    aff3 = e / e.sum(axis=1, keepdims=True)
    aff = aff3.reshape(aff3.shape[0] * aff3.shape[1], aff3.shape[2])
    capf = float(cap)

    def count_ge(t):
        return jnp.where(aff >= t, 1.0, 0.0).sum(axis=-1, keepdims=True)

    above = jnp.full((aff.shape[0], 1), 2.0, F32)
    for s in (64, 32, 16, 8, 4, 2, 1):
        cand = above * (2.0 ** -s)
        above = jnp.where(count_ge(cand) >= capf, above, cand)
    base = above * 0.5
    base = jnp.where(count_ge(base) >= capf, base, 0.0)

    def refine(_, carry):
        thr, inc = carry
        cand = thr + inc
        return jnp.where(count_ge(cand) >= capf, cand, thr), inc * 0.5

    thr, _ = lax.fori_loop(0, MANTISSA_STEPS, refine, (base, base * 0.5))
    gt = jnp.where(aff > thr, 1.0, 0.0)
    eq = jnp.where(aff == thr, 1.0, 0.0)
    need = capf - gt.sum(axis=-1, keepdims=True)
    sel = gt + eq * jnp.where(_prefix_exclusive(eq) < need, 1.0, 0.0)
    slot = jnp.where(sel > 0.0, _prefix_exclusive(sel), -1.0)
    slot_ref[...] = slot.astype(jnp.int32)
    aff_ref[...] = aff
    unused = jnp.full((LANES - N_EXPERTS, n), -1.0, F32)
    for g in range(groups):
        tile = jnp.concatenate([slot[g * N_EXPERTS:(g + 1) * N_EXPERTS], unused], axis=0)
        slott_ref[g * n:(g + 1) * n, :] = tile.T.astype(jnp.int32)


def _route(lg_t, stream, groups, cap, name):
    n_tok = N_CTX
    n = n_tok // groups
    rows = groups * N_EXPERTS
    return pl.pallas_call(
        functools.partial(_route_kernel, cap, groups),
        grid=(1,),
        in_specs=[pl.BlockSpec((N_EXPERTS, n_tok), lambda i: (0, stream))],
        out_specs=[pl.BlockSpec((rows, n), lambda i: (0, 0)), pl.BlockSpec((rows, n), lambda i: (0, 0)),
                   pl.BlockSpec((n_tok, LANES), lambda i: (0, 0))],
        out_shape=[jax.ShapeDtypeStruct((rows, n), jnp.int32), jax.ShapeDtypeStruct((rows, n), F32),
                   jax.ShapeDtypeStruct((n_tok, LANES), jnp.int32)],
        compiler_params=_params("arbitrary"),
        name=name,
    )(lg_t)


def _gather_ctx_kernel(h_ref, slot_ref, aff_ref, x_ref, g_ref, p_ref):
    s_iota = lax.broadcasted_iota(jnp.int32, (CAP_CTX, SEQ), 0)
    for e in range(N_EXPERTS):
        hit = s_iota == slot_ref[e:e + 1, :]
        p_ref[e * CAP_CTX:(e + 1) * CAP_CTX, :] = jnp.where(hit, 1.0, 0.0).astype(BF16)
        g_ref[e] = jnp.where(hit, aff_ref[e:e + 1, :], 0.0).sum(axis=-1, keepdims=True)
    x = _dot(p_ref[...], h_ref[...]).astype(BF16)
    x_ref[...] = x.reshape(N_EXPERTS, CAP_CTX, D_MODEL)


def _gather_ctx(h, slot, aff):
    return pl.pallas_call(
        _gather_ctx_kernel,
        grid=(BATCH,),
        in_specs=[pl.BlockSpec((SEQ, D_MODEL), lambda b: (b, 0)),
                  pl.BlockSpec((N_EXPERTS, SEQ), lambda b: (b, 0)),
                  pl.BlockSpec((N_EXPERTS, SEQ), lambda b: (b, 0))],
        out_specs=[pl.BlockSpec((N_EXPERTS, CAP_CTX, D_MODEL), lambda b: (0, b, 0)),
                   pl.BlockSpec((N_EXPERTS, CAP_CTX, 1), lambda b: (0, b, 0))],
        out_shape=[jax.ShapeDtypeStruct((N_EXPERTS, ROWS_CTX, D_MODEL), BF16),
                   jax.ShapeDtypeStruct((N_EXPERTS, ROWS_CTX, 1), F32)],
        scratch_shapes=[pltpu.VMEM((N_EXPERTS * CAP_CTX, SEQ), BF16)],
        compiler_params=_params("arbitrary"),
        name="gather_ctx",
    )(h, slot, aff)


def _gather_lat_kernel(h_ref, slot_ref, aff_ref, x_ref, g_ref):
    e = pl.program_id(1)
    s_iota = lax.broadcasted_iota(jnp.int32, (CAP_LAT, DEC_SEQ), 0)
    hit = s_iota == slot_ref[pl.ds(e, 1), :]
    x_ref[...] = _dot(jnp.where(hit, 1.0, 0.0).astype(BF16), h_ref[...]).astype(BF16)
    g_ref[...] = jnp.where(hit, aff_ref[pl.ds(e, 1), :], 0.0).sum(axis=-1, keepdims=True)


def _gather_lat(h, slot, aff):
    lat_blk0 = N_CTX // DEC_SEQ
    return pl.pallas_call(
        _gather_lat_kernel,
        grid=(DEC_BATCH, N_EXPERTS),
        in_specs=[pl.BlockSpec((DEC_SEQ, D_MODEL), lambda b, e: (lat_blk0 + b, 0)),
                  pl.BlockSpec((N_EXPERTS, DEC_SEQ), lambda b, e: (b, 0)),
                  pl.BlockSpec((N_EXPERTS, DEC_SEQ), lambda b, e: (b, 0))],
        out_specs=[pl.BlockSpec((None, CAP_LAT, D_MODEL), lambda b, e: (e, b, 0)),
                   pl.BlockSpec((None, CAP_LAT, 1), lambda b, e: (e, b, 0))],
        out_shape=[jax.ShapeDtypeStruct((N_EXPERTS, ROWS_LAT, D_MODEL), BF16),
                   jax.ShapeDtypeStruct((N_EXPERTS, ROWS_LAT, 1), F32)],
        compiler_params=_params("arbitrary", "arbitrary"),
        name="gather_lat",
    )(h, slot, aff)


N_FF_CHUNKS = D_FF // TF
CHUNKS_PER_STEP = 2
UP_BLOCK = CHUNKS_PER_STEP * TF
N_UP_STEPS = -(-N_FF_CHUNKS // CHUNKS_PER_STEP)
LAST_STEP_CHUNKS = N_FF_CHUNKS - (N_UP_STEPS - 1) * CHUNKS_PER_STEP


def _silu_tanh(x):
    return x * (0.5 + 0.5 * jnp.tanh(0.5 * x))


def _ffn_kernel(xc_ref, xl_ref, gc_ref, gl_ref, wg_ref, wu_ref, wd_ref, y_ref, x_sc, h_sc):
    j = pl.program_id(1)

    @pl.when(j == 0)
    def _():
        x_sc[0:ROWS_CTX, :] = xc_ref[...]
        x_sc[ROWS_CTX:, :] = xl_ref[...]

    def up(n_chunks):
        x = x_sc[...]
        for sub in range(n_chunks):
            cols = slice(sub * TF, (sub + 1) * TF)
            a = _dot(x, wg_ref[:, cols].astype(BF16))
            u = _dot(x, wu_ref[:, cols].astype(BF16))
            h_sc[j * CHUNKS_PER_STEP + sub] = (_silu_tanh(a) * u).astype(BF16)

    pl.when(j < N_UP_STEPS - 1)(functools.partial(up, CHUNKS_PER_STEP))
    pl.when(j == N_UP_STEPS - 1)(functools.partial(up, LAST_STEP_CHUNKS))

    @pl.when(j >= N_UP_STEPS)
    def _():
        for c0 in range(0, D_MODEL, TD):
            acc = None
            for k in range(N_FF_CHUNKS):
                t = _dot(h_sc[k], wd_ref[k * TF:(k + 1) * TF, c0:c0 + TD].astype(BF16))
                acc = t if acc is None else acc + t
            y_ref[0:ROWS_CTX, c0:c0 + TD] = (acc[0:ROWS_CTX] * gc_ref[...]).astype(BF16)
            y_ref[ROWS_CTX:, c0:c0 + TD] = (acc[ROWS_CTX:] * gl_ref[...]).astype(BF16)


def _ffn(l, x_c, x_l, g_c, g_l, w_gate, w_up, w_down):
    rows = ROWS_CTX + ROWS_LAT
    xin = lambda r: pl.BlockSpec((None, r, D_MODEL), lambda e, j: (e, 0, 0))
    gin = lambda r: pl.BlockSpec((None, r, 1), lambda e, j: (e, 0, 0))
    up_chunk = lambda e, j: (l, e, 0, jnp.minimum(j, N_UP_STEPS - 1))
    return pl.pallas_call(
        _ffn_kernel,
        grid=(N_EXPERTS, N_UP_STEPS + 1),
        in_specs=[xin(ROWS_CTX), xin(ROWS_LAT), gin(ROWS_CTX), gin(ROWS_LAT),
                  pl.BlockSpec((None, None, D_MODEL, UP_BLOCK), up_chunk),
                  pl.BlockSpec((None, None, D_MODEL, UP_BLOCK), up_chunk),
                  pl.BlockSpec((None, None, D_FF, D_MODEL), lambda e, j: (l, e, 0, 0))],
        out_specs=pl.BlockSpec((None, rows, D_MODEL), lambda e, j: (e, 0, 0)),
        out_shape=jax.ShapeDtypeStruct((N_EXPERTS, rows, D_MODEL), BF16),
        scratch_shapes=[pltpu.VMEM((rows, D_MODEL), BF16), pltpu.VMEM((N_FF_CHUNKS, rows, TF), BF16)],
        compiler_params=_params("arbitrary", "arbitrary"),
        name=f"experts{l}",
    )(x_c, x_l, g_c, g_l, w_gate, w_up, w_down)


def _finish(x, res, mod_ref, gf_ref, final):
    y = x + mod_ref[:, 5 * D_MODEL:] * res
    if final:
        y = y * lax.rsqrt(jnp.mean(y * y, axis=-1, keepdims=True) + RMS_EPS) * gf_ref[...]
    return y


def _combine_ctx_kernel(final, x_ref, y_ref, slot_ref, rep_ref, mod_ref, gf_ref, o_ref):
    n_col = N_EXPERTS * CAP_CTX
    spread = _dot(slot_ref[...].astype(F32).astype(BF16), rep_ref[...])
    col = lax.broadcasted_iota(jnp.int32, (SEQ, n_col), 1) % CAP_CTX
    p = jnp.where(spread == col.astype(F32), 1.0, 0.0).astype(BF16)
    res = _dot(p, y_ref[...].reshape(n_col, D_MODEL))
    o_ref[...] = _finish(x_ref[...], res, mod_ref, gf_ref, final)


def _combine_ctx(l, final, x_new, y, slot_t, mod4, g_final):
    n_col = N_EXPERTS * CAP_CTX
    rep = (np.arange(n_col)[None, :] // CAP_CTX == np.arange(LANES)[:, None]).astype(np.float32)
    return pl.pallas_call(
        functools.partial(_combine_ctx_kernel, final),
        grid=(BATCH,),
        in_specs=[pl.BlockSpec((SEQ, D_MODEL), lambda b: (b, 0)),
                  pl.BlockSpec((N_EXPERTS, CAP_CTX, D_MODEL), lambda b: (0, b, 0)),
                  pl.BlockSpec((SEQ, LANES), lambda b: (b, 0)),
                  pl.BlockSpec((LANES, n_col), lambda b: (0, 0)),
                  pl.BlockSpec((None, None, 1, 6 * D_MODEL), lambda b: (l, 0, 0, 0)),
                  pl.BlockSpec((1, D_MODEL), lambda b: (0, 0))],
        out_specs=pl.BlockSpec((SEQ, D_MODEL), lambda b: (b, 0)),
        out_shape=jax.ShapeDtypeStruct((N_CTX, D_MODEL), F32),
        compiler_params=_params("arbitrary"),
        name=f"combine_ctx{l}",
    )(x_new, y, slot_t, jnp.asarray(rep, BF16), mod4, g_final.reshape(1, D_MODEL))


TMC = 512


def _combine_lat_kernel(final, x_ref, y_ref, slot_ref, mod_ref, gf_ref, o_ref):
    s_iota = lax.broadcasted_iota(jnp.int32, (TMC, CAP_LAT), 1)
    slot = slot_ref[...]
    res = None
    for e in range(N_EXPERTS):
        p = jnp.where(slot[:, e:e + 1] == s_iota, 1.0, 0.0).astype(BF16)
        t = _dot(p, y_ref[e])
        res = t if res is None else res + t
    o_ref[...] = _finish(x_ref[...], res, mod_ref, gf_ref, final)


def _combine_lat(l, final, x_new, y, slot_t, mod4, g_final):
    nt = DEC_SEQ // TMC
    return pl.pallas_call(
        functools.partial(_combine_lat_kernel, final),
        grid=(DEC_BATCH, nt),
        in_specs=[pl.BlockSpec((TMC, D_MODEL), lambda b, t: (N_CTX // TMC + b * nt + t, 0)),
                  pl.BlockSpec((N_EXPERTS, CAP_LAT, D_MODEL), lambda b, t: (0, ROWS_CTX // CAP_LAT + b, 0)),
                  pl.BlockSpec((TMC, LANES), lambda b, t: (b * nt + t, 0)),
                  pl.BlockSpec((None, None, 1, 6 * D_MODEL), lambda b, t: (l, 1 + b, 0, 0)),
                  pl.BlockSpec((1, D_MODEL), lambda b, t: (0, 0))],
        out_specs=pl.BlockSpec((TMC, D_MODEL), lambda b, t: (b * nt + t, 0)),
        out_shape=jax.ShapeDtypeStruct((N_LAT, D_MODEL), F32),
        compiler_params=_params("arbitrary", "arbitrary"),
        name=f"combine_lat{l}",
    )(x_new, y, slot_t, mod4, g_final.reshape(1, D_MODEL))


def _split_table(t):
    hi = t.astype(BF16)
    return hi, (t - hi.astype(F32)).astype(BF16)


def _dft_tables(n):
    p = np.arange(n, dtype=np.int64)
    ang = ((p[:, None] * p[None, :]) % n).astype(np.float64) * (2.0 * np.pi / n)
    return np.cos(ang).astype(np.float32), np.sin(ang).astype(np.float32)


def _channel_dft_tables():
    c = np.arange(F_WIDTH, dtype=np.int64)
    same = (c[:, None] // HEAD_DIM) == (c[None, :] // HEAD_DIM)
    ang = (((c[:, None] % HEAD_DIM) * (c[None, :] % HEAD_DIM)) % HEAD_DIM).astype(np.float64) * (2.0 * np.pi / HEAD_DIM)
    return (np.where(same, np.cos(ang), 0.0).astype(np.float32),
            np.where(same, np.sin(ang), 0.0).astype(np.float32))


def _rope_tables():
    half = HEAD_DIM // 2
    nf = half // 2
    pos = np.arange(DEC_SEQ)
    inv = 1.0 / (ROPE_BASE ** (np.arange(nf, dtype=np.float64) / nf))
    ang_r = (pos // GRID_W).astype(np.float64)[:, None] * inv
    ang_c = (pos % GRID_W).astype(np.float64)[:, None] * inv

    def head(fn, sign):
        return np.concatenate([sign * fn(ang_r), fn(ang_r), sign * fn(ang_c), fn(ang_c)], axis=-1)

    cos = head(np.cos, 1.0)
    sin = head(np.sin, -1.0)
    return (np.concatenate([cos, cos], axis=-1).astype(np.float32),
            np.concatenate([sin, sin], axis=-1).astype(np.float32))


def _na_bias_tables(rpb):
    cq = np.arange(GRID_W)
    rel_c = np.clip(cq[None, :] - cq[:, None] + NA_COLS - 1, 0, 2 * NA_COLS - 2)
    pick = (rel_c[:, :, None] == np.arange(2 * NA_COLS - 1)).astype(np.float32)
    cs = np.clip(cq - NA_COLS // 2, 0, GRID_W - NA_COLS)
    col_ok = (cq[None, :] >= cs[:, None]) & (cq[None, :] < cs[:, None] + NA_COLS)
    bc = jnp.einsum('lhrj,qkj->lhrqk', rpb, pick, precision=lax.Precision.HIGHEST)
    bc = jnp.where(col_ok[None, None, None], bc, NEG_INF)
    return pl.pallas_call(
        _na_bias_kernel,
        grid=(DEPTH, NA_VARIANTS, NA_HEADS // 2),
        in_specs=[pl.BlockSpec((None, 2, 2 * NA_ROWS - 1, GRID_W, GRID_W), lambda l, v, p: (l, p, 0, 0, 0))],
        out_specs=pl.BlockSpec((None, None, None, 2 * NA_Q, NA_WIN_ROWS * GRID_W),
                               lambda l, v, p: (l, v, p, 0, 0)),
        out_shape=jax.ShapeDtypeStruct((DEPTH, NA_VARIANTS, NA_HEADS // 2, 2 * NA_Q, NA_WIN_ROWS * GRID_W), F32),
        compiler_params=_params("arbitrary", "arbitrary", "arbitrary"),
        name="na_bias",
    )(bc)


def _na_window_plan():
    plan = []
    for g in (0, 1, NA_BLOCKS - 1):
        start = int(np.clip(NA_G * g - NA_ROWS // 2, 0, GRID_ROWS - NA_WIN_ROWS))
        rows = []
        for a in range(NA_G):
            r = NA_G * g + a
            rs = int(np.clip(r - NA_ROWS // 2, 0, GRID_ROWS - NA_ROWS))
            rows.append([start + w - r + NA_ROWS - 1 if rs <= start + w < rs + NA_ROWS else None
                         for w in range(NA_WIN_ROWS)])
        plan.append(rows)
    return plan


def _na_bias_kernel(bc_ref, o_ref):
    outside = jnp.full((GRID_W, GRID_W), NEG_INF, F32)
    for v, rows in enumerate(_na_window_plan()):
        @pl.when(pl.program_id(1) == v)
        def _():
            for half in range(2):
                for a, rel in enumerate(rows):
                    tiles = [outside if rr is None else bc_ref[half, rr] for rr in rel]
                    r0 = half * NA_Q + a * GRID_W
                    o_ref[r0:r0 + GRID_W, :] = jnp.concatenate(tiles, axis=-1)


def kernel(x_prompt, x_sample, cache_win_k, cache_win_v, cache_nat_k, cache_nat_v, c, c_ctx, w_mod, b_mod, g_mix, g_ffn, w_in, w_out, win_sink, nat_rpb, w_router, w_gate, w_up, w_down, g_final):
    x_ctx = x_prompt.reshape(N_CTX, D_MODEL)
    x_lat = x_sample.reshape(N_LAT, D_MODEL)
    cond = jnp.concatenate([c_ctx[None, :], c, jnp.zeros((N_COND - 1 - DEC_BATCH, D_MODEL), F32)], axis=0)
    mod4 = _adaln(cond, w_mod, b_mod).reshape(DEPTH, N_COND, 1, 6 * D_MODEL)

    dft_ch = _channel_dft_tables()
    dft_ctx = _dft_tables(SEQ)
    dft_lat = _dft_tables(DEC_SEQ)
    cos_t, sin_t = _rope_tables()
    cwk = cache_win_k.reshape(DEC_BATCH, DEPTH, PAST_LEN, WIN_KV)
    cwv = cache_win_v.reshape(DEC_BATCH, DEPTH, PAST_LEN, WIN_KV)
    cnk = cache_nat_k.reshape(DEC_BATCH, DEPTH, PAST_LEN, NA_W)
    cnv = cache_nat_v.reshape(DEC_BATCH, DEPTH, PAST_LEN, NA_W)
    r_pad = jnp.pad(w_router, ((0, 0), (0, 0), (0, LANES - N_EXPERTS)))
    na_bias = _na_bias_tables(nat_rpb)

    kvt = ()
    for l in range(DEPTH):
        final = l == DEPTH - 1
        (f, qw, qn), (kw, vw, kn, vn), kvt = _project(l, x_ctx, x_lat, mod4, g_mix, w_in, kvt)

        mixed_ctx = _ctx_mixer(l, win_sink, f, qw, qn, kvt, dft_ch, dft_ctx)
        mixed_f = _lat_fourier(f, dft_ch, dft_lat)
        mixed_w = _lat_window(l, win_sink, qw, kw, vw, cwk, cwv, cos_t, sin_t)
        mixed_n = _lat_neighbourhood(l, qn, kn, vn, cnk, cnv, na_bias)

        r_hi, r_lo = _split_table(r_pad[l])
        x_new, h, lg_t = _outproj(l, x_ctx, x_lat, mixed_ctx, mixed_f, mixed_w, mixed_n, mod4, g_ffn,
                                  w_out, r_hi, r_lo)

        slot_c, aff_c, slot_ct = _route(lg_t, 0, BATCH, CAP_CTX, "route_ctx")
        slot_l, aff_l, slot_lt = _route(lg_t, 1, DEC_BATCH, CAP_LAT, "route_lat")
        xg_c, gate_c = _gather_ctx(h, slot_c, aff_c)
        xg_l, gate_l = _gather_lat(h, slot_l, aff_l)
        y = _ffn(l, xg_c, xg_l, gate_c, gate_l, w_gate, w_up, w_down)
        x_ctx = _combine_ctx(l, final, x_new, y, slot_ct, mod4, g_final)
        x_lat = _combine_lat(l, final, x_new, y, slot_lt, mod4, g_final)

    y_prompt = x_ctx.reshape(BATCH, SEQ, D_MODEL)
    y_sample = x_lat.reshape(DEC_BATCH, DEC_SEQ, D_MODEL)
    new_kv = [t.reshape(BATCH, DEPTH, w // HEAD_DIM, HEAD_DIM, SEQ).transpose(0, 1, 4, 2, 3)
              for t, w in zip(kvt, KV_WIDTHS)]
    return (y_prompt, y_sample, *new_kv)
```

```python
import functools

import numpy as np
import jax
import jax.numpy as jnp
from jax import lax
from jax.experimental import pallas as pl
from jax.experimental.pallas import tpu as pltpu

D_MODEL = 1024
BATCH = 16
SEQ = 256
DEPTH = 2
DEC_BATCH = 2
DEC_SEQ = 2048
PAST_LEN = 256
GRID_W = 64
HEAD_DIM = 64
F_WIDTH = 256
WIN_HEADS = 6
WIN_KV_HEADS = 2
WINDOW = 128
WIN_BLOCK = 128
NA_HEADS = 6
NA_ROWS = 8
NA_COLS = 16
N_EXPERTS = 16
EC_CAPACITY = 2
D_FF = 2816
ROPE_BASE = 10000.0
RMS_EPS = 1e-6
NEG_INF = -1e30
ATTN_SCALE = HEAD_DIM ** -0.5
WIN_Q = WIN_HEADS * HEAD_DIM
WIN_KV = WIN_KV_HEADS * HEAD_DIM
NA_W = NA_HEADS * HEAD_DIM
N_IN = F_WIDTH + WIN_Q + 2 * WIN_KV + 3 * NA_W
SPLITS = (0, F_WIDTH, F_WIDTH + WIN_Q, F_WIDTH + WIN_Q + WIN_KV, F_WIDTH + WIN_Q + 2 * WIN_KV,
          F_WIDTH + WIN_Q + 2 * WIN_KV + NA_W, F_WIDTH + WIN_Q + 2 * WIN_KV + 2 * NA_W, N_IN)

N_CTX = BATCH * SEQ
N_LAT = DEC_BATCH * DEC_SEQ
N_TOK = N_CTX + N_LAT
GRID_ROWS = DEC_SEQ // GRID_W
CAP_CTX = EC_CAPACITY * SEQ // N_EXPERTS
CAP_LAT = EC_CAPACITY * DEC_SEQ // N_EXPERTS
ROWS_CTX = BATCH * CAP_CTX
ROWS_LAT = DEC_BATCH * CAP_LAT
N_COND = 8

LANES = 128
MXU_COLS = 256
TM = 512
TN_MOD = 1536
TF = 256
TD = 256
VMEM_LIMIT = 56 * 1024 * 1024

F32 = jnp.float32
BF16 = jnp.bfloat16


def _params(*sem):
    return pltpu.CompilerParams(dimension_semantics=sem, vmem_limit_bytes=VMEM_LIMIT)


def _dot(a, b):
    return jnp.dot(a, b, preferred_element_type=F32)


def _dot_nt(a, b):
    return lax.dot_general(a, b, (((1,), (1,)), ((), ())), preferred_element_type=F32)


def _split(x):
    hi = x.astype(BF16)
    lo = (x - hi.astype(F32)).astype(BF16)
    return hi, lo


def _dot3(a_hi, a_lo, b_hi, b_lo):
    return _dot(a_hi, b_hi) + (_dot(a_lo, b_hi) + _dot(a_hi, b_lo))


def _silu(x):
    return x / (1.0 + jnp.exp(-x))


def _rms_mod(x, g, shift, scale):
    y = x * lax.rsqrt(jnp.mean(x * x, axis=-1, keepdims=True) + RMS_EPS)
    return (y * g) * (1.0 + scale) + shift


def _softmax_parts(parts, sink=None):
    m = parts[0].max(axis=-1, keepdims=True)
    for s in parts[1:]:
        m = jnp.maximum(m, s.max(axis=-1, keepdims=True))
    if sink is not None:
        m = jnp.maximum(m, sink)
    es = [jnp.exp(s - m) for s in parts]
    den = es[0].sum(axis=-1, keepdims=True)
    for e in es[1:]:
        den = den + e.sum(axis=-1, keepdims=True)
    if sink is not None:
        den = den + jnp.exp(sink - m)
    inv = 1.0 / den
    return [e * inv for e in es]


def _cond_of_block(i):
    n_ctx_blocks = N_CTX // TM
    return jnp.where(i < n_ctx_blocks, 0, 1 + (i - n_ctx_blocks) // (DEC_SEQ // TM))


def _two_stream_specs(width):
    n_ctx_blocks = N_CTX // TM
    ctx = pl.BlockSpec((TM, width), lambda i: (jnp.minimum(i, n_ctx_blocks - 1), 0))
    lat = pl.BlockSpec((TM, width), lambda i: (jnp.maximum(i - n_ctx_blocks, 0), 0))
    return ctx, lat


def _pick_stream(ctx_ref, lat_ref):
    return jnp.where(pl.program_id(0) < N_CTX // TM, ctx_ref[...], lat_ref[...])


def _adaln_kernel(c_ref, w_ref, b_ref, o_ref):
    s_hi, s_lo = _split(_silu(c_ref[...]))
    w_hi, w_lo = _split(w_ref[...])
    o_ref[...] = _dot3(s_hi, s_lo, w_hi, w_lo) + b_ref[...]


def _adaln(cond, w_mod, b_mod):
    return pl.pallas_call(
        _adaln_kernel,
        grid=(DEPTH, 6 * D_MODEL // TN_MOD),
        in_specs=[
            pl.BlockSpec((N_COND, D_MODEL), lambda l, j: (0, 0)),
            pl.BlockSpec((None, D_MODEL, TN_MOD), lambda l, j: (l, 0, j)),
            pl.BlockSpec((None, 1, TN_MOD), lambda l, j: (l, 0, j)),
        ],
        out_specs=pl.BlockSpec((None, N_COND, TN_MOD), lambda l, j: (l, 0, j)),
        out_shape=jax.ShapeDtypeStruct((DEPTH, N_COND, 6 * D_MODEL), F32),
        compiler_params=_params("arbitrary", "arbitrary"),
        name="adaln",
    )(cond, w_mod, b_mod.reshape(DEPTH, 1, 6 * D_MODEL))


KV_NAMES = ("kw", "vw", "kn", "vn")
KV_WIDTHS = (WIN_KV, WIN_KV, NA_W, NA_W)
KV_SPLITS = (SPLITS[2], SPLITS[3], SPLITS[5], SPLITS[6])
KV_TOTAL = sum(KV_WIDTHS)
Q_SPLITS = (SPLITS[0], SPLITS[1], SPLITS[4])


def _proj_kernel(n_prev, xc_ref, xl_ref, mod_ref, g_ref, w_ref, *rest):
    rest = rest[n_prev:]
    f_ref, qw_ref, qn_ref = rest[0:3]
    lat_kv = rest[3:7]
    ctx_kvt = rest[7:11]
    wb_ref, wt_ref = rest[11:13]
    i = pl.program_id(0)

    @pl.when(i == 0)
    def _():
        wb_ref[...] = w_ref[...].astype(BF16)
        r0 = 0
        for c0, width in zip(KV_SPLITS, KV_WIDTHS):
            wt_ref[r0:r0 + width, :] = w_ref[:, c0:c0 + width].T.astype(BF16)
            r0 += width

    natural = dict(zip(SPLITS[:-1], (f_ref, qw_ref) + tuple(lat_kv[0:2]) + (qn_ref,) + tuple(lat_kv[2:4])))

    def hidden(x_ref):
        return _rms_mod(x_ref[...], g_ref[...], mod_ref[:, 0:D_MODEL], mod_ref[:, D_MODEL:2 * D_MODEL]).astype(BF16)

    def project_tiles(h, wanted):
        for t0 in range(0, N_IN, MXU_COLS):
            hits = [(a, b) for a, b in zip(SPLITS[:-1], SPLITS[1:])
                    if a in wanted and max(a, t0) < min(b, t0 + MXU_COLS)]
            if not hits:
                continue
            acc = _dot(h, wb_ref[:, t0:t0 + MXU_COLS])
            for a, b in hits:
                lo, hi = max(a, t0), min(b, t0 + MXU_COLS)
                natural[a][:, lo - a:hi - a] = acc[:, lo - t0:hi - t0].astype(natural[a].dtype)

    @pl.when(i < N_CTX // TM)
    def _():
        h = hidden(xc_ref)
        project_tiles(h, Q_SPLITS)
        kvt = _dot_nt(wt_ref[...], h)
        r0 = 0
        for o_ref, width in zip(ctx_kvt, KV_WIDTHS):
            for bb in range(TM // SEQ):
                o_ref[bb] = kvt[r0:r0 + width, bb * SEQ:(bb + 1) * SEQ]
            r0 += width

    @pl.when(i >= N_CTX // TM)
    def _():
        project_tiles(hidden(xl_ref), Q_SPLITS + KV_SPLITS)


def _project(l, x_ctx, x_lat, mod4, g_mix, w_in, prev_kvt):
    n_ctx_blocks = N_CTX // TM
    xc_spec, xl_spec = _two_stream_specs(D_MODEL)
    both = lambda w: pl.BlockSpec((TM, w), lambda i: (i, 0))
    lat = lambda w: pl.BlockSpec((TM, w), lambda i: (jnp.maximum(i - n_ctx_blocks, 0), 0))
    ctx_t = lambda w: pl.BlockSpec((TM // SEQ, None, w, SEQ),
                                   lambda i: (jnp.minimum(i, n_ctx_blocks - 1), l, 0, 0))
    n_prev = len(prev_kvt)
    n_in = 5
    outs = pl.pallas_call(
        functools.partial(_proj_kernel, n_prev),
        grid=(N_TOK // TM,),
        in_specs=[
            xc_spec, xl_spec,
            pl.BlockSpec((None, None, 1, 6 * D_MODEL), lambda i: (l, _cond_of_block(i), 0, 0)),
            pl.BlockSpec((None, 1, D_MODEL), lambda i: (l, 0, 0)),
            pl.BlockSpec((None, D_MODEL, N_IN), lambda i: (l, 0, 0)),
        ] + [pl.BlockSpec(memory_space=pl.ANY)] * n_prev,
        out_specs=[both(F_WIDTH), both(WIN_Q), both(NA_W)] + [lat(w) for w in KV_WIDTHS]
                  + [ctx_t(w) for w in KV_WIDTHS],
        out_shape=[jax.ShapeDtypeStruct((N_TOK, w), dt) for w, dt in ((F_WIDTH, F32), (WIN_Q, F32), (NA_W, BF16))]
                  + [jax.ShapeDtypeStruct((N_LAT, w), dt)
                     for w, dt in zip(KV_WIDTHS, (F32, BF16, BF16, BF16))]
                  + [jax.ShapeDtypeStruct((BATCH, DEPTH, w, SEQ), F32) for w in KV_WIDTHS],
        input_output_aliases={n_in + k: 7 + k for k in range(n_prev)},
        scratch_shapes=[pltpu.VMEM((D_MODEL, N_IN), BF16), pltpu.VMEM((KV_TOTAL, D_MODEL), BF16)],
        compiler_params=_params("arbitrary"),
        name=f"project{l}",
    )(x_ctx, x_lat, mod4, g_mix.reshape(DEPTH, 1, D_MODEL), w_in, *prev_kvt)
    return outs[0:3], outs[3:7], outs[7:11]


def _lane_is_low(shape):
    return lax.broadcasted_iota(jnp.int32, shape, len(shape) - 1) < HEAD_DIM


def _swap_halves(x):
    return pltpu.roll(x, HEAD_DIM, axis=x.ndim - 1)


def _win_kv_copy(h):
    return 0 if (h // (WIN_HEADS // WIN_KV_HEADS)) == (h % 2) else 1


def _stack_heads(q_pairs, heads):
    low = _lane_is_low(q_pairs[heads[0] // 2].shape)
    rows = [jnp.where(low if h % 2 == 0 else jnp.logical_not(low), q_pairs[h // 2], 0.0).astype(BF16)
            for h in heads]
    return rows[0] if len(rows) == 1 else jnp.concatenate(rows, axis=0)


def _per_head_column(values, rows_per_head):
    blk = lax.broadcasted_iota(jnp.int32, (len(values) * rows_per_head, 1), 0) // rows_per_head
    col = jnp.full(blk.shape, values[0], F32)
    for i in range(1, len(values)):
        col = jnp.where(blk == i, values[i], col)
    return col


def _attend(q_stack, kv_list, extra_logit=None, transposed=False):
    scores = []
    for k, _, post in kv_list:
        s = _dot(q_stack, k) if transposed else _dot_nt(q_stack, k)
        scores.append(post(s) if post is not None else s)
    probs = _softmax_parts(scores, extra_logit)
    o = None
    for p, (_, v, _) in zip(probs, kv_list):
        t = _dot_nt(p.astype(BF16), v) if transposed else _dot(p.astype(BF16), v)
        o = t if o is None else o + t
    return o


def _merge_pair(o_even, o_odd):
    return jnp.where(_lane_is_low(o_even.shape), o_even, o_odd)


def _gqa_attention(q_pairs, rows, kv_for_copy, sinks, transposed=False):
    per_head = {}
    for copy in (0, 1):
        heads = [h for h in range(WIN_HEADS) if _win_kv_copy(h) == copy]
        o = _attend(_stack_heads(q_pairs, heads), kv_for_copy(copy),
                    _per_head_column([sinks[h] for h in heads], rows), transposed)
        for i, h in enumerate(heads):
            per_head[h] = o[i * rows:(i + 1) * rows]
    return [_merge_pair(per_head[2 * j], per_head[2 * j + 1]) for j in range(WIN_HEADS // 2)]


def _swap_row_halves(x):
    return jnp.concatenate([x[HEAD_DIM:], x[:HEAD_DIM]], axis=0)


MIX_GROUP = 2


def _ctx_mixer_kernel(l, sink_ref, f_ref, qw_ref, qn_ref, kw_ref, vw_ref, kn_ref, vn_ref,
                      bc_ref, bs_ref, cs_ref, ss_ref, o_ref):
    bc_hi, bc_lo = _split(bc_ref[...])
    bs_hi, bs_lo = _split(bs_ref[...])
    cs_hi, cs_lo = _split(cs_ref[...])
    ss_hi, ss_lo = _split(ss_ref[...])
    for bb in range(MIX_GROUP):
        rows = slice(bb * SEQ, (bb + 1) * SEQ)
        f_hi, f_lo = _split(f_ref[rows, :])
        fc_hi, fc_lo = _split(_dot3(f_hi, f_lo, bc_hi, bc_lo))
        fs_hi, fs_lo = _split(_dot3(f_hi, f_lo, bs_hi, bs_lo))
        z = _dot3(cs_hi, cs_lo, fc_hi, fc_lo) - _dot3(ss_hi, ss_lo, fs_hi, fs_lo)
        o_ref[rows, 0:F_WIDTH] = (z * (SEQ * HEAD_DIM) ** -0.5).astype(BF16)

        kv = [(kw_ref[bb].astype(BF16), vw_ref[bb].astype(BF16)),
              (_swap_row_halves(kw_ref[bb]).astype(BF16), _swap_row_halves(vw_ref[bb]).astype(BF16))]
        q_pairs = [qw_ref[rows, LANES * j:LANES * (j + 1)] * ATTN_SCALE for j in range(WIN_HEADS // 2)]
        outs = _gqa_attention(q_pairs, SEQ, lambda c: [(kv[c][0], kv[c][1], None)],
                              [sink_ref[l, h] for h in range(WIN_HEADS)], transposed=True)
        for j, o in enumerate(outs):
            o_ref[rows, F_WIDTH + LANES * j:F_WIDTH + LANES * (j + 1)] = o.astype(BF16)

        for j in range(NA_HEADS // 2):
            sl = slice(LANES * j, LANES * (j + 1))
            q_pairs = {j: qn_ref[rows, sl] * ATTN_SCALE}
            o = _attend(_stack_heads(q_pairs, (2 * j, 2 * j + 1)),
                        [(kn_ref[bb, sl, :].astype(BF16), vn_ref[bb, sl, :].astype(BF16), None)], transposed=True)
            base = F_WIDTH + WIN_Q + LANES * j
            o_ref[rows, base:base + LANES] = _merge_pair(o[:SEQ], o[SEQ:]).astype(BF16)


def _ctx_mixer(l, win_sink, f, qw, qn, kvt, dft_ch, dft_seq):
    row = lambda w: pl.BlockSpec((MIX_GROUP * SEQ, w), lambda b: (b, 0))
    col = lambda w: pl.BlockSpec((MIX_GROUP, None, w, SEQ), lambda b: (b, l, 0, 0))
    const = lambda n: pl.BlockSpec((n, n), lambda b: (0, 0))
    return pl.pallas_call(
        functools.partial(_ctx_mixer_kernel, l),
        grid=(BATCH // MIX_GROUP,),
        in_specs=[pl.BlockSpec(memory_space=pltpu.SMEM),
                  row(F_WIDTH), row(WIN_Q), row(NA_W)] + [col(w) for w in KV_WIDTHS]
                 + [const(F_WIDTH)] * 2 + [const(SEQ)] * 2,
        out_specs=pl.BlockSpec((MIX_GROUP * SEQ, D_MODEL), lambda b: (b, 0)),
        out_shape=jax.ShapeDtypeStruct((N_CTX, D_MODEL), BF16),
        compiler_params=_params("arbitrary"),
        name=f"ctx_mixer{l}",
    )(win_sink, f, qw, qn, *kvt, *dft_ch, *dft_seq)


FT_ROWS = 512


def _lat_fourier_kernel(f_ref, bc_ref, bs_ref, cs_ref, ss_ref, o_ref, st_ref):
    b = pl.program_id(1)

    @pl.when(pl.program_id(0) == 0)
    def _():
        f_hi, f_lo = _split(f_ref[pl.ds(pl.multiple_of(b * DEC_SEQ, DEC_SEQ), DEC_SEQ), :])
        fc_hi, fc_lo = _split(_dot3(f_hi, f_lo, *_split(bc_ref[...])))
        fs_hi, fs_lo = _split(_dot3(f_hi, f_lo, *_split(bs_ref[...])))
        st_ref[b, 0] = fc_hi
        st_ref[b, 1] = fc_lo
        st_ref[b, 2] = fs_hi
        st_ref[b, 3] = fs_lo

    z = (_dot3(*_split(cs_ref[...]), st_ref[b, 0], st_ref[b, 1])
         - _dot3(*_split(ss_ref[...]), st_ref[b, 2], st_ref[b, 3]))
    o_ref[...] = (z * (DEC_SEQ * HEAD_DIM) ** -0.5).astype(BF16)


def _lat_fourier(f, dft_ch, dft_seq):
    nrb = DEC_SEQ // FT_ROWS
    const = pl.BlockSpec((F_WIDTH, F_WIDTH), lambda r, b: (0, 0))
    rows = pl.BlockSpec((FT_ROWS, DEC_SEQ), lambda r, b: (r, 0))
    return pl.pallas_call(
        _lat_fourier_kernel,
        grid=(nrb, DEC_BATCH),
        in_specs=[pl.BlockSpec((N_LAT, F_WIDTH), lambda r, b: (1, 0))] + [const] * 2 + [rows] * 2,
        out_specs=pl.BlockSpec((FT_ROWS, F_WIDTH), lambda r, b: (b * nrb + r, 0)),
        out_shape=jax.ShapeDtypeStruct((N_LAT, F_WIDTH), BF16),
        scratch_shapes=[pltpu.VMEM((DEC_BATCH, 4, DEC_SEQ, F_WIDTH), BF16)],
        compiler_params=_params("arbitrary", "arbitrary"),
        name="lat_fourier",
    )(f, *dft_ch, *dft_seq)


def _rope(x, cos, sin_signed):
    n = x.shape[-1]
    lane = lax.broadcasted_iota(jnp.int32, x.shape, x.ndim - 1)
    first = (lane % 32) < 16
    partner = jnp.where(first, pltpu.roll(x, n - 16, axis=x.ndim - 1), pltpu.roll(x, 16, axis=x.ndim - 1))
    return x * cos + partner * sin_signed


def _win_kernel(l, sink_ref, q_ref, k_ref, v_ref, ck_ref, cv_ref, cos_ref, sin_ref, cosq_ref, sinq_ref,
                o_ref, kp_ref, vp_ref, cp_ref):
    n = pl.program_id(1)
    nb = DEC_SEQ // WIN_BLOCK
    pad = WIN_BLOCK

    @pl.when(n == 0)
    def _():
        zeros = jnp.zeros((pad, LANES), BF16)
        kr = _rope(k_ref[...], cos_ref[...], sin_ref[...])
        v = v_ref[...]
        for idx, (kk, vv) in enumerate(((kr, v), (_swap_halves(kr), _swap_halves(v)))):
            kp_ref[idx, 0:pad] = zeros
            kp_ref[idx, pad + DEC_SEQ:] = zeros
            kp_ref[idx, pad:pad + DEC_SEQ] = kk.astype(BF16)
            vp_ref[idx, 0:pad] = zeros
            vp_ref[idx, pad + DEC_SEQ:] = zeros
            vp_ref[idx, pad:pad + DEC_SEQ] = vv.astype(BF16)
        ck = ck_ref[...]
        cv = cv_ref[...]
        cp_ref[0] = ck.astype(BF16)
        cp_ref[1] = _swap_halves(ck).astype(BF16)
        cp_ref[2] = cv.astype(BF16)
        cp_ref[3] = _swap_halves(cv).astype(BF16)

    lo = jnp.where(n == 0, WIN_BLOCK, 0)
    hi = jnp.where(n == nb - 1, 2 * WIN_BLOCK, 3 * WIN_BLOCK)

    i = lax.broadcasted_iota(jnp.int32, (WIN_BLOCK, 3 * WIN_BLOCK), 0)
    j = lax.broadcasted_iota(jnp.int32, (WIN_BLOCK, 3 * WIN_BLOCK), 1)
    mask = (j >= i + WIN_BLOCK - WINDOW) & (j <= i + WIN_BLOCK + WINDOW) & (j >= lo) & (j < hi)
    band_bias = jnp.where(mask, 0.0, NEG_INF)

    def band(s):
        heads = s.shape[0] // WIN_BLOCK
        return (s.reshape(heads, WIN_BLOCK, s.shape[1]) + band_bias[None]).reshape(s.shape)

    start = pl.multiple_of(n * WIN_BLOCK, WIN_BLOCK)
    win = pl.ds(start, 3 * WIN_BLOCK)
    q_pairs = [_rope(q_ref[:, LANES * jp:LANES * (jp + 1)], cosq_ref[...], sinq_ref[...]) * ATTN_SCALE
               for jp in range(WIN_HEADS // 2)]
    outs = _gqa_attention(
        q_pairs, WIN_BLOCK,
        lambda c: [(kp_ref[c, win, :], vp_ref[c, win, :], band), (cp_ref[c], cp_ref[2 + c], None)],
        [sink_ref[l, h] for h in range(WIN_HEADS)])
    for jp, o in enumerate(outs):
        o_ref[:, LANES * jp:LANES * (jp + 1)] = o.astype(BF16)


def _lat_window(l, win_sink, qw, kw, vw, cache_k, cache_v, cos_t, sin_t):
    nb = DEC_SEQ // WIN_BLOCK
    kv_spec = pl.BlockSpec((DEC_SEQ, WIN_KV), lambda b, n: (b, 0))
    cache_spec = pl.BlockSpec((None, None, PAST_LEN, WIN_KV), lambda b, n: (b, l, 0, 0))
    tab_all = pl.BlockSpec((DEC_SEQ, LANES), lambda b, n: (0, 0))
    tab_blk = pl.BlockSpec((WIN_BLOCK, LANES), lambda b, n: (n, 0))
    return pl.pallas_call(
        functools.partial(_win_kernel, l),
        grid=(DEC_BATCH, nb),
        in_specs=[pl.BlockSpec(memory_space=pltpu.SMEM),
                  pl.BlockSpec((WIN_BLOCK, WIN_Q), lambda b, n: (N_CTX // WIN_BLOCK + b * nb + n, 0)),
                  kv_spec, kv_spec, cache_spec, cache_spec, tab_all, tab_all, tab_blk, tab_blk],
        out_specs=pl.BlockSpec((WIN_BLOCK, WIN_Q), lambda b, n: (b * nb + n, 0)),
        out_shape=jax.ShapeDtypeStruct((N_LAT, WIN_Q), BF16),
        scratch_shapes=[pltpu.VMEM((2, DEC_SEQ + 2 * WIN_BLOCK, LANES), BF16),
                        pltpu.VMEM((2, DEC_SEQ + 2 * WIN_BLOCK, LANES), BF16),
                        pltpu.VMEM((4, PAST_LEN, LANES), BF16)],
        compiler_params=_params("arbitrary", "arbitrary"),
        name=f"lat_window{l}",
    )(win_sink, qw, kw, vw, cache_k, cache_v, cos_t, sin_t, cos_t, sin_t)


NA_G = 4
NA_Q = NA_G * GRID_W
NA_WIN_ROWS = NA_ROWS + NA_G
NA_BLOCKS = GRID_ROWS // NA_G
NA_VARIANTS = 3


def _na_block_start(g):
    return jnp.clip(NA_G * g - NA_ROWS // 2, 0, GRID_ROWS - NA_WIN_ROWS)


def _na_kernel(q_ref, k_ref, v_ref, ck_ref, cv_ref, bias_ref, o_ref):
    g = pl.program_id(1)
    start = pl.multiple_of(_na_block_start(g) * GRID_W, GRID_W)
    win = pl.ds(start, NA_WIN_ROWS * GRID_W)
    for jp in range(NA_HEADS // 2):
        sl = slice(LANES * jp, LANES * (jp + 1))
        q_stack = _stack_heads({jp: q_ref[:, sl] * ATTN_SCALE}, (2 * jp, 2 * jp + 1))
        bias = bias_ref[jp]
        o = _attend(q_stack, [(k_ref[win, sl].astype(BF16), v_ref[win, sl].astype(BF16), lambda s: s + bias),
                              (ck_ref[:, sl].astype(BF16), cv_ref[:, sl].astype(BF16), None)])
        o_ref[:, sl] = _merge_pair(o[:NA_Q], o[NA_Q:]).astype(BF16)


def _lat_neighbourhood(l, qn, kn, vn, cache_k, cache_v, bias_tab):
    kv_spec = pl.BlockSpec((DEC_SEQ, NA_W), lambda b, g: (b, 0))
    cache_spec = pl.BlockSpec((None, None, PAST_LEN, NA_W), lambda b, g: (b, l, 0, 0))
    variant = lambda g: jnp.where(g == 0, 0, jnp.where(g == NA_BLOCKS - 1, 2, 1))
    return pl.pallas_call(
        _na_kernel,
        grid=(DEC_BATCH, NA_BLOCKS),
        in_specs=[pl.BlockSpec((NA_Q, NA_W), lambda b, g: (N_CTX // NA_Q + b * NA_BLOCKS + g, 0)),
                  kv_spec, kv_spec, cache_spec, cache_spec,
                  pl.BlockSpec((None, None, NA_HEADS // 2, 2 * NA_Q, NA_WIN_ROWS * GRID_W),
                               lambda b, g: (l, variant(g), 0, 0, 0))],
        out_specs=pl.BlockSpec((NA_Q, NA_W), lambda b, g: (b * NA_BLOCKS + g, 0)),
        out_shape=jax.ShapeDtypeStruct((N_LAT, NA_W), BF16),
        compiler_params=_params("arbitrary", "arbitrary"),
        name=f"lat_neighbourhood{l}",
    )(qn, kn, vn, cache_k, cache_v, bias_tab)


OUTPROJ_ROWS = 256


def _outproj_kernel(xc_ref, xl_ref, mc_ref, mf_ref, mw_ref, mn_ref, mod_ref, g_ref, w_ref, r_hi, r_lo,
                    x_ref, h_ref, lg_ref, wb_ref):
    i = pl.program_id(0)

    @pl.when(i == 0)
    def _():
        wb_ref[...] = w_ref[...].astype(BF16)

    d = D_MODEL

    def block(x_in_ref, mixed_rows):
        for r0 in range(0, TM, OUTPROJ_ROWS):
            rows = slice(r0, r0 + OUTPROJ_ROWS)
            x = x_in_ref[rows, :] + mod_ref[:, 2 * d:3 * d] * _dot(mixed_rows(rows), wb_ref[...])
            x_ref[rows, :] = x
            h = _rms_mod(x, g_ref[...], mod_ref[:, 3 * d:4 * d], mod_ref[:, 4 * d:5 * d])
            h_ref[rows, :] = h.astype(BF16)
            h_hi, h_lo = _split(h)
            lg_ref[:, rows] = _dot3(h_hi, h_lo, r_hi[...], r_lo[...]).T[0:N_EXPERTS, :]

    @pl.when(i < N_CTX // TM)
    def _():
        block(xc_ref, lambda rows: mc_ref[rows, :])

    @pl.when(i >= N_CTX // TM)
    def _():
        block(xl_ref, lambda rows: jnp.concatenate([mf_ref[rows, :], mw_ref[rows, :], mn_ref[rows, :]], axis=1))


def _outproj(l, x_ctx, x_lat, mixed_ctx, mixed_f, mixed_w, mixed_n, mod4, g_ffn, w_out, r_hi, r_lo):
    n_ctx_blocks = N_CTX // TM
    xc_spec, xl_spec = _two_stream_specs(D_MODEL)
    lat = lambda w: pl.BlockSpec((TM, w), lambda i: (jnp.maximum(i - n_ctx_blocks, 0), 0))
    whole = lambda shape: pl.BlockSpec(shape, lambda i: (0,) * len(shape))
    row = lambda w: pl.BlockSpec((TM, w), lambda i: (i, 0))
    return pl.pallas_call(
        _outproj_kernel,
        grid=(N_TOK // TM,),
        in_specs=[xc_spec, xl_spec,
                  pl.BlockSpec((TM, D_MODEL), lambda i: (jnp.minimum(i, n_ctx_blocks - 1), 0)),
                  lat(F_WIDTH), lat(WIN_Q), lat(NA_W),
                  pl.BlockSpec((None, None, 1, 6 * D_MODEL), lambda i: (l, _cond_of_block(i), 0, 0)),
                  pl.BlockSpec((None, 1, D_MODEL), lambda i: (l, 0, 0)),
                  pl.BlockSpec((None, D_MODEL, D_MODEL), lambda i: (l, 0, 0)),
                  whole((D_MODEL, LANES)), whole((D_MODEL, LANES))],
        out_specs=[row(D_MODEL), row(D_MODEL), pl.BlockSpec((N_EXPERTS, TM), lambda i: (0, i))],
        out_shape=[jax.ShapeDtypeStruct((N_TOK, D_MODEL), F32),
                   jax.ShapeDtypeStruct((N_TOK, D_MODEL), BF16),
                   jax.ShapeDtypeStruct((N_EXPERTS, N_TOK), F32)],
        scratch_shapes=[pltpu.VMEM((D_MODEL, D_MODEL), BF16)],
        compiler_params=_params("arbitrary"),
        name=f"outproj{l}",
    )(x_ctx, x_lat, mixed_ctx, mixed_f, mixed_w, mixed_n, mod4, g_ffn.reshape(DEPTH, 1, D_MODEL), w_out,
      r_hi, r_lo)


PREFIX_CHUNK = 256
MANTISSA_STEPS = 44


def _prefix_exclusive(m):
    rows, n = m.shape
    t0 = lax.broadcasted_iota(jnp.int32, (PREFIX_CHUNK, PREFIX_CHUNK), 0)
    t1 = lax.broadcasted_iota(jnp.int32, (PREFIX_CHUNK, PREFIX_CHUNK), 1)
    upper = jnp.where(t0 < t1, 1.0, 0.0).astype(BF16)
    carry = jnp.zeros((rows, 1), F32)
    outs = []
    for c in range(n // PREFIX_CHUNK):
        blk = m[:, c * PREFIX_CHUNK:(c + 1) * PREFIX_CHUNK]
        outs.append(_dot(blk.astype(BF16), upper) + carry)
        carry = carry + blk.sum(axis=-1, keepdims=True)
    return outs[0] if len(outs) == 1 else jnp.concatenate(outs, axis=-1)


def _route_kernel(cap, groups, lg_ref, slot_ref, aff_ref, slott_ref):
    n = lg_ref.shape[1] // groups
    x = jnp.concatenate([lg_ref[:, g * n:(g + 1) * n] for g in range(groups)], axis=0)
    x = x.reshape(groups, N_EXPERTS, n)
    e = jnp.exp(x - x.max(axis=1, keepdims=True))
    aff3 = e / e.sum(axis=1, keepdims=True)
    aff = aff3.reshape(aff3.shape[0] * aff3.shape[1], aff3.shape[2])
    capf = float(cap)

    def count_ge(t):
        return jnp.where(aff >= t, 1.0, 0.0).sum(axis=-1, keepdims=True)

    above = jnp.full((aff.shape[0], 1), 2.0, F32)
    for s in (64, 32, 16, 8, 4, 2, 1):
        cand = above * (2.0 ** -s)
        above = jnp.where(count_ge(cand) >= capf, above, cand)
    base = above * 0.5
    base = jnp.where(count_ge(base) >= capf, base, 0.0)

    def refine(_, carry):
        thr, inc = carry
        cand = thr + inc
        return jnp.where(count_ge(cand) >= capf, cand, thr), inc * 0.5

    thr, _ = lax.fori_loop(0, MANTISSA_STEPS, refine, (base, base * 0.5))
    gt = jnp.where(aff > thr, 1.0, 0.0)
    eq = jnp.where(aff == thr, 1.0, 0.0)
    need = capf - gt.sum(axis=-1, keepdims=True)
    sel = gt + eq * jnp.where(_prefix_exclusive(eq) < need, 1.0, 0.0)
    slot = jnp.where(sel > 0.0, _prefix_exclusive(sel), -1.0)
    slot_ref[...] = slot.astype(jnp.int32)
    aff_ref[...] = aff
    unused = jnp.full((LANES - N_EXPERTS, n), -1.0, F32)
    for g in range(groups):
        tile = jnp.concatenate([slot[g * N_EXPERTS:(g + 1) * N_EXPERTS], unused], axis=0)
        slott_ref[g * n:(g + 1) * n, :] = tile.T.astype(jnp.int32)


def _route(lg_t, stream, groups, cap, name):
    n_tok = N_CTX
    n = n_tok // groups
    rows = groups * N_EXPERTS
    return pl.pallas_call(
        functools.partial(_route_kernel, cap, groups),
        grid=(1,),
        in_specs=[pl.BlockSpec((N_EXPERTS, n_tok), lambda i: (0, stream))],
        out_specs=[pl.BlockSpec((rows, n), lambda i: (0, 0)), pl.BlockSpec((rows, n), lambda i: (0, 0)),
                   pl.BlockSpec((n_tok, LANES), lambda i: (0, 0))],
        out_shape=[jax.ShapeDtypeStruct((rows, n), jnp.int32), jax.ShapeDtypeStruct((rows, n), F32),
                   jax.ShapeDtypeStruct((n_tok, LANES), jnp.int32)],
        compiler_params=_params("arbitrary"),
        name=name,
    )(lg_t)


CTX_GROUP = 4


def _gather_ctx_kernel(h_ref, slot_ref, aff_ref, x_ref, g_ref, p_ref):
    s_iota = lax.broadcasted_iota(jnp.int32, (CAP_CTX, SEQ), 0)
    for bb in range(CTX_GROUP):
        slots = slice(bb * CAP_CTX, (bb + 1) * CAP_CTX)
        for e in range(N_EXPERTS):
            row = bb * N_EXPERTS + e
            hit = s_iota == slot_ref[row:row + 1, :]
            p_ref[bb, e * CAP_CTX:(e + 1) * CAP_CTX, :] = jnp.where(hit, 1.0, 0.0).astype(BF16)
            g_ref[e, slots] = jnp.where(hit, aff_ref[row:row + 1, :], 0.0).sum(axis=-1, keepdims=True)
        x = _dot(p_ref[bb], h_ref[bb * SEQ:(bb + 1) * SEQ, :]).astype(BF16)
        x_ref[:, slots, :] = x.reshape(N_EXPERTS, CAP_CTX, D_MODEL)


def _gather_ctx(h, slot, aff):
    return pl.pallas_call(
        _gather_ctx_kernel,
        grid=(BATCH // CTX_GROUP,),
        in_specs=[pl.BlockSpec((CTX_GROUP * SEQ, D_MODEL), lambda b: (b, 0)),
                  pl.BlockSpec((CTX_GROUP * N_EXPERTS, SEQ), lambda b: (b, 0)),
                  pl.BlockSpec((CTX_GROUP * N_EXPERTS, SEQ), lambda b: (b, 0))],
        out_specs=[pl.BlockSpec((N_EXPERTS, CTX_GROUP * CAP_CTX, D_MODEL), lambda b: (0, b, 0)),
                   pl.BlockSpec((N_EXPERTS, CTX_GROUP * CAP_CTX, 1), lambda b: (0, b, 0))],
        out_shape=[jax.ShapeDtypeStruct((N_EXPERTS, ROWS_CTX, D_MODEL), BF16),
                   jax.ShapeDtypeStruct((N_EXPERTS, ROWS_CTX, 1), F32)],
        scratch_shapes=[pltpu.VMEM((CTX_GROUP, N_EXPERTS * CAP_CTX, SEQ), BF16)],
        compiler_params=_params("arbitrary"),
        name="gather_ctx",
    )(h, slot, aff)


LAT_GATHER_EXPERTS = 2


def _gather_lat_kernel(h_ref, slot_ref, aff_ref, x_ref, g_ref):
    s_iota = lax.broadcasted_iota(jnp.int32, (CAP_LAT, DEC_SEQ), 0)
    for k in range(LAT_GATHER_EXPERTS):
        e = pl.program_id(1) * LAT_GATHER_EXPERTS + k
        hit = s_iota == slot_ref[pl.ds(e, 1), :]
        x_ref[k] = _dot(jnp.where(hit, 1.0, 0.0).astype(BF16), h_ref[...]).astype(BF16)
        g_ref[k] = jnp.where(hit, aff_ref[pl.ds(e, 1), :], 0.0).sum(axis=-1, keepdims=True)


def _gather_lat(h, slot, aff):
    lat_blk0 = N_CTX // DEC_SEQ
    return pl.pallas_call(
        _gather_lat_kernel,
        grid=(DEC_BATCH, N_EXPERTS // LAT_GATHER_EXPERTS),
        in_specs=[pl.BlockSpec((DEC_SEQ, D_MODEL), lambda b, e: (lat_blk0 + b, 0)),
                  pl.BlockSpec((N_EXPERTS, DEC_SEQ), lambda b, e: (b, 0)),
                  pl.BlockSpec((N_EXPERTS, DEC_SEQ), lambda b, e: (b, 0))],
        out_specs=[pl.BlockSpec((LAT_GATHER_EXPERTS, CAP_LAT, D_MODEL), lambda b, e: (e, b, 0)),
                   pl.BlockSpec((LAT_GATHER_EXPERTS, CAP_LAT, 1), lambda b, e: (e, b, 0))],
        out_shape=[jax.ShapeDtypeStruct((N_EXPERTS, ROWS_LAT, D_MODEL), BF16),
                   jax.ShapeDtypeStruct((N_EXPERTS, ROWS_LAT, 1), F32)],
        compiler_params=_params("arbitrary", "arbitrary"),
        name="gather_lat",
    )(h, slot, aff)


N_FF_CHUNKS = D_FF // TF
CHUNKS_PER_STEP = 2
UP_BLOCK = CHUNKS_PER_STEP * TF
N_UP_STEPS = -(-N_FF_CHUNKS // CHUNKS_PER_STEP)
LAST_STEP_CHUNKS = N_FF_CHUNKS - (N_UP_STEPS - 1) * CHUNKS_PER_STEP


def _silu_tanh(x):
    return x * (0.5 + 0.5 * jnp.tanh(0.5 * x))


def _ffn_kernel(xc_ref, xl_ref, gc_ref, gl_ref, wg_ref, wu_ref, wd_ref, y_ref, x_sc, h_sc):
    j = pl.program_id(1)

    @pl.when(j == 0)
    def _():
        x_sc[0:ROWS_CTX, :] = xc_ref[...]
        x_sc[ROWS_CTX:, :] = xl_ref[...]

    def up(n_chunks):
        x = x_sc[...]
        for sub in range(n_chunks):
            cols = slice(sub * TF, (sub + 1) * TF)
            a = _dot(x, wg_ref[:, cols].astype(BF16))
            u = _dot(x, wu_ref[:, cols].astype(BF16))
            h_sc[j * CHUNKS_PER_STEP + sub] = (_silu_tanh(a) * u).astype(BF16)

    pl.when(j < N_UP_STEPS - 1)(functools.partial(up, CHUNKS_PER_STEP))
    pl.when(j == N_UP_STEPS - 1)(functools.partial(up, LAST_STEP_CHUNKS))

    @pl.when(j >= N_UP_STEPS)
    def _():
        for c0 in range(0, D_MODEL, TD):
            acc = None
            for k in range(N_FF_CHUNKS):
                t = _dot(h_sc[k], wd_ref[k * TF:(k + 1) * TF, c0:c0 + TD].astype(BF16))
                acc = t if acc is None else acc + t
            y_ref[0:ROWS_CTX, c0:c0 + TD] = (acc[0:ROWS_CTX] * gc_ref[...]).astype(BF16)
            y_ref[ROWS_CTX:, c0:c0 + TD] = (acc[ROWS_CTX:] * gl_ref[...]).astype(BF16)


def _ffn(l, x_c, x_l, g_c, g_l, w_gate, w_up, w_down):
    rows = ROWS_CTX + ROWS_LAT
    xin = lambda r: pl.BlockSpec((None, r, D_MODEL), lambda e, j: (e, 0, 0))
    gin = lambda r: pl.BlockSpec((None, r, 1), lambda e, j: (e, 0, 0))
    up_chunk = lambda e, j: (l, e, 0, jnp.minimum(j, N_UP_STEPS - 1))
    return pl.pallas_call(
        _ffn_kernel,
        grid=(N_EXPERTS, N_UP_STEPS + 1),
        in_specs=[xin(ROWS_CTX), xin(ROWS_LAT), gin(ROWS_CTX), gin(ROWS_LAT),
                  pl.BlockSpec((None, None, D_MODEL, UP_BLOCK), up_chunk),
                  pl.BlockSpec((None, None, D_MODEL, UP_BLOCK), up_chunk),
                  pl.BlockSpec((None, None, D_FF, D_MODEL), lambda e, j: (l, e, 0, 0))],
        out_specs=pl.BlockSpec((None, rows, D_MODEL), lambda e, j: (e, 0, 0)),
        out_shape=jax.ShapeDtypeStruct((N_EXPERTS, rows, D_MODEL), BF16),
        scratch_shapes=[pltpu.VMEM((rows, D_MODEL), BF16), pltpu.VMEM((N_FF_CHUNKS, rows, TF), BF16)],
        compiler_params=_params("arbitrary", "arbitrary"),
        name=f"experts{l}",
    )(x_c, x_l, g_c, g_l, w_gate, w_up, w_down)


def _finish(x, res, mod_ref, gf_ref, final):
    y = x + mod_ref[:, 5 * D_MODEL:] * res
    if final:
        y = y * lax.rsqrt(jnp.mean(y * y, axis=-1, keepdims=True) + RMS_EPS) * gf_ref[...]
    return y


def _combine_ctx_kernel(final, x_ref, y_ref, slot_ref, rep_ref, mod_ref, gf_ref, o_ref):
    n_col = N_EXPERTS * CAP_CTX
    col = (lax.broadcasted_iota(jnp.int32, (SEQ, n_col), 1) % CAP_CTX).astype(F32)
    for bb in range(CTX_GROUP):
        rows = slice(bb * SEQ, (bb + 1) * SEQ)
        spread = _dot(slot_ref[rows, :].astype(F32).astype(BF16), rep_ref[...])
        p = jnp.where(spread == col, 1.0, 0.0).astype(BF16)
        y = y_ref[:, bb * CAP_CTX:(bb + 1) * CAP_CTX, :].reshape(n_col, D_MODEL)
        o_ref[rows, :] = _finish(x_ref[rows, :], _dot(p, y), mod_ref, gf_ref, final)


def _combine_ctx(l, final, x_new, y, slot_t, mod4, g_final):
    n_col = N_EXPERTS * CAP_CTX
    rep = (np.arange(n_col)[None, :] // CAP_CTX == np.arange(LANES)[:, None]).astype(np.float32)
    return pl.pallas_call(
        functools.partial(_combine_ctx_kernel, final),
        grid=(BATCH // CTX_GROUP,),
        in_specs=[pl.BlockSpec((CTX_GROUP * SEQ, D_MODEL), lambda b: (b, 0)),
                  pl.BlockSpec((N_EXPERTS, CTX_GROUP * CAP_CTX, D_MODEL), lambda b: (0, b, 0)),
                  pl.BlockSpec((CTX_GROUP * SEQ, LANES), lambda b: (b, 0)),
                  pl.BlockSpec((LANES, n_col), lambda b: (0, 0)),
                  pl.BlockSpec((None, None, 1, 6 * D_MODEL), lambda b: (l, 0, 0, 0)),
                  pl.BlockSpec((1, D_MODEL), lambda b: (0, 0))],
        out_specs=pl.BlockSpec((CTX_GROUP * SEQ, D_MODEL), lambda b: (b, 0)),
        out_shape=jax.ShapeDtypeStruct((N_CTX, D_MODEL), F32),
        compiler_params=_params("arbitrary"),
        name=f"combine_ctx{l}",
    )(x_new, y, slot_t, jnp.asarray(rep, BF16), mod4, g_final.reshape(1, D_MODEL))


TMC = 512


def _combine_lat_kernel(final, x_ref, y_ref, slot_ref, mod_ref, gf_ref, o_ref):
    s_iota = lax.broadcasted_iota(jnp.int32, (TMC, CAP_LAT), 1)
    slot = slot_ref[...]
    res = None
    for e in range(N_EXPERTS):
        p = jnp.where(slot[:, e:e + 1] == s_iota, 1.0, 0.0).astype(BF16)
        t = _dot(p, y_ref[e])
        res = t if res is None else res + t
    o_ref[...] = _finish(x_ref[...], res, mod_ref, gf_ref, final)


def _combine_lat(l, final, x_new, y, slot_t, mod4, g_final):
    nt = DEC_SEQ // TMC
    return pl.pallas_call(
        functools.partial(_combine_lat_kernel, final),
        grid=(DEC_BATCH, nt),
        in_specs=[pl.BlockSpec((TMC, D_MODEL), lambda b, t: (N_CTX // TMC + b * nt + t, 0)),
                  pl.BlockSpec((N_EXPERTS, CAP_LAT, D_MODEL), lambda b, t: (0, ROWS_CTX // CAP_LAT + b, 0)),
                  pl.BlockSpec((TMC, LANES), lambda b, t: (b * nt + t, 0)),
                  pl.BlockSpec((None, None, 1, 6 * D_MODEL), lambda b, t: (l, 1 + b, 0, 0)),
                  pl.BlockSpec((1, D_MODEL), lambda b, t: (0, 0))],
        out_specs=pl.BlockSpec((TMC, D_MODEL), lambda b, t: (b * nt + t, 0)),
        out_shape=jax.ShapeDtypeStruct((N_LAT, D_MODEL), F32),
        compiler_params=_params("arbitrary", "arbitrary"),
        name=f"combine_lat{l}",
    )(x_new, y, slot_t, mod4, g_final.reshape(1, D_MODEL))


def _split_table(t):
    hi = t.astype(BF16)
    return hi, (t - hi.astype(F32)).astype(BF16)


def _dft_tables(n):
    p = np.arange(n, dtype=np.int64)
    ang = ((p[:, None] * p[None, :]) % n).astype(np.float64) * (2.0 * np.pi / n)
    return np.cos(ang).astype(np.float32), np.sin(ang).astype(np.float32)


def _channel_dft_tables():
    c = np.arange(F_WIDTH, dtype=np.int64)
    same = (c[:, None] // HEAD_DIM) == (c[None, :] // HEAD_DIM)
    ang = (((c[:, None] % HEAD_DIM) * (c[None, :] % HEAD_DIM)) % HEAD_DIM).astype(np.float64) * (2.0 * np.pi / HEAD_DIM)
    return (np.where(same, np.cos(ang), 0.0).astype(np.float32),
            np.where(same, np.sin(ang), 0.0).astype(np.float32))


def _rope_tables():
    half = HEAD_DIM // 2
    nf = half // 2
    pos = np.arange(DEC_SEQ)
    inv = 1.0 / (ROPE_BASE ** (np.arange(nf, dtype=np.float64) / nf))
    ang_r = (pos // GRID_W).astype(np.float64)[:, None] * inv
    ang_c = (pos % GRID_W).astype(np.float64)[:, None] * inv

    def head(fn, sign):
        return np.concatenate([sign * fn(ang_r), fn(ang_r), sign * fn(ang_c), fn(ang_c)], axis=-1)

    cos = head(np.cos, 1.0)
    sin = head(np.sin, -1.0)
    return (np.concatenate([cos, cos], axis=-1).astype(np.float32),
            np.concatenate([sin, sin], axis=-1).astype(np.float32))


def _na_bias_tables(rpb):
    cq = np.arange(GRID_W)
    rel_c = np.clip(cq[None, :] - cq[:, None] + NA_COLS - 1, 0, 2 * NA_COLS - 2)
    pick = (rel_c[:, :, None] == np.arange(2 * NA_COLS - 1)).astype(np.float32)
    cs = np.clip(cq - NA_COLS // 2, 0, GRID_W - NA_COLS)
    col_ok = (cq[None, :] >= cs[:, None]) & (cq[None, :] < cs[:, None] + NA_COLS)
    bc = jnp.einsum('lhrj,qkj->lhrqk', rpb, pick, precision=lax.Precision.HIGHEST)
    bc = jnp.where(col_ok[None, None, None], bc, NEG_INF)
    return pl.pallas_call(
        _na_bias_kernel,
        grid=(DEPTH, NA_VARIANTS, NA_HEADS // 2),
        in_specs=[pl.BlockSpec((None, 2, 2 * NA_ROWS - 1, GRID_W, GRID_W), lambda l, v, p: (l, p, 0, 0, 0))],
        out_specs=pl.BlockSpec((None, None, None, 2 * NA_Q, NA_WIN_ROWS * GRID_W),
                               lambda l, v, p: (l, v, p, 0, 0)),
        out_shape=jax.ShapeDtypeStruct((DEPTH, NA_VARIANTS, NA_HEADS // 2, 2 * NA_Q, NA_WIN_ROWS * GRID_W), F32),
        compiler_params=_params("arbitrary", "arbitrary", "arbitrary"),
        name="na_bias",
    )(bc)


def _na_window_plan():
    plan = []
    for g in (0, 1, NA_BLOCKS - 1):
        start = int(np.clip(NA_G * g - NA_ROWS // 2, 0, GRID_ROWS - NA_WIN_ROWS))
        rows = []
        for a in range(NA_G):
            r = NA_G * g + a
            rs = int(np.clip(r - NA_ROWS // 2, 0, GRID_ROWS - NA_ROWS))
            rows.append([start + w - r + NA_ROWS - 1 if rs <= start + w < rs + NA_ROWS else None
                         for w in range(NA_WIN_ROWS)])
        plan.append(rows)
    return plan


def _na_bias_kernel(bc_ref, o_ref):
    outside = jnp.full((GRID_W, GRID_W), NEG_INF, F32)
    for v, rows in enumerate(_na_window_plan()):
        @pl.when(pl.program_id(1) == v)
        def _():
            for half in range(2):
                for a, rel in enumerate(rows):
                    tiles = [outside if rr is None else bc_ref[half, rr] for rr in rel]
                    r0 = half * NA_Q + a * GRID_W
                    o_ref[r0:r0 + GRID_W, :] = jnp.concatenate(tiles, axis=-1)


def kernel(x_prompt, x_sample, cache_win_k, cache_win_v, cache_nat_k, cache_nat_v, c, c_ctx, w_mod, b_mod, g_mix, g_ffn, w_in, w_out, win_sink, nat_rpb, w_router, w_gate, w_up, w_down, g_final):
    x_ctx = x_prompt.reshape(N_CTX, D_MODEL)
    x_lat = x_sample.reshape(N_LAT, D_MODEL)
    cond = jnp.concatenate([c_ctx[None, :], c, jnp.zeros((N_COND - 1 - DEC_BATCH, D_MODEL), F32)], axis=0)
    mod4 = _adaln(cond, w_mod, b_mod).reshape(DEPTH, N_COND, 1, 6 * D_MODEL)

    dft_ch = _channel_dft_tables()
    dft_ctx = _dft_tables(SEQ)
    dft_lat = _dft_tables(DEC_SEQ)
    cos_t, sin_t = _rope_tables()
    cwk = cache_win_k.reshape(DEC_BATCH, DEPTH, PAST_LEN, WIN_KV)
    cwv = cache_win_v.reshape(DEC_BATCH, DEPTH, PAST_LEN, WIN_KV)
    cnk = cache_nat_k.reshape(DEC_BATCH, DEPTH, PAST_LEN, NA_W)
    cnv = cache_nat_v.reshape(DEC_BATCH, DEPTH, PAST_LEN, NA_W)
    r_pad = jnp.pad(w_router, ((0, 0), (0, 0), (0, LANES - N_EXPERTS)))
    na_bias = _na_bias_tables(nat_rpb)

    kvt = ()
    for l in range(DEPTH):
        final = l == DEPTH - 1
        (f, qw, qn), (kw, vw, kn, vn), kvt = _project(l, x_ctx, x_lat, mod4, g_mix, w_in, kvt)

        mixed_ctx = _ctx_mixer(l, win_sink, f, qw, qn, kvt, dft_ch, dft_ctx)
        mixed_f = _lat_fourier(f, dft_ch, dft_lat)
        mixed_w = _lat_window(l, win_sink, qw, kw, vw, cwk, cwv, cos_t, sin_t)
        mixed_n = _lat_neighbourhood(l, qn, kn, vn, cnk, cnv, na_bias)

        r_hi, r_lo = _split_table(r_pad[l])
        x_new, h, lg_t = _outproj(l, x_ctx, x_lat, mixed_ctx, mixed_f, mixed_w, mixed_n, mod4, g_ffn,
                                  w_out, r_hi, r_lo)

        slot_c, aff_c, slot_ct = _route(lg_t, 0, BATCH, CAP_CTX, "route_ctx")
        slot_l, aff_l, slot_lt = _route(lg_t, 1, DEC_BATCH, CAP_LAT, "route_lat")
        xg_c, gate_c = _gather_ctx(h, slot_c, aff_c)
        xg_l, gate_l = _gather_lat(h, slot_l, aff_l)
        y = _ffn(l, xg_c, xg_l, gate_c, gate_l, w_gate, w_up, w_down)
        x_ctx = _combine_ctx(l, final, x_new, y, slot_ct, mod4, g_final)
        x_lat = _combine_lat(l, final, x_new, y, slot_lt, mod4, g_final)

    y_prompt = x_ctx.reshape(BATCH, SEQ, D_MODEL)
    y_sample = x_lat.reshape(DEC_BATCH, DEC_SEQ, D_MODEL)
    new_kv = [t.reshape(BATCH, DEPTH, w // HEAD_DIM, HEAD_DIM, SEQ).transpose(0, 1, 4, 2, 3)
              for t, w in zip(kvt, KV_WIDTHS)]
    return (y_prompt, y_sample, *new_kv)
```

```python
import functools

import numpy as np
import jax
import jax.numpy as jnp
from jax import lax
from jax.experimental import pallas as pl
from jax.experimental.pallas import tpu as pltpu

D_MODEL = 1024
BATCH = 16
SEQ = 256
DEPTH = 2
DEC_BATCH = 2
DEC_SEQ = 2048
PAST_LEN = 256
GRID_W = 64
HEAD_DIM = 64
F_WIDTH = 256
WIN_HEADS = 6
WIN_KV_HEADS = 2
WINDOW = 128
WIN_BLOCK = 128
NA_HEADS = 6
NA_ROWS = 8
NA_COLS = 16
N_EXPERTS = 16
EC_CAPACITY = 2
D_FF = 2816
ROPE_BASE = 10000.0
RMS_EPS = 1e-6
NEG_INF = -1e30
ATTN_SCALE = HEAD_DIM ** -0.5
WIN_Q = WIN_HEADS * HEAD_DIM
WIN_KV = WIN_KV_HEADS * HEAD_DIM
NA_W = NA_HEADS * HEAD_DIM
N_IN = F_WIDTH + WIN_Q + 2 * WIN_KV + 3 * NA_W
SPLITS = (0, F_WIDTH, F_WIDTH + WIN_Q, F_WIDTH + WIN_Q + WIN_KV, F_WIDTH + WIN_Q + 2 * WIN_KV,
          F_WIDTH + WIN_Q + 2 * WIN_KV + NA_W, F_WIDTH + WIN_Q + 2 * WIN_KV + 2 * NA_W, N_IN)

N_CTX = BATCH * SEQ
N_LAT = DEC_BATCH * DEC_SEQ
N_TOK = N_CTX + N_LAT
GRID_ROWS = DEC_SEQ // GRID_W
CAP_CTX = EC_CAPACITY * SEQ // N_EXPERTS
CAP_LAT = EC_CAPACITY * DEC_SEQ // N_EXPERTS
ROWS_CTX = BATCH * CAP_CTX
ROWS_LAT = DEC_BATCH * CAP_LAT
N_COND = 8

LANES = 128
MXU_COLS = 256
TM = 512
TN_MOD = 1536
TF = 256
TD = 256
VMEM_LIMIT = 56 * 1024 * 1024

F32 = jnp.float32
BF16 = jnp.bfloat16


def _params(*sem):
    return pltpu.CompilerParams(dimension_semantics=sem, vmem_limit_bytes=VMEM_LIMIT)


def _dot(a, b):
    return jnp.dot(a, b, preferred_element_type=F32)


def _dot_nt(a, b):
    return lax.dot_general(a, b, (((1,), (1,)), ((), ())), preferred_element_type=F32)


def _split(x):
    hi = x.astype(BF16)
    lo = (x - hi.astype(F32)).astype(BF16)
    return hi, lo


def _dot3(a_hi, a_lo, b_hi, b_lo):
    return _dot(a_hi, b_hi) + (_dot(a_lo, b_hi) + _dot(a_hi, b_lo))


def _silu(x):
    return x / (1.0 + jnp.exp(-x))


def _rms_mod(x, g, shift, scale):
    y = x * lax.rsqrt(jnp.mean(x * x, axis=-1, keepdims=True) + RMS_EPS)
    return (y * g) * (1.0 + scale) + shift


def _softmax_parts(parts, sink=None):
    m = parts[0].max(axis=-1, keepdims=True)
    for s in parts[1:]:
        m = jnp.maximum(m, s.max(axis=-1, keepdims=True))
    if sink is not None:
        m = jnp.maximum(m, sink)
    es = [jnp.exp(s - m) for s in parts]
    den = es[0].sum(axis=-1, keepdims=True)
    for e in es[1:]:
        den = den + e.sum(axis=-1, keepdims=True)
    if sink is not None:
        den = den + jnp.exp(sink - m)
    inv = 1.0 / den
    return [e * inv for e in es]


def _cond_of_block(i):
    n_ctx_blocks = N_CTX // TM
    return jnp.where(i < n_ctx_blocks, 0, 1 + (i - n_ctx_blocks) // (DEC_SEQ // TM))


def _two_stream_specs(width):
    n_ctx_blocks = N_CTX // TM
    ctx = pl.BlockSpec((TM, width), lambda i: (jnp.minimum(i, n_ctx_blocks - 1), 0))
    lat = pl.BlockSpec((TM, width), lambda i: (jnp.maximum(i - n_ctx_blocks, 0), 0))
    return ctx, lat


def _pick_stream(ctx_ref, lat_ref):
    return jnp.where(pl.program_id(0) < N_CTX // TM, ctx_ref[...], lat_ref[...])


def _adaln_kernel(c_ref, w_ref, b_ref, o_ref):
    s_hi, s_lo = _split(_silu(c_ref[...]))
    w_hi, w_lo = _split(w_ref[...])
    o_ref[...] = _dot3(s_hi, s_lo, w_hi, w_lo) + b_ref[...]


def _adaln(cond, w_mod, b_mod):
    return pl.pallas_call(
        _adaln_kernel,
        grid=(DEPTH, 6 * D_MODEL // TN_MOD),
        in_specs=[
            pl.BlockSpec((N_COND, D_MODEL), lambda l, j: (0, 0)),
            pl.BlockSpec((None, D_MODEL, TN_MOD), lambda l, j: (l, 0, j)),
            pl.BlockSpec((None, 1, TN_MOD), lambda l, j: (l, 0, j)),
        ],
        out_specs=pl.BlockSpec((None, N_COND, TN_MOD), lambda l, j: (l, 0, j)),
        out_shape=jax.ShapeDtypeStruct((DEPTH, N_COND, 6 * D_MODEL), F32),
        compiler_params=_params("arbitrary", "arbitrary"),
        name="adaln",
    )(cond, w_mod, b_mod.reshape(DEPTH, 1, 6 * D_MODEL))


KV_NAMES = ("kw", "vw", "kn", "vn")
KV_WIDTHS = (WIN_KV, WIN_KV, NA_W, NA_W)
KV_SPLITS = (SPLITS[2], SPLITS[3], SPLITS[5], SPLITS[6])
KV_TOTAL = sum(KV_WIDTHS)
Q_SPLITS = (SPLITS[0], SPLITS[1], SPLITS[4])


def _proj_kernel(n_prev, xc_ref, xl_ref, mod_ref, g_ref, w_ref, *rest):
    rest = rest[n_prev:]
    f_ref, qw_ref, qn_ref = rest[0:3]
    lat_kv = rest[3:7]
    ctx_kvt = rest[7:11]
    wb_ref, wt_ref = rest[11:13]
    i = pl.program_id(0)

    @pl.when(i == 0)
    def _():
        wb_ref[...] = w_ref[...].astype(BF16)
        r0 = 0
        for c0, width in zip(KV_SPLITS, KV_WIDTHS):
            wt_ref[r0:r0 + width, :] = w_ref[:, c0:c0 + width].T.astype(BF16)
            r0 += width

    natural = dict(zip(SPLITS[:-1], (f_ref, qw_ref) + tuple(lat_kv[0:2]) + (qn_ref,) + tuple(lat_kv[2:4])))

    def hidden(x_ref):
        return _rms_mod(x_ref[...], g_ref[...], mod_ref[:, 0:D_MODEL], mod_ref[:, D_MODEL:2 * D_MODEL]).astype(BF16)

    def project_tiles(h, wanted):
        for t0 in range(0, N_IN, MXU_COLS):
            hits = [(a, b) for a, b in zip(SPLITS[:-1], SPLITS[1:])
                    if a in wanted and max(a, t0) < min(b, t0 + MXU_COLS)]
            if not hits:
                continue
            acc = _dot(h, wb_ref[:, t0:t0 + MXU_COLS])
            for a, b in hits:
                lo, hi = max(a, t0), min(b, t0 + MXU_COLS)
                natural[a][:, lo - a:hi - a] = acc[:, lo - t0:hi - t0].astype(natural[a].dtype)

    @pl.when(i < N_CTX // TM)
    def _():
        h = hidden(xc_ref)
        project_tiles(h, Q_SPLITS)
        kvt = _dot_nt(wt_ref[...], h)
        r0 = 0
        for o_ref, width in zip(ctx_kvt, KV_WIDTHS):
            for bb in range(TM // SEQ):
                o_ref[bb] = kvt[r0:r0 + width, bb * SEQ:(bb + 1) * SEQ]
            r0 += width

    @pl.when(i >= N_CTX // TM)
    def _():
        project_tiles(hidden(xl_ref), Q_SPLITS + KV_SPLITS)


def _project(l, x_ctx, x_lat, mod4, g_mix, w_in, prev_kvt):
    n_ctx_blocks = N_CTX // TM
    xc_spec, xl_spec = _two_stream_specs(D_MODEL)
    both = lambda w: pl.BlockSpec((TM, w), lambda i: (i, 0))
    lat = lambda w: pl.BlockSpec((TM, w), lambda i: (jnp.maximum(i - n_ctx_blocks, 0), 0))
    ctx_t = lambda w: pl.BlockSpec((TM // SEQ, None, w, SEQ),
                                   lambda i: (jnp.minimum(i, n_ctx_blocks - 1), l, 0, 0))
    n_prev = len(prev_kvt)
    n_in = 5
    outs = pl.pallas_call(
        functools.partial(_proj_kernel, n_prev),
        grid=(N_TOK // TM,),
        in_specs=[
            xc_spec, xl_spec,
            pl.BlockSpec((None, None, 1, 6 * D_MODEL), lambda i: (l, _cond_of_block(i), 0, 0)),
            pl.BlockSpec((None, 1, D_MODEL), lambda i: (l, 0, 0)),
            pl.BlockSpec((None, D_MODEL, N_IN), lambda i: (l, 0, 0)),
        ] + [pl.BlockSpec(memory_space=pl.ANY)] * n_prev,
        out_specs=[both(F_WIDTH), both(WIN_Q), both(NA_W)] + [lat(w) for w in KV_WIDTHS]
                  + [ctx_t(w) for w in KV_WIDTHS],
        out_shape=[jax.ShapeDtypeStruct((N_TOK, w), dt) for w, dt in ((F_WIDTH, F32), (WIN_Q, F32), (NA_W, BF16))]
                  + [jax.ShapeDtypeStruct((N_LAT, w), dt)
                     for w, dt in zip(KV_WIDTHS, (F32, BF16, BF16, BF16))]
                  + [jax.ShapeDtypeStruct((BATCH, DEPTH, w, SEQ), F32) for w in KV_WIDTHS],
        input_output_aliases={n_in + k: 7 + k for k in range(n_prev)},
        scratch_shapes=[pltpu.VMEM((D_MODEL, N_IN), BF16), pltpu.VMEM((KV_TOTAL, D_MODEL), BF16)],
        compiler_params=_params("arbitrary"),
        name=f"project{l}",
    )(x_ctx, x_lat, mod4, g_mix.reshape(DEPTH, 1, D_MODEL), w_in, *prev_kvt)
    return outs[0:3], outs[3:7], outs[7:11]


def _lane_is_low(shape):
    return lax.broadcasted_iota(jnp.int32, shape, len(shape) - 1) < HEAD_DIM


def _swap_halves(x):
    return pltpu.roll(x, HEAD_DIM, axis=x.ndim - 1)


def _win_kv_copy(h):
    return 0 if (h // (WIN_HEADS // WIN_KV_HEADS)) == (h % 2) else 1


def _stack_heads(q_pairs, heads):
    low = _lane_is_low(q_pairs[heads[0] // 2].shape)
    rows = [jnp.where(low if h % 2 == 0 else jnp.logical_not(low), q_pairs[h // 2], 0.0).astype(BF16)
            for h in heads]
    return rows[0] if len(rows) == 1 else jnp.concatenate(rows, axis=0)


def _per_head_column(values, rows_per_head):
    blk = lax.broadcasted_iota(jnp.int32, (len(values) * rows_per_head, 1), 0) // rows_per_head
    col = jnp.full(blk.shape, values[0], F32)
    for i in range(1, len(values)):
        col = jnp.where(blk == i, values[i], col)
    return col


def _attend(q_stack, kv_list, extra_logit=None, transposed=False):
    scores = []
    for k, _, post in kv_list:
        s = _dot(q_stack, k) if transposed else _dot_nt(q_stack, k)
        scores.append(post(s) if post is not None else s)
    probs = _softmax_parts(scores, extra_logit)
    o = None
    for p, (_, v, _) in zip(probs, kv_list):
        t = _dot_nt(p.astype(BF16), v) if transposed else _dot(p.astype(BF16), v)
        o = t if o is None else o + t
    return o


def _merge_pair(o_even, o_odd):
    return jnp.where(_lane_is_low(o_even.shape), o_even, o_odd)


def _gqa_attention(q_pairs, rows, kv_for_copy, sinks, transposed=False):
    per_head = {}
    for copy in (0, 1):
        heads = [h for h in range(WIN_HEADS) if _win_kv_copy(h) == copy]
        o = _attend(_stack_heads(q_pairs, heads), kv_for_copy(copy),
                    _per_head_column([sinks[h] for h in heads], rows), transposed)
        for i, h in enumerate(heads):
            per_head[h] = o[i * rows:(i + 1) * rows]
    return [_merge_pair(per_head[2 * j], per_head[2 * j + 1]) for j in range(WIN_HEADS // 2)]


def _swap_row_halves(x):
    return jnp.concatenate([x[HEAD_DIM:], x[:HEAD_DIM]], axis=0)


MIX_GROUP = 2


def _ctx_mixer_kernel(l, sink_ref, f_ref, qw_ref, qn_ref, kw_ref, vw_ref, kn_ref, vn_ref,
                      bc_ref, bs_ref, cs_ref, ss_ref, o_ref):
    bc_hi, bc_lo = _split(bc_ref[...])
    bs_hi, bs_lo = _split(bs_ref[...])
    cs_hi, cs_lo = _split(cs_ref[...])
    ss_hi, ss_lo = _split(ss_ref[...])
    for bb in range(MIX_GROUP):
        rows = slice(bb * SEQ, (bb + 1) * SEQ)
        f_hi, f_lo = _split(f_ref[rows, :])
        fc_hi, fc_lo = _split(_dot3(f_hi, f_lo, bc_hi, bc_lo))
        fs_hi, fs_lo = _split(_dot3(f_hi, f_lo, bs_hi, bs_lo))
        z = _dot3(cs_hi, cs_lo, fc_hi, fc_lo) - _dot3(ss_hi, ss_lo, fs_hi, fs_lo)
        o_ref[rows, 0:F_WIDTH] = (z * (SEQ * HEAD_DIM) ** -0.5).astype(BF16)

        kv = [(kw_ref[bb].astype(BF16), vw_ref[bb].astype(BF16)),
              (_swap_row_halves(kw_ref[bb]).astype(BF16), _swap_row_halves(vw_ref[bb]).astype(BF16))]
        q_pairs = [qw_ref[rows, LANES * j:LANES * (j + 1)] * ATTN_SCALE for j in range(WIN_HEADS // 2)]
        outs = _gqa_attention(q_pairs, SEQ, lambda c: [(kv[c][0], kv[c][1], None)],
                              [sink_ref[l, h] for h in range(WIN_HEADS)], transposed=True)
        for j, o in enumerate(outs):
            o_ref[rows, F_WIDTH + LANES * j:F_WIDTH + LANES * (j + 1)] = o.astype(BF16)

        for j in range(NA_HEADS // 2):
            sl = slice(LANES * j, LANES * (j + 1))
            q_pairs = {j: qn_ref[rows, sl] * ATTN_SCALE}
            o = _attend(_stack_heads(q_pairs, (2 * j, 2 * j + 1)),
                        [(kn_ref[bb, sl, :].astype(BF16), vn_ref[bb, sl, :].astype(BF16), None)], transposed=True)
            base = F_WIDTH + WIN_Q + LANES * j
            o_ref[rows, base:base + LANES] = _merge_pair(o[:SEQ], o[SEQ:]).astype(BF16)


def _ctx_mixer(l, win_sink, f, qw, qn, kvt, dft_ch, dft_seq):
    row = lambda w: pl.BlockSpec((MIX_GROUP * SEQ, w), lambda b: (b, 0))
    col = lambda w: pl.BlockSpec((MIX_GROUP, None, w, SEQ), lambda b: (b, l, 0, 0))
    const = lambda n: pl.BlockSpec((n, n), lambda b: (0, 0))
    return pl.pallas_call(
        functools.partial(_ctx_mixer_kernel, l),
        grid=(BATCH // MIX_GROUP,),
        in_specs=[pl.BlockSpec(memory_space=pltpu.SMEM),
                  row(F_WIDTH), row(WIN_Q), row(NA_W)] + [col(w) for w in KV_WIDTHS]
                 + [const(F_WIDTH)] * 2 + [const(SEQ)] * 2,
        out_specs=pl.BlockSpec((MIX_GROUP * SEQ, D_MODEL), lambda b: (b, 0)),
        out_shape=jax.ShapeDtypeStruct((N_CTX, D_MODEL), BF16),
        compiler_params=_params("arbitrary"),
        name=f"ctx_mixer{l}",
    )(win_sink, f, qw, qn, *kvt, *dft_ch, *dft_seq)


FT_ROWS = 512


def _lat_fourier_kernel(f_ref, bc_ref, bs_ref, cs_ref, ss_ref, o_ref, st_ref):
    b = pl.program_id(1)

    @pl.when(pl.program_id(0) == 0)
    def _():
        f_hi, f_lo = _split(f_ref[pl.ds(pl.multiple_of(b * DEC_SEQ, DEC_SEQ), DEC_SEQ), :])
        fc_hi, fc_lo = _split(_dot3(f_hi, f_lo, *_split(bc_ref[...])))
        fs_hi, fs_lo = _split(_dot3(f_hi, f_lo, *_split(bs_ref[...])))
        st_ref[b, 0] = fc_hi
        st_ref[b, 1] = fc_lo
        st_ref[b, 2] = fs_hi
        st_ref[b, 3] = fs_lo

    z = (_dot3(*_split(cs_ref[...]), st_ref[b, 0], st_ref[b, 1])
         - _dot3(*_split(ss_ref[...]), st_ref[b, 2], st_ref[b, 3]))
    o_ref[...] = (z * (DEC_SEQ * HEAD_DIM) ** -0.5).astype(BF16)


def _lat_fourier(f, dft_ch, dft_seq):
    nrb = DEC_SEQ // FT_ROWS
    const = pl.BlockSpec((F_WIDTH, F_WIDTH), lambda r, b: (0, 0))
    rows = pl.BlockSpec((FT_ROWS, DEC_SEQ), lambda r, b: (r, 0))
    return pl.pallas_call(
        _lat_fourier_kernel,
        grid=(nrb, DEC_BATCH),
        in_specs=[pl.BlockSpec((N_LAT, F_WIDTH), lambda r, b: (1, 0))] + [const] * 2 + [rows] * 2,
        out_specs=pl.BlockSpec((FT_ROWS, F_WIDTH), lambda r, b: (b * nrb + r, 0)),
        out_shape=jax.ShapeDtypeStruct((N_LAT, F_WIDTH), BF16),
        scratch_shapes=[pltpu.VMEM((DEC_BATCH, 4, DEC_SEQ, F_WIDTH), BF16)],
        compiler_params=_params("arbitrary", "arbitrary"),
        name="lat_fourier",
    )(f, *dft_ch, *dft_seq)


def _rope(x, cos, sin_signed):
    n = x.shape[-1]
    lane = lax.broadcasted_iota(jnp.int32, x.shape, x.ndim - 1)
    first = (lane % 32) < 16
    partner = jnp.where(first, pltpu.roll(x, n - 16, axis=x.ndim - 1), pltpu.roll(x, 16, axis=x.ndim - 1))
    return x * cos + partner * sin_signed


def _win_kernel(l, sink_ref, q_ref, k_ref, v_ref, ck_ref, cv_ref, cos_ref, sin_ref, cosq_ref, sinq_ref,
                o_ref, kp_ref, vp_ref, cp_ref):
    n = pl.program_id(1)
    nb = DEC_SEQ // WIN_BLOCK
    pad = WIN_BLOCK

    @pl.when(n == 0)
    def _():
        zeros = jnp.zeros((pad, LANES), BF16)
        kr = _rope(k_ref[...], cos_ref[...], sin_ref[...])
        v = v_ref[...]
        for idx, (kk, vv) in enumerate(((kr, v), (_swap_halves(kr), _swap_halves(v)))):
            kp_ref[idx, 0:pad] = zeros
            kp_ref[idx, pad + DEC_SEQ:] = zeros
            kp_ref[idx, pad:pad + DEC_SEQ] = kk.astype(BF16)
            vp_ref[idx, 0:pad] = zeros
            vp_ref[idx, pad + DEC_SEQ:] = zeros
            vp_ref[idx, pad:pad + DEC_SEQ] = vv.astype(BF16)
        ck = ck_ref[...]
        cv = cv_ref[...]
        cp_ref[0] = ck.astype(BF16)
        cp_ref[1] = _swap_halves(ck).astype(BF16)
        cp_ref[2] = cv.astype(BF16)
        cp_ref[3] = _swap_halves(cv).astype(BF16)

    lo = jnp.where(n == 0, WIN_BLOCK, 0)
    hi = jnp.where(n == nb - 1, 2 * WIN_BLOCK, 3 * WIN_BLOCK)

    i = lax.broadcasted_iota(jnp.int32, (WIN_BLOCK, 3 * WIN_BLOCK), 0)
    j = lax.broadcasted_iota(jnp.int32, (WIN_BLOCK, 3 * WIN_BLOCK), 1)
    mask = (j >= i + WIN_BLOCK - WINDOW) & (j <= i + WIN_BLOCK + WINDOW) & (j >= lo) & (j < hi)
    band_bias = jnp.where(mask, 0.0, NEG_INF)

    def band(s):
        heads = s.shape[0] // WIN_BLOCK
        return (s.reshape(heads, WIN_BLOCK, s.shape[1]) + band_bias[None]).reshape(s.shape)

    start = pl.multiple_of(n * WIN_BLOCK, WIN_BLOCK)
    win = pl.ds(start, 3 * WIN_BLOCK)
    q_pairs = [_rope(q_ref[:, LANES * jp:LANES * (jp + 1)], cosq_ref[...], sinq_ref[...]) * ATTN_SCALE
               for jp in range(WIN_HEADS // 2)]
    outs = _gqa_attention(
        q_pairs, WIN_BLOCK,
        lambda c: [(kp_ref[c, win, :], vp_ref[c, win, :], band), (cp_ref[c], cp_ref[2 + c], None)],
        [sink_ref[l, h] for h in range(WIN_HEADS)])
    for jp, o in enumerate(outs):
        o_ref[:, LANES * jp:LANES * (jp + 1)] = o.astype(BF16)


def _lat_window(l, win_sink, qw, kw, vw, cache_k, cache_v, cos_t, sin_t):
    nb = DEC_SEQ // WIN_BLOCK
    kv_spec = pl.BlockSpec((DEC_SEQ, WIN_KV), lambda b, n: (b, 0))
    cache_spec = pl.BlockSpec((None, None, PAST_LEN, WIN_KV), lambda b, n: (b, l, 0, 0))
    tab_all = pl.BlockSpec((DEC_SEQ, LANES), lambda b, n: (0, 0))
    tab_blk = pl.BlockSpec((WIN_BLOCK, LANES), lambda b, n: (n, 0))
    return pl.pallas_call(
        functools.partial(_win_kernel, l),
        grid=(DEC_BATCH, nb),
        in_specs=[pl.BlockSpec(memory_space=pltpu.SMEM),
                  pl.BlockSpec((WIN_BLOCK, WIN_Q), lambda b, n: (N_CTX // WIN_BLOCK + b * nb + n, 0)),
                  kv_spec, kv_spec, cache_spec, cache_spec, tab_all, tab_all, tab_blk, tab_blk],
        out_specs=pl.BlockSpec((WIN_BLOCK, WIN_Q), lambda b, n: (b * nb + n, 0)),
        out_shape=jax.ShapeDtypeStruct((N_LAT, WIN_Q), BF16),
        scratch_shapes=[pltpu.VMEM((2, DEC_SEQ + 2 * WIN_BLOCK, LANES), BF16),
                        pltpu.VMEM((2, DEC_SEQ + 2 * WIN_BLOCK, LANES), BF16),
                        pltpu.VMEM((4, PAST_LEN, LANES), BF16)],
        compiler_params=_params("arbitrary", "arbitrary"),
        name=f"lat_window{l}",
    )(win_sink, qw, kw, vw, cache_k, cache_v, cos_t, sin_t, cos_t, sin_t)


NA_G = 4
NA_Q = NA_G * GRID_W
NA_WIN_ROWS = NA_ROWS + NA_G
NA_BLOCKS = GRID_ROWS // NA_G
NA_VARIANTS = 3


def _na_block_start(g):
    return jnp.clip(NA_G * g - NA_ROWS // 2, 0, GRID_ROWS - NA_WIN_ROWS)


def _na_kernel(q_ref, k_ref, v_ref, ck_ref, cv_ref, bias_ref, o_ref):
    g = pl.program_id(1)
    start = pl.multiple_of(_na_block_start(g) * GRID_W, GRID_W)
    win = pl.ds(start, NA_WIN_ROWS * GRID_W)
    for jp in range(NA_HEADS // 2):
        sl = slice(LANES * jp, LANES * (jp + 1))
        q_stack = _stack_heads({jp: q_ref[:, sl] * ATTN_SCALE}, (2 * jp, 2 * jp + 1))
        bias = bias_ref[jp]
        o = _attend(q_stack, [(k_ref[win, sl].astype(BF16), v_ref[win, sl].astype(BF16), lambda s: s + bias),
                              (ck_ref[:, sl].astype(BF16), cv_ref[:, sl].astype(BF16), None)])
        o_ref[:, sl] = _merge_pair(o[:NA_Q], o[NA_Q:]).astype(BF16)


def _lat_neighbourhood(l, qn, kn, vn, cache_k, cache_v, bias_tab):
    kv_spec = pl.BlockSpec((DEC_SEQ, NA_W), lambda b, g: (b, 0))
    cache_spec = pl.BlockSpec((None, None, PAST_LEN, NA_W), lambda b, g: (b, l, 0, 0))
    variant = lambda g: jnp.where(g == 0, 0, jnp.where(g == NA_BLOCKS - 1, 2, 1))
    return pl.pallas_call(
        _na_kernel,
        grid=(DEC_BATCH, NA_BLOCKS),
        in_specs=[pl.BlockSpec((NA_Q, NA_W), lambda b, g: (N_CTX // NA_Q + b * NA_BLOCKS + g, 0)),
                  kv_spec, kv_spec, cache_spec, cache_spec,
                  pl.BlockSpec((None, None, NA_HEADS // 2, 2 * NA_Q, NA_WIN_ROWS * GRID_W),
                               lambda b, g: (l, variant(g), 0, 0, 0))],
        out_specs=pl.BlockSpec((NA_Q, NA_W), lambda b, g: (b * NA_BLOCKS + g, 0)),
        out_shape=jax.ShapeDtypeStruct((N_LAT, NA_W), BF16),
        compiler_params=_params("arbitrary", "arbitrary"),
        name=f"lat_neighbourhood{l}",
    )(qn, kn, vn, cache_k, cache_v, bias_tab)


OUTPROJ_ROWS = 256


def _outproj_kernel(xc_ref, xl_ref, mc_ref, mf_ref, mw_ref, mn_ref, mod_ref, g_ref, w_ref, r_hi, r_lo,
                    x_ref, h_ref, lg_ref, wb_ref):
    i = pl.program_id(0)

    @pl.when(i == 0)
    def _():
        wb_ref[...] = w_ref[...].astype(BF16)

    d = D_MODEL

    def block(x_in_ref, mixed_rows):
        for r0 in range(0, TM, OUTPROJ_ROWS):
            rows = slice(r0, r0 + OUTPROJ_ROWS)
            x = x_in_ref[rows, :] + mod_ref[:, 2 * d:3 * d] * _dot(mixed_rows(rows), wb_ref[...])
            x_ref[rows, :] = x
            h = _rms_mod(x, g_ref[...], mod_ref[:, 3 * d:4 * d], mod_ref[:, 4 * d:5 * d])
            h_ref[rows, :] = h.astype(BF16)
            h_hi, h_lo = _split(h)
            lg_ref[:, rows] = _dot3(h_hi, h_lo, r_hi[...], r_lo[...]).T[0:N_EXPERTS, :]

    @pl.when(i < N_CTX // TM)
    def _():
        block(xc_ref, lambda rows: mc_ref[rows, :])

    @pl.when(i >= N_CTX // TM)
    def _():
        block(xl_ref, lambda rows: jnp.concatenate([mf_ref[rows, :], mw_ref[rows, :], mn_ref[rows, :]], axis=1))


def _outproj(l, x_ctx, x_lat, mixed_ctx, mixed_f, mixed_w, mixed_n, mod4, g_ffn, w_out, r_hi, r_lo):
    n_ctx_blocks = N_CTX // TM
    xc_spec, xl_spec = _two_stream_specs(D_MODEL)
    lat = lambda w: pl.BlockSpec((TM, w), lambda i: (jnp.maximum(i - n_ctx_blocks, 0), 0))
    whole = lambda shape: pl.BlockSpec(shape, lambda i: (0,) * len(shape))
    row = lambda w: pl.BlockSpec((TM, w), lambda i: (i, 0))
    return pl.pallas_call(
        _outproj_kernel,
        grid=(N_TOK // TM,),
        in_specs=[xc_spec, xl_spec,
                  pl.BlockSpec((TM, D_MODEL), lambda i: (jnp.minimum(i, n_ctx_blocks - 1), 0)),
                  lat(F_WIDTH), lat(WIN_Q), lat(NA_W),
                  pl.BlockSpec((None, None, 1, 6 * D_MODEL), lambda i: (l, _cond_of_block(i), 0, 0)),
                  pl.BlockSpec((None, 1, D_MODEL), lambda i: (l, 0, 0)),
                  pl.BlockSpec((None, D_MODEL, D_MODEL), lambda i: (l, 0, 0)),
                  whole((D_MODEL, LANES)), whole((D_MODEL, LANES))],
        out_specs=[row(D_MODEL), row(D_MODEL), pl.BlockSpec((N_EXPERTS, TM), lambda i: (0, i))],
        out_shape=[jax.ShapeDtypeStruct((N_TOK, D_MODEL), F32),
                   jax.ShapeDtypeStruct((N_TOK, D_MODEL), BF16),
                   jax.ShapeDtypeStruct((N_EXPERTS, N_TOK), F32)],
        scratch_shapes=[pltpu.VMEM((D_MODEL, D_MODEL), BF16)],
        compiler_params=_params("arbitrary"),
        name=f"outproj{l}",
    )(x_ctx, x_lat, mixed_ctx, mixed_f, mixed_w, mixed_n, mod4, g_ffn.reshape(DEPTH, 1, D_MODEL), w_out,
      r_hi, r_lo)


PREFIX_CHUNK = 256
MANTISSA_STEPS = 44


def _prefix_exclusive(m):
    rows, n = m.shape
    t0 = lax.broadcasted_iota(jnp.int32, (PREFIX_CHUNK, PREFIX_CHUNK), 0)
    t1 = lax.broadcasted_iota(jnp.int32, (PREFIX_CHUNK, PREFIX_CHUNK), 1)
    upper = jnp.where(t0 < t1, 1.0, 0.0).astype(BF16)
    carry = jnp.zeros((rows, 1), F32)
    outs = []
    for c in range(n // PREFIX_CHUNK):
        blk = m[:, c * PREFIX_CHUNK:(c + 1) * PREFIX_CHUNK]
        outs.append(_dot(blk.astype(BF16), upper) + carry)
        carry = carry + blk.sum(axis=-1, keepdims=True)
    return outs[0] if len(outs) == 1 else jnp.concatenate(outs, axis=-1)


ROUTE_STREAMS = ((0, BATCH, CAP_CTX), (N_CTX, DEC_BATCH, CAP_LAT))


def _route_kernel(lg_ref, *out_refs):
    affs, caps = [], []
    for t0, groups, cap in ROUTE_STREAMS:
        n = (N_CTX if t0 == 0 else N_LAT) // groups
        x = jnp.concatenate([lg_ref[:, t0 + g * n:t0 + (g + 1) * n] for g in range(groups)], axis=0)
        x = x.reshape(groups, N_EXPERTS, n)
        e = jnp.exp(x - x.max(axis=1, keepdims=True))
        affs.append((e / e.sum(axis=1, keepdims=True)).reshape(groups * N_EXPERTS, n))
        caps.append(float(cap))

    def count_ge(aff, t):
        return jnp.where(aff >= t, 1.0, 0.0).sum(axis=-1, keepdims=True)

    def keep_if_enough(aff, cap, cand, otherwise):
        return jnp.where(count_ge(aff, cand) >= cap, cand, otherwise)

    above = [jnp.full((aff.shape[0], 1), 2.0, F32) for aff in affs]
    for s in (64, 32, 16, 8, 4, 2, 1):
        cands = [a * (2.0 ** -s) for a in above]
        above = [jnp.where(count_ge(aff, c) >= cap, a, c) for aff, cap, a, c in zip(affs, caps, above, cands)]
    bases = [keep_if_enough(aff, cap, a * 0.5, 0.0) for aff, cap, a in zip(affs, caps, above)]

    def refine(_, carry):
        thrs, incs = carry
        thrs = tuple(keep_if_enough(aff, cap, t + i, t) for aff, cap, t, i in zip(affs, caps, thrs, incs))
        return thrs, tuple(i * 0.5 for i in incs)

    thrs, _ = lax.fori_loop(0, MANTISSA_STEPS, refine, (tuple(bases), tuple(b * 0.5 for b in bases)))

    for k, ((_, groups, _), aff, cap, thr) in enumerate(zip(ROUTE_STREAMS, affs, caps, thrs)):
        slot_ref, aff_ref, slott_ref = out_refs[3 * k:3 * k + 3]
        n = aff.shape[1]
        gt = jnp.where(aff > thr, 1.0, 0.0)
        eq = jnp.where(aff == thr, 1.0, 0.0)
        need = cap - gt.sum(axis=-1, keepdims=True)
        sel = gt + eq * jnp.where(_prefix_exclusive(eq) < need, 1.0, 0.0)
        slot = jnp.where(sel > 0.0, _prefix_exclusive(sel), -1.0)
        slot_ref[...] = slot.astype(jnp.int32)
        aff_ref[...] = aff
        unused = jnp.full((LANES - N_EXPERTS, n), -1.0, F32)
        for g in range(groups):
            tile = jnp.concatenate([slot[g * N_EXPERTS:(g + 1) * N_EXPERTS], unused], axis=0)
            slott_ref[g * n:(g + 1) * n, :] = tile.T.astype(jnp.int32)


def _route(lg_t):
    shapes = []
    for t0, groups, _ in ROUTE_STREAMS:
        n_tok = N_CTX if t0 == 0 else N_LAT
        rows, n = groups * N_EXPERTS, n_tok // groups
        shapes += [((rows, n), jnp.int32), ((rows, n), F32), ((n_tok, LANES), jnp.int32)]
    return pl.pallas_call(
        _route_kernel,
        grid=(1,),
        in_specs=[pl.BlockSpec((N_EXPERTS, N_TOK), lambda i: (0, 0))],
        out_specs=[pl.BlockSpec(shape, lambda i: (0, 0)) for shape, _ in shapes],
        out_shape=[jax.ShapeDtypeStruct(shape, dt) for shape, dt in shapes],
        compiler_params=_params("arbitrary"),
        name="route",
    )(lg_t)


CTX_GROUP = 4


def _gather_ctx_kernel(h_ref, slot_ref, aff_ref, x_ref, g_ref, p_ref):
    s_iota = lax.broadcasted_iota(jnp.int32, (CAP_CTX, SEQ), 0)
    for bb in range(CTX_GROUP):
        slots = slice(bb * CAP_CTX, (bb + 1) * CAP_CTX)
        for e in range(N_EXPERTS):
            row = bb * N_EXPERTS + e
            hit = s_iota == slot_ref[row:row + 1, :]
            p_ref[bb, e * CAP_CTX:(e + 1) * CAP_CTX, :] = jnp.where(hit, 1.0, 0.0).astype(BF16)
            g_ref[e, slots] = jnp.where(hit, aff_ref[row:row + 1, :], 0.0).sum(axis=-1, keepdims=True)
        x = _dot(p_ref[bb], h_ref[bb * SEQ:(bb + 1) * SEQ, :]).astype(BF16)
        x_ref[:, slots, :] = x.reshape(N_EXPERTS, CAP_CTX, D_MODEL)


def _gather_ctx(h, slot, aff):
    return pl.pallas_call(
        _gather_ctx_kernel,
        grid=(BATCH // CTX_GROUP,),
        in_specs=[pl.BlockSpec((CTX_GROUP * SEQ, D_MODEL), lambda b: (b, 0)),
                  pl.BlockSpec((CTX_GROUP * N_EXPERTS, SEQ), lambda b: (b, 0)),
                  pl.BlockSpec((CTX_GROUP * N_EXPERTS, SEQ), lambda b: (b, 0))],
        out_specs=[pl.BlockSpec((N_EXPERTS, CTX_GROUP * CAP_CTX, D_MODEL), lambda b: (0, b, 0)),
                   pl.BlockSpec((N_EXPERTS, CTX_GROUP * CAP_CTX, 1), lambda b: (0, b, 0))],
        out_shape=[jax.ShapeDtypeStruct((N_EXPERTS, ROWS_CTX, D_MODEL), BF16),
                   jax.ShapeDtypeStruct((N_EXPERTS, ROWS_CTX, 1), F32)],
        scratch_shapes=[pltpu.VMEM((CTX_GROUP, N_EXPERTS * CAP_CTX, SEQ), BF16)],
        compiler_params=_params("arbitrary"),
        name="gather_ctx",
    )(h, slot, aff)


LAT_GATHER_EXPERTS = 2


def _gather_lat_kernel(h_ref, slot_ref, aff_ref, x_ref, g_ref):
    s_iota = lax.broadcasted_iota(jnp.int32, (CAP_LAT, DEC_SEQ), 0)
    for k in range(LAT_GATHER_EXPERTS):
        e = pl.program_id(1) * LAT_GATHER_EXPERTS + k
        hit = s_iota == slot_ref[pl.ds(e, 1), :]
        x_ref[k] = _dot(jnp.where(hit, 1.0, 0.0).astype(BF16), h_ref[...]).astype(BF16)
        g_ref[k] = jnp.where(hit, aff_ref[pl.ds(e, 1), :], 0.0).sum(axis=-1, keepdims=True)


def _gather_lat(h, slot, aff):
    lat_blk0 = N_CTX // DEC_SEQ
    return pl.pallas_call(
        _gather_lat_kernel,
        grid=(DEC_BATCH, N_EXPERTS // LAT_GATHER_EXPERTS),
        in_specs=[pl.BlockSpec((DEC_SEQ, D_MODEL), lambda b, e: (lat_blk0 + b, 0)),
                  pl.BlockSpec((N_EXPERTS, DEC_SEQ), lambda b, e: (b, 0)),
                  pl.BlockSpec((N_EXPERTS, DEC_SEQ), lambda b, e: (b, 0))],
        out_specs=[pl.BlockSpec((LAT_GATHER_EXPERTS, CAP_LAT, D_MODEL), lambda b, e: (e, b, 0)),
                   pl.BlockSpec((LAT_GATHER_EXPERTS, CAP_LAT, 1), lambda b, e: (e, b, 0))],
        out_shape=[jax.ShapeDtypeStruct((N_EXPERTS, ROWS_LAT, D_MODEL), BF16),
                   jax.ShapeDtypeStruct((N_EXPERTS, ROWS_LAT, 1), F32)],
        compiler_params=_params("arbitrary", "arbitrary"),
        name="gather_lat",
    )(h, slot, aff)


N_FF_CHUNKS = D_FF // TF
CHUNKS_PER_STEP = 3
UP_BLOCK = CHUNKS_PER_STEP * TF
N_UP_STEPS = -(-N_FF_CHUNKS // CHUNKS_PER_STEP)
LAST_STEP_CHUNKS = N_FF_CHUNKS - (N_UP_STEPS - 1) * CHUNKS_PER_STEP


def _silu_tanh(x):
    return x * (0.5 + 0.5 * jnp.tanh(0.5 * x))


def _ffn_kernel(xc_ref, xl_ref, gc_ref, gl_ref, wg_ref, wu_ref, wd_ref, y_ref, x_sc, h_sc):
    j = pl.program_id(1)

    @pl.when(j == 0)
    def _():
        x_sc[0:ROWS_CTX, :] = xc_ref[...]
        x_sc[ROWS_CTX:, :] = xl_ref[...]

    def up(n_chunks):
        x = x_sc[...]
        for sub in range(n_chunks):
            cols = slice(sub * TF, (sub + 1) * TF)
            a = _dot(x, wg_ref[:, cols].astype(BF16))
            u = _dot(x, wu_ref[:, cols].astype(BF16))
            h_sc[j * CHUNKS_PER_STEP + sub] = (_silu_tanh(a) * u).astype(BF16)

    pl.when(j < N_UP_STEPS - 1)(functools.partial(up, CHUNKS_PER_STEP))
    pl.when(j == N_UP_STEPS - 1)(functools.partial(up, LAST_STEP_CHUNKS))

    @pl.when(j >= N_UP_STEPS)
    def _():
        for c0 in range(0, D_MODEL, TD):
            acc = None
            for k in range(N_FF_CHUNKS):
                t = _dot(h_sc[k], wd_ref[k * TF:(k + 1) * TF, c0:c0 + TD].astype(BF16))
                acc = t if acc is None else acc + t
            y_ref[0:ROWS_CTX, c0:c0 + TD] = (acc[0:ROWS_CTX] * gc_ref[...]).astype(BF16)
            y_ref[ROWS_CTX:, c0:c0 + TD] = (acc[ROWS_CTX:] * gl_ref[...]).astype(BF16)


def _ffn(l, x_c, x_l, g_c, g_l, w_gate, w_up, w_down):
    rows = ROWS_CTX + ROWS_LAT
    xin = lambda r: pl.BlockSpec((None, r, D_MODEL), lambda e, j: (e, 0, 0))
    gin = lambda r: pl.BlockSpec((None, r, 1), lambda e, j: (e, 0, 0))
    up_chunk = lambda e, j: (l, e, 0, jnp.minimum(j, N_UP_STEPS - 1))
    return pl.pallas_call(
        _ffn_kernel,
        grid=(N_EXPERTS, N_UP_STEPS + 1),
        in_specs=[xin(ROWS_CTX), xin(ROWS_LAT), gin(ROWS_CTX), gin(ROWS_LAT),
                  pl.BlockSpec((None, None, D_MODEL, UP_BLOCK), up_chunk),
                  pl.BlockSpec((None, None, D_MODEL, UP_BLOCK), up_chunk),
                  pl.BlockSpec((None, None, D_FF, D_MODEL), lambda e, j: (l, e, 0, 0))],
        out_specs=pl.BlockSpec((None, rows, D_MODEL), lambda e, j: (e, 0, 0)),
        out_shape=jax.ShapeDtypeStruct((N_EXPERTS, rows, D_MODEL), BF16),
        scratch_shapes=[pltpu.VMEM((rows, D_MODEL), BF16), pltpu.VMEM((N_FF_CHUNKS, rows, TF), BF16)],
        compiler_params=_params("arbitrary", "arbitrary"),
        name=f"experts{l}",
    )(x_c, x_l, g_c, g_l, w_gate, w_up, w_down)


def _finish(x, res, mod_ref, gf_ref, final):
    y = x + mod_ref[:, 5 * D_MODEL:] * res
    if final:
        y = y * lax.rsqrt(jnp.mean(y * y, axis=-1, keepdims=True) + RMS_EPS) * gf_ref[...]
    return y


def _combine_ctx_kernel(final, x_ref, y_ref, slot_ref, rep_ref, mod_ref, gf_ref, o_ref):
    n_col = N_EXPERTS * CAP_CTX
    col = (lax.broadcasted_iota(jnp.int32, (SEQ, n_col), 1) % CAP_CTX).astype(F32)
    for bb in range(CTX_GROUP):
        rows = slice(bb * SEQ, (bb + 1) * SEQ)
        spread = _dot(slot_ref[rows, :].astype(F32).astype(BF16), rep_ref[...])
        p = jnp.where(spread == col, 1.0, 0.0).astype(BF16)
        y = y_ref[:, bb * CAP_CTX:(bb + 1) * CAP_CTX, :].reshape(n_col, D_MODEL)
        o_ref[rows, :] = _finish(x_ref[rows, :], _dot(p, y), mod_ref, gf_ref, final)


def _combine_ctx(l, final, x_new, y, slot_t, mod4, g_final):
    n_col = N_EXPERTS * CAP_CTX
    rep = (np.arange(n_col)[None, :] // CAP_CTX == np.arange(LANES)[:, None]).astype(np.float32)
    return pl.pallas_call(
        functools.partial(_combine_ctx_kernel, final),
        grid=(BATCH // CTX_GROUP,),
        in_specs=[pl.BlockSpec((CTX_GROUP * SEQ, D_MODEL), lambda b: (b, 0)),
                  pl.BlockSpec((N_EXPERTS, CTX_GROUP * CAP_CTX, D_MODEL), lambda b: (0, b, 0)),
                  pl.BlockSpec((CTX_GROUP * SEQ, LANES), lambda b: (b, 0)),
                  pl.BlockSpec((LANES, n_col), lambda b: (0, 0)),
                  pl.BlockSpec((None, None, 1, 6 * D_MODEL), lambda b: (l, 0, 0, 0)),
                  pl.BlockSpec((1, D_MODEL), lambda b: (0, 0))],
        out_specs=pl.BlockSpec((CTX_GROUP * SEQ, D_MODEL), lambda b: (b, 0)),
        out_shape=jax.ShapeDtypeStruct((N_CTX, D_MODEL), F32),
        compiler_params=_params("arbitrary"),
        name=f"combine_ctx{l}",
    )(x_new, y, slot_t, jnp.asarray(rep, BF16), mod4, g_final.reshape(1, D_MODEL))


TMC = 512


def _combine_lat_kernel(final, x_ref, y_ref, slot_ref, mod_ref, gf_ref, o_ref):
    s_iota = lax.broadcasted_iota(jnp.int32, (TMC, CAP_LAT), 1)
    slot = slot_ref[...]
    res = None
    for e in range(N_EXPERTS):
        p = jnp.where(slot[:, e:e + 1] == s_iota, 1.0, 0.0).astype(BF16)
        t = _dot(p, y_ref[e])
        res = t if res is None else res + t
    o_ref[...] = _finish(x_ref[...], res, mod_ref, gf_ref, final)


def _combine_lat(l, final, x_new, y, slot_t, mod4, g_final):
    nt = DEC_SEQ // TMC
    return pl.pallas_call(
        functools.partial(_combine_lat_kernel, final),
        grid=(DEC_BATCH, nt),
        in_specs=[pl.BlockSpec((TMC, D_MODEL), lambda b, t: (N_CTX // TMC + b * nt + t, 0)),
                  pl.BlockSpec((N_EXPERTS, CAP_LAT, D_MODEL), lambda b, t: (0, ROWS_CTX // CAP_LAT + b, 0)),
                  pl.BlockSpec((TMC, LANES), lambda b, t: (b * nt + t, 0)),
                  pl.BlockSpec((None, None, 1, 6 * D_MODEL), lambda b, t: (l, 1 + b, 0, 0)),
                  pl.BlockSpec((1, D_MODEL), lambda b, t: (0, 0))],
        out_specs=pl.BlockSpec((TMC, D_MODEL), lambda b, t: (b * nt + t, 0)),
        out_shape=jax.ShapeDtypeStruct((N_LAT, D_MODEL), F32),
        compiler_params=_params("arbitrary", "arbitrary"),
        name=f"combine_lat{l}",
    )(x_new, y, slot_t, mod4, g_final.reshape(1, D_MODEL))


def _split_table(t):
    hi = t.astype(BF16)
    return hi, (t - hi.astype(F32)).astype(BF16)


def _dft_tables(n):
    p = np.arange(n, dtype=np.int64)
    ang = ((p[:, None] * p[None, :]) % n).astype(np.float64) * (2.0 * np.pi / n)
    return np.cos(ang).astype(np.float32), np.sin(ang).astype(np.float32)


def _channel_dft_tables():
    c = np.arange(F_WIDTH, dtype=np.int64)
    same = (c[:, None] // HEAD_DIM) == (c[None, :] // HEAD_DIM)
    ang = (((c[:, None] % HEAD_DIM) * (c[None, :] % HEAD_DIM)) % HEAD_DIM).astype(np.float64) * (2.0 * np.pi / HEAD_DIM)
    return (np.where(same, np.cos(ang), 0.0).astype(np.float32),
            np.where(same, np.sin(ang), 0.0).astype(np.float32))


def _rope_tables():
    half = HEAD_DIM // 2
    nf = half // 2
    pos = np.arange(DEC_SEQ)
    inv = 1.0 / (ROPE_BASE ** (np.arange(nf, dtype=np.float64) / nf))
    ang_r = (pos // GRID_W).astype(np.float64)[:, None] * inv
    ang_c = (pos % GRID_W).astype(np.float64)[:, None] * inv

    def head(fn, sign):
        return np.concatenate([sign * fn(ang_r), fn(ang_r), sign * fn(ang_c), fn(ang_c)], axis=-1)

    cos = head(np.cos, 1.0)
    sin = head(np.sin, -1.0)
    return (np.concatenate([cos, cos], axis=-1).astype(np.float32),
            np.concatenate([sin, sin], axis=-1).astype(np.float32))


def _na_bias_tables(rpb):
    cq = np.arange(GRID_W)
    rel_c = np.clip(cq[None, :] - cq[:, None] + NA_COLS - 1, 0, 2 * NA_COLS - 2)
    pick = (rel_c[:, :, None] == np.arange(2 * NA_COLS - 1)).astype(np.float32)
    cs = np.clip(cq - NA_COLS // 2, 0, GRID_W - NA_COLS)
    col_ok = (cq[None, :] >= cs[:, None]) & (cq[None, :] < cs[:, None] + NA_COLS)
    bc = jnp.einsum('lhrj,qkj->lhrqk', rpb, pick, precision=lax.Precision.HIGHEST)
    bc = jnp.where(col_ok[None, None, None], bc, NEG_INF)
    return pl.pallas_call(
        _na_bias_kernel,
        grid=(DEPTH, NA_VARIANTS, NA_HEADS // 2),
        in_specs=[pl.BlockSpec((None, 2, 2 * NA_ROWS - 1, GRID_W, GRID_W), lambda l, v, p: (l, p, 0, 0, 0))],
        out_specs=pl.BlockSpec((None, None, None, 2 * NA_Q, NA_WIN_ROWS * GRID_W),
                               lambda l, v, p: (l, v, p, 0, 0)),
        out_shape=jax.ShapeDtypeStruct((DEPTH, NA_VARIANTS, NA_HEADS // 2, 2 * NA_Q, NA_WIN_ROWS * GRID_W), F32),
        compiler_params=_params("arbitrary", "arbitrary", "arbitrary"),
        name="na_bias",
    )(bc)


def _na_window_plan():
    plan = []
    for g in (0, 1, NA_BLOCKS - 1):
        start = int(np.clip(NA_G * g - NA_ROWS // 2, 0, GRID_ROWS - NA_WIN_ROWS))
        rows = []
        for a in range(NA_G):
            r = NA_G * g + a
            rs = int(np.clip(r - NA_ROWS // 2, 0, GRID_ROWS - NA_ROWS))
            rows.append([start + w - r + NA_ROWS - 1 if rs <= start + w < rs + NA_ROWS else None
                         for w in range(NA_WIN_ROWS)])
        plan.append(rows)
    return plan


def _na_bias_kernel(bc_ref, o_ref):
    outside = jnp.full((GRID_W, GRID_W), NEG_INF, F32)
    for v, rows in enumerate(_na_window_plan()):
        @pl.when(pl.program_id(1) == v)
        def _():
            for half in range(2):
                for a, rel in enumerate(rows):
                    tiles = [outside if rr is None else bc_ref[half, rr] for rr in rel]
                    r0 = half * NA_Q + a * GRID_W
                    o_ref[r0:r0 + GRID_W, :] = jnp.concatenate(tiles, axis=-1)


def kernel(x_prompt, x_sample, cache_win_k, cache_win_v, cache_nat_k, cache_nat_v, c, c_ctx, w_mod, b_mod, g_mix, g_ffn, w_in, w_out, win_sink, nat_rpb, w_router, w_gate, w_up, w_down, g_final):
    x_ctx = x_prompt.reshape(N_CTX, D_MODEL)
    x_lat = x_sample.reshape(N_LAT, D_MODEL)
    cond = jnp.concatenate([c_ctx[None, :], c, jnp.zeros((N_COND - 1 - DEC_BATCH, D_MODEL), F32)], axis=0)
    mod4 = _adaln(cond, w_mod, b_mod).reshape(DEPTH, N_COND, 1, 6 * D_MODEL)

    dft_ch = _channel_dft_tables()
    dft_ctx = _dft_tables(SEQ)
    dft_lat = _dft_tables(DEC_SEQ)
    cos_t, sin_t = _rope_tables()
    cwk = cache_win_k.reshape(DEC_BATCH, DEPTH, PAST_LEN, WIN_KV)
    cwv = cache_win_v.reshape(DEC_BATCH, DEPTH, PAST_LEN, WIN_KV)
    cnk = cache_nat_k.reshape(DEC_BATCH, DEPTH, PAST_LEN, NA_W)
    cnv = cache_nat_v.reshape(DEC_BATCH, DEPTH, PAST_LEN, NA_W)
    r_pad = jnp.pad(w_router, ((0, 0), (0, 0), (0, LANES - N_EXPERTS)))
    na_bias = _na_bias_tables(nat_rpb)

    kvt = ()
    for l in range(DEPTH):
        final = l == DEPTH - 1
        (f, qw, qn), (kw, vw, kn, vn), kvt = _project(l, x_ctx, x_lat, mod4, g_mix, w_in, kvt)

        mixed_ctx = _ctx_mixer(l, win_sink, f, qw, qn, kvt, dft_ch, dft_ctx)
        mixed_f = _lat_fourier(f, dft_ch, dft_lat)
        mixed_w = _lat_window(l, win_sink, qw, kw, vw, cwk, cwv, cos_t, sin_t)
        mixed_n = _lat_neighbourhood(l, qn, kn, vn, cnk, cnv, na_bias)

        r_hi, r_lo = _split_table(r_pad[l])
        x_new, h, lg_t = _outproj(l, x_ctx, x_lat, mixed_ctx, mixed_f, mixed_w, mixed_n, mod4, g_ffn,
                                  w_out, r_hi, r_lo)

        slot_c, aff_c, slot_ct, slot_l, aff_l, slot_lt = _route(lg_t)
        xg_c, gate_c = _gather_ctx(h, slot_c, aff_c)
        xg_l, gate_l = _gather_lat(h, slot_l, aff_l)
        y = _ffn(l, xg_c, xg_l, gate_c, gate_l, w_gate, w_up, w_down)
        x_ctx = _combine_ctx(l, final, x_new, y, slot_ct, mod4, g_final)
        x_lat = _combine_lat(l, final, x_new, y, slot_lt, mod4, g_final)

    y_prompt = x_ctx.reshape(BATCH, SEQ, D_MODEL)
    y_sample = x_lat.reshape(DEC_BATCH, DEC_SEQ, D_MODEL)
    new_kv = [t.reshape(BATCH, DEPTH, w // HEAD_DIM, HEAD_DIM, SEQ).transpose(0, 1, 4, 2, 3)
              for t, w in zip(kvt, KV_WIDTHS)]
    return (y_prompt, y_sample, *new_kv)
```

```python
import functools

import numpy as np
import jax
import jax.numpy as jnp
from jax import lax
from jax.experimental import pallas as pl
from jax.experimental.pallas import tpu as pltpu

D_MODEL = 1024
BATCH = 16
SEQ = 256
DEPTH = 2
DEC_BATCH = 2
DEC_SEQ = 2048
PAST_LEN = 256
GRID_W = 64
HEAD_DIM = 64
F_WIDTH = 256
WIN_HEADS = 6
WIN_KV_HEADS = 2
WINDOW = 128
WIN_BLOCK = 128
NA_HEADS = 6
NA_ROWS = 8
NA_COLS = 16
N_EXPERTS = 16
EC_CAPACITY = 2
D_FF = 2816
ROPE_BASE = 10000.0
RMS_EPS = 1e-6
NEG_INF = -1e30
ATTN_SCALE = HEAD_DIM ** -0.5
WIN_Q = WIN_HEADS * HEAD_DIM
WIN_KV = WIN_KV_HEADS * HEAD_DIM
NA_W = NA_HEADS * HEAD_DIM
N_IN = F_WIDTH + WIN_Q + 2 * WIN_KV + 3 * NA_W
SPLITS = (0, F_WIDTH, F_WIDTH + WIN_Q, F_WIDTH + WIN_Q + WIN_KV, F_WIDTH + WIN_Q + 2 * WIN_KV,
          F_WIDTH + WIN_Q + 2 * WIN_KV + NA_W, F_WIDTH + WIN_Q + 2 * WIN_KV + 2 * NA_W, N_IN)

N_CTX = BATCH * SEQ
N_LAT = DEC_BATCH * DEC_SEQ
N_TOK = N_CTX + N_LAT
GRID_ROWS = DEC_SEQ // GRID_W
CAP_CTX = EC_CAPACITY * SEQ // N_EXPERTS
CAP_LAT = EC_CAPACITY * DEC_SEQ // N_EXPERTS
ROWS_CTX = BATCH * CAP_CTX
ROWS_LAT = DEC_BATCH * CAP_LAT
N_COND = 8

LANES = 128
MXU_COLS = 256
TM = 512
TN_MOD = 1536
TF = 256
TD = 256
VMEM_LIMIT = 56 * 1024 * 1024

F32 = jnp.float32
BF16 = jnp.bfloat16


def _params(*sem):
    return pltpu.CompilerParams(dimension_semantics=sem, vmem_limit_bytes=VMEM_LIMIT)


def _dot(a, b):
    return jnp.dot(a, b, preferred_element_type=F32)


def _dot_nt(a, b):
    return lax.dot_general(a, b, (((1,), (1,)), ((), ())), preferred_element_type=F32)


def _split(x):
    hi = x.astype(BF16)
    lo = (x - hi.astype(F32)).astype(BF16)
    return hi, lo


def _dot3(a_hi, a_lo, b_hi, b_lo):
    return _dot(a_hi, b_hi) + (_dot(a_lo, b_hi) + _dot(a_hi, b_lo))


def _silu(x):
    return x / (1.0 + jnp.exp(-x))


def _rms_mod(x, g, shift, scale):
    y = x * lax.rsqrt(jnp.mean(x * x, axis=-1, keepdims=True) + RMS_EPS)
    return (y * g) * (1.0 + scale) + shift


def _softmax_parts(parts, sink=None):
    m = parts[0].max(axis=-1, keepdims=True)
    for s in parts[1:]:
        m = jnp.maximum(m, s.max(axis=-1, keepdims=True))
    if sink is not None:
        m = jnp.maximum(m, sink)
    es = [jnp.exp(s - m) for s in parts]
    den = es[0].sum(axis=-1, keepdims=True)
    for e in es[1:]:
        den = den + e.sum(axis=-1, keepdims=True)
    if sink is not None:
        den = den + jnp.exp(sink - m)
    inv = 1.0 / den
    return [e * inv for e in es]


def _cond_of_block(i):
    n_ctx_blocks = N_CTX // TM
    return jnp.where(i < n_ctx_blocks, 0, 1 + (i - n_ctx_blocks) // (DEC_SEQ // TM))


def _two_stream_specs(width):
    n_ctx_blocks = N_CTX // TM
    ctx = pl.BlockSpec((TM, width), lambda i: (jnp.minimum(i, n_ctx_blocks - 1), 0))
    lat = pl.BlockSpec((TM, width), lambda i: (jnp.maximum(i - n_ctx_blocks, 0), 0))
    return ctx, lat


def _pick_stream(ctx_ref, lat_ref):
    return jnp.where(pl.program_id(0) < N_CTX // TM, ctx_ref[...], lat_ref[...])


def _adaln_kernel(c_ref, w_ref, b_ref, o_ref):
    s_hi, s_lo = _split(_silu(c_ref[...]))
    w_hi, w_lo = _split(w_ref[...])
    o_ref[...] = _dot3(s_hi, s_lo, w_hi, w_lo) + b_ref[...]


def _adaln(cond, w_mod, b_mod):
    return pl.pallas_call(
        _adaln_kernel,
        grid=(DEPTH, 6 * D_MODEL // TN_MOD),
        in_specs=[
            pl.BlockSpec((N_COND, D_MODEL), lambda l, j: (0, 0)),
            pl.BlockSpec((None, D_MODEL, TN_MOD), lambda l, j: (l, 0, j)),
            pl.BlockSpec((None, 1, TN_MOD), lambda l, j: (l, 0, j)),
        ],
        out_specs=pl.BlockSpec((None, N_COND, TN_MOD), lambda l, j: (l, 0, j)),
        out_shape=jax.ShapeDtypeStruct((DEPTH, N_COND, 6 * D_MODEL), F32),
        compiler_params=_params("arbitrary", "arbitrary"),
        name="adaln",
    )(cond, w_mod, b_mod.reshape(DEPTH, 1, 6 * D_MODEL))


KV_NAMES = ("kw", "vw", "kn", "vn")
KV_WIDTHS = (WIN_KV, WIN_KV, NA_W, NA_W)
KV_SPLITS = (SPLITS[2], SPLITS[3], SPLITS[5], SPLITS[6])
KV_TOTAL = sum(KV_WIDTHS)
Q_SPLITS = (SPLITS[0], SPLITS[1], SPLITS[4])


def _proj_kernel(n_prev, xc_ref, xl_ref, mod_ref, g_ref, w_ref, *rest):
    rest = rest[n_prev:]
    f_ref, qw_ref, qn_ref = rest[0:3]
    lat_kv = rest[3:7]
    ctx_kvt = rest[7:11]
    wb_ref, wt_ref = rest[11:13]
    i = pl.program_id(0)

    @pl.when(i == 0)
    def _():
        wb_ref[...] = w_ref[...].astype(BF16)
        r0 = 0
        for c0, width in zip(KV_SPLITS, KV_WIDTHS):
            wt_ref[r0:r0 + width, :] = w_ref[:, c0:c0 + width].T.astype(BF16)
            r0 += width

    natural = dict(zip(SPLITS[:-1], (f_ref, qw_ref) + tuple(lat_kv[0:2]) + (qn_ref,) + tuple(lat_kv[2:4])))

    def hidden(x_ref):
        return _rms_mod(x_ref[...], g_ref[...], mod_ref[:, 0:D_MODEL], mod_ref[:, D_MODEL:2 * D_MODEL]).astype(BF16)

    def project_tiles(h, wanted):
        for t0 in range(0, N_IN, MXU_COLS):
            hits = [(a, b) for a, b in zip(SPLITS[:-1], SPLITS[1:])
                    if a in wanted and max(a, t0) < min(b, t0 + MXU_COLS)]
            if not hits:
                continue
            acc = _dot(h, wb_ref[:, t0:t0 + MXU_COLS])
            for a, b in hits:
                lo, hi = max(a, t0), min(b, t0 + MXU_COLS)
                natural[a][:, lo - a:hi - a] = acc[:, lo - t0:hi - t0].astype(natural[a].dtype)

    @pl.when(i < N_CTX // TM)
    def _():
        h = hidden(xc_ref)
        project_tiles(h, Q_SPLITS)
        kvt = _dot_nt(wt_ref[...], h)
        r0 = 0
        for o_ref, width in zip(ctx_kvt, KV_WIDTHS):
            for bb in range(TM // SEQ):
                o_ref[bb] = kvt[r0:r0 + width, bb * SEQ:(bb + 1) * SEQ]
            r0 += width

    @pl.when(i >= N_CTX // TM)
    def _():
        project_tiles(hidden(xl_ref), Q_SPLITS + KV_SPLITS)


def _project(l, x_ctx, x_lat, mod4, g_mix, w_in, prev_kvt):
    n_ctx_blocks = N_CTX // TM
    xc_spec, xl_spec = _two_stream_specs(D_MODEL)
    both = lambda w: pl.BlockSpec((TM, w), lambda i: (i, 0))
    lat = lambda w: pl.BlockSpec((TM, w), lambda i: (jnp.maximum(i - n_ctx_blocks, 0), 0))
    ctx_t = lambda w: pl.BlockSpec((TM // SEQ, None, w, SEQ),
                                   lambda i: (jnp.minimum(i, n_ctx_blocks - 1), l, 0, 0))
    n_prev = len(prev_kvt)
    n_in = 5
    outs = pl.pallas_call(
        functools.partial(_proj_kernel, n_prev),
        grid=(N_TOK // TM,),
        in_specs=[
            xc_spec, xl_spec,
            pl.BlockSpec((None, None, 1, 6 * D_MODEL), lambda i: (l, _cond_of_block(i), 0, 0)),
            pl.BlockSpec((None, 1, D_MODEL), lambda i: (l, 0, 0)),
            pl.BlockSpec((None, D_MODEL, N_IN), lambda i: (l, 0, 0)),
        ] + [pl.BlockSpec(memory_space=pl.ANY)] * n_prev,
        out_specs=[both(F_WIDTH), both(WIN_Q), both(NA_W)] + [lat(w) for w in KV_WIDTHS]
                  + [ctx_t(w) for w in KV_WIDTHS],
        out_shape=[jax.ShapeDtypeStruct((N_TOK, w), dt) for w, dt in ((F_WIDTH, F32), (WIN_Q, F32), (NA_W, BF16))]
                  + [jax.ShapeDtypeStruct((N_LAT, w), dt)
                     for w, dt in zip(KV_WIDTHS, (F32, BF16, BF16, BF16))]
                  + [jax.ShapeDtypeStruct((BATCH, DEPTH, w, SEQ), F32) for w in KV_WIDTHS],
        input_output_aliases={n_in + k: 7 + k for k in range(n_prev)},
        scratch_shapes=[pltpu.VMEM((D_MODEL, N_IN), BF16), pltpu.VMEM((KV_TOTAL, D_MODEL), BF16)],
        compiler_params=_params("arbitrary"),
        name=f"project{l}",
    )(x_ctx, x_lat, mod4, g_mix.reshape(DEPTH, 1, D_MODEL), w_in, *prev_kvt)
    return outs[0:3], outs[3:7], outs[7:11]


def _lane_is_low(shape):
    return lax.broadcasted_iota(jnp.int32, shape, len(shape) - 1) < HEAD_DIM


def _swap_halves(x):
    return pltpu.roll(x, HEAD_DIM, axis=x.ndim - 1)


def _win_kv_copy(h):
    return 0 if (h // (WIN_HEADS // WIN_KV_HEADS)) == (h % 2) else 1


def _stack_heads(q_pairs, heads):
    low = _lane_is_low(q_pairs[heads[0] // 2].shape)
    rows = [jnp.where(low if h % 2 == 0 else jnp.logical_not(low), q_pairs[h // 2], 0.0).astype(BF16)
            for h in heads]
    return rows[0] if len(rows) == 1 else jnp.concatenate(rows, axis=0)


def _per_head_column(values, rows_per_head):
    blk = lax.broadcasted_iota(jnp.int32, (len(values) * rows_per_head, 1), 0) // rows_per_head
    col = jnp.full(blk.shape, values[0], F32)
    for i in range(1, len(values)):
        col = jnp.where(blk == i, values[i], col)
    return col


def _attend(q_stack, kv_list, extra_logit=None, transposed=False):
    scores = []
    for k, _, post in kv_list:
        s = _dot(q_stack, k) if transposed else _dot_nt(q_stack, k)
        scores.append(post(s) if post is not None else s)
    probs = _softmax_parts(scores, extra_logit)
    o = None
    for p, (_, v, _) in zip(probs, kv_list):
        t = _dot_nt(p.astype(BF16), v) if transposed else _dot(p.astype(BF16), v)
        o = t if o is None else o + t
    return o


def _merge_pair(o_even, o_odd):
    return jnp.where(_lane_is_low(o_even.shape), o_even, o_odd)


def _gqa_attention(q_pairs, rows, kv_for_copy, sinks, transposed=False):
    per_head = {}
    for copy in (0, 1):
        heads = [h for h in range(WIN_HEADS) if _win_kv_copy(h) == copy]
        o = _attend(_stack_heads(q_pairs, heads), kv_for_copy(copy),
                    _per_head_column([sinks[h] for h in heads], rows), transposed)
        for i, h in enumerate(heads):
            per_head[h] = o[i * rows:(i + 1) * rows]
    return [_merge_pair(per_head[2 * j], per_head[2 * j + 1]) for j in range(WIN_HEADS // 2)]


def _swap_row_halves(x):
    return jnp.concatenate([x[HEAD_DIM:], x[:HEAD_DIM]], axis=0)


MIX_GROUP = 2


def _ctx_mixer_kernel(l, sink_ref, f_ref, qw_ref, qn_ref, kw_ref, vw_ref, kn_ref, vn_ref,
                      bc_ref, bs_ref, cs_ref, ss_ref, o_ref):
    bc_hi, bc_lo = _split(bc_ref[...])
    bs_hi, bs_lo = _split(bs_ref[...])
    cs_hi, cs_lo = _split(cs_ref[...])
    ss_hi, ss_lo = _split(ss_ref[...])
    for bb in range(MIX_GROUP):
        rows = slice(bb * SEQ, (bb + 1) * SEQ)
        f_hi, f_lo = _split(f_ref[rows, :])
        fc_hi, fc_lo = _split(_dot3(f_hi, f_lo, bc_hi, bc_lo))
        fs_hi, fs_lo = _split(_dot3(f_hi, f_lo, bs_hi, bs_lo))
        z = _dot3(cs_hi, cs_lo, fc_hi, fc_lo) - _dot3(ss_hi, ss_lo, fs_hi, fs_lo)
        o_ref[rows, 0:F_WIDTH] = (z * (SEQ * HEAD_DIM) ** -0.5).astype(BF16)

        kv = [(kw_ref[bb].astype(BF16), vw_ref[bb].astype(BF16)),
              (_swap_row_halves(kw_ref[bb]).astype(BF16), _swap_row_halves(vw_ref[bb]).astype(BF16))]
        q_pairs = [qw_ref[rows, LANES * j:LANES * (j + 1)] * ATTN_SCALE for j in range(WIN_HEADS // 2)]
        outs = _gqa_attention(q_pairs, SEQ, lambda c: [(kv[c][0], kv[c][1], None)],
                              [sink_ref[l, h] for h in range(WIN_HEADS)], transposed=True)
        for j, o in enumerate(outs):
            o_ref[rows, F_WIDTH + LANES * j:F_WIDTH + LANES * (j + 1)] = o.astype(BF16)

        for j in range(NA_HEADS // 2):
            sl = slice(LANES * j, LANES * (j + 1))
            q_pairs = {j: qn_ref[rows, sl] * ATTN_SCALE}
            o = _attend(_stack_heads(q_pairs, (2 * j, 2 * j + 1)),
                        [(kn_ref[bb, sl, :].astype(BF16), vn_ref[bb, sl, :].astype(BF16), None)], transposed=True)
            base = F_WIDTH + WIN_Q + LANES * j
            o_ref[rows, base:base + LANES] = _merge_pair(o[:SEQ], o[SEQ:]).astype(BF16)


def _ctx_mixer(l, win_sink, f, qw, qn, kvt, dft_ch, dft_seq):
    row = lambda w: pl.BlockSpec((MIX_GROUP * SEQ, w), lambda b: (b, 0))
    col = lambda w: pl.BlockSpec((MIX_GROUP, None, w, SEQ), lambda b: (b, l, 0, 0))
    const = lambda n: pl.BlockSpec((n, n), lambda b: (0, 0))
    return pl.pallas_call(
        functools.partial(_ctx_mixer_kernel, l),
        grid=(BATCH // MIX_GROUP,),
        in_specs=[pl.BlockSpec(memory_space=pltpu.SMEM),
                  row(F_WIDTH), row(WIN_Q), row(NA_W)] + [col(w) for w in KV_WIDTHS]
                 + [const(F_WIDTH)] * 2 + [const(SEQ)] * 2,
        out_specs=pl.BlockSpec((MIX_GROUP * SEQ, D_MODEL), lambda b: (b, 0)),
        out_shape=jax.ShapeDtypeStruct((N_CTX, D_MODEL), BF16),
        compiler_params=_params("arbitrary"),
        name=f"ctx_mixer{l}",
    )(win_sink, f, qw, qn, *kvt, *dft_ch, *dft_seq)


FT_ROWS = 512


def _lat_fourier_kernel(f_ref, bc_ref, bs_ref, cs_ref, ss_ref, o_ref, st_ref):
    b = pl.program_id(1)

    @pl.when(pl.program_id(0) == 0)
    def _():
        f_hi, f_lo = _split(f_ref[pl.ds(pl.multiple_of(b * DEC_SEQ, DEC_SEQ), DEC_SEQ), :])
        fc_hi, fc_lo = _split(_dot3(f_hi, f_lo, *_split(bc_ref[...])))
        fs_hi, fs_lo = _split(_dot3(f_hi, f_lo, *_split(bs_ref[...])))
        st_ref[b, 0] = fc_hi
        st_ref[b, 1] = fc_lo
        st_ref[b, 2] = fs_hi
        st_ref[b, 3] = fs_lo

    z = (_dot3(*_split(cs_ref[...]), st_ref[b, 0], st_ref[b, 1])
         - _dot3(*_split(ss_ref[...]), st_ref[b, 2], st_ref[b, 3]))
    o_ref[...] = (z * (DEC_SEQ * HEAD_DIM) ** -0.5).astype(BF16)


def _lat_fourier(f, dft_ch, dft_seq):
    nrb = DEC_SEQ // FT_ROWS
    const = pl.BlockSpec((F_WIDTH, F_WIDTH), lambda r, b: (0, 0))
    rows = pl.BlockSpec((FT_ROWS, DEC_SEQ), lambda r, b: (r, 0))
    return pl.pallas_call(
        _lat_fourier_kernel,
        grid=(nrb, DEC_BATCH),
        in_specs=[pl.BlockSpec((N_LAT, F_WIDTH), lambda r, b: (1, 0))] + [const] * 2 + [rows] * 2,
        out_specs=pl.BlockSpec((FT_ROWS, F_WIDTH), lambda r, b: (b * nrb + r, 0)),
        out_shape=jax.ShapeDtypeStruct((N_LAT, F_WIDTH), BF16),
        scratch_shapes=[pltpu.VMEM((DEC_BATCH, 4, DEC_SEQ, F_WIDTH), BF16)],
        compiler_params=_params("arbitrary", "arbitrary"),
        name="lat_fourier",
    )(f, *dft_ch, *dft_seq)


def _rope(x, cos, sin_signed):
    n = x.shape[-1]
    lane = lax.broadcasted_iota(jnp.int32, x.shape, x.ndim - 1)
    first = (lane % 32) < 16
    partner = jnp.where(first, pltpu.roll(x, n - 16, axis=x.ndim - 1), pltpu.roll(x, 16, axis=x.ndim - 1))
    return x * cos + partner * sin_signed


def _win_kernel(l, sink_ref, q_ref, k_ref, v_ref, ck_ref, cv_ref, cos_ref, sin_ref, cosq_ref, sinq_ref,
                o_ref, kp_ref, vp_ref, cp_ref):
    n = pl.program_id(1)
    nb = DEC_SEQ // WIN_BLOCK
    pad = WIN_BLOCK

    @pl.when(n == 0)
    def _():
        zeros = jnp.zeros((pad, LANES), BF16)
        kr = _rope(k_ref[...], cos_ref[...], sin_ref[...])
        v = v_ref[...]
        for idx, (kk, vv) in enumerate(((kr, v), (_swap_halves(kr), _swap_halves(v)))):
            kp_ref[idx, 0:pad] = zeros
            kp_ref[idx, pad + DEC_SEQ:] = zeros
            kp_ref[idx, pad:pad + DEC_SEQ] = kk.astype(BF16)
            vp_ref[idx, 0:pad] = zeros
            vp_ref[idx, pad + DEC_SEQ:] = zeros
            vp_ref[idx, pad:pad + DEC_SEQ] = vv.astype(BF16)
        ck = ck_ref[...]
        cv = cv_ref[...]
        cp_ref[0] = ck.astype(BF16)
        cp_ref[1] = _swap_halves(ck).astype(BF16)
        cp_ref[2] = cv.astype(BF16)
        cp_ref[3] = _swap_halves(cv).astype(BF16)

    lo = jnp.where(n == 0, WIN_BLOCK, 0)
    hi = jnp.where(n == nb - 1, 2 * WIN_BLOCK, 3 * WIN_BLOCK)

    i = lax.broadcasted_iota(jnp.int32, (WIN_BLOCK, 3 * WIN_BLOCK), 0)
    j = lax.broadcasted_iota(jnp.int32, (WIN_BLOCK, 3 * WIN_BLOCK), 1)
    mask = (j >= i + WIN_BLOCK - WINDOW) & (j <= i + WIN_BLOCK + WINDOW) & (j >= lo) & (j < hi)
    band_bias = jnp.where(mask, 0.0, NEG_INF)

    def band(s):
        heads = s.shape[0] // WIN_BLOCK
        return (s.reshape(heads, WIN_BLOCK, s.shape[1]) + band_bias[None]).reshape(s.shape)

    start = pl.multiple_of(n * WIN_BLOCK, WIN_BLOCK)
    win = pl.ds(start, 3 * WIN_BLOCK)
    q_pairs = [_rope(q_ref[:, LANES * jp:LANES * (jp + 1)], cosq_ref[...], sinq_ref[...]) * ATTN_SCALE
               for jp in range(WIN_HEADS // 2)]
    outs = _gqa_attention(
        q_pairs, WIN_BLOCK,
        lambda c: [(kp_ref[c, win, :], vp_ref[c, win, :], band), (cp_ref[c], cp_ref[2 + c], None)],
        [sink_ref[l, h] for h in range(WIN_HEADS)])
    for jp, o in enumerate(outs):
        o_ref[:, LANES * jp:LANES * (jp + 1)] = o.astype(BF16)


def _lat_window(l, win_sink, qw, kw, vw, cache_k, cache_v, cos_t, sin_t):
    nb = DEC_SEQ // WIN_BLOCK
    kv_spec = pl.BlockSpec((DEC_SEQ, WIN_KV), lambda b, n: (b, 0))
    cache_spec = pl.BlockSpec((None, None, PAST_LEN, WIN_KV), lambda b, n: (b, l, 0, 0))
    tab_all = pl.BlockSpec((DEC_SEQ, LANES), lambda b, n: (0, 0))
    tab_blk = pl.BlockSpec((WIN_BLOCK, LANES), lambda b, n: (n, 0))
    return pl.pallas_call(
        functools.partial(_win_kernel, l),
        grid=(DEC_BATCH, nb),
        in_specs=[pl.BlockSpec(memory_space=pltpu.SMEM),
                  pl.BlockSpec((WIN_BLOCK, WIN_Q), lambda b, n: (N_CTX // WIN_BLOCK + b * nb + n, 0)),
                  kv_spec, kv_spec, cache_spec, cache_spec, tab_all, tab_all, tab_blk, tab_blk],
        out_specs=pl.BlockSpec((WIN_BLOCK, WIN_Q), lambda b, n: (b * nb + n, 0)),
        out_shape=jax.ShapeDtypeStruct((N_LAT, WIN_Q), BF16),
        scratch_shapes=[pltpu.VMEM((2, DEC_SEQ + 2 * WIN_BLOCK, LANES), BF16),
                        pltpu.VMEM((2, DEC_SEQ + 2 * WIN_BLOCK, LANES), BF16),
                        pltpu.VMEM((4, PAST_LEN, LANES), BF16)],
        compiler_params=_params("arbitrary", "arbitrary"),
        name=f"lat_window{l}",
    )(win_sink, qw, kw, vw, cache_k, cache_v, cos_t, sin_t, cos_t, sin_t)


NA_G = 4
NA_Q = NA_G * GRID_W
NA_WIN_ROWS = NA_ROWS + NA_G
NA_BLOCKS = GRID_ROWS // NA_G
NA_VARIANTS = 3


def _na_block_start(g):
    return jnp.clip(NA_G * g - NA_ROWS // 2, 0, GRID_ROWS - NA_WIN_ROWS)


def _na_kernel(q_ref, k_ref, v_ref, ck_ref, cv_ref, bias_ref, o_ref):
    g = pl.program_id(1)
    start = pl.multiple_of(_na_block_start(g) * GRID_W, GRID_W)
    win = pl.ds(start, NA_WIN_ROWS * GRID_W)
    for jp in range(NA_HEADS // 2):
        sl = slice(LANES * jp, LANES * (jp + 1))
        q_stack = _stack_heads({jp: q_ref[:, sl] * ATTN_SCALE}, (2 * jp, 2 * jp + 1))
        bias = bias_ref[jp]
        o = _attend(q_stack, [(k_ref[win, sl].astype(BF16), v_ref[win, sl].astype(BF16), lambda s: s + bias),
                              (ck_ref[:, sl].astype(BF16), cv_ref[:, sl].astype(BF16), None)])
        o_ref[:, sl] = _merge_pair(o[:NA_Q], o[NA_Q:]).astype(BF16)


def _lat_neighbourhood(l, qn, kn, vn, cache_k, cache_v, bias_tab):
    kv_spec = pl.BlockSpec((DEC_SEQ, NA_W), lambda b, g: (b, 0))
    cache_spec = pl.BlockSpec((None, None, PAST_LEN, NA_W), lambda b, g: (b, l, 0, 0))
    variant = lambda g: jnp.where(g == 0, 0, jnp.where(g == NA_BLOCKS - 1, 2, 1))
    return pl.pallas_call(
        _na_kernel,
        grid=(DEC_BATCH, NA_BLOCKS),
        in_specs=[pl.BlockSpec((NA_Q, NA_W), lambda b, g: (N_CTX // NA_Q + b * NA_BLOCKS + g, 0)),
                  kv_spec, kv_spec, cache_spec, cache_spec,
                  pl.BlockSpec((None, None, NA_HEADS // 2, 2 * NA_Q, NA_WIN_ROWS * GRID_W),
                               lambda b, g: (l, variant(g), 0, 0, 0))],
        out_specs=pl.BlockSpec((NA_Q, NA_W), lambda b, g: (b * NA_BLOCKS + g, 0)),
        out_shape=jax.ShapeDtypeStruct((N_LAT, NA_W), BF16),
        compiler_params=_params("arbitrary", "arbitrary"),
        name=f"lat_neighbourhood{l}",
    )(qn, kn, vn, cache_k, cache_v, bias_tab)


OUTPROJ_ROWS = 256


def _outproj_kernel(xc_ref, xl_ref, mc_ref, mf_ref, mw_ref, mn_ref, mod_ref, g_ref, w_ref, r_hi, r_lo,
                    x_ref, h_ref, lg_ref, wb_ref):
    i = pl.program_id(0)

    @pl.when(i == 0)
    def _():
        wb_ref[...] = w_ref[...].astype(BF16)

    d = D_MODEL

    def block(x_in_ref, mixed_rows):
        for r0 in range(0, TM, OUTPROJ_ROWS):
            rows = slice(r0, r0 + OUTPROJ_ROWS)
            x = x_in_ref[rows, :] + mod_ref[:, 2 * d:3 * d] * _dot(mixed_rows(rows), wb_ref[...])
            x_ref[rows, :] = x
            h = _rms_mod(x, g_ref[...], mod_ref[:, 3 * d:4 * d], mod_ref[:, 4 * d:5 * d])
            h_ref[rows, :] = h.astype(BF16)
            h_hi, h_lo = _split(h)
            lg_ref[:, rows] = _dot3(h_hi, h_lo, r_hi[...], r_lo[...]).T[0:N_EXPERTS, :]

    @pl.when(i < N_CTX // TM)
    def _():
        block(xc_ref, lambda rows: mc_ref[rows, :])

    @pl.when(i >= N_CTX // TM)
    def _():
        block(xl_ref, lambda rows: jnp.concatenate([mf_ref[rows, :], mw_ref[rows, :], mn_ref[rows, :]], axis=1))


def _outproj(l, x_ctx, x_lat, mixed_ctx, mixed_f, mixed_w, mixed_n, mod4, g_ffn, w_out, r_hi, r_lo):
    n_ctx_blocks = N_CTX // TM
    xc_spec, xl_spec = _two_stream_specs(D_MODEL)
    lat = lambda w: pl.BlockSpec((TM, w), lambda i: (jnp.maximum(i - n_ctx_blocks, 0), 0))
    whole = lambda shape: pl.BlockSpec(shape, lambda i: (0,) * len(shape))
    row = lambda w: pl.BlockSpec((TM, w), lambda i: (i, 0))
    return pl.pallas_call(
        _outproj_kernel,
        grid=(N_TOK // TM,),
        in_specs=[xc_spec, xl_spec,
                  pl.BlockSpec((TM, D_MODEL), lambda i: (jnp.minimum(i, n_ctx_blocks - 1), 0)),
                  lat(F_WIDTH), lat(WIN_Q), lat(NA_W),
                  pl.BlockSpec((None, None, 1, 6 * D_MODEL), lambda i: (l, _cond_of_block(i), 0, 0)),
                  pl.BlockSpec((None, 1, D_MODEL), lambda i: (l, 0, 0)),
                  pl.BlockSpec((None, D_MODEL, D_MODEL), lambda i: (l, 0, 0)),
                  whole((D_MODEL, LANES)), whole((D_MODEL, LANES))],
        out_specs=[row(D_MODEL), row(D_MODEL), pl.BlockSpec((N_EXPERTS, TM), lambda i: (0, i))],
        out_shape=[jax.ShapeDtypeStruct((N_TOK, D_MODEL), F32),
                   jax.ShapeDtypeStruct((N_TOK, D_MODEL), BF16),
                   jax.ShapeDtypeStruct((N_EXPERTS, N_TOK), F32)],
        scratch_shapes=[pltpu.VMEM((D_MODEL, D_MODEL), BF16)],
        compiler_params=_params("arbitrary"),
        name=f"outproj{l}",
    )(x_ctx, x_lat, mixed_ctx, mixed_f, mixed_w, mixed_n, mod4, g_ffn.reshape(DEPTH, 1, D_MODEL), w_out,
      r_hi, r_lo)


PREFIX_CHUNK = 256
MANTISSA_STEPS = 44


def _prefix_exclusive(m):
    rows, n = m.shape
    t0 = lax.broadcasted_iota(jnp.int32, (PREFIX_CHUNK, PREFIX_CHUNK), 0)
    t1 = lax.broadcasted_iota(jnp.int32, (PREFIX_CHUNK, PREFIX_CHUNK), 1)
    upper = jnp.where(t0 < t1, 1.0, 0.0).astype(BF16)
    carry = jnp.zeros((rows, 1), F32)
    outs = []
    for c in range(n // PREFIX_CHUNK):
        blk = m[:, c * PREFIX_CHUNK:(c + 1) * PREFIX_CHUNK]
        outs.append(_dot(blk.astype(BF16), upper) + carry)
        carry = carry + blk.sum(axis=-1, keepdims=True)
    return outs[0] if len(outs) == 1 else jnp.concatenate(outs, axis=-1)


ROUTE_STREAMS = ((0, BATCH, CAP_CTX), (N_CTX, DEC_BATCH, CAP_LAT))


def _route_kernel(lg_ref, *out_refs):
    affs, caps = [], []
    for t0, groups, cap in ROUTE_STREAMS:
        n = (N_CTX if t0 == 0 else N_LAT) // groups
        x = jnp.concatenate([lg_ref[:, t0 + g * n:t0 + (g + 1) * n] for g in range(groups)], axis=0)
        x = x.reshape(groups, N_EXPERTS, n)
        e = jnp.exp(x - x.max(axis=1, keepdims=True))
        affs.append((e / e.sum(axis=1, keepdims=True)).reshape(groups * N_EXPERTS, n))
        caps.append(float(cap))

    def count_ge(aff, t):
        return jnp.where(aff >= t, 1.0, 0.0).sum(axis=-1, keepdims=True)

    def keep_if_enough(aff, cap, cand, otherwise):
        return jnp.where(count_ge(aff, cand) >= cap, cand, otherwise)

    above = [jnp.full((aff.shape[0], 1), 2.0, F32) for aff in affs]
    for s in (64, 32, 16, 8, 4, 2, 1):
        cands = [a * (2.0 ** -s) for a in above]
        above = [jnp.where(count_ge(aff, c) >= cap, a, c) for aff, cap, a, c in zip(affs, caps, above, cands)]
    bases = [keep_if_enough(aff, cap, a * 0.5, 0.0) for aff, cap, a in zip(affs, caps, above)]

    def refine(_, carry):
        thrs, incs = carry
        thrs = tuple(keep_if_enough(aff, cap, t + i, t) for aff, cap, t, i in zip(affs, caps, thrs, incs))
        return thrs, tuple(i * 0.5 for i in incs)

    thrs, _ = lax.fori_loop(0, MANTISSA_STEPS, refine, (tuple(bases), tuple(b * 0.5 for b in bases)))

    for k, ((_, groups, _), aff, cap, thr) in enumerate(zip(ROUTE_STREAMS, affs, caps, thrs)):
        slot_ref, aff_ref, slott_ref = out_refs[3 * k:3 * k + 3]
        n = aff.shape[1]
        gt = jnp.where(aff > thr, 1.0, 0.0)
        eq = jnp.where(aff == thr, 1.0, 0.0)
        need = cap - gt.sum(axis=-1, keepdims=True)
        sel = gt + eq * jnp.where(_prefix_exclusive(eq) < need, 1.0, 0.0)
        slot = jnp.where(sel > 0.0, _prefix_exclusive(sel), -1.0)
        slot_ref[...] = slot.astype(jnp.int32)
        aff_ref[...] = aff
        unused = jnp.full((LANES - N_EXPERTS, n), -1.0, F32)
        for g in range(groups):
            tile = jnp.concatenate([slot[g * N_EXPERTS:(g + 1) * N_EXPERTS], unused], axis=0)
            slott_ref[g * n:(g + 1) * n, :] = tile.T.astype(jnp.int32)


def _route(lg_t):
    shapes = []
    for t0, groups, _ in ROUTE_STREAMS:
        n_tok = N_CTX if t0 == 0 else N_LAT
        rows, n = groups * N_EXPERTS, n_tok // groups
        shapes += [((rows, n), jnp.int32), ((rows, n), F32), ((n_tok, LANES), jnp.int32)]
    return pl.pallas_call(
        _route_kernel,
        grid=(1,),
        in_specs=[pl.BlockSpec((N_EXPERTS, N_TOK), lambda i: (0, 0))],
        out_specs=[pl.BlockSpec(shape, lambda i: (0, 0)) for shape, _ in shapes],
        out_shape=[jax.ShapeDtypeStruct(shape, dt) for shape, dt in shapes],
        compiler_params=_params("arbitrary"),
        name="route",
    )(lg_t)


CTX_GROUP = 4


def _gather_ctx_kernel(h_ref, slot_ref, aff_ref, x_ref, g_ref, p_ref):
    s_iota = lax.broadcasted_iota(jnp.int32, (CAP_CTX, SEQ), 0)
    for bb in range(CTX_GROUP):
        slots = slice(bb * CAP_CTX, (bb + 1) * CAP_CTX)
        for e in range(N_EXPERTS):
            row = bb * N_EXPERTS + e
            hit = s_iota == slot_ref[row:row + 1, :]
            p_ref[bb, e * CAP_CTX:(e + 1) * CAP_CTX, :] = jnp.where(hit, 1.0, 0.0).astype(BF16)
            g_ref[e, slots] = jnp.where(hit, aff_ref[row:row + 1, :], 0.0).sum(axis=-1, keepdims=True)
        x = _dot(p_ref[bb], h_ref[bb * SEQ:(bb + 1) * SEQ, :]).astype(BF16)
        x_ref[:, slots, :] = x.reshape(N_EXPERTS, CAP_CTX, D_MODEL)


def _gather_ctx(h, slot, aff):
    return pl.pallas_call(
        _gather_ctx_kernel,
        grid=(BATCH // CTX_GROUP,),
        in_specs=[pl.BlockSpec((CTX_GROUP * SEQ, D_MODEL), lambda b: (b, 0)),
                  pl.BlockSpec((CTX_GROUP * N_EXPERTS, SEQ), lambda b: (b, 0)),
                  pl.BlockSpec((CTX_GROUP * N_EXPERTS, SEQ), lambda b: (b, 0))],
        out_specs=[pl.BlockSpec((N_EXPERTS, CTX_GROUP * CAP_CTX, D_MODEL), lambda b: (0, b, 0)),
                   pl.BlockSpec((N_EXPERTS, CTX_GROUP * CAP_CTX, 1), lambda b: (0, b, 0))],
        out_shape=[jax.ShapeDtypeStruct((N_EXPERTS, ROWS_CTX, D_MODEL), BF16),
                   jax.ShapeDtypeStruct((N_EXPERTS, ROWS_CTX, 1), F32)],
        scratch_shapes=[pltpu.VMEM((CTX_GROUP, N_EXPERTS * CAP_CTX, SEQ), BF16)],
        compiler_params=_params("arbitrary"),
        name="gather_ctx",
    )(h, slot, aff)


LAT_GATHER_EXPERTS = 2


def _gather_lat_kernel(h_ref, slot_ref, aff_ref, x_ref, g_ref):
    s_iota = lax.broadcasted_iota(jnp.int32, (CAP_LAT, DEC_SEQ), 0)
    for k in range(LAT_GATHER_EXPERTS):
        e = pl.program_id(1) * LAT_GATHER_EXPERTS + k
        hit = s_iota == slot_ref[pl.ds(e, 1), :]
        x_ref[k] = _dot(jnp.where(hit, 1.0, 0.0).astype(BF16), h_ref[...]).astype(BF16)
        g_ref[k] = jnp.where(hit, aff_ref[pl.ds(e, 1), :], 0.0).sum(axis=-1, keepdims=True)


def _gather_lat(h, slot, aff):
    lat_blk0 = N_CTX // DEC_SEQ
    return pl.pallas_call(
        _gather_lat_kernel,
        grid=(DEC_BATCH, N_EXPERTS // LAT_GATHER_EXPERTS),
        in_specs=[pl.BlockSpec((DEC_SEQ, D_MODEL), lambda b, e: (lat_blk0 + b, 0)),
                  pl.BlockSpec((N_EXPERTS, DEC_SEQ), lambda b, e: (b, 0)),
                  pl.BlockSpec((N_EXPERTS, DEC_SEQ), lambda b, e: (b, 0))],
        out_specs=[pl.BlockSpec((LAT_GATHER_EXPERTS, CAP_LAT, D_MODEL), lambda b, e: (e, b, 0)),
                   pl.BlockSpec((LAT_GATHER_EXPERTS, CAP_LAT, 1), lambda b, e: (e, b, 0))],
        out_shape=[jax.ShapeDtypeStruct((N_EXPERTS, ROWS_LAT, D_MODEL), BF16),
                   jax.ShapeDtypeStruct((N_EXPERTS, ROWS_LAT, 1), F32)],
        compiler_params=_params("arbitrary", "arbitrary"),
        name="gather_lat",
    )(h, slot, aff)


N_FF_CHUNKS = D_FF // TF
CHUNKS_PER_STEP = 4
FF_BLOCK = CHUNKS_PER_STEP * TF
N_FF_STEPS = -(-N_FF_CHUNKS // CHUNKS_PER_STEP)
LAST_STEP_CHUNKS = N_FF_CHUNKS - (N_FF_STEPS - 1) * CHUNKS_PER_STEP


def _silu_tanh(x):
    return x * (0.5 + 0.5 * jnp.tanh(0.5 * x))


def _ffn_kernel(xc_ref, xl_ref, gc_ref, gl_ref, wg_ref, wu_ref, wd_ref, y_ref, x_sc, h_sc, acc_sc):
    j = pl.program_id(1)

    @pl.when(j == 0)
    def _():
        x_sc[0:ROWS_CTX, :] = xc_ref[...]
        x_sc[ROWS_CTX:, :] = xl_ref[...]

    def step(n_chunks, first, last):
        x = x_sc[...]
        for sub in range(n_chunks):
            cols = slice(sub * TF, (sub + 1) * TF)
            a = _dot(x, wg_ref[:, cols].astype(BF16))
            u = _dot(x, wu_ref[:, cols].astype(BF16))
            h_sc[sub] = (_silu_tanh(a) * u).astype(BF16)
        for c0 in range(0, D_MODEL, TD):
            out_cols = slice(c0, c0 + TD)
            part = None
            for sub in range(n_chunks):
                t = _dot(h_sc[sub], wd_ref[sub * TF:(sub + 1) * TF, out_cols].astype(BF16))
                part = t if part is None else part + t
            if not first:
                part = acc_sc[:, out_cols] + part
            if last:
                y_ref[0:ROWS_CTX, out_cols] = (part[0:ROWS_CTX] * gc_ref[...]).astype(BF16)
                y_ref[ROWS_CTX:, out_cols] = (part[ROWS_CTX:] * gl_ref[...]).astype(BF16)
            else:
                acc_sc[:, out_cols] = part

    pl.when(j == 0)(functools.partial(step, CHUNKS_PER_STEP, True, False))
    pl.when((j > 0) & (j < N_FF_STEPS - 1))(functools.partial(step, CHUNKS_PER_STEP, False, False))
    pl.when(j == N_FF_STEPS - 1)(functools.partial(step, LAST_STEP_CHUNKS, False, True))


def _ffn(l, x_c, x_l, g_c, g_l, w_gate, w_up, w_down):
    rows = ROWS_CTX + ROWS_LAT
    xin = lambda r: pl.BlockSpec((None, r, D_MODEL), lambda e, j: (e, 0, 0))
    gin = lambda r: pl.BlockSpec((None, r, 1), lambda e, j: (e, 0, 0))
    return pl.pallas_call(
        _ffn_kernel,
        grid=(N_EXPERTS, N_FF_STEPS),
        in_specs=[xin(ROWS_CTX), xin(ROWS_LAT), gin(ROWS_CTX), gin(ROWS_LAT),
                  pl.BlockSpec((None, None, D_MODEL, FF_BLOCK), lambda e, j: (l, e, 0, j)),
                  pl.BlockSpec((None, None, D_MODEL, FF_BLOCK), lambda e, j: (l, e, 0, j)),
                  pl.BlockSpec((None, None, FF_BLOCK, D_MODEL), lambda e, j: (l, e, j, 0))],
        out_specs=pl.BlockSpec((None, rows, D_MODEL), lambda e, j: (e, 0, 0)),
        out_shape=jax.ShapeDtypeStruct((N_EXPERTS, rows, D_MODEL), BF16),
        scratch_shapes=[pltpu.VMEM((rows, D_MODEL), BF16), pltpu.VMEM((CHUNKS_PER_STEP, rows, TF), BF16),
                        pltpu.VMEM((rows, D_MODEL), F32)],
        compiler_params=_params("arbitrary", "arbitrary"),
        name=f"experts{l}",
    )(x_c, x_l, g_c, g_l, w_gate, w_up, w_down)


def _finish(x, res, mod_ref, gf_ref, final):
    y = x + mod_ref[:, 5 * D_MODEL:] * res
    if final:
        y = y * lax.rsqrt(jnp.mean(y * y, axis=-1, keepdims=True) + RMS_EPS) * gf_ref[...]
    return y


def _combine_ctx_kernel(final, x_ref, y_ref, slot_ref, rep_ref, mod_ref, gf_ref, o_ref):
    n_col = N_EXPERTS * CAP_CTX
    col = (lax.broadcasted_iota(jnp.int32, (SEQ, n_col), 1) % CAP_CTX).astype(F32)
    for bb in range(CTX_GROUP):
        rows = slice(bb * SEQ, (bb + 1) * SEQ)
        spread = _dot(slot_ref[rows, :].astype(F32).astype(BF16), rep_ref[...])
        p = jnp.where(spread == col, 1.0, 0.0).astype(BF16)
        y = y_ref[:, bb * CAP_CTX:(bb + 1) * CAP_CTX, :].reshape(n_col, D_MODEL)
        o_ref[rows, :] = _finish(x_ref[rows, :], _dot(p, y), mod_ref, gf_ref, final)


def _combine_ctx(l, final, x_new, y, slot_t, mod4, g_final):
    n_col = N_EXPERTS * CAP_CTX
    rep = (np.arange(n_col)[None, :] // CAP_CTX == np.arange(LANES)[:, None]).astype(np.float32)
    return pl.pallas_call(
        functools.partial(_combine_ctx_kernel, final),
        grid=(BATCH // CTX_GROUP,),
        in_specs=[pl.BlockSpec((CTX_GROUP * SEQ, D_MODEL), lambda b: (b, 0)),
                  pl.BlockSpec((N_EXPERTS, CTX_GROUP * CAP_CTX, D_MODEL), lambda b: (0, b, 0)),
                  pl.BlockSpec((CTX_GROUP * SEQ, LANES), lambda b: (b, 0)),
                  pl.BlockSpec((LANES, n_col), lambda b: (0, 0)),
                  pl.BlockSpec((None, None, 1, 6 * D_MODEL), lambda b: (l, 0, 0, 0)),
                  pl.BlockSpec((1, D_MODEL), lambda b: (0, 0))],
        out_specs=pl.BlockSpec((CTX_GROUP * SEQ, D_MODEL), lambda b: (b, 0)),
        out_shape=jax.ShapeDtypeStruct((N_CTX, D_MODEL), F32),
        compiler_params=_params("arbitrary"),
        name=f"combine_ctx{l}",
    )(x_new, y, slot_t, jnp.asarray(rep, BF16), mod4, g_final.reshape(1, D_MODEL))


TMC = 512


def _combine_lat_kernel(final, x_ref, y_ref, slot_ref, mod_ref, gf_ref, o_ref):
    s_iota = lax.broadcasted_iota(jnp.int32, (TMC, CAP_LAT), 1)
    slot = slot_ref[...]
    res = None
    for e in range(N_EXPERTS):
        p = jnp.where(slot[:, e:e + 1] == s_iota, 1.0, 0.0).astype(BF16)
        t = _dot(p, y_ref[e])
        res = t if res is None else res + t
    o_ref[...] = _finish(x_ref[...], res, mod_ref, gf_ref, final)


def _combine_lat(l, final, x_new, y, slot_t, mod4, g_final):
    nt = DEC_SEQ // TMC
    return pl.pallas_call(
        functools.partial(_combine_lat_kernel, final),
        grid=(DEC_BATCH, nt),
        in_specs=[pl.BlockSpec((TMC, D_MODEL), lambda b, t: (N_CTX // TMC + b * nt + t, 0)),
                  pl.BlockSpec((N_EXPERTS, CAP_LAT, D_MODEL), lambda b, t: (0, ROWS_CTX // CAP_LAT + b, 0)),
                  pl.BlockSpec((TMC, LANES), lambda b, t: (b * nt + t, 0)),
                  pl.BlockSpec((None, None, 1, 6 * D_MODEL), lambda b, t: (l, 1 + b, 0, 0)),
                  pl.BlockSpec((1, D_MODEL), lambda b, t: (0, 0))],
        out_specs=pl.BlockSpec((TMC, D_MODEL), lambda b, t: (b * nt + t, 0)),
        out_shape=jax.ShapeDtypeStruct((N_LAT, D_MODEL), F32),
        compiler_params=_params("arbitrary", "arbitrary"),
        name=f"combine_lat{l}",
    )(x_new, y, slot_t, mod4, g_final.reshape(1, D_MODEL))


def _split_table(t):
    hi = t.astype(BF16)
    return hi, (t - hi.astype(F32)).astype(BF16)


def _dft_tables(n):
    p = np.arange(n, dtype=np.int64)
    ang = ((p[:, None] * p[None, :]) % n).astype(np.float64) * (2.0 * np.pi / n)
    return np.cos(ang).astype(np.float32), np.sin(ang).astype(np.float32)


def _channel_dft_tables():
    c = np.arange(F_WIDTH, dtype=np.int64)
    same = (c[:, None] // HEAD_DIM) == (c[None, :] // HEAD_DIM)
    ang = (((c[:, None] % HEAD_DIM) * (c[None, :] % HEAD_DIM)) % HEAD_DIM).astype(np.float64) * (2.0 * np.pi / HEAD_DIM)
    return (np.where(same, np.cos(ang), 0.0).astype(np.float32),
            np.where(same, np.sin(ang), 0.0).astype(np.float32))


def _rope_tables():
    half = HEAD_DIM // 2
    nf = half // 2
    pos = np.arange(DEC_SEQ)
    inv = 1.0 / (ROPE_BASE ** (np.arange(nf, dtype=np.float64) / nf))
    ang_r = (pos // GRID_W).astype(np.float64)[:, None] * inv
    ang_c = (pos % GRID_W).astype(np.float64)[:, None] * inv

    def head(fn, sign):
        return np.concatenate([sign * fn(ang_r), fn(ang_r), sign * fn(ang_c), fn(ang_c)], axis=-1)

    cos = head(np.cos, 1.0)
    sin = head(np.sin, -1.0)
    return (np.concatenate([cos, cos], axis=-1).astype(np.float32),
            np.concatenate([sin, sin], axis=-1).astype(np.float32))


def _na_bias_tables(rpb):
    cq = np.arange(GRID_W)
    rel_c = np.clip(cq[None, :] - cq[:, None] + NA_COLS - 1, 0, 2 * NA_COLS - 2)
    pick = (rel_c[:, :, None] == np.arange(2 * NA_COLS - 1)).astype(np.float32)
    cs = np.clip(cq - NA_COLS // 2, 0, GRID_W - NA_COLS)
    col_ok = (cq[None, :] >= cs[:, None]) & (cq[None, :] < cs[:, None] + NA_COLS)
    bc = jnp.einsum('lhrj,qkj->lhrqk', rpb, pick, precision=lax.Precision.HIGHEST)
    bc = jnp.where(col_ok[None, None, None], bc, NEG_INF)
    return pl.pallas_call(
        _na_bias_kernel,
        grid=(DEPTH, NA_VARIANTS, NA_HEADS // 2),
        in_specs=[pl.BlockSpec((None, 2, 2 * NA_ROWS - 1, GRID_W, GRID_W), lambda l, v, p: (l, p, 0, 0, 0))],
        out_specs=pl.BlockSpec((None, None, None, 2 * NA_Q, NA_WIN_ROWS * GRID_W),
                               lambda l, v, p: (l, v, p, 0, 0)),
        out_shape=jax.ShapeDtypeStruct((DEPTH, NA_VARIANTS, NA_HEADS // 2, 2 * NA_Q, NA_WIN_ROWS * GRID_W), F32),
        compiler_params=_params("arbitrary", "arbitrary", "arbitrary"),
        name="na_bias",
    )(bc)


def _na_window_plan():
    plan = []
    for g in (0, 1, NA_BLOCKS - 1):
        start = int(np.clip(NA_G * g - NA_ROWS // 2, 0, GRID_ROWS - NA_WIN_ROWS))
        rows = []
        for a in range(NA_G):
            r = NA_G * g + a
            rs = int(np.clip(r - NA_ROWS // 2, 0, GRID_ROWS - NA_ROWS))
            rows.append([start + w - r + NA_ROWS - 1 if rs <= start + w < rs + NA_ROWS else None
                         for w in range(NA_WIN_ROWS)])
        plan.append(rows)
    return plan


def _na_bias_kernel(bc_ref, o_ref):
    outside = jnp.full((GRID_W, GRID_W), NEG_INF, F32)
    for v, rows in enumerate(_na_window_plan()):
        @pl.when(pl.program_id(1) == v)
        def _():
            for half in range(2):
                for a, rel in enumerate(rows):
                    tiles = [outside if rr is None else bc_ref[half, rr] for rr in rel]
                    r0 = half * NA_Q + a * GRID_W
                    o_ref[r0:r0 + GRID_W, :] = jnp.concatenate(tiles, axis=-1)


def kernel(x_prompt, x_sample, cache_win_k, cache_win_v, cache_nat_k, cache_nat_v, c, c_ctx, w_mod, b_mod, g_mix, g_ffn, w_in, w_out, win_sink, nat_rpb, w_router, w_gate, w_up, w_down, g_final):
    x_ctx = x_prompt.reshape(N_CTX, D_MODEL)
    x_lat = x_sample.reshape(N_LAT, D_MODEL)
    cond = jnp.concatenate([c_ctx[None, :], c, jnp.zeros((N_COND - 1 - DEC_BATCH, D_MODEL), F32)], axis=0)
    mod4 = _adaln(cond, w_mod, b_mod).reshape(DEPTH, N_COND, 1, 6 * D_MODEL)

    dft_ch = _channel_dft_tables()
    dft_ctx = _dft_tables(SEQ)
    dft_lat = _dft_tables(DEC_SEQ)
    cos_t, sin_t = _rope_tables()
    cwk = cache_win_k.reshape(DEC_BATCH, DEPTH, PAST_LEN, WIN_KV)
    cwv = cache_win_v.reshape(DEC_BATCH, DEPTH, PAST_LEN, WIN_KV)
    cnk = cache_nat_k.reshape(DEC_BATCH, DEPTH, PAST_LEN, NA_W)
    cnv = cache_nat_v.reshape(DEC_BATCH, DEPTH, PAST_LEN, NA_W)
    r_pad = jnp.pad(w_router, ((0, 0), (0, 0), (0, LANES - N_EXPERTS)))
    na_bias = _na_bias_tables(nat_rpb)

    kvt = ()
    for l in range(DEPTH):
        final = l == DEPTH - 1
        (f, qw, qn), (kw, vw, kn, vn), kvt = _project(l, x_ctx, x_lat, mod4, g_mix, w_in, kvt)

        mixed_ctx = _ctx_mixer(l, win_sink, f, qw, qn, kvt, dft_ch, dft_ctx)
        mixed_f = _lat_fourier(f, dft_ch, dft_lat)
        mixed_w = _lat_window(l, win_sink, qw, kw, vw, cwk, cwv, cos_t, sin_t)
        mixed_n = _lat_neighbourhood(l, qn, kn, vn, cnk, cnv, na_bias)

        r_hi, r_lo = _split_table(r_pad[l])
        x_new, h, lg_t = _outproj(l, x_ctx, x_lat, mixed_ctx, mixed_f, mixed_w, mixed_n, mod4, g_ffn,
                                  w_out, r_hi, r_lo)

        slot_c, aff_c, slot_ct, slot_l, aff_l, slot_lt = _route(lg_t)
        xg_c, gate_c = _gather_ctx(h, slot_c, aff_c)
        xg_l, gate_l = _gather_lat(h, slot_l, aff_l)
        y = _ffn(l, xg_c, xg_l, gate_c, gate_l, w_gate, w_up, w_down)
        x_ctx = _combine_ctx(l, final, x_new, y, slot_ct, mod4, g_final)
        x_lat = _combine_lat(l, final, x_new, y, slot_lt, mod4, g_final)

    y_prompt = x_ctx.reshape(BATCH, SEQ, D_MODEL)
    y_sample = x_lat.reshape(DEC_BATCH, DEC_SEQ, D_MODEL)
    new_kv = [t.reshape(BATCH, DEPTH, w // HEAD_DIM, HEAD_DIM, SEQ).transpose(0, 1, 4, 2, 3)
              for t, w in zip(kvt, KV_WIDTHS)]
    return (y_prompt, y_sample, *new_kv)
```

```python
import functools

import numpy as np
import jax
import jax.numpy as jnp
from jax import lax
from jax.experimental import pallas as pl
from jax.experimental.pallas import tpu as pltpu

D_MODEL = 1024
BATCH = 16
SEQ = 256
DEPTH = 2
DEC_BATCH = 2
DEC_SEQ = 2048
PAST_LEN = 256
GRID_W = 64
HEAD_DIM = 64
F_WIDTH = 256
WIN_HEADS = 6
WIN_KV_HEADS = 2
WINDOW = 128
WIN_BLOCK = 128
NA_HEADS = 6
NA_ROWS = 8
NA_COLS = 16
N_EXPERTS = 16
EC_CAPACITY = 2
D_FF = 2816
ROPE_BASE = 10000.0
RMS_EPS = 1e-6
NEG_INF = -1e30
ATTN_SCALE = HEAD_DIM ** -0.5
WIN_Q = WIN_HEADS * HEAD_DIM
WIN_KV = WIN_KV_HEADS * HEAD_DIM
NA_W = NA_HEADS * HEAD_DIM
N_IN = F_WIDTH + WIN_Q + 2 * WIN_KV + 3 * NA_W
SPLITS = (0, F_WIDTH, F_WIDTH + WIN_Q, F_WIDTH + WIN_Q + WIN_KV, F_WIDTH + WIN_Q + 2 * WIN_KV,
          F_WIDTH + WIN_Q + 2 * WIN_KV + NA_W, F_WIDTH + WIN_Q + 2 * WIN_KV + 2 * NA_W, N_IN)

N_CTX = BATCH * SEQ
N_LAT = DEC_BATCH * DEC_SEQ
N_TOK = N_CTX + N_LAT
GRID_ROWS = DEC_SEQ // GRID_W
CAP_CTX = EC_CAPACITY * SEQ // N_EXPERTS
CAP_LAT = EC_CAPACITY * DEC_SEQ // N_EXPERTS
ROWS_CTX = BATCH * CAP_CTX
ROWS_LAT = DEC_BATCH * CAP_LAT
N_COND = 8

LANES = 128
MXU_COLS = 256
TM = 512
TN_MOD = 1536
TF = 256
TD = 256
VMEM_LIMIT = 56 * 1024 * 1024

F32 = jnp.float32
BF16 = jnp.bfloat16


def _params(*sem):
    return pltpu.CompilerParams(dimension_semantics=sem, vmem_limit_bytes=VMEM_LIMIT)


def _dot(a, b):
    return jnp.dot(a, b, preferred_element_type=F32)


def _dot_nt(a, b):
    return lax.dot_general(a, b, (((1,), (1,)), ((), ())), preferred_element_type=F32)


def _split(x):
    hi = x.astype(BF16)
    lo = (x - hi.astype(F32)).astype(BF16)
    return hi, lo


def _dot3(a_hi, a_lo, b_hi, b_lo):
    return _dot(a_hi, b_hi) + (_dot(a_lo, b_hi) + _dot(a_hi, b_lo))


def _silu(x):
    return x / (1.0 + jnp.exp(-x))


def _rms_mod(x, g, shift, scale):
    y = x * lax.rsqrt(jnp.mean(x * x, axis=-1, keepdims=True) + RMS_EPS)
    return (y * g) * (1.0 + scale) + shift


def _softmax_parts(parts, sink=None):
    m = parts[0].max(axis=-1, keepdims=True)
    for s in parts[1:]:
        m = jnp.maximum(m, s.max(axis=-1, keepdims=True))
    if sink is not None:
        m = jnp.maximum(m, sink)
    es = [jnp.exp(s - m) for s in parts]
    den = es[0].sum(axis=-1, keepdims=True)
    for e in es[1:]:
        den = den + e.sum(axis=-1, keepdims=True)
    if sink is not None:
        den = den + jnp.exp(sink - m)
    inv = 1.0 / den
    return [e * inv for e in es]


def _cond_of_block(i):
    n_ctx_blocks = N_CTX // TM
    return jnp.where(i < n_ctx_blocks, 0, 1 + (i - n_ctx_blocks) // (DEC_SEQ // TM))


def _two_stream_specs(width):
    n_ctx_blocks = N_CTX // TM
    ctx = pl.BlockSpec((TM, width), lambda i: (jnp.minimum(i, n_ctx_blocks - 1), 0))
    lat = pl.BlockSpec((TM, width), lambda i: (jnp.maximum(i - n_ctx_blocks, 0), 0))
    return ctx, lat


def _pick_stream(ctx_ref, lat_ref):
    return jnp.where(pl.program_id(0) < N_CTX // TM, ctx_ref[...], lat_ref[...])


def _adaln_kernel(c_ref, w_ref, b_ref, o_ref):
    s_hi, s_lo = _split(_silu(c_ref[...]))
    w_hi, w_lo = _split(w_ref[...])
    o_ref[...] = _dot3(s_hi, s_lo, w_hi, w_lo) + b_ref[...]


def _adaln(cond, w_mod, b_mod):
    return pl.pallas_call(
        _adaln_kernel,
        grid=(DEPTH, 6 * D_MODEL // TN_MOD),
        in_specs=[
            pl.BlockSpec((N_COND, D_MODEL), lambda l, j: (0, 0)),
            pl.BlockSpec((None, D_MODEL, TN_MOD), lambda l, j: (l, 0, j)),
            pl.BlockSpec((None, 1, TN_MOD), lambda l, j: (l, 0, j)),
        ],
        out_specs=pl.BlockSpec((None, N_COND, TN_MOD), lambda l, j: (l, 0, j)),
        out_shape=jax.ShapeDtypeStruct((DEPTH, N_COND, 6 * D_MODEL), F32),
        compiler_params=_params("arbitrary", "arbitrary"),
        name="adaln",
    )(cond, w_mod, b_mod.reshape(DEPTH, 1, 6 * D_MODEL))


KV_NAMES = ("kw", "vw", "kn", "vn")
KV_WIDTHS = (WIN_KV, WIN_KV, NA_W, NA_W)
KV_SPLITS = (SPLITS[2], SPLITS[3], SPLITS[5], SPLITS[6])
KV_TOTAL = sum(KV_WIDTHS)
Q_SPLITS = (SPLITS[0], SPLITS[1], SPLITS[4])


def _proj_kernel(n_prev, xc_ref, xl_ref, mod_ref, g_ref, w_ref, *rest):
    rest = rest[n_prev:]
    f_ref, qw_ref, qn_ref = rest[0:3]
    lat_kv = rest[3:7]
    ctx_kvt = rest[7:11]
    wb_ref, wt_ref = rest[11:13]
    i = pl.program_id(0)

    @pl.when(i == 0)
    def _():
        wb_ref[...] = w_ref[...].astype(BF16)
        r0 = 0
        for c0, width in zip(KV_SPLITS, KV_WIDTHS):
            wt_ref[r0:r0 + width, :] = w_ref[:, c0:c0 + width].T.astype(BF16)
            r0 += width

    natural = dict(zip(SPLITS[:-1], (f_ref, qw_ref) + tuple(lat_kv[0:2]) + (qn_ref,) + tuple(lat_kv[2:4])))

    def hidden(x_ref):
        return _rms_mod(x_ref[...], g_ref[...], mod_ref[:, 0:D_MODEL], mod_ref[:, D_MODEL:2 * D_MODEL]).astype(BF16)

    def project_tiles(h, wanted):
        for t0 in range(0, N_IN, MXU_COLS):
            hits = [(a, b) for a, b in zip(SPLITS[:-1], SPLITS[1:])
                    if a in wanted and max(a, t0) < min(b, t0 + MXU_COLS)]
            if not hits:
                continue
            acc = _dot(h, wb_ref[:, t0:t0 + MXU_COLS])
            for a, b in hits:
                lo, hi = max(a, t0), min(b, t0 + MXU_COLS)
                natural[a][:, lo - a:hi - a] = acc[:, lo - t0:hi - t0].astype(natural[a].dtype)

    @pl.when(i < N_CTX // TM)
    def _():
        h = hidden(xc_ref)
        project_tiles(h, Q_SPLITS)
        kvt = _dot_nt(wt_ref[...], h)
        r0 = 0
        for o_ref, width in zip(ctx_kvt, KV_WIDTHS):
            for bb in range(TM // SEQ):
                o_ref[bb] = kvt[r0:r0 + width, bb * SEQ:(bb + 1) * SEQ]
            r0 += width

    @pl.when(i >= N_CTX // TM)
    def _():
        project_tiles(hidden(xl_ref), Q_SPLITS + KV_SPLITS)


def _project(l, x_ctx, x_lat, mod4, g_mix, w_in, prev_kvt):
    n_ctx_blocks = N_CTX // TM
    xc_spec, xl_spec = _two_stream_specs(D_MODEL)
    both = lambda w: pl.BlockSpec((TM, w), lambda i: (i, 0))
    lat = lambda w: pl.BlockSpec((TM, w), lambda i: (jnp.maximum(i - n_ctx_blocks, 0), 0))
    ctx_t = lambda w: pl.BlockSpec((TM // SEQ, None, w, SEQ),
                                   lambda i: (jnp.minimum(i, n_ctx_blocks - 1), l, 0, 0))
    n_prev = len(prev_kvt)
    n_in = 5
    outs = pl.pallas_call(
        functools.partial(_proj_kernel, n_prev),
        grid=(N_TOK // TM,),
        in_specs=[
            xc_spec, xl_spec,
            pl.BlockSpec((None, None, 1, 6 * D_MODEL), lambda i: (l, _cond_of_block(i), 0, 0)),
            pl.BlockSpec((None, 1, D_MODEL), lambda i: (l, 0, 0)),
            pl.BlockSpec((None, D_MODEL, N_IN), lambda i: (l, 0, 0)),
        ] + [pl.BlockSpec(memory_space=pl.ANY)] * n_prev,
        out_specs=[both(F_WIDTH), both(WIN_Q), both(NA_W)] + [lat(w) for w in KV_WIDTHS]
                  + [ctx_t(w) for w in KV_WIDTHS],
        out_shape=[jax.ShapeDtypeStruct((N_TOK, w), dt) for w, dt in ((F_WIDTH, F32), (WIN_Q, F32), (NA_W, BF16))]
                  + [jax.ShapeDtypeStruct((N_LAT, w), dt)
                     for w, dt in zip(KV_WIDTHS, (F32, BF16, BF16, BF16))]
                  + [jax.ShapeDtypeStruct((BATCH, DEPTH, w, SEQ), F32) for w in KV_WIDTHS],
        input_output_aliases={n_in + k: 7 + k for k in range(n_prev)},
        scratch_shapes=[pltpu.VMEM((D_MODEL, N_IN), BF16), pltpu.VMEM((KV_TOTAL, D_MODEL), BF16)],
        compiler_params=_params("arbitrary"),
        name=f"project{l}",
    )(x_ctx, x_lat, mod4, g_mix.reshape(DEPTH, 1, D_MODEL), w_in, *prev_kvt)
    return outs[0:3], outs[3:7], outs[7:11]


def _lane_is_low(shape):
    return lax.broadcasted_iota(jnp.int32, shape, len(shape) - 1) < HEAD_DIM


def _swap_halves(x):
    return pltpu.roll(x, HEAD_DIM, axis=x.ndim - 1)


def _win_kv_copy(h):
    return 0 if (h // (WIN_HEADS // WIN_KV_HEADS)) == (h % 2) else 1


def _stack_heads(q_pairs, heads):
    low = _lane_is_low(q_pairs[heads[0] // 2].shape)
    rows = [jnp.where(low if h % 2 == 0 else jnp.logical_not(low), q_pairs[h // 2], 0.0).astype(BF16)
            for h in heads]
    return rows[0] if len(rows) == 1 else jnp.concatenate(rows, axis=0)


def _per_head_column(values, rows_per_head):
    blk = lax.broadcasted_iota(jnp.int32, (len(values) * rows_per_head, 1), 0) // rows_per_head
    col = jnp.full(blk.shape, values[0], F32)
    for i in range(1, len(values)):
        col = jnp.where(blk == i, values[i], col)
    return col


def _attend(q_stack, kv_list, extra_logit=None, transposed=False):
    scores = []
    for k, _, post in kv_list:
        s = _dot(q_stack, k) if transposed else _dot_nt(q_stack, k)
        scores.append(post(s) if post is not None else s)
    probs = _softmax_parts(scores, extra_logit)
    o = None
    for p, (_, v, _) in zip(probs, kv_list):
        t = _dot_nt(p.astype(BF16), v) if transposed else _dot(p.astype(BF16), v)
        o = t if o is None else o + t
    return o


def _merge_pair(o_even, o_odd):
    return jnp.where(_lane_is_low(o_even.shape), o_even, o_odd)


def _gqa_attention(q_pairs, rows, kv_for_copy, sinks, transposed=False):
    per_head = {}
    for copy in (0, 1):
        heads = [h for h in range(WIN_HEADS) if _win_kv_copy(h) == copy]
        o = _attend(_stack_heads(q_pairs, heads), kv_for_copy(copy),
                    _per_head_column([sinks[h] for h in heads], rows), transposed)
        for i, h in enumerate(heads):
            per_head[h] = o[i * rows:(i + 1) * rows]
    return [_merge_pair(per_head[2 * j], per_head[2 * j + 1]) for j in range(WIN_HEADS // 2)]


def _swap_row_halves(x):
    return jnp.concatenate([x[HEAD_DIM:], x[:HEAD_DIM]], axis=0)


MIX_GROUP = 2


def _ctx_mixer_kernel(l, sink_ref, f_ref, qw_ref, qn_ref, kw_ref, vw_ref, kn_ref, vn_ref,
                      bc_ref, bs_ref, cs_ref, ss_ref, o_ref):
    bc_hi, bc_lo = _split(bc_ref[...])
    bs_hi, bs_lo = _split(bs_ref[...])
    cs_hi, cs_lo = _split(cs_ref[...])
    ss_hi, ss_lo = _split(ss_ref[...])
    for bb in range(MIX_GROUP):
        rows = slice(bb * SEQ, (bb + 1) * SEQ)
        f_hi, f_lo = _split(f_ref[rows, :])
        fc_hi, fc_lo = _split(_dot3(f_hi, f_lo, bc_hi, bc_lo))
        fs_hi, fs_lo = _split(_dot3(f_hi, f_lo, bs_hi, bs_lo))
        z = _dot3(cs_hi, cs_lo, fc_hi, fc_lo) - _dot3(ss_hi, ss_lo, fs_hi, fs_lo)
        o_ref[rows, 0:F_WIDTH] = (z * (SEQ * HEAD_DIM) ** -0.5).astype(BF16)

        kv = [(kw_ref[bb].astype(BF16), vw_ref[bb].astype(BF16)),
              (_swap_row_halves(kw_ref[bb]).astype(BF16), _swap_row_halves(vw_ref[bb]).astype(BF16))]
        q_pairs = [qw_ref[rows, LANES * j:LANES * (j + 1)] * ATTN_SCALE for j in range(WIN_HEADS // 2)]
        outs = _gqa_attention(q_pairs, SEQ, lambda c: [(kv[c][0], kv[c][1], None)],
                              [sink_ref[l, h] for h in range(WIN_HEADS)], transposed=True)
        for j, o in enumerate(outs):
            o_ref[rows, F_WIDTH + LANES * j:F_WIDTH + LANES * (j + 1)] = o.astype(BF16)

        for j in range(NA_HEADS // 2):
            sl = slice(LANES * j, LANES * (j + 1))
            q_pairs = {j: qn_ref[rows, sl] * ATTN_SCALE}
            o = _attend(_stack_heads(q_pairs, (2 * j, 2 * j + 1)),
                        [(kn_ref[bb, sl, :].astype(BF16), vn_ref[bb, sl, :].astype(BF16), None)], transposed=True)
            base = F_WIDTH + WIN_Q + LANES * j
            o_ref[rows, base:base + LANES] = _merge_pair(o[:SEQ], o[SEQ:]).astype(BF16)


def _ctx_mixer(l, win_sink, f, qw, qn, kvt, dft_ch, dft_seq):
    row = lambda w: pl.BlockSpec((MIX_GROUP * SEQ, w), lambda b: (b, 0))
    col = lambda w: pl.BlockSpec((MIX_GROUP, None, w, SEQ), lambda b: (b, l, 0, 0))
    const = lambda n: pl.BlockSpec((n, n), lambda b: (0, 0))
    return pl.pallas_call(
        functools.partial(_ctx_mixer_kernel, l),
        grid=(BATCH // MIX_GROUP,),
        in_specs=[pl.BlockSpec(memory_space=pltpu.SMEM),
                  row(F_WIDTH), row(WIN_Q), row(NA_W)] + [col(w) for w in KV_WIDTHS]
                 + [const(F_WIDTH)] * 2 + [const(SEQ)] * 2,
        out_specs=pl.BlockSpec((MIX_GROUP * SEQ, D_MODEL), lambda b: (b, 0)),
        out_shape=jax.ShapeDtypeStruct((N_CTX, D_MODEL), BF16),
        compiler_params=_params("arbitrary"),
        name=f"ctx_mixer{l}",
    )(win_sink, f, qw, qn, *kvt, *dft_ch, *dft_seq)


FT_ROWS = 512


def _lat_fourier_kernel(f_ref, bc_ref, bs_ref, cs_ref, ss_ref, o_ref, st_ref):
    b = pl.program_id(1)

    @pl.when(pl.program_id(0) == 0)
    def _():
        f_hi, f_lo = _split(f_ref[pl.ds(pl.multiple_of(b * DEC_SEQ, DEC_SEQ), DEC_SEQ), :])
        fc_hi, fc_lo = _split(_dot3(f_hi, f_lo, *_split(bc_ref[...])))
        fs_hi, fs_lo = _split(_dot3(f_hi, f_lo, *_split(bs_ref[...])))
        st_ref[b, 0] = fc_hi
        st_ref[b, 1] = fc_lo
        st_ref[b, 2] = fs_hi
        st_ref[b, 3] = fs_lo

    z = (_dot3(*_split(cs_ref[...]), st_ref[b, 0], st_ref[b, 1])
         - _dot3(*_split(ss_ref[...]), st_ref[b, 2], st_ref[b, 3]))
    o_ref[...] = (z * (DEC_SEQ * HEAD_DIM) ** -0.5).astype(BF16)


def _lat_fourier(f, dft_ch, dft_seq):
    nrb = DEC_SEQ // FT_ROWS
    const = pl.BlockSpec((F_WIDTH, F_WIDTH), lambda r, b: (0, 0))
    rows = pl.BlockSpec((FT_ROWS, DEC_SEQ), lambda r, b: (r, 0))
    return pl.pallas_call(
        _lat_fourier_kernel,
        grid=(nrb, DEC_BATCH),
        in_specs=[pl.BlockSpec((N_LAT, F_WIDTH), lambda r, b: (1, 0))] + [const] * 2 + [rows] * 2,
        out_specs=pl.BlockSpec((FT_ROWS, F_WIDTH), lambda r, b: (b * nrb + r, 0)),
        out_shape=jax.ShapeDtypeStruct((N_LAT, F_WIDTH), BF16),
        scratch_shapes=[pltpu.VMEM((DEC_BATCH, 4, DEC_SEQ, F_WIDTH), BF16)],
        compiler_params=_params("arbitrary", "arbitrary"),
        name="lat_fourier",
    )(f, *dft_ch, *dft_seq)


def _rope(x, cos, sin_signed):
    n = x.shape[-1]
    lane = lax.broadcasted_iota(jnp.int32, x.shape, x.ndim - 1)
    first = (lane % 32) < 16
    partner = jnp.where(first, pltpu.roll(x, n - 16, axis=x.ndim - 1), pltpu.roll(x, 16, axis=x.ndim - 1))
    return x * cos + partner * sin_signed


def _win_kernel(l, sink_ref, q_ref, k_ref, v_ref, ck_ref, cv_ref, cos_ref, sin_ref, cosq_ref, sinq_ref,
                o_ref, kp_ref, vp_ref, cp_ref):
    n = pl.program_id(1)
    nb = DEC_SEQ // WIN_BLOCK
    pad = WIN_BLOCK

    @pl.when(n == 0)
    def _():
        zeros = jnp.zeros((pad, LANES), BF16)
        kr = _rope(k_ref[...], cos_ref[...], sin_ref[...])
        v = v_ref[...]
        for idx, (kk, vv) in enumerate(((kr, v), (_swap_halves(kr), _swap_halves(v)))):
            kp_ref[idx, 0:pad] = zeros
            kp_ref[idx, pad + DEC_SEQ:] = zeros
            kp_ref[idx, pad:pad + DEC_SEQ] = kk.astype(BF16)
            vp_ref[idx, 0:pad] = zeros
            vp_ref[idx, pad + DEC_SEQ:] = zeros
            vp_ref[idx, pad:pad + DEC_SEQ] = vv.astype(BF16)
        ck = ck_ref[...]
        cv = cv_ref[...]
        cp_ref[0] = ck.astype(BF16)
        cp_ref[1] = _swap_halves(ck).astype(BF16)
        cp_ref[2] = cv.astype(BF16)
        cp_ref[3] = _swap_halves(cv).astype(BF16)

    lo = jnp.where(n == 0, WIN_BLOCK, 0)
    hi = jnp.where(n == nb - 1, 2 * WIN_BLOCK, 3 * WIN_BLOCK)

    i = lax.broadcasted_iota(jnp.int32, (WIN_BLOCK, 3 * WIN_BLOCK), 0)
    j = lax.broadcasted_iota(jnp.int32, (WIN_BLOCK, 3 * WIN_BLOCK), 1)
    mask = (j >= i + WIN_BLOCK - WINDOW) & (j <= i + WIN_BLOCK + WINDOW) & (j >= lo) & (j < hi)
    band_bias = jnp.where(mask, 0.0, NEG_INF)

    def band(s):
        heads = s.shape[0] // WIN_BLOCK
        return (s.reshape(heads, WIN_BLOCK, s.shape[1]) + band_bias[None]).reshape(s.shape)

    start = pl.multiple_of(n * WIN_BLOCK, WIN_BLOCK)
    win = pl.ds(start, 3 * WIN_BLOCK)
    q_pairs = [_rope(q_ref[:, LANES * jp:LANES * (jp + 1)], cosq_ref[...], sinq_ref[...]) * ATTN_SCALE
               for jp in range(WIN_HEADS // 2)]
    outs = _gqa_attention(
        q_pairs, WIN_BLOCK,
        lambda c: [(kp_ref[c, win, :], vp_ref[c, win, :], band), (cp_ref[c], cp_ref[2 + c], None)],
        [sink_ref[l, h] for h in range(WIN_HEADS)])
    for jp, o in enumerate(outs):
        o_ref[:, LANES * jp:LANES * (jp + 1)] = o.astype(BF16)


def _lat_window(l, win_sink, qw, kw, vw, cache_k, cache_v, cos_t, sin_t):
    nb = DEC_SEQ // WIN_BLOCK
    kv_spec = pl.BlockSpec((DEC_SEQ, WIN_KV), lambda b, n: (b, 0))
    cache_spec = pl.BlockSpec((None, None, PAST_LEN, WIN_KV), lambda b, n: (b, l, 0, 0))
    tab_all = pl.BlockSpec((DEC_SEQ, LANES), lambda b, n: (0, 0))
    tab_blk = pl.BlockSpec((WIN_BLOCK, LANES), lambda b, n: (n, 0))
    return pl.pallas_call(
        functools.partial(_win_kernel, l),
        grid=(DEC_BATCH, nb),
        in_specs=[pl.BlockSpec(memory_space=pltpu.SMEM),
                  pl.BlockSpec((WIN_BLOCK, WIN_Q), lambda b, n: (N_CTX // WIN_BLOCK + b * nb + n, 0)),
                  kv_spec, kv_spec, cache_spec, cache_spec, tab_all, tab_all, tab_blk, tab_blk],
        out_specs=pl.BlockSpec((WIN_BLOCK, WIN_Q), lambda b, n: (b * nb + n, 0)),
        out_shape=jax.ShapeDtypeStruct((N_LAT, WIN_Q), BF16),
        scratch_shapes=[pltpu.VMEM((2, DEC_SEQ + 2 * WIN_BLOCK, LANES), BF16),
                        pltpu.VMEM((2, DEC_SEQ + 2 * WIN_BLOCK, LANES), BF16),
                        pltpu.VMEM((4, PAST_LEN, LANES), BF16)],
        compiler_params=_params("arbitrary", "arbitrary"),
        name=f"lat_window{l}",
    )(win_sink, qw, kw, vw, cache_k, cache_v, cos_t, sin_t, cos_t, sin_t)


NA_G = 4
NA_Q = NA_G * GRID_W
NA_WIN_ROWS = NA_ROWS + NA_G
NA_BLOCKS = GRID_ROWS // NA_G
NA_VARIANTS = 3


def _na_block_start(g):
    return jnp.clip(NA_G * g - NA_ROWS // 2, 0, GRID_ROWS - NA_WIN_ROWS)


def _na_kernel(q_ref, k_ref, v_ref, ck_ref, cv_ref, bias_ref, o_ref):
    g = pl.program_id(1)
    start = pl.multiple_of(_na_block_start(g) * GRID_W, GRID_W)
    win = pl.ds(start, NA_WIN_ROWS * GRID_W)
    for jp in range(NA_HEADS // 2):
        sl = slice(LANES * jp, LANES * (jp + 1))
        q_stack = _stack_heads({jp: q_ref[:, sl] * ATTN_SCALE}, (2 * jp, 2 * jp + 1))
        bias = bias_ref[jp]
        o = _attend(q_stack, [(k_ref[win, sl].astype(BF16), v_ref[win, sl].astype(BF16), lambda s: s + bias),
                              (ck_ref[:, sl].astype(BF16), cv_ref[:, sl].astype(BF16), None)])
        o_ref[:, sl] = _merge_pair(o[:NA_Q], o[NA_Q:]).astype(BF16)


def _lat_neighbourhood(l, qn, kn, vn, cache_k, cache_v, bias_tab):
    kv_spec = pl.BlockSpec((DEC_SEQ, NA_W), lambda b, g: (b, 0))
    cache_spec = pl.BlockSpec((None, None, PAST_LEN, NA_W), lambda b, g: (b, l, 0, 0))
    variant = lambda g: jnp.where(g == 0, 0, jnp.where(g == NA_BLOCKS - 1, 2, 1))
    return pl.pallas_call(
        _na_kernel,
        grid=(DEC_BATCH, NA_BLOCKS),
        in_specs=[pl.BlockSpec((NA_Q, NA_W), lambda b, g: (N_CTX // NA_Q + b * NA_BLOCKS + g, 0)),
                  kv_spec, kv_spec, cache_spec, cache_spec,
                  pl.BlockSpec((None, None, NA_HEADS // 2, 2 * NA_Q, NA_WIN_ROWS * GRID_W),
                               lambda b, g: (l, variant(g), 0, 0, 0))],
        out_specs=pl.BlockSpec((NA_Q, NA_W), lambda b, g: (b * NA_BLOCKS + g, 0)),
        out_shape=jax.ShapeDtypeStruct((N_LAT, NA_W), BF16),
        compiler_params=_params("arbitrary", "arbitrary"),
        name=f"lat_neighbourhood{l}",
    )(qn, kn, vn, cache_k, cache_v, bias_tab)


OUTPROJ_ROWS = 256


def _outproj_kernel(xc_ref, xl_ref, mc_ref, mf_ref, mw_ref, mn_ref, mod_ref, g_ref, w_ref, r_hi, r_lo,
                    x_ref, h_ref, lg_ref, wb_ref):
    i = pl.program_id(0)

    @pl.when(i == 0)
    def _():
        wb_ref[...] = w_ref[...].astype(BF16)

    d = D_MODEL

    def block(x_in_ref, mixed_rows):
        for r0 in range(0, TM, OUTPROJ_ROWS):
            rows = slice(r0, r0 + OUTPROJ_ROWS)
            x = x_in_ref[rows, :] + mod_ref[:, 2 * d:3 * d] * _dot(mixed_rows(rows), wb_ref[...])
            x_ref[rows, :] = x
            h = _rms_mod(x, g_ref[...], mod_ref[:, 3 * d:4 * d], mod_ref[:, 4 * d:5 * d])
            h_ref[rows, :] = h.astype(BF16)
            h_hi, h_lo = _split(h)
            lg_ref[:, rows] = _dot3(h_hi, h_lo, r_hi[...], r_lo[...]).T[0:N_EXPERTS, :]

    @pl.when(i < N_CTX // TM)
    def _():
        block(xc_ref, lambda rows: mc_ref[rows, :])

    @pl.when(i >= N_CTX // TM)
    def _():
        block(xl_ref, lambda rows: jnp.concatenate([mf_ref[rows, :], mw_ref[rows, :], mn_ref[rows, :]], axis=1))


def _outproj(l, x_ctx, x_lat, mixed_ctx, mixed_f, mixed_w, mixed_n, mod4, g_ffn, w_out, r_hi, r_lo):
    n_ctx_blocks = N_CTX // TM
    xc_spec, xl_spec = _two_stream_specs(D_MODEL)
    lat = lambda w: pl.BlockSpec((TM, w), lambda i: (jnp.maximum(i - n_ctx_blocks, 0), 0))
    whole = lambda shape: pl.BlockSpec(shape, lambda i: (0,) * len(shape))
    row = lambda w: pl.BlockSpec((TM, w), lambda i: (i, 0))
    return pl.pallas_call(
        _outproj_kernel,
        grid=(N_TOK // TM,),
        in_specs=[xc_spec, xl_spec,
                  pl.BlockSpec((TM, D_MODEL), lambda i: (jnp.minimum(i, n_ctx_blocks - 1), 0)),
                  lat(F_WIDTH), lat(WIN_Q), lat(NA_W),
                  pl.BlockSpec((None, None, 1, 6 * D_MODEL), lambda i: (l, _cond_of_block(i), 0, 0)),
                  pl.BlockSpec((None, 1, D_MODEL), lambda i: (l, 0, 0)),
                  pl.BlockSpec((None, D_MODEL, D_MODEL), lambda i: (l, 0, 0)),
                  whole((D_MODEL, LANES)), whole((D_MODEL, LANES))],
        out_specs=[row(D_MODEL), row(D_MODEL), pl.BlockSpec((N_EXPERTS, TM), lambda i: (0, i))],
        out_shape=[jax.ShapeDtypeStruct((N_TOK, D_MODEL), F32),
                   jax.ShapeDtypeStruct((N_TOK, D_MODEL), BF16),
                   jax.ShapeDtypeStruct((N_EXPERTS, N_TOK), F32)],
        scratch_shapes=[pltpu.VMEM((D_MODEL, D_MODEL), BF16)],
        compiler_params=_params("arbitrary"),
        name=f"outproj{l}",
    )(x_ctx, x_lat, mixed_ctx, mixed_f, mixed_w, mixed_n, mod4, g_ffn.reshape(DEPTH, 1, D_MODEL), w_out,
      r_hi, r_lo)


PREFIX_CHUNK = 256
MANTISSA_STEPS = 44


def _prefix_exclusive(m):
    rows, n = m.shape
    t0 = lax.broadcasted_iota(jnp.int32, (PREFIX_CHUNK, PREFIX_CHUNK), 0)
    t1 = lax.broadcasted_iota(jnp.int32, (PREFIX_CHUNK, PREFIX_CHUNK), 1)
    upper = jnp.where(t0 < t1, 1.0, 0.0).astype(BF16)
    carry = jnp.zeros((rows, 1), F32)
    outs = []
    for c in range(n // PREFIX_CHUNK):
        blk = m[:, c * PREFIX_CHUNK:(c + 1) * PREFIX_CHUNK]
        outs.append(_dot(blk.astype(BF16), upper) + carry)
        carry = carry + blk.sum(axis=-1, keepdims=True)
    return outs[0] if len(outs) == 1 else jnp.concatenate(outs, axis=-1)


ROUTE_STREAMS = ((0, BATCH, CAP_CTX), (N_CTX, DEC_BATCH, CAP_LAT))


def _route_kernel(lg_ref, *out_refs):
    affs, caps = [], []
    for t0, groups, cap in ROUTE_STREAMS:
        n = (N_CTX if t0 == 0 else N_LAT) // groups
        x = jnp.concatenate([lg_ref[:, t0 + g * n:t0 + (g + 1) * n] for g in range(groups)], axis=0)
        x = x.reshape(groups, N_EXPERTS, n)
        e = jnp.exp(x - x.max(axis=1, keepdims=True))
        affs.append((e / e.sum(axis=1, keepdims=True)).reshape(groups * N_EXPERTS, n))
        caps.append(float(cap))

    def count_ge(aff, t):
        return jnp.where(aff >= t, 1.0, 0.0).sum(axis=-1, keepdims=True)

    def keep_if_enough(aff, cap, cand, otherwise):
        return jnp.where(count_ge(aff, cand) >= cap, cand, otherwise)

    above = [jnp.full((aff.shape[0], 1), 2.0, F32) for aff in affs]
    for s in (64, 32, 16, 8, 4, 2, 1):
        cands = [a * (2.0 ** -s) for a in above]
        above = [jnp.where(count_ge(aff, c) >= cap, a, c) for aff, cap, a, c in zip(affs, caps, above, cands)]
    bases = [keep_if_enough(aff, cap, a * 0.5, 0.0) for aff, cap, a in zip(affs, caps, above)]

    def refine(_, carry):
        thrs, incs = carry
        thrs = tuple(keep_if_enough(aff, cap, t + i, t) for aff, cap, t, i in zip(affs, caps, thrs, incs))
        return thrs, tuple(i * 0.5 for i in incs)

    thrs, _ = lax.fori_loop(0, MANTISSA_STEPS, refine, (tuple(bases), tuple(b * 0.5 for b in bases)))

    for k, ((_, groups, _), aff, cap, thr) in enumerate(zip(ROUTE_STREAMS, affs, caps, thrs)):
        slot_ref, aff_ref, slott_ref = out_refs[3 * k:3 * k + 3]
        n = aff.shape[1]
        gt = jnp.where(aff > thr, 1.0, 0.0)
        eq = jnp.where(aff == thr, 1.0, 0.0)
        need = cap - gt.sum(axis=-1, keepdims=True)
        sel = gt + eq * jnp.where(_prefix_exclusive(eq) < need, 1.0, 0.0)
        slot = jnp.where(sel > 0.0, _prefix_exclusive(sel), -1.0)
        slot_ref[...] = slot.astype(jnp.int32)
        aff_ref[...] = aff
        unused = jnp.full((LANES - N_EXPERTS, n), -1.0, F32)
        for g in range(groups):
            tile = jnp.concatenate([slot[g * N_EXPERTS:(g + 1) * N_EXPERTS], unused], axis=0)
            slott_ref[g * n:(g + 1) * n, :] = tile.T.astype(jnp.int32)


def _route(lg_t):
    shapes = []
    for t0, groups, _ in ROUTE_STREAMS:
        n_tok = N_CTX if t0 == 0 else N_LAT
        rows, n = groups * N_EXPERTS, n_tok // groups
        shapes += [((rows, n), jnp.int32), ((rows, n), F32), ((n_tok, LANES), jnp.int32)]
    return pl.pallas_call(
        _route_kernel,
        grid=(1,),
        in_specs=[pl.BlockSpec((N_EXPERTS, N_TOK), lambda i: (0, 0))],
        out_specs=[pl.BlockSpec(shape, lambda i: (0, 0)) for shape, _ in shapes],
        out_shape=[jax.ShapeDtypeStruct(shape, dt) for shape, dt in shapes],
        compiler_params=_params("arbitrary"),
        name="route",
    )(lg_t)


CTX_GROUP = 4


def _gather_ctx_kernel(h_ref, slot_ref, aff_ref, x_ref, g_ref, p_ref):
    s_iota = lax.broadcasted_iota(jnp.int32, (CAP_CTX, SEQ), 0)
    for bb in range(CTX_GROUP):
        slots = slice(bb * CAP_CTX, (bb + 1) * CAP_CTX)
        for e in range(N_EXPERTS):
            row = bb * N_EXPERTS + e
            hit = s_iota == slot_ref[row:row + 1, :]
            p_ref[bb, e * CAP_CTX:(e + 1) * CAP_CTX, :] = jnp.where(hit, 1.0, 0.0).astype(BF16)
            g_ref[e, slots] = jnp.where(hit, aff_ref[row:row + 1, :], 0.0).sum(axis=-1, keepdims=True)
        x = _dot(p_ref[bb], h_ref[bb * SEQ:(bb + 1) * SEQ, :]).astype(BF16)
        x_ref[:, slots, :] = x.reshape(N_EXPERTS, CAP_CTX, D_MODEL)


def _gather_ctx(h, slot, aff):
    return pl.pallas_call(
        _gather_ctx_kernel,
        grid=(BATCH // CTX_GROUP,),
        in_specs=[pl.BlockSpec((CTX_GROUP * SEQ, D_MODEL), lambda b: (b, 0)),
                  pl.BlockSpec((CTX_GROUP * N_EXPERTS, SEQ), lambda b: (b, 0)),
                  pl.BlockSpec((CTX_GROUP * N_EXPERTS, SEQ), lambda b: (b, 0))],
        out_specs=[pl.BlockSpec((N_EXPERTS, CTX_GROUP * CAP_CTX, D_MODEL), lambda b: (0, b, 0)),
                   pl.BlockSpec((N_EXPERTS, CTX_GROUP * CAP_CTX, 1), lambda b: (0, b, 0))],
        out_shape=[jax.ShapeDtypeStruct((N_EXPERTS, ROWS_CTX, D_MODEL), BF16),
                   jax.ShapeDtypeStruct((N_EXPERTS, ROWS_CTX, 1), F32)],
        scratch_shapes=[pltpu.VMEM((CTX_GROUP, N_EXPERTS * CAP_CTX, SEQ), BF16)],
        compiler_params=_params("arbitrary"),
        name="gather_ctx",
    )(h, slot, aff)


LAT_GATHER_EXPERTS = 2


def _gather_lat_kernel(h_ref, slot_ref, aff_ref, x_ref, g_ref):
    s_iota = lax.broadcasted_iota(jnp.int32, (CAP_LAT, DEC_SEQ), 0)
    for k in range(LAT_GATHER_EXPERTS):
        e = pl.program_id(1) * LAT_GATHER_EXPERTS + k
        hit = s_iota == slot_ref[pl.ds(e, 1), :]
        x_ref[k] = _dot(jnp.where(hit, 1.0, 0.0).astype(BF16), h_ref[...]).astype(BF16)
        g_ref[k] = jnp.where(hit, aff_ref[pl.ds(e, 1), :], 0.0).sum(axis=-1, keepdims=True)


def _gather_lat(h, slot, aff):
    lat_blk0 = N_CTX // DEC_SEQ
    return pl.pallas_call(
        _gather_lat_kernel,
        grid=(DEC_BATCH, N_EXPERTS // LAT_GATHER_EXPERTS),
        in_specs=[pl.BlockSpec((DEC_SEQ, D_MODEL), lambda b, e: (lat_blk0 + b, 0)),
                  pl.BlockSpec((N_EXPERTS, DEC_SEQ), lambda b, e: (b, 0)),
                  pl.BlockSpec((N_EXPERTS, DEC_SEQ), lambda b, e: (b, 0))],
        out_specs=[pl.BlockSpec((LAT_GATHER_EXPERTS, CAP_LAT, D_MODEL), lambda b, e: (e, b, 0)),
                   pl.BlockSpec((LAT_GATHER_EXPERTS, CAP_LAT, 1), lambda b, e: (e, b, 0))],
        out_shape=[jax.ShapeDtypeStruct((N_EXPERTS, ROWS_LAT, D_MODEL), BF16),
                   jax.ShapeDtypeStruct((N_EXPERTS, ROWS_LAT, 1), F32)],
        compiler_params=_params("arbitrary", "arbitrary"),
        name="gather_lat",
    )(h, slot, aff)


N_FF_CHUNKS = D_FF // TF
CHUNKS_PER_STEP = 4
FF_BLOCK = CHUNKS_PER_STEP * TF
N_FF_STEPS = -(-N_FF_CHUNKS // CHUNKS_PER_STEP)
LAST_STEP_CHUNKS = N_FF_CHUNKS - (N_FF_STEPS - 1) * CHUNKS_PER_STEP
W_PARTS = 2
PART_CHUNKS = CHUNKS_PER_STEP // W_PARTS


def _silu_tanh(x):
    return x * (0.5 + 0.5 * jnp.tanh(0.5 * x))


def _ffn_kernel(xc_ref, xl_ref, gc_ref, gl_ref, *rest):
    wg_refs, wu_refs, wd_refs = (rest[k * W_PARTS:(k + 1) * W_PARTS] for k in range(3))
    y_ref, x_sc, h_sc, acc_sc = rest[3 * W_PARTS:]
    j = pl.program_id(1)

    @pl.when(j == 0)
    def _():
        x_sc[0:ROWS_CTX, :] = xc_ref[...]
        x_sc[ROWS_CTX:, :] = xl_ref[...]

    def step(n_chunks, first, last):
        x = x_sc[...]
        for sub in range(n_chunks):
            part_idx, k = divmod(sub, PART_CHUNKS)
            cols = slice(k * TF, (k + 1) * TF)
            a = _dot(x, wg_refs[part_idx][:, cols].astype(BF16))
            u = _dot(x, wu_refs[part_idx][:, cols].astype(BF16))
            h_sc[sub] = (_silu_tanh(a) * u).astype(BF16)
        for c0 in range(0, D_MODEL, TD):
            out_cols = slice(c0, c0 + TD)
            part = None
            for sub in range(n_chunks):
                part_idx, k = divmod(sub, PART_CHUNKS)
                t = _dot(h_sc[sub], wd_refs[part_idx][k * TF:(k + 1) * TF, out_cols].astype(BF16))
                part = t if part is None else part + t
            if not first:
                part = acc_sc[:, out_cols] + part
            if last:
                y_ref[0:ROWS_CTX, out_cols] = (part[0:ROWS_CTX] * gc_ref[...]).astype(BF16)
                y_ref[ROWS_CTX:, out_cols] = (part[ROWS_CTX:] * gl_ref[...]).astype(BF16)
            else:
                acc_sc[:, out_cols] = part

    pl.when(j == 0)(functools.partial(step, CHUNKS_PER_STEP, True, False))
    pl.when((j > 0) & (j < N_FF_STEPS - 1))(functools.partial(step, CHUNKS_PER_STEP, False, False))
    pl.when(j == N_FF_STEPS - 1)(functools.partial(step, LAST_STEP_CHUNKS, False, True))


def _ffn(l, x_c, x_l, g_c, g_l, w_gate, w_up, w_down):
    rows = ROWS_CTX + ROWS_LAT
    xin = lambda r: pl.BlockSpec((None, r, D_MODEL), lambda e, j: (e, 0, 0))
    gin = lambda r: pl.BlockSpec((None, r, 1), lambda e, j: (e, 0, 0))
    parts, part = range(W_PARTS), PART_CHUNKS * TF
    col_part = lambda p, e, j: (l, e, 0, j * W_PARTS + p)
    row_part = lambda p, e, j: (l, e, j * W_PARTS + p, 0)
    return pl.pallas_call(
        _ffn_kernel,
        grid=(N_EXPERTS, N_FF_STEPS),
        in_specs=[xin(ROWS_CTX), xin(ROWS_LAT), gin(ROWS_CTX), gin(ROWS_LAT),
                  *[pl.BlockSpec((None, None, D_MODEL, part), functools.partial(col_part, p)) for p in parts],
                  *[pl.BlockSpec((None, None, D_MODEL, part), functools.partial(col_part, p)) for p in parts],
                  *[pl.BlockSpec((None, None, part, D_MODEL), functools.partial(row_part, p)) for p in parts]],
        out_specs=pl.BlockSpec((None, rows, D_MODEL), lambda e, j: (e, 0, 0)),
        out_shape=jax.ShapeDtypeStruct((N_EXPERTS, rows, D_MODEL), BF16),
        scratch_shapes=[pltpu.VMEM((rows, D_MODEL), BF16), pltpu.VMEM((CHUNKS_PER_STEP, rows, TF), BF16),
                        pltpu.VMEM((rows, D_MODEL), F32)],
        compiler_params=_params("arbitrary", "arbitrary"),
        name=f"experts{l}",
    )(x_c, x_l, g_c, g_l, *[w_gate] * W_PARTS, *[w_up] * W_PARTS, *[w_down] * W_PARTS)


def _finish(x, res, mod_ref, gf_ref, final):
    y = x + mod_ref[:, 5 * D_MODEL:] * res
    if final:
        y = y * lax.rsqrt(jnp.mean(y * y, axis=-1, keepdims=True) + RMS_EPS) * gf_ref[...]
    return y


def _combine_ctx_kernel(final, x_ref, y_ref, slot_ref, rep_ref, mod_ref, gf_ref, o_ref):
    n_col = N_EXPERTS * CAP_CTX
    col = (lax.broadcasted_iota(jnp.int32, (SEQ, n_col), 1) % CAP_CTX).astype(F32)
    for bb in range(CTX_GROUP):
        rows = slice(bb * SEQ, (bb + 1) * SEQ)
        spread = _dot(slot_ref[rows, :].astype(F32).astype(BF16), rep_ref[...])
        p = jnp.where(spread == col, 1.0, 0.0).astype(BF16)
        y = y_ref[:, bb * CAP_CTX:(bb + 1) * CAP_CTX, :].reshape(n_col, D_MODEL)
        o_ref[rows, :] = _finish(x_ref[rows, :], _dot(p, y), mod_ref, gf_ref, final)


def _combine_ctx(l, final, x_new, y, slot_t, mod4, g_final):
    n_col = N_EXPERTS * CAP_CTX
    rep = (np.arange(n_col)[None, :] // CAP_CTX == np.arange(LANES)[:, None]).astype(np.float32)
    return pl.pallas_call(
        functools.partial(_combine_ctx_kernel, final),
        grid=(BATCH // CTX_GROUP,),
        in_specs=[pl.BlockSpec((CTX_GROUP * SEQ, D_MODEL), lambda b: (b, 0)),
                  pl.BlockSpec((N_EXPERTS, CTX_GROUP * CAP_CTX, D_MODEL), lambda b: (0, b, 0)),
                  pl.BlockSpec((CTX_GROUP * SEQ, LANES), lambda b: (b, 0)),
                  pl.BlockSpec((LANES, n_col), lambda b: (0, 0)),
                  pl.BlockSpec((None, None, 1, 6 * D_MODEL), lambda b: (l, 0, 0, 0)),
                  pl.BlockSpec((1, D_MODEL), lambda b: (0, 0))],
        out_specs=pl.BlockSpec((CTX_GROUP * SEQ, D_MODEL), lambda b: (b, 0)),
        out_shape=jax.ShapeDtypeStruct((N_CTX, D_MODEL), F32),
        compiler_params=_params("arbitrary"),
        name=f"combine_ctx{l}",
    )(x_new, y, slot_t, jnp.asarray(rep, BF16), mod4, g_final.reshape(1, D_MODEL))


TMC = 512


def _combine_lat_kernel(final, x_ref, y_ref, slot_ref, mod_ref, gf_ref, o_ref):
    s_iota = lax.broadcasted_iota(jnp.int32, (TMC, CAP_LAT), 1)
    slot = slot_ref[...]
    res = None
    for e in range(N_EXPERTS):
        p = jnp.where(slot[:, e:e + 1] == s_iota, 1.0, 0.0).astype(BF16)
        t = _dot(p, y_ref[e])
        res = t if res is None else res + t
    o_ref[...] = _finish(x_ref[...], res, mod_ref, gf_ref, final)


def _combine_lat(l, final, x_new, y, slot_t, mod4, g_final):
    nt = DEC_SEQ // TMC
    return pl.pallas_call(
        functools.partial(_combine_lat_kernel, final),
        grid=(DEC_BATCH, nt),
        in_specs=[pl.BlockSpec((TMC, D_MODEL), lambda b, t: (N_CTX // TMC + b * nt + t, 0)),
                  pl.BlockSpec((N_EXPERTS, CAP_LAT, D_MODEL), lambda b, t: (0, ROWS_CTX // CAP_LAT + b, 0)),
                  pl.BlockSpec((TMC, LANES), lambda b, t: (b * nt + t, 0)),
                  pl.BlockSpec((None, None, 1, 6 * D_MODEL), lambda b, t: (l, 1 + b, 0, 0)),
                  pl.BlockSpec((1, D_MODEL), lambda b, t: (0, 0))],
        out_specs=pl.BlockSpec((TMC, D_MODEL), lambda b, t: (b * nt + t, 0)),
        out_shape=jax.ShapeDtypeStruct((N_LAT, D_MODEL), F32),
        compiler_params=_params("arbitrary", "arbitrary"),
        name=f"combine_lat{l}",
    )(x_new, y, slot_t, mod4, g_final.reshape(1, D_MODEL))


def _split_table(t):
    hi = t.astype(BF16)
    return hi, (t - hi.astype(F32)).astype(BF16)


def _dft_tables(n):
    p = np.arange(n, dtype=np.int64)
    ang = ((p[:, None] * p[None, :]) % n).astype(np.float64) * (2.0 * np.pi / n)
    return np.cos(ang).astype(np.float32), np.sin(ang).astype(np.float32)


def _channel_dft_tables():
    c = np.arange(F_WIDTH, dtype=np.int64)
    same = (c[:, None] // HEAD_DIM) == (c[None, :] // HEAD_DIM)
    ang = (((c[:, None] % HEAD_DIM) * (c[None, :] % HEAD_DIM)) % HEAD_DIM).astype(np.float64) * (2.0 * np.pi / HEAD_DIM)
    return (np.where(same, np.cos(ang), 0.0).astype(np.float32),
            np.where(same, np.sin(ang), 0.0).astype(np.float32))


def _rope_tables():
    half = HEAD_DIM // 2
    nf = half // 2
    pos = np.arange(DEC_SEQ)
    inv = 1.0 / (ROPE_BASE ** (np.arange(nf, dtype=np.float64) / nf))
    ang_r = (pos // GRID_W).astype(np.float64)[:, None] * inv
    ang_c = (pos % GRID_W).astype(np.float64)[:, None] * inv

    def head(fn, sign):
        return np.concatenate([sign * fn(ang_r), fn(ang_r), sign * fn(ang_c), fn(ang_c)], axis=-1)

    cos = head(np.cos, 1.0)
    sin = head(np.sin, -1.0)
    return (np.concatenate([cos, cos], axis=-1).astype(np.float32),
            np.concatenate([sin, sin], axis=-1).astype(np.float32))


def _na_bias_tables(rpb):
    cq = np.arange(GRID_W)
    rel_c = np.clip(cq[None, :] - cq[:, None] + NA_COLS - 1, 0, 2 * NA_COLS - 2)
    pick = (rel_c[:, :, None] == np.arange(2 * NA_COLS - 1)).astype(np.float32)
    cs = np.clip(cq - NA_COLS // 2, 0, GRID_W - NA_COLS)
    col_ok = (cq[None, :] >= cs[:, None]) & (cq[None, :] < cs[:, None] + NA_COLS)
    bc = jnp.einsum('lhrj,qkj->lhrqk', rpb, pick, precision=lax.Precision.HIGHEST)
    bc = jnp.where(col_ok[None, None, None], bc, NEG_INF)
    return pl.pallas_call(
        _na_bias_kernel,
        grid=(DEPTH, NA_VARIANTS, NA_HEADS // 2),
        in_specs=[pl.BlockSpec((None, 2, 2 * NA_ROWS - 1, GRID_W, GRID_W), lambda l, v, p: (l, p, 0, 0, 0))],
        out_specs=pl.BlockSpec((None, None, None, 2 * NA_Q, NA_WIN_ROWS * GRID_W),
                               lambda l, v, p: (l, v, p, 0, 0)),
        out_shape=jax.ShapeDtypeStruct((DEPTH, NA_VARIANTS, NA_HEADS // 2, 2 * NA_Q, NA_WIN_ROWS * GRID_W), F32),
        compiler_params=_params("arbitrary", "arbitrary", "arbitrary"),
        name="na_bias",
    )(bc)


def _na_window_plan():
    plan = []
    for g in (0, 1, NA_BLOCKS - 1):
        start = int(np.clip(NA_G * g - NA_ROWS // 2, 0, GRID_ROWS - NA_WIN_ROWS))
        rows = []
        for a in range(NA_G):
            r = NA_G * g + a
            rs = int(np.clip(r - NA_ROWS // 2, 0, GRID_ROWS - NA_ROWS))
            rows.append([start + w - r + NA_ROWS - 1 if rs <= start + w < rs + NA_ROWS else None
                         for w in range(NA_WIN_ROWS)])
        plan.append(rows)
    return plan


def _na_bias_kernel(bc_ref, o_ref):
    outside = jnp.full((GRID_W, GRID_W), NEG_INF, F32)
    for v, rows in enumerate(_na_window_plan()):
        @pl.when(pl.program_id(1) == v)
        def _():
            for half in range(2):
                for a, rel in enumerate(rows):
                    tiles = [outside if rr is None else bc_ref[half, rr] for rr in rel]
                    r0 = half * NA_Q + a * GRID_W
                    o_ref[r0:r0 + GRID_W, :] = jnp.concatenate(tiles, axis=-1)


def kernel(x_prompt, x_sample, cache_win_k, cache_win_v, cache_nat_k, cache_nat_v, c, c_ctx, w_mod, b_mod, g_mix, g_ffn, w_in, w_out, win_sink, nat_rpb, w_router, w_gate, w_up, w_down, g_final):
    x_ctx = x_prompt.reshape(N_CTX, D_MODEL)
    x_lat = x_sample.reshape(N_LAT, D_MODEL)
    cond = jnp.concatenate([c_ctx[None, :], c, jnp.zeros((N_COND - 1 - DEC_BATCH, D_MODEL), F32)], axis=0)
    mod4 = _adaln(cond, w_mod, b_mod).reshape(DEPTH, N_COND, 1, 6 * D_MODEL)

    dft_ch = _channel_dft_tables()
    dft_ctx = _dft_tables(SEQ)
    dft_lat = _dft_tables(DEC_SEQ)
    cos_t, sin_t = _rope_tables()
    cwk = cache_win_k.reshape(DEC_BATCH, DEPTH, PAST_LEN, WIN_KV)
    cwv = cache_win_v.reshape(DEC_BATCH, DEPTH, PAST_LEN, WIN_KV)
    cnk = cache_nat_k.reshape(DEC_BATCH, DEPTH, PAST_LEN, NA_W)
    cnv = cache_nat_v.reshape(DEC_BATCH, DEPTH, PAST_LEN, NA_W)
    r_pad = jnp.pad(w_router, ((0, 0), (0, 0), (0, LANES - N_EXPERTS)))
    na_bias = _na_bias_tables(nat_rpb)

    kvt = ()
    for l in range(DEPTH):
        final = l == DEPTH - 1
        (f, qw, qn), (kw, vw, kn, vn), kvt = _project(l, x_ctx, x_lat, mod4, g_mix, w_in, kvt)

        mixed_ctx = _ctx_mixer(l, win_sink, f, qw, qn, kvt, dft_ch, dft_ctx)
        mixed_f = _lat_fourier(f, dft_ch, dft_lat)
        mixed_w = _lat_window(l, win_sink, qw, kw, vw, cwk, cwv, cos_t, sin_t)
        mixed_n = _lat_neighbourhood(l, qn, kn, vn, cnk, cnv, na_bias)

        r_hi, r_lo = _split_table(r_pad[l])
        x_new, h, lg_t = _outproj(l, x_ctx, x_lat, mixed_ctx, mixed_f, mixed_w, mixed_n, mod4, g_ffn,
                                  w_out, r_hi, r_lo)

        slot_c, aff_c, slot_ct, slot_l, aff_l, slot_lt = _route(lg_t)
        xg_c, gate_c = _gather_ctx(h, slot_c, aff_c)
        xg_l, gate_l = _gather_lat(h, slot_l, aff_l)
        y = _ffn(l, xg_c, xg_l, gate_c, gate_l, w_gate, w_up, w_down)
        x_ctx = _combine_ctx(l, final, x_new, y, slot_ct, mod4, g_final)
        x_lat = _combine_lat(l, final, x_new, y, slot_lt, mod4, g_final)

    y_prompt = x_ctx.reshape(BATCH, SEQ, D_MODEL)
    y_sample = x_lat.reshape(DEC_BATCH, DEC_SEQ, D_MODEL)
    new_kv = [t.reshape(BATCH, DEPTH, w // HEAD_DIM, HEAD_DIM, SEQ).transpose(0, 1, 4, 2, 3)
              for t, w in zip(kvt, KV_WIDTHS)]
    return (y_prompt, y_sample, *new_kv)
```

```python
import functools

import numpy as np
import jax
import jax.numpy as jnp
from jax import lax
from jax.experimental import pallas as pl
from jax.experimental.pallas import tpu as pltpu

D_MODEL = 1024
BATCH = 16
SEQ = 256
DEPTH = 2
DEC_BATCH = 2
DEC_SEQ = 2048
PAST_LEN = 256
GRID_W = 64
HEAD_DIM = 64
F_WIDTH = 256
WIN_HEADS = 6
WIN_KV_HEADS = 2
WINDOW = 128
WIN_BLOCK = 128
NA_HEADS = 6
NA_ROWS = 8
NA_COLS = 16
N_EXPERTS = 16
EC_CAPACITY = 2
D_FF = 2816
ROPE_BASE = 10000.0
RMS_EPS = 1e-6
NEG_INF = -1e30
ATTN_SCALE = HEAD_DIM ** -0.5
WIN_Q = WIN_HEADS * HEAD_DIM
WIN_KV = WIN_KV_HEADS * HEAD_DIM
NA_W = NA_HEADS * HEAD_DIM
N_IN = F_WIDTH + WIN_Q + 2 * WIN_KV + 3 * NA_W
SPLITS = (0, F_WIDTH, F_WIDTH + WIN_Q, F_WIDTH + WIN_Q + WIN_KV, F_WIDTH + WIN_Q + 2 * WIN_KV,
          F_WIDTH + WIN_Q + 2 * WIN_KV + NA_W, F_WIDTH + WIN_Q + 2 * WIN_KV + 2 * NA_W, N_IN)

N_CTX = BATCH * SEQ
N_LAT = DEC_BATCH * DEC_SEQ
N_TOK = N_CTX + N_LAT
GRID_ROWS = DEC_SEQ // GRID_W
CAP_CTX = EC_CAPACITY * SEQ // N_EXPERTS
CAP_LAT = EC_CAPACITY * DEC_SEQ // N_EXPERTS
ROWS_CTX = BATCH * CAP_CTX
ROWS_LAT = DEC_BATCH * CAP_LAT
N_COND = 8

LANES = 128
MXU_COLS = 256
TM = 512
TN_MOD = 1536
TF = 256
TD = 256
VMEM_LIMIT = 56 * 1024 * 1024

F32 = jnp.float32
BF16 = jnp.bfloat16


def _params(*sem):
    return pltpu.CompilerParams(dimension_semantics=sem, vmem_limit_bytes=VMEM_LIMIT)


def _dot(a, b):
    return jnp.dot(a, b, preferred_element_type=F32)


def _dot_nt(a, b):
    return lax.dot_general(a, b, (((1,), (1,)), ((), ())), preferred_element_type=F32)


def _split(x):
    hi = x.astype(BF16)
    lo = (x - hi.astype(F32)).astype(BF16)
    return hi, lo


def _dot3(a_hi, a_lo, b_hi, b_lo):
    return _dot(a_hi, b_hi) + (_dot(a_lo, b_hi) + _dot(a_hi, b_lo))


def _silu(x):
    return x / (1.0 + jnp.exp(-x))


def _rms_mod(x, g, shift, scale):
    y = x * lax.rsqrt(jnp.mean(x * x, axis=-1, keepdims=True) + RMS_EPS)
    return (y * g) * (1.0 + scale) + shift


def _softmax_parts(parts, sink=None):
    m = parts[0].max(axis=-1, keepdims=True)
    for s in parts[1:]:
        m = jnp.maximum(m, s.max(axis=-1, keepdims=True))
    if sink is not None:
        m = jnp.maximum(m, sink)
    es = [jnp.exp(s - m) for s in parts]
    den = es[0].sum(axis=-1, keepdims=True)
    for e in es[1:]:
        den = den + e.sum(axis=-1, keepdims=True)
    if sink is not None:
        den = den + jnp.exp(sink - m)
    inv = 1.0 / den
    return [e * inv for e in es]


def _cond_of_block(i):
    n_ctx_blocks = N_CTX // TM
    return jnp.where(i < n_ctx_blocks, 0, 1 + (i - n_ctx_blocks) // (DEC_SEQ // TM))


def _two_stream_specs(width):
    n_ctx_blocks = N_CTX // TM
    ctx = pl.BlockSpec((TM, width), lambda i: (jnp.minimum(i, n_ctx_blocks - 1), 0))
    lat = pl.BlockSpec((TM, width), lambda i: (jnp.maximum(i - n_ctx_blocks, 0), 0))
    return ctx, lat


def _pick_stream(ctx_ref, lat_ref):
    return jnp.where(pl.program_id(0) < N_CTX // TM, ctx_ref[...], lat_ref[...])


def _adaln_kernel(c_ref, w_ref, b_ref, o_ref):
    s_hi, s_lo = _split(_silu(c_ref[...]))
    w_hi, w_lo = _split(w_ref[...])
    o_ref[...] = _dot3(s_hi, s_lo, w_hi, w_lo) + b_ref[...]


def _adaln(cond, w_mod, b_mod):
    return pl.pallas_call(
        _adaln_kernel,
        grid=(DEPTH, 6 * D_MODEL // TN_MOD),
        in_specs=[
            pl.BlockSpec((N_COND, D_MODEL), lambda l, j: (0, 0)),
            pl.BlockSpec((None, D_MODEL, TN_MOD), lambda l, j: (l, 0, j)),
            pl.BlockSpec((None, 1, TN_MOD), lambda l, j: (l, 0, j)),
        ],
        out_specs=pl.BlockSpec((None, N_COND, TN_MOD), lambda l, j: (l, 0, j)),
        out_shape=jax.ShapeDtypeStruct((DEPTH, N_COND, 6 * D_MODEL), F32),
        compiler_params=_params("arbitrary", "arbitrary"),
        name="adaln",
    )(cond, w_mod, b_mod.reshape(DEPTH, 1, 6 * D_MODEL))


KV_NAMES = ("kw", "vw", "kn", "vn")
KV_WIDTHS = (WIN_KV, WIN_KV, NA_W, NA_W)
KV_SPLITS = (SPLITS[2], SPLITS[3], SPLITS[5], SPLITS[6])
KV_TOTAL = sum(KV_WIDTHS)
Q_SPLITS = (SPLITS[0], SPLITS[1], SPLITS[4])


def _proj_kernel(n_prev, xc_ref, xl_ref, mod_ref, g_ref, w_ref, *rest):
    rest = rest[n_prev:]
    f_ref, qw_ref, qn_ref = rest[0:3]
    lat_kv = rest[3:7]
    ctx_kvt = rest[7:11]
    wb_ref, wt_ref = rest[11:13]
    i = pl.program_id(0)

    @pl.when(i == 0)
    def _():
        wb_ref[...] = w_ref[...].astype(BF16)
        r0 = 0
        for c0, width in zip(KV_SPLITS, KV_WIDTHS):
            wt_ref[r0:r0 + width, :] = w_ref[:, c0:c0 + width].T.astype(BF16)
            r0 += width

    natural = dict(zip(SPLITS[:-1], (f_ref, qw_ref) + tuple(lat_kv[0:2]) + (qn_ref,) + tuple(lat_kv[2:4])))

    def hidden(x_ref):
        return _rms_mod(x_ref[...], g_ref[...], mod_ref[:, 0:D_MODEL], mod_ref[:, D_MODEL:2 * D_MODEL]).astype(BF16)

    def project_tiles(h, wanted):
        for t0 in range(0, N_IN, MXU_COLS):
            hits = [(a, b) for a, b in zip(SPLITS[:-1], SPLITS[1:])
                    if a in wanted and max(a, t0) < min(b, t0 + MXU_COLS)]
            if not hits:
                continue
            acc = _dot(h, wb_ref[:, t0:t0 + MXU_COLS])
            for a, b in hits:
                lo, hi = max(a, t0), min(b, t0 + MXU_COLS)
                natural[a][:, lo - a:hi - a] = acc[:, lo - t0:hi - t0].astype(natural[a].dtype)

    @pl.when(i < N_CTX // TM)
    def _():
        h = hidden(xc_ref)
        project_tiles(h, Q_SPLITS)
        kvt = _dot_nt(wt_ref[...], h)
        r0 = 0
        for o_ref, width in zip(ctx_kvt, KV_WIDTHS):
            for bb in range(TM // SEQ):
                o_ref[bb] = kvt[r0:r0 + width, bb * SEQ:(bb + 1) * SEQ]
            r0 += width

    @pl.when(i >= N_CTX // TM)
    def _():
        project_tiles(hidden(xl_ref), Q_SPLITS + KV_SPLITS)


def _project(l, x_ctx, x_lat, mod4, g_mix, w_in, prev_kvt):
    n_ctx_blocks = N_CTX // TM
    xc_spec, xl_spec = _two_stream_specs(D_MODEL)
    both = lambda w: pl.BlockSpec((TM, w), lambda i: (i, 0))
    lat = lambda w: pl.BlockSpec((TM, w), lambda i: (jnp.maximum(i - n_ctx_blocks, 0), 0))
    ctx_t = lambda w: pl.BlockSpec((TM // SEQ, None, w, SEQ),
                                   lambda i: (jnp.minimum(i, n_ctx_blocks - 1), l, 0, 0))
    n_prev = len(prev_kvt)
    n_in = 5
    outs = pl.pallas_call(
        functools.partial(_proj_kernel, n_prev),
        grid=(N_TOK // TM,),
        in_specs=[
            xc_spec, xl_spec,
            pl.BlockSpec((None, None, 1, 6 * D_MODEL), lambda i: (l, _cond_of_block(i), 0, 0)),
            pl.BlockSpec((None, 1, D_MODEL), lambda i: (l, 0, 0)),
            pl.BlockSpec((None, D_MODEL, N_IN), lambda i: (l, 0, 0)),
        ] + [pl.BlockSpec(memory_space=pl.ANY)] * n_prev,
        out_specs=[both(F_WIDTH), both(WIN_Q), both(NA_W)] + [lat(w) for w in KV_WIDTHS]
                  + [ctx_t(w) for w in KV_WIDTHS],
        out_shape=[jax.ShapeDtypeStruct((N_TOK, w), dt) for w, dt in ((F_WIDTH, F32), (WIN_Q, F32), (NA_W, BF16))]
                  + [jax.ShapeDtypeStruct((N_LAT, w), dt)
                     for w, dt in zip(KV_WIDTHS, (F32, BF16, BF16, BF16))]
                  + [jax.ShapeDtypeStruct((BATCH, DEPTH, w, SEQ), F32) for w in KV_WIDTHS],
        input_output_aliases={n_in + k: 7 + k for k in range(n_prev)},
        scratch_shapes=[pltpu.VMEM((D_MODEL, N_IN), BF16), pltpu.VMEM((KV_TOTAL, D_MODEL), BF16)],
        compiler_params=_params("arbitrary"),
        name=f"project{l}",
    )(x_ctx, x_lat, mod4, g_mix.reshape(DEPTH, 1, D_MODEL), w_in, *prev_kvt)
    return outs[0:3], outs[3:7], outs[7:11]


def _lane_is_low(shape):
    return lax.broadcasted_iota(jnp.int32, shape, len(shape) - 1) < HEAD_DIM


def _swap_halves(x):
    return pltpu.roll(x, HEAD_DIM, axis=x.ndim - 1)


def _win_kv_copy(h):
    return 0 if (h // (WIN_HEADS // WIN_KV_HEADS)) == (h % 2) else 1


def _stack_heads(q_pairs, heads):
    low = _lane_is_low(q_pairs[heads[0] // 2].shape)
    rows = [jnp.where(low if h % 2 == 0 else jnp.logical_not(low), q_pairs[h // 2], 0.0).astype(BF16)
            for h in heads]
    return rows[0] if len(rows) == 1 else jnp.concatenate(rows, axis=0)


def _per_head_column(values, rows_per_head):
    blk = lax.broadcasted_iota(jnp.int32, (len(values) * rows_per_head, 1), 0) // rows_per_head
    col = jnp.full(blk.shape, values[0], F32)
    for i in range(1, len(values)):
        col = jnp.where(blk == i, values[i], col)
    return col


def _attend(q_stack, kv_list, extra_logit=None, transposed=False):
    scores = []
    for k, _, post in kv_list:
        s = _dot(q_stack, k) if transposed else _dot_nt(q_stack, k)
        scores.append(post(s) if post is not None else s)
    probs = _softmax_parts(scores, extra_logit)
    o = None
    for p, (_, v, _) in zip(probs, kv_list):
        t = _dot_nt(p.astype(BF16), v) if transposed else _dot(p.astype(BF16), v)
        o = t if o is None else o + t
    return o


def _merge_pair(o_even, o_odd):
    return jnp.where(_lane_is_low(o_even.shape), o_even, o_odd)


def _gqa_attention(q_pairs, rows, kv_for_copy, sinks, transposed=False):
    per_head = {}
    for copy in (0, 1):
        heads = [h for h in range(WIN_HEADS) if _win_kv_copy(h) == copy]
        o = _attend(_stack_heads(q_pairs, heads), kv_for_copy(copy),
                    _per_head_column([sinks[h] for h in heads], rows), transposed)
        for i, h in enumerate(heads):
            per_head[h] = o[i * rows:(i + 1) * rows]
    return [_merge_pair(per_head[2 * j], per_head[2 * j + 1]) for j in range(WIN_HEADS // 2)]


def _swap_row_halves(x):
    return jnp.concatenate([x[HEAD_DIM:], x[:HEAD_DIM]], axis=0)


MIX_GROUP = 2


def _ctx_mixer_kernel(l, sink_ref, f_ref, qw_ref, qn_ref, kw_ref, vw_ref, kn_ref, vn_ref,
                      bc_ref, bs_ref, cs_ref, ss_ref, o_ref):
    bc_hi, bc_lo = _split(bc_ref[...])
    bs_hi, bs_lo = _split(bs_ref[...])
    cs_hi, cs_lo = _split(cs_ref[...])
    ss_hi, ss_lo = _split(ss_ref[...])
    for bb in range(MIX_GROUP):
        rows = slice(bb * SEQ, (bb + 1) * SEQ)
        f_hi, f_lo = _split(f_ref[rows, :])
        fc_hi, fc_lo = _split(_dot3(f_hi, f_lo, bc_hi, bc_lo))
        fs_hi, fs_lo = _split(_dot3(f_hi, f_lo, bs_hi, bs_lo))
        z = _dot3(cs_hi, cs_lo, fc_hi, fc_lo) - _dot3(ss_hi, ss_lo, fs_hi, fs_lo)
        o_ref[rows, 0:F_WIDTH] = (z * (SEQ * HEAD_DIM) ** -0.5).astype(BF16)

        kv = [(kw_ref[bb].astype(BF16), vw_ref[bb].astype(BF16)),
              (_swap_row_halves(kw_ref[bb]).astype(BF16), _swap_row_halves(vw_ref[bb]).astype(BF16))]
        q_pairs = [qw_ref[rows, LANES * j:LANES * (j + 1)] * ATTN_SCALE for j in range(WIN_HEADS // 2)]
        outs = _gqa_attention(q_pairs, SEQ, lambda c: [(kv[c][0], kv[c][1], None)],
                              [sink_ref[l, h] for h in range(WIN_HEADS)], transposed=True)
        for j, o in enumerate(outs):
            o_ref[rows, F_WIDTH + LANES * j:F_WIDTH + LANES * (j + 1)] = o.astype(BF16)

        for j in range(NA_HEADS // 2):
            sl = slice(LANES * j, LANES * (j + 1))
            q_pairs = {j: qn_ref[rows, sl] * ATTN_SCALE}
            o = _attend(_stack_heads(q_pairs, (2 * j, 2 * j + 1)),
                        [(kn_ref[bb, sl, :].astype(BF16), vn_ref[bb, sl, :].astype(BF16), None)], transposed=True)
            base = F_WIDTH + WIN_Q + LANES * j
            o_ref[rows, base:base + LANES] = _merge_pair(o[:SEQ], o[SEQ:]).astype(BF16)


def _ctx_mixer(l, win_sink, f, qw, qn, kvt, dft_ch, dft_seq):
    row = lambda w: pl.BlockSpec((MIX_GROUP * SEQ, w), lambda b: (b, 0))
    col = lambda w: pl.BlockSpec((MIX_GROUP, None, w, SEQ), lambda b: (b, l, 0, 0))
    const = lambda n: pl.BlockSpec((n, n), lambda b: (0, 0))
    return pl.pallas_call(
        functools.partial(_ctx_mixer_kernel, l),
        grid=(BATCH // MIX_GROUP,),
        in_specs=[pl.BlockSpec(memory_space=pltpu.SMEM),
                  row(F_WIDTH), row(WIN_Q), row(NA_W)] + [col(w) for w in KV_WIDTHS]
                 + [const(F_WIDTH)] * 2 + [const(SEQ)] * 2,
        out_specs=pl.BlockSpec((MIX_GROUP * SEQ, D_MODEL), lambda b: (b, 0)),
        out_shape=jax.ShapeDtypeStruct((N_CTX, D_MODEL), BF16),
        compiler_params=_params("arbitrary"),
        name=f"ctx_mixer{l}",
    )(win_sink, f, qw, qn, *kvt, *dft_ch, *dft_seq)


FT_ROWS = 512


HALF_SEQ = DEC_SEQ // 2
SUBLANES = 8


def _lat_fourier_kernel(f_ref, g_ref, bc_ref, bs_ref, cs_ref, ss_ref, o_ref, st_ref, mid_ref):
    r, b = pl.program_id(0), pl.program_id(1)

    @pl.when(r == 0)
    def _():
        bc = _split(bc_ref[...])
        bs = _split(bs_ref[...])
        base = pl.multiple_of(b * DEC_SEQ, DEC_SEQ)
        f_hi, f_lo = _split(f_ref[pl.ds(base, HALF_SEQ), :])
        g_hi, g_lo = _split(g_ref[pl.ds(base, HALF_SEQ), :])
        fc = _dot3(f_hi, f_lo, *bc) + _dot3(g_hi, g_lo, *bc)
        first = lax.broadcasted_iota(jnp.int32, (HALF_SEQ, 1), 0) == 0
        fc = jnp.where(first, 0.5 * fc, fc)
        fs = _dot3(f_hi, f_lo, *bs) - _dot3(g_hi, g_lo, *bs)
        st_ref[b, 0], st_ref[b, 1] = _split(fc)
        st_ref[b, 2], st_ref[b, 3] = _split(fs)
        m_hi, m_lo = _split(f_ref[pl.ds(base + HALF_SEQ, SUBLANES), :])
        mid_ref[b] = _dot3(m_hi, m_lo, *bc)

    z = (_dot3(*_split(cs_ref[...]), st_ref[b, 0], st_ref[b, 1])
         - _dot3(*_split(ss_ref[...]), st_ref[b, 2], st_ref[b, 3]))
    k = r * FT_ROWS + lax.broadcasted_iota(jnp.int32, (FT_ROWS, 1), 0)
    z = z + jnp.where(k % 2 == 0, 1.0, -1.0) * mid_ref[b, 0:1, :]
    o_ref[...] = (z * (DEC_SEQ * HEAD_DIM) ** -0.5).astype(BF16)


def _lat_fourier(f, dft_ch, dft_seq):
    nrb = DEC_SEQ // FT_ROWS
    f_lat = f[N_CTX:].reshape(DEC_BATCH, DEC_SEQ, F_WIDTH)
    mirrored = jnp.roll(jnp.flip(f_lat, axis=1), 1, axis=1).reshape(N_LAT, F_WIDTH)
    const = pl.BlockSpec((F_WIDTH, F_WIDTH), lambda r, b: (0, 0))
    rows = pl.BlockSpec((FT_ROWS, HALF_SEQ), lambda r, b: (r, 0))
    return pl.pallas_call(
        _lat_fourier_kernel,
        grid=(nrb, DEC_BATCH),
        in_specs=[pl.BlockSpec((N_LAT, F_WIDTH), lambda r, b: (1, 0)),
                  pl.BlockSpec((N_LAT, F_WIDTH), lambda r, b: (0, 0))] + [const] * 2 + [rows] * 2,
        out_specs=pl.BlockSpec((FT_ROWS, F_WIDTH), lambda r, b: (b * nrb + r, 0)),
        out_shape=jax.ShapeDtypeStruct((N_LAT, F_WIDTH), BF16),
        scratch_shapes=[pltpu.VMEM((DEC_BATCH, 4, HALF_SEQ, F_WIDTH), BF16),
                        pltpu.VMEM((DEC_BATCH, SUBLANES, F_WIDTH), F32)],
        compiler_params=_params("arbitrary", "arbitrary"),
        name="lat_fourier",
    )(f, mirrored, *dft_ch, *dft_seq)


def _rope(x, cos, sin_signed):
    n = x.shape[-1]
    lane = lax.broadcasted_iota(jnp.int32, x.shape, x.ndim - 1)
    first = (lane % 32) < 16
    partner = jnp.where(first, pltpu.roll(x, n - 16, axis=x.ndim - 1), pltpu.roll(x, 16, axis=x.ndim - 1))
    return x * cos + partner * sin_signed


def _win_kernel(l, sink_ref, q_ref, k_ref, v_ref, ck_ref, cv_ref, cos_ref, sin_ref, cosq_ref, sinq_ref,
                o_ref, kp_ref, vp_ref, cp_ref):
    n = pl.program_id(1)
    nb = DEC_SEQ // WIN_BLOCK
    pad = WIN_BLOCK

    @pl.when(n == 0)
    def _():
        zeros = jnp.zeros((pad, LANES), BF16)
        kr = _rope(k_ref[...], cos_ref[...], sin_ref[...])
        v = v_ref[...]
        for idx, (kk, vv) in enumerate(((kr, v), (_swap_halves(kr), _swap_halves(v)))):
            kp_ref[idx, 0:pad] = zeros
            kp_ref[idx, pad + DEC_SEQ:] = zeros
            kp_ref[idx, pad:pad + DEC_SEQ] = kk.astype(BF16)
            vp_ref[idx, 0:pad] = zeros
            vp_ref[idx, pad + DEC_SEQ:] = zeros
            vp_ref[idx, pad:pad + DEC_SEQ] = vv.astype(BF16)
        ck = ck_ref[...]
        cv = cv_ref[...]
        cp_ref[0] = ck.astype(BF16)
        cp_ref[1] = _swap_halves(ck).astype(BF16)
        cp_ref[2] = cv.astype(BF16)
        cp_ref[3] = _swap_halves(cv).astype(BF16)

    lo = jnp.where(n == 0, WIN_BLOCK, 0)
    hi = jnp.where(n == nb - 1, 2 * WIN_BLOCK, 3 * WIN_BLOCK)

    i = lax.broadcasted_iota(jnp.int32, (WIN_BLOCK, 3 * WIN_BLOCK), 0)
    j = lax.broadcasted_iota(jnp.int32, (WIN_BLOCK, 3 * WIN_BLOCK), 1)
    mask = (j >= i + WIN_BLOCK - WINDOW) & (j <= i + WIN_BLOCK + WINDOW) & (j >= lo) & (j < hi)
    band_bias = jnp.where(mask, 0.0, NEG_INF)

    def band(s):
        heads = s.shape[0] // WIN_BLOCK
        return (s.reshape(heads, WIN_BLOCK, s.shape[1]) + band_bias[None]).reshape(s.shape)

    start = pl.multiple_of(n * WIN_BLOCK, WIN_BLOCK)
    win = pl.ds(start, 3 * WIN_BLOCK)
    q_pairs = [_rope(q_ref[:, LANES * jp:LANES * (jp + 1)], cosq_ref[...], sinq_ref[...]) * ATTN_SCALE
               for jp in range(WIN_HEADS // 2)]
    outs = _gqa_attention(
        q_pairs, WIN_BLOCK,
        lambda c: [(kp_ref[c, win, :], vp_ref[c, win, :], band), (cp_ref[c], cp_ref[2 + c], None)],
        [sink_ref[l, h] for h in range(WIN_HEADS)])
    for jp, o in enumerate(outs):
        o_ref[:, LANES * jp:LANES * (jp + 1)] = o.astype(BF16)


def _lat_window(l, win_sink, qw, kw, vw, cache_k, cache_v, cos_t, sin_t):
    nb = DEC_SEQ // WIN_BLOCK
    kv_spec = pl.BlockSpec((DEC_SEQ, WIN_KV), lambda b, n: (b, 0))
    cache_spec = pl.BlockSpec((None, None, PAST_LEN, WIN_KV), lambda b, n: (b, l, 0, 0))
    tab_all = pl.BlockSpec((DEC_SEQ, LANES), lambda b, n: (0, 0))
    tab_blk = pl.BlockSpec((WIN_BLOCK, LANES), lambda b, n: (n, 0))
    return pl.pallas_call(
        functools.partial(_win_kernel, l),
        grid=(DEC_BATCH, nb),
        in_specs=[pl.BlockSpec(memory_space=pltpu.SMEM),
                  pl.BlockSpec((WIN_BLOCK, WIN_Q), lambda b, n: (N_CTX // WIN_BLOCK + b * nb + n, 0)),
                  kv_spec, kv_spec, cache_spec, cache_spec, tab_all, tab_all, tab_blk, tab_blk],
        out_specs=pl.BlockSpec((WIN_BLOCK, WIN_Q), lambda b, n: (b * nb + n, 0)),
        out_shape=jax.ShapeDtypeStruct((N_LAT, WIN_Q), BF16),
        scratch_shapes=[pltpu.VMEM((2, DEC_SEQ + 2 * WIN_BLOCK, LANES), BF16),
                        pltpu.VMEM((2, DEC_SEQ + 2 * WIN_BLOCK, LANES), BF16),
                        pltpu.VMEM((4, PAST_LEN, LANES), BF16)],
        compiler_params=_params("arbitrary", "arbitrary"),
        name=f"lat_window{l}",
    )(win_sink, qw, kw, vw, cache_k, cache_v, cos_t, sin_t, cos_t, sin_t)


NA_G = 4
NA_Q = NA_G * GRID_W
NA_WIN_ROWS = NA_ROWS + NA_G
NA_BLOCKS = GRID_ROWS // NA_G
NA_VARIANTS = 3


def _na_block_start(g):
    return jnp.clip(NA_G * g - NA_ROWS // 2, 0, GRID_ROWS - NA_WIN_ROWS)


def _na_kernel(q_ref, k_ref, v_ref, ck_ref, cv_ref, bias_ref, o_ref):
    g = pl.program_id(1)
    start = pl.multiple_of(_na_block_start(g) * GRID_W, GRID_W)
    win = pl.ds(start, NA_WIN_ROWS * GRID_W)
    for jp in range(NA_HEADS // 2):
        sl = slice(LANES * jp, LANES * (jp + 1))
        q_stack = _stack_heads({jp: q_ref[:, sl] * ATTN_SCALE}, (2 * jp, 2 * jp + 1))
        bias = bias_ref[jp]
        o = _attend(q_stack, [(k_ref[win, sl].astype(BF16), v_ref[win, sl].astype(BF16), lambda s: s + bias),
                              (ck_ref[:, sl].astype(BF16), cv_ref[:, sl].astype(BF16), None)])
        o_ref[:, sl] = _merge_pair(o[:NA_Q], o[NA_Q:]).astype(BF16)


def _lat_neighbourhood(l, qn, kn, vn, cache_k, cache_v, bias_tab):
    kv_spec = pl.BlockSpec((DEC_SEQ, NA_W), lambda b, g: (b, 0))
    cache_spec = pl.BlockSpec((None, None, PAST_LEN, NA_W), lambda b, g: (b, l, 0, 0))
    variant = lambda g: jnp.where(g == 0, 0, jnp.where(g == NA_BLOCKS - 1, 2, 1))
    return pl.pallas_call(
        _na_kernel,
        grid=(DEC_BATCH, NA_BLOCKS),
        in_specs=[pl.BlockSpec((NA_Q, NA_W), lambda b, g: (N_CTX // NA_Q + b * NA_BLOCKS + g, 0)),
                  kv_spec, kv_spec, cache_spec, cache_spec,
                  pl.BlockSpec((None, None, NA_HEADS // 2, 2 * NA_Q, NA_WIN_ROWS * GRID_W),
                               lambda b, g: (l, variant(g), 0, 0, 0))],
        out_specs=pl.BlockSpec((NA_Q, NA_W), lambda b, g: (b * NA_BLOCKS + g, 0)),
        out_shape=jax.ShapeDtypeStruct((N_LAT, NA_W), BF16),
        compiler_params=_params("arbitrary", "arbitrary"),
        name=f"lat_neighbourhood{l}",
    )(qn, kn, vn, cache_k, cache_v, bias_tab)


OUTPROJ_ROWS = 256


def _outproj_kernel(xc_ref, xl_ref, mc_ref, mf_ref, mw_ref, mn_ref, mod_ref, g_ref, w_ref, r_hi, r_lo,
                    x_ref, h_ref, lg_ref, wb_ref):
    i = pl.program_id(0)

    @pl.when(i == 0)
    def _():
        wb_ref[...] = w_ref[...].astype(BF16)

    d = D_MODEL

    def block(x_in_ref, mixed_rows):
        for r0 in range(0, TM, OUTPROJ_ROWS):
            rows = slice(r0, r0 + OUTPROJ_ROWS)
            x = x_in_ref[rows, :] + mod_ref[:, 2 * d:3 * d] * _dot(mixed_rows(rows), wb_ref[...])
            x_ref[rows, :] = x
            h = _rms_mod(x, g_ref[...], mod_ref[:, 3 * d:4 * d], mod_ref[:, 4 * d:5 * d])
            h_ref[rows, :] = h.astype(BF16)
            h_hi, h_lo = _split(h)
            lg_ref[:, rows] = _dot3(h_hi, h_lo, r_hi[...], r_lo[...]).T[0:N_EXPERTS, :]

    @pl.when(i < N_CTX // TM)
    def _():
        block(xc_ref, lambda rows: mc_ref[rows, :])

    @pl.when(i >= N_CTX // TM)
    def _():
        block(xl_ref, lambda rows: jnp.concatenate([mf_ref[rows, :], mw_ref[rows, :], mn_ref[rows, :]], axis=1))


def _outproj(l, x_ctx, x_lat, mixed_ctx, mixed_f, mixed_w, mixed_n, mod4, g_ffn, w_out, r_hi, r_lo):
    n_ctx_blocks = N_CTX // TM
    xc_spec, xl_spec = _two_stream_specs(D_MODEL)
    lat = lambda w: pl.BlockSpec((TM, w), lambda i: (jnp.maximum(i - n_ctx_blocks, 0), 0))
    whole = lambda shape: pl.BlockSpec(shape, lambda i: (0,) * len(shape))
    row = lambda w: pl.BlockSpec((TM, w), lambda i: (i, 0))
    return pl.pallas_call(
        _outproj_kernel,
        grid=(N_TOK // TM,),
        in_specs=[xc_spec, xl_spec,
                  pl.BlockSpec((TM, D_MODEL), lambda i: (jnp.minimum(i, n_ctx_blocks - 1), 0)),
                  lat(F_WIDTH), lat(WIN_Q), lat(NA_W),
                  pl.BlockSpec((None, None, 1, 6 * D_MODEL), lambda i: (l, _cond_of_block(i), 0, 0)),
                  pl.BlockSpec((None, 1, D_MODEL), lambda i: (l, 0, 0)),
                  pl.BlockSpec((None, D_MODEL, D_MODEL), lambda i: (l, 0, 0)),
                  whole((D_MODEL, LANES)), whole((D_MODEL, LANES))],
        out_specs=[row(D_MODEL), row(D_MODEL), pl.BlockSpec((N_EXPERTS, TM), lambda i: (0, i))],
        out_shape=[jax.ShapeDtypeStruct((N_TOK, D_MODEL), F32),
                   jax.ShapeDtypeStruct((N_TOK, D_MODEL), BF16),
                   jax.ShapeDtypeStruct((N_EXPERTS, N_TOK), F32)],
        scratch_shapes=[pltpu.VMEM((D_MODEL, D_MODEL), BF16)],
        compiler_params=_params("arbitrary"),
        name=f"outproj{l}",
    )(x_ctx, x_lat, mixed_ctx, mixed_f, mixed_w, mixed_n, mod4, g_ffn.reshape(DEPTH, 1, D_MODEL), w_out,
      r_hi, r_lo)


PREFIX_CHUNK = 256
MANTISSA_STEPS = 44


def _prefix_exclusive(m):
    rows, n = m.shape
    t0 = lax.broadcasted_iota(jnp.int32, (PREFIX_CHUNK, PREFIX_CHUNK), 0)
    t1 = lax.broadcasted_iota(jnp.int32, (PREFIX_CHUNK, PREFIX_CHUNK), 1)
    upper = jnp.where(t0 < t1, 1.0, 0.0).astype(BF16)
    carry = jnp.zeros((rows, 1), F32)
    outs = []
    for c in range(n // PREFIX_CHUNK):
        blk = m[:, c * PREFIX_CHUNK:(c + 1) * PREFIX_CHUNK]
        outs.append(_dot(blk.astype(BF16), upper) + carry)
        carry = carry + blk.sum(axis=-1, keepdims=True)
    return outs[0] if len(outs) == 1 else jnp.concatenate(outs, axis=-1)


ROUTE_STREAMS = ((0, BATCH, CAP_CTX), (N_CTX, DEC_BATCH, CAP_LAT))


def _route_kernel(lg_ref, *out_refs):
    affs, caps = [], []
    for t0, groups, cap in ROUTE_STREAMS:
        n = (N_CTX if t0 == 0 else N_LAT) // groups
        x = jnp.concatenate([lg_ref[:, t0 + g * n:t0 + (g + 1) * n] for g in range(groups)], axis=0)
        x = x.reshape(groups, N_EXPERTS, n)
        e = jnp.exp(x - x.max(axis=1, keepdims=True))
        affs.append((e / e.sum(axis=1, keepdims=True)).reshape(groups * N_EXPERTS, n))
        caps.append(float(cap))

    def count_ge(aff, t):
        return jnp.where(aff >= t, 1.0, 0.0).sum(axis=-1, keepdims=True)

    def keep_if_enough(aff, cap, cand, otherwise):
        return jnp.where(count_ge(aff, cand) >= cap, cand, otherwise)

    above = [jnp.full((aff.shape[0], 1), 2.0, F32) for aff in affs]
    for s in (64, 32, 16, 8, 4, 2, 1):
        cands = [a * (2.0 ** -s) for a in above]
        above = [jnp.where(count_ge(aff, c) >= cap, a, c) for aff, cap, a, c in zip(affs, caps, above, cands)]
    bases = [keep_if_enough(aff, cap, a * 0.5, 0.0) for aff, cap, a in zip(affs, caps, above)]

    def refine(_, carry):
        thrs, incs = carry
        thrs = tuple(keep_if_enough(aff, cap, t + i, t) for aff, cap, t, i in zip(affs, caps, thrs, incs))
        return thrs, tuple(i * 0.5 for i in incs)

    thrs, _ = lax.fori_loop(0, MANTISSA_STEPS, refine, (tuple(bases), tuple(b * 0.5 for b in bases)))

    for k, ((_, groups, _), aff, cap, thr) in enumerate(zip(ROUTE_STREAMS, affs, caps, thrs)):
        slot_ref, aff_ref, slott_ref = out_refs[3 * k:3 * k + 3]
        n = aff.shape[1]
        gt = jnp.where(aff > thr, 1.0, 0.0)
        eq = jnp.where(aff == thr, 1.0, 0.0)
        need = cap - gt.sum(axis=-1, keepdims=True)
        sel = gt + eq * jnp.where(_prefix_exclusive(eq) < need, 1.0, 0.0)
        slot = jnp.where(sel > 0.0, _prefix_exclusive(sel), -1.0)
        slot_ref[...] = slot.astype(jnp.int32)
        aff_ref[...] = aff
        unused = jnp.full((LANES - N_EXPERTS, n), -1.0, F32)
        for g in range(groups):
            tile = jnp.concatenate([slot[g * N_EXPERTS:(g + 1) * N_EXPERTS], unused], axis=0)
            slott_ref[g * n:(g + 1) * n, :] = tile.T.astype(jnp.int32)


def _route(lg_t):
    shapes = []
    for t0, groups, _ in ROUTE_STREAMS:
        n_tok = N_CTX if t0 == 0 else N_LAT
        rows, n = groups * N_EXPERTS, n_tok // groups
        shapes += [((rows, n), jnp.int32), ((rows, n), F32), ((n_tok, LANES), jnp.int32)]
    return pl.pallas_call(
        _route_kernel,
        grid=(1,),
        in_specs=[pl.BlockSpec((N_EXPERTS, N_TOK), lambda i: (0, 0))],
        out_specs=[pl.BlockSpec(shape, lambda i: (0, 0)) for shape, _ in shapes],
        out_shape=[jax.ShapeDtypeStruct(shape, dt) for shape, dt in shapes],
        compiler_params=_params("arbitrary"),
        name="route",
    )(lg_t)


CTX_GROUP = 4


def _gather_ctx_kernel(h_ref, slot_ref, aff_ref, x_ref, g_ref, p_ref):
    s_iota = lax.broadcasted_iota(jnp.int32, (CAP_CTX, SEQ), 0)
    for bb in range(CTX_GROUP):
        slots = slice(bb * CAP_CTX, (bb + 1) * CAP_CTX)
        for e in range(N_EXPERTS):
            row = bb * N_EXPERTS + e
            hit = s_iota == slot_ref[row:row + 1, :]
            p_ref[bb, e * CAP_CTX:(e + 1) * CAP_CTX, :] = jnp.where(hit, 1.0, 0.0).astype(BF16)
            g_ref[e, slots] = jnp.where(hit, aff_ref[row:row + 1, :], 0.0).sum(axis=-1, keepdims=True)
        x = _dot(p_ref[bb], h_ref[bb * SEQ:(bb + 1) * SEQ, :]).astype(BF16)
        x_ref[:, slots, :] = x.reshape(N_EXPERTS, CAP_CTX, D_MODEL)


def _gather_ctx(h, slot, aff):
    return pl.pallas_call(
        _gather_ctx_kernel,
        grid=(BATCH // CTX_GROUP,),
        in_specs=[pl.BlockSpec((CTX_GROUP * SEQ, D_MODEL), lambda b: (b, 0)),
                  pl.BlockSpec((CTX_GROUP * N_EXPERTS, SEQ), lambda b: (b, 0)),
                  pl.BlockSpec((CTX_GROUP * N_EXPERTS, SEQ), lambda b: (b, 0))],
        out_specs=[pl.BlockSpec((N_EXPERTS, CTX_GROUP * CAP_CTX, D_MODEL), lambda b: (0, b, 0)),
                   pl.BlockSpec((N_EXPERTS, CTX_GROUP * CAP_CTX, 1), lambda b: (0, b, 0))],
        out_shape=[jax.ShapeDtypeStruct((N_EXPERTS, ROWS_CTX, D_MODEL), BF16),
                   jax.ShapeDtypeStruct((N_EXPERTS, ROWS_CTX, 1), F32)],
        scratch_shapes=[pltpu.VMEM((CTX_GROUP, N_EXPERTS * CAP_CTX, SEQ), BF16)],
        compiler_params=_params("arbitrary"),
        name="gather_ctx",
    )(h, slot, aff)


LAT_GATHER_EXPERTS = 2


def _gather_lat_kernel(h_ref, slot_ref, aff_ref, x_ref, g_ref):
    s_iota = lax.broadcasted_iota(jnp.int32, (CAP_LAT, DEC_SEQ), 0)
    for k in range(LAT_GATHER_EXPERTS):
        e = pl.program_id(1) * LAT_GATHER_EXPERTS + k
        hit = s_iota == slot_ref[pl.ds(e, 1), :]
        x_ref[k] = _dot(jnp.where(hit, 1.0, 0.0).astype(BF16), h_ref[...]).astype(BF16)
        g_ref[k] = jnp.where(hit, aff_ref[pl.ds(e, 1), :], 0.0).sum(axis=-1, keepdims=True)


def _gather_lat(h, slot, aff):
    lat_blk0 = N_CTX // DEC_SEQ
    return pl.pallas_call(
        _gather_lat_kernel,
        grid=(DEC_BATCH, N_EXPERTS // LAT_GATHER_EXPERTS),
        in_specs=[pl.BlockSpec((DEC_SEQ, D_MODEL), lambda b, e: (lat_blk0 + b, 0)),
                  pl.BlockSpec((N_EXPERTS, DEC_SEQ), lambda b, e: (b, 0)),
                  pl.BlockSpec((N_EXPERTS, DEC_SEQ), lambda b, e: (b, 0))],
        out_specs=[pl.BlockSpec((LAT_GATHER_EXPERTS, CAP_LAT, D_MODEL), lambda b, e: (e, b, 0)),
                   pl.BlockSpec((LAT_GATHER_EXPERTS, CAP_LAT, 1), lambda b, e: (e, b, 0))],
        out_shape=[jax.ShapeDtypeStruct((N_EXPERTS, ROWS_LAT, D_MODEL), BF16),
                   jax.ShapeDtypeStruct((N_EXPERTS, ROWS_LAT, 1), F32)],
        compiler_params=_params("arbitrary", "arbitrary"),
        name="gather_lat",
    )(h, slot, aff)


N_FF_CHUNKS = D_FF // TF
CHUNKS_PER_STEP = 4
FF_BLOCK = CHUNKS_PER_STEP * TF
N_FF_STEPS = -(-N_FF_CHUNKS // CHUNKS_PER_STEP)
LAST_STEP_CHUNKS = N_FF_CHUNKS - (N_FF_STEPS - 1) * CHUNKS_PER_STEP


def _silu_tanh(x):
    return x * (0.5 + 0.5 * jnp.tanh(0.5 * x))


def _ffn_kernel(xc_ref, xl_ref, gc_ref, gl_ref, wg_ref, wu_ref, wd_ref, y_ref, x_sc, h_sc, acc_sc):
    j = pl.program_id(1)

    @pl.when(j == 0)
    def _():
        x_sc[0:ROWS_CTX, :] = xc_ref[...]
        x_sc[ROWS_CTX:, :] = xl_ref[...]

    def step(n_chunks, first, last):
        x = x_sc[...]
        for sub in range(n_chunks):
            cols = slice(sub * TF, (sub + 1) * TF)
            a = _dot(x, wg_ref[:, cols].astype(BF16))
            u = _dot(x, wu_ref[:, cols].astype(BF16))
            h_sc[sub] = (_silu_tanh(a) * u).astype(BF16)
        for c0 in range(0, D_MODEL, TD):
            out_cols = slice(c0, c0 + TD)
            part = None
            for sub in range(n_chunks):
                t = _dot(h_sc[sub], wd_ref[sub * TF:(sub + 1) * TF, out_cols].astype(BF16))
                part = t if part is None else part + t
            if not first:
                part = acc_sc[:, out_cols] + part
            if last:
                y_ref[0:ROWS_CTX, out_cols] = (part[0:ROWS_CTX] * gc_ref[...]).astype(BF16)
                y_ref[ROWS_CTX:, out_cols] = (part[ROWS_CTX:] * gl_ref[...]).astype(BF16)
            else:
                acc_sc[:, out_cols] = part

    pl.when(j == 0)(functools.partial(step, CHUNKS_PER_STEP, True, False))
    pl.when((j > 0) & (j < N_FF_STEPS - 1))(functools.partial(step, CHUNKS_PER_STEP, False, False))
    pl.when(j == N_FF_STEPS - 1)(functools.partial(step, LAST_STEP_CHUNKS, False, True))


def _ffn(l, x_c, x_l, g_c, g_l, w_gate, w_up, w_down):
    rows = ROWS_CTX + ROWS_LAT
    xin = lambda r: pl.BlockSpec((None, r, D_MODEL), lambda e, j: (e, 0, 0))
    gin = lambda r: pl.BlockSpec((None, r, 1), lambda e, j: (e, 0, 0))
    return pl.pallas_call(
        _ffn_kernel,
        grid=(N_EXPERTS, N_FF_STEPS),
        in_specs=[xin(ROWS_CTX), xin(ROWS_LAT), gin(ROWS_CTX), gin(ROWS_LAT),
                  pl.BlockSpec((None, None, D_MODEL, FF_BLOCK), lambda e, j: (l, e, 0, j)),
                  pl.BlockSpec((None, None, D_MODEL, FF_BLOCK), lambda e, j: (l, e, 0, j)),
                  pl.BlockSpec((None, None, FF_BLOCK, D_MODEL), lambda e, j: (l, e, j, 0))],
        out_specs=pl.BlockSpec((None, rows, D_MODEL), lambda e, j: (e, 0, 0)),
        out_shape=jax.ShapeDtypeStruct((N_EXPERTS, rows, D_MODEL), BF16),
        scratch_shapes=[pltpu.VMEM((rows, D_MODEL), BF16), pltpu.VMEM((CHUNKS_PER_STEP, rows, TF), BF16),
                        pltpu.VMEM((rows, D_MODEL), F32)],
        compiler_params=_params("arbitrary", "arbitrary"),
        name=f"experts{l}",
    )(x_c, x_l, g_c, g_l, w_gate, w_up, w_down)


def _finish(x, res, mod_ref, gf_ref, final):
    y = x + mod_ref[:, 5 * D_MODEL:] * res
    if final:
        y = y * lax.rsqrt(jnp.mean(y * y, axis=-1, keepdims=True) + RMS_EPS) * gf_ref[...]
    return y


def _combine_ctx_kernel(final, x_ref, y_ref, slot_ref, rep_ref, mod_ref, gf_ref, o_ref):
    n_col = N_EXPERTS * CAP_CTX
    col = (lax.broadcasted_iota(jnp.int32, (SEQ, n_col), 1) % CAP_CTX).astype(F32)
    for bb in range(CTX_GROUP):
        rows = slice(bb * SEQ, (bb + 1) * SEQ)
        spread = _dot(slot_ref[rows, :].astype(F32).astype(BF16), rep_ref[...])
        p = jnp.where(spread == col, 1.0, 0.0).astype(BF16)
        y = y_ref[:, bb * CAP_CTX:(bb + 1) * CAP_CTX, :].reshape(n_col, D_MODEL)
        o_ref[rows, :] = _finish(x_ref[rows, :], _dot(p, y), mod_ref, gf_ref, final)


def _combine_ctx(l, final, x_new, y, slot_t, mod4, g_final):
    n_col = N_EXPERTS * CAP_CTX
    rep = (np.arange(n_col)[None, :] // CAP_CTX == np.arange(LANES)[:, None]).astype(np.float32)
    return pl.pallas_call(
        functools.partial(_combine_ctx_kernel, final),
        grid=(BATCH // CTX_GROUP,),
        in_specs=[pl.BlockSpec((CTX_GROUP * SEQ, D_MODEL), lambda b: (b, 0)),
                  pl.BlockSpec((N_EXPERTS, CTX_GROUP * CAP_CTX, D_MODEL), lambda b: (0, b, 0)),
                  pl.BlockSpec((CTX_GROUP * SEQ, LANES), lambda b: (b, 0)),
                  pl.BlockSpec((LANES, n_col), lambda b: (0, 0)),
                  pl.BlockSpec((None, None, 1, 6 * D_MODEL), lambda b: (l, 0, 0, 0)),
                  pl.BlockSpec((1, D_MODEL), lambda b: (0, 0))],
        out_specs=pl.BlockSpec((CTX_GROUP * SEQ, D_MODEL), lambda b: (b, 0)),
        out_shape=jax.ShapeDtypeStruct((N_CTX, D_MODEL), F32),
        compiler_params=_params("arbitrary"),
        name=f"combine_ctx{l}",
    )(x_new, y, slot_t, jnp.asarray(rep, BF16), mod4, g_final.reshape(1, D_MODEL))


TMC = 512


def _combine_lat_kernel(final, x_ref, y_ref, slot_ref, mod_ref, gf_ref, o_ref):
    s_iota = lax.broadcasted_iota(jnp.int32, (TMC, CAP_LAT), 1)
    slot = slot_ref[...]
    res = None
    for e in range(N_EXPERTS):
        p = jnp.where(slot[:, e:e + 1] == s_iota, 1.0, 0.0).astype(BF16)
        t = _dot(p, y_ref[e])
        res = t if res is None else res + t
    o_ref[...] = _finish(x_ref[...], res, mod_ref, gf_ref, final)


def _combine_lat(l, final, x_new, y, slot_t, mod4, g_final):
    nt = DEC_SEQ // TMC
    return pl.pallas_call(
        functools.partial(_combine_lat_kernel, final),
        grid=(DEC_BATCH, nt),
        in_specs=[pl.BlockSpec((TMC, D_MODEL), lambda b, t: (N_CTX // TMC + b * nt + t, 0)),
                  pl.BlockSpec((N_EXPERTS, CAP_LAT, D_MODEL), lambda b, t: (0, ROWS_CTX // CAP_LAT + b, 0)),
                  pl.BlockSpec((TMC, LANES), lambda b, t: (b * nt + t, 0)),
                  pl.BlockSpec((None, None, 1, 6 * D_MODEL), lambda b, t: (l, 1 + b, 0, 0)),
                  pl.BlockSpec((1, D_MODEL), lambda b, t: (0, 0))],
        out_specs=pl.BlockSpec((TMC, D_MODEL), lambda b, t: (b * nt + t, 0)),
        out_shape=jax.ShapeDtypeStruct((N_LAT, D_MODEL), F32),
        compiler_params=_params("arbitrary", "arbitrary"),
        name=f"combine_lat{l}",
    )(x_new, y, slot_t, mod4, g_final.reshape(1, D_MODEL))


def _split_table(t):
    hi = t.astype(BF16)
    return hi, (t - hi.astype(F32)).astype(BF16)


def _dft_tables(n):
    p = np.arange(n, dtype=np.int64)
    ang = ((p[:, None] * p[None, :]) % n).astype(np.float64) * (2.0 * np.pi / n)
    return np.cos(ang).astype(np.float32), np.sin(ang).astype(np.float32)


def _channel_dft_tables():
    c = np.arange(F_WIDTH, dtype=np.int64)
    same = (c[:, None] // HEAD_DIM) == (c[None, :] // HEAD_DIM)
    ang = (((c[:, None] % HEAD_DIM) * (c[None, :] % HEAD_DIM)) % HEAD_DIM).astype(np.float64) * (2.0 * np.pi / HEAD_DIM)
    return (np.where(same, np.cos(ang), 0.0).astype(np.float32),
            np.where(same, np.sin(ang), 0.0).astype(np.float32))


def _rope_tables():
    half = HEAD_DIM // 2
    nf = half // 2
    pos = np.arange(DEC_SEQ)
    inv = 1.0 / (ROPE_BASE ** (np.arange(nf, dtype=np.float64) / nf))
    ang_r = (pos // GRID_W).astype(np.float64)[:, None] * inv
    ang_c = (pos % GRID_W).astype(np.float64)[:, None] * inv

    def head(fn, sign):
        return np.concatenate([sign * fn(ang_r), fn(ang_r), sign * fn(ang_c), fn(ang_c)], axis=-1)

    cos = head(np.cos, 1.0)
    sin = head(np.sin, -1.0)
    return (np.concatenate([cos, cos], axis=-1).astype(np.float32),
            np.concatenate([sin, sin], axis=-1).astype(np.float32))


def _na_bias_tables(rpb):
    cq = np.arange(GRID_W)
    rel_c = np.clip(cq[None, :] - cq[:, None] + NA_COLS - 1, 0, 2 * NA_COLS - 2)
    pick = (rel_c[:, :, None] == np.arange(2 * NA_COLS - 1)).astype(np.float32)
    cs = np.clip(cq - NA_COLS // 2, 0, GRID_W - NA_COLS)
    col_ok = (cq[None, :] >= cs[:, None]) & (cq[None, :] < cs[:, None] + NA_COLS)
    bc = jnp.einsum('lhrj,qkj->lhrqk', rpb, pick, precision=lax.Precision.HIGHEST)
    bc = jnp.where(col_ok[None, None, None], bc, NEG_INF)
    return pl.pallas_call(
        _na_bias_kernel,
        grid=(DEPTH, NA_VARIANTS, NA_HEADS // 2),
        in_specs=[pl.BlockSpec((None, 2, 2 * NA_ROWS - 1, GRID_W, GRID_W), lambda l, v, p: (l, p, 0, 0, 0))],
        out_specs=pl.BlockSpec((None, None, None, 2 * NA_Q, NA_WIN_ROWS * GRID_W),
                               lambda l, v, p: (l, v, p, 0, 0)),
        out_shape=jax.ShapeDtypeStruct((DEPTH, NA_VARIANTS, NA_HEADS // 2, 2 * NA_Q, NA_WIN_ROWS * GRID_W), F32),
        compiler_params=_params("arbitrary", "arbitrary", "arbitrary"),
        name="na_bias",
    )(bc)


def _na_window_plan():
    plan = []
    for g in (0, 1, NA_BLOCKS - 1):
        start = int(np.clip(NA_G * g - NA_ROWS // 2, 0, GRID_ROWS - NA_WIN_ROWS))
        rows = []
        for a in range(NA_G):
            r = NA_G * g + a
            rs = int(np.clip(r - NA_ROWS // 2, 0, GRID_ROWS - NA_ROWS))
            rows.append([start + w - r + NA_ROWS - 1 if rs <= start + w < rs + NA_ROWS else None
                         for w in range(NA_WIN_ROWS)])
        plan.append(rows)
    return plan


def _na_bias_kernel(bc_ref, o_ref):
    outside = jnp.full((GRID_W, GRID_W), NEG_INF, F32)
    for v, rows in enumerate(_na_window_plan()):
        @pl.when(pl.program_id(1) == v)
        def _():
            for half in range(2):
                for a, rel in enumerate(rows):
                    tiles = [outside if rr is None else bc_ref[half, rr] for rr in rel]
                    r0 = half * NA_Q + a * GRID_W
                    o_ref[r0:r0 + GRID_W, :] = jnp.concatenate(tiles, axis=-1)


def kernel(x_prompt, x_sample, cache_win_k, cache_win_v, cache_nat_k, cache_nat_v, c, c_ctx, w_mod, b_mod, g_mix, g_ffn, w_in, w_out, win_sink, nat_rpb, w_router, w_gate, w_up, w_down, g_final):
    x_ctx = x_prompt.reshape(N_CTX, D_MODEL)
    x_lat = x_sample.reshape(N_LAT, D_MODEL)
    cond = jnp.concatenate([c_ctx[None, :], c, jnp.zeros((N_COND - 1 - DEC_BATCH, D_MODEL), F32)], axis=0)
    mod4 = _adaln(cond, w_mod, b_mod).reshape(DEPTH, N_COND, 1, 6 * D_MODEL)

    dft_ch = _channel_dft_tables()
    dft_ctx = _dft_tables(SEQ)
    dft_lat = _dft_tables(DEC_SEQ)
    cos_t, sin_t = _rope_tables()
    cwk = cache_win_k.reshape(DEC_BATCH, DEPTH, PAST_LEN, WIN_KV)
    cwv = cache_win_v.reshape(DEC_BATCH, DEPTH, PAST_LEN, WIN_KV)
    cnk = cache_nat_k.reshape(DEC_BATCH, DEPTH, PAST_LEN, NA_W)
    cnv = cache_nat_v.reshape(DEC_BATCH, DEPTH, PAST_LEN, NA_W)
    r_pad = jnp.pad(w_router, ((0, 0), (0, 0), (0, LANES - N_EXPERTS)))
    na_bias = _na_bias_tables(nat_rpb)

    kvt = ()
    for l in range(DEPTH):
        final = l == DEPTH - 1
        (f, qw, qn), (kw, vw, kn, vn), kvt = _project(l, x_ctx, x_lat, mod4, g_mix, w_in, kvt)

        mixed_ctx = _ctx_mixer(l, win_sink, f, qw, qn, kvt, dft_ch, dft_ctx)
        mixed_f = _lat_fourier(f, dft_ch, dft_lat)
        mixed_w = _lat_window(l, win_sink, qw, kw, vw, cwk, cwv, cos_t, sin_t)
        mixed_n = _lat_neighbourhood(l, qn, kn, vn, cnk, cnv, na_bias)

        r_hi, r_lo = _split_table(r_pad[l])
        x_new, h, lg_t = _outproj(l, x_ctx, x_lat, mixed_ctx, mixed_f, mixed_w, mixed_n, mod4, g_ffn,
                                  w_out, r_hi, r_lo)

        slot_c, aff_c, slot_ct, slot_l, aff_l, slot_lt = _route(lg_t)
        xg_c, gate_c = _gather_ctx(h, slot_c, aff_c)
        xg_l, gate_l = _gather_lat(h, slot_l, aff_l)
        y = _ffn(l, xg_c, xg_l, gate_c, gate_l, w_gate, w_up, w_down)
        x_ctx = _combine_ctx(l, final, x_new, y, slot_ct, mod4, g_final)
        x_lat = _combine_lat(l, final, x_new, y, slot_lt, mod4, g_final)

    y_prompt = x_ctx.reshape(BATCH, SEQ, D_MODEL)
    y_sample = x_lat.reshape(DEC_BATCH, DEC_SEQ, D_MODEL)
    new_kv = [t.reshape(BATCH, DEPTH, w // HEAD_DIM, HEAD_DIM, SEQ).transpose(0, 1, 4, 2, 3)
              for t, w in zip(kvt, KV_WIDTHS)]
    return (y_prompt, y_sample, *new_kv)
```

```python
import functools

import numpy as np
import jax
import jax.numpy as jnp
from jax import lax
from jax.experimental import pallas as pl
from jax.experimental.pallas import tpu as pltpu

D_MODEL = 1024
BATCH = 16
SEQ = 256
DEPTH = 2
DEC_BATCH = 2
DEC_SEQ = 2048
PAST_LEN = 256
GRID_W = 64
HEAD_DIM = 64
F_WIDTH = 256
WIN_HEADS = 6
WIN_KV_HEADS = 2
WINDOW = 128
WIN_BLOCK = 128
NA_HEADS = 6
NA_ROWS = 8
NA_COLS = 16
N_EXPERTS = 16
EC_CAPACITY = 2
D_FF = 2816
ROPE_BASE = 10000.0
RMS_EPS = 1e-6
NEG_INF = -1e30
ATTN_SCALE = HEAD_DIM ** -0.5
WIN_Q = WIN_HEADS * HEAD_DIM
WIN_KV = WIN_KV_HEADS * HEAD_DIM
NA_W = NA_HEADS * HEAD_DIM
N_IN = F_WIDTH + WIN_Q + 2 * WIN_KV + 3 * NA_W
SPLITS = (0, F_WIDTH, F_WIDTH + WIN_Q, F_WIDTH + WIN_Q + WIN_KV, F_WIDTH + WIN_Q + 2 * WIN_KV,
          F_WIDTH + WIN_Q + 2 * WIN_KV + NA_W, F_WIDTH + WIN_Q + 2 * WIN_KV + 2 * NA_W, N_IN)

N_CTX = BATCH * SEQ
N_LAT = DEC_BATCH * DEC_SEQ
N_TOK = N_CTX + N_LAT
GRID_ROWS = DEC_SEQ // GRID_W
CAP_CTX = EC_CAPACITY * SEQ // N_EXPERTS
CAP_LAT = EC_CAPACITY * DEC_SEQ // N_EXPERTS
ROWS_CTX = BATCH * CAP_CTX
ROWS_LAT = DEC_BATCH * CAP_LAT
N_COND = 8

LANES = 128
MXU_COLS = 256
TM = 512
TN_MOD = 1536
TF = 256
TD = 256
VMEM_LIMIT = 56 * 1024 * 1024

F32 = jnp.float32
BF16 = jnp.bfloat16


def _params(*sem):
    return pltpu.CompilerParams(dimension_semantics=sem, vmem_limit_bytes=VMEM_LIMIT)


def _dot(a, b):
    return jnp.dot(a, b, preferred_element_type=F32)


def _dot_nt(a, b):
    return lax.dot_general(a, b, (((1,), (1,)), ((), ())), preferred_element_type=F32)


def _split(x):
    hi = x.astype(BF16)
    lo = (x - hi.astype(F32)).astype(BF16)
    return hi, lo


def _dot3(a_hi, a_lo, b_hi, b_lo):
    return _dot(a_hi, b_hi) + (_dot(a_lo, b_hi) + _dot(a_hi, b_lo))


def _silu(x):
    return x / (1.0 + jnp.exp(-x))


def _rms_mod(x, g, shift, scale):
    y = x * lax.rsqrt(jnp.mean(x * x, axis=-1, keepdims=True) + RMS_EPS)
    return (y * g) * (1.0 + scale) + shift


def _softmax_parts(parts, sink=None):
    m = parts[0].max(axis=-1, keepdims=True)
    for s in parts[1:]:
        m = jnp.maximum(m, s.max(axis=-1, keepdims=True))
    if sink is not None:
        m = jnp.maximum(m, sink)
    es = [jnp.exp(s - m) for s in parts]
    den = es[0].sum(axis=-1, keepdims=True)
    for e in es[1:]:
        den = den + e.sum(axis=-1, keepdims=True)
    if sink is not None:
        den = den + jnp.exp(sink - m)
    inv = 1.0 / den
    return [e * inv for e in es]


def _cond_of_block(i):
    n_ctx_blocks = N_CTX // TM
    return jnp.where(i < n_ctx_blocks, 0, 1 + (i - n_ctx_blocks) // (DEC_SEQ // TM))


def _two_stream_specs(width):
    n_ctx_blocks = N_CTX // TM
    ctx = pl.BlockSpec((TM, width), lambda i: (jnp.minimum(i, n_ctx_blocks - 1), 0))
    lat = pl.BlockSpec((TM, width), lambda i: (jnp.maximum(i - n_ctx_blocks, 0), 0))
    return ctx, lat


def _pick_stream(ctx_ref, lat_ref):
    return jnp.where(pl.program_id(0) < N_CTX // TM, ctx_ref[...], lat_ref[...])


def _adaln_kernel(c_ref, w_ref, b_ref, o_ref):
    s_hi, s_lo = _split(_silu(c_ref[...]))
    w_hi, w_lo = _split(w_ref[...])
    o_ref[...] = _dot3(s_hi, s_lo, w_hi, w_lo) + b_ref[...]


def _adaln(cond, w_mod, b_mod):
    return pl.pallas_call(
        _adaln_kernel,
        grid=(DEPTH, 6 * D_MODEL // TN_MOD),
        in_specs=[
            pl.BlockSpec((N_COND, D_MODEL), lambda l, j: (0, 0)),
            pl.BlockSpec((None, D_MODEL, TN_MOD), lambda l, j: (l, 0, j)),
            pl.BlockSpec((None, 1, TN_MOD), lambda l, j: (l, 0, j)),
        ],
        out_specs=pl.BlockSpec((None, N_COND, TN_MOD), lambda l, j: (l, 0, j)),
        out_shape=jax.ShapeDtypeStruct((DEPTH, N_COND, 6 * D_MODEL), F32),
        compiler_params=_params("arbitrary", "arbitrary"),
        name="adaln",
    )(cond, w_mod, b_mod.reshape(DEPTH, 1, 6 * D_MODEL))


KV_NAMES = ("kw", "vw", "kn", "vn")
KV_WIDTHS = (WIN_KV, WIN_KV, NA_W, NA_W)
KV_SPLITS = (SPLITS[2], SPLITS[3], SPLITS[5], SPLITS[6])
KV_TOTAL = sum(KV_WIDTHS)
Q_SPLITS = (SPLITS[0], SPLITS[1], SPLITS[4])


def _proj_kernel(n_prev, xc_ref, xl_ref, mod_ref, g_ref, w_ref, *rest):
    rest = rest[n_prev:]
    f_ref, qw_ref, qn_ref = rest[0:3]
    lat_kv = rest[3:7]
    ctx_kvt = rest[7:11]
    wb_ref, wt_ref = rest[11:13]
    i = pl.program_id(0)

    @pl.when(i == 0)
    def _():
        wb_ref[...] = w_ref[...].astype(BF16)
        r0 = 0
        for c0, width in zip(KV_SPLITS, KV_WIDTHS):
            wt_ref[r0:r0 + width, :] = w_ref[:, c0:c0 + width].T.astype(BF16)
            r0 += width

    natural = dict(zip(SPLITS[:-1], (f_ref, qw_ref) + tuple(lat_kv[0:2]) + (qn_ref,) + tuple(lat_kv[2:4])))

    def hidden(x_ref):
        return _rms_mod(x_ref[...], g_ref[...], mod_ref[:, 0:D_MODEL], mod_ref[:, D_MODEL:2 * D_MODEL]).astype(BF16)

    def project_tiles(h, wanted):
        for t0 in range(0, N_IN, MXU_COLS):
            hits = [(a, b) for a, b in zip(SPLITS[:-1], SPLITS[1:])
                    if a in wanted and max(a, t0) < min(b, t0 + MXU_COLS)]
            if not hits:
                continue
            acc = _dot(h, wb_ref[:, t0:t0 + MXU_COLS])
            for a, b in hits:
                lo, hi = max(a, t0), min(b, t0 + MXU_COLS)
                natural[a][:, lo - a:hi - a] = acc[:, lo - t0:hi - t0].astype(natural[a].dtype)

    @pl.when(i < N_CTX // TM)
    def _():
        h = hidden(xc_ref)
        project_tiles(h, Q_SPLITS)
        kvt = _dot_nt(wt_ref[...], h)
        r0 = 0
        for o_ref, width in zip(ctx_kvt, KV_WIDTHS):
            for bb in range(TM // SEQ):
                o_ref[bb] = kvt[r0:r0 + width, bb * SEQ:(bb + 1) * SEQ]
            r0 += width

    @pl.when(i >= N_CTX // TM)
    def _():
        project_tiles(hidden(xl_ref), Q_SPLITS + KV_SPLITS)


def _project(l, x_ctx, x_lat, mod4, g_mix, w_in, prev_kvt):
    n_ctx_blocks = N_CTX // TM
    xc_spec, xl_spec = _two_stream_specs(D_MODEL)
    both = lambda w: pl.BlockSpec((TM, w), lambda i: (i, 0))
    lat = lambda w: pl.BlockSpec((TM, w), lambda i: (jnp.maximum(i - n_ctx_blocks, 0), 0))
    ctx_t = lambda w: pl.BlockSpec((TM // SEQ, None, w, SEQ),
                                   lambda i: (jnp.minimum(i, n_ctx_blocks - 1), l, 0, 0))
    n_prev = len(prev_kvt)
    n_in = 5
    outs = pl.pallas_call(
        functools.partial(_proj_kernel, n_prev),
        grid=(N_TOK // TM,),
        in_specs=[
            xc_spec, xl_spec,
            pl.BlockSpec((None, None, 1, 6 * D_MODEL), lambda i: (l, _cond_of_block(i), 0, 0)),
            pl.BlockSpec((None, 1, D_MODEL), lambda i: (l, 0, 0)),
            pl.BlockSpec((None, D_MODEL, N_IN), lambda i: (l, 0, 0)),
        ] + [pl.BlockSpec(memory_space=pl.ANY)] * n_prev,
        out_specs=[both(F_WIDTH), both(WIN_Q), both(NA_W)] + [lat(w) for w in KV_WIDTHS]
                  + [ctx_t(w) for w in KV_WIDTHS],
        out_shape=[jax.ShapeDtypeStruct((N_TOK, w), dt) for w, dt in ((F_WIDTH, F32), (WIN_Q, F32), (NA_W, BF16))]
                  + [jax.ShapeDtypeStruct((N_LAT, w), dt)
                     for w, dt in zip(KV_WIDTHS, (F32, BF16, BF16, BF16))]
                  + [jax.ShapeDtypeStruct((BATCH, DEPTH, w, SEQ), F32) for w in KV_WIDTHS],
        input_output_aliases={n_in + k: 7 + k for k in range(n_prev)},
        scratch_shapes=[pltpu.VMEM((D_MODEL, N_IN), BF16), pltpu.VMEM((KV_TOTAL, D_MODEL), BF16)],
        compiler_params=_params("arbitrary"),
        name=f"project{l}",
    )(x_ctx, x_lat, mod4, g_mix.reshape(DEPTH, 1, D_MODEL), w_in, *prev_kvt)
    return outs[0:3], outs[3:7], outs[7:11]


def _lane_is_low(shape):
    return lax.broadcasted_iota(jnp.int32, shape, len(shape) - 1) < HEAD_DIM


def _swap_halves(x):
    return pltpu.roll(x, HEAD_DIM, axis=x.ndim - 1)


def _win_kv_copy(h):
    return 0 if (h // (WIN_HEADS // WIN_KV_HEADS)) == (h % 2) else 1


def _stack_heads(q_pairs, heads):
    low = _lane_is_low(q_pairs[heads[0] // 2].shape)
    rows = [jnp.where(low if h % 2 == 0 else jnp.logical_not(low), q_pairs[h // 2], 0.0).astype(BF16)
            for h in heads]
    return rows[0] if len(rows) == 1 else jnp.concatenate(rows, axis=0)


def _per_head_column(values, rows_per_head):
    blk = lax.broadcasted_iota(jnp.int32, (len(values) * rows_per_head, 1), 0) // rows_per_head
    col = jnp.full(blk.shape, values[0], F32)
    for i in range(1, len(values)):
        col = jnp.where(blk == i, values[i], col)
    return col


def _attend(q_stack, kv_list, extra_logit=None, transposed=False):
    scores = []
    for k, _, post in kv_list:
        s = _dot(q_stack, k) if transposed else _dot_nt(q_stack, k)
        scores.append(post(s) if post is not None else s)
    probs = _softmax_parts(scores, extra_logit)
    o = None
    for p, (_, v, _) in zip(probs, kv_list):
        t = _dot_nt(p.astype(BF16), v) if transposed else _dot(p.astype(BF16), v)
        o = t if o is None else o + t
    return o


def _merge_pair(o_even, o_odd):
    return jnp.where(_lane_is_low(o_even.shape), o_even, o_odd)


def _gqa_attention(q_pairs, rows, kv_for_copy, sinks, transposed=False):
    per_head = {}
    for copy in (0, 1):
        heads = [h for h in range(WIN_HEADS) if _win_kv_copy(h) == copy]
        o = _attend(_stack_heads(q_pairs, heads), kv_for_copy(copy),
                    _per_head_column([sinks[h] for h in heads], rows), transposed)
        for i, h in enumerate(heads):
            per_head[h] = o[i * rows:(i + 1) * rows]
    return [_merge_pair(per_head[2 * j], per_head[2 * j + 1]) for j in range(WIN_HEADS // 2)]


def _swap_row_halves(x):
    return jnp.concatenate([x[HEAD_DIM:], x[:HEAD_DIM]], axis=0)


MIX_GROUP = 2


def _ctx_mixer_kernel(l, sink_ref, f_ref, qw_ref, qn_ref, kw_ref, vw_ref, kn_ref, vn_ref,
                      bc_ref, bs_ref, cs_ref, ss_ref, o_ref):
    bc_hi, bc_lo = _split(bc_ref[...])
    bs_hi, bs_lo = _split(bs_ref[...])
    cs_hi, cs_lo = _split(cs_ref[...])
    ss_hi, ss_lo = _split(ss_ref[...])
    for bb in range(MIX_GROUP):
        rows = slice(bb * SEQ, (bb + 1) * SEQ)
        f_hi, f_lo = _split(f_ref[rows, :])
        fc_hi, fc_lo = _split(_dot3(f_hi, f_lo, bc_hi, bc_lo))
        fs_hi, fs_lo = _split(_dot3(f_hi, f_lo, bs_hi, bs_lo))
        z = _dot3(cs_hi, cs_lo, fc_hi, fc_lo) - _dot3(ss_hi, ss_lo, fs_hi, fs_lo)
        o_ref[rows, 0:F_WIDTH] = (z * (SEQ * HEAD_DIM) ** -0.5).astype(BF16)

        kv = [(kw_ref[bb].astype(BF16), vw_ref[bb].astype(BF16)),
              (_swap_row_halves(kw_ref[bb]).astype(BF16), _swap_row_halves(vw_ref[bb]).astype(BF16))]
        q_pairs = [qw_ref[rows, LANES * j:LANES * (j + 1)] * ATTN_SCALE for j in range(WIN_HEADS // 2)]
        outs = _gqa_attention(q_pairs, SEQ, lambda c: [(kv[c][0], kv[c][1], None)],
                              [sink_ref[l, h] for h in range(WIN_HEADS)], transposed=True)
        for j, o in enumerate(outs):
            o_ref[rows, F_WIDTH + LANES * j:F_WIDTH + LANES * (j + 1)] = o.astype(BF16)

        for j in range(NA_HEADS // 2):
            sl = slice(LANES * j, LANES * (j + 1))
            q_pairs = {j: qn_ref[rows, sl] * ATTN_SCALE}
            o = _attend(_stack_heads(q_pairs, (2 * j, 2 * j + 1)),
                        [(kn_ref[bb, sl, :].astype(BF16), vn_ref[bb, sl, :].astype(BF16), None)], transposed=True)
            base = F_WIDTH + WIN_Q + LANES * j
            o_ref[rows, base:base + LANES] = _merge_pair(o[:SEQ], o[SEQ:]).astype(BF16)


def _ctx_mixer(l, win_sink, f, qw, qn, kvt, dft_ch, dft_seq):
    row = lambda w: pl.BlockSpec((MIX_GROUP * SEQ, w), lambda b: (b, 0))
    col = lambda w: pl.BlockSpec((MIX_GROUP, None, w, SEQ), lambda b: (b, l, 0, 0))
    const = lambda n: pl.BlockSpec((n, n), lambda b: (0, 0))
    return pl.pallas_call(
        functools.partial(_ctx_mixer_kernel, l),
        grid=(BATCH // MIX_GROUP,),
        in_specs=[pl.BlockSpec(memory_space=pltpu.SMEM),
                  row(F_WIDTH), row(WIN_Q), row(NA_W)] + [col(w) for w in KV_WIDTHS]
                 + [const(F_WIDTH)] * 2 + [const(SEQ)] * 2,
        out_specs=pl.BlockSpec((MIX_GROUP * SEQ, D_MODEL), lambda b: (b, 0)),
        out_shape=jax.ShapeDtypeStruct((N_CTX, D_MODEL), BF16),
        compiler_params=_params("arbitrary"),
        name=f"ctx_mixer{l}",
    )(win_sink, f, qw, qn, *kvt, *dft_ch, *dft_seq)


FT_ROWS = 512


HALF_SEQ = DEC_SEQ // 2
SUBLANES = 8


def _lat_fourier_kernel(f_ref, mir_ref, bc_ref, bs_ref, cs_ref, ss_ref, o_ref, st_ref, mid_ref):
    r, b = pl.program_id(0), pl.program_id(1)

    @pl.when(r == 0)
    def _():
        f_hi, f_lo = _split(f_ref[pl.ds(pl.multiple_of(b * DEC_SEQ, DEC_SEQ), DEC_SEQ), :])
        fc = _dot3(f_hi, f_lo, *_split(bc_ref[...]))
        fs = _dot3(f_hi, f_lo, *_split(bs_ref[...]))

        def mirrored(t):
            hi, lo = _split(t[HALF_SEQ:])
            return _dot(mir_ref[...], hi) + _dot(mir_ref[...], lo)

        st_ref[b, 0], st_ref[b, 1] = _split(fc[:HALF_SEQ] + mirrored(fc))
        st_ref[b, 2], st_ref[b, 3] = _split(fs[:HALF_SEQ] - mirrored(fs))
        mid_ref[b] = fc[HALF_SEQ:HALF_SEQ + SUBLANES]

    z = (_dot3(*_split(cs_ref[...]), st_ref[b, 0], st_ref[b, 1])
         - _dot3(*_split(ss_ref[...]), st_ref[b, 2], st_ref[b, 3]))
    k = r * FT_ROWS + lax.broadcasted_iota(jnp.int32, (FT_ROWS, 1), 0)
    z = z + jnp.where(k % 2 == 0, 1.0, -1.0) * mid_ref[b, 0:1, :]
    o_ref[...] = (z * (DEC_SEQ * HEAD_DIM) ** -0.5).astype(BF16)


def _lat_fourier(f, dft_ch, dft_seq):
    nrb = DEC_SEQ // FT_ROWS
    p = np.arange(HALF_SEQ)
    mirror = ((p[:, None] + p[None, :] == HALF_SEQ) & (p[:, None] > 0)).astype(np.float32)
    const = pl.BlockSpec((F_WIDTH, F_WIDTH), lambda r, b: (0, 0))
    rows = pl.BlockSpec((FT_ROWS, HALF_SEQ), lambda r, b: (r, 0))
    return pl.pallas_call(
        _lat_fourier_kernel,
        grid=(nrb, DEC_BATCH),
        in_specs=[pl.BlockSpec((N_LAT, F_WIDTH), lambda r, b: (1, 0)),
                  pl.BlockSpec((HALF_SEQ, HALF_SEQ), lambda r, b: (0, 0))] + [const] * 2 + [rows] * 2,
        out_specs=pl.BlockSpec((FT_ROWS, F_WIDTH), lambda r, b: (b * nrb + r, 0)),
        out_shape=jax.ShapeDtypeStruct((N_LAT, F_WIDTH), BF16),
        scratch_shapes=[pltpu.VMEM((DEC_BATCH, 4, HALF_SEQ, F_WIDTH), BF16),
                        pltpu.VMEM((DEC_BATCH, SUBLANES, F_WIDTH), F32)],
        compiler_params=_params("arbitrary", "arbitrary"),
        name="lat_fourier",
    )(f, jnp.asarray(mirror, BF16), *dft_ch, *dft_seq)


def _rope(x, cos, sin_signed):
    n = x.shape[-1]
    lane = lax.broadcasted_iota(jnp.int32, x.shape, x.ndim - 1)
    first = (lane % 32) < 16
    partner = jnp.where(first, pltpu.roll(x, n - 16, axis=x.ndim - 1), pltpu.roll(x, 16, axis=x.ndim - 1))
    return x * cos + partner * sin_signed


def _win_kernel(l, sink_ref, q_ref, k_ref, v_ref, ck_ref, cv_ref, cos_ref, sin_ref, cosq_ref, sinq_ref,
                o_ref, kp_ref, vp_ref, cp_ref):
    n = pl.program_id(1)
    nb = DEC_SEQ // WIN_BLOCK
    pad = WIN_BLOCK

    @pl.when(n == 0)
    def _():
        zeros = jnp.zeros((pad, LANES), BF16)
        kr = _rope(k_ref[...], cos_ref[...], sin_ref[...])
        v = v_ref[...]
        for idx, (kk, vv) in enumerate(((kr, v), (_swap_halves(kr), _swap_halves(v)))):
            kp_ref[idx, 0:pad] = zeros
            kp_ref[idx, pad + DEC_SEQ:] = zeros
            kp_ref[idx, pad:pad + DEC_SEQ] = kk.astype(BF16)
            vp_ref[idx, 0:pad] = zeros
            vp_ref[idx, pad + DEC_SEQ:] = zeros
            vp_ref[idx, pad:pad + DEC_SEQ] = vv.astype(BF16)
        ck = ck_ref[...]
        cv = cv_ref[...]
        cp_ref[0] = ck.astype(BF16)
        cp_ref[1] = _swap_halves(ck).astype(BF16)
        cp_ref[2] = cv.astype(BF16)
        cp_ref[3] = _swap_halves(cv).astype(BF16)

    lo = jnp.where(n == 0, WIN_BLOCK, 0)
    hi = jnp.where(n == nb - 1, 2 * WIN_BLOCK, 3 * WIN_BLOCK)

    i = lax.broadcasted_iota(jnp.int32, (WIN_BLOCK, 3 * WIN_BLOCK), 0)
    j = lax.broadcasted_iota(jnp.int32, (WIN_BLOCK, 3 * WIN_BLOCK), 1)
    mask = (j >= i + WIN_BLOCK - WINDOW) & (j <= i + WIN_BLOCK + WINDOW) & (j >= lo) & (j < hi)
    band_bias = jnp.where(mask, 0.0, NEG_INF)

    def band(s):
        heads = s.shape[0] // WIN_BLOCK
        return (s.reshape(heads, WIN_BLOCK, s.shape[1]) + band_bias[None]).reshape(s.shape)

    start = pl.multiple_of(n * WIN_BLOCK, WIN_BLOCK)
    win = pl.ds(start, 3 * WIN_BLOCK)
    q_pairs = [_rope(q_ref[:, LANES * jp:LANES * (jp + 1)], cosq_ref[...], sinq_ref[...]) * ATTN_SCALE
               for jp in range(WIN_HEADS // 2)]
    outs = _gqa_attention(
        q_pairs, WIN_BLOCK,
        lambda c: [(kp_ref[c, win, :], vp_ref[c, win, :], band), (cp_ref[c], cp_ref[2 + c], None)],
        [sink_ref[l, h] for h in range(WIN_HEADS)])
    for jp, o in enumerate(outs):
        o_ref[:, LANES * jp:LANES * (jp + 1)] = o.astype(BF16)


def _lat_window(l, win_sink, qw, kw, vw, cache_k, cache_v, cos_t, sin_t):
    nb = DEC_SEQ // WIN_BLOCK
    kv_spec = pl.BlockSpec((DEC_SEQ, WIN_KV), lambda b, n: (b, 0))
    cache_spec = pl.BlockSpec((None, None, PAST_LEN, WIN_KV), lambda b, n: (b, l, 0, 0))
    tab_all = pl.BlockSpec((DEC_SEQ, LANES), lambda b, n: (0, 0))
    tab_blk = pl.BlockSpec((WIN_BLOCK, LANES), lambda b, n: (n, 0))
    return pl.pallas_call(
        functools.partial(_win_kernel, l),
        grid=(DEC_BATCH, nb),
        in_specs=[pl.BlockSpec(memory_space=pltpu.SMEM),
                  pl.BlockSpec((WIN_BLOCK, WIN_Q), lambda b, n: (N_CTX // WIN_BLOCK + b * nb + n, 0)),
                  kv_spec, kv_spec, cache_spec, cache_spec, tab_all, tab_all, tab_blk, tab_blk],
        out_specs=pl.BlockSpec((WIN_BLOCK, WIN_Q), lambda b, n: (b * nb + n, 0)),
        out_shape=jax.ShapeDtypeStruct((N_LAT, WIN_Q), BF16),
        scratch_shapes=[pltpu.VMEM((2, DEC_SEQ + 2 * WIN_BLOCK, LANES), BF16),
                        pltpu.VMEM((2, DEC_SEQ + 2 * WIN_BLOCK, LANES), BF16),
                        pltpu.VMEM((4, PAST_LEN, LANES), BF16)],
        compiler_params=_params("arbitrary", "arbitrary"),
        name=f"lat_window{l}",
    )(win_sink, qw, kw, vw, cache_k, cache_v, cos_t, sin_t, cos_t, sin_t)


NA_G = 4
NA_Q = NA_G * GRID_W
NA_WIN_ROWS = NA_ROWS + NA_G
NA_BLOCKS = GRID_ROWS // NA_G
NA_VARIANTS = 3


def _na_block_start(g):
    return jnp.clip(NA_G * g - NA_ROWS // 2, 0, GRID_ROWS - NA_WIN_ROWS)


def _na_kernel(q_ref, k_ref, v_ref, ck_ref, cv_ref, bias_ref, o_ref):
    g = pl.program_id(1)
    start = pl.multiple_of(_na_block_start(g) * GRID_W, GRID_W)
    win = pl.ds(start, NA_WIN_ROWS * GRID_W)
    for jp in range(NA_HEADS // 2):
        sl = slice(LANES * jp, LANES * (jp + 1))
        q_stack = _stack_heads({jp: q_ref[:, sl] * ATTN_SCALE}, (2 * jp, 2 * jp + 1))
        bias = bias_ref[jp]
        o = _attend(q_stack, [(k_ref[win, sl].astype(BF16), v_ref[win, sl].astype(BF16), lambda s: s + bias),
                              (ck_ref[:, sl].astype(BF16), cv_ref[:, sl].astype(BF16), None)])
        o_ref[:, sl] = _merge_pair(o[:NA_Q], o[NA_Q:]).astype(BF16)


def _lat_neighbourhood(l, qn, kn, vn, cache_k, cache_v, bias_tab):
    kv_spec = pl.BlockSpec((DEC_SEQ, NA_W), lambda b, g: (b, 0))
    cache_spec = pl.BlockSpec((None, None, PAST_LEN, NA_W), lambda b, g: (b, l, 0, 0))
    variant = lambda g: jnp.where(g == 0, 0, jnp.where(g == NA_BLOCKS - 1, 2, 1))
    return pl.pallas_call(
        _na_kernel,
        grid=(DEC_BATCH, NA_BLOCKS),
        in_specs=[pl.BlockSpec((NA_Q, NA_W), lambda b, g: (N_CTX // NA_Q + b * NA_BLOCKS + g, 0)),
                  kv_spec, kv_spec, cache_spec, cache_spec,
                  pl.BlockSpec((None, None, NA_HEADS // 2, 2 * NA_Q, NA_WIN_ROWS * GRID_W),
                               lambda b, g: (l, variant(g), 0, 0, 0))],
        out_specs=pl.BlockSpec((NA_Q, NA_W), lambda b, g: (b * NA_BLOCKS + g, 0)),
        out_shape=jax.ShapeDtypeStruct((N_LAT, NA_W), BF16),
        compiler_params=_params("arbitrary", "arbitrary"),
        name=f"lat_neighbourhood{l}",
    )(qn, kn, vn, cache_k, cache_v, bias_tab)


OUTPROJ_ROWS = 256


def _outproj_kernel(xc_ref, xl_ref, mc_ref, mf_ref, mw_ref, mn_ref, mod_ref, g_ref, w_ref, r_hi, r_lo,
                    x_ref, h_ref, lg_ref, wb_ref):
    i = pl.program_id(0)

    @pl.when(i == 0)
    def _():
        wb_ref[...] = w_ref[...].astype(BF16)

    d = D_MODEL

    def block(x_in_ref, mixed_rows):
        for r0 in range(0, TM, OUTPROJ_ROWS):
            rows = slice(r0, r0 + OUTPROJ_ROWS)
            x = x_in_ref[rows, :] + mod_ref[:, 2 * d:3 * d] * _dot(mixed_rows(rows), wb_ref[...])
            x_ref[rows, :] = x
            h = _rms_mod(x, g_ref[...], mod_ref[:, 3 * d:4 * d], mod_ref[:, 4 * d:5 * d])
            h_ref[rows, :] = h.astype(BF16)
            h_hi, h_lo = _split(h)
            lg_ref[:, rows] = _dot3(h_hi, h_lo, r_hi[...], r_lo[...]).T[0:N_EXPERTS, :]

    @pl.when(i < N_CTX // TM)
    def _():
        block(xc_ref, lambda rows: mc_ref[rows, :])

    @pl.when(i >= N_CTX // TM)
    def _():
        block(xl_ref, lambda rows: jnp.concatenate([mf_ref[rows, :], mw_ref[rows, :], mn_ref[rows, :]], axis=1))


def _outproj(l, x_ctx, x_lat, mixed_ctx, mixed_f, mixed_w, mixed_n, mod4, g_ffn, w_out, r_hi, r_lo):
    n_ctx_blocks = N_CTX // TM
    xc_spec, xl_spec = _two_stream_specs(D_MODEL)
    lat = lambda w: pl.BlockSpec((TM, w), lambda i: (jnp.maximum(i - n_ctx_blocks, 0), 0))
    whole = lambda shape: pl.BlockSpec(shape, lambda i: (0,) * len(shape))
    row = lambda w: pl.BlockSpec((TM, w), lambda i: (i, 0))
    return pl.pallas_call(
        _outproj_kernel,
        grid=(N_TOK // TM,),
        in_specs=[xc_spec, xl_spec,
                  pl.BlockSpec((TM, D_MODEL), lambda i: (jnp.minimum(i, n_ctx_blocks - 1), 0)),
                  lat(F_WIDTH), lat(WIN_Q), lat(NA_W),
                  pl.BlockSpec((None, None, 1, 6 * D_MODEL), lambda i: (l, _cond_of_block(i), 0, 0)),
                  pl.BlockSpec((None, 1, D_MODEL), lambda i: (l, 0, 0)),
                  pl.BlockSpec((None, D_MODEL, D_MODEL), lambda i: (l, 0, 0)),
                  whole((D_MODEL, LANES)), whole((D_MODEL, LANES))],
        out_specs=[row(D_MODEL), row(D_MODEL), pl.BlockSpec((N_EXPERTS, TM), lambda i: (0, i))],
        out_shape=[jax.ShapeDtypeStruct((N_TOK, D_MODEL), F32),
                   jax.ShapeDtypeStruct((N_TOK, D_MODEL), BF16),
                   jax.ShapeDtypeStruct((N_EXPERTS, N_TOK), F32)],
        scratch_shapes=[pltpu.VMEM((D_MODEL, D_MODEL), BF16)],
        compiler_params=_params("arbitrary"),
        name=f"outproj{l}",
    )(x_ctx, x_lat, mixed_ctx, mixed_f, mixed_w, mixed_n, mod4, g_ffn.reshape(DEPTH, 1, D_MODEL), w_out,
      r_hi, r_lo)


PREFIX_CHUNK = 256
MANTISSA_STEPS = 44


def _prefix_exclusive(m):
    rows, n = m.shape
    t0 = lax.broadcasted_iota(jnp.int32, (PREFIX_CHUNK, PREFIX_CHUNK), 0)
    t1 = lax.broadcasted_iota(jnp.int32, (PREFIX_CHUNK, PREFIX_CHUNK), 1)
    upper = jnp.where(t0 < t1, 1.0, 0.0).astype(BF16)
    carry = jnp.zeros((rows, 1), F32)
    outs = []
    for c in range(n // PREFIX_CHUNK):
        blk = m[:, c * PREFIX_CHUNK:(c + 1) * PREFIX_CHUNK]
        outs.append(_dot(blk.astype(BF16), upper) + carry)
        carry = carry + blk.sum(axis=-1, keepdims=True)
    return outs[0] if len(outs) == 1 else jnp.concatenate(outs, axis=-1)


ROUTE_STREAMS = ((0, BATCH, CAP_CTX), (N_CTX, DEC_BATCH, CAP_LAT))


def _route_kernel(lg_ref, *out_refs):
    affs, caps = [], []
    for t0, groups, cap in ROUTE_STREAMS:
        n = (N_CTX if t0 == 0 else N_LAT) // groups
        x = jnp.concatenate([lg_ref[:, t0 + g * n:t0 + (g + 1) * n] for g in range(groups)], axis=0)
        x = x.reshape(groups, N_EXPERTS, n)
        e = jnp.exp(x - x.max(axis=1, keepdims=True))
        affs.append((e / e.sum(axis=1, keepdims=True)).reshape(groups * N_EXPERTS, n))
        caps.append(float(cap))

    def count_ge(aff, t):
        return jnp.where(aff >= t, 1.0, 0.0).sum(axis=-1, keepdims=True)

    def keep_if_enough(aff, cap, cand, otherwise):
        return jnp.where(count_ge(aff, cand) >= cap, cand, otherwise)

    above = [jnp.full((aff.shape[0], 1), 2.0, F32) for aff in affs]
    for s in (64, 32, 16, 8, 4, 2, 1):
        cands = [a * (2.0 ** -s) for a in above]
        above = [jnp.where(count_ge(aff, c) >= cap, a, c) for aff, cap, a, c in zip(affs, caps, above, cands)]
    bases = [keep_if_enough(aff, cap, a * 0.5, 0.0) for aff, cap, a in zip(affs, caps, above)]

    def refine(_, carry):
        thrs, incs = carry
        thrs = tuple(keep_if_enough(aff, cap, t + i, t) for aff, cap, t, i in zip(affs, caps, thrs, incs))
        return thrs, tuple(i * 0.5 for i in incs)

    thrs, _ = lax.fori_loop(0, MANTISSA_STEPS, refine, (tuple(bases), tuple(b * 0.5 for b in bases)))

    for k, ((_, groups, _), aff, cap, thr) in enumerate(zip(ROUTE_STREAMS, affs, caps, thrs)):
        slot_ref, aff_ref, slott_ref = out_refs[3 * k:3 * k + 3]
        n = aff.shape[1]
        gt = jnp.where(aff > thr, 1.0, 0.0)
        eq = jnp.where(aff == thr, 1.0, 0.0)
        need = cap - gt.sum(axis=-1, keepdims=True)
        sel = gt + eq * jnp.where(_prefix_exclusive(eq) < need, 1.0, 0.0)
        slot = jnp.where(sel > 0.0, _prefix_exclusive(sel), -1.0)
        slot_ref[...] = slot.astype(jnp.int32)
        aff_ref[...] = aff
        unused = jnp.full((LANES - N_EXPERTS, n), -1.0, F32)
        for g in range(groups):
            tile = jnp.concatenate([slot[g * N_EXPERTS:(g + 1) * N_EXPERTS], unused], axis=0)
            slott_ref[g * n:(g + 1) * n, :] = tile.T.astype(jnp.int32)


def _route(lg_t):
    shapes = []
    for t0, groups, _ in ROUTE_STREAMS:
        n_tok = N_CTX if t0 == 0 else N_LAT
        rows, n = groups * N_EXPERTS, n_tok // groups
        shapes += [((rows, n), jnp.int32), ((rows, n), F32), ((n_tok, LANES), jnp.int32)]
    return pl.pallas_call(
        _route_kernel,
        grid=(1,),
        in_specs=[pl.BlockSpec((N_EXPERTS, N_TOK), lambda i: (0, 0))],
        out_specs=[pl.BlockSpec(shape, lambda i: (0, 0)) for shape, _ in shapes],
        out_shape=[jax.ShapeDtypeStruct(shape, dt) for shape, dt in shapes],
        compiler_params=_params("arbitrary"),
        name="route",
    )(lg_t)


CTX_GROUP = 4


def _gather_ctx_kernel(h_ref, slot_ref, aff_ref, x_ref, g_ref, p_ref):
    s_iota = lax.broadcasted_iota(jnp.int32, (CAP_CTX, SEQ), 0)
    for bb in range(CTX_GROUP):
        slots = slice(bb * CAP_CTX, (bb + 1) * CAP_CTX)
        for e in range(N_EXPERTS):
            row = bb * N_EXPERTS + e
            hit = s_iota == slot_ref[row:row + 1, :]
            p_ref[bb, e * CAP_CTX:(e + 1) * CAP_CTX, :] = jnp.where(hit, 1.0, 0.0).astype(BF16)
            g_ref[e, slots] = jnp.where(hit, aff_ref[row:row + 1, :], 0.0).sum(axis=-1, keepdims=True)
        x = _dot(p_ref[bb], h_ref[bb * SEQ:(bb + 1) * SEQ, :]).astype(BF16)
        x_ref[:, slots, :] = x.reshape(N_EXPERTS, CAP_CTX, D_MODEL)


def _gather_ctx(h, slot, aff):
    return pl.pallas_call(
        _gather_ctx_kernel,
        grid=(BATCH // CTX_GROUP,),
        in_specs=[pl.BlockSpec((CTX_GROUP * SEQ, D_MODEL), lambda b: (b, 0)),
                  pl.BlockSpec((CTX_GROUP * N_EXPERTS, SEQ), lambda b: (b, 0)),
                  pl.BlockSpec((CTX_GROUP * N_EXPERTS, SEQ), lambda b: (b, 0))],
        out_specs=[pl.BlockSpec((N_EXPERTS, CTX_GROUP * CAP_CTX, D_MODEL), lambda b: (0, b, 0)),
                   pl.BlockSpec((N_EXPERTS, CTX_GROUP * CAP_CTX, 1), lambda b: (0, b, 0))],
        out_shape=[jax.ShapeDtypeStruct((N_EXPERTS, ROWS_CTX, D_MODEL), BF16),
                   jax.ShapeDtypeStruct((N_EXPERTS, ROWS_CTX, 1), F32)],
        scratch_shapes=[pltpu.VMEM((CTX_GROUP, N_EXPERTS * CAP_CTX, SEQ), BF16)],
        compiler_params=_params("arbitrary"),
        name="gather_ctx",
    )(h, slot, aff)


LAT_GATHER_EXPERTS = 2


def _gather_lat_kernel(h_ref, slot_ref, aff_ref, x_ref, g_ref):
    s_iota = lax.broadcasted_iota(jnp.int32, (CAP_LAT, DEC_SEQ), 0)
    for k in range(LAT_GATHER_EXPERTS):
        e = pl.program_id(1) * LAT_GATHER_EXPERTS + k
        hit = s_iota == slot_ref[pl.ds(e, 1), :]
        x_ref[k] = _dot(jnp.where(hit, 1.0, 0.0).astype(BF16), h_ref[...]).astype(BF16)
        g_ref[k] = jnp.where(hit, aff_ref[pl.ds(e, 1), :], 0.0).sum(axis=-1, keepdims=True)


def _gather_lat(h, slot, aff):
    lat_blk0 = N_CTX // DEC_SEQ
    return pl.pallas_call(
        _gather_lat_kernel,
        grid=(DEC_BATCH, N_EXPERTS // LAT_GATHER_EXPERTS),
        in_specs=[pl.BlockSpec((DEC_SEQ, D_MODEL), lambda b, e: (lat_blk0 + b, 0)),
                  pl.BlockSpec((N_EXPERTS, DEC_SEQ), lambda b, e: (b, 0)),
                  pl.BlockSpec((N_EXPERTS, DEC_SEQ), lambda b, e: (b, 0))],
        out_specs=[pl.BlockSpec((LAT_GATHER_EXPERTS, CAP_LAT, D_MODEL), lambda b, e: (e, b, 0)),
                   pl.BlockSpec((LAT_GATHER_EXPERTS, CAP_LAT, 1), lambda b, e: (e, b, 0))],
        out_shape=[jax.ShapeDtypeStruct((N_EXPERTS, ROWS_LAT, D_MODEL), BF16),
                   jax.ShapeDtypeStruct((N_EXPERTS, ROWS_LAT, 1), F32)],
        compiler_params=_params("arbitrary", "arbitrary"),
        name="gather_lat",
    )(h, slot, aff)


N_FF_CHUNKS = D_FF // TF
CHUNKS_PER_STEP = 4
FF_BLOCK = CHUNKS_PER_STEP * TF
N_FF_STEPS = -(-N_FF_CHUNKS // CHUNKS_PER_STEP)
LAST_STEP_CHUNKS = N_FF_CHUNKS - (N_FF_STEPS - 1) * CHUNKS_PER_STEP


def _silu_tanh(x):
    return x * (0.5 + 0.5 * jnp.tanh(0.5 * x))


def _ffn_kernel(xc_ref, xl_ref, gc_ref, gl_ref, wg_ref, wu_ref, wd_ref, y_ref, x_sc, h_sc, acc_sc):
    j = pl.program_id(1)

    @pl.when(j == 0)
    def _():
        x_sc[0:ROWS_CTX, :] = xc_ref[...]
        x_sc[ROWS_CTX:, :] = xl_ref[...]

    def step(n_chunks, first, last):
        x = x_sc[...]
        for sub in range(n_chunks):
            cols = slice(sub * TF, (sub + 1) * TF)
            a = _dot(x, wg_ref[:, cols].astype(BF16))
            u = _dot(x, wu_ref[:, cols].astype(BF16))
            h_sc[sub] = (_silu_tanh(a) * u).astype(BF16)
        for c0 in range(0, D_MODEL, TD):
            out_cols = slice(c0, c0 + TD)
            part = None
            for sub in range(n_chunks):
                t = _dot(h_sc[sub], wd_ref[sub * TF:(sub + 1) * TF, out_cols].astype(BF16))
                part = t if part is None else part + t
            if not first:
                part = acc_sc[:, out_cols] + part
            if last:
                y_ref[0:ROWS_CTX, out_cols] = (part[0:ROWS_CTX] * gc_ref[...]).astype(BF16)
                y_ref[ROWS_CTX:, out_cols] = (part[ROWS_CTX:] * gl_ref[...]).astype(BF16)
            else:
                acc_sc[:, out_cols] = part

    pl.when(j == 0)(functools.partial(step, CHUNKS_PER_STEP, True, False))
    pl.when((j > 0) & (j < N_FF_STEPS - 1))(functools.partial(step, CHUNKS_PER_STEP, False, False))
    pl.when(j == N_FF_STEPS - 1)(functools.partial(step, LAST_STEP_CHUNKS, False, True))


def _ffn(l, x_c, x_l, g_c, g_l, w_gate, w_up, w_down):
    rows = ROWS_CTX + ROWS_LAT
    xin = lambda r: pl.BlockSpec((None, r, D_MODEL), lambda e, j: (e, 0, 0))
    gin = lambda r: pl.BlockSpec((None, r, 1), lambda e, j: (e, 0, 0))
    return pl.pallas_call(
        _ffn_kernel,
        grid=(N_EXPERTS, N_FF_STEPS),
        in_specs=[xin(ROWS_CTX), xin(ROWS_LAT), gin(ROWS_CTX), gin(ROWS_LAT),
                  pl.BlockSpec((None, None, D_MODEL, FF_BLOCK), lambda e, j: (l, e, 0, j)),
                  pl.BlockSpec((None, None, D_MODEL, FF_BLOCK), lambda e, j: (l, e, 0, j)),
                  pl.BlockSpec((None, None, FF_BLOCK, D_MODEL), lambda e, j: (l, e, j, 0))],
        out_specs=pl.BlockSpec((None, rows, D_MODEL), lambda e, j: (e, 0, 0)),
        out_shape=jax.ShapeDtypeStruct((N_EXPERTS, rows, D_MODEL), BF16),
        scratch_shapes=[pltpu.VMEM((rows, D_MODEL), BF16), pltpu.VMEM((CHUNKS_PER_STEP, rows, TF), BF16),
                        pltpu.VMEM((rows, D_MODEL), F32)],
        compiler_params=_params("arbitrary", "arbitrary"),
        name=f"experts{l}",
    )(x_c, x_l, g_c, g_l, w_gate, w_up, w_down)


def _finish(x, res, mod_ref, gf_ref, final):
    y = x + mod_ref[:, 5 * D_MODEL:] * res
    if final:
        y = y * lax.rsqrt(jnp.mean(y * y, axis=-1, keepdims=True) + RMS_EPS) * gf_ref[...]
    return y


def _combine_ctx_kernel(final, x_ref, y_ref, slot_ref, rep_ref, mod_ref, gf_ref, o_ref):
    n_col = N_EXPERTS * CAP_CTX
    col = (lax.broadcasted_iota(jnp.int32, (SEQ, n_col), 1) % CAP_CTX).astype(F32)
    for bb in range(CTX_GROUP):
        rows = slice(bb * SEQ, (bb + 1) * SEQ)
        spread = _dot(slot_ref[rows, :].astype(F32).astype(BF16), rep_ref[...])
        p = jnp.where(spread == col, 1.0, 0.0).astype(BF16)
        y = y_ref[:, bb * CAP_CTX:(bb + 1) * CAP_CTX, :].reshape(n_col, D_MODEL)
        o_ref[rows, :] = _finish(x_ref[rows, :], _dot(p, y), mod_ref, gf_ref, final)


def _combine_ctx(l, final, x_new, y, slot_t, mod4, g_final):
    n_col = N_EXPERTS * CAP_CTX
    rep = (np.arange(n_col)[None, :] // CAP_CTX == np.arange(LANES)[:, None]).astype(np.float32)
    return pl.pallas_call(
        functools.partial(_combine_ctx_kernel, final),
        grid=(BATCH // CTX_GROUP,),
        in_specs=[pl.BlockSpec((CTX_GROUP * SEQ, D_MODEL), lambda b: (b, 0)),
                  pl.BlockSpec((N_EXPERTS, CTX_GROUP * CAP_CTX, D_MODEL), lambda b: (0, b, 0)),
                  pl.BlockSpec((CTX_GROUP * SEQ, LANES), lambda b: (b, 0)),
                  pl.BlockSpec((LANES, n_col), lambda b: (0, 0)),
                  pl.BlockSpec((None, None, 1, 6 * D_MODEL), lambda b: (l, 0, 0, 0)),
                  pl.BlockSpec((1, D_MODEL), lambda b: (0, 0))],
        out_specs=pl.BlockSpec((CTX_GROUP * SEQ, D_MODEL), lambda b: (b, 0)),
        out_shape=jax.ShapeDtypeStruct((N_CTX, D_MODEL), F32),
        compiler_params=_params("arbitrary"),
        name=f"combine_ctx{l}",
    )(x_new, y, slot_t, jnp.asarray(rep, BF16), mod4, g_final.reshape(1, D_MODEL))


TMC = 512


def _combine_lat_kernel(final, x_ref, y_ref, slot_ref, mod_ref, gf_ref, o_ref):
    s_iota = lax.broadcasted_iota(jnp.int32, (TMC, CAP_LAT), 1)
    slot = slot_ref[...]
    res = None
    for e in range(N_EXPERTS):
        p = jnp.where(slot[:, e:e + 1] == s_iota, 1.0, 0.0).astype(BF16)
        t = _dot(p, y_ref[e])
        res = t if res is None else res + t
    o_ref[...] = _finish(x_ref[...], res, mod_ref, gf_ref, final)


def _combine_lat(l, final, x_new, y, slot_t, mod4, g_final):
    nt = DEC_SEQ // TMC
    return pl.pallas_call(
        functools.partial(_combine_lat_kernel, final),
        grid=(DEC_BATCH, nt),
        in_specs=[pl.BlockSpec((TMC, D_MODEL), lambda b, t: (N_CTX // TMC + b * nt + t, 0)),
                  pl.BlockSpec((N_EXPERTS, CAP_LAT, D_MODEL), lambda b, t: (0, ROWS_CTX // CAP_LAT + b, 0)),
                  pl.BlockSpec((TMC, LANES), lambda b, t: (b * nt + t, 0)),
                  pl.BlockSpec((None, None, 1, 6 * D_MODEL), lambda b, t: (l, 1 + b, 0, 0)),
                  pl.BlockSpec((1, D_MODEL), lambda b, t: (0, 0))],
        out_specs=pl.BlockSpec((TMC, D_MODEL), lambda b, t: (b * nt + t, 0)),
        out_shape=jax.ShapeDtypeStruct((N_LAT, D_MODEL), F32),
        compiler_params=_params("arbitrary", "arbitrary"),
        name=f"combine_lat{l}",
    )(x_new, y, slot_t, mod4, g_final.reshape(1, D_MODEL))


def _split_table(t):
    hi = t.astype(BF16)
    return hi, (t - hi.astype(F32)).astype(BF16)


def _dft_tables(n):
    p = np.arange(n, dtype=np.int64)
    ang = ((p[:, None] * p[None, :]) % n).astype(np.float64) * (2.0 * np.pi / n)
    return np.cos(ang).astype(np.float32), np.sin(ang).astype(np.float32)


def _channel_dft_tables():
    c = np.arange(F_WIDTH, dtype=np.int64)
    same = (c[:, None] // HEAD_DIM) == (c[None, :] // HEAD_DIM)
    ang = (((c[:, None] % HEAD_DIM) * (c[None, :] % HEAD_DIM)) % HEAD_DIM).astype(np.float64) * (2.0 * np.pi / HEAD_DIM)
    return (np.where(same, np.cos(ang), 0.0).astype(np.float32),
            np.where(same, np.sin(ang), 0.0).astype(np.float32))


def _rope_tables():
    half = HEAD_DIM // 2
    nf = half // 2
    pos = np.arange(DEC_SEQ)
    inv = 1.0 / (ROPE_BASE ** (np.arange(nf, dtype=np.float64) / nf))
    ang_r = (pos // GRID_W).astype(np.float64)[:, None] * inv
    ang_c = (pos % GRID_W).astype(np.float64)[:, None] * inv

    def head(fn, sign):
        return np.concatenate([sign * fn(ang_r), fn(ang_r), sign * fn(ang_c), fn(ang_c)], axis=-1)

    cos = head(np.cos, 1.0)
    sin = head(np.sin, -1.0)
    return (np.concatenate([cos, cos], axis=-1).astype(np.float32),
            np.concatenate([sin, sin], axis=-1).astype(np.float32))


def _na_bias_tables(rpb):
    cq = np.arange(GRID_W)
    rel_c = np.clip(cq[None, :] - cq[:, None] + NA_COLS - 1, 0, 2 * NA_COLS - 2)
    pick = (rel_c[:, :, None] == np.arange(2 * NA_COLS - 1)).astype(np.float32)
    cs = np.clip(cq - NA_COLS // 2, 0, GRID_W - NA_COLS)
    col_ok = (cq[None, :] >= cs[:, None]) & (cq[None, :] < cs[:, None] + NA_COLS)
    bc = jnp.einsum('lhrj,qkj->lhrqk', rpb, pick, precision=lax.Precision.HIGHEST)
    bc = jnp.where(col_ok[None, None, None], bc, NEG_INF)
    bc = jnp.pad(bc, ((0, 0), (0, 0), (1, 1), (0, 0), (0, 0)), constant_values=NEG_INF)
    bc = jnp.concatenate([bc[:, :, :-1], bc[:, :, 1:]], axis=-1)
    return pl.pallas_call(
        _na_bias_kernel,
        grid=(DEPTH, NA_VARIANTS, NA_HEADS // 2),
        in_specs=[pl.BlockSpec((None, 2, 2 * NA_ROWS, GRID_W, 2 * GRID_W), lambda l, v, p: (l, p, 0, 0, 0))],
        out_specs=pl.BlockSpec((None, None, None, 2 * NA_Q, NA_WIN_ROWS * GRID_W),
                               lambda l, v, p: (l, v, p, 0, 0)),
        out_shape=jax.ShapeDtypeStruct((DEPTH, NA_VARIANTS, NA_HEADS // 2, 2 * NA_Q, NA_WIN_ROWS * GRID_W), F32),
        compiler_params=_params("arbitrary", "arbitrary", "arbitrary"),
        name="na_bias",
    )(bc)


def _na_window_plan():
    plan = []
    for g in (0, 1, NA_BLOCKS - 1):
        start = int(np.clip(NA_G * g - NA_ROWS // 2, 0, GRID_ROWS - NA_WIN_ROWS))
        rows = []
        for a in range(NA_G):
            r = NA_G * g + a
            rs = int(np.clip(r - NA_ROWS // 2, 0, GRID_ROWS - NA_ROWS))
            rows.append([start + w - r + NA_ROWS - 1 if rs <= start + w < rs + NA_ROWS else None
                         for w in range(NA_WIN_ROWS)])
        plan.append(rows)
    return plan


def _na_bias_kernel(bc_ref, o_ref):
    outside = jnp.full((GRID_W, 2 * GRID_W), NEG_INF, F32)
    low = _lane_is_low(outside.shape)

    def pair_tile(half, ra, rb):
        if ra is None and rb is None:
            return outside
        if rb is None:
            return jnp.where(low, bc_ref[half, ra + 1], NEG_INF)
        if ra is None:
            return jnp.where(low, NEG_INF, bc_ref[half, rb])
        return bc_ref[half, ra + 1]

    for v, rows in enumerate(_na_window_plan()):
        @pl.when(pl.program_id(1) == v)
        def _():
            for half in range(2):
                for a, rel in enumerate(rows):
                    tiles = [pair_tile(half, rel[w], rel[w + 1]) for w in range(0, NA_WIN_ROWS, 2)]
                    r0 = half * NA_Q + a * GRID_W
                    o_ref[r0:r0 + GRID_W, :] = jnp.concatenate(tiles, axis=-1)


def kernel(x_prompt, x_sample, cache_win_k, cache_win_v, cache_nat_k, cache_nat_v, c, c_ctx, w_mod, b_mod, g_mix, g_ffn, w_in, w_out, win_sink, nat_rpb, w_router, w_gate, w_up, w_down, g_final):
    x_ctx = x_prompt.reshape(N_CTX, D_MODEL)
    x_lat = x_sample.reshape(N_LAT, D_MODEL)
    cond = jnp.concatenate([c_ctx[None, :], c, jnp.zeros((N_COND - 1 - DEC_BATCH, D_MODEL), F32)], axis=0)
    mod4 = _adaln(cond, w_mod, b_mod).reshape(DEPTH, N_COND, 1, 6 * D_MODEL)

    dft_ch = _channel_dft_tables()
    dft_ctx = _dft_tables(SEQ)
    dft_lat = _dft_tables(DEC_SEQ)
    cos_t, sin_t = _rope_tables()
    cwk = cache_win_k.reshape(DEC_BATCH, DEPTH, PAST_LEN, WIN_KV)
    cwv = cache_win_v.reshape(DEC_BATCH, DEPTH, PAST_LEN, WIN_KV)
    cnk = cache_nat_k.reshape(DEC_BATCH, DEPTH, PAST_LEN, NA_W)
    cnv = cache_nat_v.reshape(DEC_BATCH, DEPTH, PAST_LEN, NA_W)
    r_pad = jnp.pad(w_router, ((0, 0), (0, 0), (0, LANES - N_EXPERTS)))
    na_bias = _na_bias_tables(nat_rpb)

    kvt = ()
    for l in range(DEPTH):
        final = l == DEPTH - 1
        (f, qw, qn), (kw, vw, kn, vn), kvt = _project(l, x_ctx, x_lat, mod4, g_mix, w_in, kvt)

        mixed_ctx = _ctx_mixer(l, win_sink, f, qw, qn, kvt, dft_ch, dft_ctx)
        mixed_f = _lat_fourier(f, dft_ch, dft_lat)
        mixed_w = _lat_window(l, win_sink, qw, kw, vw, cwk, cwv, cos_t, sin_t)
        mixed_n = _lat_neighbourhood(l, qn, kn, vn, cnk, cnv, na_bias)

        r_hi, r_lo = _split_table(r_pad[l])
        x_new, h, lg_t = _outproj(l, x_ctx, x_lat, mixed_ctx, mixed_f, mixed_w, mixed_n, mod4, g_ffn,
                                  w_out, r_hi, r_lo)

        slot_c, aff_c, slot_ct, slot_l, aff_l, slot_lt = _route(lg_t)
        xg_c, gate_c = _gather_ctx(h, slot_c, aff_c)
        xg_l, gate_l = _gather_lat(h, slot_l, aff_l)
        y = _ffn(l, xg_c, xg_l, gate_c, gate_l, w_gate, w_up, w_down)
        x_ctx = _combine_ctx(l, final, x_new, y, slot_ct, mod4, g_final)
        x_lat = _combine_lat(l, final, x_new, y, slot_lt, mod4, g_final)

    y_prompt = x_ctx.reshape(BATCH, SEQ, D_MODEL)
    y_sample = x_lat.reshape(DEC_BATCH, DEC_SEQ, D_MODEL)
    new_kv = [t.reshape(BATCH, DEPTH, w // HEAD_DIM, HEAD_DIM, SEQ).transpose(0, 1, 4, 2, 3)
              for t, w in zip(kvt, KV_WIDTHS)]
    return (y_prompt, y_sample, *new_kv)
```

```python
import functools

import numpy as np
import jax
import jax.numpy as jnp
from jax import lax
from jax.experimental import pallas as pl
from jax.experimental.pallas import tpu as pltpu

D_MODEL = 1024
BATCH = 16
SEQ = 256
DEPTH = 2
DEC_BATCH = 2
DEC_SEQ = 2048
PAST_LEN = 256
GRID_W = 64
HEAD_DIM = 64
F_WIDTH = 256
WIN_HEADS = 6
WIN_KV_HEADS = 2
WINDOW = 128
NA_HEADS = 6
NA_ROWS = 8
NA_COLS = 16
N_EXPERTS = 16
EC_CAPACITY = 2
D_FF = 2816
ROPE_BASE = 10000.0
RMS_EPS = 1e-6
NEG_INF = -1e30
ATTN_SCALE = HEAD_DIM ** -0.5
WIN_Q = WIN_HEADS * HEAD_DIM
WIN_KV = WIN_KV_HEADS * HEAD_DIM
NA_W = NA_HEADS * HEAD_DIM
N_IN = F_WIDTH + WIN_Q + 2 * WIN_KV + 3 * NA_W
SPLITS = (0, F_WIDTH, F_WIDTH + WIN_Q, F_WIDTH + WIN_Q + WIN_KV, F_WIDTH + WIN_Q + 2 * WIN_KV,
          F_WIDTH + WIN_Q + 2 * WIN_KV + NA_W, F_WIDTH + WIN_Q + 2 * WIN_KV + 2 * NA_W, N_IN)

N_CTX = BATCH * SEQ
N_LAT = DEC_BATCH * DEC_SEQ
N_TOK = N_CTX + N_LAT
GRID_ROWS = DEC_SEQ // GRID_W
CAP_CTX = EC_CAPACITY * SEQ // N_EXPERTS
CAP_LAT = EC_CAPACITY * DEC_SEQ // N_EXPERTS
ROWS_CTX = BATCH * CAP_CTX
ROWS_LAT = DEC_BATCH * CAP_LAT
N_COND = 8

LANES = 128
MXU_COLS = 256
TM = 512
TN_MOD = 1536
TF = 256
TD = 256
VMEM_LIMIT = 56 * 1024 * 1024

F32 = jnp.float32
BF16 = jnp.bfloat16


def _params(*sem):
    return pltpu.CompilerParams(dimension_semantics=sem, vmem_limit_bytes=VMEM_LIMIT)


def _dot(a, b):
    return jnp.dot(a, b, preferred_element_type=F32)


def _dot_nt(a, b):
    return lax.dot_general(a, b, (((1,), (1,)), ((), ())), preferred_element_type=F32)


def _split(x):
    hi = x.astype(BF16)
    lo = (x - hi.astype(F32)).astype(BF16)
    return hi, lo


def _dot3(a_hi, a_lo, b_hi, b_lo):
    return _dot(a_hi, b_hi) + (_dot(a_lo, b_hi) + _dot(a_hi, b_lo))


def _silu(x):
    return x / (1.0 + jnp.exp(-x))


def _rms_mod(x, g, shift, scale):
    y = x * lax.rsqrt(jnp.mean(x * x, axis=-1, keepdims=True) + RMS_EPS)
    return (y * g) * (1.0 + scale) + shift


def _softmax_parts(parts, sink=None):
    m = parts[0].max(axis=-1, keepdims=True)
    for s in parts[1:]:
        m = jnp.maximum(m, s.max(axis=-1, keepdims=True))
    if sink is not None:
        m = jnp.maximum(m, sink)
    es = [jnp.exp(s - m) for s in parts]
    den = es[0].sum(axis=-1, keepdims=True)
    for e in es[1:]:
        den = den + e.sum(axis=-1, keepdims=True)
    if sink is not None:
        den = den + jnp.exp(sink - m)
    inv = 1.0 / den
    return [e * inv for e in es]


def _cond_of_block(i):
    n_ctx_blocks = N_CTX // TM
    return jnp.where(i < n_ctx_blocks, 0, 1 + (i - n_ctx_blocks) // (DEC_SEQ // TM))


def _two_stream_specs(width):
    n_ctx_blocks = N_CTX // TM
    ctx = pl.BlockSpec((TM, width), lambda i: (jnp.minimum(i, n_ctx_blocks - 1), 0))
    lat = pl.BlockSpec((TM, width), lambda i: (jnp.maximum(i - n_ctx_blocks, 0), 0))
    return ctx, lat


def _pick_stream(ctx_ref, lat_ref):
    return jnp.where(pl.program_id(0) < N_CTX // TM, ctx_ref[...], lat_ref[...])


def _adaln_kernel(c_ref, w_ref, b_ref, o_ref):
    s_hi, s_lo = _split(_silu(c_ref[...]))
    w_hi, w_lo = _split(w_ref[...])
    o_ref[...] = _dot3(s_hi, s_lo, w_hi, w_lo) + b_ref[...]


def _adaln(cond, w_mod, b_mod):
    return pl.pallas_call(
        _adaln_kernel,
        grid=(DEPTH, 6 * D_MODEL // TN_MOD),
        in_specs=[
            pl.BlockSpec((N_COND, D_MODEL), lambda l, j: (0, 0)),
            pl.BlockSpec((None, D_MODEL, TN_MOD), lambda l, j: (l, 0, j)),
            pl.BlockSpec((None, 1, TN_MOD), lambda l, j: (l, 0, j)),
        ],
        out_specs=pl.BlockSpec((None, N_COND, TN_MOD), lambda l, j: (l, 0, j)),
        out_shape=jax.ShapeDtypeStruct((DEPTH, N_COND, 6 * D_MODEL), F32),
        compiler_params=_params("arbitrary", "arbitrary"),
        name="adaln",
    )(cond, w_mod, b_mod.reshape(DEPTH, 1, 6 * D_MODEL))


KV_NAMES = ("kw", "vw", "kn", "vn")
KV_WIDTHS = (WIN_KV, WIN_KV, NA_W, NA_W)
KV_SPLITS = (SPLITS[2], SPLITS[3], SPLITS[5], SPLITS[6])
KV_TOTAL = sum(KV_WIDTHS)
Q_SPLITS = (SPLITS[0], SPLITS[1], SPLITS[4])


def _proj_kernel(n_prev, xc_ref, xl_ref, mod_ref, g_ref, w_ref, *rest):
    rest = rest[n_prev:]
    f_ref, qw_ref, qn_ref = rest[0:3]
    lat_kv = rest[3:7]
    ctx_kvt = rest[7:11]
    wb_ref, wt_ref = rest[11:13]
    i = pl.program_id(0)

    @pl.when(i == 0)
    def _():
        wb_ref[...] = w_ref[...].astype(BF16)
        r0 = 0
        for c0, width in zip(KV_SPLITS, KV_WIDTHS):
            wt_ref[r0:r0 + width, :] = w_ref[:, c0:c0 + width].T.astype(BF16)
            r0 += width

    natural = dict(zip(SPLITS[:-1], (f_ref, qw_ref) + tuple(lat_kv[0:2]) + (qn_ref,) + tuple(lat_kv[2:4])))

    def hidden(x_ref):
        return _rms_mod(x_ref[...], g_ref[...], mod_ref[:, 0:D_MODEL], mod_ref[:, D_MODEL:2 * D_MODEL]).astype(BF16)

    def project_tiles(h, wanted):
        for t0 in range(0, N_IN, MXU_COLS):
            hits = [(a, b) for a, b in zip(SPLITS[:-1], SPLITS[1:])
                    if a in wanted and max(a, t0) < min(b, t0 + MXU_COLS)]
            if not hits:
                continue
            acc = _dot(h, wb_ref[:, t0:t0 + MXU_COLS])
            for a, b in hits:
                lo, hi = max(a, t0), min(b, t0 + MXU_COLS)
                natural[a][:, lo - a:hi - a] = acc[:, lo - t0:hi - t0].astype(natural[a].dtype)

    @pl.when(i < N_CTX // TM)
    def _():
        h = hidden(xc_ref)
        project_tiles(h, Q_SPLITS)
        kvt = _dot_nt(wt_ref[...], h)
        r0 = 0
        for o_ref, width in zip(ctx_kvt, KV_WIDTHS):
            for bb in range(TM // SEQ):
                o_ref[bb] = kvt[r0:r0 + width, bb * SEQ:(bb + 1) * SEQ]
            r0 += width

    @pl.when(i >= N_CTX // TM)
    def _():
        project_tiles(hidden(xl_ref), Q_SPLITS + KV_SPLITS)


def _project(l, x_ctx, x_lat, mod4, g_mix, w_in, prev_kvt):
    n_ctx_blocks = N_CTX // TM
    xc_spec, xl_spec = _two_stream_specs(D_MODEL)
    both = lambda w: pl.BlockSpec((TM, w), lambda i: (i, 0))
    lat = lambda w: pl.BlockSpec((TM, w), lambda i: (jnp.maximum(i - n_ctx_blocks, 0), 0))
    ctx_t = lambda w: pl.BlockSpec((TM // SEQ, None, w, SEQ),
                                   lambda i: (jnp.minimum(i, n_ctx_blocks - 1), l, 0, 0))
    n_prev = len(prev_kvt)
    n_in = 5
    outs = pl.pallas_call(
        functools.partial(_proj_kernel, n_prev),
        grid=(N_TOK // TM,),
        in_specs=[
            xc_spec, xl_spec,
            pl.BlockSpec((None, None, 1, 6 * D_MODEL), lambda i: (l, _cond_of_block(i), 0, 0)),
            pl.BlockSpec((None, 1, D_MODEL), lambda i: (l, 0, 0)),
            pl.BlockSpec((None, D_MODEL, N_IN), lambda i: (l, 0, 0)),
        ] + [pl.BlockSpec(memory_space=pl.ANY)] * n_prev,
        out_specs=[both(F_WIDTH), both(WIN_Q), both(NA_W)] + [lat(w) for w in KV_WIDTHS]
                  + [ctx_t(w) for w in KV_WIDTHS],
        out_shape=[jax.ShapeDtypeStruct((N_TOK, w), dt) for w, dt in ((F_WIDTH, F32), (WIN_Q, F32), (NA_W, BF16))]
                  + [jax.ShapeDtypeStruct((N_LAT, w), dt)
                     for w, dt in zip(KV_WIDTHS, (F32, BF16, BF16, BF16))]
                  + [jax.ShapeDtypeStruct((BATCH, DEPTH, w, SEQ), F32) for w in KV_WIDTHS],
        input_output_aliases={n_in + k: 7 + k for k in range(n_prev)},
        scratch_shapes=[pltpu.VMEM((D_MODEL, N_IN), BF16), pltpu.VMEM((KV_TOTAL, D_MODEL), BF16)],
        compiler_params=_params("arbitrary"),
        name=f"project{l}",
    )(x_ctx, x_lat, mod4, g_mix.reshape(DEPTH, 1, D_MODEL), w_in, *prev_kvt)
    return outs[0:3], outs[3:7], outs[7:11]


def _lane_is_low(shape):
    return lax.broadcasted_iota(jnp.int32, shape, len(shape) - 1) < HEAD_DIM


def _swap_halves(x):
    return pltpu.roll(x, HEAD_DIM, axis=x.ndim - 1)


def _win_kv_copy(h):
    return 0 if (h // (WIN_HEADS // WIN_KV_HEADS)) == (h % 2) else 1


def _stack_heads(q_pairs, heads):
    low = _lane_is_low(q_pairs[heads[0] // 2].shape)
    rows = [jnp.where(low if h % 2 == 0 else jnp.logical_not(low), q_pairs[h // 2], 0.0).astype(BF16)
            for h in heads]
    return rows[0] if len(rows) == 1 else jnp.concatenate(rows, axis=0)


def _per_head_column(values, rows_per_head):
    blk = lax.broadcasted_iota(jnp.int32, (len(values) * rows_per_head, 1), 0) // rows_per_head
    col = jnp.full(blk.shape, values[0], F32)
    for i in range(1, len(values)):
        col = jnp.where(blk == i, values[i], col)
    return col


def _attend(q_stack, kv_list, extra_logit=None, transposed=False):
    scores = []
    for k, _, post in kv_list:
        s = _dot(q_stack, k) if transposed else _dot_nt(q_stack, k)
        scores.append(post(s) if post is not None else s)
    probs = _softmax_parts(scores, extra_logit)
    o = None
    for p, (_, v, _) in zip(probs, kv_list):
        t = _dot_nt(p.astype(BF16), v) if transposed else _dot(p.astype(BF16), v)
        o = t if o is None else o + t
    return o


def _merge_pair(o_even, o_odd):
    return jnp.where(_lane_is_low(o_even.shape), o_even, o_odd)


def _gqa_attention(q_pairs, rows, kv_for_copy, sinks, transposed=False):
    per_head = {}
    for copy in (0, 1):
        heads = [h for h in range(WIN_HEADS) if _win_kv_copy(h) == copy]
        o = _attend(_stack_heads(q_pairs, heads), kv_for_copy(copy),
                    _per_head_column([sinks[h] for h in heads], rows), transposed)
        for i, h in enumerate(heads):
            per_head[h] = o[i * rows:(i + 1) * rows]
    return [_merge_pair(per_head[2 * j], per_head[2 * j + 1]) for j in range(WIN_HEADS // 2)]


def _swap_row_halves(x):
    return jnp.concatenate([x[HEAD_DIM:], x[:HEAD_DIM]], axis=0)


MIX_GROUP = 2


def _ctx_mixer_kernel(l, sink_ref, f_ref, qw_ref, qn_ref, kw_ref, vw_ref, kn_ref, vn_ref,
                      bc_ref, bs_ref, cs_ref, ss_ref, o_ref):
    bc_hi, bc_lo = _split(bc_ref[...])
    bs_hi, bs_lo = _split(bs_ref[...])
    cs_hi, cs_lo = _split(cs_ref[...])
    ss_hi, ss_lo = _split(ss_ref[...])
    for bb in range(MIX_GROUP):
        rows = slice(bb * SEQ, (bb + 1) * SEQ)
        f_hi, f_lo = _split(f_ref[rows, :])
        fc_hi, fc_lo = _split(_dot3(f_hi, f_lo, bc_hi, bc_lo))
        fs_hi, fs_lo = _split(_dot3(f_hi, f_lo, bs_hi, bs_lo))
        z = _dot3(cs_hi, cs_lo, fc_hi, fc_lo) - _dot3(ss_hi, ss_lo, fs_hi, fs_lo)
        o_ref[rows, 0:F_WIDTH] = (z * (SEQ * HEAD_DIM) ** -0.5).astype(BF16)

        kv = [(kw_ref[bb].astype(BF16), vw_ref[bb].astype(BF16)),
              (_swap_row_halves(kw_ref[bb]).astype(BF16), _swap_row_halves(vw_ref[bb]).astype(BF16))]
        q_pairs = [qw_ref[rows, LANES * j:LANES * (j + 1)] * ATTN_SCALE for j in range(WIN_HEADS // 2)]
        outs = _gqa_attention(q_pairs, SEQ, lambda c: [(kv[c][0], kv[c][1], None)],
                              [sink_ref[l, h] for h in range(WIN_HEADS)], transposed=True)
        for j, o in enumerate(outs):
            o_ref[rows, F_WIDTH + LANES * j:F_WIDTH + LANES * (j + 1)] = o.astype(BF16)

        for j in range(NA_HEADS // 2):
            sl = slice(LANES * j, LANES * (j + 1))
            q_pairs = {j: qn_ref[rows, sl] * ATTN_SCALE}
            o = _attend(_stack_heads(q_pairs, (2 * j, 2 * j + 1)),
                        [(kn_ref[bb, sl, :].astype(BF16), vn_ref[bb, sl, :].astype(BF16), None)], transposed=True)
            base = F_WIDTH + WIN_Q + LANES * j
            o_ref[rows, base:base + LANES] = _merge_pair(o[:SEQ], o[SEQ:]).astype(BF16)


def _ctx_mixer(l, win_sink, f, qw, qn, kvt, dft_ch, dft_seq):
    row = lambda w: pl.BlockSpec((MIX_GROUP * SEQ, w), lambda b: (b, 0))
    col = lambda w: pl.BlockSpec((MIX_GROUP, None, w, SEQ), lambda b: (b, l, 0, 0))
    const = lambda n: pl.BlockSpec((n, n), lambda b: (0, 0))
    return pl.pallas_call(
        functools.partial(_ctx_mixer_kernel, l),
        grid=(BATCH // MIX_GROUP,),
        in_specs=[pl.BlockSpec(memory_space=pltpu.SMEM),
                  row(F_WIDTH), row(WIN_Q), row(NA_W)] + [col(w) for w in KV_WIDTHS]
                 + [const(F_WIDTH)] * 2 + [const(SEQ)] * 2,
        out_specs=pl.BlockSpec((MIX_GROUP * SEQ, D_MODEL), lambda b: (b, 0)),
        out_shape=jax.ShapeDtypeStruct((N_CTX, D_MODEL), BF16),
        compiler_params=_params("arbitrary"),
        name=f"ctx_mixer{l}",
    )(win_sink, f, qw, qn, *kvt, *dft_ch, *dft_seq)


FT_ROWS = 512


HALF_SEQ = DEC_SEQ // 2
SUBLANES = 8


def _lat_fourier_kernel(f_ref, mir_ref, bc_ref, bs_ref, cs_ref, ss_ref, o_ref, st_ref, mid_ref):
    r, b = pl.program_id(0), pl.program_id(1)

    @pl.when(r == 0)
    def _():
        f_hi, f_lo = _split(f_ref[pl.ds(pl.multiple_of(b * DEC_SEQ, DEC_SEQ), DEC_SEQ), :])
        fc = _dot3(f_hi, f_lo, *_split(bc_ref[...]))
        fs = _dot3(f_hi, f_lo, *_split(bs_ref[...]))

        def mirrored(t):
            hi, lo = _split(t[HALF_SEQ:])
            return _dot(mir_ref[...], hi) + _dot(mir_ref[...], lo)

        st_ref[b, 0], st_ref[b, 1] = _split(fc[:HALF_SEQ] + mirrored(fc))
        st_ref[b, 2], st_ref[b, 3] = _split(fs[:HALF_SEQ] - mirrored(fs))
        mid_ref[b] = fc[HALF_SEQ:HALF_SEQ + SUBLANES]

    z = (_dot3(*_split(cs_ref[...]), st_ref[b, 0], st_ref[b, 1])
         - _dot3(*_split(ss_ref[...]), st_ref[b, 2], st_ref[b, 3]))
    k = r * FT_ROWS + lax.broadcasted_iota(jnp.int32, (FT_ROWS, 1), 0)
    z = z + jnp.where(k % 2 == 0, 1.0, -1.0) * mid_ref[b, 0:1, :]
    o_ref[...] = (z * (DEC_SEQ * HEAD_DIM) ** -0.5).astype(BF16)


def _lat_fourier(f, dft_ch, dft_seq):
    nrb = DEC_SEQ // FT_ROWS
    p = np.arange(HALF_SEQ)
    mirror = ((p[:, None] + p[None, :] == HALF_SEQ) & (p[:, None] > 0)).astype(np.float32)
    const = pl.BlockSpec((F_WIDTH, F_WIDTH), lambda r, b: (0, 0))
    rows = pl.BlockSpec((FT_ROWS, HALF_SEQ), lambda r, b: (r, 0))
    return pl.pallas_call(
        _lat_fourier_kernel,
        grid=(nrb, DEC_BATCH),
        in_specs=[pl.BlockSpec((N_LAT, F_WIDTH), lambda r, b: (1, 0)),
                  pl.BlockSpec((HALF_SEQ, HALF_SEQ), lambda r, b: (0, 0))] + [const] * 2 + [rows] * 2,
        out_specs=pl.BlockSpec((FT_ROWS, F_WIDTH), lambda r, b: (b * nrb + r, 0)),
        out_shape=jax.ShapeDtypeStruct((N_LAT, F_WIDTH), BF16),
        scratch_shapes=[pltpu.VMEM((DEC_BATCH, 4, HALF_SEQ, F_WIDTH), BF16),
                        pltpu.VMEM((DEC_BATCH, SUBLANES, F_WIDTH), F32)],
        compiler_params=_params("arbitrary", "arbitrary"),
        name="lat_fourier",
    )(f, jnp.asarray(mirror, BF16), *dft_ch, *dft_seq)


def _rope(x, cos, sin_signed):
    n = x.shape[-1]
    lane = lax.broadcasted_iota(jnp.int32, x.shape, x.ndim - 1)
    first = (lane % 32) < 16
    partner = jnp.where(first, pltpu.roll(x, n - 16, axis=x.ndim - 1), pltpu.roll(x, 16, axis=x.ndim - 1))
    return x * cos + partner * sin_signed


WIN_QUERIES = 256
WIN_KEYS = WIN_QUERIES + 2 * WINDOW


def _win_kernel(l, sink_ref, q_ref, k_ref, v_ref, ck_ref, cv_ref, cos_ref, sin_ref, cosq_ref, sinq_ref,
                o_ref, kp_ref, vp_ref, cp_ref):
    n = pl.program_id(1)
    nb = DEC_SEQ // WIN_QUERIES
    pad = WINDOW

    @pl.when(n == 0)
    def _():
        zeros = jnp.zeros((pad, LANES), BF16)
        kr = _rope(k_ref[...], cos_ref[...], sin_ref[...])
        v = v_ref[...]
        for idx, (kk, vv) in enumerate(((kr, v), (_swap_halves(kr), _swap_halves(v)))):
            kp_ref[idx, 0:pad] = zeros
            kp_ref[idx, pad + DEC_SEQ:] = zeros
            kp_ref[idx, pad:pad + DEC_SEQ] = kk.astype(BF16)
            vp_ref[idx, 0:pad] = zeros
            vp_ref[idx, pad + DEC_SEQ:] = zeros
            vp_ref[idx, pad:pad + DEC_SEQ] = vv.astype(BF16)
        ck = ck_ref[...]
        cv = cv_ref[...]
        cp_ref[0] = ck.astype(BF16)
        cp_ref[1] = _swap_halves(ck).astype(BF16)
        cp_ref[2] = cv.astype(BF16)
        cp_ref[3] = _swap_halves(cv).astype(BF16)

    lo = jnp.where(n == 0, WINDOW, 0)
    hi = jnp.where(n == nb - 1, WIN_QUERIES + WINDOW, WIN_KEYS)

    i = lax.broadcasted_iota(jnp.int32, (WIN_QUERIES, WIN_KEYS), 0)
    j = lax.broadcasted_iota(jnp.int32, (WIN_QUERIES, WIN_KEYS), 1)
    mask = (j >= i) & (j <= i + 2 * WINDOW) & (j >= lo) & (j < hi)
    band_bias = jnp.where(mask, 0.0, NEG_INF)

    def band(s):
        heads = s.shape[0] // WIN_QUERIES
        return (s.reshape(heads, WIN_QUERIES, s.shape[1]) + band_bias[None]).reshape(s.shape)

    start = pl.multiple_of(n * WIN_QUERIES, WIN_QUERIES)
    win = pl.ds(start, WIN_KEYS)
    q_pairs = [_rope(q_ref[:, LANES * jp:LANES * (jp + 1)], cosq_ref[...], sinq_ref[...]) * ATTN_SCALE
               for jp in range(WIN_HEADS // 2)]
    outs = _gqa_attention(
        q_pairs, WIN_QUERIES,
        lambda c: [(kp_ref[c, win, :], vp_ref[c, win, :], band), (cp_ref[c], cp_ref[2 + c], None)],
        [sink_ref[l, h] for h in range(WIN_HEADS)])
    for jp, o in enumerate(outs):
        o_ref[:, LANES * jp:LANES * (jp + 1)] = o.astype(BF16)


def _lat_window(l, win_sink, qw, kw, vw, cache_k, cache_v, cos_t, sin_t):
    nb = DEC_SEQ // WIN_QUERIES
    kv_spec = pl.BlockSpec((DEC_SEQ, WIN_KV), lambda b, n: (b, 0))
    cache_spec = pl.BlockSpec((None, None, PAST_LEN, WIN_KV), lambda b, n: (b, l, 0, 0))
    tab_all = pl.BlockSpec((DEC_SEQ, LANES), lambda b, n: (0, 0))
    tab_blk = pl.BlockSpec((WIN_QUERIES, LANES), lambda b, n: (n, 0))
    return pl.pallas_call(
        functools.partial(_win_kernel, l),
        grid=(DEC_BATCH, nb),
        in_specs=[pl.BlockSpec(memory_space=pltpu.SMEM),
                  pl.BlockSpec((WIN_QUERIES, WIN_Q), lambda b, n: (N_CTX // WIN_QUERIES + b * nb + n, 0)),
                  kv_spec, kv_spec, cache_spec, cache_spec, tab_all, tab_all, tab_blk, tab_blk],
        out_specs=pl.BlockSpec((WIN_QUERIES, WIN_Q), lambda b, n: (b * nb + n, 0)),
        out_shape=jax.ShapeDtypeStruct((N_LAT, WIN_Q), BF16),
        scratch_shapes=[pltpu.VMEM((2, DEC_SEQ + 2 * WINDOW, LANES), BF16),
                        pltpu.VMEM((2, DEC_SEQ + 2 * WINDOW, LANES), BF16),
                        pltpu.VMEM((4, PAST_LEN, LANES), BF16)],
        compiler_params=_params("arbitrary", "arbitrary"),
        name=f"lat_window{l}",
    )(win_sink, qw, kw, vw, cache_k, cache_v, cos_t, sin_t, cos_t, sin_t)


NA_G = 4
NA_Q = NA_G * GRID_W
NA_WIN_ROWS = NA_ROWS + NA_G
NA_BLOCKS = GRID_ROWS // NA_G
NA_VARIANTS = 3


def _na_block_start(g):
    return jnp.clip(NA_G * g - NA_ROWS // 2, 0, GRID_ROWS - NA_WIN_ROWS)


def _na_kernel(q_ref, k_ref, v_ref, ck_ref, cv_ref, bias_ref, o_ref):
    g = pl.program_id(1)
    start = pl.multiple_of(_na_block_start(g) * GRID_W, GRID_W)
    win = pl.ds(start, NA_WIN_ROWS * GRID_W)
    for jp in range(NA_HEADS // 2):
        sl = slice(LANES * jp, LANES * (jp + 1))
        q_stack = _stack_heads({jp: q_ref[:, sl] * ATTN_SCALE}, (2 * jp, 2 * jp + 1))
        bias = bias_ref[jp]
        o = _attend(q_stack, [(k_ref[win, sl].astype(BF16), v_ref[win, sl].astype(BF16), lambda s: s + bias),
                              (ck_ref[:, sl].astype(BF16), cv_ref[:, sl].astype(BF16), None)])
        o_ref[:, sl] = _merge_pair(o[:NA_Q], o[NA_Q:]).astype(BF16)


def _lat_neighbourhood(l, qn, kn, vn, cache_k, cache_v, bias_tab):
    kv_spec = pl.BlockSpec((DEC_SEQ, NA_W), lambda b, g: (b, 0))
    cache_spec = pl.BlockSpec((None, None, PAST_LEN, NA_W), lambda b, g: (b, l, 0, 0))
    variant = lambda g: jnp.where(g == 0, 0, jnp.where(g == NA_BLOCKS - 1, 2, 1))
    return pl.pallas_call(
        _na_kernel,
        grid=(DEC_BATCH, NA_BLOCKS),
        in_specs=[pl.BlockSpec((NA_Q, NA_W), lambda b, g: (N_CTX // NA_Q + b * NA_BLOCKS + g, 0)),
                  kv_spec, kv_spec, cache_spec, cache_spec,
                  pl.BlockSpec((None, None, NA_HEADS // 2, 2 * NA_Q, NA_WIN_ROWS * GRID_W),
                               lambda b, g: (l, variant(g), 0, 0, 0))],
        out_specs=pl.BlockSpec((NA_Q, NA_W), lambda b, g: (b * NA_BLOCKS + g, 0)),
        out_shape=jax.ShapeDtypeStruct((N_LAT, NA_W), BF16),
        compiler_params=_params("arbitrary", "arbitrary"),
        name=f"lat_neighbourhood{l}",
    )(qn, kn, vn, cache_k, cache_v, bias_tab)


OUTPROJ_ROWS = 256


def _outproj_kernel(xc_ref, xl_ref, mc_ref, mf_ref, mw_ref, mn_ref, mod_ref, g_ref, w_ref, r_hi, r_lo,
                    x_ref, h_ref, lg_ref, wb_ref):
    i = pl.program_id(0)

    @pl.when(i == 0)
    def _():
        wb_ref[...] = w_ref[...].astype(BF16)

    d = D_MODEL

    def block(x_in_ref, mixed_rows):
        for r0 in range(0, TM, OUTPROJ_ROWS):
            rows = slice(r0, r0 + OUTPROJ_ROWS)
            x = x_in_ref[rows, :] + mod_ref[:, 2 * d:3 * d] * _dot(mixed_rows(rows), wb_ref[...])
            x_ref[rows, :] = x
            h = _rms_mod(x, g_ref[...], mod_ref[:, 3 * d:4 * d], mod_ref[:, 4 * d:5 * d])
            h_ref[rows, :] = h.astype(BF16)
            h_hi, h_lo = _split(h)
            lg_ref[:, rows] = _dot3(h_hi, h_lo, r_hi[...], r_lo[...]).T[0:N_EXPERTS, :]

    @pl.when(i < N_CTX // TM)
    def _():
        block(xc_ref, lambda rows: mc_ref[rows, :])

    @pl.when(i >= N_CTX // TM)
    def _():
        block(xl_ref, lambda rows: jnp.concatenate([mf_ref[rows, :], mw_ref[rows, :], mn_ref[rows, :]], axis=1))


def _outproj(l, x_ctx, x_lat, mixed_ctx, mixed_f, mixed_w, mixed_n, mod4, g_ffn, w_out, r_hi, r_lo):
    n_ctx_blocks = N_CTX // TM
    xc_spec, xl_spec = _two_stream_specs(D_MODEL)
    lat = lambda w: pl.BlockSpec((TM, w), lambda i: (jnp.maximum(i - n_ctx_blocks, 0), 0))
    whole = lambda shape: pl.BlockSpec(shape, lambda i: (0,) * len(shape))
    row = lambda w: pl.BlockSpec((TM, w), lambda i: (i, 0))
    return pl.pallas_call(
        _outproj_kernel,
        grid=(N_TOK // TM,),
        in_specs=[xc_spec, xl_spec,
                  pl.BlockSpec((TM, D_MODEL), lambda i: (jnp.minimum(i, n_ctx_blocks - 1), 0)),
                  lat(F_WIDTH), lat(WIN_Q), lat(NA_W),
                  pl.BlockSpec((None, None, 1, 6 * D_MODEL), lambda i: (l, _cond_of_block(i), 0, 0)),
                  pl.BlockSpec((None, 1, D_MODEL), lambda i: (l, 0, 0)),
                  pl.BlockSpec((None, D_MODEL, D_MODEL), lambda i: (l, 0, 0)),
                  whole((D_MODEL, LANES)), whole((D_MODEL, LANES))],
        out_specs=[row(D_MODEL), row(D_MODEL), pl.BlockSpec((N_EXPERTS, TM), lambda i: (0, i))],
        out_shape=[jax.ShapeDtypeStruct((N_TOK, D_MODEL), F32),
                   jax.ShapeDtypeStruct((N_TOK, D_MODEL), BF16),
                   jax.ShapeDtypeStruct((N_EXPERTS, N_TOK), F32)],
        scratch_shapes=[pltpu.VMEM((D_MODEL, D_MODEL), BF16)],
        compiler_params=_params("arbitrary"),
        name=f"outproj{l}",
    )(x_ctx, x_lat, mixed_ctx, mixed_f, mixed_w, mixed_n, mod4, g_ffn.reshape(DEPTH, 1, D_MODEL), w_out,
      r_hi, r_lo)


PREFIX_CHUNK = 256
MANTISSA_STEPS = 44


def _prefix_exclusive(m):
    rows, n = m.shape
    t0 = lax.broadcasted_iota(jnp.int32, (PREFIX_CHUNK, PREFIX_CHUNK), 0)
    t1 = lax.broadcasted_iota(jnp.int32, (PREFIX_CHUNK, PREFIX_CHUNK), 1)
    upper = jnp.where(t0 < t1, 1.0, 0.0).astype(BF16)
    carry = jnp.zeros((rows, 1), F32)
    outs = []
    for c in range(n // PREFIX_CHUNK):
        blk = m[:, c * PREFIX_CHUNK:(c + 1) * PREFIX_CHUNK]
        outs.append(_dot(blk.astype(BF16), upper) + carry)
        carry = carry + blk.sum(axis=-1, keepdims=True)
    return outs[0] if len(outs) == 1 else jnp.concatenate(outs, axis=-1)


ROUTE_STREAMS = ((0, BATCH, CAP_CTX), (N_CTX, DEC_BATCH, CAP_LAT))


def _route_kernel(lg_ref, *out_refs):
    affs, caps = [], []
    for t0, groups, cap in ROUTE_STREAMS:
        n = (N_CTX if t0 == 0 else N_LAT) // groups
        x = jnp.concatenate([lg_ref[:, t0 + g * n:t0 + (g + 1) * n] for g in range(groups)], axis=0)
        x = x.reshape(groups, N_EXPERTS, n)
        e = jnp.exp(x - x.max(axis=1, keepdims=True))
        affs.append((e / e.sum(axis=1, keepdims=True)).reshape(groups * N_EXPERTS, n))
        caps.append(float(cap))

    def count_ge(aff, t):
        return jnp.where(aff >= t, 1.0, 0.0).sum(axis=-1, keepdims=True)

    def keep_if_enough(aff, cap, cand, otherwise):
        return jnp.where(count_ge(aff, cand) >= cap, cand, otherwise)

    above = [jnp.full((aff.shape[0], 1), 2.0, F32) for aff in affs]
    for s in (64, 32, 16, 8, 4, 2, 1):
        cands = [a * (2.0 ** -s) for a in above]
        above = [jnp.where(count_ge(aff, c) >= cap, a, c) for aff, cap, a, c in zip(affs, caps, above, cands)]
    bases = [keep_if_enough(aff, cap, a * 0.5, 0.0) for aff, cap, a in zip(affs, caps, above)]

    def refine(_, carry):
        thrs, incs = carry
        thrs = tuple(keep_if_enough(aff, cap, t + i, t) for aff, cap, t, i in zip(affs, caps, thrs, incs))
        return thrs, tuple(i * 0.5 for i in incs)

    thrs, _ = lax.fori_loop(0, MANTISSA_STEPS, refine, (tuple(bases), tuple(b * 0.5 for b in bases)))

    for k, ((_, groups, _), aff, cap, thr) in enumerate(zip(ROUTE_STREAMS, affs, caps, thrs)):
        slot_ref, aff_ref, slott_ref = out_refs[3 * k:3 * k + 3]
        n = aff.shape[1]
        gt = jnp.where(aff > thr, 1.0, 0.0)
        eq = jnp.where(aff == thr, 1.0, 0.0)
        need = cap - gt.sum(axis=-1, keepdims=True)
        sel = gt + eq * jnp.where(_prefix_exclusive(eq) < need, 1.0, 0.0)
        slot = jnp.where(sel > 0.0, _prefix_exclusive(sel), -1.0)
        slot_ref[...] = slot.astype(jnp.int32)
        aff_ref[...] = aff
        unused = jnp.full((LANES - N_EXPERTS, n), -1.0, F32)
        for g in range(groups):
            tile = jnp.concatenate([slot[g * N_EXPERTS:(g + 1) * N_EXPERTS], unused], axis=0)
            slott_ref[g * n:(g + 1) * n, :] = tile.T.astype(jnp.int32)


def _route(lg_t):
    shapes = []
    for t0, groups, _ in ROUTE_STREAMS:
        n_tok = N_CTX if t0 == 0 else N_LAT
        rows, n = groups * N_EXPERTS, n_tok // groups
        shapes += [((rows, n), jnp.int32), ((rows, n), F32), ((n_tok, LANES), jnp.int32)]
    return pl.pallas_call(
        _route_kernel,
        grid=(1,),
        in_specs=[pl.BlockSpec((N_EXPERTS, N_TOK), lambda i: (0, 0))],
        out_specs=[pl.BlockSpec(shape, lambda i: (0, 0)) for shape, _ in shapes],
        out_shape=[jax.ShapeDtypeStruct(shape, dt) for shape, dt in shapes],
        compiler_params=_params("arbitrary"),
        name="route",
    )(lg_t)


CTX_GROUP = 4


def _gather_ctx_kernel(h_ref, slot_ref, aff_ref, x_ref, g_ref, p_ref):
    s_iota = lax.broadcasted_iota(jnp.int32, (CAP_CTX, SEQ), 0)
    for bb in range(CTX_GROUP):
        slots = slice(bb * CAP_CTX, (bb + 1) * CAP_CTX)
        for e in range(N_EXPERTS):
            row = bb * N_EXPERTS + e
            hit = s_iota == slot_ref[row:row + 1, :]
            p_ref[bb, e * CAP_CTX:(e + 1) * CAP_CTX, :] = jnp.where(hit, 1.0, 0.0).astype(BF16)
            g_ref[e, slots] = jnp.where(hit, aff_ref[row:row + 1, :], 0.0).sum(axis=-1, keepdims=True)
        x = _dot(p_ref[bb], h_ref[bb * SEQ:(bb + 1) * SEQ, :]).astype(BF16)
        x_ref[:, slots, :] = x.reshape(N_EXPERTS, CAP_CTX, D_MODEL)


def _gather_ctx(h, slot, aff):
    return pl.pallas_call(
        _gather_ctx_kernel,
        grid=(BATCH // CTX_GROUP,),
        in_specs=[pl.BlockSpec((CTX_GROUP * SEQ, D_MODEL), lambda b: (b, 0)),
                  pl.BlockSpec((CTX_GROUP * N_EXPERTS, SEQ), lambda b: (b, 0)),
                  pl.BlockSpec((CTX_GROUP * N_EXPERTS, SEQ), lambda b: (b, 0))],
        out_specs=[pl.BlockSpec((N_EXPERTS, CTX_GROUP * CAP_CTX, D_MODEL), lambda b: (0, b, 0)),
                   pl.BlockSpec((N_EXPERTS, CTX_GROUP * CAP_CTX, 1), lambda b: (0, b, 0))],
        out_shape=[jax.ShapeDtypeStruct((N_EXPERTS, ROWS_CTX, D_MODEL), BF16),
                   jax.ShapeDtypeStruct((N_EXPERTS, ROWS_CTX, 1), F32)],
        scratch_shapes=[pltpu.VMEM((CTX_GROUP, N_EXPERTS * CAP_CTX, SEQ), BF16)],
        compiler_params=_params("arbitrary"),
        name="gather_ctx",
    )(h, slot, aff)


LAT_GATHER_EXPERTS = 2


def _gather_lat_kernel(h_ref, slot_ref, aff_ref, x_ref, g_ref):
    s_iota = lax.broadcasted_iota(jnp.int32, (CAP_LAT, DEC_SEQ), 0)
    for k in range(LAT_GATHER_EXPERTS):
        e = pl.program_id(1) * LAT_GATHER_EXPERTS + k
        hit = s_iota == slot_ref[pl.ds(e, 1), :]
        x_ref[k] = _dot(jnp.where(hit, 1.0, 0.0).astype(BF16), h_ref[...]).astype(BF16)
        g_ref[k] = jnp.where(hit, aff_ref[pl.ds(e, 1), :], 0.0).sum(axis=-1, keepdims=True)


def _gather_lat(h, slot, aff):
    lat_blk0 = N_CTX // DEC_SEQ
    return pl.pallas_call(
        _gather_lat_kernel,
        grid=(DEC_BATCH, N_EXPERTS // LAT_GATHER_EXPERTS),
        in_specs=[pl.BlockSpec((DEC_SEQ, D_MODEL), lambda b, e: (lat_blk0 + b, 0)),
                  pl.BlockSpec((N_EXPERTS, DEC_SEQ), lambda b, e: (b, 0)),
                  pl.BlockSpec((N_EXPERTS, DEC_SEQ), lambda b, e: (b, 0))],
        out_specs=[pl.BlockSpec((LAT_GATHER_EXPERTS, CAP_LAT, D_MODEL), lambda b, e: (e, b, 0)),
                   pl.BlockSpec((LAT_GATHER_EXPERTS, CAP_LAT, 1), lambda b, e: (e, b, 0))],
        out_shape=[jax.ShapeDtypeStruct((N_EXPERTS, ROWS_LAT, D_MODEL), BF16),
                   jax.ShapeDtypeStruct((N_EXPERTS, ROWS_LAT, 1), F32)],
        compiler_params=_params("arbitrary", "arbitrary"),
        name="gather_lat",
    )(h, slot, aff)


N_FF_CHUNKS = D_FF // TF
CHUNKS_PER_STEP = 4
FF_BLOCK = CHUNKS_PER_STEP * TF
N_FF_STEPS = -(-N_FF_CHUNKS // CHUNKS_PER_STEP)
LAST_STEP_CHUNKS = N_FF_CHUNKS - (N_FF_STEPS - 1) * CHUNKS_PER_STEP


def _silu_tanh(x):
    return x * (0.5 + 0.5 * jnp.tanh(0.5 * x))


def _ffn_kernel(xc_ref, xl_ref, gc_ref, gl_ref, wg_ref, wu_ref, wd_ref, y_ref, x_sc, h_sc, acc_sc):
    j = pl.program_id(1)

    @pl.when(j == 0)
    def _():
        x_sc[0:ROWS_CTX, :] = xc_ref[...]
        x_sc[ROWS_CTX:, :] = xl_ref[...]

    def step(n_chunks, first, last):
        x = x_sc[...]
        for sub in range(n_chunks):
            cols = slice(sub * TF, (sub + 1) * TF)
            a = _dot(x, wg_ref[:, cols].astype(BF16))
            u = _dot(x, wu_ref[:, cols].astype(BF16))
            h_sc[sub] = (_silu_tanh(a) * u).astype(BF16)
        for c0 in range(0, D_MODEL, TD):
            out_cols = slice(c0, c0 + TD)
            part = None
            for sub in range(n_chunks):
                t = _dot(h_sc[sub], wd_ref[sub * TF:(sub + 1) * TF, out_cols].astype(BF16))
                part = t if part is None else part + t
            if not first:
                part = acc_sc[:, out_cols] + part
            if last:
                y_ref[0:ROWS_CTX, out_cols] = (part[0:ROWS_CTX] * gc_ref[...]).astype(BF16)
                y_ref[ROWS_CTX:, out_cols] = (part[ROWS_CTX:] * gl_ref[...]).astype(BF16)
            else:
                acc_sc[:, out_cols] = part

    pl.when(j == 0)(functools.partial(step, CHUNKS_PER_STEP, True, False))
    pl.when((j > 0) & (j < N_FF_STEPS - 1))(functools.partial(step, CHUNKS_PER_STEP, False, False))
    pl.when(j == N_FF_STEPS - 1)(functools.partial(step, LAST_STEP_CHUNKS, False, True))


def _ffn(l, x_c, x_l, g_c, g_l, w_gate, w_up, w_down):
    rows = ROWS_CTX + ROWS_LAT
    xin = lambda r: pl.BlockSpec((None, r, D_MODEL), lambda e, j: (e, 0, 0))
    gin = lambda r: pl.BlockSpec((None, r, 1), lambda e, j: (e, 0, 0))
    return pl.pallas_call(
        _ffn_kernel,
        grid=(N_EXPERTS, N_FF_STEPS),
        in_specs=[xin(ROWS_CTX), xin(ROWS_LAT), gin(ROWS_CTX), gin(ROWS_LAT),
                  pl.BlockSpec((None, None, D_MODEL, FF_BLOCK), lambda e, j: (l, e, 0, j)),
                  pl.BlockSpec((None, None, D_MODEL, FF_BLOCK), lambda e, j: (l, e, 0, j)),
                  pl.BlockSpec((None, None, FF_BLOCK, D_MODEL), lambda e, j: (l, e, j, 0))],
        out_specs=pl.BlockSpec((None, rows, D_MODEL), lambda e, j: (e, 0, 0)),
        out_shape=jax.ShapeDtypeStruct((N_EXPERTS, rows, D_MODEL), BF16),
        scratch_shapes=[pltpu.VMEM((rows, D_MODEL), BF16), pltpu.VMEM((CHUNKS_PER_STEP, rows, TF), BF16),
                        pltpu.VMEM((rows, D_MODEL), F32)],
        compiler_params=_params("arbitrary", "arbitrary"),
        name=f"experts{l}",
    )(x_c, x_l, g_c, g_l, w_gate, w_up, w_down)


def _finish(x, res, mod_ref, gf_ref, final):
    y = x + mod_ref[:, 5 * D_MODEL:] * res
    if final:
        y = y * lax.rsqrt(jnp.mean(y * y, axis=-1, keepdims=True) + RMS_EPS) * gf_ref[...]
    return y


def _combine_ctx_kernel(final, x_ref, y_ref, slot_ref, rep_ref, mod_ref, gf_ref, o_ref):
    n_col = N_EXPERTS * CAP_CTX
    col = (lax.broadcasted_iota(jnp.int32, (SEQ, n_col), 1) % CAP_CTX).astype(F32)
    for bb in range(CTX_GROUP):
        rows = slice(bb * SEQ, (bb + 1) * SEQ)
        spread = _dot(slot_ref[rows, :].astype(F32).astype(BF16), rep_ref[...])
        p = jnp.where(spread == col, 1.0, 0.0).astype(BF16)
        y = y_ref[:, bb * CAP_CTX:(bb + 1) * CAP_CTX, :].reshape(n_col, D_MODEL)
        o_ref[rows, :] = _finish(x_ref[rows, :], _dot(p, y), mod_ref, gf_ref, final)


def _combine_ctx(l, final, x_new, y, slot_t, mod4, g_final):
    n_col = N_EXPERTS * CAP_CTX
    rep = (np.arange(n_col)[None, :] // CAP_CTX == np.arange(LANES)[:, None]).astype(np.float32)
    return pl.pallas_call(
        functools.partial(_combine_ctx_kernel, final),
        grid=(BATCH // CTX_GROUP,),
        in_specs=[pl.BlockSpec((CTX_GROUP * SEQ, D_MODEL), lambda b: (b, 0)),
                  pl.BlockSpec((N_EXPERTS, CTX_GROUP * CAP_CTX, D_MODEL), lambda b: (0, b, 0)),
                  pl.BlockSpec((CTX_GROUP * SEQ, LANES), lambda b: (b, 0)),
                  pl.BlockSpec((LANES, n_col), lambda b: (0, 0)),
                  pl.BlockSpec((None, None, 1, 6 * D_MODEL), lambda b: (l, 0, 0, 0)),
                  pl.BlockSpec((1, D_MODEL), lambda b: (0, 0))],
        out_specs=pl.BlockSpec((CTX_GROUP * SEQ, D_MODEL), lambda b: (b, 0)),
        out_shape=jax.ShapeDtypeStruct((N_CTX, D_MODEL), F32),
        compiler_params=_params("arbitrary"),
        name=f"combine_ctx{l}",
    )(x_new, y, slot_t, jnp.asarray(rep, BF16), mod4, g_final.reshape(1, D_MODEL))


TMC = 512


def _combine_lat_kernel(final, x_ref, y_ref, slot_ref, mod_ref, gf_ref, o_ref):
    s_iota = lax.broadcasted_iota(jnp.int32, (TMC, CAP_LAT), 1)
    slot = slot_ref[...]
    res = None
    for e in range(N_EXPERTS):
        p = jnp.where(slot[:, e:e + 1] == s_iota, 1.0, 0.0).astype(BF16)
        t = _dot(p, y_ref[e])
        res = t if res is None else res + t
    o_ref[...] = _finish(x_ref[...], res, mod_ref, gf_ref, final)


def _combine_lat(l, final, x_new, y, slot_t, mod4, g_final):
    nt = DEC_SEQ // TMC
    return pl.pallas_call(
        functools.partial(_combine_lat_kernel, final),
        grid=(DEC_BATCH, nt),
        in_specs=[pl.BlockSpec((TMC, D_MODEL), lambda b, t: (N_CTX // TMC + b * nt + t, 0)),
                  pl.BlockSpec((N_EXPERTS, CAP_LAT, D_MODEL), lambda b, t: (0, ROWS_CTX // CAP_LAT + b, 0)),
                  pl.BlockSpec((TMC, LANES), lambda b, t: (b * nt + t, 0)),
                  pl.BlockSpec((None, None, 1, 6 * D_MODEL), lambda b, t: (l, 1 + b, 0, 0)),
                  pl.BlockSpec((1, D_MODEL), lambda b, t: (0, 0))],
        out_specs=pl.BlockSpec((TMC, D_MODEL), lambda b, t: (b * nt + t, 0)),
        out_shape=jax.ShapeDtypeStruct((N_LAT, D_MODEL), F32),
        compiler_params=_params("arbitrary", "arbitrary"),
        name=f"combine_lat{l}",
    )(x_new, y, slot_t, mod4, g_final.reshape(1, D_MODEL))


def _split_table(t):
    hi = t.astype(BF16)
    return hi, (t - hi.astype(F32)).astype(BF16)


def _dft_tables(n):
    p = np.arange(n, dtype=np.int64)
    ang = ((p[:, None] * p[None, :]) % n).astype(np.float64) * (2.0 * np.pi / n)
    return np.cos(ang).astype(np.float32), np.sin(ang).astype(np.float32)


def _channel_dft_tables():
    c = np.arange(F_WIDTH, dtype=np.int64)
    same = (c[:, None] // HEAD_DIM) == (c[None, :] // HEAD_DIM)
    ang = (((c[:, None] % HEAD_DIM) * (c[None, :] % HEAD_DIM)) % HEAD_DIM).astype(np.float64) * (2.0 * np.pi / HEAD_DIM)
    return (np.where(same, np.cos(ang), 0.0).astype(np.float32),
            np.where(same, np.sin(ang), 0.0).astype(np.float32))


def _rope_tables():
    half = HEAD_DIM // 2
    nf = half // 2
    pos = np.arange(DEC_SEQ)
    inv = 1.0 / (ROPE_BASE ** (np.arange(nf, dtype=np.float64) / nf))
    ang_r = (pos // GRID_W).astype(np.float64)[:, None] * inv
    ang_c = (pos % GRID_W).astype(np.float64)[:, None] * inv

    def head(fn, sign):
        return np.concatenate([sign * fn(ang_r), fn(ang_r), sign * fn(ang_c), fn(ang_c)], axis=-1)

    cos = head(np.cos, 1.0)
    sin = head(np.sin, -1.0)
    return (np.concatenate([cos, cos], axis=-1).astype(np.float32),
            np.concatenate([sin, sin], axis=-1).astype(np.float32))


def _na_bias_tables(rpb):
    cq = np.arange(GRID_W)
    rel_c = np.clip(cq[None, :] - cq[:, None] + NA_COLS - 1, 0, 2 * NA_COLS - 2)
    pick = (rel_c[:, :, None] == np.arange(2 * NA_COLS - 1)).astype(np.float32)
    cs = np.clip(cq - NA_COLS // 2, 0, GRID_W - NA_COLS)
    col_ok = (cq[None, :] >= cs[:, None]) & (cq[None, :] < cs[:, None] + NA_COLS)
    bc = jnp.einsum('lhrj,qkj->lhrqk', rpb, pick, precision=lax.Precision.HIGHEST)
    bc = jnp.where(col_ok[None, None, None], bc, NEG_INF)
    bc = jnp.pad(bc, ((0, 0), (0, 0), (1, 1), (0, 0), (0, 0)), constant_values=NEG_INF)
    bc = jnp.concatenate([bc[:, :, :-1], bc[:, :, 1:]], axis=-1)
    return pl.pallas_call(
        _na_bias_kernel,
        grid=(DEPTH, NA_VARIANTS, NA_HEADS // 2),
        in_specs=[pl.BlockSpec((None, 2, 2 * NA_ROWS, GRID_W, 2 * GRID_W), lambda l, v, p: (l, p, 0, 0, 0))],
        out_specs=pl.BlockSpec((None, None, None, 2 * NA_Q, NA_WIN_ROWS * GRID_W),
                               lambda l, v, p: (l, v, p, 0, 0)),
        out_shape=jax.ShapeDtypeStruct((DEPTH, NA_VARIANTS, NA_HEADS // 2, 2 * NA_Q, NA_WIN_ROWS * GRID_W), F32),
        compiler_params=_params("arbitrary", "arbitrary", "arbitrary"),
        name="na_bias",
    )(bc)


def _na_window_plan():
    plan = []
    for g in (0, 1, NA_BLOCKS - 1):
        start = int(np.clip(NA_G * g - NA_ROWS // 2, 0, GRID_ROWS - NA_WIN_ROWS))
        rows = []
        for a in range(NA_G):
            r = NA_G * g + a
            rs = int(np.clip(r - NA_ROWS // 2, 0, GRID_ROWS - NA_ROWS))
            rows.append([start + w - r + NA_ROWS - 1 if rs <= start + w < rs + NA_ROWS else None
                         for w in range(NA_WIN_ROWS)])
        plan.append(rows)
    return plan


def _na_bias_kernel(bc_ref, o_ref):
    outside = jnp.full((GRID_W, 2 * GRID_W), NEG_INF, F32)
    low = _lane_is_low(outside.shape)

    def pair_tile(half, ra, rb):
        if ra is None and rb is None:
            return outside
        if rb is None:
            return jnp.where(low, bc_ref[half, ra + 1], NEG_INF)
        if ra is None:
            return jnp.where(low, NEG_INF, bc_ref[half, rb])
        return bc_ref[half, ra + 1]

    for v, rows in enumerate(_na_window_plan()):
        @pl.when(pl.program_id(1) == v)
        def _():
            for half in range(2):
                for a, rel in enumerate(rows):
                    tiles = [pair_tile(half, rel[w], rel[w + 1]) for w in range(0, NA_WIN_ROWS, 2)]
                    r0 = half * NA_Q + a * GRID_W
                    o_ref[r0:r0 + GRID_W, :] = jnp.concatenate(tiles, axis=-1)


def kernel(x_prompt, x_sample, cache_win_k, cache_win_v, cache_nat_k, cache_nat_v, c, c_ctx, w_mod, b_mod, g_mix, g_ffn, w_in, w_out, win_sink, nat_rpb, w_router, w_gate, w_up, w_down, g_final):
    x_ctx = x_prompt.reshape(N_CTX, D_MODEL)
    x_lat = x_sample.reshape(N_LAT, D_MODEL)
    cond = jnp.concatenate([c_ctx[None, :], c, jnp.zeros((N_COND - 1 - DEC_BATCH, D_MODEL), F32)], axis=0)
    mod4 = _adaln(cond, w_mod, b_mod).reshape(DEPTH, N_COND, 1, 6 * D_MODEL)

    dft_ch = _channel_dft_tables()
    dft_ctx = _dft_tables(SEQ)
    dft_lat = _dft_tables(DEC_SEQ)
    cos_t, sin_t = _rope_tables()
    cwk = cache_win_k.reshape(DEC_BATCH, DEPTH, PAST_LEN, WIN_KV)
    cwv = cache_win_v.reshape(DEC_BATCH, DEPTH, PAST_LEN, WIN_KV)
    cnk = cache_nat_k.reshape(DEC_BATCH, DEPTH, PAST_LEN, NA_W)
    cnv = cache_nat_v.reshape(DEC_BATCH, DEPTH, PAST_LEN, NA_W)
    r_pad = jnp.pad(w_router, ((0, 0), (0, 0), (0, LANES - N_EXPERTS)))
    na_bias = _na_bias_tables(nat_rpb)

    kvt = ()
    for l in range(DEPTH):
        final = l == DEPTH - 1
        (f, qw, qn), (kw, vw, kn, vn), kvt = _project(l, x_ctx, x_lat, mod4, g_mix, w_in, kvt)

        mixed_ctx = _ctx_mixer(l, win_sink, f, qw, qn, kvt, dft_ch, dft_ctx)
        mixed_f = _lat_fourier(f, dft_ch, dft_lat)
        mixed_w = _lat_window(l, win_sink, qw, kw, vw, cwk, cwv, cos_t, sin_t)
        mixed_n = _lat_neighbourhood(l, qn, kn, vn, cnk, cnv, na_bias)

        r_hi, r_lo = _split_table(r_pad[l])
        x_new, h, lg_t = _outproj(l, x_ctx, x_lat, mixed_ctx, mixed_f, mixed_w, mixed_n, mod4, g_ffn,
                                  w_out, r_hi, r_lo)

        slot_c, aff_c, slot_ct, slot_l, aff_l, slot_lt = _route(lg_t)
        xg_c, gate_c = _gather_ctx(h, slot_c, aff_c)
        xg_l, gate_l = _gather_lat(h, slot_l, aff_l)
        y = _ffn(l, xg_c, xg_l, gate_c, gate_l, w_gate, w_up, w_down)
        x_ctx = _combine_ctx(l, final, x_new, y, slot_ct, mod4, g_final)
        x_lat = _combine_lat(l, final, x_new, y, slot_lt, mod4, g_final)

    y_prompt = x_ctx.reshape(BATCH, SEQ, D_MODEL)
    y_sample = x_lat.reshape(DEC_BATCH, DEC_SEQ, D_MODEL)
    new_kv = [t.reshape(BATCH, DEPTH, w // HEAD_DIM, HEAD_DIM, SEQ).transpose(0, 1, 4, 2, 3)
              for t, w in zip(kvt, KV_WIDTHS)]
    return (y_prompt, y_sample, *new_kv)
```

```python
import functools

import numpy as np
import jax
import jax.numpy as jnp
from jax import lax
from jax.experimental import pallas as pl
from jax.experimental.pallas import tpu as pltpu

D_MODEL = 1024
BATCH = 16
SEQ = 256
DEPTH = 2
DEC_BATCH = 2
DEC_SEQ = 2048
PAST_LEN = 256
GRID_W = 64
HEAD_DIM = 64
F_WIDTH = 256
WIN_HEADS = 6
WIN_KV_HEADS = 2
WINDOW = 128
NA_HEADS = 6
NA_ROWS = 8
NA_COLS = 16
N_EXPERTS = 16
EC_CAPACITY = 2
D_FF = 2816
ROPE_BASE = 10000.0
RMS_EPS = 1e-6
NEG_INF = -1e30
ATTN_SCALE = HEAD_DIM ** -0.5
WIN_Q = WIN_HEADS * HEAD_DIM
WIN_KV = WIN_KV_HEADS * HEAD_DIM
NA_W = NA_HEADS * HEAD_DIM
N_IN = F_WIDTH + WIN_Q + 2 * WIN_KV + 3 * NA_W
SPLITS = (0, F_WIDTH, F_WIDTH + WIN_Q, F_WIDTH + WIN_Q + WIN_KV, F_WIDTH + WIN_Q + 2 * WIN_KV,
          F_WIDTH + WIN_Q + 2 * WIN_KV + NA_W, F_WIDTH + WIN_Q + 2 * WIN_KV + 2 * NA_W, N_IN)

N_CTX = BATCH * SEQ
N_LAT = DEC_BATCH * DEC_SEQ
N_TOK = N_CTX + N_LAT
GRID_ROWS = DEC_SEQ // GRID_W
CAP_CTX = EC_CAPACITY * SEQ // N_EXPERTS
CAP_LAT = EC_CAPACITY * DEC_SEQ // N_EXPERTS
ROWS_CTX = BATCH * CAP_CTX
ROWS_LAT = DEC_BATCH * CAP_LAT
N_COND = 8

LANES = 128
MXU_COLS = 256
TM = 512
TN_MOD = 1536
TF = 256
TD = 256
VMEM_LIMIT = 56 * 1024 * 1024

F32 = jnp.float32
BF16 = jnp.bfloat16


def _params(*sem):
    return pltpu.CompilerParams(dimension_semantics=sem, vmem_limit_bytes=VMEM_LIMIT)


def _dot(a, b):
    return jnp.dot(a, b, preferred_element_type=F32)


def _dot_nt(a, b):
    return lax.dot_general(a, b, (((1,), (1,)), ((), ())), preferred_element_type=F32)


def _split(x):
    hi = x.astype(BF16)
    lo = (x - hi.astype(F32)).astype(BF16)
    return hi, lo


def _dot3(a_hi, a_lo, b_hi, b_lo):
    return _dot(a_hi, b_hi) + (_dot(a_lo, b_hi) + _dot(a_hi, b_lo))


def _silu(x):
    return x / (1.0 + jnp.exp(-x))


def _rms_mod(x, g, shift, scale):
    y = x * lax.rsqrt(jnp.mean(x * x, axis=-1, keepdims=True) + RMS_EPS)
    return (y * g) * (1.0 + scale) + shift


def _softmax_parts(parts, sink=None):
    m = parts[0].max(axis=-1, keepdims=True)
    for s in parts[1:]:
        m = jnp.maximum(m, s.max(axis=-1, keepdims=True))
    if sink is not None:
        m = jnp.maximum(m, sink)
    es = [jnp.exp(s - m) for s in parts]
    den = es[0].sum(axis=-1, keepdims=True)
    for e in es[1:]:
        den = den + e.sum(axis=-1, keepdims=True)
    if sink is not None:
        den = den + jnp.exp(sink - m)
    return es, 1.0 / den


def _cond_of_block(i):
    n_ctx_blocks = N_CTX // TM
    return jnp.where(i < n_ctx_blocks, 0, 1 + (i - n_ctx_blocks) // (DEC_SEQ // TM))


def _two_stream_specs(width):
    n_ctx_blocks = N_CTX // TM
    ctx = pl.BlockSpec((TM, width), lambda i: (jnp.minimum(i, n_ctx_blocks - 1), 0))
    lat = pl.BlockSpec((TM, width), lambda i: (jnp.maximum(i - n_ctx_blocks, 0), 0))
    return ctx, lat


def _pick_stream(ctx_ref, lat_ref):
    return jnp.where(pl.program_id(0) < N_CTX // TM, ctx_ref[...], lat_ref[...])


def _adaln_kernel(c_ref, w_ref, b_ref, o_ref):
    s_hi, s_lo = _split(_silu(c_ref[...]))
    w_hi, w_lo = _split(w_ref[...])
    o_ref[...] = _dot3(s_hi, s_lo, w_hi, w_lo) + b_ref[...]


def _adaln(cond, w_mod, b_mod):
    return pl.pallas_call(
        _adaln_kernel,
        grid=(DEPTH, 6 * D_MODEL // TN_MOD),
        in_specs=[
            pl.BlockSpec((N_COND, D_MODEL), lambda l, j: (0, 0)),
            pl.BlockSpec((None, D_MODEL, TN_MOD), lambda l, j: (l, 0, j)),
            pl.BlockSpec((None, 1, TN_MOD), lambda l, j: (l, 0, j)),
        ],
        out_specs=pl.BlockSpec((None, N_COND, TN_MOD), lambda l, j: (l, 0, j)),
        out_shape=jax.ShapeDtypeStruct((DEPTH, N_COND, 6 * D_MODEL), F32),
        compiler_params=_params("arbitrary", "arbitrary"),
        name="adaln",
    )(cond, w_mod, b_mod.reshape(DEPTH, 1, 6 * D_MODEL))


KV_NAMES = ("kw", "vw", "kn", "vn")
KV_WIDTHS = (WIN_KV, WIN_KV, NA_W, NA_W)
KV_SPLITS = (SPLITS[2], SPLITS[3], SPLITS[5], SPLITS[6])
KV_TOTAL = sum(KV_WIDTHS)
Q_SPLITS = (SPLITS[0], SPLITS[1], SPLITS[4])


def _proj_kernel(n_prev, xc_ref, xl_ref, mod_ref, g_ref, w_ref, *rest):
    rest = rest[n_prev:]
    f_ref, qw_ref, qn_ref = rest[0:3]
    lat_kv = rest[3:7]
    ctx_kvt = rest[7:11]
    wb_ref, wt_ref = rest[11:13]
    i = pl.program_id(0)

    @pl.when(i == 0)
    def _():
        wb_ref[...] = w_ref[...].astype(BF16)
        r0 = 0
        for c0, width in zip(KV_SPLITS, KV_WIDTHS):
            wt_ref[r0:r0 + width, :] = w_ref[:, c0:c0 + width].T.astype(BF16)
            r0 += width

    natural = dict(zip(SPLITS[:-1], (f_ref, qw_ref) + tuple(lat_kv[0:2]) + (qn_ref,) + tuple(lat_kv[2:4])))

    def hidden(x_ref):
        return _rms_mod(x_ref[...], g_ref[...], mod_ref[:, 0:D_MODEL], mod_ref[:, D_MODEL:2 * D_MODEL]).astype(BF16)

    def project_tiles(h, wanted):
        for t0 in range(0, N_IN, MXU_COLS):
            hits = [(a, b) for a, b in zip(SPLITS[:-1], SPLITS[1:])
                    if a in wanted and max(a, t0) < min(b, t0 + MXU_COLS)]
            if not hits:
                continue
            acc = _dot(h, wb_ref[:, t0:t0 + MXU_COLS])
            for a, b in hits:
                lo, hi = max(a, t0), min(b, t0 + MXU_COLS)
                natural[a][:, lo - a:hi - a] = acc[:, lo - t0:hi - t0].astype(natural[a].dtype)

    @pl.when(i < N_CTX // TM)
    def _():
        h = hidden(xc_ref)
        project_tiles(h, Q_SPLITS)
        kvt = _dot_nt(wt_ref[...], h)
        r0 = 0
        for o_ref, width in zip(ctx_kvt, KV_WIDTHS):
            for bb in range(TM // SEQ):
                o_ref[bb] = kvt[r0:r0 + width, bb * SEQ:(bb + 1) * SEQ]
            r0 += width

    @pl.when(i >= N_CTX // TM)
    def _():
        project_tiles(hidden(xl_ref), Q_SPLITS + KV_SPLITS)


def _project(l, x_ctx, x_lat, mod4, g_mix, w_in, prev_kvt):
    n_ctx_blocks = N_CTX // TM
    xc_spec, xl_spec = _two_stream_specs(D_MODEL)
    both = lambda w: pl.BlockSpec((TM, w), lambda i: (i, 0))
    lat = lambda w: pl.BlockSpec((TM, w), lambda i: (jnp.maximum(i - n_ctx_blocks, 0), 0))
    ctx_t = lambda w: pl.BlockSpec((TM // SEQ, None, w, SEQ),
                                   lambda i: (jnp.minimum(i, n_ctx_blocks - 1), l, 0, 0))
    n_prev = len(prev_kvt)
    n_in = 5
    outs = pl.pallas_call(
        functools.partial(_proj_kernel, n_prev),
        grid=(N_TOK // TM,),
        in_specs=[
            xc_spec, xl_spec,
            pl.BlockSpec((None, None, 1, 6 * D_MODEL), lambda i: (l, _cond_of_block(i), 0, 0)),
            pl.BlockSpec((None, 1, D_MODEL), lambda i: (l, 0, 0)),
            pl.BlockSpec((None, D_MODEL, N_IN), lambda i: (l, 0, 0)),
        ] + [pl.BlockSpec(memory_space=pl.ANY)] * n_prev,
        out_specs=[both(F_WIDTH), both(WIN_Q), both(NA_W)] + [lat(w) for w in KV_WIDTHS]
                  + [ctx_t(w) for w in KV_WIDTHS],
        out_shape=[jax.ShapeDtypeStruct((N_TOK, w), dt) for w, dt in ((F_WIDTH, F32), (WIN_Q, F32), (NA_W, BF16))]
                  + [jax.ShapeDtypeStruct((N_LAT, w), dt)
                     for w, dt in zip(KV_WIDTHS, (F32, BF16, BF16, BF16))]
                  + [jax.ShapeDtypeStruct((BATCH, DEPTH, w, SEQ), F32) for w in KV_WIDTHS],
        input_output_aliases={n_in + k: 7 + k for k in range(n_prev)},
        scratch_shapes=[pltpu.VMEM((D_MODEL, N_IN), BF16), pltpu.VMEM((KV_TOTAL, D_MODEL), BF16)],
        compiler_params=_params("arbitrary"),
        name=f"project{l}",
    )(x_ctx, x_lat, mod4, g_mix.reshape(DEPTH, 1, D_MODEL), w_in, *prev_kvt)
    return outs[0:3], outs[3:7], outs[7:11]


def _lane_is_low(shape):
    return lax.broadcasted_iota(jnp.int32, shape, len(shape) - 1) < HEAD_DIM


def _swap_halves(x):
    return pltpu.roll(x, HEAD_DIM, axis=x.ndim - 1)


def _win_kv_copy(h):
    return 0 if (h // (WIN_HEADS // WIN_KV_HEADS)) == (h % 2) else 1


def _stack_heads(q_pairs, heads):
    low = _lane_is_low(q_pairs[heads[0] // 2].shape)
    rows = [jnp.where(low if h % 2 == 0 else jnp.logical_not(low), q_pairs[h // 2], 0.0).astype(BF16)
            for h in heads]
    return rows[0] if len(rows) == 1 else jnp.concatenate(rows, axis=0)


def _per_head_column(values, rows_per_head):
    blk = lax.broadcasted_iota(jnp.int32, (len(values) * rows_per_head, 1), 0) // rows_per_head
    col = jnp.full(blk.shape, values[0], F32)
    for i in range(1, len(values)):
        col = jnp.where(blk == i, values[i], col)
    return col


def _attend(q_stack, kv_list, extra_logit=None, transposed=False):
    scores = []
    for k, _, post in kv_list:
        s = _dot(q_stack, k) if transposed else _dot_nt(q_stack, k)
        scores.append(post(s) if post is not None else s)
    exps, inv = _softmax_parts(scores, extra_logit)
    o = None
    for e, (_, v, _) in zip(exps, kv_list):
        t = _dot_nt(e.astype(BF16), v) if transposed else _dot(e.astype(BF16), v)
        o = t if o is None else o + t
    return o * inv


def _merge_pair(o_even, o_odd):
    return jnp.where(_lane_is_low(o_even.shape), o_even, o_odd)


def _gqa_attention(q_pairs, rows, kv_for_copy, sinks, transposed=False):
    per_head = {}
    for copy in (0, 1):
        heads = [h for h in range(WIN_HEADS) if _win_kv_copy(h) == copy]
        o = _attend(_stack_heads(q_pairs, heads), kv_for_copy(copy),
                    _per_head_column([sinks[h] for h in heads], rows), transposed)
        for i, h in enumerate(heads):
            per_head[h] = o[i * rows:(i + 1) * rows]
    return [_merge_pair(per_head[2 * j], per_head[2 * j + 1]) for j in range(WIN_HEADS // 2)]


def _swap_row_halves(x):
    return jnp.concatenate([x[HEAD_DIM:], x[:HEAD_DIM]], axis=0)


MIX_GROUP = 2


def _ctx_mixer_kernel(l, sink_ref, f_ref, qw_ref, qn_ref, kw_ref, vw_ref, kn_ref, vn_ref,
                      bc_ref, bs_ref, cs_ref, ss_ref, o_ref):
    bc_hi, bc_lo = _split(bc_ref[...])
    bs_hi, bs_lo = _split(bs_ref[...])
    cs_hi, cs_lo = _split(cs_ref[...])
    ss_hi, ss_lo = _split(ss_ref[...])
    for bb in range(MIX_GROUP):
        rows = slice(bb * SEQ, (bb + 1) * SEQ)
        f_hi, f_lo = _split(f_ref[rows, :])
        fc_hi, fc_lo = _split(_dot3(f_hi, f_lo, bc_hi, bc_lo))
        fs_hi, fs_lo = _split(_dot3(f_hi, f_lo, bs_hi, bs_lo))
        z = _dot3(cs_hi, cs_lo, fc_hi, fc_lo) - _dot3(ss_hi, ss_lo, fs_hi, fs_lo)
        o_ref[rows, 0:F_WIDTH] = (z * (SEQ * HEAD_DIM) ** -0.5).astype(BF16)

        kv = [(kw_ref[bb].astype(BF16), vw_ref[bb].astype(BF16)),
              (_swap_row_halves(kw_ref[bb]).astype(BF16), _swap_row_halves(vw_ref[bb]).astype(BF16))]
        q_pairs = [qw_ref[rows, LANES * j:LANES * (j + 1)] * ATTN_SCALE for j in range(WIN_HEADS // 2)]
        outs = _gqa_attention(q_pairs, SEQ, lambda c: [(kv[c][0], kv[c][1], None)],
                              [sink_ref[l, h] for h in range(WIN_HEADS)], transposed=True)
        for j, o in enumerate(outs):
            o_ref[rows, F_WIDTH + LANES * j:F_WIDTH + LANES * (j + 1)] = o.astype(BF16)

        for j in range(NA_HEADS // 2):
            sl = slice(LANES * j, LANES * (j + 1))
            q_pairs = {j: qn_ref[rows, sl] * ATTN_SCALE}
            o = _attend(_stack_heads(q_pairs, (2 * j, 2 * j + 1)),
                        [(kn_ref[bb, sl, :].astype(BF16), vn_ref[bb, sl, :].astype(BF16), None)], transposed=True)
            base = F_WIDTH + WIN_Q + LANES * j
            o_ref[rows, base:base + LANES] = _merge_pair(o[:SEQ], o[SEQ:]).astype(BF16)


def _ctx_mixer(l, win_sink, f, qw, qn, kvt, dft_ch, dft_seq):
    row = lambda w: pl.BlockSpec((MIX_GROUP * SEQ, w), lambda b: (b, 0))
    col = lambda w: pl.BlockSpec((MIX_GROUP, None, w, SEQ), lambda b: (b, l, 0, 0))
    const = lambda n: pl.BlockSpec((n, n), lambda b: (0, 0))
    return pl.pallas_call(
        functools.partial(_ctx_mixer_kernel, l),
        grid=(BATCH // MIX_GROUP,),
        in_specs=[pl.BlockSpec(memory_space=pltpu.SMEM),
                  row(F_WIDTH), row(WIN_Q), row(NA_W)] + [col(w) for w in KV_WIDTHS]
                 + [const(F_WIDTH)] * 2 + [const(SEQ)] * 2,
        out_specs=pl.BlockSpec((MIX_GROUP * SEQ, D_MODEL), lambda b: (b, 0)),
        out_shape=jax.ShapeDtypeStruct((N_CTX, D_MODEL), BF16),
        compiler_params=_params("arbitrary"),
        name=f"ctx_mixer{l}",
    )(win_sink, f, qw, qn, *kvt, *dft_ch, *dft_seq)


FT_ROWS = 512


HALF_SEQ = DEC_SEQ // 2
SUBLANES = 8


def _lat_fourier_kernel(f_ref, mir_ref, bc_ref, bs_ref, cs_ref, ss_ref, o_ref, st_ref, mid_ref):
    r, b = pl.program_id(0), pl.program_id(1)

    @pl.when(r == 0)
    def _():
        f_hi, f_lo = _split(f_ref[pl.ds(pl.multiple_of(b * DEC_SEQ, DEC_SEQ), DEC_SEQ), :])
        fc = _dot3(f_hi, f_lo, *_split(bc_ref[...]))
        fs = _dot3(f_hi, f_lo, *_split(bs_ref[...]))

        def mirrored(t):
            hi, lo = _split(t[HALF_SEQ:])
            return _dot(mir_ref[...], hi) + _dot(mir_ref[...], lo)

        st_ref[b, 0], st_ref[b, 1] = _split(fc[:HALF_SEQ] + mirrored(fc))
        st_ref[b, 2], st_ref[b, 3] = _split(fs[:HALF_SEQ] - mirrored(fs))
        mid_ref[b] = fc[HALF_SEQ:HALF_SEQ + SUBLANES]

    z = (_dot3(*_split(cs_ref[...]), st_ref[b, 0], st_ref[b, 1])
         - _dot3(*_split(ss_ref[...]), st_ref[b, 2], st_ref[b, 3]))
    k = r * FT_ROWS + lax.broadcasted_iota(jnp.int32, (FT_ROWS, 1), 0)
    z = z + jnp.where(k % 2 == 0, 1.0, -1.0) * mid_ref[b, 0:1, :]
    o_ref[...] = (z * (DEC_SEQ * HEAD_DIM) ** -0.5).astype(BF16)


def _lat_fourier(f, dft_ch, dft_seq):
    nrb = DEC_SEQ // FT_ROWS
    p = np.arange(HALF_SEQ)
    mirror = ((p[:, None] + p[None, :] == HALF_SEQ) & (p[:, None] > 0)).astype(np.float32)
    const = pl.BlockSpec((F_WIDTH, F_WIDTH), lambda r, b: (0, 0))
    rows = pl.BlockSpec((FT_ROWS, HALF_SEQ), lambda r, b: (r, 0))
    return pl.pallas_call(
        _lat_fourier_kernel,
        grid=(nrb, DEC_BATCH),
        in_specs=[pl.BlockSpec((N_LAT, F_WIDTH), lambda r, b: (1, 0)),
                  pl.BlockSpec((HALF_SEQ, HALF_SEQ), lambda r, b: (0, 0))] + [const] * 2 + [rows] * 2,
        out_specs=pl.BlockSpec((FT_ROWS, F_WIDTH), lambda r, b: (b * nrb + r, 0)),
        out_shape=jax.ShapeDtypeStruct((N_LAT, F_WIDTH), BF16),
        scratch_shapes=[pltpu.VMEM((DEC_BATCH, 4, HALF_SEQ, F_WIDTH), BF16),
                        pltpu.VMEM((DEC_BATCH, SUBLANES, F_WIDTH), F32)],
        compiler_params=_params("arbitrary", "arbitrary"),
        name="lat_fourier",
    )(f, jnp.asarray(mirror, BF16), *dft_ch, *dft_seq)


def _rope(x, cos, sin_signed):
    n = x.shape[-1]
    lane = lax.broadcasted_iota(jnp.int32, x.shape, x.ndim - 1)
    first = (lane % 32) < 16
    partner = jnp.where(first, pltpu.roll(x, n - 16, axis=x.ndim - 1), pltpu.roll(x, 16, axis=x.ndim - 1))
    return x * cos + partner * sin_signed


WIN_QUERIES = 256
WIN_KEYS = WIN_QUERIES + 2 * WINDOW


def _win_kernel(l, sink_ref, q_ref, k_ref, v_ref, ck_ref, cv_ref, cos_ref, sin_ref, cosq_ref, sinq_ref,
                o_ref, kp_ref, vp_ref, cp_ref):
    n = pl.program_id(1)
    nb = DEC_SEQ // WIN_QUERIES
    pad = WINDOW

    @pl.when(n == 0)
    def _():
        zeros = jnp.zeros((pad, LANES), BF16)
        kr = _rope(k_ref[...], cos_ref[...], sin_ref[...])
        v = v_ref[...]
        for idx, (kk, vv) in enumerate(((kr, v), (_swap_halves(kr), _swap_halves(v)))):
            kp_ref[idx, 0:pad] = zeros
            kp_ref[idx, pad + DEC_SEQ:] = zeros
            kp_ref[idx, pad:pad + DEC_SEQ] = kk.astype(BF16)
            vp_ref[idx, 0:pad] = zeros
            vp_ref[idx, pad + DEC_SEQ:] = zeros
            vp_ref[idx, pad:pad + DEC_SEQ] = vv.astype(BF16)
        ck = ck_ref[...]
        cv = cv_ref[...]
        cp_ref[0] = ck.astype(BF16)
        cp_ref[1] = _swap_halves(ck).astype(BF16)
        cp_ref[2] = cv.astype(BF16)
        cp_ref[3] = _swap_halves(cv).astype(BF16)

    lo = jnp.where(n == 0, WINDOW, 0)
    hi = jnp.where(n == nb - 1, WIN_QUERIES + WINDOW, WIN_KEYS)

    i = lax.broadcasted_iota(jnp.int32, (WIN_QUERIES, WIN_KEYS), 0)
    j = lax.broadcasted_iota(jnp.int32, (WIN_QUERIES, WIN_KEYS), 1)
    mask = (j >= i) & (j <= i + 2 * WINDOW) & (j >= lo) & (j < hi)
    band_bias = jnp.where(mask, 0.0, NEG_INF)

    def band(s):
        heads = s.shape[0] // WIN_QUERIES
        return (s.reshape(heads, WIN_QUERIES, s.shape[1]) + band_bias[None]).reshape(s.shape)

    start = pl.multiple_of(n * WIN_QUERIES, WIN_QUERIES)
    win = pl.ds(start, WIN_KEYS)
    q_pairs = [_rope(q_ref[:, LANES * jp:LANES * (jp + 1)], cosq_ref[...], sinq_ref[...]) * ATTN_SCALE
               for jp in range(WIN_HEADS // 2)]
    outs = _gqa_attention(
        q_pairs, WIN_QUERIES,
        lambda c: [(kp_ref[c, win, :], vp_ref[c, win, :], band), (cp_ref[c], cp_ref[2 + c], None)],
        [sink_ref[l, h] for h in range(WIN_HEADS)])
    for jp, o in enumerate(outs):
        o_ref[:, LANES * jp:LANES * (jp + 1)] = o.astype(BF16)


def _lat_window(l, win_sink, qw, kw, vw, cache_k, cache_v, cos_t, sin_t):
    nb = DEC_SEQ // WIN_QUERIES
    kv_spec = pl.BlockSpec((DEC_SEQ, WIN_KV), lambda b, n: (b, 0))
    cache_spec = pl.BlockSpec((None, None, PAST_LEN, WIN_KV), lambda b, n: (b, l, 0, 0))
    tab_all = pl.BlockSpec((DEC_SEQ, LANES), lambda b, n: (0, 0))
    tab_blk = pl.BlockSpec((WIN_QUERIES, LANES), lambda b, n: (n, 0))
    return pl.pallas_call(
        functools.partial(_win_kernel, l),
        grid=(DEC_BATCH, nb),
        in_specs=[pl.BlockSpec(memory_space=pltpu.SMEM),
                  pl.BlockSpec((WIN_QUERIES, WIN_Q), lambda b, n: (N_CTX // WIN_QUERIES + b * nb + n, 0)),
                  kv_spec, kv_spec, cache_spec, cache_spec, tab_all, tab_all, tab_blk, tab_blk],
        out_specs=pl.BlockSpec((WIN_QUERIES, WIN_Q), lambda b, n: (b * nb + n, 0)),
        out_shape=jax.ShapeDtypeStruct((N_LAT, WIN_Q), BF16),
        scratch_shapes=[pltpu.VMEM((2, DEC_SEQ + 2 * WINDOW, LANES), BF16),
                        pltpu.VMEM((2, DEC_SEQ + 2 * WINDOW, LANES), BF16),
                        pltpu.VMEM((4, PAST_LEN, LANES), BF16)],
        compiler_params=_params("arbitrary", "arbitrary"),
        name=f"lat_window{l}",
    )(win_sink, qw, kw, vw, cache_k, cache_v, cos_t, sin_t, cos_t, sin_t)


NA_G = 4
NA_Q = NA_G * GRID_W
NA_WIN_ROWS = NA_ROWS + NA_G
NA_BLOCKS = GRID_ROWS // NA_G
NA_VARIANTS = 3


def _na_block_start(g):
    return jnp.clip(NA_G * g - NA_ROWS // 2, 0, GRID_ROWS - NA_WIN_ROWS)


def _na_kernel(q_ref, k_ref, v_ref, ck_ref, cv_ref, bias_ref, o_ref):
    g = pl.program_id(1)
    start = pl.multiple_of(_na_block_start(g) * GRID_W, GRID_W)
    win = pl.ds(start, NA_WIN_ROWS * GRID_W)
    for jp in range(NA_HEADS // 2):
        sl = slice(LANES * jp, LANES * (jp + 1))
        q_stack = _stack_heads({jp: q_ref[:, sl] * ATTN_SCALE}, (2 * jp, 2 * jp + 1))
        bias = bias_ref[jp]
        o = _attend(q_stack, [(k_ref[win, sl].astype(BF16), v_ref[win, sl].astype(BF16), lambda s: s + bias),
                              (ck_ref[:, sl].astype(BF16), cv_ref[:, sl].astype(BF16), None)])
        o_ref[:, sl] = _merge_pair(o[:NA_Q], o[NA_Q:]).astype(BF16)


def _lat_neighbourhood(l, qn, kn, vn, cache_k, cache_v, bias_tab):
    kv_spec = pl.BlockSpec((DEC_SEQ, NA_W), lambda b, g: (b, 0))
    cache_spec = pl.BlockSpec((None, None, PAST_LEN, NA_W), lambda b, g: (b, l, 0, 0))
    variant = lambda g: jnp.where(g == 0, 0, jnp.where(g == NA_BLOCKS - 1, 2, 1))
    return pl.pallas_call(
        _na_kernel,
        grid=(DEC_BATCH, NA_BLOCKS),
        in_specs=[pl.BlockSpec((NA_Q, NA_W), lambda b, g: (N_CTX // NA_Q + b * NA_BLOCKS + g, 0)),
                  kv_spec, kv_spec, cache_spec, cache_spec,
                  pl.BlockSpec((None, None, NA_HEADS // 2, 2 * NA_Q, NA_WIN_ROWS * GRID_W),
                               lambda b, g: (l, variant(g), 0, 0, 0))],
        out_specs=pl.BlockSpec((NA_Q, NA_W), lambda b, g: (b * NA_BLOCKS + g, 0)),
        out_shape=jax.ShapeDtypeStruct((N_LAT, NA_W), BF16),
        compiler_params=_params("arbitrary", "arbitrary"),
        name=f"lat_neighbourhood{l}",
    )(qn, kn, vn, cache_k, cache_v, bias_tab)


OUTPROJ_ROWS = 256


def _outproj_kernel(xc_ref, xl_ref, mc_ref, mf_ref, mw_ref, mn_ref, mod_ref, g_ref, w_ref, r_hi, r_lo,
                    x_ref, h_ref, lg_ref, wb_ref):
    i = pl.program_id(0)

    @pl.when(i == 0)
    def _():
        wb_ref[...] = w_ref[...].astype(BF16)

    d = D_MODEL

    def block(x_in_ref, mixed_rows):
        for r0 in range(0, TM, OUTPROJ_ROWS):
            rows = slice(r0, r0 + OUTPROJ_ROWS)
            x = x_in_ref[rows, :] + mod_ref[:, 2 * d:3 * d] * _dot(mixed_rows(rows), wb_ref[...])
            x_ref[rows, :] = x
            h = _rms_mod(x, g_ref[...], mod_ref[:, 3 * d:4 * d], mod_ref[:, 4 * d:5 * d])
            h_ref[rows, :] = h.astype(BF16)
            h_hi, h_lo = _split(h)
            lg_ref[:, rows] = _dot3(h_hi, h_lo, r_hi[...], r_lo[...]).T[0:N_EXPERTS, :]

    @pl.when(i < N_CTX // TM)
    def _():
        block(xc_ref, lambda rows: mc_ref[rows, :])

    @pl.when(i >= N_CTX // TM)
    def _():
        block(xl_ref, lambda rows: jnp.concatenate([mf_ref[rows, :], mw_ref[rows, :], mn_ref[rows, :]], axis=1))


def _outproj(l, x_ctx, x_lat, mixed_ctx, mixed_f, mixed_w, mixed_n, mod4, g_ffn, w_out, r_hi, r_lo):
    n_ctx_blocks = N_CTX // TM
    xc_spec, xl_spec = _two_stream_specs(D_MODEL)
    lat = lambda w: pl.BlockSpec((TM, w), lambda i: (jnp.maximum(i - n_ctx_blocks, 0), 0))
    whole = lambda shape: pl.BlockSpec(shape, lambda i: (0,) * len(shape))
    row = lambda w: pl.BlockSpec((TM, w), lambda i: (i, 0))
    return pl.pallas_call(
        _outproj_kernel,
        grid=(N_TOK // TM,),
        in_specs=[xc_spec, xl_spec,
                  pl.BlockSpec((TM, D_MODEL), lambda i: (jnp.minimum(i, n_ctx_blocks - 1), 0)),
                  lat(F_WIDTH), lat(WIN_Q), lat(NA_W),
                  pl.BlockSpec((None, None, 1, 6 * D_MODEL), lambda i: (l, _cond_of_block(i), 0, 0)),
                  pl.BlockSpec((None, 1, D_MODEL), lambda i: (l, 0, 0)),
                  pl.BlockSpec((None, D_MODEL, D_MODEL), lambda i: (l, 0, 0)),
                  whole((D_MODEL, LANES)), whole((D_MODEL, LANES))],
        out_specs=[row(D_MODEL), row(D_MODEL), pl.BlockSpec((N_EXPERTS, TM), lambda i: (0, i))],
        out_shape=[jax.ShapeDtypeStruct((N_TOK, D_MODEL), F32),
                   jax.ShapeDtypeStruct((N_TOK, D_MODEL), BF16),
                   jax.ShapeDtypeStruct((N_EXPERTS, N_TOK), F32)],
        scratch_shapes=[pltpu.VMEM((D_MODEL, D_MODEL), BF16)],
        compiler_params=_params("arbitrary"),
        name=f"outproj{l}",
    )(x_ctx, x_lat, mixed_ctx, mixed_f, mixed_w, mixed_n, mod4, g_ffn.reshape(DEPTH, 1, D_MODEL), w_out,
      r_hi, r_lo)


PREFIX_CHUNK = 256
MANTISSA_STEPS = 44


def _prefix_exclusive(m):
    rows, n = m.shape
    t0 = lax.broadcasted_iota(jnp.int32, (PREFIX_CHUNK, PREFIX_CHUNK), 0)
    t1 = lax.broadcasted_iota(jnp.int32, (PREFIX_CHUNK, PREFIX_CHUNK), 1)
    upper = jnp.where(t0 < t1, 1.0, 0.0).astype(BF16)
    carry = jnp.zeros((rows, 1), F32)
    outs = []
    for c in range(n // PREFIX_CHUNK):
        blk = m[:, c * PREFIX_CHUNK:(c + 1) * PREFIX_CHUNK]
        outs.append(_dot(blk.astype(BF16), upper) + carry)
        carry = carry + blk.sum(axis=-1, keepdims=True)
    return outs[0] if len(outs) == 1 else jnp.concatenate(outs, axis=-1)


ROUTE_STREAMS = ((0, BATCH, CAP_CTX), (N_CTX, DEC_BATCH, CAP_LAT))


def _route_kernel(lg_ref, *out_refs):
    affs, caps = [], []
    for t0, groups, cap in ROUTE_STREAMS:
        n = (N_CTX if t0 == 0 else N_LAT) // groups
        x = jnp.concatenate([lg_ref[:, t0 + g * n:t0 + (g + 1) * n] for g in range(groups)], axis=0)
        x = x.reshape(groups, N_EXPERTS, n)
        e = jnp.exp(x - x.max(axis=1, keepdims=True))
        affs.append((e / e.sum(axis=1, keepdims=True)).reshape(groups * N_EXPERTS, n))
        caps.append(float(cap))

    def count_ge(aff, t):
        return jnp.where(aff >= t, 1.0, 0.0).sum(axis=-1, keepdims=True)

    def keep_if_enough(aff, cap, cand, otherwise):
        return jnp.where(count_ge(aff, cand) >= cap, cand, otherwise)

    above = [jnp.full((aff.shape[0], 1), 2.0, F32) for aff in affs]
    for s in (64, 32, 16, 8, 4, 2, 1):
        cands = [a * (2.0 ** -s) for a in above]
        above = [jnp.where(count_ge(aff, c) >= cap, a, c) for aff, cap, a, c in zip(affs, caps, above, cands)]
    bases = [keep_if_enough(aff, cap, a * 0.5, 0.0) for aff, cap, a in zip(affs, caps, above)]

    def refine(_, carry):
        thrs, incs = carry
        thrs = tuple(keep_if_enough(aff, cap, t + i, t) for aff, cap, t, i in zip(affs, caps, thrs, incs))
        return thrs, tuple(i * 0.5 for i in incs)

    thrs, _ = lax.fori_loop(0, MANTISSA_STEPS, refine, (tuple(bases), tuple(b * 0.5 for b in bases)))

    for k, ((_, groups, _), aff, cap, thr) in enumerate(zip(ROUTE_STREAMS, affs, caps, thrs)):
        slot_ref, aff_ref, slott_ref = out_refs[3 * k:3 * k + 3]
        n = aff.shape[1]
        gt = jnp.where(aff > thr, 1.0, 0.0)
        eq = jnp.where(aff == thr, 1.0, 0.0)
        need = cap - gt.sum(axis=-1, keepdims=True)
        sel = gt + eq * jnp.where(_prefix_exclusive(eq) < need, 1.0, 0.0)
        slot = jnp.where(sel > 0.0, _prefix_exclusive(sel), -1.0)
        slot_ref[...] = slot.astype(jnp.int32)
        aff_ref[...] = aff
        unused = jnp.full((LANES - N_EXPERTS, n), -1.0, F32)
        for g in range(groups):
            tile = jnp.concatenate([slot[g * N_EXPERTS:(g + 1) * N_EXPERTS], unused], axis=0)
            slott_ref[g * n:(g + 1) * n, :] = tile.T.astype(jnp.int32)


def _route(lg_t):
    shapes = []
    for t0, groups, _ in ROUTE_STREAMS:
        n_tok = N_CTX if t0 == 0 else N_LAT
        rows, n = groups * N_EXPERTS, n_tok // groups
        shapes += [((rows, n), jnp.int32), ((rows, n), F32), ((n_tok, LANES), jnp.int32)]
    return pl.pallas_call(
        _route_kernel,
        grid=(1,),
        in_specs=[pl.BlockSpec((N_EXPERTS, N_TOK), lambda i: (0, 0))],
        out_specs=[pl.BlockSpec(shape, lambda i: (0, 0)) for shape, _ in shapes],
        out_shape=[jax.ShapeDtypeStruct(shape, dt) for shape, dt in shapes],
        compiler_params=_params("arbitrary"),
        name="route",
    )(lg_t)


CTX_GROUP = 4


def _gather_ctx_kernel(h_ref, slot_ref, aff_ref, x_ref, g_ref, p_ref):
    s_iota = lax.broadcasted_iota(jnp.int32, (CAP_CTX, SEQ), 0)
    for bb in range(CTX_GROUP):
        slots = slice(bb * CAP_CTX, (bb + 1) * CAP_CTX)
        for e in range(N_EXPERTS):
            row = bb * N_EXPERTS + e
            hit = s_iota == slot_ref[row:row + 1, :]
            p_ref[bb, e * CAP_CTX:(e + 1) * CAP_CTX, :] = jnp.where(hit, 1.0, 0.0).astype(BF16)
            g_ref[e, slots] = jnp.where(hit, aff_ref[row:row + 1, :], 0.0).sum(axis=-1, keepdims=True)
        x = _dot(p_ref[bb], h_ref[bb * SEQ:(bb + 1) * SEQ, :]).astype(BF16)
        x_ref[:, slots, :] = x.reshape(N_EXPERTS, CAP_CTX, D_MODEL)


def _gather_ctx(h, slot, aff):
    return pl.pallas_call(
        _gather_ctx_kernel,
        grid=(BATCH // CTX_GROUP,),
        in_specs=[pl.BlockSpec((CTX_GROUP * SEQ, D_MODEL), lambda b: (b, 0)),
                  pl.BlockSpec((CTX_GROUP * N_EXPERTS, SEQ), lambda b: (b, 0)),
                  pl.BlockSpec((CTX_GROUP * N_EXPERTS, SEQ), lambda b: (b, 0))],
        out_specs=[pl.BlockSpec((N_EXPERTS, CTX_GROUP * CAP_CTX, D_MODEL), lambda b: (0, b, 0)),
                   pl.BlockSpec((N_EXPERTS, CTX_GROUP * CAP_CTX, 1), lambda b: (0, b, 0))],
        out_shape=[jax.ShapeDtypeStruct((N_EXPERTS, ROWS_CTX, D_MODEL), BF16),
                   jax.ShapeDtypeStruct((N_EXPERTS, ROWS_CTX, 1), F32)],
        scratch_shapes=[pltpu.VMEM((CTX_GROUP, N_EXPERTS * CAP_CTX, SEQ), BF16)],
        compiler_params=_params("arbitrary"),
        name="gather_ctx",
    )(h, slot, aff)


LAT_GATHER_EXPERTS = 2


def _gather_lat_kernel(h_ref, slot_ref, aff_ref, x_ref, g_ref):
    s_iota = lax.broadcasted_iota(jnp.int32, (CAP_LAT, DEC_SEQ), 0)
    for k in range(LAT_GATHER_EXPERTS):
        e = pl.program_id(1) * LAT_GATHER_EXPERTS + k
        hit = s_iota == slot_ref[pl.ds(e, 1), :]
        x_ref[k] = _dot(jnp.where(hit, 1.0, 0.0).astype(BF16), h_ref[...]).astype(BF16)
        g_ref[k] = jnp.where(hit, aff_ref[pl.ds(e, 1), :], 0.0).sum(axis=-1, keepdims=True)


def _gather_lat(h, slot, aff):
    lat_blk0 = N_CTX // DEC_SEQ
    return pl.pallas_call(
        _gather_lat_kernel,
        grid=(DEC_BATCH, N_EXPERTS // LAT_GATHER_EXPERTS),
        in_specs=[pl.BlockSpec((DEC_SEQ, D_MODEL), lambda b, e: (lat_blk0 + b, 0)),
                  pl.BlockSpec((N_EXPERTS, DEC_SEQ), lambda b, e: (b, 0)),
                  pl.BlockSpec((N_EXPERTS, DEC_SEQ), lambda b, e: (b, 0))],
        out_specs=[pl.BlockSpec((LAT_GATHER_EXPERTS, CAP_LAT, D_MODEL), lambda b, e: (e, b, 0)),
                   pl.BlockSpec((LAT_GATHER_EXPERTS, CAP_LAT, 1), lambda b, e: (e, b, 0))],
        out_shape=[jax.ShapeDtypeStruct((N_EXPERTS, ROWS_LAT, D_MODEL), BF16),
                   jax.ShapeDtypeStruct((N_EXPERTS, ROWS_LAT, 1), F32)],
        compiler_params=_params("arbitrary", "arbitrary"),
        name="gather_lat",
    )(h, slot, aff)


N_FF_CHUNKS = D_FF // TF
CHUNKS_PER_STEP = 4
FF_BLOCK = CHUNKS_PER_STEP * TF
N_FF_STEPS = -(-N_FF_CHUNKS // CHUNKS_PER_STEP)
LAST_STEP_CHUNKS = N_FF_CHUNKS - (N_FF_STEPS - 1) * CHUNKS_PER_STEP


def _silu_tanh(x):
    return x * (0.5 + 0.5 * jnp.tanh(0.5 * x))


def _ffn_kernel(xc_ref, xl_ref, gc_ref, gl_ref, wg_ref, wu_ref, wd_ref, y_ref, x_sc, h_sc, acc_sc):
    j = pl.program_id(1)

    @pl.when(j == 0)
    def _():
        x_sc[0:ROWS_CTX, :] = xc_ref[...]
        x_sc[ROWS_CTX:, :] = xl_ref[...]

    def step(n_chunks, first, last):
        x = x_sc[...]
        for sub in range(n_chunks):
            cols = slice(sub * TF, (sub + 1) * TF)
            a = _dot(x, wg_ref[:, cols].astype(BF16))
            u = _dot(x, wu_ref[:, cols].astype(BF16))
            h_sc[sub] = (_silu_tanh(a) * u).astype(BF16)
        for c0 in range(0, D_MODEL, TD):
            out_cols = slice(c0, c0 + TD)
            part = None
            for sub in range(n_chunks):
                t = _dot(h_sc[sub], wd_ref[sub * TF:(sub + 1) * TF, out_cols].astype(BF16))
                part = t if part is None else part + t
            if not first:
                part = acc_sc[:, out_cols] + part
            if last:
                y_ref[0:ROWS_CTX, out_cols] = (part[0:ROWS_CTX] * gc_ref[...]).astype(BF16)
                y_ref[ROWS_CTX:, out_cols] = (part[ROWS_CTX:] * gl_ref[...]).astype(BF16)
            else:
                acc_sc[:, out_cols] = part

    pl.when(j == 0)(functools.partial(step, CHUNKS_PER_STEP, True, False))
    pl.when((j > 0) & (j < N_FF_STEPS - 1))(functools.partial(step, CHUNKS_PER_STEP, False, False))
    pl.when(j == N_FF_STEPS - 1)(functools.partial(step, LAST_STEP_CHUNKS, False, True))


def _ffn(l, x_c, x_l, g_c, g_l, w_gate, w_up, w_down):
    rows = ROWS_CTX + ROWS_LAT
    xin = lambda r: pl.BlockSpec((None, r, D_MODEL), lambda e, j: (e, 0, 0))
    gin = lambda r: pl.BlockSpec((None, r, 1), lambda e, j: (e, 0, 0))
    return pl.pallas_call(
        _ffn_kernel,
        grid=(N_EXPERTS, N_FF_STEPS),
        in_specs=[xin(ROWS_CTX), xin(ROWS_LAT), gin(ROWS_CTX), gin(ROWS_LAT),
                  pl.BlockSpec((None, None, D_MODEL, FF_BLOCK), lambda e, j: (l, e, 0, j)),
                  pl.BlockSpec((None, None, D_MODEL, FF_BLOCK), lambda e, j: (l, e, 0, j)),
                  pl.BlockSpec((None, None, FF_BLOCK, D_MODEL), lambda e, j: (l, e, j, 0))],
        out_specs=pl.BlockSpec((None, rows, D_MODEL), lambda e, j: (e, 0, 0)),
        out_shape=jax.ShapeDtypeStruct((N_EXPERTS, rows, D_MODEL), BF16),
        scratch_shapes=[pltpu.VMEM((rows, D_MODEL), BF16), pltpu.VMEM((CHUNKS_PER_STEP, rows, TF), BF16),
                        pltpu.VMEM((rows, D_MODEL), F32)],
        compiler_params=_params("arbitrary", "arbitrary"),
        name=f"experts{l}",
    )(x_c, x_l, g_c, g_l, w_gate, w_up, w_down)


def _finish(x, res, mod_ref, gf_ref, final):
    y = x + mod_ref[:, 5 * D_MODEL:] * res
    if final:
        y = y * lax.rsqrt(jnp.mean(y * y, axis=-1, keepdims=True) + RMS_EPS) * gf_ref[...]
    return y


def _combine_ctx_kernel(final, x_ref, y_ref, slot_ref, rep_ref, mod_ref, gf_ref, o_ref):
    n_col = N_EXPERTS * CAP_CTX
    col = (lax.broadcasted_iota(jnp.int32, (SEQ, n_col), 1) % CAP_CTX).astype(F32)
    for bb in range(CTX_GROUP):
        rows = slice(bb * SEQ, (bb + 1) * SEQ)
        spread = _dot(slot_ref[rows, :].astype(F32).astype(BF16), rep_ref[...])
        p = jnp.where(spread == col, 1.0, 0.0).astype(BF16)
        y = y_ref[:, bb * CAP_CTX:(bb + 1) * CAP_CTX, :].reshape(n_col, D_MODEL)
        o_ref[rows, :] = _finish(x_ref[rows, :], _dot(p, y), mod_ref, gf_ref, final)


def _combine_ctx(l, final, x_new, y, slot_t, mod4, g_final):
    n_col = N_EXPERTS * CAP_CTX
    rep = (np.arange(n_col)[None, :] // CAP_CTX == np.arange(LANES)[:, None]).astype(np.float32)
    return pl.pallas_call(
        functools.partial(_combine_ctx_kernel, final),
        grid=(BATCH // CTX_GROUP,),
        in_specs=[pl.BlockSpec((CTX_GROUP * SEQ, D_MODEL), lambda b: (b, 0)),
                  pl.BlockSpec((N_EXPERTS, CTX_GROUP * CAP_CTX, D_MODEL), lambda b: (0, b, 0)),
                  pl.BlockSpec((CTX_GROUP * SEQ, LANES), lambda b: (b, 0)),
                  pl.BlockSpec((LANES, n_col), lambda b: (0, 0)),
                  pl.BlockSpec((None, None, 1, 6 * D_MODEL), lambda b: (l, 0, 0, 0)),
                  pl.BlockSpec((1, D_MODEL), lambda b: (0, 0))],
        out_specs=pl.BlockSpec((CTX_GROUP * SEQ, D_MODEL), lambda b: (b, 0)),
        out_shape=jax.ShapeDtypeStruct((N_CTX, D_MODEL), F32),
        compiler_params=_params("arbitrary"),
        name=f"combine_ctx{l}",
    )(x_new, y, slot_t, jnp.asarray(rep, BF16), mod4, g_final.reshape(1, D_MODEL))


TMC = 512


def _combine_lat_kernel(final, x_ref, y_ref, slot_ref, mod_ref, gf_ref, o_ref):
    s_iota = lax.broadcasted_iota(jnp.int32, (TMC, CAP_LAT), 1)
    slot = slot_ref[...]
    res = None
    for e in range(N_EXPERTS):
        p = jnp.where(slot[:, e:e + 1] == s_iota, 1.0, 0.0).astype(BF16)
        t = _dot(p, y_ref[e])
        res = t if res is None else res + t
    o_ref[...] = _finish(x_ref[...], res, mod_ref, gf_ref, final)


def _combine_lat(l, final, x_new, y, slot_t, mod4, g_final):
    nt = DEC_SEQ // TMC
    return pl.pallas_call(
        functools.partial(_combine_lat_kernel, final),
        grid=(DEC_BATCH, nt),
        in_specs=[pl.BlockSpec((TMC, D_MODEL), lambda b, t: (N_CTX // TMC + b * nt + t, 0)),
                  pl.BlockSpec((N_EXPERTS, CAP_LAT, D_MODEL), lambda b, t: (0, ROWS_CTX // CAP_LAT + b, 0)),
                  pl.BlockSpec((TMC, LANES), lambda b, t: (b * nt + t, 0)),
                  pl.BlockSpec((None, None, 1, 6 * D_MODEL), lambda b, t: (l, 1 + b, 0, 0)),
                  pl.BlockSpec((1, D_MODEL), lambda b, t: (0, 0))],
        out_specs=pl.BlockSpec((TMC, D_MODEL), lambda b, t: (b * nt + t, 0)),
        out_shape=jax.ShapeDtypeStruct((N_LAT, D_MODEL), F32),
        compiler_params=_params("arbitrary", "arbitrary"),
        name=f"combine_lat{l}",
    )(x_new, y, slot_t, mod4, g_final.reshape(1, D_MODEL))


def _split_table(t):
    hi = t.astype(BF16)
    return hi, (t - hi.astype(F32)).astype(BF16)


def _dft_tables(n):
    p = np.arange(n, dtype=np.int64)
    ang = ((p[:, None] * p[None, :]) % n).astype(np.float64) * (2.0 * np.pi / n)
    return np.cos(ang).astype(np.float32), np.sin(ang).astype(np.float32)


def _channel_dft_tables():
    c = np.arange(F_WIDTH, dtype=np.int64)
    same = (c[:, None] // HEAD_DIM) == (c[None, :] // HEAD_DIM)
    ang = (((c[:, None] % HEAD_DIM) * (c[None, :] % HEAD_DIM)) % HEAD_DIM).astype(np.float64) * (2.0 * np.pi / HEAD_DIM)
    return (np.where(same, np.cos(ang), 0.0).astype(np.float32),
            np.where(same, np.sin(ang), 0.0).astype(np.float32))


def _rope_tables():
    half = HEAD_DIM // 2
    nf = half // 2
    pos = np.arange(DEC_SEQ)
    inv = 1.0 / (ROPE_BASE ** (np.arange(nf, dtype=np.float64) / nf))
    ang_r = (pos // GRID_W).astype(np.float64)[:, None] * inv
    ang_c = (pos % GRID_W).astype(np.float64)[:, None] * inv

    def head(fn, sign):
        return np.concatenate([sign * fn(ang_r), fn(ang_r), sign * fn(ang_c), fn(ang_c)], axis=-1)

    cos = head(np.cos, 1.0)
    sin = head(np.sin, -1.0)
    return (np.concatenate([cos, cos], axis=-1).astype(np.float32),
            np.concatenate([sin, sin], axis=-1).astype(np.float32))


def _na_bias_tables(rpb):
    cq = np.arange(GRID_W)
    rel_c = np.clip(cq[None, :] - cq[:, None] + NA_COLS - 1, 0, 2 * NA_COLS - 2)
    pick = (rel_c[:, :, None] == np.arange(2 * NA_COLS - 1)).astype(np.float32)
    cs = np.clip(cq - NA_COLS // 2, 0, GRID_W - NA_COLS)
    col_ok = (cq[None, :] >= cs[:, None]) & (cq[None, :] < cs[:, None] + NA_COLS)
    bc = jnp.einsum('lhrj,qkj->lhrqk', rpb, pick, precision=lax.Precision.HIGHEST)
    bc = jnp.where(col_ok[None, None, None], bc, NEG_INF)
    bc = jnp.pad(bc, ((0, 0), (0, 0), (1, 1), (0, 0), (0, 0)), constant_values=NEG_INF)
    bc = jnp.concatenate([bc[:, :, :-1], bc[:, :, 1:]], axis=-1)
    return pl.pallas_call(
        _na_bias_kernel,
        grid=(DEPTH, NA_VARIANTS, NA_HEADS // 2),
        in_specs=[pl.BlockSpec((None, 2, 2 * NA_ROWS, GRID_W, 2 * GRID_W), lambda l, v, p: (l, p, 0, 0, 0))],
        out_specs=pl.BlockSpec((None, None, None, 2 * NA_Q, NA_WIN_ROWS * GRID_W),
                               lambda l, v, p: (l, v, p, 0, 0)),
        out_shape=jax.ShapeDtypeStruct((DEPTH, NA_VARIANTS, NA_HEADS // 2, 2 * NA_Q, NA_WIN_ROWS * GRID_W), F32),
        compiler_params=_params("arbitrary", "arbitrary", "arbitrary"),
        name="na_bias",
    )(bc)


def _na_window_plan():
    plan = []
    for g in (0, 1, NA_BLOCKS - 1):
        start = int(np.clip(NA_G * g - NA_ROWS // 2, 0, GRID_ROWS - NA_WIN_ROWS))
        rows = []
        for a in range(NA_G):
            r = NA_G * g + a
            rs = int(np.clip(r - NA_ROWS // 2, 0, GRID_ROWS - NA_ROWS))
            rows.append([start + w - r + NA_ROWS - 1 if rs <= start + w < rs + NA_ROWS else None
                         for w in range(NA_WIN_ROWS)])
        plan.append(rows)
    return plan


def _na_bias_kernel(bc_ref, o_ref):
    outside = jnp.full((GRID_W, 2 * GRID_W), NEG_INF, F32)
    low = _lane_is_low(outside.shape)

    def pair_tile(half, ra, rb):
        if ra is None and rb is None:
            return outside
        if rb is None:
            return jnp.where(low, bc_ref[half, ra + 1], NEG_INF)
        if ra is None:
            return jnp.where(low, NEG_INF, bc_ref[half, rb])
        return bc_ref[half, ra + 1]

    for v, rows in enumerate(_na_window_plan()):
        @pl.when(pl.program_id(1) == v)
        def _():
            for half in range(2):
                for a, rel in enumerate(rows):
                    tiles = [pair_tile(half, rel[w], rel[w + 1]) for w in range(0, NA_WIN_ROWS, 2)]
                    r0 = half * NA_Q + a * GRID_W
                    o_ref[r0:r0 + GRID_W, :] = jnp.concatenate(tiles, axis=-1)


def kernel(x_prompt, x_sample, cache_win_k, cache_win_v, cache_nat_k, cache_nat_v, c, c_ctx, w_mod, b_mod, g_mix, g_ffn, w_in, w_out, win_sink, nat_rpb, w_router, w_gate, w_up, w_down, g_final):
    x_ctx = x_prompt.reshape(N_CTX, D_MODEL)
    x_lat = x_sample.reshape(N_LAT, D_MODEL)
    cond = jnp.concatenate([c_ctx[None, :], c, jnp.zeros((N_COND - 1 - DEC_BATCH, D_MODEL), F32)], axis=0)
    mod4 = _adaln(cond, w_mod, b_mod).reshape(DEPTH, N_COND, 1, 6 * D_MODEL)

    dft_ch = _channel_dft_tables()
    dft_ctx = _dft_tables(SEQ)
    dft_lat = _dft_tables(DEC_SEQ)
    cos_t, sin_t = _rope_tables()
    cwk = cache_win_k.reshape(DEC_BATCH, DEPTH, PAST_LEN, WIN_KV)
    cwv = cache_win_v.reshape(DEC_BATCH, DEPTH, PAST_LEN, WIN_KV)
    cnk = cache_nat_k.reshape(DEC_BATCH, DEPTH, PAST_LEN, NA_W)
    cnv = cache_nat_v.reshape(DEC_BATCH, DEPTH, PAST_LEN, NA_W)
    r_pad = jnp.pad(w_router, ((0, 0), (0, 0), (0, LANES - N_EXPERTS)))
    na_bias = _na_bias_tables(nat_rpb)

    kvt = ()
    for l in range(DEPTH):
        final = l == DEPTH - 1
        (f, qw, qn), (kw, vw, kn, vn), kvt = _project(l, x_ctx, x_lat, mod4, g_mix, w_in, kvt)

        mixed_ctx = _ctx_mixer(l, win_sink, f, qw, qn, kvt, dft_ch, dft_ctx)
        mixed_f = _lat_fourier(f, dft_ch, dft_lat)
        mixed_w = _lat_window(l, win_sink, qw, kw, vw, cwk, cwv, cos_t, sin_t)
        mixed_n = _lat_neighbourhood(l, qn, kn, vn, cnk, cnv, na_bias)

        r_hi, r_lo = _split_table(r_pad[l])
        x_new, h, lg_t = _outproj(l, x_ctx, x_lat, mixed_ctx, mixed_f, mixed_w, mixed_n, mod4, g_ffn,
                                  w_out, r_hi, r_lo)

        slot_c, aff_c, slot_ct, slot_l, aff_l, slot_lt = _route(lg_t)
        xg_c, gate_c = _gather_ctx(h, slot_c, aff_c)
        xg_l, gate_l = _gather_lat(h, slot_l, aff_l)
        y = _ffn(l, xg_c, xg_l, gate_c, gate_l, w_gate, w_up, w_down)
        x_ctx = _combine_ctx(l, final, x_new, y, slot_ct, mod4, g_final)
        x_lat = _combine_lat(l, final, x_new, y, slot_lt, mod4, g_final)

    y_prompt = x_ctx.reshape(BATCH, SEQ, D_MODEL)
    y_sample = x_lat.reshape(DEC_BATCH, DEC_SEQ, D_MODEL)
    new_kv = [t.reshape(BATCH, DEPTH, w // HEAD_DIM, HEAD_DIM, SEQ).transpose(0, 1, 4, 2, 3)
              for t, w in zip(kvt, KV_WIDTHS)]
    return (y_prompt, y_sample, *new_kv)
```

```python
import functools

import numpy as np
import jax
import jax.numpy as jnp
from jax import lax
from jax.experimental import pallas as pl
from jax.experimental.pallas import tpu as pltpu

D_MODEL = 1024
BATCH = 16
SEQ = 256
DEPTH = 2
DEC_BATCH = 2
DEC_SEQ = 2048
PAST_LEN = 256
GRID_W = 64
HEAD_DIM = 64
F_WIDTH = 256
WIN_HEADS = 6
WIN_KV_HEADS = 2
WINDOW = 128
NA_HEADS = 6
NA_ROWS = 8
NA_COLS = 16
N_EXPERTS = 16
EC_CAPACITY = 2
D_FF = 2816
ROPE_BASE = 10000.0
RMS_EPS = 1e-6
NEG_INF = -1e30
ATTN_SCALE = HEAD_DIM ** -0.5
WIN_Q = WIN_HEADS * HEAD_DIM
WIN_KV = WIN_KV_HEADS * HEAD_DIM
NA_W = NA_HEADS * HEAD_DIM
N_IN = F_WIDTH + WIN_Q + 2 * WIN_KV + 3 * NA_W
SPLITS = (0, F_WIDTH, F_WIDTH + WIN_Q, F_WIDTH + WIN_Q + WIN_KV, F_WIDTH + WIN_Q + 2 * WIN_KV,
          F_WIDTH + WIN_Q + 2 * WIN_KV + NA_W, F_WIDTH + WIN_Q + 2 * WIN_KV + 2 * NA_W, N_IN)

N_CTX = BATCH * SEQ
N_LAT = DEC_BATCH * DEC_SEQ
N_TOK = N_CTX + N_LAT
GRID_ROWS = DEC_SEQ // GRID_W
CAP_CTX = EC_CAPACITY * SEQ // N_EXPERTS
CAP_LAT = EC_CAPACITY * DEC_SEQ // N_EXPERTS
ROWS_CTX = BATCH * CAP_CTX
ROWS_LAT = DEC_BATCH * CAP_LAT
N_COND = 8

LANES = 128
MXU_COLS = 256
TM = 512
TN_MOD = 1536
TF = 256
TD = 256
VMEM_LIMIT = 56 * 1024 * 1024

F32 = jnp.float32
BF16 = jnp.bfloat16


def _params(*sem):
    return pltpu.CompilerParams(dimension_semantics=sem, vmem_limit_bytes=VMEM_LIMIT)


def _dot(a, b):
    return jnp.dot(a, b, preferred_element_type=F32)


def _dot_nt(a, b):
    return lax.dot_general(a, b, (((1,), (1,)), ((), ())), preferred_element_type=F32)


def _split(x):
    hi = x.astype(BF16)
    lo = (x - hi.astype(F32)).astype(BF16)
    return hi, lo


def _dot3(a_hi, a_lo, b_hi, b_lo):
    return _dot(a_hi, b_hi) + (_dot(a_lo, b_hi) + _dot(a_hi, b_lo))


def _silu(x):
    return x / (1.0 + jnp.exp(-x))


def _rms_mod(x, g, shift, scale):
    y = x * lax.rsqrt(jnp.mean(x * x, axis=-1, keepdims=True) + RMS_EPS)
    return (y * g) * (1.0 + scale) + shift


def _softmax_parts(parts, sink=None):
    m = parts[0].max(axis=-1, keepdims=True)
    for s in parts[1:]:
        m = jnp.maximum(m, s.max(axis=-1, keepdims=True))
    if sink is not None:
        m = jnp.maximum(m, sink)
    es = [jnp.exp(s - m) for s in parts]
    den = es[0].sum(axis=-1, keepdims=True)
    for e in es[1:]:
        den = den + e.sum(axis=-1, keepdims=True)
    if sink is not None:
        den = den + jnp.exp(sink - m)
    return es, 1.0 / den


def _cond_of_block(i):
    n_ctx_blocks = N_CTX // TM
    return jnp.where(i < n_ctx_blocks, 0, 1 + (i - n_ctx_blocks) // (DEC_SEQ // TM))


def _two_stream_specs(width):
    n_ctx_blocks = N_CTX // TM
    ctx = pl.BlockSpec((TM, width), lambda i: (jnp.minimum(i, n_ctx_blocks - 1), 0))
    lat = pl.BlockSpec((TM, width), lambda i: (jnp.maximum(i - n_ctx_blocks, 0), 0))
    return ctx, lat


def _pick_stream(ctx_ref, lat_ref):
    return jnp.where(pl.program_id(0) < N_CTX // TM, ctx_ref[...], lat_ref[...])


def _adaln_kernel(c_ref, w_ref, b_ref, o_ref):
    s_hi, s_lo = _split(_silu(c_ref[...]))
    w_hi, w_lo = _split(w_ref[...])
    o_ref[...] = _dot3(s_hi, s_lo, w_hi, w_lo) + b_ref[...]


def _adaln(cond, w_mod, b_mod):
    return pl.pallas_call(
        _adaln_kernel,
        grid=(DEPTH, 6 * D_MODEL // TN_MOD),
        in_specs=[
            pl.BlockSpec((N_COND, D_MODEL), lambda l, j: (0, 0)),
            pl.BlockSpec((None, D_MODEL, TN_MOD), lambda l, j: (l, 0, j)),
            pl.BlockSpec((None, 1, TN_MOD), lambda l, j: (l, 0, j)),
        ],
        out_specs=pl.BlockSpec((None, N_COND, TN_MOD), lambda l, j: (l, 0, j)),
        out_shape=jax.ShapeDtypeStruct((DEPTH, N_COND, 6 * D_MODEL), F32),
        compiler_params=_params("arbitrary", "arbitrary"),
        name="adaln",
    )(cond, w_mod, b_mod.reshape(DEPTH, 1, 6 * D_MODEL))


KV_NAMES = ("kw", "vw", "kn", "vn")
KV_WIDTHS = (WIN_KV, WIN_KV, NA_W, NA_W)
KV_SPLITS = (SPLITS[2], SPLITS[3], SPLITS[5], SPLITS[6])
KV_TOTAL = sum(KV_WIDTHS)
Q_SPLITS = (SPLITS[0], SPLITS[1], SPLITS[4])


def _proj_kernel(n_prev, xc_ref, xl_ref, mod_ref, g_ref, w_ref, *rest):
    rest = rest[n_prev:]
    f_ref, qw_ref, qn_ref = rest[0:3]
    lat_kv = rest[3:7]
    ctx_kvt = rest[7:11]
    wb_ref, wt_ref = rest[11:13]
    i = pl.program_id(0)

    @pl.when(i == 0)
    def _():
        wb_ref[...] = w_ref[...].astype(BF16)
        r0 = 0
        for c0, width in zip(KV_SPLITS, KV_WIDTHS):
            wt_ref[r0:r0 + width, :] = w_ref[:, c0:c0 + width].T.astype(BF16)
            r0 += width

    natural = dict(zip(SPLITS[:-1], (f_ref, qw_ref) + tuple(lat_kv[0:2]) + (qn_ref,) + tuple(lat_kv[2:4])))

    def hidden(x_ref):
        return _rms_mod(x_ref[...], g_ref[...], mod_ref[:, 0:D_MODEL], mod_ref[:, D_MODEL:2 * D_MODEL]).astype(BF16)

    def project_tiles(h, wanted):
        for t0 in range(0, N_IN, MXU_COLS):
            hits = [(a, b) for a, b in zip(SPLITS[:-1], SPLITS[1:])
                    if a in wanted and max(a, t0) < min(b, t0 + MXU_COLS)]
            if not hits:
                continue
            acc = _dot(h, wb_ref[:, t0:t0 + MXU_COLS])
            for a, b in hits:
                lo, hi = max(a, t0), min(b, t0 + MXU_COLS)
                natural[a][:, lo - a:hi - a] = acc[:, lo - t0:hi - t0].astype(natural[a].dtype)

    @pl.when(i < N_CTX // TM)
    def _():
        h = hidden(xc_ref)
        project_tiles(h, Q_SPLITS)
        kvt = _dot_nt(wt_ref[...], h)
        r0 = 0
        for o_ref, width in zip(ctx_kvt, KV_WIDTHS):
            for bb in range(TM // SEQ):
                o_ref[bb] = kvt[r0:r0 + width, bb * SEQ:(bb + 1) * SEQ]
            r0 += width

    @pl.when(i >= N_CTX // TM)
    def _():
        project_tiles(hidden(xl_ref), Q_SPLITS + KV_SPLITS)


def _project(l, x_ctx, x_lat, mod4, g_mix, w_in, prev_kvt):
    n_ctx_blocks = N_CTX // TM
    xc_spec, xl_spec = _two_stream_specs(D_MODEL)
    both = lambda w: pl.BlockSpec((TM, w), lambda i: (i, 0))
    lat = lambda w: pl.BlockSpec((TM, w), lambda i: (jnp.maximum(i - n_ctx_blocks, 0), 0))
    ctx_t = lambda w: pl.BlockSpec((TM // SEQ, None, w, SEQ),
                                   lambda i: (jnp.minimum(i, n_ctx_blocks - 1), l, 0, 0))
    n_prev = len(prev_kvt)
    n_in = 5
    outs = pl.pallas_call(
        functools.partial(_proj_kernel, n_prev),
        grid=(N_TOK // TM,),
        in_specs=[
            xc_spec, xl_spec,
            pl.BlockSpec((None, None, 1, 6 * D_MODEL), lambda i: (l, _cond_of_block(i), 0, 0)),
            pl.BlockSpec((None, 1, D_MODEL), lambda i: (l, 0, 0)),
            pl.BlockSpec((None, D_MODEL, N_IN), lambda i: (l, 0, 0)),
        ] + [pl.BlockSpec(memory_space=pl.ANY)] * n_prev,
        out_specs=[both(F_WIDTH), both(WIN_Q), both(NA_W)] + [lat(w) for w in KV_WIDTHS]
                  + [ctx_t(w) for w in KV_WIDTHS],
        out_shape=[jax.ShapeDtypeStruct((N_TOK, w), dt) for w, dt in ((F_WIDTH, F32), (WIN_Q, F32), (NA_W, BF16))]
                  + [jax.ShapeDtypeStruct((N_LAT, w), dt)
                     for w, dt in zip(KV_WIDTHS, (F32, BF16, BF16, BF16))]
                  + [jax.ShapeDtypeStruct((BATCH, DEPTH, w, SEQ), F32) for w in KV_WIDTHS],
        input_output_aliases={n_in + k: 7 + k for k in range(n_prev)},
        scratch_shapes=[pltpu.VMEM((D_MODEL, N_IN), BF16), pltpu.VMEM((KV_TOTAL, D_MODEL), BF16)],
        compiler_params=_params("arbitrary"),
        name=f"project{l}",
    )(x_ctx, x_lat, mod4, g_mix.reshape(DEPTH, 1, D_MODEL), w_in, *prev_kvt)
    return outs[0:3], outs[3:7], outs[7:11]


def _lane_is_low(shape):
    return lax.broadcasted_iota(jnp.int32, shape, len(shape) - 1) < HEAD_DIM


def _swap_halves(x):
    return pltpu.roll(x, HEAD_DIM, axis=x.ndim - 1)


def _win_kv_copy(h):
    return 0 if (h // (WIN_HEADS // WIN_KV_HEADS)) == (h % 2) else 1


def _stack_heads(q_pairs, heads):
    low = _lane_is_low(q_pairs[heads[0] // 2].shape)
    rows = [jnp.where(low if h % 2 == 0 else jnp.logical_not(low), q_pairs[h // 2], 0.0).astype(BF16)
            for h in heads]
    return rows[0] if len(rows) == 1 else jnp.concatenate(rows, axis=0)


def _per_head_column(values, rows_per_head):
    blk = lax.broadcasted_iota(jnp.int32, (len(values) * rows_per_head, 1), 0) // rows_per_head
    col = jnp.full(blk.shape, values[0], F32)
    for i in range(1, len(values)):
        col = jnp.where(blk == i, values[i], col)
    return col


def _attend(q_stack, kv_list, extra_logit=None, transposed=False):
    scores = []
    for k, _, post in kv_list:
        s = _dot(q_stack, k) if transposed else _dot_nt(q_stack, k)
        scores.append(post(s) if post is not None else s)
    exps, inv = _softmax_parts(scores, extra_logit)
    o = None
    for e, (_, v, _) in zip(exps, kv_list):
        t = _dot_nt(e.astype(BF16), v) if transposed else _dot(e.astype(BF16), v)
        o = t if o is None else o + t
    return o * inv


def _merge_pair(o_even, o_odd):
    return jnp.where(_lane_is_low(o_even.shape), o_even, o_odd)


def _gqa_attention(q_pairs, rows, kv_for_copy, sinks, transposed=False):
    per_head = {}
    for copy in (0, 1):
        heads = [h for h in range(WIN_HEADS) if _win_kv_copy(h) == copy]
        o = _attend(_stack_heads(q_pairs, heads), kv_for_copy(copy),
                    _per_head_column([sinks[h] for h in heads], rows), transposed)
        for i, h in enumerate(heads):
            per_head[h] = o[i * rows:(i + 1) * rows]
    return [_merge_pair(per_head[2 * j], per_head[2 * j + 1]) for j in range(WIN_HEADS // 2)]


def _swap_row_halves(x):
    return jnp.concatenate([x[HEAD_DIM:], x[:HEAD_DIM]], axis=0)


MIX_GROUP = 4


def _ctx_mixer_kernel(l, sink_ref, f_ref, qw_ref, qn_ref, kw_ref, vw_ref, kn_ref, vn_ref,
                      bc_ref, bs_ref, cs_ref, ss_ref, o_ref):
    bc_hi, bc_lo = _split(bc_ref[...])
    bs_hi, bs_lo = _split(bs_ref[...])
    cs_hi, cs_lo = _split(cs_ref[...])
    ss_hi, ss_lo = _split(ss_ref[...])
    for bb in range(MIX_GROUP):
        rows = slice(bb * SEQ, (bb + 1) * SEQ)
        f_hi, f_lo = _split(f_ref[rows, :])
        fc_hi, fc_lo = _split(_dot3(f_hi, f_lo, bc_hi, bc_lo))
        fs_hi, fs_lo = _split(_dot3(f_hi, f_lo, bs_hi, bs_lo))
        z = _dot3(cs_hi, cs_lo, fc_hi, fc_lo) - _dot3(ss_hi, ss_lo, fs_hi, fs_lo)
        o_ref[rows, 0:F_WIDTH] = (z * (SEQ * HEAD_DIM) ** -0.5).astype(BF16)

        kv = [(kw_ref[bb].astype(BF16), vw_ref[bb].astype(BF16)),
              (_swap_row_halves(kw_ref[bb]).astype(BF16), _swap_row_halves(vw_ref[bb]).astype(BF16))]
        q_pairs = [qw_ref[rows, LANES * j:LANES * (j + 1)] * ATTN_SCALE for j in range(WIN_HEADS // 2)]
        outs = _gqa_attention(q_pairs, SEQ, lambda c: [(kv[c][0], kv[c][1], None)],
                              [sink_ref[l, h] for h in range(WIN_HEADS)], transposed=True)
        for j, o in enumerate(outs):
            o_ref[rows, F_WIDTH + LANES * j:F_WIDTH + LANES * (j + 1)] = o.astype(BF16)

        for j in range(NA_HEADS // 2):
            sl = slice(LANES * j, LANES * (j + 1))
            q_pairs = {j: qn_ref[rows, sl] * ATTN_SCALE}
            o = _attend(_stack_heads(q_pairs, (2 * j, 2 * j + 1)),
                        [(kn_ref[bb, sl, :].astype(BF16), vn_ref[bb, sl, :].astype(BF16), None)], transposed=True)
            base = F_WIDTH + WIN_Q + LANES * j
            o_ref[rows, base:base + LANES] = _merge_pair(o[:SEQ], o[SEQ:]).astype(BF16)


def _ctx_mixer(l, win_sink, f, qw, qn, kvt, dft_ch, dft_seq):
    row = lambda w: pl.BlockSpec((MIX_GROUP * SEQ, w), lambda b: (b, 0))
    col = lambda w: pl.BlockSpec((MIX_GROUP, None, w, SEQ), lambda b: (b, l, 0, 0))
    const = lambda n: pl.BlockSpec((n, n), lambda b: (0, 0))
    return pl.pallas_call(
        functools.partial(_ctx_mixer_kernel, l),
        grid=(BATCH // MIX_GROUP,),
        in_specs=[pl.BlockSpec(memory_space=pltpu.SMEM),
                  row(F_WIDTH), row(WIN_Q), row(NA_W)] + [col(w) for w in KV_WIDTHS]
                 + [const(F_WIDTH)] * 2 + [const(SEQ)] * 2,
        out_specs=pl.BlockSpec((MIX_GROUP * SEQ, D_MODEL), lambda b: (b, 0)),
        out_shape=jax.ShapeDtypeStruct((N_CTX, D_MODEL), BF16),
        compiler_params=_params("arbitrary"),
        name=f"ctx_mixer{l}",
    )(win_sink, f, qw, qn, *kvt, *dft_ch, *dft_seq)


FT_ROWS = 512


HALF_SEQ = DEC_SEQ // 2
SUBLANES = 8


def _lat_fourier_kernel(f_ref, mir_ref, bc_ref, bs_ref, cs_ref, ss_ref, o_ref, st_ref, mid_ref):
    r, b = pl.program_id(0), pl.program_id(1)

    @pl.when(r == 0)
    def _():
        f_hi, f_lo = _split(f_ref[pl.ds(pl.multiple_of(b * DEC_SEQ, DEC_SEQ), DEC_SEQ), :])
        fc = _dot3(f_hi, f_lo, *_split(bc_ref[...]))
        fs = _dot3(f_hi, f_lo, *_split(bs_ref[...]))

        def mirrored(t):
            hi, lo = _split(t[HALF_SEQ:])
            return _dot(mir_ref[...], hi) + _dot(mir_ref[...], lo)

        st_ref[b, 0], st_ref[b, 1] = _split(fc[:HALF_SEQ] + mirrored(fc))
        st_ref[b, 2], st_ref[b, 3] = _split(fs[:HALF_SEQ] - mirrored(fs))
        mid_ref[b] = fc[HALF_SEQ:HALF_SEQ + SUBLANES]

    z = (_dot3(*_split(cs_ref[...]), st_ref[b, 0], st_ref[b, 1])
         - _dot3(*_split(ss_ref[...]), st_ref[b, 2], st_ref[b, 3]))
    k = r * FT_ROWS + lax.broadcasted_iota(jnp.int32, (FT_ROWS, 1), 0)
    z = z + jnp.where(k % 2 == 0, 1.0, -1.0) * mid_ref[b, 0:1, :]
    o_ref[...] = (z * (DEC_SEQ * HEAD_DIM) ** -0.5).astype(BF16)


def _lat_fourier(f, dft_ch, dft_seq):
    nrb = DEC_SEQ // FT_ROWS
    p = np.arange(HALF_SEQ)
    mirror = ((p[:, None] + p[None, :] == HALF_SEQ) & (p[:, None] > 0)).astype(np.float32)
    const = pl.BlockSpec((F_WIDTH, F_WIDTH), lambda r, b: (0, 0))
    rows = pl.BlockSpec((FT_ROWS, HALF_SEQ), lambda r, b: (r, 0))
    return pl.pallas_call(
        _lat_fourier_kernel,
        grid=(nrb, DEC_BATCH),
        in_specs=[pl.BlockSpec((N_LAT, F_WIDTH), lambda r, b: (1, 0)),
                  pl.BlockSpec((HALF_SEQ, HALF_SEQ), lambda r, b: (0, 0))] + [const] * 2 + [rows] * 2,
        out_specs=pl.BlockSpec((FT_ROWS, F_WIDTH), lambda r, b: (b * nrb + r, 0)),
        out_shape=jax.ShapeDtypeStruct((N_LAT, F_WIDTH), BF16),
        scratch_shapes=[pltpu.VMEM((DEC_BATCH, 4, HALF_SEQ, F_WIDTH), BF16),
                        pltpu.VMEM((DEC_BATCH, SUBLANES, F_WIDTH), F32)],
        compiler_params=_params("arbitrary", "arbitrary"),
        name="lat_fourier",
    )(f, jnp.asarray(mirror, BF16), *dft_ch, *dft_seq)


def _rope(x, cos, sin_signed):
    n = x.shape[-1]
    lane = lax.broadcasted_iota(jnp.int32, x.shape, x.ndim - 1)
    first = (lane % 32) < 16
    partner = jnp.where(first, pltpu.roll(x, n - 16, axis=x.ndim - 1), pltpu.roll(x, 16, axis=x.ndim - 1))
    return x * cos + partner * sin_signed


WIN_QUERIES = 256
WIN_KEYS = WIN_QUERIES + 2 * WINDOW


def _win_kernel(l, sink_ref, q_ref, k_ref, v_ref, ck_ref, cv_ref, cos_ref, sin_ref, cosq_ref, sinq_ref,
                o_ref, kp_ref, vp_ref, cp_ref):
    n = pl.program_id(1)
    nb = DEC_SEQ // WIN_QUERIES
    pad = WINDOW

    @pl.when(n == 0)
    def _():
        zeros = jnp.zeros((pad, LANES), BF16)
        kr = _rope(k_ref[...], cos_ref[...], sin_ref[...])
        v = v_ref[...]
        for idx, (kk, vv) in enumerate(((kr, v), (_swap_halves(kr), _swap_halves(v)))):
            kp_ref[idx, 0:pad] = zeros
            kp_ref[idx, pad + DEC_SEQ:] = zeros
            kp_ref[idx, pad:pad + DEC_SEQ] = kk.astype(BF16)
            vp_ref[idx, 0:pad] = zeros
            vp_ref[idx, pad + DEC_SEQ:] = zeros
            vp_ref[idx, pad:pad + DEC_SEQ] = vv.astype(BF16)
        ck = ck_ref[...]
        cv = cv_ref[...]
        cp_ref[0] = ck.astype(BF16)
        cp_ref[1] = _swap_halves(ck).astype(BF16)
        cp_ref[2] = cv.astype(BF16)
        cp_ref[3] = _swap_halves(cv).astype(BF16)

    lo = jnp.where(n == 0, WINDOW, 0)
    hi = jnp.where(n == nb - 1, WIN_QUERIES + WINDOW, WIN_KEYS)

    i = lax.broadcasted_iota(jnp.int32, (WIN_QUERIES, WIN_KEYS), 0)
    j = lax.broadcasted_iota(jnp.int32, (WIN_QUERIES, WIN_KEYS), 1)
    mask = (j >= i) & (j <= i + 2 * WINDOW) & (j >= lo) & (j < hi)
    band_bias = jnp.where(mask, 0.0, NEG_INF)

    def band(s):
        heads = s.shape[0] // WIN_QUERIES
        return (s.reshape(heads, WIN_QUERIES, s.shape[1]) + band_bias[None]).reshape(s.shape)

    start = pl.multiple_of(n * WIN_QUERIES, WIN_QUERIES)
    win = pl.ds(start, WIN_KEYS)
    q_pairs = [_rope(q_ref[:, LANES * jp:LANES * (jp + 1)], cosq_ref[...], sinq_ref[...]) * ATTN_SCALE
               for jp in range(WIN_HEADS // 2)]
    outs = _gqa_attention(
        q_pairs, WIN_QUERIES,
        lambda c: [(kp_ref[c, win, :], vp_ref[c, win, :], band), (cp_ref[c], cp_ref[2 + c], None)],
        [sink_ref[l, h] for h in range(WIN_HEADS)])
    for jp, o in enumerate(outs):
        o_ref[:, LANES * jp:LANES * (jp + 1)] = o.astype(BF16)


def _lat_window(l, win_sink, qw, kw, vw, cache_k, cache_v, cos_t, sin_t):
    nb = DEC_SEQ // WIN_QUERIES
    kv_spec = pl.BlockSpec((DEC_SEQ, WIN_KV), lambda b, n: (b, 0))
    cache_spec = pl.BlockSpec((None, None, PAST_LEN, WIN_KV), lambda b, n: (b, l, 0, 0))
    tab_all = pl.BlockSpec((DEC_SEQ, LANES), lambda b, n: (0, 0))
    tab_blk = pl.BlockSpec((WIN_QUERIES, LANES), lambda b, n: (n, 0))
    return pl.pallas_call(
        functools.partial(_win_kernel, l),
        grid=(DEC_BATCH, nb),
        in_specs=[pl.BlockSpec(memory_space=pltpu.SMEM),
                  pl.BlockSpec((WIN_QUERIES, WIN_Q), lambda b, n: (N_CTX // WIN_QUERIES + b * nb + n, 0)),
                  kv_spec, kv_spec, cache_spec, cache_spec, tab_all, tab_all, tab_blk, tab_blk],
        out_specs=pl.BlockSpec((WIN_QUERIES, WIN_Q), lambda b, n: (b * nb + n, 0)),
        out_shape=jax.ShapeDtypeStruct((N_LAT, WIN_Q), BF16),
        scratch_shapes=[pltpu.VMEM((2, DEC_SEQ + 2 * WINDOW, LANES), BF16),
                        pltpu.VMEM((2, DEC_SEQ + 2 * WINDOW, LANES), BF16),
                        pltpu.VMEM((4, PAST_LEN, LANES), BF16)],
        compiler_params=_params("arbitrary", "arbitrary"),
        name=f"lat_window{l}",
    )(win_sink, qw, kw, vw, cache_k, cache_v, cos_t, sin_t, cos_t, sin_t)


NA_G = 4
NA_Q = NA_G * GRID_W
NA_WIN_ROWS = NA_ROWS + NA_G
NA_BLOCKS = GRID_ROWS // NA_G
NA_VARIANTS = 3


def _na_block_start(g):
    return jnp.clip(NA_G * g - NA_ROWS // 2, 0, GRID_ROWS - NA_WIN_ROWS)


def _na_kernel(q_ref, k_ref, v_ref, ck_ref, cv_ref, bias_ref, o_ref):
    g = pl.program_id(1)
    start = pl.multiple_of(_na_block_start(g) * GRID_W, GRID_W)
    win = pl.ds(start, NA_WIN_ROWS * GRID_W)
    for jp in range(NA_HEADS // 2):
        sl = slice(LANES * jp, LANES * (jp + 1))
        q_stack = _stack_heads({jp: q_ref[:, sl] * ATTN_SCALE}, (2 * jp, 2 * jp + 1))
        bias = bias_ref[jp]
        o = _attend(q_stack, [(k_ref[win, sl].astype(BF16), v_ref[win, sl].astype(BF16), lambda s: s + bias),
                              (ck_ref[:, sl].astype(BF16), cv_ref[:, sl].astype(BF16), None)])
        o_ref[:, sl] = _merge_pair(o[:NA_Q], o[NA_Q:]).astype(BF16)


def _lat_neighbourhood(l, qn, kn, vn, cache_k, cache_v, bias_tab):
    kv_spec = pl.BlockSpec((DEC_SEQ, NA_W), lambda b, g: (b, 0))
    cache_spec = pl.BlockSpec((None, None, PAST_LEN, NA_W), lambda b, g: (b, l, 0, 0))
    variant = lambda g: jnp.where(g == 0, 0, jnp.where(g == NA_BLOCKS - 1, 2, 1))
    return pl.pallas_call(
        _na_kernel,
        grid=(DEC_BATCH, NA_BLOCKS),
        in_specs=[pl.BlockSpec((NA_Q, NA_W), lambda b, g: (N_CTX // NA_Q + b * NA_BLOCKS + g, 0)),
                  kv_spec, kv_spec, cache_spec, cache_spec,
                  pl.BlockSpec((None, None, NA_HEADS // 2, 2 * NA_Q, NA_WIN_ROWS * GRID_W),
                               lambda b, g: (l, variant(g), 0, 0, 0))],
        out_specs=pl.BlockSpec((NA_Q, NA_W), lambda b, g: (b * NA_BLOCKS + g, 0)),
        out_shape=jax.ShapeDtypeStruct((N_LAT, NA_W), BF16),
        compiler_params=_params("arbitrary", "arbitrary"),
        name=f"lat_neighbourhood{l}",
    )(qn, kn, vn, cache_k, cache_v, bias_tab)


OUTPROJ_ROWS = 256


def _outproj_kernel(xc_ref, xl_ref, mc_ref, mf_ref, mw_ref, mn_ref, mod_ref, g_ref, w_ref, r_hi, r_lo,
                    x_ref, h_ref, lg_ref, wb_ref):
    i = pl.program_id(0)

    @pl.when(i == 0)
    def _():
        wb_ref[...] = w_ref[...].astype(BF16)

    d = D_MODEL

    def block(x_in_ref, mixed_rows):
        for r0 in range(0, TM, OUTPROJ_ROWS):
            rows = slice(r0, r0 + OUTPROJ_ROWS)
            x = x_in_ref[rows, :] + mod_ref[:, 2 * d:3 * d] * _dot(mixed_rows(rows), wb_ref[...])
            x_ref[rows, :] = x
            h = _rms_mod(x, g_ref[...], mod_ref[:, 3 * d:4 * d], mod_ref[:, 4 * d:5 * d])
            h_ref[rows, :] = h.astype(BF16)
            h_hi, h_lo = _split(h)
            lg_ref[:, rows] = _dot3(h_hi, h_lo, r_hi[...], r_lo[...]).T[0:N_EXPERTS, :]

    @pl.when(i < N_CTX // TM)
    def _():
        block(xc_ref, lambda rows: mc_ref[rows, :])

    @pl.when(i >= N_CTX // TM)
    def _():
        block(xl_ref, lambda rows: jnp.concatenate([mf_ref[rows, :], mw_ref[rows, :], mn_ref[rows, :]], axis=1))


def _outproj(l, x_ctx, x_lat, mixed_ctx, mixed_f, mixed_w, mixed_n, mod4, g_ffn, w_out, r_hi, r_lo):
    n_ctx_blocks = N_CTX // TM
    xc_spec, xl_spec = _two_stream_specs(D_MODEL)
    lat = lambda w: pl.BlockSpec((TM, w), lambda i: (jnp.maximum(i - n_ctx_blocks, 0), 0))
    whole = lambda shape: pl.BlockSpec(shape, lambda i: (0,) * len(shape))
    row = lambda w: pl.BlockSpec((TM, w), lambda i: (i, 0))
    return pl.pallas_call(
        _outproj_kernel,
        grid=(N_TOK // TM,),
        in_specs=[xc_spec, xl_spec,
                  pl.BlockSpec((TM, D_MODEL), lambda i: (jnp.minimum(i, n_ctx_blocks - 1), 0)),
                  lat(F_WIDTH), lat(WIN_Q), lat(NA_W),
                  pl.BlockSpec((None, None, 1, 6 * D_MODEL), lambda i: (l, _cond_of_block(i), 0, 0)),
                  pl.BlockSpec((None, 1, D_MODEL), lambda i: (l, 0, 0)),
                  pl.BlockSpec((None, D_MODEL, D_MODEL), lambda i: (l, 0, 0)),
                  whole((D_MODEL, LANES)), whole((D_MODEL, LANES))],
        out_specs=[row(D_MODEL), row(D_MODEL), pl.BlockSpec((N_EXPERTS, TM), lambda i: (0, i))],
        out_shape=[jax.ShapeDtypeStruct((N_TOK, D_MODEL), F32),
                   jax.ShapeDtypeStruct((N_TOK, D_MODEL), BF16),
                   jax.ShapeDtypeStruct((N_EXPERTS, N_TOK), F32)],
        scratch_shapes=[pltpu.VMEM((D_MODEL, D_MODEL), BF16)],
        compiler_params=_params("arbitrary"),
        name=f"outproj{l}",
    )(x_ctx, x_lat, mixed_ctx, mixed_f, mixed_w, mixed_n, mod4, g_ffn.reshape(DEPTH, 1, D_MODEL), w_out,
      r_hi, r_lo)


PREFIX_CHUNK = 256
MANTISSA_STEPS = 44


def _prefix_exclusive(m):
    rows, n = m.shape
    t0 = lax.broadcasted_iota(jnp.int32, (PREFIX_CHUNK, PREFIX_CHUNK), 0)
    t1 = lax.broadcasted_iota(jnp.int32, (PREFIX_CHUNK, PREFIX_CHUNK), 1)
    upper = jnp.where(t0 < t1, 1.0, 0.0).astype(BF16)
    carry = jnp.zeros((rows, 1), F32)
    outs = []
    for c in range(n // PREFIX_CHUNK):
        blk = m[:, c * PREFIX_CHUNK:(c + 1) * PREFIX_CHUNK]
        outs.append(_dot(blk.astype(BF16), upper) + carry)
        carry = carry + blk.sum(axis=-1, keepdims=True)
    return outs[0] if len(outs) == 1 else jnp.concatenate(outs, axis=-1)


ROUTE_STREAMS = ((0, BATCH, CAP_CTX), (N_CTX, DEC_BATCH, CAP_LAT))


def _route_kernel(lg_ref, *out_refs):
    affs, caps = [], []
    for t0, groups, cap in ROUTE_STREAMS:
        n = (N_CTX if t0 == 0 else N_LAT) // groups
        x = jnp.concatenate([lg_ref[:, t0 + g * n:t0 + (g + 1) * n] for g in range(groups)], axis=0)
        x = x.reshape(groups, N_EXPERTS, n)
        e = jnp.exp(x - x.max(axis=1, keepdims=True))
        affs.append((e / e.sum(axis=1, keepdims=True)).reshape(groups * N_EXPERTS, n))
        caps.append(float(cap))

    def count_ge(aff, t):
        return jnp.where(aff >= t, 1.0, 0.0).sum(axis=-1, keepdims=True)

    def keep_if_enough(aff, cap, cand, otherwise):
        return jnp.where(count_ge(aff, cand) >= cap, cand, otherwise)

    above = [jnp.full((aff.shape[0], 1), 2.0, F32) for aff in affs]
    for s in (64, 32, 16, 8, 4, 2, 1):
        cands = [a * (2.0 ** -s) for a in above]
        above = [jnp.where(count_ge(aff, c) >= cap, a, c) for aff, cap, a, c in zip(affs, caps, above, cands)]
    bases = [keep_if_enough(aff, cap, a * 0.5, 0.0) for aff, cap, a in zip(affs, caps, above)]

    def refine(_, carry):
        thrs, incs = carry
        thrs = tuple(keep_if_enough(aff, cap, t + i, t) for aff, cap, t, i in zip(affs, caps, thrs, incs))
        return thrs, tuple(i * 0.5 for i in incs)

    thrs, _ = lax.fori_loop(0, MANTISSA_STEPS, refine, (tuple(bases), tuple(b * 0.5 for b in bases)))

    for k, ((_, groups, _), aff, cap, thr) in enumerate(zip(ROUTE_STREAMS, affs, caps, thrs)):
        slot_ref, aff_ref, slott_ref = out_refs[3 * k:3 * k + 3]
        n = aff.shape[1]
        gt = jnp.where(aff > thr, 1.0, 0.0)
        eq = jnp.where(aff == thr, 1.0, 0.0)
        need = cap - gt.sum(axis=-1, keepdims=True)
        sel = gt + eq * jnp.where(_prefix_exclusive(eq) < need, 1.0, 0.0)
        slot = jnp.where(sel > 0.0, _prefix_exclusive(sel), -1.0)
        slot_ref[...] = slot.astype(jnp.int32)
        aff_ref[...] = aff
        unused = jnp.full((LANES - N_EXPERTS, n), -1.0, F32)
        for g in range(groups):
            tile = jnp.concatenate([slot[g * N_EXPERTS:(g + 1) * N_EXPERTS], unused], axis=0)
            slott_ref[g * n:(g + 1) * n, :] = tile.T.astype(jnp.int32)


def _route(lg_t):
    shapes = []
    for t0, groups, _ in ROUTE_STREAMS:
        n_tok = N_CTX if t0 == 0 else N_LAT
        rows, n = groups * N_EXPERTS, n_tok // groups
        shapes += [((rows, n), jnp.int32), ((rows, n), F32), ((n_tok, LANES), jnp.int32)]
    return pl.pallas_call(
        _route_kernel,
        grid=(1,),
        in_specs=[pl.BlockSpec((N_EXPERTS, N_TOK), lambda i: (0, 0))],
        out_specs=[pl.BlockSpec(shape, lambda i: (0, 0)) for shape, _ in shapes],
        out_shape=[jax.ShapeDtypeStruct(shape, dt) for shape, dt in shapes],
        compiler_params=_params("arbitrary"),
        name="route",
    )(lg_t)


CTX_GROUP = 4


def _gather_ctx_kernel(h_ref, slot_ref, aff_ref, x_ref, g_ref, p_ref):
    s_iota = lax.broadcasted_iota(jnp.int32, (CAP_CTX, SEQ), 0)
    for bb in range(CTX_GROUP):
        slots = slice(bb * CAP_CTX, (bb + 1) * CAP_CTX)
        for e in range(N_EXPERTS):
            row = bb * N_EXPERTS + e
            hit = s_iota == slot_ref[row:row + 1, :]
            p_ref[bb, e * CAP_CTX:(e + 1) * CAP_CTX, :] = jnp.where(hit, 1.0, 0.0).astype(BF16)
            g_ref[e, slots] = jnp.where(hit, aff_ref[row:row + 1, :], 0.0).sum(axis=-1, keepdims=True)
        x = _dot(p_ref[bb], h_ref[bb * SEQ:(bb + 1) * SEQ, :]).astype(BF16)
        x_ref[:, slots, :] = x.reshape(N_EXPERTS, CAP_CTX, D_MODEL)


def _gather_ctx(h, slot, aff):
    return pl.pallas_call(
        _gather_ctx_kernel,
        grid=(BATCH // CTX_GROUP,),
        in_specs=[pl.BlockSpec((CTX_GROUP * SEQ, D_MODEL), lambda b: (b, 0)),
                  pl.BlockSpec((CTX_GROUP * N_EXPERTS, SEQ), lambda b: (b, 0)),
                  pl.BlockSpec((CTX_GROUP * N_EXPERTS, SEQ), lambda b: (b, 0))],
        out_specs=[pl.BlockSpec((N_EXPERTS, CTX_GROUP * CAP_CTX, D_MODEL), lambda b: (0, b, 0)),
                   pl.BlockSpec((N_EXPERTS, CTX_GROUP * CAP_CTX, 1), lambda b: (0, b, 0))],
        out_shape=[jax.ShapeDtypeStruct((N_EXPERTS, ROWS_CTX, D_MODEL), BF16),
                   jax.ShapeDtypeStruct((N_EXPERTS, ROWS_CTX, 1), F32)],
        scratch_shapes=[pltpu.VMEM((CTX_GROUP, N_EXPERTS * CAP_CTX, SEQ), BF16)],
        compiler_params=_params("arbitrary"),
        name="gather_ctx",
    )(h, slot, aff)


LAT_GATHER_EXPERTS = 4


def _gather_lat_kernel(h_ref, slot_ref, aff_ref, x_ref, g_ref):
    s_iota = lax.broadcasted_iota(jnp.int32, (CAP_LAT, DEC_SEQ), 0)
    for k in range(LAT_GATHER_EXPERTS):
        e = pl.program_id(1) * LAT_GATHER_EXPERTS + k
        hit = s_iota == slot_ref[pl.ds(e, 1), :]
        x_ref[k] = _dot(jnp.where(hit, 1.0, 0.0).astype(BF16), h_ref[...]).astype(BF16)
        g_ref[k] = jnp.where(hit, aff_ref[pl.ds(e, 1), :], 0.0).sum(axis=-1, keepdims=True)


def _gather_lat(h, slot, aff):
    lat_blk0 = N_CTX // DEC_SEQ
    return pl.pallas_call(
        _gather_lat_kernel,
        grid=(DEC_BATCH, N_EXPERTS // LAT_GATHER_EXPERTS),
        in_specs=[pl.BlockSpec((DEC_SEQ, D_MODEL), lambda b, e: (lat_blk0 + b, 0)),
                  pl.BlockSpec((N_EXPERTS, DEC_SEQ), lambda b, e: (b, 0)),
                  pl.BlockSpec((N_EXPERTS, DEC_SEQ), lambda b, e: (b, 0))],
        out_specs=[pl.BlockSpec((LAT_GATHER_EXPERTS, CAP_LAT, D_MODEL), lambda b, e: (e, b, 0)),
                   pl.BlockSpec((LAT_GATHER_EXPERTS, CAP_LAT, 1), lambda b, e: (e, b, 0))],
        out_shape=[jax.ShapeDtypeStruct((N_EXPERTS, ROWS_LAT, D_MODEL), BF16),
                   jax.ShapeDtypeStruct((N_EXPERTS, ROWS_LAT, 1), F32)],
        compiler_params=_params("arbitrary", "arbitrary"),
        name="gather_lat",
    )(h, slot, aff)


N_FF_CHUNKS = D_FF // TF
CHUNKS_PER_STEP = 4
FF_BLOCK = CHUNKS_PER_STEP * TF
N_FF_STEPS = -(-N_FF_CHUNKS // CHUNKS_PER_STEP)
LAST_STEP_CHUNKS = N_FF_CHUNKS - (N_FF_STEPS - 1) * CHUNKS_PER_STEP


def _silu_tanh(x):
    return x * (0.5 + 0.5 * jnp.tanh(0.5 * x))


def _ffn_kernel(xc_ref, xl_ref, gc_ref, gl_ref, wg_ref, wu_ref, wd_ref, y_ref, x_sc, h_sc, acc_sc):
    j = pl.program_id(1)

    @pl.when(j == 0)
    def _():
        x_sc[0:ROWS_CTX, :] = xc_ref[...]
        x_sc[ROWS_CTX:, :] = xl_ref[...]

    def step(n_chunks, first, last):
        x = x_sc[...]
        for sub in range(n_chunks):
            cols = slice(sub * TF, (sub + 1) * TF)
            a = _dot(x, wg_ref[:, cols].astype(BF16))
            u = _dot(x, wu_ref[:, cols].astype(BF16))
            h_sc[sub] = (_silu_tanh(a) * u).astype(BF16)
        for c0 in range(0, D_MODEL, TD):
            out_cols = slice(c0, c0 + TD)
            part = None
            for sub in range(n_chunks):
                t = _dot(h_sc[sub], wd_ref[sub * TF:(sub + 1) * TF, out_cols].astype(BF16))
                part = t if part is None else part + t
            if not first:
                part = acc_sc[:, out_cols] + part
            if last:
                y_ref[0:ROWS_CTX, out_cols] = (part[0:ROWS_CTX] * gc_ref[...]).astype(BF16)
                y_ref[ROWS_CTX:, out_cols] = (part[ROWS_CTX:] * gl_ref[...]).astype(BF16)
            else:
                acc_sc[:, out_cols] = part

    pl.when(j == 0)(functools.partial(step, CHUNKS_PER_STEP, True, False))
    pl.when((j > 0) & (j < N_FF_STEPS - 1))(functools.partial(step, CHUNKS_PER_STEP, False, False))
    pl.when(j == N_FF_STEPS - 1)(functools.partial(step, LAST_STEP_CHUNKS, False, True))


def _ffn(l, x_c, x_l, g_c, g_l, w_gate, w_up, w_down):
    rows = ROWS_CTX + ROWS_LAT
    xin = lambda r: pl.BlockSpec((None, r, D_MODEL), lambda e, j: (e, 0, 0))
    gin = lambda r: pl.BlockSpec((None, r, 1), lambda e, j: (e, 0, 0))
    return pl.pallas_call(
        _ffn_kernel,
        grid=(N_EXPERTS, N_FF_STEPS),
        in_specs=[xin(ROWS_CTX), xin(ROWS_LAT), gin(ROWS_CTX), gin(ROWS_LAT),
                  pl.BlockSpec((None, None, D_MODEL, FF_BLOCK), lambda e, j: (l, e, 0, j)),
                  pl.BlockSpec((None, None, D_MODEL, FF_BLOCK), lambda e, j: (l, e, 0, j)),
                  pl.BlockSpec((None, None, FF_BLOCK, D_MODEL), lambda e, j: (l, e, j, 0))],
        out_specs=pl.BlockSpec((None, rows, D_MODEL), lambda e, j: (e, 0, 0)),
        out_shape=jax.ShapeDtypeStruct((N_EXPERTS, rows, D_MODEL), BF16),
        scratch_shapes=[pltpu.VMEM((rows, D_MODEL), BF16), pltpu.VMEM((CHUNKS_PER_STEP, rows, TF), BF16),
                        pltpu.VMEM((rows, D_MODEL), F32)],
        compiler_params=_params("arbitrary", "arbitrary"),
        name=f"experts{l}",
    )(x_c, x_l, g_c, g_l, w_gate, w_up, w_down)


def _finish(x, res, mod_ref, gf_ref, final):
    y = x + mod_ref[:, 5 * D_MODEL:] * res
    if final:
        y = y * lax.rsqrt(jnp.mean(y * y, axis=-1, keepdims=True) + RMS_EPS) * gf_ref[...]
    return y


def _combine_ctx_kernel(final, x_ref, y_ref, slot_ref, rep_ref, mod_ref, gf_ref, o_ref):
    n_col = N_EXPERTS * CAP_CTX
    col = (lax.broadcasted_iota(jnp.int32, (SEQ, n_col), 1) % CAP_CTX).astype(F32)
    for bb in range(CTX_GROUP):
        rows = slice(bb * SEQ, (bb + 1) * SEQ)
        spread = _dot(slot_ref[rows, :].astype(F32).astype(BF16), rep_ref[...])
        p = jnp.where(spread == col, 1.0, 0.0).astype(BF16)
        y = y_ref[:, bb * CAP_CTX:(bb + 1) * CAP_CTX, :].reshape(n_col, D_MODEL)
        o_ref[rows, :] = _finish(x_ref[rows, :], _dot(p, y), mod_ref, gf_ref, final)


def _combine_ctx(l, final, x_new, y, slot_t, mod4, g_final):
    n_col = N_EXPERTS * CAP_CTX
    rep = (np.arange(n_col)[None, :] // CAP_CTX == np.arange(LANES)[:, None]).astype(np.float32)
    return pl.pallas_call(
        functools.partial(_combine_ctx_kernel, final),
        grid=(BATCH // CTX_GROUP,),
        in_specs=[pl.BlockSpec((CTX_GROUP * SEQ, D_MODEL), lambda b: (b, 0)),
                  pl.BlockSpec((N_EXPERTS, CTX_GROUP * CAP_CTX, D_MODEL), lambda b: (0, b, 0)),
                  pl.BlockSpec((CTX_GROUP * SEQ, LANES), lambda b: (b, 0)),
                  pl.BlockSpec((LANES, n_col), lambda b: (0, 0)),
                  pl.BlockSpec((None, None, 1, 6 * D_MODEL), lambda b: (l, 0, 0, 0)),
                  pl.BlockSpec((1, D_MODEL), lambda b: (0, 0))],
        out_specs=pl.BlockSpec((CTX_GROUP * SEQ, D_MODEL), lambda b: (b, 0)),
        out_shape=jax.ShapeDtypeStruct((N_CTX, D_MODEL), F32),
        compiler_params=_params("arbitrary"),
        name=f"combine_ctx{l}",
    )(x_new, y, slot_t, jnp.asarray(rep, BF16), mod4, g_final.reshape(1, D_MODEL))


TMC = 1024


def _combine_lat_kernel(final, x_ref, y_ref, slot_ref, mod_ref, gf_ref, o_ref):
    s_iota = lax.broadcasted_iota(jnp.int32, (TMC, CAP_LAT), 1)
    slot = slot_ref[...]
    res = None
    for e in range(N_EXPERTS):
        p = jnp.where(slot[:, e:e + 1] == s_iota, 1.0, 0.0).astype(BF16)
        t = _dot(p, y_ref[e])
        res = t if res is None else res + t
    o_ref[...] = _finish(x_ref[...], res, mod_ref, gf_ref, final)


def _combine_lat(l, final, x_new, y, slot_t, mod4, g_final):
    nt = DEC_SEQ // TMC
    return pl.pallas_call(
        functools.partial(_combine_lat_kernel, final),
        grid=(DEC_BATCH, nt),
        in_specs=[pl.BlockSpec((TMC, D_MODEL), lambda b, t: (N_CTX // TMC + b * nt + t, 0)),
                  pl.BlockSpec((N_EXPERTS, CAP_LAT, D_MODEL), lambda b, t: (0, ROWS_CTX // CAP_LAT + b, 0)),
                  pl.BlockSpec((TMC, LANES), lambda b, t: (b * nt + t, 0)),
                  pl.BlockSpec((None, None, 1, 6 * D_MODEL), lambda b, t: (l, 1 + b, 0, 0)),
                  pl.BlockSpec((1, D_MODEL), lambda b, t: (0, 0))],
        out_specs=pl.BlockSpec((TMC, D_MODEL), lambda b, t: (b * nt + t, 0)),
        out_shape=jax.ShapeDtypeStruct((N_LAT, D_MODEL), F32),
        compiler_params=_params("arbitrary", "arbitrary"),
        name=f"combine_lat{l}",
    )(x_new, y, slot_t, mod4, g_final.reshape(1, D_MODEL))


def _split_table(t):
    hi = t.astype(BF16)
    return hi, (t - hi.astype(F32)).astype(BF16)


def _dft_tables(n):
    p = np.arange(n, dtype=np.int64)
    ang = ((p[:, None] * p[None, :]) % n).astype(np.float64) * (2.0 * np.pi / n)
    return np.cos(ang).astype(np.float32), np.sin(ang).astype(np.float32)


def _channel_dft_tables():
    c = np.arange(F_WIDTH, dtype=np.int64)
    same = (c[:, None] // HEAD_DIM) == (c[None, :] // HEAD_DIM)
    ang = (((c[:, None] % HEAD_DIM) * (c[None, :] % HEAD_DIM)) % HEAD_DIM).astype(np.float64) * (2.0 * np.pi / HEAD_DIM)
    return (np.where(same, np.cos(ang), 0.0).astype(np.float32),
            np.where(same, np.sin(ang), 0.0).astype(np.float32))


def _rope_tables():
    half = HEAD_DIM // 2
    nf = half // 2
    pos = np.arange(DEC_SEQ)
    inv = 1.0 / (ROPE_BASE ** (np.arange(nf, dtype=np.float64) / nf))
    ang_r = (pos // GRID_W).astype(np.float64)[:, None] * inv
    ang_c = (pos % GRID_W).astype(np.float64)[:, None] * inv

    def head(fn, sign):
        return np.concatenate([sign * fn(ang_r), fn(ang_r), sign * fn(ang_c), fn(ang_c)], axis=-1)

    cos = head(np.cos, 1.0)
    sin = head(np.sin, -1.0)
    return (np.concatenate([cos, cos], axis=-1).astype(np.float32),
            np.concatenate([sin, sin], axis=-1).astype(np.float32))


def _na_bias_tables(rpb):
    cq = np.arange(GRID_W)
    rel_c = np.clip(cq[None, :] - cq[:, None] + NA_COLS - 1, 0, 2 * NA_COLS - 2)
    pick = (rel_c[:, :, None] == np.arange(2 * NA_COLS - 1)).astype(np.float32)
    cs = np.clip(cq - NA_COLS // 2, 0, GRID_W - NA_COLS)
    col_ok = (cq[None, :] >= cs[:, None]) & (cq[None, :] < cs[:, None] + NA_COLS)
    bc = jnp.einsum('lhrj,qkj->lhrqk', rpb, pick, precision=lax.Precision.HIGHEST)
    bc = jnp.where(col_ok[None, None, None], bc, NEG_INF)
    bc = jnp.pad(bc, ((0, 0), (0, 0), (1, 1), (0, 0), (0, 0)), constant_values=NEG_INF)
    bc = jnp.concatenate([bc[:, :, :-1], bc[:, :, 1:]], axis=-1)
    return pl.pallas_call(
        _na_bias_kernel,
        grid=(DEPTH, NA_VARIANTS, NA_HEADS // 2),
        in_specs=[pl.BlockSpec((None, 2, 2 * NA_ROWS, GRID_W, 2 * GRID_W), lambda l, v, p: (l, p, 0, 0, 0))],
        out_specs=pl.BlockSpec((None, None, None, 2 * NA_Q, NA_WIN_ROWS * GRID_W),
                               lambda l, v, p: (l, v, p, 0, 0)),
        out_shape=jax.ShapeDtypeStruct((DEPTH, NA_VARIANTS, NA_HEADS // 2, 2 * NA_Q, NA_WIN_ROWS * GRID_W), F32),
        compiler_params=_params("arbitrary", "arbitrary", "arbitrary"),
        name="na_bias",
    )(bc)


def _na_window_plan():
    plan = []
    for g in (0, 1, NA_BLOCKS - 1):
        start = int(np.clip(NA_G * g - NA_ROWS // 2, 0, GRID_ROWS - NA_WIN_ROWS))
        rows = []
        for a in range(NA_G):
            r = NA_G * g + a
            rs = int(np.clip(r - NA_ROWS // 2, 0, GRID_ROWS - NA_ROWS))
            rows.append([start + w - r + NA_ROWS - 1 if rs <= start + w < rs + NA_ROWS else None
                         for w in range(NA_WIN_ROWS)])
        plan.append(rows)
    return plan


def _na_bias_kernel(bc_ref, o_ref):
    outside = jnp.full((GRID_W, 2 * GRID_W), NEG_INF, F32)
    low = _lane_is_low(outside.shape)

    def pair_tile(half, ra, rb):
        if ra is None and rb is None:
            return outside
        if rb is None:
            return jnp.where(low, bc_ref[half, ra + 1], NEG_INF)
        if ra is None:
            return jnp.where(low, NEG_INF, bc_ref[half, rb])
        return bc_ref[half, ra + 1]

    for v, rows in enumerate(_na_window_plan()):
        @pl.when(pl.program_id(1) == v)
        def _():
            for half in range(2):
                for a, rel in enumerate(rows):
                    tiles = [pair_tile(half, rel[w], rel[w + 1]) for w in range(0, NA_WIN_ROWS, 2)]
                    r0 = half * NA_Q + a * GRID_W
                    o_ref[r0:r0 + GRID_W, :] = jnp.concatenate(tiles, axis=-1)


def kernel(x_prompt, x_sample, cache_win_k, cache_win_v, cache_nat_k, cache_nat_v, c, c_ctx, w_mod, b_mod, g_mix, g_ffn, w_in, w_out, win_sink, nat_rpb, w_router, w_gate, w_up, w_down, g_final):
    x_ctx = x_prompt.reshape(N_CTX, D_MODEL)
    x_lat = x_sample.reshape(N_LAT, D_MODEL)
    cond = jnp.concatenate([c_ctx[None, :], c, jnp.zeros((N_COND - 1 - DEC_BATCH, D_MODEL), F32)], axis=0)
    mod4 = _adaln(cond, w_mod, b_mod).reshape(DEPTH, N_COND, 1, 6 * D_MODEL)

    dft_ch = _channel_dft_tables()
    dft_ctx = _dft_tables(SEQ)
    dft_lat = _dft_tables(DEC_SEQ)
    cos_t, sin_t = _rope_tables()
    cwk = cache_win_k.reshape(DEC_BATCH, DEPTH, PAST_LEN, WIN_KV)
    cwv = cache_win_v.reshape(DEC_BATCH, DEPTH, PAST_LEN, WIN_KV)
    cnk = cache_nat_k.reshape(DEC_BATCH, DEPTH, PAST_LEN, NA_W)
    cnv = cache_nat_v.reshape(DEC_BATCH, DEPTH, PAST_LEN, NA_W)
    r_pad = jnp.pad(w_router, ((0, 0), (0, 0), (0, LANES - N_EXPERTS)))
    na_bias = _na_bias_tables(nat_rpb)

    kvt = ()
    for l in range(DEPTH):
        final = l == DEPTH - 1
        (f, qw, qn), (kw, vw, kn, vn), kvt = _project(l, x_ctx, x_lat, mod4, g_mix, w_in, kvt)

        mixed_ctx = _ctx_mixer(l, win_sink, f, qw, qn, kvt, dft_ch, dft_ctx)
        mixed_f = _lat_fourier(f, dft_ch, dft_lat)
        mixed_w = _lat_window(l, win_sink, qw, kw, vw, cwk, cwv, cos_t, sin_t)
        mixed_n = _lat_neighbourhood(l, qn, kn, vn, cnk, cnv, na_bias)

        r_hi, r_lo = _split_table(r_pad[l])
        x_new, h, lg_t = _outproj(l, x_ctx, x_lat, mixed_ctx, mixed_f, mixed_w, mixed_n, mod4, g_ffn,
                                  w_out, r_hi, r_lo)

        slot_c, aff_c, slot_ct, slot_l, aff_l, slot_lt = _route(lg_t)
        xg_c, gate_c = _gather_ctx(h, slot_c, aff_c)
        xg_l, gate_l = _gather_lat(h, slot_l, aff_l)
        y = _ffn(l, xg_c, xg_l, gate_c, gate_l, w_gate, w_up, w_down)
        x_ctx = _combine_ctx(l, final, x_new, y, slot_ct, mod4, g_final)
        x_lat = _combine_lat(l, final, x_new, y, slot_lt, mod4, g_final)

    y_prompt = x_ctx.reshape(BATCH, SEQ, D_MODEL)
    y_sample = x_lat.reshape(DEC_BATCH, DEC_SEQ, D_MODEL)
    new_kv = [t.reshape(BATCH, DEPTH, w // HEAD_DIM, HEAD_DIM, SEQ).transpose(0, 1, 4, 2, 3)
              for t, w in zip(kvt, KV_WIDTHS)]
    return (y_prompt, y_sample, *new_kv)
```

```python
import functools

import numpy as np
import jax
import jax.numpy as jnp
from jax import lax
from jax.experimental import pallas as pl
from jax.experimental.pallas import tpu as pltpu

D_MODEL = 1024
BATCH = 16
SEQ = 256
DEPTH = 2
DEC_BATCH = 2
DEC_SEQ = 2048
PAST_LEN = 256
GRID_W = 64
HEAD_DIM = 64
F_WIDTH = 256
WIN_HEADS = 6
WIN_KV_HEADS = 2
WINDOW = 128
NA_HEADS = 6
NA_ROWS = 8
NA_COLS = 16
N_EXPERTS = 16
EC_CAPACITY = 2
D_FF = 2816
ROPE_BASE = 10000.0
RMS_EPS = 1e-6
NEG_INF = -1e30
ATTN_SCALE = HEAD_DIM ** -0.5
WIN_Q = WIN_HEADS * HEAD_DIM
WIN_KV = WIN_KV_HEADS * HEAD_DIM
NA_W = NA_HEADS * HEAD_DIM
N_IN = F_WIDTH + WIN_Q + 2 * WIN_KV + 3 * NA_W
SPLITS = (0, F_WIDTH, F_WIDTH + WIN_Q, F_WIDTH + WIN_Q + WIN_KV, F_WIDTH + WIN_Q + 2 * WIN_KV,
          F_WIDTH + WIN_Q + 2 * WIN_KV + NA_W, F_WIDTH + WIN_Q + 2 * WIN_KV + 2 * NA_W, N_IN)

N_CTX = BATCH * SEQ
N_LAT = DEC_BATCH * DEC_SEQ
N_TOK = N_CTX + N_LAT
GRID_ROWS = DEC_SEQ // GRID_W
CAP_CTX = EC_CAPACITY * SEQ // N_EXPERTS
CAP_LAT = EC_CAPACITY * DEC_SEQ // N_EXPERTS
ROWS_CTX = BATCH * CAP_CTX
ROWS_LAT = DEC_BATCH * CAP_LAT
N_COND = 8

LANES = 128
MXU_COLS = 256
TM = 512
TN_MOD = 1536
TF = 256
TD = 256
VMEM_LIMIT = 56 * 1024 * 1024

F32 = jnp.float32
BF16 = jnp.bfloat16


def _params(*sem):
    return pltpu.CompilerParams(dimension_semantics=sem, vmem_limit_bytes=VMEM_LIMIT)


def _dot(a, b):
    return jnp.dot(a, b, preferred_element_type=F32)


def _dot_nt(a, b):
    return lax.dot_general(a, b, (((1,), (1,)), ((), ())), preferred_element_type=F32)


def _split(x):
    hi = x.astype(BF16)
    lo = (x - hi.astype(F32)).astype(BF16)
    return hi, lo


def _dot3(a_hi, a_lo, b_hi, b_lo):
    return _dot(a_hi, b_hi) + (_dot(a_lo, b_hi) + _dot(a_hi, b_lo))


def _silu(x):
    return x / (1.0 + jnp.exp(-x))


def _rms_mod(x, g, shift, scale):
    y = x * lax.rsqrt(jnp.mean(x * x, axis=-1, keepdims=True) + RMS_EPS)
    return (y * g) * (1.0 + scale) + shift


def _softmax_parts(parts, sink=None):
    m = parts[0].max(axis=-1, keepdims=True)
    for s in parts[1:]:
        m = jnp.maximum(m, s.max(axis=-1, keepdims=True))
    if sink is not None:
        m = jnp.maximum(m, sink)
    es = [jnp.exp(s - m) for s in parts]
    den = es[0].sum(axis=-1, keepdims=True)
    for e in es[1:]:
        den = den + e.sum(axis=-1, keepdims=True)
    if sink is not None:
        den = den + jnp.exp(sink - m)
    return es, 1.0 / den


def _cond_of_block(i):
    n_ctx_blocks = N_CTX // TM
    return jnp.where(i < n_ctx_blocks, 0, 1 + (i - n_ctx_blocks) // (DEC_SEQ // TM))


def _two_stream_specs(width):
    n_ctx_blocks = N_CTX // TM
    ctx = pl.BlockSpec((TM, width), lambda i: (jnp.minimum(i, n_ctx_blocks - 1), 0))
    lat = pl.BlockSpec((TM, width), lambda i: (jnp.maximum(i - n_ctx_blocks, 0), 0))
    return ctx, lat


def _adaln_kernel(c_ref, w_ref, b_ref, o_ref):
    s_hi, s_lo = _split(_silu(c_ref[...]))
    w_hi, w_lo = _split(w_ref[...])
    o_ref[...] = _dot3(s_hi, s_lo, w_hi, w_lo) + b_ref[...]


def _adaln(cond, w_mod, b_mod):
    return pl.pallas_call(
        _adaln_kernel,
        grid=(DEPTH, 6 * D_MODEL // TN_MOD),
        in_specs=[
            pl.BlockSpec((N_COND, D_MODEL), lambda l, j: (0, 0)),
            pl.BlockSpec((None, D_MODEL, TN_MOD), lambda l, j: (l, 0, j)),
            pl.BlockSpec((None, 1, TN_MOD), lambda l, j: (l, 0, j)),
        ],
        out_specs=pl.BlockSpec((None, N_COND, TN_MOD), lambda l, j: (l, 0, j)),
        out_shape=jax.ShapeDtypeStruct((DEPTH, N_COND, 6 * D_MODEL), F32),
        compiler_params=_params("arbitrary", "arbitrary"),
        name="adaln",
    )(cond, w_mod, b_mod.reshape(DEPTH, 1, 6 * D_MODEL))


KV_WIDTHS = (WIN_KV, WIN_KV, NA_W, NA_W)
KV_SPLITS = (SPLITS[2], SPLITS[3], SPLITS[5], SPLITS[6])
KV_TOTAL = sum(KV_WIDTHS)
Q_SPLITS = (SPLITS[0], SPLITS[1], SPLITS[4])


def _proj_kernel(n_prev, xc_ref, xl_ref, mod_ref, g_ref, w_ref, *rest):
    rest = rest[n_prev:]
    f_ref, qw_ref, qn_ref = rest[0:3]
    lat_kv = rest[3:7]
    ctx_kvt = rest[7:11]
    wb_ref, wt_ref = rest[11:13]
    i = pl.program_id(0)

    @pl.when(i == 0)
    def _():
        wb_ref[...] = w_ref[...].astype(BF16)
        r0 = 0
        for c0, width in zip(KV_SPLITS, KV_WIDTHS):
            wt_ref[r0:r0 + width, :] = w_ref[:, c0:c0 + width].T.astype(BF16)
            r0 += width

    natural = dict(zip(SPLITS[:-1], (f_ref, qw_ref) + tuple(lat_kv[0:2]) + (qn_ref,) + tuple(lat_kv[2:4])))

    def hidden(x_ref):
        return _rms_mod(x_ref[...], g_ref[...], mod_ref[:, 0:D_MODEL], mod_ref[:, D_MODEL:2 * D_MODEL]).astype(BF16)

    def project_tiles(h, wanted):
        for t0 in range(0, N_IN, MXU_COLS):
            hits = [(a, b) for a, b in zip(SPLITS[:-1], SPLITS[1:])
                    if a in wanted and max(a, t0) < min(b, t0 + MXU_COLS)]
            if not hits:
                continue
            acc = _dot(h, wb_ref[:, t0:t0 + MXU_COLS])
            for a, b in hits:
                lo, hi = max(a, t0), min(b, t0 + MXU_COLS)
                natural[a][:, lo - a:hi - a] = acc[:, lo - t0:hi - t0].astype(natural[a].dtype)

    @pl.when(i < N_CTX // TM)
    def _():
        h = hidden(xc_ref)
        project_tiles(h, Q_SPLITS)
        kvt = _dot_nt(wt_ref[...], h)
        r0 = 0
        for o_ref, width in zip(ctx_kvt, KV_WIDTHS):
            for bb in range(TM // SEQ):
                o_ref[bb] = kvt[r0:r0 + width, bb * SEQ:(bb + 1) * SEQ]
            r0 += width

    @pl.when(i >= N_CTX // TM)
    def _():
        project_tiles(hidden(xl_ref), Q_SPLITS + KV_SPLITS)


def _project(l, x_ctx, x_lat, mod4, g_mix, w_in, prev_kvt):
    n_ctx_blocks = N_CTX // TM
    xc_spec, xl_spec = _two_stream_specs(D_MODEL)
    both = lambda w: pl.BlockSpec((TM, w), lambda i: (i, 0))
    lat = lambda w: pl.BlockSpec((TM, w), lambda i: (jnp.maximum(i - n_ctx_blocks, 0), 0))
    ctx_t = lambda w: pl.BlockSpec((TM // SEQ, None, w, SEQ),
                                   lambda i: (jnp.minimum(i, n_ctx_blocks - 1), l, 0, 0))
    n_prev = len(prev_kvt)
    n_in = 5
    outs = pl.pallas_call(
        functools.partial(_proj_kernel, n_prev),
        grid=(N_TOK // TM,),
        in_specs=[
            xc_spec, xl_spec,
            pl.BlockSpec((None, None, 1, 6 * D_MODEL), lambda i: (l, _cond_of_block(i), 0, 0)),
            pl.BlockSpec((None, 1, D_MODEL), lambda i: (l, 0, 0)),
            pl.BlockSpec((None, D_MODEL, N_IN), lambda i: (l, 0, 0)),
        ] + [pl.BlockSpec(memory_space=pl.ANY)] * n_prev,
        out_specs=[both(F_WIDTH), both(WIN_Q), both(NA_W)] + [lat(w) for w in KV_WIDTHS]
                  + [ctx_t(w) for w in KV_WIDTHS],
        out_shape=[jax.ShapeDtypeStruct((N_TOK, w), dt) for w, dt in ((F_WIDTH, F32), (WIN_Q, F32), (NA_W, BF16))]
                  + [jax.ShapeDtypeStruct((N_LAT, w), dt)
                     for w, dt in zip(KV_WIDTHS, (F32, BF16, BF16, BF16))]
                  + [jax.ShapeDtypeStruct((BATCH, DEPTH, w, SEQ), F32) for w in KV_WIDTHS],
        input_output_aliases={n_in + k: 7 + k for k in range(n_prev)},
        scratch_shapes=[pltpu.VMEM((D_MODEL, N_IN), BF16), pltpu.VMEM((KV_TOTAL, D_MODEL), BF16)],
        compiler_params=_params("arbitrary"),
        name=f"project{l}",
    )(x_ctx, x_lat, mod4, g_mix.reshape(DEPTH, 1, D_MODEL), w_in, *prev_kvt)
    return outs[0:3], outs[3:7], outs[7:11]


def _lane_is_low(shape):
    return lax.broadcasted_iota(jnp.int32, shape, len(shape) - 1) < HEAD_DIM


def _swap_halves(x):
    return pltpu.roll(x, HEAD_DIM, axis=x.ndim - 1)


def _win_kv_copy(h):
    return 0 if (h // (WIN_HEADS // WIN_KV_HEADS)) == (h % 2) else 1


def _stack_heads(q_pairs, heads):
    low = _lane_is_low(q_pairs[heads[0] // 2].shape)
    rows = [jnp.where(low if h % 2 == 0 else jnp.logical_not(low), q_pairs[h // 2], 0.0).astype(BF16)
            for h in heads]
    return rows[0] if len(rows) == 1 else jnp.concatenate(rows, axis=0)


def _per_head_column(values, rows_per_head):
    blk = lax.broadcasted_iota(jnp.int32, (len(values) * rows_per_head, 1), 0) // rows_per_head
    col = jnp.full(blk.shape, values[0], F32)
    for i in range(1, len(values)):
        col = jnp.where(blk == i, values[i], col)
    return col


def _attend(q_stack, kv_list, extra_logit=None, transposed=False):
    scores = []
    for k, _, post in kv_list:
        s = _dot(q_stack, k) if transposed else _dot_nt(q_stack, k)
        scores.append(post(s) if post is not None else s)
    exps, inv = _softmax_parts(scores, extra_logit)
    o = None
    for e, (_, v, _) in zip(exps, kv_list):
        t = _dot_nt(e.astype(BF16), v) if transposed else _dot(e.astype(BF16), v)
        o = t if o is None else o + t
    return o * inv


def _merge_pair(o_even, o_odd):
    return jnp.where(_lane_is_low(o_even.shape), o_even, o_odd)


def _gqa_attention(q_pairs, rows, kv_for_copy, sinks, transposed=False):
    per_head = {}
    for copy in (0, 1):
        heads = [h for h in range(WIN_HEADS) if _win_kv_copy(h) == copy]
        o = _attend(_stack_heads(q_pairs, heads), kv_for_copy(copy),
                    _per_head_column([sinks[h] for h in heads], rows), transposed)
        for i, h in enumerate(heads):
            per_head[h] = o[i * rows:(i + 1) * rows]
    return [_merge_pair(per_head[2 * j], per_head[2 * j + 1]) for j in range(WIN_HEADS // 2)]


def _swap_row_halves(x):
    return jnp.concatenate([x[HEAD_DIM:], x[:HEAD_DIM]], axis=0)


MIX_GROUP = 2


def _ctx_mixer_kernel(l, sink_ref, f_ref, qw_ref, qn_ref, kw_ref, vw_ref, kn_ref, vn_ref,
                      bc_ref, bs_ref, cs_ref, ss_ref, o_ref):
    bc_hi, bc_lo = _split(bc_ref[...])
    bs_hi, bs_lo = _split(bs_ref[...])
    cs_hi, cs_lo = _split(cs_ref[...])
    ss_hi, ss_lo = _split(ss_ref[...])
    for bb in range(MIX_GROUP):
        rows = slice(bb * SEQ, (bb + 1) * SEQ)
        f_hi, f_lo = _split(f_ref[rows, :])
        fc_hi, fc_lo = _split(_dot3(f_hi, f_lo, bc_hi, bc_lo))
        fs_hi, fs_lo = _split(_dot3(f_hi, f_lo, bs_hi, bs_lo))
        z = _dot3(cs_hi, cs_lo, fc_hi, fc_lo) - _dot3(ss_hi, ss_lo, fs_hi, fs_lo)
        o_ref[rows, 0:F_WIDTH] = (z * (SEQ * HEAD_DIM) ** -0.5).astype(BF16)

        kv = [(kw_ref[bb].astype(BF16), vw_ref[bb].astype(BF16)),
              (_swap_row_halves(kw_ref[bb]).astype(BF16), _swap_row_halves(vw_ref[bb]).astype(BF16))]
        q_pairs = [qw_ref[rows, LANES * j:LANES * (j + 1)] * ATTN_SCALE for j in range(WIN_HEADS // 2)]
        outs = _gqa_attention(q_pairs, SEQ, lambda c: [(kv[c][0], kv[c][1], None)],
                              [sink_ref[l, h] for h in range(WIN_HEADS)], transposed=True)
        for j, o in enumerate(outs):
            o_ref[rows, F_WIDTH + LANES * j:F_WIDTH + LANES * (j + 1)] = o.astype(BF16)

        for j in range(NA_HEADS // 2):
            sl = slice(LANES * j, LANES * (j + 1))
            q_pairs = {j: qn_ref[rows, sl] * ATTN_SCALE}
            o = _attend(_stack_heads(q_pairs, (2 * j, 2 * j + 1)),
                        [(kn_ref[bb, sl, :].astype(BF16), vn_ref[bb, sl, :].astype(BF16), None)], transposed=True)
            base = F_WIDTH + WIN_Q + LANES * j
            o_ref[rows, base:base + LANES] = _merge_pair(o[:SEQ], o[SEQ:]).astype(BF16)


def _ctx_mixer(l, win_sink, f, qw, qn, kvt, dft_ch, dft_seq):
    row = lambda w: pl.BlockSpec((MIX_GROUP * SEQ, w), lambda b: (b, 0))
    col = lambda w: pl.BlockSpec((MIX_GROUP, None, w, SEQ), lambda b: (b, l, 0, 0))
    const = lambda n: pl.BlockSpec((n, n), lambda b: (0, 0))
    return pl.pallas_call(
        functools.partial(_ctx_mixer_kernel, l),
        grid=(BATCH // MIX_GROUP,),
        in_specs=[pl.BlockSpec(memory_space=pltpu.SMEM),
                  row(F_WIDTH), row(WIN_Q), row(NA_W)] + [col(w) for w in KV_WIDTHS]
                 + [const(F_WIDTH)] * 2 + [const(SEQ)] * 2,
        out_specs=pl.BlockSpec((MIX_GROUP * SEQ, D_MODEL), lambda b: (b, 0)),
        out_shape=jax.ShapeDtypeStruct((N_CTX, D_MODEL), BF16),
        compiler_params=_params("arbitrary"),
        name=f"ctx_mixer{l}",
    )(win_sink, f, qw, qn, *kvt, *dft_ch, *dft_seq)


FT_ROWS = 512


HALF_SEQ = DEC_SEQ // 2
SUBLANES = 8


def _lat_fourier_kernel(f_ref, mir_ref, bc_ref, bs_ref, cs_ref, ss_ref, o_ref, st_ref, mid_ref):
    r, b = pl.program_id(0), pl.program_id(1)

    @pl.when(r == 0)
    def _():
        f_hi, f_lo = _split(f_ref[pl.ds(pl.multiple_of(b * DEC_SEQ, DEC_SEQ), DEC_SEQ), :])
        fc = _dot3(f_hi, f_lo, *_split(bc_ref[...]))
        fs = _dot3(f_hi, f_lo, *_split(bs_ref[...]))

        def mirrored(t):
            hi, lo = _split(t[HALF_SEQ:])
            return _dot(mir_ref[...], hi) + _dot(mir_ref[...], lo)

        st_ref[b, 0], st_ref[b, 1] = _split(fc[:HALF_SEQ] + mirrored(fc))
        st_ref[b, 2], st_ref[b, 3] = _split(fs[:HALF_SEQ] - mirrored(fs))
        mid_ref[b] = fc[HALF_SEQ:HALF_SEQ + SUBLANES]

    z = (_dot3(*_split(cs_ref[...]), st_ref[b, 0], st_ref[b, 1])
         - _dot3(*_split(ss_ref[...]), st_ref[b, 2], st_ref[b, 3]))
    k = r * FT_ROWS + lax.broadcasted_iota(jnp.int32, (FT_ROWS, 1), 0)
    z = z + jnp.where(k % 2 == 0, 1.0, -1.0) * mid_ref[b, 0:1, :]
    o_ref[...] = (z * (DEC_SEQ * HEAD_DIM) ** -0.5).astype(BF16)


def _lat_fourier(f, dft_ch, dft_seq):
    nrb = DEC_SEQ // FT_ROWS
    p = np.arange(HALF_SEQ)
    mirror = ((p[:, None] + p[None, :] == HALF_SEQ) & (p[:, None] > 0)).astype(np.float32)
    const = pl.BlockSpec((F_WIDTH, F_WIDTH), lambda r, b: (0, 0))
    rows = pl.BlockSpec((FT_ROWS, HALF_SEQ), lambda r, b: (r, 0))
    return pl.pallas_call(
        _lat_fourier_kernel,
        grid=(nrb, DEC_BATCH),
        in_specs=[pl.BlockSpec((N_LAT, F_WIDTH), lambda r, b: (1, 0)),
                  pl.BlockSpec((HALF_SEQ, HALF_SEQ), lambda r, b: (0, 0))] + [const] * 2 + [rows] * 2,
        out_specs=pl.BlockSpec((FT_ROWS, F_WIDTH), lambda r, b: (b * nrb + r, 0)),
        out_shape=jax.ShapeDtypeStruct((N_LAT, F_WIDTH), BF16),
        scratch_shapes=[pltpu.VMEM((DEC_BATCH, 4, HALF_SEQ, F_WIDTH), BF16),
                        pltpu.VMEM((DEC_BATCH, SUBLANES, F_WIDTH), F32)],
        compiler_params=_params("arbitrary", "arbitrary"),
        name="lat_fourier",
    )(f, jnp.asarray(mirror, BF16), *dft_ch, *dft_seq)


def _rope(x, cos, sin_signed):
    n = x.shape[-1]
    lane = lax.broadcasted_iota(jnp.int32, x.shape, x.ndim - 1)
    first = (lane % 32) < 16
    partner = jnp.where(first, pltpu.roll(x, n - 16, axis=x.ndim - 1), pltpu.roll(x, 16, axis=x.ndim - 1))
    return x * cos + partner * sin_signed


WIN_QUERIES = 256
WIN_KEYS = WIN_QUERIES + 2 * WINDOW


def _win_kernel(l, sink_ref, q_ref, k_ref, v_ref, ck_ref, cv_ref, cos_ref, sin_ref, cosq_ref, sinq_ref,
                o_ref, kp_ref, vp_ref, cp_ref):
    n = pl.program_id(1)
    nb = DEC_SEQ // WIN_QUERIES
    pad = WINDOW

    @pl.when(n == 0)
    def _():
        zeros = jnp.zeros((pad, LANES), BF16)
        kr = _rope(k_ref[...], cos_ref[...], sin_ref[...])
        v = v_ref[...]
        for idx, (kk, vv) in enumerate(((kr, v), (_swap_halves(kr), _swap_halves(v)))):
            kp_ref[idx, 0:pad] = zeros
            kp_ref[idx, pad + DEC_SEQ:] = zeros
            kp_ref[idx, pad:pad + DEC_SEQ] = kk.astype(BF16)
            vp_ref[idx, 0:pad] = zeros
            vp_ref[idx, pad + DEC_SEQ:] = zeros
            vp_ref[idx, pad:pad + DEC_SEQ] = vv.astype(BF16)
        ck = ck_ref[...]
        cv = cv_ref[...]
        cp_ref[0] = ck.astype(BF16)
        cp_ref[1] = _swap_halves(ck).astype(BF16)
        cp_ref[2] = cv.astype(BF16)
        cp_ref[3] = _swap_halves(cv).astype(BF16)

    lo = jnp.where(n == 0, WINDOW, 0)
    hi = jnp.where(n == nb - 1, WIN_QUERIES + WINDOW, WIN_KEYS)

    i = lax.broadcasted_iota(jnp.int32, (WIN_QUERIES, WIN_KEYS), 0)
    j = lax.broadcasted_iota(jnp.int32, (WIN_QUERIES, WIN_KEYS), 1)
    mask = (j >= i) & (j <= i + 2 * WINDOW) & (j >= lo) & (j < hi)
    band_bias = jnp.where(mask, 0.0, NEG_INF)

    def band(s):
        heads = s.shape[0] // WIN_QUERIES
        return (s.reshape(heads, WIN_QUERIES, s.shape[1]) + band_bias[None]).reshape(s.shape)

    start = pl.multiple_of(n * WIN_QUERIES, WIN_QUERIES)
    win = pl.ds(start, WIN_KEYS)
    q_pairs = [_rope(q_ref[:, LANES * jp:LANES * (jp + 1)], cosq_ref[...], sinq_ref[...]) * ATTN_SCALE
               for jp in range(WIN_HEADS // 2)]
    outs = _gqa_attention(
        q_pairs, WIN_QUERIES,
        lambda c: [(kp_ref[c, win, :], vp_ref[c, win, :], band), (cp_ref[c], cp_ref[2 + c], None)],
        [sink_ref[l, h] for h in range(WIN_HEADS)])
    for jp, o in enumerate(outs):
        o_ref[:, LANES * jp:LANES * (jp + 1)] = o.astype(BF16)


def _lat_window(l, win_sink, qw, kw, vw, cache_k, cache_v, cos_t, sin_t):
    nb = DEC_SEQ // WIN_QUERIES
    kv_spec = pl.BlockSpec((DEC_SEQ, WIN_KV), lambda b, n: (b, 0))
    cache_spec = pl.BlockSpec((None, None, PAST_LEN, WIN_KV), lambda b, n: (b, l, 0, 0))
    tab_all = pl.BlockSpec((DEC_SEQ, LANES), lambda b, n: (0, 0))
    tab_blk = pl.BlockSpec((WIN_QUERIES, LANES), lambda b, n: (n, 0))
    return pl.pallas_call(
        functools.partial(_win_kernel, l),
        grid=(DEC_BATCH, nb),
        in_specs=[pl.BlockSpec(memory_space=pltpu.SMEM),
                  pl.BlockSpec((WIN_QUERIES, WIN_Q), lambda b, n: (N_CTX // WIN_QUERIES + b * nb + n, 0)),
                  kv_spec, kv_spec, cache_spec, cache_spec, tab_all, tab_all, tab_blk, tab_blk],
        out_specs=pl.BlockSpec((WIN_QUERIES, WIN_Q), lambda b, n: (b * nb + n, 0)),
        out_shape=jax.ShapeDtypeStruct((N_LAT, WIN_Q), BF16),
        scratch_shapes=[pltpu.VMEM((2, DEC_SEQ + 2 * WINDOW, LANES), BF16),
                        pltpu.VMEM((2, DEC_SEQ + 2 * WINDOW, LANES), BF16),
                        pltpu.VMEM((4, PAST_LEN, LANES), BF16)],
        compiler_params=_params("arbitrary", "arbitrary"),
        name=f"lat_window{l}",
    )(win_sink, qw, kw, vw, cache_k, cache_v, cos_t, sin_t, cos_t, sin_t)


NA_G = 4
NA_Q = NA_G * GRID_W
NA_WIN_ROWS = NA_ROWS + NA_G
NA_BLOCKS = GRID_ROWS // NA_G
NA_VARIANTS = 3


def _na_block_start(g):
    return jnp.clip(NA_G * g - NA_ROWS // 2, 0, GRID_ROWS - NA_WIN_ROWS)


def _na_kernel(q_ref, k_ref, v_ref, ck_ref, cv_ref, bias_ref, o_ref):
    g = pl.program_id(1)
    start = pl.multiple_of(_na_block_start(g) * GRID_W, GRID_W)
    win = pl.ds(start, NA_WIN_ROWS * GRID_W)
    for jp in range(NA_HEADS // 2):
        sl = slice(LANES * jp, LANES * (jp + 1))
        q_stack = _stack_heads({jp: q_ref[:, sl] * ATTN_SCALE}, (2 * jp, 2 * jp + 1))
        bias = bias_ref[jp]
        o = _attend(q_stack, [(k_ref[win, sl].astype(BF16), v_ref[win, sl].astype(BF16), lambda s: s + bias),
                              (ck_ref[:, sl].astype(BF16), cv_ref[:, sl].astype(BF16), None)])
        o_ref[:, sl] = _merge_pair(o[:NA_Q], o[NA_Q:]).astype(BF16)


def _lat_neighbourhood(l, qn, kn, vn, cache_k, cache_v, bias_tab):
    kv_spec = pl.BlockSpec((DEC_SEQ, NA_W), lambda b, g: (b, 0))
    cache_spec = pl.BlockSpec((None, None, PAST_LEN, NA_W), lambda b, g: (b, l, 0, 0))
    variant = lambda g: jnp.where(g == 0, 0, jnp.where(g == NA_BLOCKS - 1, 2, 1))
    return pl.pallas_call(
        _na_kernel,
        grid=(DEC_BATCH, NA_BLOCKS),
        in_specs=[pl.BlockSpec((NA_Q, NA_W), lambda b, g: (N_CTX // NA_Q + b * NA_BLOCKS + g, 0)),
                  kv_spec, kv_spec, cache_spec, cache_spec,
                  pl.BlockSpec((None, None, NA_HEADS // 2, 2 * NA_Q, NA_WIN_ROWS * GRID_W),
                               lambda b, g: (l, variant(g), 0, 0, 0))],
        out_specs=pl.BlockSpec((NA_Q, NA_W), lambda b, g: (b * NA_BLOCKS + g, 0)),
        out_shape=jax.ShapeDtypeStruct((N_LAT, NA_W), BF16),
        compiler_params=_params("arbitrary", "arbitrary"),
        name=f"lat_neighbourhood{l}",
    )(qn, kn, vn, cache_k, cache_v, bias_tab)


OUTPROJ_ROWS = 256


def _outproj_kernel(xc_ref, xl_ref, mc_ref, mf_ref, mw_ref, mn_ref, mod_ref, g_ref, w_ref, r_hi, r_lo,
                    x_ref, h_ref, lg_ref, wb_ref):
    i = pl.program_id(0)

    @pl.when(i == 0)
    def _():
        wb_ref[...] = w_ref[...].astype(BF16)

    d = D_MODEL

    def block(x_in_ref, mixed_rows):
        for r0 in range(0, TM, OUTPROJ_ROWS):
            rows = slice(r0, r0 + OUTPROJ_ROWS)
            x = x_in_ref[rows, :] + mod_ref[:, 2 * d:3 * d] * _dot(mixed_rows(rows), wb_ref[...])
            x_ref[rows, :] = x
            h = _rms_mod(x, g_ref[...], mod_ref[:, 3 * d:4 * d], mod_ref[:, 4 * d:5 * d])
            h_ref[rows, :] = h.astype(BF16)
            h_hi, h_lo = _split(h)
            lg_ref[:, rows] = _dot3(h_hi, h_lo, r_hi[...], r_lo[...]).T[0:N_EXPERTS, :]

    @pl.when(i < N_CTX // TM)
    def _():
        block(xc_ref, lambda rows: mc_ref[rows, :])

    @pl.when(i >= N_CTX // TM)
    def _():
        block(xl_ref, lambda rows: jnp.concatenate([mf_ref[rows, :], mw_ref[rows, :], mn_ref[rows, :]], axis=1))


def _outproj(l, x_ctx, x_lat, mixed_ctx, mixed_f, mixed_w, mixed_n, mod4, g_ffn, w_out, r_hi, r_lo):
    n_ctx_blocks = N_CTX // TM
    xc_spec, xl_spec = _two_stream_specs(D_MODEL)
    lat = lambda w: pl.BlockSpec((TM, w), lambda i: (jnp.maximum(i - n_ctx_blocks, 0), 0))
    whole = lambda shape: pl.BlockSpec(shape, lambda i: (0,) * len(shape))
    row = lambda w: pl.BlockSpec((TM, w), lambda i: (i, 0))
    return pl.pallas_call(
        _outproj_kernel,
        grid=(N_TOK // TM,),
        in_specs=[xc_spec, xl_spec,
                  pl.BlockSpec((TM, D_MODEL), lambda i: (jnp.minimum(i, n_ctx_blocks - 1), 0)),
                  lat(F_WIDTH), lat(WIN_Q), lat(NA_W),
                  pl.BlockSpec((None, None, 1, 6 * D_MODEL), lambda i: (l, _cond_of_block(i), 0, 0)),
                  pl.BlockSpec((None, 1, D_MODEL), lambda i: (l, 0, 0)),
                  pl.BlockSpec((None, D_MODEL, D_MODEL), lambda i: (l, 0, 0)),
                  whole((D_MODEL, LANES)), whole((D_MODEL, LANES))],
        out_specs=[row(D_MODEL), row(D_MODEL), pl.BlockSpec((N_EXPERTS, TM), lambda i: (0, i))],
        out_shape=[jax.ShapeDtypeStruct((N_TOK, D_MODEL), F32),
                   jax.ShapeDtypeStruct((N_TOK, D_MODEL), BF16),
                   jax.ShapeDtypeStruct((N_EXPERTS, N_TOK), F32)],
        scratch_shapes=[pltpu.VMEM((D_MODEL, D_MODEL), BF16)],
        compiler_params=_params("arbitrary"),
        name=f"outproj{l}",
    )(x_ctx, x_lat, mixed_ctx, mixed_f, mixed_w, mixed_n, mod4, g_ffn.reshape(DEPTH, 1, D_MODEL), w_out,
      r_hi, r_lo)


PREFIX_CHUNK = 256
MANTISSA_STEPS = 44


def _prefix_exclusive(m):
    rows, n = m.shape
    t0 = lax.broadcasted_iota(jnp.int32, (PREFIX_CHUNK, PREFIX_CHUNK), 0)
    t1 = lax.broadcasted_iota(jnp.int32, (PREFIX_CHUNK, PREFIX_CHUNK), 1)
    upper = jnp.where(t0 < t1, 1.0, 0.0).astype(BF16)
    carry = jnp.zeros((rows, 1), F32)
    outs = []
    for c in range(n // PREFIX_CHUNK):
        blk = m[:, c * PREFIX_CHUNK:(c + 1) * PREFIX_CHUNK]
        outs.append(_dot(blk.astype(BF16), upper) + carry)
        carry = carry + blk.sum(axis=-1, keepdims=True)
    return outs[0] if len(outs) == 1 else jnp.concatenate(outs, axis=-1)


ROUTE_STREAMS = ((0, BATCH, CAP_CTX), (N_CTX, DEC_BATCH, CAP_LAT))


def _route_kernel(lg_ref, *out_refs):
    affs, caps = [], []
    for t0, groups, cap in ROUTE_STREAMS:
        n = (N_CTX if t0 == 0 else N_LAT) // groups
        x = jnp.concatenate([lg_ref[:, t0 + g * n:t0 + (g + 1) * n] for g in range(groups)], axis=0)
        x = x.reshape(groups, N_EXPERTS, n)
        e = jnp.exp(x - x.max(axis=1, keepdims=True))
        affs.append((e / e.sum(axis=1, keepdims=True)).reshape(groups * N_EXPERTS, n))
        caps.append(float(cap))

    def count_ge(aff, t):
        return jnp.where(aff >= t, 1.0, 0.0).sum(axis=-1, keepdims=True)

    def keep_if_enough(aff, cap, cand, otherwise):
        return jnp.where(count_ge(aff, cand) >= cap, cand, otherwise)

    above = [jnp.full((aff.shape[0], 1), 2.0, F32) for aff in affs]
    for s in (64, 32, 16, 8, 4, 2, 1):
        cands = [a * (2.0 ** -s) for a in above]
        above = [jnp.where(count_ge(aff, c) >= cap, a, c) for aff, cap, a, c in zip(affs, caps, above, cands)]
    bases = [keep_if_enough(aff, cap, a * 0.5, 0.0) for aff, cap, a in zip(affs, caps, above)]

    def refine(_, carry):
        thrs, incs = carry
        thrs = tuple(keep_if_enough(aff, cap, t + i, t) for aff, cap, t, i in zip(affs, caps, thrs, incs))
        return thrs, tuple(i * 0.5 for i in incs)

    thrs, _ = lax.fori_loop(0, MANTISSA_STEPS, refine, (tuple(bases), tuple(b * 0.5 for b in bases)))

    for k, ((_, groups, _), aff, cap, thr) in enumerate(zip(ROUTE_STREAMS, affs, caps, thrs)):
        slot_ref, aff_ref, slott_ref = out_refs[3 * k:3 * k + 3]
        n = aff.shape[1]
        gt = jnp.where(aff > thr, 1.0, 0.0)
        eq = jnp.where(aff == thr, 1.0, 0.0)
        need = cap - gt.sum(axis=-1, keepdims=True)
        sel = gt + eq * jnp.where(_prefix_exclusive(eq) < need, 1.0, 0.0)
        slot = jnp.where(sel > 0.0, _prefix_exclusive(sel), -1.0)
        slot_ref[...] = slot.astype(jnp.int32)
        aff_ref[...] = aff
        unused = jnp.full((LANES - N_EXPERTS, n), -1.0, F32)
        for g in range(groups):
            tile = jnp.concatenate([slot[g * N_EXPERTS:(g + 1) * N_EXPERTS], unused], axis=0)
            slott_ref[g * n:(g + 1) * n, :] = tile.T.astype(jnp.int32)


def _route(lg_t):
    shapes = []
    for t0, groups, _ in ROUTE_STREAMS:
        n_tok = N_CTX if t0 == 0 else N_LAT
        rows, n = groups * N_EXPERTS, n_tok // groups
        shapes += [((rows, n), jnp.int32), ((rows, n), F32), ((n_tok, LANES), jnp.int32)]
    return pl.pallas_call(
        _route_kernel,
        grid=(1,),
        in_specs=[pl.BlockSpec((N_EXPERTS, N_TOK), lambda i: (0, 0))],
        out_specs=[pl.BlockSpec(shape, lambda i: (0, 0)) for shape, _ in shapes],
        out_shape=[jax.ShapeDtypeStruct(shape, dt) for shape, dt in shapes],
        compiler_params=_params("arbitrary"),
        name="route",
    )(lg_t)


CTX_GROUP = 4


def _gather_ctx_kernel(h_ref, slot_ref, aff_ref, x_ref, g_ref, p_ref):
    s_iota = lax.broadcasted_iota(jnp.int32, (CAP_CTX, SEQ), 0)
    for bb in range(CTX_GROUP):
        slots = slice(bb * CAP_CTX, (bb + 1) * CAP_CTX)
        for e in range(N_EXPERTS):
            row = bb * N_EXPERTS + e
            hit = s_iota == slot_ref[row:row + 1, :]
            p_ref[bb, e * CAP_CTX:(e + 1) * CAP_CTX, :] = jnp.where(hit, 1.0, 0.0).astype(BF16)
            g_ref[e, slots] = jnp.where(hit, aff_ref[row:row + 1, :], 0.0).sum(axis=-1, keepdims=True)
        x = _dot(p_ref[bb], h_ref[bb * SEQ:(bb + 1) * SEQ, :]).astype(BF16)
        x_ref[:, slots, :] = x.reshape(N_EXPERTS, CAP_CTX, D_MODEL)


def _gather_ctx(h, slot, aff):
    return pl.pallas_call(
        _gather_ctx_kernel,
        grid=(BATCH // CTX_GROUP,),
        in_specs=[pl.BlockSpec((CTX_GROUP * SEQ, D_MODEL), lambda b: (b, 0)),
                  pl.BlockSpec((CTX_GROUP * N_EXPERTS, SEQ), lambda b: (b, 0)),
                  pl.BlockSpec((CTX_GROUP * N_EXPERTS, SEQ), lambda b: (b, 0))],
        out_specs=[pl.BlockSpec((N_EXPERTS, CTX_GROUP * CAP_CTX, D_MODEL), lambda b: (0, b, 0)),
                   pl.BlockSpec((N_EXPERTS, CTX_GROUP * CAP_CTX, 1), lambda b: (0, b, 0))],
        out_shape=[jax.ShapeDtypeStruct((N_EXPERTS, ROWS_CTX, D_MODEL), BF16),
                   jax.ShapeDtypeStruct((N_EXPERTS, ROWS_CTX, 1), F32)],
        scratch_shapes=[pltpu.VMEM((CTX_GROUP, N_EXPERTS * CAP_CTX, SEQ), BF16)],
        compiler_params=_params("arbitrary"),
        name="gather_ctx",
    )(h, slot, aff)


LAT_GATHER_EXPERTS = 2


def _gather_lat_kernel(h_ref, slot_ref, aff_ref, x_ref, g_ref):
    s_iota = lax.broadcasted_iota(jnp.int32, (CAP_LAT, DEC_SEQ), 0)
    for k in range(LAT_GATHER_EXPERTS):
        e = pl.program_id(1) * LAT_GATHER_EXPERTS + k
        hit = s_iota == slot_ref[pl.ds(e, 1), :]
        x_ref[k] = _dot(jnp.where(hit, 1.0, 0.0).astype(BF16), h_ref[...]).astype(BF16)
        g_ref[k] = jnp.where(hit, aff_ref[pl.ds(e, 1), :], 0.0).sum(axis=-1, keepdims=True)


def _gather_lat(h, slot, aff):
    lat_blk0 = N_CTX // DEC_SEQ
    return pl.pallas_call(
        _gather_lat_kernel,
        grid=(DEC_BATCH, N_EXPERTS // LAT_GATHER_EXPERTS),
        in_specs=[pl.BlockSpec((DEC_SEQ, D_MODEL), lambda b, e: (lat_blk0 + b, 0)),
                  pl.BlockSpec((N_EXPERTS, DEC_SEQ), lambda b, e: (b, 0)),
                  pl.BlockSpec((N_EXPERTS, DEC_SEQ), lambda b, e: (b, 0))],
        out_specs=[pl.BlockSpec((LAT_GATHER_EXPERTS, CAP_LAT, D_MODEL), lambda b, e: (e, b, 0)),
                   pl.BlockSpec((LAT_GATHER_EXPERTS, CAP_LAT, 1), lambda b, e: (e, b, 0))],
        out_shape=[jax.ShapeDtypeStruct((N_EXPERTS, ROWS_LAT, D_MODEL), BF16),
                   jax.ShapeDtypeStruct((N_EXPERTS, ROWS_LAT, 1), F32)],
        compiler_params=_params("arbitrary", "arbitrary"),
        name="gather_lat",
    )(h, slot, aff)


N_FF_CHUNKS = D_FF // TF
CHUNKS_PER_STEP = 4
FF_BLOCK = CHUNKS_PER_STEP * TF
N_FF_STEPS = -(-N_FF_CHUNKS // CHUNKS_PER_STEP)
LAST_STEP_CHUNKS = N_FF_CHUNKS - (N_FF_STEPS - 1) * CHUNKS_PER_STEP


def _silu_tanh(x):
    return x * (0.5 + 0.5 * jnp.tanh(0.5 * x))


def _ffn_kernel(xc_ref, xl_ref, gc_ref, gl_ref, wg_ref, wu_ref, wd_ref, y_ref, x_sc, h_sc, acc_sc):
    j = pl.program_id(1)

    @pl.when(j == 0)
    def _():
        x_sc[0:ROWS_CTX, :] = xc_ref[...]
        x_sc[ROWS_CTX:, :] = xl_ref[...]

    def step(n_chunks, first, last):
        x = x_sc[...]
        for sub in range(n_chunks):
            cols = slice(sub * TF, (sub + 1) * TF)
            a = _dot(x, wg_ref[:, cols].astype(BF16))
            u = _dot(x, wu_ref[:, cols].astype(BF16))
            h_sc[sub] = (_silu_tanh(a) * u).astype(BF16)
        for c0 in range(0, D_MODEL, TD):
            out_cols = slice(c0, c0 + TD)
            part = None
            for sub in range(n_chunks):
                t = _dot(h_sc[sub], wd_ref[sub * TF:(sub + 1) * TF, out_cols].astype(BF16))
                part = t if part is None else part + t
            if not first:
                part = acc_sc[:, out_cols] + part
            if last:
                y_ref[0:ROWS_CTX, out_cols] = (part[0:ROWS_CTX] * gc_ref[...]).astype(BF16)
                y_ref[ROWS_CTX:, out_cols] = (part[ROWS_CTX:] * gl_ref[...]).astype(BF16)
            else:
                acc_sc[:, out_cols] = part

    pl.when(j == 0)(functools.partial(step, CHUNKS_PER_STEP, True, False))
    pl.when((j > 0) & (j < N_FF_STEPS - 1))(functools.partial(step, CHUNKS_PER_STEP, False, False))
    pl.when(j == N_FF_STEPS - 1)(functools.partial(step, LAST_STEP_CHUNKS, False, True))


def _ffn(l, x_c, x_l, g_c, g_l, w_gate, w_up, w_down):
    rows = ROWS_CTX + ROWS_LAT
    xin = lambda r: pl.BlockSpec((None, r, D_MODEL), lambda e, j: (e, 0, 0))
    gin = lambda r: pl.BlockSpec((None, r, 1), lambda e, j: (e, 0, 0))
    return pl.pallas_call(
        _ffn_kernel,
        grid=(N_EXPERTS, N_FF_STEPS),
        in_specs=[xin(ROWS_CTX), xin(ROWS_LAT), gin(ROWS_CTX), gin(ROWS_LAT),
                  pl.BlockSpec((None, None, D_MODEL, FF_BLOCK), lambda e, j: (l, e, 0, j)),
                  pl.BlockSpec((None, None, D_MODEL, FF_BLOCK), lambda e, j: (l, e, 0, j)),
                  pl.BlockSpec((None, None, FF_BLOCK, D_MODEL), lambda e, j: (l, e, j, 0))],
        out_specs=pl.BlockSpec((None, rows, D_MODEL), lambda e, j: (e, 0, 0)),
        out_shape=jax.ShapeDtypeStruct((N_EXPERTS, rows, D_MODEL), BF16),
        scratch_shapes=[pltpu.VMEM((rows, D_MODEL), BF16), pltpu.VMEM((CHUNKS_PER_STEP, rows, TF), BF16),
                        pltpu.VMEM((rows, D_MODEL), F32)],
        compiler_params=_params("arbitrary", "arbitrary"),
        name=f"experts{l}",
    )(x_c, x_l, g_c, g_l, w_gate, w_up, w_down)


def _finish(x, res, mod_ref, gf_ref, final):
    y = x + mod_ref[:, 5 * D_MODEL:] * res
    if final:
        y = y * lax.rsqrt(jnp.mean(y * y, axis=-1, keepdims=True) + RMS_EPS) * gf_ref[...]
    return y


def _combine_ctx_kernel(final, x_ref, y_ref, slot_ref, rep_ref, mod_ref, gf_ref, o_ref):
    n_col = N_EXPERTS * CAP_CTX
    col = (lax.broadcasted_iota(jnp.int32, (SEQ, n_col), 1) % CAP_CTX).astype(F32)
    for bb in range(CTX_GROUP):
        rows = slice(bb * SEQ, (bb + 1) * SEQ)
        spread = _dot(slot_ref[rows, :].astype(F32).astype(BF16), rep_ref[...])
        p = jnp.where(spread == col, 1.0, 0.0).astype(BF16)
        y = y_ref[:, bb * CAP_CTX:(bb + 1) * CAP_CTX, :].reshape(n_col, D_MODEL)
        o_ref[rows, :] = _finish(x_ref[rows, :], _dot(p, y), mod_ref, gf_ref, final)


def _combine_ctx(l, final, x_new, y, slot_t, mod4, g_final):
    n_col = N_EXPERTS * CAP_CTX
    rep = (np.arange(n_col)[None, :] // CAP_CTX == np.arange(LANES)[:, None]).astype(np.float32)
    return pl.pallas_call(
        functools.partial(_combine_ctx_kernel, final),
        grid=(BATCH // CTX_GROUP,),
        in_specs=[pl.BlockSpec((CTX_GROUP * SEQ, D_MODEL), lambda b: (b, 0)),
                  pl.BlockSpec((N_EXPERTS, CTX_GROUP * CAP_CTX, D_MODEL), lambda b: (0, b, 0)),
                  pl.BlockSpec((CTX_GROUP * SEQ, LANES), lambda b: (b, 0)),
                  pl.BlockSpec((LANES, n_col), lambda b: (0, 0)),
                  pl.BlockSpec((None, None, 1, 6 * D_MODEL), lambda b: (l, 0, 0, 0)),
                  pl.BlockSpec((1, D_MODEL), lambda b: (0, 0))],
        out_specs=pl.BlockSpec((CTX_GROUP * SEQ, D_MODEL), lambda b: (b, 0)),
        out_shape=jax.ShapeDtypeStruct((N_CTX, D_MODEL), F32),
        compiler_params=_params("arbitrary"),
        name=f"combine_ctx{l}",
    )(x_new, y, slot_t, jnp.asarray(rep, BF16), mod4, g_final.reshape(1, D_MODEL))


TMC = 512


def _combine_lat_kernel(final, x_ref, y_ref, slot_ref, mod_ref, gf_ref, o_ref):
    s_iota = lax.broadcasted_iota(jnp.int32, (TMC, CAP_LAT), 1)
    slot = slot_ref[...]
    res = None
    for e in range(N_EXPERTS):
        p = jnp.where(slot[:, e:e + 1] == s_iota, 1.0, 0.0).astype(BF16)
        t = _dot(p, y_ref[e])
        res = t if res is None else res + t
    o_ref[...] = _finish(x_ref[...], res, mod_ref, gf_ref, final)


def _combine_lat(l, final, x_new, y, slot_t, mod4, g_final):
    nt = DEC_SEQ // TMC
    return pl.pallas_call(
        functools.partial(_combine_lat_kernel, final),
        grid=(DEC_BATCH, nt),
        in_specs=[pl.BlockSpec((TMC, D_MODEL), lambda b, t: (N_CTX // TMC + b * nt + t, 0)),
                  pl.BlockSpec((N_EXPERTS, CAP_LAT, D_MODEL), lambda b, t: (0, ROWS_CTX // CAP_LAT + b, 0)),
                  pl.BlockSpec((TMC, LANES), lambda b, t: (b * nt + t, 0)),
                  pl.BlockSpec((None, None, 1, 6 * D_MODEL), lambda b, t: (l, 1 + b, 0, 0)),
                  pl.BlockSpec((1, D_MODEL), lambda b, t: (0, 0))],
        out_specs=pl.BlockSpec((TMC, D_MODEL), lambda b, t: (b * nt + t, 0)),
        out_shape=jax.ShapeDtypeStruct((N_LAT, D_MODEL), F32),
        compiler_params=_params("arbitrary", "arbitrary"),
        name=f"combine_lat{l}",
    )(x_new, y, slot_t, mod4, g_final.reshape(1, D_MODEL))


def _split_table(t):
    hi = t.astype(BF16)
    return hi, (t - hi.astype(F32)).astype(BF16)


def _dft_tables(n):
    p = np.arange(n, dtype=np.int64)
    ang = ((p[:, None] * p[None, :]) % n).astype(np.float64) * (2.0 * np.pi / n)
    return np.cos(ang).astype(np.float32), np.sin(ang).astype(np.float32)


def _channel_dft_tables():
    c = np.arange(F_WIDTH, dtype=np.int64)
    same = (c[:, None] // HEAD_DIM) == (c[None, :] // HEAD_DIM)
    ang = (((c[:, None] % HEAD_DIM) * (c[None, :] % HEAD_DIM)) % HEAD_DIM).astype(np.float64) * (2.0 * np.pi / HEAD_DIM)
    return (np.where(same, np.cos(ang), 0.0).astype(np.float32),
            np.where(same, np.sin(ang), 0.0).astype(np.float32))


def _rope_tables():
    half = HEAD_DIM // 2
    nf = half // 2
    pos = np.arange(DEC_SEQ)
    inv = 1.0 / (ROPE_BASE ** (np.arange(nf, dtype=np.float64) / nf))
    ang_r = (pos // GRID_W).astype(np.float64)[:, None] * inv
    ang_c = (pos % GRID_W).astype(np.float64)[:, None] * inv

    def head(fn, sign):
        return np.concatenate([sign * fn(ang_r), fn(ang_r), sign * fn(ang_c), fn(ang_c)], axis=-1)

    cos = head(np.cos, 1.0)
    sin = head(np.sin, -1.0)
    return (np.concatenate([cos, cos], axis=-1).astype(np.float32),
            np.concatenate([sin, sin], axis=-1).astype(np.float32))


def _na_bias_tables(rpb):
    cq = np.arange(GRID_W)
    rel_c = np.clip(cq[None, :] - cq[:, None] + NA_COLS - 1, 0, 2 * NA_COLS - 2)
    pick = (rel_c[:, :, None] == np.arange(2 * NA_COLS - 1)).astype(np.float32)
    cs = np.clip(cq - NA_COLS // 2, 0, GRID_W - NA_COLS)
    col_ok = (cq[None, :] >= cs[:, None]) & (cq[None, :] < cs[:, None] + NA_COLS)
    bc = jnp.einsum('lhrj,qkj->lhrqk', rpb, pick, precision=lax.Precision.HIGHEST)
    bc = jnp.where(col_ok[None, None, None], bc, NEG_INF)
    bc = jnp.pad(bc, ((0, 0), (0, 0), (1, 1), (0, 0), (0, 0)), constant_values=NEG_INF)
    bc = jnp.concatenate([bc[:, :, :-1], bc[:, :, 1:]], axis=-1)
    return pl.pallas_call(
        _na_bias_kernel,
        grid=(DEPTH, NA_VARIANTS, NA_HEADS // 2),
        in_specs=[pl.BlockSpec((None, 2, 2 * NA_ROWS, GRID_W, 2 * GRID_W), lambda l, v, p: (l, p, 0, 0, 0))],
        out_specs=pl.BlockSpec((None, None, None, 2 * NA_Q, NA_WIN_ROWS * GRID_W),
                               lambda l, v, p: (l, v, p, 0, 0)),
        out_shape=jax.ShapeDtypeStruct((DEPTH, NA_VARIANTS, NA_HEADS // 2, 2 * NA_Q, NA_WIN_ROWS * GRID_W), F32),
        compiler_params=_params("arbitrary", "arbitrary", "arbitrary"),
        name="na_bias",
    )(bc)


def _na_window_plan():
    plan = []
    for g in (0, 1, NA_BLOCKS - 1):
        start = int(np.clip(NA_G * g - NA_ROWS // 2, 0, GRID_ROWS - NA_WIN_ROWS))
        rows = []
        for a in range(NA_G):
            r = NA_G * g + a
            rs = int(np.clip(r - NA_ROWS // 2, 0, GRID_ROWS - NA_ROWS))
            rows.append([start + w - r + NA_ROWS - 1 if rs <= start + w < rs + NA_ROWS else None
                         for w in range(NA_WIN_ROWS)])
        plan.append(rows)
    return plan


def _na_bias_kernel(bc_ref, o_ref):
    outside = jnp.full((GRID_W, 2 * GRID_W), NEG_INF, F32)
    low = _lane_is_low(outside.shape)

    def pair_tile(half, ra, rb):
        if ra is None and rb is None:
            return outside
        if rb is None:
            return jnp.where(low, bc_ref[half, ra + 1], NEG_INF)
        if ra is None:
            return jnp.where(low, NEG_INF, bc_ref[half, rb])
        return bc_ref[half, ra + 1]

    for v, rows in enumerate(_na_window_plan()):
        @pl.when(pl.program_id(1) == v)
        def _():
            for half in range(2):
                for a, rel in enumerate(rows):
                    tiles = [pair_tile(half, rel[w], rel[w + 1]) for w in range(0, NA_WIN_ROWS, 2)]
                    r0 = half * NA_Q + a * GRID_W
                    o_ref[r0:r0 + GRID_W, :] = jnp.concatenate(tiles, axis=-1)


def kernel(x_prompt, x_sample, cache_win_k, cache_win_v, cache_nat_k, cache_nat_v, c, c_ctx, w_mod, b_mod, g_mix, g_ffn, w_in, w_out, win_sink, nat_rpb, w_router, w_gate, w_up, w_down, g_final):
    x_ctx = x_prompt.reshape(N_CTX, D_MODEL)
    x_lat = x_sample.reshape(N_LAT, D_MODEL)
    cond = jnp.concatenate([c_ctx[None, :], c, jnp.zeros((N_COND - 1 - DEC_BATCH, D_MODEL), F32)], axis=0)
    mod4 = _adaln(cond, w_mod, b_mod).reshape(DEPTH, N_COND, 1, 6 * D_MODEL)

    dft_ch = _channel_dft_tables()
    dft_ctx = _dft_tables(SEQ)
    dft_lat = _dft_tables(DEC_SEQ)
    cos_t, sin_t = _rope_tables()
    cwk = cache_win_k.reshape(DEC_BATCH, DEPTH, PAST_LEN, WIN_KV)
    cwv = cache_win_v.reshape(DEC_BATCH, DEPTH, PAST_LEN, WIN_KV)
    cnk = cache_nat_k.reshape(DEC_BATCH, DEPTH, PAST_LEN, NA_W)
    cnv = cache_nat_v.reshape(DEC_BATCH, DEPTH, PAST_LEN, NA_W)
    r_pad = jnp.pad(w_router, ((0, 0), (0, 0), (0, LANES - N_EXPERTS)))
    na_bias = _na_bias_tables(nat_rpb)

    kvt = ()
    for l in range(DEPTH):
        final = l == DEPTH - 1
        (f, qw, qn), (kw, vw, kn, vn), kvt = _project(l, x_ctx, x_lat, mod4, g_mix, w_in, kvt)

        mixed_ctx = _ctx_mixer(l, win_sink, f, qw, qn, kvt, dft_ch, dft_ctx)
        mixed_f = _lat_fourier(f, dft_ch, dft_lat)
        mixed_w = _lat_window(l, win_sink, qw, kw, vw, cwk, cwv, cos_t, sin_t)
        mixed_n = _lat_neighbourhood(l, qn, kn, vn, cnk, cnv, na_bias)

        r_hi, r_lo = _split_table(r_pad[l])
        x_new, h, lg_t = _outproj(l, x_ctx, x_lat, mixed_ctx, mixed_f, mixed_w, mixed_n, mod4, g_ffn,
                                  w_out, r_hi, r_lo)

        slot_c, aff_c, slot_ct, slot_l, aff_l, slot_lt = _route(lg_t)
        xg_c, gate_c = _gather_ctx(h, slot_c, aff_c)
        xg_l, gate_l = _gather_lat(h, slot_l, aff_l)
        y = _ffn(l, xg_c, xg_l, gate_c, gate_l, w_gate, w_up, w_down)
        x_ctx = _combine_ctx(l, final, x_new, y, slot_ct, mod4, g_final)
        x_lat = _combine_lat(l, final, x_new, y, slot_lt, mod4, g_final)

    y_prompt = x_ctx.reshape(BATCH, SEQ, D_MODEL)
    y_sample = x_lat.reshape(DEC_BATCH, DEC_SEQ, D_MODEL)
    new_kv = [t.reshape(BATCH, DEPTH, w // HEAD_DIM, HEAD_DIM, SEQ).transpose(0, 1, 4, 2, 3)
              for t, w in zip(kvt, KV_WIDTHS)]
    return (y_prompt, y_sample, *new_kv)
```
